```python
import math
import jax, jax.numpy as jnp
from jax import lax
import numpy as np

D_MODEL = 1024
BATCH = 8
SEQ = 4096
DEPTH = 4

MIX_WIDTH = D_MODEL
BRANCH = MIX_WIDTH // 4
CONF_KERNEL = 31
S5_GROUP = 16
S5_GROUPS = BRANCH // S5_GROUP
S5_STATE = 64
SC_KERNEL = 3
DN_HEADS = 4
DN_HEAD_DIM = BRANCH // DN_HEADS
DN_CONV = 4
DN_CHUNK = 64
NORM_EPS = 1e-6
IN_WIDTHS = (BRANCH, BRANCH, BRANCH,
             BRANCH, BRANCH,
             BRANCH, BRANCH, BRANCH, BRANCH,
             BRANCH, BRANCH, BRANCH, DN_HEADS, DN_HEADS, BRANCH)
IN_COLS = 13 * BRANCH + 2 * DN_HEADS

kernel_name = "hybrid_parallel_conv_s5_shortconv_deltanet"


def rms_norm(x, g):
    x32 = x.astype(jnp.float32)
    y = x32 * lax.rsqrt(jnp.mean(x32 * x32, axis=-1, keepdims=True) + NORM_EPS)
    return (y * g.astype(jnp.float32)).astype(x.dtype)


def layer_norm(x, g, b):
    x32 = x.astype(jnp.float32)
    mu = jnp.mean(x32, axis=-1, keepdims=True)
    xc = x32 - mu
    y = xc * lax.rsqrt(jnp.mean(xc * xc, axis=-1, keepdims=True) + NORM_EPS)
    return (y * g.astype(jnp.float32) + b.astype(jnp.float32)).astype(x.dtype)


def l2_normalize(x):
    return x * lax.rsqrt(jnp.sum(x * x, axis=-1, keepdims=True) + NORM_EPS)


def causal_depthwise_conv(x, w):
    K, C = w.shape
    return lax.conv_general_dilated(
        x, w[:, None, :], window_strides=(1,), padding=[(K - 1, 0)],
        dimension_numbers=("NWC", "WIO", "NWC"), feature_group_count=C)


def split_columns(p):
    outs, start = [], 0
    for w in IN_WIDTHS:
        outs.append(p[..., start:start + w])
        start += w
    return outs


def conformer_conv_branch(val, gate, conv_w, conv_b, ln_g, ln_b, pw_w, pw_b):
    a = val * jax.nn.sigmoid(gate)
    a = causal_depthwise_conv(a, conv_w) + conv_b
    a = layer_norm(a, ln_g, ln_b)
    a = jax.nn.silu(a)
    return a @ pw_w + pw_b


def s5_branch(u, lam_re, lam_im, b_re, b_im, c_re, c_im, d_skip, log_dt, glu_w, glu_b):
    bsz, L, _ = u.shape
    f32 = jnp.float32
    u32 = u.astype(f32)
    lam = lax.complex(jnp.minimum(lam_re.astype(f32), -1e-4), lam_im.astype(f32))
    dt = jnp.exp(log_dt.astype(f32))[:, None]
    lam_bar = jnp.exp(lam * dt)
    b = lax.complex(b_re.astype(f32), b_im.astype(f32))
    b_bar = ((lam_bar - 1.0) / lam)[..., None] * b
    ug = u32.reshape(bsz, L, S5_GROUPS, S5_GROUP).astype(jnp.complex64)
    bu = jnp.einsum("blgh,gph->blgp", ug, b_bar)
    a = jnp.broadcast_to(lam_bar, bu.shape)

    def combine(e1, e2):
        a1, s1 = e1
        a2, s2 = e2
        return a1 * a2, a2 * s1 + s2

    _, states = lax.associative_scan(combine, (a, bu), axis=1)
    c = lax.complex(c_re.astype(f32), c_im.astype(f32))
    y = jnp.real(jnp.einsum("blgp,ghp->blgh", states, c)).reshape(bsz, L, BRANCH)
    y = y + d_skip.astype(f32) * u32
    y = jax.nn.gelu(y).astype(u.dtype)
    return y * jax.nn.sigmoid(y @ glu_w + glu_b)


def short_conv_branch(bg, cg, xc, conv_w):
    return bg * causal_depthwise_conv(cg * xc, conv_w)


def gated_delta_rule_chunked(q, k, v, beta, g):
    bsz, L, H, dk = q.shape
    dv = v.shape[-1]
    C = DN_CHUNK
    N = L // C

    def chunk(t):
        t = t.reshape(bsz, N, C, H, t.shape[-1])
        return jnp.transpose(t, (1, 0, 3, 2, 4))

    q, k, v = chunk(q), chunk(k), chunk(v)
    beta = chunk(beta[..., None])[..., 0]
    g = chunk(g[..., None])[..., 0]
    gc = jnp.cumsum(g, axis=-1)
    idx = jnp.arange(C)
    causal = idx[:, None] >= idx[None, :]
    strict = idx[:, None] > idx[None, :]
    decay = jnp.exp(jnp.where(causal, gc[..., :, None] - gc[..., None, :], -jnp.inf))
    k_beta = k * beta[..., None]
    lmat = jnp.where(strict, jnp.einsum("nbhid,nbhjd->nbhij", k_beta, k) * decay, 0.0)
    rhs = jnp.concatenate([v * beta[..., None], k_beta * jnp.exp(gc)[..., None]], axis=-1)
    sol = lax.linalg.triangular_solve(lmat + jnp.eye(C, dtype=lmat.dtype), rhs,
                                      left_side=True, lower=True)
    u, w = sol[..., :dv], sol[..., dv:]
    attn = jnp.einsum("nbhid,nbhjd->nbhij", q, k) * decay
    q_dec = q * jnp.exp(gc)[..., None]
    g_last = gc[..., -1]
    k_dec = k * jnp.exp(g_last[..., None] - gc)[..., None]

    def step(S, inp):
        attn_c, u_c, w_c, qd, kd, gl = inp
        v_new = u_c - jnp.einsum("bhik,bhkv->bhiv", w_c, S)
        o = jnp.einsum("bhik,bhkv->bhiv", qd, S) + jnp.einsum("bhij,bhjv->bhiv", attn_c, v_new)
        S = S * jnp.exp(gl)[..., None, None] + jnp.einsum("bhik,bhiv->bhkv", kd, v_new)
        return S, o

    S0 = jnp.zeros((bsz, H, dk, dv), jnp.float32)
    _, o = lax.scan(step, S0, (attn, u, w, q_dec, k_dec, g_last))
    return jnp.transpose(o, (1, 0, 3, 2, 4)).reshape(bsz, L, H, dv)


def deltanet_branch(q, k, v, alpha, beta_logit, conv_w, a_log, dt_bias, norm_g):
    bsz, L, _ = q.shape
    f32 = jnp.float32
    qkv = jax.nn.silu(causal_depthwise_conv(jnp.concatenate([q, k, v], axis=-1), conv_w))
    q, k, v = qkv[..., :BRANCH], qkv[..., BRANCH:2 * BRANCH], qkv[..., 2 * BRANCH:]
    q = q.astype(f32).reshape(bsz, L, DN_HEADS, DN_HEAD_DIM)
    k = k.astype(f32).reshape(bsz, L, DN_HEADS, DN_HEAD_DIM)
    v = v.astype(f32).reshape(bsz, L, DN_HEADS, DN_HEAD_DIM)
    q = l2_normalize(q) * (DN_HEAD_DIM ** -0.5)
    k = l2_normalize(k)
    beta = jax.nn.sigmoid(beta_logit.astype(f32))
    g = -jnp.exp(a_log.astype(f32)) * jax.nn.softplus(alpha.astype(f32) + dt_bias.astype(f32))
    o = gated_delta_rule_chunked(q, k, v, beta, g)
    o = rms_norm(o, norm_g)
    return o.reshape(bsz, L, BRANCH)


def _fwd_setup_inputs(seed: int = 0) -> dict:
    key = jax.random.key(seed)
    ks = jax.random.split(key, 32)
    f32 = jnp.float32

    def nrm(k, shape, s):
        return jax.random.normal(k, shape, f32) * s

    G, P, H = S5_GROUPS, S5_STATE, S5_GROUP
    x = nrm(ks[0], (BATCH, SEQ, D_MODEL), 1.0)
    norm_g = 1.0 + nrm(ks[1], (DEPTH, D_MODEL), 0.02)
    w_in = nrm(ks[2], (DEPTH, D_MODEL, IN_COLS), D_MODEL ** -0.5)
    a_conv_w = nrm(ks[3], (DEPTH, CONF_KERNEL, BRANCH), CONF_KERNEL ** -0.5)
    a_conv_b = nrm(ks[4], (DEPTH, BRANCH), 0.02)
    a_ln_g = 1.0 + nrm(ks[5], (DEPTH, BRANCH), 0.02)
    a_ln_b = nrm(ks[6], (DEPTH, BRANCH), 0.02)
    a_pw_w = nrm(ks[7], (DEPTH, BRANCH, BRANCH), BRANCH ** -0.5)
    a_pw_b = nrm(ks[8], (DEPTH, BRANCH), 0.02)
    n_idx = jnp.arange(P, dtype=f32)
    s5_lambda_re = -0.5 + nrm(ks[9], (DEPTH, G, P), 0.01)
    s5_lambda_im = math.pi * n_idx + nrm(ks[10], (DEPTH, G, P), 0.01)
    s5_b_re = nrm(ks[11], (DEPTH, G, P, H), (2.0 * H) ** -0.5)
    s5_b_im = nrm(ks[12], (DEPTH, G, P, H), (2.0 * H) ** -0.5)
    s5_c_re = nrm(ks[13], (DEPTH, G, H, P), (2.0 * P) ** -0.5)
    s5_c_im = nrm(ks[14], (DEPTH, G, H, P), (2.0 * P) ** -0.5)
    s5_d = nrm(ks[15], (DEPTH, BRANCH), 0.5)
    s5_log_dt = jax.random.uniform(ks[16], (DEPTH, G), f32, math.log(1e-3), math.log(1e-1))
    s5_glu_w = nrm(ks[17], (DEPTH, BRANCH, BRANCH), BRANCH ** -0.5)
    s5_glu_b = nrm(ks[18], (DEPTH, BRANCH), 0.02)
    c_conv_w = nrm(ks[19], (DEPTH, SC_KERNEL, BRANCH), SC_KERNEL ** -0.5)
    d_conv_w = nrm(ks[20], (DEPTH, DN_CONV, 3 * BRANCH), DN_CONV ** -0.5)
    d_a_log = jnp.log(jax.random.uniform(ks[21], (DEPTH, DN_HEADS), f32, 1.0, 16.0))
    dt0 = jnp.exp(jax.random.uniform(ks[22], (DEPTH, DN_HEADS), f32, math.log(1e-3), math.log(1e-1)))
    d_dt_bias = dt0 + jnp.log(-jnp.expm1(-dt0))
    d_norm_g = 1.0 + nrm(ks[23], (DEPTH, DN_HEAD_DIM), 0.02)
    w_out = nrm(ks[24], (DEPTH, MIX_WIDTH, D_MODEL), MIX_WIDTH ** -0.5)
    final_g = 1.0 + nrm(ks[25], (D_MODEL,), 0.02)
    return {"x": x, "norm_g": norm_g, "w_in": w_in,
            "a_conv_w": a_conv_w, "a_conv_b": a_conv_b, "a_ln_g": a_ln_g, "a_ln_b": a_ln_b,
            "a_pw_w": a_pw_w, "a_pw_b": a_pw_b,
            "s5_lambda_re": s5_lambda_re, "s5_lambda_im": s5_lambda_im,
            "s5_b_re": s5_b_re, "s5_b_im": s5_b_im, "s5_c_re": s5_c_re, "s5_c_im": s5_c_im,
            "s5_d": s5_d, "s5_log_dt": s5_log_dt, "s5_glu_w": s5_glu_w, "s5_glu_b": s5_glu_b,
            "c_conv_w": c_conv_w,
            "d_conv_w": d_conv_w, "d_a_log": d_a_log, "d_dt_bias": d_dt_bias, "d_norm_g": d_norm_g,
            "w_out": w_out, "final_g": final_g}


def _fwd_reference(x, norm_g, w_in, a_conv_w, a_conv_b, a_ln_g, a_ln_b, a_pw_w, a_pw_b,
              s5_lambda_re, s5_lambda_im, s5_b_re, s5_b_im, s5_c_re, s5_c_im,
              s5_d, s5_log_dt, s5_glu_w, s5_glu_b, c_conv_w,
              d_conv_w, d_a_log, d_dt_bias, d_norm_g, w_out, final_g):
    for l in range(DEPTH):
        h = rms_norm(x, norm_g[l])
        proj = h @ w_in[l]
        (a_val, a_gate, a_z, b_u, b_z, c_b, c_c, c_x, c_z,
         d_q, d_k, d_v, d_alpha, d_beta, d_z) = split_columns(proj)
        ya = conformer_conv_branch(a_val, a_gate, a_conv_w[l], a_conv_b[l], a_ln_g[l],
                                   a_ln_b[l], a_pw_w[l], a_pw_b[l]) * jax.nn.silu(a_z)
        yb = s5_branch(b_u, s5_lambda_re[l], s5_lambda_im[l], s5_b_re[l], s5_b_im[l],
                       s5_c_re[l], s5_c_im[l], s5_d[l], s5_log_dt[l],
                       s5_glu_w[l], s5_glu_b[l]) * jax.nn.silu(b_z)
        yc = short_conv_branch(c_b, c_c, c_x, c_conv_w[l]) * jax.nn.silu(c_z)
        yd = deltanet_branch(d_q, d_k, d_v, d_alpha, d_beta, d_conv_w[l], d_a_log[l],
                             d_dt_bias[l], d_norm_g[l]).astype(x.dtype) * jax.nn.silu(d_z)
        mixed = jnp.concatenate([ya, yb.astype(x.dtype), yc, yd], axis=-1)
        x = x + mixed @ w_out[l]
    return rms_norm(x, final_g)


import jax as _jax
import jax.numpy as _jnp

TWIN_FORMAT = 'train_step'
FWD_PARAMS = ['x', 'norm_g', 'w_in', 'a_conv_w', 'a_conv_b', 'a_ln_g', 'a_ln_b', 'a_pw_w', 'a_pw_b', 's5_lambda_re', 's5_lambda_im', 's5_b_re', 's5_b_im', 's5_c_re', 's5_c_im', 's5_d', 's5_log_dt', 's5_glu_w', 's5_glu_b', 'c_conv_w', 'd_conv_w', 'd_a_log', 'd_dt_bias', 'd_norm_g', 'w_out', 'final_g']
TWIN_WEIGHTS = ['norm_g', 'w_in', 'a_conv_w', 'a_conv_b', 'a_ln_g', 'a_ln_b', 'a_pw_w', 'a_pw_b', 's5_lambda_re', 's5_lambda_im', 's5_b_re', 's5_b_im', 's5_c_re', 's5_c_im', 's5_d', 's5_log_dt', 's5_glu_w', 's5_glu_b', 'c_conv_w', 'd_conv_w', 'd_a_log', 'd_dt_bias', 'd_norm_g', 'w_out', 'final_g']
TWIN_DIFF_INPUT = 'x'
TWIN_INPUTS = ['x', 'norm_g', 'w_in', 'a_conv_w', 'a_conv_b', 'a_ln_g', 'a_ln_b', 'a_pw_w', 'a_pw_b', 's5_lambda_re', 's5_lambda_im', 's5_b_re', 's5_b_im', 's5_c_re', 's5_c_im', 's5_d', 's5_log_dt', 's5_glu_w', 's5_glu_b', 'c_conv_w', 'd_conv_w', 'd_a_log', 'd_dt_bias', 'd_norm_g', 'w_out', 'final_g', 'loss_target', 'm_norm_g', 'm_w_in', 'm_a_conv_w', 'm_a_conv_b', 'm_a_ln_g', 'm_a_ln_b', 'm_a_pw_w', 'm_a_pw_b', 'm_s5_lambda_re', 'm_s5_lambda_im', 'm_s5_b_re', 'm_s5_b_im', 'm_s5_c_re', 'm_s5_c_im', 'm_s5_d', 'm_s5_log_dt', 'm_s5_glu_w', 'm_s5_glu_b', 'm_c_conv_w', 'm_d_conv_w', 'm_d_a_log', 'm_d_dt_bias', 'm_d_norm_g', 'm_w_out', 'm_final_g', 'v_norm_g', 'v_w_in', 'v_a_conv_w', 'v_a_conv_b', 'v_a_ln_g', 'v_a_ln_b', 'v_a_pw_w', 'v_a_pw_b', 'v_s5_lambda_re', 'v_s5_lambda_im', 'v_s5_b_re', 'v_s5_b_im', 'v_s5_c_re', 'v_s5_c_im', 'v_s5_d', 'v_s5_log_dt', 'v_s5_glu_w', 'v_s5_glu_b', 'v_c_conv_w', 'v_d_conv_w', 'v_d_a_log', 'v_d_dt_bias', 'v_d_norm_g', 'v_w_out', 'v_final_g']
TWIN_OUTPUTS = ['loss', 'grad_x', 'grad_norm_g', 'grad_w_in', 'grad_a_conv_w', 'grad_a_conv_b', 'grad_a_ln_g', 'grad_a_ln_b', 'grad_a_pw_w', 'grad_a_pw_b', 'grad_s5_lambda_re', 'grad_s5_lambda_im', 'grad_s5_b_re', 'grad_s5_b_im', 'grad_s5_c_re', 'grad_s5_c_im', 'grad_s5_d', 'grad_s5_log_dt', 'grad_s5_glu_w', 'grad_s5_glu_b', 'grad_c_conv_w', 'grad_d_conv_w', 'grad_d_a_log', 'grad_d_dt_bias', 'grad_d_norm_g', 'grad_w_out', 'grad_final_g', 'delta_norm_g', 'delta_w_in', 'delta_a_conv_w', 'delta_a_conv_b', 'delta_a_ln_g', 'delta_a_ln_b', 'delta_a_pw_w', 'delta_a_pw_b', 'delta_s5_lambda_re', 'delta_s5_lambda_im', 'delta_s5_b_re', 'delta_s5_b_im', 'delta_s5_c_re', 'delta_s5_c_im', 'delta_s5_d', 'delta_s5_log_dt', 'delta_s5_glu_w', 'delta_s5_glu_b', 'delta_c_conv_w', 'delta_d_conv_w', 'delta_d_a_log', 'delta_d_dt_bias', 'delta_d_norm_g', 'delta_w_out', 'delta_final_g', 'new_m_norm_g', 'new_m_w_in', 'new_m_a_conv_w', 'new_m_a_conv_b', 'new_m_a_ln_g', 'new_m_a_ln_b', 'new_m_a_pw_w', 'new_m_a_pw_b', 'new_m_s5_lambda_re', 'new_m_s5_lambda_im', 'new_m_s5_b_re', 'new_m_s5_b_im', 'new_m_s5_c_re', 'new_m_s5_c_im', 'new_m_s5_d', 'new_m_s5_log_dt', 'new_m_s5_glu_w', 'new_m_s5_glu_b', 'new_m_c_conv_w', 'new_m_d_conv_w', 'new_m_d_a_log', 'new_m_d_dt_bias', 'new_m_d_norm_g', 'new_m_w_out', 'new_m_final_g', 'new_v_norm_g', 'new_v_w_in', 'new_v_a_conv_w', 'new_v_a_conv_b', 'new_v_a_ln_g', 'new_v_a_ln_b', 'new_v_a_pw_w', 'new_v_a_pw_b', 'new_v_s5_lambda_re', 'new_v_s5_lambda_im', 'new_v_s5_b_re', 'new_v_s5_b_im', 'new_v_s5_c_re', 'new_v_s5_c_im', 'new_v_s5_d', 'new_v_s5_log_dt', 'new_v_s5_glu_w', 'new_v_s5_glu_b', 'new_v_c_conv_w', 'new_v_d_conv_w', 'new_v_d_a_log', 'new_v_d_dt_bias', 'new_v_d_norm_g', 'new_v_w_out', 'new_v_final_g']
TWIN_LEAF_KINDS = {'loss': 'loss', 'grad_x': 'grad_x', 'grad_norm_g': 'grad_w', 'grad_w_in': 'grad_w', 'grad_a_conv_w': 'grad_w', 'grad_a_conv_b': 'grad_w', 'grad_a_ln_g': 'grad_w', 'grad_a_ln_b': 'grad_w', 'grad_a_pw_w': 'grad_w', 'grad_a_pw_b': 'grad_w', 'grad_s5_lambda_re': 'grad_w', 'grad_s5_lambda_im': 'grad_w', 'grad_s5_b_re': 'grad_w', 'grad_s5_b_im': 'grad_w', 'grad_s5_c_re': 'grad_w', 'grad_s5_c_im': 'grad_w', 'grad_s5_d': 'grad_w', 'grad_s5_log_dt': 'grad_w', 'grad_s5_glu_w': 'grad_w', 'grad_s5_glu_b': 'grad_w', 'grad_c_conv_w': 'grad_w', 'grad_d_conv_w': 'grad_w', 'grad_d_a_log': 'grad_w', 'grad_d_dt_bias': 'grad_w', 'grad_d_norm_g': 'grad_w', 'grad_w_out': 'grad_w', 'grad_final_g': 'grad_w', 'delta_norm_g': 'delta_w', 'delta_w_in': 'delta_w', 'delta_a_conv_w': 'delta_w', 'delta_a_conv_b': 'delta_w', 'delta_a_ln_g': 'delta_w', 'delta_a_ln_b': 'delta_w', 'delta_a_pw_w': 'delta_w', 'delta_a_pw_b': 'delta_w', 'delta_s5_lambda_re': 'delta_w', 'delta_s5_lambda_im': 'delta_w', 'delta_s5_b_re': 'delta_w', 'delta_s5_b_im': 'delta_w', 'delta_s5_c_re': 'delta_w', 'delta_s5_c_im': 'delta_w', 'delta_s5_d': 'delta_w', 'delta_s5_log_dt': 'delta_w', 'delta_s5_glu_w': 'delta_w', 'delta_s5_glu_b': 'delta_w', 'delta_c_conv_w': 'delta_w', 'delta_d_conv_w': 'delta_w', 'delta_d_a_log': 'delta_w', 'delta_d_dt_bias': 'delta_w', 'delta_d_norm_g': 'delta_w', 'delta_w_out': 'delta_w', 'delta_final_g': 'delta_w', 'new_m_norm_g': 'new_m', 'new_m_w_in': 'new_m', 'new_m_a_conv_w': 'new_m', 'new_m_a_conv_b': 'new_m', 'new_m_a_ln_g': 'new_m', 'new_m_a_ln_b': 'new_m', 'new_m_a_pw_w': 'new_m', 'new_m_a_pw_b': 'new_m', 'new_m_s5_lambda_re': 'new_m', 'new_m_s5_lambda_im': 'new_m', 'new_m_s5_b_re': 'new_m', 'new_m_s5_b_im': 'new_m', 'new_m_s5_c_re': 'new_m', 'new_m_s5_c_im': 'new_m', 'new_m_s5_d': 'new_m', 'new_m_s5_log_dt': 'new_m', 'new_m_s5_glu_w': 'new_m', 'new_m_s5_glu_b': 'new_m', 'new_m_c_conv_w': 'new_m', 'new_m_d_conv_w': 'new_m', 'new_m_d_a_log': 'new_m', 'new_m_d_dt_bias': 'new_m', 'new_m_d_norm_g': 'new_m', 'new_m_w_out': 'new_m', 'new_m_final_g': 'new_m', 'new_v_norm_g': 'new_v', 'new_v_w_in': 'new_v', 'new_v_a_conv_w': 'new_v', 'new_v_a_conv_b': 'new_v', 'new_v_a_ln_g': 'new_v', 'new_v_a_ln_b': 'new_v', 'new_v_a_pw_w': 'new_v', 'new_v_a_pw_b': 'new_v', 'new_v_s5_lambda_re': 'new_v', 'new_v_s5_lambda_im': 'new_v', 'new_v_s5_b_re': 'new_v', 'new_v_s5_b_im': 'new_v', 'new_v_s5_c_re': 'new_v', 'new_v_s5_c_im': 'new_v', 'new_v_s5_d': 'new_v', 'new_v_s5_log_dt': 'new_v', 'new_v_s5_glu_w': 'new_v', 'new_v_s5_glu_b': 'new_v', 'new_v_c_conv_w': 'new_v', 'new_v_d_conv_w': 'new_v', 'new_v_d_a_log': 'new_v', 'new_v_d_dt_bias': 'new_v', 'new_v_d_norm_g': 'new_v', 'new_v_w_out': 'new_v', 'new_v_final_g': 'new_v'}


def _forward(args):
    return _fwd_reference(*[args[k] for k in FWD_PARAMS])


def _output_shape():
    def fwd():
        inp = _fwd_setup_inputs(0)
        return _fwd_reference(*[inp[k] for k in FWD_PARAMS])
    out = _jax.eval_shape(fwd)
    return out.shape, out.dtype

N_MICROBATCH = 1
ADAM_LR = 0.001
ADAM_B1 = 0.9
ADAM_B2 = 0.999
ADAM_EPS = 1e-08
ADAM_WD = 0.01
ADAM_STEP = 10
PER_EXAMPLE_BATCH_AXIS = {'x': 0, 'loss_target': 0}
SHARED_INPUTS = []
_WEIGHT_DTYPES = {'norm_g': _jnp.float32, 'w_in': _jnp.float32, 'a_conv_w': _jnp.float32, 'a_conv_b': _jnp.float32, 'a_ln_g': _jnp.float32, 'a_ln_b': _jnp.float32, 'a_pw_w': _jnp.float32, 'a_pw_b': _jnp.float32, 's5_lambda_re': _jnp.float32, 's5_lambda_im': _jnp.float32, 's5_b_re': _jnp.float32, 's5_b_im': _jnp.float32, 's5_c_re': _jnp.float32, 's5_c_im': _jnp.float32, 's5_d': _jnp.float32, 's5_log_dt': _jnp.float32, 's5_glu_w': _jnp.float32, 's5_glu_b': _jnp.float32, 'c_conv_w': _jnp.float32, 'd_conv_w': _jnp.float32, 'd_a_log': _jnp.float32, 'd_dt_bias': _jnp.float32, 'd_norm_g': _jnp.float32, 'w_out': _jnp.float32, 'final_g': _jnp.float32}
MOMENT_SCALE = {'norm_g': 1.664648e-01, 'w_in': 9.195574e-02, 'a_conv_w': 7.065971e-02, 'a_conv_b': 1.606762e-01, 'a_ln_g': 8.288496e-02, 'a_ln_b': 7.227826e-02, 'a_pw_w': 6.843780e-02, 'a_pw_b': 1.168932e-01, 's5_lambda_re': 1.726862e-03, 's5_lambda_im': 1.598592e-03, 's5_b_re': 1.191989e-03, 's5_b_im': 1.195595e-03, 's5_c_re': 2.344892e-03, 's5_c_im': 2.377052e-03, 's5_d': 4.101118e-02, 's5_log_dt': 1.184541e+00, 's5_glu_w': 2.982314e-03, 's5_glu_b': 9.998051e-03, 'c_conv_w': 1.127775e-01, 'd_conv_w': 1.087398e-01, 'd_a_log': 4.305778e-01, 'd_dt_bias': 4.025376e-01, 'd_norm_g': 2.193750e-01, 'w_out': 8.623871e-02, 'final_g': 3.201784e+01}


def _to_microbatches(a, axis):
    t = _jnp.moveaxis(a, axis, 0)
    t = t.reshape((N_MICROBATCH, t.shape[0] // N_MICROBATCH) + t.shape[1:])
    return _jnp.moveaxis(t, 1, axis + 1)


def setup_inputs(seed: int = 0) -> dict:
    inp = _fwd_setup_inputs(seed)
    key = _jax.random.fold_in(_jax.random.key(seed), 7919)
    shape, _ = _output_shape()
    out = dict(inp)
    out["loss_target"] = _jax.random.normal(_jax.random.fold_in(key, 0), shape, _jnp.float32)
    for i, name in enumerate(TWIN_WEIGHTS):
        w = inp[name].astype(_jnp.float32)
        if MOMENT_SCALE is None:
            s = _jnp.sqrt(_jnp.mean(_jnp.square(w)) + 1e-30)
        else:
            s = MOMENT_SCALE[name]
        km, kv = _jax.random.split(_jax.random.fold_in(key, i + 1))
        out[name] = w
        out["m_" + name] = s * _jax.random.normal(km, w.shape, _jnp.float32)
        out["v_" + name] = (s * s) * _jax.random.uniform(kv, w.shape, _jnp.float32, 0.5, 1.5)
    if N_MICROBATCH > 1:
        for name, axis in PER_EXAMPLE_BATCH_AXIS.items():
            out[name] = _to_microbatches(out[name], axis)
    return {'x': out['x'], 'norm_g': out['norm_g'], 'w_in': out['w_in'], 'a_conv_w': out['a_conv_w'], 'a_conv_b': out['a_conv_b'], 'a_ln_g': out['a_ln_g'], 'a_ln_b': out['a_ln_b'], 'a_pw_w': out['a_pw_w'], 'a_pw_b': out['a_pw_b'], 's5_lambda_re': out['s5_lambda_re'], 's5_lambda_im': out['s5_lambda_im'], 's5_b_re': out['s5_b_re'], 's5_b_im': out['s5_b_im'], 's5_c_re': out['s5_c_re'], 's5_c_im': out['s5_c_im'], 's5_d': out['s5_d'], 's5_log_dt': out['s5_log_dt'], 's5_glu_w': out['s5_glu_w'], 's5_glu_b': out['s5_glu_b'], 'c_conv_w': out['c_conv_w'], 'd_conv_w': out['d_conv_w'], 'd_a_log': out['d_a_log'], 'd_dt_bias': out['d_dt_bias'], 'd_norm_g': out['d_norm_g'], 'w_out': out['w_out'], 'final_g': out['final_g'], 'loss_target': out['loss_target'], 'm_norm_g': out['m_norm_g'], 'm_w_in': out['m_w_in'], 'm_a_conv_w': out['m_a_conv_w'], 'm_a_conv_b': out['m_a_conv_b'], 'm_a_ln_g': out['m_a_ln_g'], 'm_a_ln_b': out['m_a_ln_b'], 'm_a_pw_w': out['m_a_pw_w'], 'm_a_pw_b': out['m_a_pw_b'], 'm_s5_lambda_re': out['m_s5_lambda_re'], 'm_s5_lambda_im': out['m_s5_lambda_im'], 'm_s5_b_re': out['m_s5_b_re'], 'm_s5_b_im': out['m_s5_b_im'], 'm_s5_c_re': out['m_s5_c_re'], 'm_s5_c_im': out['m_s5_c_im'], 'm_s5_d': out['m_s5_d'], 'm_s5_log_dt': out['m_s5_log_dt'], 'm_s5_glu_w': out['m_s5_glu_w'], 'm_s5_glu_b': out['m_s5_glu_b'], 'm_c_conv_w': out['m_c_conv_w'], 'm_d_conv_w': out['m_d_conv_w'], 'm_d_a_log': out['m_d_a_log'], 'm_d_dt_bias': out['m_d_dt_bias'], 'm_d_norm_g': out['m_d_norm_g'], 'm_w_out': out['m_w_out'], 'm_final_g': out['m_final_g'], 'v_norm_g': out['v_norm_g'], 'v_w_in': out['v_w_in'], 'v_a_conv_w': out['v_a_conv_w'], 'v_a_conv_b': out['v_a_conv_b'], 'v_a_ln_g': out['v_a_ln_g'], 'v_a_ln_b': out['v_a_ln_b'], 'v_a_pw_w': out['v_a_pw_w'], 'v_a_pw_b': out['v_a_pw_b'], 'v_s5_lambda_re': out['v_s5_lambda_re'], 'v_s5_lambda_im': out['v_s5_lambda_im'], 'v_s5_b_re': out['v_s5_b_re'], 'v_s5_b_im': out['v_s5_b_im'], 'v_s5_c_re': out['v_s5_c_re'], 'v_s5_c_im': out['v_s5_c_im'], 'v_s5_d': out['v_s5_d'], 'v_s5_log_dt': out['v_s5_log_dt'], 'v_s5_glu_w': out['v_s5_glu_w'], 'v_s5_glu_b': out['v_s5_glu_b'], 'v_c_conv_w': out['v_c_conv_w'], 'v_d_conv_w': out['v_d_conv_w'], 'v_d_a_log': out['v_d_a_log'], 'v_d_dt_bias': out['v_d_dt_bias'], 'v_d_norm_g': out['v_d_norm_g'], 'v_w_out': out['v_w_out'], 'v_final_g': out['v_final_g']}


def _loss(weights, diff, rest, loss_target):
    with _jax.named_scope("forward"):
        args = {**rest, TWIN_DIFF_INPUT: diff, **{k: w.astype(_WEIGHT_DTYPES[k]) for k, w in weights.items()}}
        y = _forward(args)
    with _jax.named_scope("loss_head"):
        err = _jnp.square(y.astype(_jnp.float32) - loss_target)
        return 0.5 * _jnp.sum(_jnp.mean(err, axis=-1)) if err.ndim else 0.5 * err


def _adamw(w, g, m, v):
    m = ADAM_B1 * m + (1.0 - ADAM_B1) * g
    v = ADAM_B2 * v + (1.0 - ADAM_B2) * _jnp.square(g)
    m_hat = m / (1.0 - ADAM_B1 ** ADAM_STEP)
    v_hat = v / (1.0 - ADAM_B2 ** ADAM_STEP)
    delta = -ADAM_LR * (m_hat / (_jnp.sqrt(v_hat) + ADAM_EPS) + ADAM_WD * w)
    return delta, m, v


def reference(x, norm_g, w_in, a_conv_w, a_conv_b, a_ln_g, a_ln_b, a_pw_w, a_pw_b, s5_lambda_re, s5_lambda_im, s5_b_re, s5_b_im, s5_c_re, s5_c_im, s5_d, s5_log_dt, s5_glu_w, s5_glu_b, c_conv_w, d_conv_w, d_a_log, d_dt_bias, d_norm_g, w_out, final_g, loss_target, m_norm_g, m_w_in, m_a_conv_w, m_a_conv_b, m_a_ln_g, m_a_ln_b, m_a_pw_w, m_a_pw_b, m_s5_lambda_re, m_s5_lambda_im, m_s5_b_re, m_s5_b_im, m_s5_c_re, m_s5_c_im, m_s5_d, m_s5_log_dt, m_s5_glu_w, m_s5_glu_b, m_c_conv_w, m_d_conv_w, m_d_a_log, m_d_dt_bias, m_d_norm_g, m_w_out, m_final_g, v_norm_g, v_w_in, v_a_conv_w, v_a_conv_b, v_a_ln_g, v_a_ln_b, v_a_pw_w, v_a_pw_b, v_s5_lambda_re, v_s5_lambda_im, v_s5_b_re, v_s5_b_im, v_s5_c_re, v_s5_c_im, v_s5_d, v_s5_log_dt, v_s5_glu_w, v_s5_glu_b, v_c_conv_w, v_d_conv_w, v_d_a_log, v_d_dt_bias, v_d_norm_g, v_w_out, v_final_g):
    given = dict(x=x, norm_g=norm_g, w_in=w_in, a_conv_w=a_conv_w, a_conv_b=a_conv_b, a_ln_g=a_ln_g, a_ln_b=a_ln_b, a_pw_w=a_pw_w, a_pw_b=a_pw_b, s5_lambda_re=s5_lambda_re, s5_lambda_im=s5_lambda_im, s5_b_re=s5_b_re, s5_b_im=s5_b_im, s5_c_re=s5_c_re, s5_c_im=s5_c_im, s5_d=s5_d, s5_log_dt=s5_log_dt, s5_glu_w=s5_glu_w, s5_glu_b=s5_glu_b, c_conv_w=c_conv_w, d_conv_w=d_conv_w, d_a_log=d_a_log, d_dt_bias=d_dt_bias, d_norm_g=d_norm_g, w_out=w_out, final_g=final_g, loss_target=loss_target, m_norm_g=m_norm_g, m_w_in=m_w_in, m_a_conv_w=m_a_conv_w, m_a_conv_b=m_a_conv_b, m_a_ln_g=m_a_ln_g, m_a_ln_b=m_a_ln_b, m_a_pw_w=m_a_pw_w, m_a_pw_b=m_a_pw_b, m_s5_lambda_re=m_s5_lambda_re, m_s5_lambda_im=m_s5_lambda_im, m_s5_b_re=m_s5_b_re, m_s5_b_im=m_s5_b_im, m_s5_c_re=m_s5_c_re, m_s5_c_im=m_s5_c_im, m_s5_d=m_s5_d, m_s5_log_dt=m_s5_log_dt, m_s5_glu_w=m_s5_glu_w, m_s5_glu_b=m_s5_glu_b, m_c_conv_w=m_c_conv_w, m_d_conv_w=m_d_conv_w, m_d_a_log=m_d_a_log, m_d_dt_bias=m_d_dt_bias, m_d_norm_g=m_d_norm_g, m_w_out=m_w_out, m_final_g=m_final_g, v_norm_g=v_norm_g, v_w_in=v_w_in, v_a_conv_w=v_a_conv_w, v_a_conv_b=v_a_conv_b, v_a_ln_g=v_a_ln_g, v_a_ln_b=v_a_ln_b, v_a_pw_w=v_a_pw_w, v_a_pw_b=v_a_pw_b, v_s5_lambda_re=v_s5_lambda_re, v_s5_lambda_im=v_s5_lambda_im, v_s5_b_re=v_s5_b_re, v_s5_b_im=v_s5_b_im, v_s5_c_re=v_s5_c_re, v_s5_c_im=v_s5_c_im, v_s5_d=v_s5_d, v_s5_log_dt=v_s5_log_dt, v_s5_glu_w=v_s5_glu_w, v_s5_glu_b=v_s5_glu_b, v_c_conv_w=v_c_conv_w, v_d_conv_w=v_d_conv_w, v_d_a_log=v_d_a_log, v_d_dt_bias=v_d_dt_bias, v_d_norm_g=v_d_norm_g, v_w_out=v_w_out, v_final_g=v_final_g)
    weights = {n: given[n] for n in TWIN_WEIGHTS}
    shared = {n: given[n] for n in SHARED_INPUTS}
    per_example = {n: given[n] for n in ['x']}
    grad_fn = _jax.value_and_grad(_loss, argnums=(0, 1))

    def one_microbatch(ex, loss_target):
        ex = dict(ex)
        diff = ex.pop(TWIN_DIFF_INPUT)
        return grad_fn(weights, diff, {**shared, **ex}, loss_target)

    if N_MICROBATCH == 1:
        loss, (grad_w, grad_x) = one_microbatch(per_example, given["loss_target"])
    else:
        def body(carry, xs):
            loss_sum, grad_sum = carry
            l_k, (gw_k, gx_k) = one_microbatch(xs[0], xs[1])
            with _jax.named_scope("update"):
                return (loss_sum + l_k, _jax.tree.map(_jnp.add, grad_sum, gw_k)), gx_k

        init = (_jnp.zeros((), _jnp.float32), _jax.tree.map(_jnp.zeros_like, weights))
        (loss, grad_w), grad_x = _jax.lax.scan(body, init, (per_example, given["loss_target"]))
    with _jax.named_scope("update"):
        delta_w, new_m, new_v = {}, {}, {}
        for n in TWIN_WEIGHTS:
            delta_w[n], new_m[n], new_v[n] = _adamw(weights[n], grad_w[n], given["m_" + n], given["v_" + n])
    return (loss, grad_x, *[grad_w[n] for n in TWIN_WEIGHTS], *[delta_w[n] for n in TWIN_WEIGHTS],
            *[new_m[n] for n in TWIN_WEIGHTS], *[new_v[n] for n in TWIN_WEIGHTS])
```

```python
import functools

import jax
import jax.numpy as jnp
from jax import lax
from jax.experimental import pallas as pl
from jax.experimental.pallas import tpu as pltpu

F32 = jnp.float32
BF16 = jnp.bfloat16
HI = lax.Precision.HIGHEST
SDS = jax.ShapeDtypeStruct

N_DEV = 8
D = 1024
BR = 256
DEPTH = 4
IN_COLS = 3336
PW = 3456
AB_COL = 3328
EPS = 1e-6
TL = 512
SEG = TL // 8
HALO = 32
HALO_S = 8
KA, KC, KD = 31, 3, 4
CH = 64
NH, HD = 4, 64
NSTATE = 1024
VMEM_LIMIT = 56 * 1024 * 1024

ADAM_LR, ADAM_B1, ADAM_B2, ADAM_EPS, ADAM_WD, ADAM_STEP = 0.001, 0.9, 0.999, 1e-08, 0.01, 10

NN = ((1,), (0,))
NT = ((1,), (1,))
TN = ((0,), (0,))


def _dot(a, b, dims, prec=None):
    return lax.dot_general(a, b, (dims, ((), ())), precision=prec, preferred_element_type=F32)


def _make_mm(cast, prec, fwd_dims):
    def prep(t):
        return t.astype(cast) if cast is not None else t

    @jax.custom_vjp
    def mm(a, w):
        return _dot(prep(a), prep(w), fwd_dims, prec)

    def fwd(a, w):
        return mm(a, w), (a, w)

    def bwd(res, dy):
        a, w = res
        a, w, dy = prep(a), prep(w), prep(dy)
        if fwd_dims == NN:
            return _dot(dy, w, NT, prec), _dot(a, dy, TN, prec)
        if fwd_dims == NT:
            return _dot(dy, w, NN, prec), _dot(dy, a, TN, prec)
        return _dot(w, dy, NT, prec), _dot(a, dy, NN, prec)

    mm.defvjp(fwd, bwd)
    return mm


mm = _make_mm(BF16, None, NN)
mm_nt = _make_mm(BF16, None, NT)
mm_tn = _make_mm(BF16, None, TN)
mmh = _make_mm(None, HI, NN)
mmh_nt = _make_mm(None, HI, NT)


def _sigmoid(x):
    return jax.nn.sigmoid(x)


def _silu(x):
    return x * jax.nn.sigmoid(x)


def _gelu(x):
    return 0.5 * x * (1.0 + jnp.tanh(0.7978845608028654 * (x + 0.044715 * (x * x * x))))


def _softplus(x):
    return jnp.maximum(x, 0.0) + jnp.log1p(jnp.exp(-jnp.abs(x)))


def _rms(x, g):
    return x * lax.rsqrt(jnp.mean(x * x, axis=-1, keepdims=True) + EPS) * g


def _cparams(sem):
    return pltpu.CompilerParams(dimension_semantics=sem, vmem_limit_bytes=VMEM_LIMIT)


def _tap_offsets(halo, taps):
    return [halo - (taps - 1) + k for k in range(taps)]


def _conv_fwd_impl(acat, w, tile, halo, taps):
    n = tile + halo
    out = None
    for k, off in enumerate(_tap_offsets(halo, taps)):
        src = jnp.roll(acat, n - off, axis=0)[:tile, :] if off != halo else acat[halo:, :]
        term = src * w[k:k + 1, :]
        out = term if out is None else out + term
    return out


def _make_conv(tile, halo, taps):
    @jax.custom_vjp
    def conv(acat, w):
        return _conv_fwd_impl(acat, w, tile, halo, taps)

    def fwd(acat, w):
        return conv(acat, w), (acat, w)

    def bwd(res, dy):
        acat, w = res
        n = tile + halo
        dyp = jnp.concatenate([dy, jnp.zeros((halo, dy.shape[1]), F32)], axis=0)
        rows = lax.broadcasted_iota(jnp.int32, w.shape, 0)
        dacat = None
        dw = jnp.zeros(w.shape, F32)
        for k, off in enumerate(_tap_offsets(halo, taps)):
            term = jnp.roll(dyp, off, axis=0) * w[k:k + 1, :]
            dacat = term if dacat is None else dacat + term
            src = jnp.roll(acat, n - off, axis=0)[:tile, :] if off != halo else acat[halo:, :]
            dw = dw + jnp.where(rows == k, jnp.sum(dy * src, axis=0, keepdims=True), 0.0)
        return dacat, dw

    conv.defvjp(fwd, bwd)
    return conv


def _halo_spec(tile, halo, width, col):
    per = tile // halo
    return pl.BlockSpec((halo, width), lambda i: (jnp.maximum(i * per - 1, 0), col))


def _halo_spec_rev(nt, tile, halo, width, col):
    per = tile // halo
    return pl.BlockSpec((halo, width), lambda i: (jnp.maximum((nt - 1 - i) * per - 1, 0), col))


def _inproj_call(x, g, w):
    L = x.shape[0]
    tn = 1152

    def body(x_ref, g_ref, w_ref, p_ref, h_ref):
        h = _rms(x_ref[...], g_ref[...]).astype(BF16)
        h_ref[...] = h
        p_ref[...] = _dot(h, w_ref[...], NN)

    return pl.pallas_call(
        body, name="inproj", grid=(L // TL, PW // tn),
        in_specs=[pl.BlockSpec((TL, D), lambda i, j: (i, 0)), pl.BlockSpec((1, D), lambda i, j: (0, 0)),
                  pl.BlockSpec((D, tn), lambda i, j: (0, j))],
        out_specs=[pl.BlockSpec((TL, tn), lambda i, j: (i, j)), pl.BlockSpec((TL, D), lambda i, j: (i, 0))],
        out_shape=[SDS((L, PW), F32), SDS((L, D), BF16)],
        compiler_params=_cparams(("parallel", "arbitrary")),
    )(x, g, w)


def _outproj_call(x, ys, w):
    L = x.shape[0]

    def body(x_ref, a_ref, b_ref, c_ref, d_ref, w_ref, o_ref):
        acc = x_ref[...]
        for b, y_ref in enumerate((a_ref, b_ref, c_ref, d_ref)):
            acc = acc + _dot(y_ref[...].astype(BF16), w_ref[b * BR:(b + 1) * BR, :], NN)
        o_ref[...] = acc

    yspec = pl.BlockSpec((TL, BR), lambda i: (i, 0))
    return pl.pallas_call(
        body, name="outproj", grid=(L // TL,),
        in_specs=[pl.BlockSpec((TL, D), lambda i: (i, 0)), yspec, yspec, yspec, yspec,
                  pl.BlockSpec((D, D), lambda i: (0, 0))],
        out_specs=pl.BlockSpec((TL, D), lambda i: (i, 0)),
        out_shape=SDS((L, D), F32),
        compiler_params=_cparams(("parallel",)),
    )(x, *ys, w)


def _loss_call(x, g, target):
    L = x.shape[0]

    def body(x_ref, g_ref, t_ref, dx_ref, dg_ref, loss_ref):
        @pl.when(pl.program_id(0) == 0)
        def _():
            dg_ref[...] = jnp.zeros_like(dg_ref)
            loss_ref[...] = jnp.zeros_like(loss_ref)

        y, vjp = jax.vjp(_rms, x_ref[...], g_ref[...])
        err = y - t_ref[...]
        dx, dg = vjp(err * (1.0 / D))
        dx_ref[...] = dx
        dg_ref[...] += dg
        tot = jnp.sum(jnp.sum(err * err, axis=1, keepdims=True), axis=0, keepdims=True)
        loss_ref[...] += jnp.broadcast_to(tot * (0.5 / D), loss_ref.shape)

    return pl.pallas_call(
        body, name="loss_head", grid=(L // TL,),
        in_specs=[pl.BlockSpec((TL, D), lambda i: (i, 0)), pl.BlockSpec((1, D), lambda i: (0, 0)),
                  pl.BlockSpec((TL, D), lambda i: (i, 0))],
        out_specs=[pl.BlockSpec((TL, D), lambda i: (i, 0)), pl.BlockSpec((1, D), lambda i: (0, 0)),
                   pl.BlockSpec((1, 128), lambda i: (0, 0))],
        out_shape=[SDS((L, D), F32), SDS((1, D), F32), SDS((1, 128), F32)],
        compiler_params=_cparams(("arbitrary",)),
    )(x, g, target)


def _branch_a(valw, gatew, z, cw, cb, lg, lb, pw, pb, conv):
    a = conv(valw * _sigmoid(gatew), cw) + cb
    mu = jnp.mean(a, axis=-1, keepdims=True)
    xc = a - mu
    y = xc * lax.rsqrt(jnp.mean(xc * xc, axis=-1, keepdims=True) + EPS) * lg + lb
    y = mm(_silu(y), pw) + pb
    return y * _silu(z)


def _branch_c(bg, cw_, xw, z, w3, conv):
    return bg * conv(cw_ * xw, w3) * _silu(z)


def _ac_fwd_call(proj, p):
    L = proj.shape[0]

    def body(val, gate, za, hval, hgate, cb_, cc, cx, cz, hcc, hcx,
             acw, acb, alg, alb, apw, apb, ccw, ya_ref, yc_ref):
        nf = (pl.program_id(0) > 0).astype(F32)
        win = lambda h, m: jnp.concatenate([h[...] * nf, m[...]], axis=0)
        conv_a = functools.partial(_conv_fwd_impl, tile=TL, halo=HALO, taps=KA)
        conv_c = functools.partial(_conv_fwd_impl, tile=TL, halo=HALO, taps=KC)
        ya_ref[...] = _branch_a(win(hval, val), win(hgate, gate), za[...], acw[...], acb[...], alg[...], alb[...],
                                apw[...], apb[...], conv_a)
        yc_ref[...] = _branch_c(cb_[...], win(hcc, cc), win(hcx, cx), cz[...], ccw[...], conv_c)

    col = lambda j: pl.BlockSpec((TL, BR), lambda i: (i, j))
    hal = lambda j: _halo_spec(TL, HALO, BR, j)
    full = lambda a: pl.BlockSpec(a.shape, lambda i: (0,) * a.ndim)
    params = (p["a_conv_w"], p["a_conv_b"], p["a_ln_g"], p["a_ln_b"], p["a_pw_w"], p["a_pw_b"], p["c_conv_w"])
    return pl.pallas_call(
        body, name="ac_fwd", grid=(L // TL,),
        in_specs=[col(0), col(1), col(2), hal(0), hal(1), col(5), col(6), col(7), col(8), hal(6), hal(7)]
        + [full(a) for a in params],
        out_specs=[pl.BlockSpec((TL, BR), lambda i: (i, 0))] * 2,
        out_shape=[SDS((L, BR), F32)] * 2,
        compiler_params=_cparams(("parallel",)),
    )(*([proj] * 11), *params)


def _ac_bwd_call(proj, p, dya, dyc):
    L = proj.shape[0]
    nt = L // TL

    def body(val, gate, za, hval, hgate, cb_, cc, cx, cz, hcc, hcx,
             acw, acb, alg, alb, apw, apb, ccw, dya_ref, dyc_ref,
             da_ref, dc_ref, g_acw, g_acb, g_alg, g_alb, g_apw, g_apb, g_ccw, carry):
        i = pl.program_id(0)
        gouts = (g_acw, g_acb, g_alg, g_alb, g_apw, g_apb, g_ccw)

        @pl.when(i == 0)
        def _():
            carry[...] = jnp.zeros_like(carry)
            for r in gouts:
                r[...] = jnp.zeros_like(r)

        nf = (i < nt - 1).astype(F32)
        win = lambda h, m: jnp.concatenate([h[...] * nf, m[...]], axis=0)
        conv_a = _make_conv(TL, HALO, KA)
        conv_c = _make_conv(TL, HALO, KC)

        def f(valw, gatew, z, bg, ccw_, cxw, czv, w1, b1, lg, lb, pw, pb, w3):
            return (_branch_a(valw, gatew, z, w1, b1, lg, lb, pw, pb, conv_a),
                    _branch_c(bg, ccw_, cxw, czv, w3, conv_c))

        _, vjp = jax.vjp(f, win(hval, val), win(hgate, gate), za[...], cb_[...], win(hcc, cc), win(hcx, cx), cz[...],
                         acw[...], acb[...], alg[...], alb[...], apw[...].astype(F32), apb[...], ccw[...])
        (dvalw, dgatew, dz, dbg, dccw, dcxw, dczv, d1, d2, d3, d4, d5, d6, d7) = vjp((dya_ref[...], dyc_ref[...]))

        def settle(slot, dwin):
            tail = jnp.concatenate([jnp.zeros((TL - HALO, BR), F32), carry[slot]], axis=0)
            carry[slot] = dwin[:HALO, :]
            return (dwin[HALO:, :] + tail).astype(BF16)

        da_ref[:, 0:BR] = settle(0, dvalw)
        da_ref[:, BR:2 * BR] = settle(1, dgatew)
        da_ref[:, 2 * BR:3 * BR] = dz.astype(BF16)
        dc_ref[:, 0:BR] = dbg.astype(BF16)
        dc_ref[:, BR:2 * BR] = settle(2, dccw)
        dc_ref[:, 2 * BR:3 * BR] = settle(3, dcxw)
        dc_ref[:, 3 * BR:4 * BR] = dczv.astype(BF16)
        for r, g in zip(gouts, (d1, d2, d3, d4, d5, d6, d7)):
            r[...] += g

    col = lambda j: pl.BlockSpec((TL, BR), lambda i: (nt - 1 - i, j))
    hal = lambda j: _halo_spec_rev(nt, TL, HALO, BR, j)
    full = lambda a: pl.BlockSpec(a.shape, lambda i: (0,) * a.ndim)
    params = (p["a_conv_w"], p["a_conv_b"], p["a_ln_g"], p["a_ln_b"], p["a_pw_w"], p["a_pw_b"], p["c_conv_w"])
    rev = lambda w: pl.BlockSpec((TL, w), lambda i: (nt - 1 - i, 0))
    return pl.pallas_call(
        body, name="ac_bwd", grid=(nt,),
        in_specs=[col(0), col(1), col(2), hal(0), hal(1), col(5), col(6), col(7), col(8), hal(6), hal(7)]
        + [full(a) for a in params] + [rev(BR), rev(BR)],
        out_specs=[rev(3 * BR), rev(4 * BR)] + [full(a) for a in params],
        out_shape=[SDS((L, 3 * BR), BF16), SDS((L, 4 * BR), BF16)] + [SDS(a.shape, F32) for a in params],
        scratch_shapes=[pltpu.VMEM((4, HALO, BR), F32)],
        compiler_params=_cparams(("arbitrary",)),
    )(*([proj] * 11), *params, dya, dyc)


def _iota2(shape, dim):
    return lax.broadcasted_iota(jnp.int32, shape, dim)


def _s5_params(lam_re, lam_im, logdt, b_re, b_im, c_re, c_im):
    eg = (_iota2((16, NSTATE), 1) >> 6 == _iota2((16, NSTATE), 0)).astype(F32)
    dt = jnp.exp(mmh(jnp.broadcast_to(logdt, (8, 16)), eg)[0:1, :])
    lr = jnp.minimum(lam_re, -1e-4)
    li = lam_im
    mag = jnp.exp(lr * dt)
    lbr = mag * jnp.cos(li * dt)
    lbi = mag * jnp.sin(li * dt)
    den = lr * lr + li * li
    nr = lbr - 1.0
    fr = (nr * lr + lbi * li) / den
    fi = (lbi * lr - nr * li) / den
    row = _iota2((8, NSTATE), 0)
    f8 = jnp.where(row == 0, fr, jnp.where(row == 1, fi, 0.0))
    eye = (_iota2((NSTATE, NSTATE), 0) == _iota2((NSTATE, NSTATE), 1)).astype(F32)
    fcol = mmh_nt(eye, f8)
    frc, fic = fcol[:, 0:1], fcol[:, 1:2]
    bbr = frc * b_re - fic * b_im
    bbi = frc * b_im + fic * b_re
    e1 = ((_iota2((16, BR), 1) & 15) == _iota2((16, BR), 0)).astype(F32)
    m1 = ((_iota2((NSTATE, BR), 0) >> 6) == (_iota2((NSTATE, BR), 1) >> 4)).astype(F32)
    wbr = mmh(bbr, e1) * m1
    wbi = mmh(bbi, e1) * m1
    e2 = ((_iota2((64, NSTATE), 1) & 63) == _iota2((64, NSTATE), 0)).astype(F32)
    m2 = ((_iota2((BR, NSTATE), 0) >> 4) == (_iota2((BR, NSTATE), 1) >> 6)).astype(F32)
    wcr = mmh(c_re, e2) * m2
    wci = mmh(c_im, e2) * m2
    return lbr, lbi, wbr, wbi, wcr, wci


_S5_OUT = [(1, NSTATE), (1, NSTATE), (NSTATE, BR), (NSTATE, BR), (BR, NSTATE), (BR, NSTATE)]


def _s5_prep_call(sp):
    def body(lre, lim, ldt, bre, bim, cre, cim, o_lbr, o_lbi, o_wbr, o_wbi, o_wcr, o_wci, pwr, pwi, qwr, qwi):
        lbr, lbi, wbr, wbi, wcr, wci = _s5_params(lre[...], lim[...], ldt[...], bre[...], bim[...], cre[...], cim[...])
        o_lbr[...], o_lbi[...], o_wbr[...], o_wbi[...], o_wcr[...], o_wci[...] = lbr, lbi, wbr, wbi, wcr, wci
        pr, pi = lbr, lbi
        for i in range(SEG):
            pwr[i:i + 1, :] = pr
            pwi[i:i + 1, :] = pi
            qwr[SEG - 1 - i:SEG - i, :] = pr
            qwi[SEG - 1 - i:SEG - i, :] = -pi
            pr, pi = pr * lbr - pi * lbi, pr * lbi + pi * lbr

    args = (sp["lam_re"], sp["lam_im"], sp["log_dt"], sp["b_re"], sp["b_im"], sp["c_re"], sp["c_im"])
    return pl.pallas_call(
        body, name="s5_prep",
        out_shape=[SDS(s, F32) for s in _S5_OUT] + [SDS((SEG, NSTATE), F32)] * 4,
        compiler_params=pltpu.CompilerParams(vmem_limit_bytes=VMEM_LIMIT),
    )(*args)


def _s5_prep_bwd_call(sp, cots):
    def body(lre, lim, ldt, bre, bim, cre, cim, c0, c1, c2, c3, c4, c5, *outs):
        _, vjp = jax.vjp(_s5_params, lre[...], lim[...], ldt[...], bre[...], bim[...], cre[...], cim[...])
        grads = vjp((c0[...], c1[...], c2[...], c3[...], c4[...], c5[...]))
        for o, g in zip(outs, grads):
            o[...] = g

    args = (sp["lam_re"], sp["lam_im"], sp["log_dt"], sp["b_re"], sp["b_im"], sp["c_re"], sp["c_im"])
    return pl.pallas_call(
        body, name="s5_prep_bwd",
        out_shape=[SDS(a.shape, F32) for a in args],
        compiler_params=pltpu.CompilerParams(vmem_limit_bytes=VMEM_LIMIT),
    )(*args, *cots)


def _lanes(v, j):
    return v[:, j * 128:(j + 1) * 128]


def _s5_scan(sre, sim, lbr, lbi, pwr, pwi, cin_r, cin_i, reverse):
    lr = [jnp.broadcast_to(_lanes(lbr, j), (8, 128)) for j in range(8)]
    li = [jnp.broadcast_to(_lanes(lbi, j), (8, 128)) for j in range(8)]

    def step(t, st):
        i = SEG - 1 - t if reverse else t
        rows = pl.ds(i, 8, stride=SEG)
        new = []
        for j in range(8):
            sr, si = st[2 * j], st[2 * j + 1]
            vr, vi = sre.at[j], sim.at[j]
            nr = lr[j] * sr - li[j] * si + vr[rows, :]
            ni = lr[j] * si + li[j] * sr + vi[rows, :]
            vr[rows, :] = nr
            vi[rows, :] = ni
            new += [nr, ni]
        return tuple(new)

    ends = lax.fori_loop(0, SEG, step, tuple(jnp.zeros((8, 128), F32) for _ in range(16)))
    e_r = jnp.concatenate([ends[2 * j] for j in range(8)], axis=1)
    e_i = jnp.concatenate([ends[2 * j + 1] for j in range(8)], axis=1)
    l64r, l64i = (pwr[0:1, :], pwi[0:1, :]) if reverse else (pwr[SEG - 1:SEG, :], pwi[SEG - 1:SEG, :])
    order = range(7, -1, -1) if reverse else range(8)
    c_r, c_i = cin_r, cin_i
    carries = {}
    for k in order:
        carries[k] = (c_r, c_i)
        er, ei = e_r[k:k + 1, :], e_i[k:k + 1, :]
        c_r, c_i = er + l64r * c_r - l64i * c_i, ei + l64r * c_i + l64i * c_r
    for j in range(8):
        pr, pi = pwr[:, j * 128:(j + 1) * 128], pwi[:, j * 128:(j + 1) * 128]
        for k in range(8):
            cr, ci = _lanes(carries[k][0], j), _lanes(carries[k][1], j)
            blk = slice(k * SEG, (k + 1) * SEG)
            sre[j, blk, :] = sre[j, blk, :] + pr * cr - pi * ci
            sim[j, blk, :] = sim[j, blk, :] + pr * ci + pi * cr
    return c_r, c_i


def _s5_states(u, wbr, wbi, sre, sim):
    bur = _dot(u, wbr, NT, HI)
    bui = _dot(u, wbi, NT, HI)
    for j in range(8):
        sre[j] = _lanes(bur, j)
        sim[j] = _lanes(bui, j)


def _gather_lanes(s):
    return jnp.concatenate([s[j] for j in range(8)], axis=1)


def _s5_post(s_re, s_im, u, z, wcr, wci, dsk, gw, gb):
    y = mmh_nt(s_re, wcr) - mmh_nt(s_im, wci) + dsk * u
    yg = _gelu(y)
    return yg * _sigmoid(mm(yg, gw) + gb) * _silu(z)


def _s5_fwd_call(proj, prep, p):
    L = proj.shape[0]
    nt = L // TL
    lbr, lbi, wbr, wbi, wcr, wci, pwr, pwi, _, _ = prep

    def body(u_ref, z_ref, lbr_r, lbi_r, wbr_r, wbi_r, wcr_r, wci_r, pwr_r, pwi_r, d_r, gw_r, gb_r,
             yb_ref, cinr_ref, cini_ref, sre, sim, car, cai):
        @pl.when(pl.program_id(0) == 0)
        def _():
            car[...] = jnp.zeros_like(car)
            cai[...] = jnp.zeros_like(cai)

        u = u_ref[...]
        cinr_ref[0] = car[...]
        cini_ref[0] = cai[...]
        _s5_states(u, wbr_r[...], wbi_r[...], sre, sim)
        nr, ni = _s5_scan(sre, sim, lbr_r[...], lbi_r[...], pwr_r, pwi_r, car[...], cai[...], False)
        car[...] = nr
        cai[...] = ni
        yb_ref[...] = _s5_post(_gather_lanes(sre), _gather_lanes(sim), u, z_ref[...], wcr_r[...], wci_r[...],
                               d_r[...], gw_r[...], gb_r[...])

    full = lambda a: pl.BlockSpec(a.shape, lambda i: (0,) * a.ndim)
    consts = (lbr, lbi, wbr, wbi, wcr, wci, pwr, pwi, p["s5_d"], p["s5_glu_w"], p["s5_glu_b"])
    cspec = pl.BlockSpec((1, 1, NSTATE), lambda i: (i, 0, 0))
    return pl.pallas_call(
        body, name="s5_fwd", grid=(nt,),
        in_specs=[pl.BlockSpec((TL, BR), lambda i: (i, 3)), pl.BlockSpec((TL, BR), lambda i: (i, 4))]
        + [full(a) for a in consts],
        out_specs=[pl.BlockSpec((TL, BR), lambda i: (i, 0)), cspec, cspec],
        out_shape=[SDS((L, BR), F32), SDS((nt, 1, NSTATE), F32), SDS((nt, 1, NSTATE), F32)],
        scratch_shapes=[pltpu.VMEM((8, TL, 128), F32), pltpu.VMEM((8, TL, 128), F32),
                        pltpu.VMEM((1, NSTATE), F32), pltpu.VMEM((1, NSTATE), F32)],
        compiler_params=_cparams(("arbitrary",)),
    )(proj, proj, *consts)


def _s5_bwd_call(proj, prep, p, cin_r, cin_i, dyb):
    L = proj.shape[0]
    nt = L // TL
    lbr, lbi, wbr, wbi, wcr, wci, pwr, pwi, qwr, qwi = prep

    def body(u_ref, z_ref, lbr_r, lbi_r, wbr_r, wbi_r, wcr_r, wci_r, pwr_r, pwi_r, qwr_r, qwi_r, d_r, gw_r, gb_r,
             cinr_ref, cini_ref, dy_ref,
             db_ref, g_lbr, g_lbi, g_wbr, g_wbi, g_wcr, g_wci, g_d, g_gw, g_gb, sre, sim, gre, gim, car, cai):
        gouts = (g_lbr, g_lbi, g_wbr, g_wbi, g_wcr, g_wci, g_d, g_gw, g_gb)

        @pl.when(pl.program_id(0) == 0)
        def _():
            car[...] = jnp.zeros_like(car)
            cai[...] = jnp.zeros_like(cai)
            for r in gouts:
                r[...] = jnp.zeros_like(r)

        u = u_ref[...]
        lr, li = lbr_r[...], lbi_r[...]
        c0r, c0i = cinr_ref[0], cini_ref[0]
        _s5_states(u, wbr_r[...], wbi_r[...], sre, sim)
        _s5_scan(sre, sim, lr, li, pwr_r, pwi_r, c0r, c0i, False)
        s_re, s_im = _gather_lanes(sre), _gather_lanes(sim)
        _, vjp = jax.vjp(_s5_post, s_re, s_im, u, z_ref[...], wcr_r[...], wci_r[...], d_r[...],
                         gw_r[...].astype(F32), gb_r[...])
        ds_re, ds_im, du, dz, dwcr, dwci, dd, dgw, dgb = vjp(dy_ref[...])
        for j in range(8):
            gre[j] = _lanes(ds_re, j)
            gim[j] = _lanes(ds_im, j)
        nr, ni = _s5_scan(gre, gim, lr, -li, qwr_r, qwi_r, car[...], cai[...], True)
        car[...] = nr
        cai[...] = ni
        a_re, a_im = _gather_lanes(gre), _gather_lanes(gim)
        first = _iota2((TL, NSTATE), 0) == 0
        p_re = jnp.where(first, c0r, jnp.roll(s_re, 1, axis=0))
        p_im = jnp.where(first, c0i, jnp.roll(s_im, 1, axis=0))
        g_lbr[...] += jnp.sum(a_re * p_re + a_im * p_im, axis=0, keepdims=True)
        g_lbi[...] += jnp.sum(a_im * p_re - a_re * p_im, axis=0, keepdims=True)
        du = du + _dot(a_re, wbr_r[...], NN, HI) + _dot(a_im, wbi_r[...], NN, HI)
        g_wbr[...] += _dot(a_re, u, TN, HI)
        g_wbi[...] += _dot(a_im, u, TN, HI)
        g_wcr[...] += dwcr
        g_wci[...] += dwci
        g_d[...] += dd
        g_gw[...] += dgw
        g_gb[...] += dgb
        db_ref[:, 0:BR] = du.astype(BF16)
        db_ref[:, BR:2 * BR] = dz.astype(BF16)

    full = lambda a: pl.BlockSpec(a.shape, lambda i: (0,) * a.ndim)
    consts = (lbr, lbi, wbr, wbi, wcr, wci, pwr, pwi, qwr, qwi, p["s5_d"], p["s5_glu_w"], p["s5_glu_b"])
    cspec = pl.BlockSpec((1, 1, NSTATE), lambda i: (nt - 1 - i, 0, 0))
    gshapes = _S5_OUT + [(1, BR), (BR, BR), (1, BR)]
    return pl.pallas_call(
        body, name="s5_bwd", grid=(nt,),
        in_specs=[pl.BlockSpec((TL, BR), lambda i: (nt - 1 - i, 3)), pl.BlockSpec((TL, BR), lambda i: (nt - 1 - i, 4))]
        + [full(a) for a in consts] + [cspec, cspec, pl.BlockSpec((TL, BR), lambda i: (nt - 1 - i, 0))],
        out_specs=[pl.BlockSpec((TL, 2 * BR), lambda i: (nt - 1 - i, 0))]
        + [pl.BlockSpec(s, lambda i: (0, 0)) for s in gshapes],
        out_shape=[SDS((L, 2 * BR), BF16)] + [SDS(s, F32) for s in gshapes],
        scratch_shapes=[pltpu.VMEM((8, TL, 128), F32)] * 4 + [pltpu.VMEM((1, NSTATE), F32)] * 2,
        compiler_params=_cparams(("arbitrary",)),
    )(proj, proj, *consts, cin_r, cin_i, dyb)


def _heads(x):
    return [x[:, h * HD:(h + 1) * HD] for h in range(NH)]


def _l2n(x, scale):
    return jnp.concatenate([xh * (lax.rsqrt(jnp.sum(xh * xh, axis=-1, keepdims=True) + EPS) * scale)
                            for xh in _heads(x)], axis=1)


def _dn_pre(qkvw, ab, cw, alog, dtb, conv, rows):
    c = _silu(conv(qkvw, cw))
    q = _l2n(c[:, 0:BR], HD ** -0.5)
    k = _l2n(c[:, BR:2 * BR], 1.0)
    v = c[:, 2 * BR:3 * BR]
    g = -jnp.exp(alog) * _softplus(ab + dtb)
    ri, ci = _iota2((rows, rows), 0), _iota2((rows, rows), 1)
    tri = ((ri >= ci) & ((ri >> 6) == (ci >> 6))).astype(F32)
    gc = mmh(tri, g)
    lane = _iota2(ab.shape, 1)
    return q, k, v, jnp.where(lane < NH, gc, jnp.where(lane < 2 * NH, _sigmoid(ab), 0.0))


def _dn_pre_fwd_call(proj, p):
    L = proj.shape[0]

    def body(m_ref, h_ref, ab_ref, cw, alog, dtb, q_ref, k_ref, v_ref, gb_ref):
        nf = (pl.program_id(0) > 0).astype(F32)
        qkvw = jnp.concatenate([h_ref[...] * nf, m_ref[...]], axis=0)
        conv = functools.partial(_conv_fwd_impl, tile=TL, halo=HALO_S, taps=KD)
        q_ref[...], k_ref[...], v_ref[...], gb_ref[...] = _dn_pre(qkvw, ab_ref[...], cw[...], alog[...], dtb[...], conv, TL)

    full = lambda a: pl.BlockSpec(a.shape, lambda i: (0,) * a.ndim)
    params = (p["d_conv_w"], p["d_a_log"], p["d_dt_bias"])
    o = pl.BlockSpec((TL, BR), lambda i: (i, 0))
    return pl.pallas_call(
        body, name="dn_pre_fwd", grid=(L // TL,),
        in_specs=[pl.BlockSpec((TL, 3 * BR), lambda i: (i, 3)), _halo_spec(TL, HALO_S, 3 * BR, 3),
                  pl.BlockSpec((TL, 128), lambda i: (i, AB_COL // 128))] + [full(a) for a in params],
        out_specs=[o, o, o, pl.BlockSpec((TL, 128), lambda i: (i, 0))],
        out_shape=[SDS((L, BR), F32)] * 3 + [SDS((L, 128), F32)],
        compiler_params=_cparams(("parallel",)),
    )(proj, proj, proj, *params)


def _dn_pre_bwd_call(proj, p, dq, dk, dv, dgb):
    L = proj.shape[0]
    nt = L // TL

    def body(m_ref, h_ref, ab_ref, cw, alog, dtb, dq_r, dk_r, dv_r, dgb_r,
             dqkv_ref, dab_ref, g_cw, g_alog, g_dtb, carry):
        i = pl.program_id(0)

        @pl.when(i == 0)
        def _():
            carry[...] = jnp.zeros_like(carry)
            for r in (g_cw, g_alog, g_dtb):
                r[...] = jnp.zeros_like(r)

        nf = (i < nt - 1).astype(F32)
        qkvw = jnp.concatenate([h_ref[...] * nf, m_ref[...]], axis=0)
        conv = _make_conv(TL, HALO_S, KD)
        _, vjp = jax.vjp(lambda a, b, c, d, e: _dn_pre(a, b, c, d, e, conv, TL),
                         qkvw, ab_ref[...], cw[...], alog[...], dtb[...])
        dwin, dab, dcw, dalog, ddtb = vjp((dq_r[...], dk_r[...], dv_r[...], dgb_r[...]))
        tail = jnp.concatenate([jnp.zeros((TL - HALO_S, 3 * BR), F32), carry[...]], axis=0)
        carry[...] = dwin[:HALO_S, :]
        dqkv_ref[...] = (dwin[HALO_S:, :] + tail).astype(BF16)
        dab_ref[...] = dab.astype(BF16)
        g_cw[...] += dcw
        g_alog[...] += dalog
        g_dtb[...] += ddtb

    full = lambda a: pl.BlockSpec(a.shape, lambda i: (0,) * a.ndim)
    params = (p["d_conv_w"], p["d_a_log"], p["d_dt_bias"])
    rev = lambda w: pl.BlockSpec((TL, w), lambda i: (nt - 1 - i, 0))
    return pl.pallas_call(
        body, name="dn_pre_bwd", grid=(nt,),
        in_specs=[pl.BlockSpec((TL, 3 * BR), lambda i: (nt - 1 - i, 3)), _halo_spec_rev(nt, TL, HALO_S, 3 * BR, 3),
                  pl.BlockSpec((TL, 128), lambda i: (nt - 1 - i, AB_COL // 128))] + [full(a) for a in params]
        + [rev(BR), rev(BR), rev(BR), rev(128)],
        out_specs=[rev(3 * BR), rev(128)] + [full(a) for a in params],
        out_shape=[SDS((L, 3 * BR), BF16), SDS((L, 128), BF16)] + [SDS(a.shape, F32) for a in params],
        scratch_shapes=[pltpu.VMEM((HALO_S, 3 * BR), F32)],
        compiler_params=_cparams(("arbitrary",)),
    )(proj, proj, proj, *params, dq, dk, dv, dgb)


def _tri_inverse(lm, eye):
    t = eye - lm
    pw = lm
    for _ in range(5):
        pw = mm(pw, pw)
        t = mm(t, eye + pw)
    return t


def _dn_chunk(q, k, v, gb, z, ng, s):
    ri, ci = _iota2((CH, CH), 0), _iota2((CH, CH), 1)
    causal, strict = ri >= ci, ri > ci
    eye = (ri == ci).astype(F32)
    outs, s_new = [], []
    for h, (qh, kh, vh, zh) in enumerate(zip(_heads(q), _heads(k), _heads(v), _heads(z))):
        gc = jnp.broadcast_to(gb[:, h:h + 1], (CH, HD))
        beta = gb[:, NH + h:NH + h + 1]
        diff = gc - gc.T
        decay = jnp.where(causal, jnp.exp(jnp.where(causal, diff, 0.0)), 0.0)
        egc = jnp.exp(gc)
        glast = gc[CH - 1:CH, :]
        kb = kh * beta
        lmat = jnp.where(strict, mm_nt(kb, kh) * decay, 0.0)
        t = _tri_inverse(lmat, eye)
        u = mm(t, vh * beta)
        w = mm(t, kb * egc)
        attn = mm_nt(qh, kh) * decay
        sh = s[h]
        v_new = u - mm(w, sh)
        o = mm(qh * egc, sh) + mm(attn, v_new)
        s_new.append(sh * jnp.exp(glast[:, 0:1]) + mm_tn(kh * jnp.exp(glast - gc), v_new))
        o = o * lax.rsqrt(jnp.mean(o * o, axis=-1, keepdims=True) + EPS) * ng
        outs.append(o * _silu(zh))
    return jnp.concatenate(outs, axis=1), jnp.stack(s_new, axis=0)


def _dn_core_fwd_call(proj, q, k, v, gb, ng):
    L = q.shape[0]
    nc = L // CH

    def body(q_r, k_r, v_r, gb_r, z_r, ng_r, yd_ref, ssave_ref, s_scr):
        @pl.when(pl.program_id(0) == 0)
        def _():
            s_scr[...] = jnp.zeros_like(s_scr)

        s = s_scr[...]
        ssave_ref[0] = s
        yd, s2 = _dn_chunk(q_r[...], k_r[...], v_r[...], gb_r[...], z_r[...], ng_r[...], s)
        yd_ref[...] = yd
        s_scr[...] = s2

    c = pl.BlockSpec((CH, BR), lambda i: (i, 0))
    return pl.pallas_call(
        body, name="dn_core_fwd", grid=(nc,),
        in_specs=[c, c, c, pl.BlockSpec((CH, 128), lambda i: (i, 0)), pl.BlockSpec((CH, BR), lambda i: (i, 12)),
                  pl.BlockSpec((1, HD), lambda i: (0, 0))],
        out_specs=[c, pl.BlockSpec((1, NH, HD, HD), lambda i: (i, 0, 0, 0))],
        out_shape=[SDS((L, BR), F32), SDS((nc, NH, HD, HD), F32)],
        scratch_shapes=[pltpu.VMEM((NH, HD, HD), F32)],
        compiler_params=_cparams(("arbitrary",)),
    )(q, k, v, gb, proj, ng)


def _dn_core_bwd_call(proj, q, k, v, gb, ng, ssave, dyd):
    L = q.shape[0]
    nc = L // CH

    def body(q_r, k_r, v_r, gb_r, z_r, ng_r, s_r, dy_r, dq_ref, dk_ref, dv_ref, dgb_ref, dz_ref, g_ng, ds_scr):
        @pl.when(pl.program_id(0) == 0)
        def _():
            ds_scr[...] = jnp.zeros_like(ds_scr)
            g_ng[...] = jnp.zeros_like(g_ng)

        _, vjp = jax.vjp(_dn_chunk, q_r[...], k_r[...], v_r[...], gb_r[...], z_r[...], ng_r[...], s_r[0])
        dq, dk, dv, dgb, dz, dng, ds = vjp((dy_r[...], ds_scr[...]))
        dq_ref[...], dk_ref[...], dv_ref[...], dgb_ref[...] = dq, dk, dv, dgb
        dz_ref[...] = dz.astype(BF16)
        g_ng[...] += dng
        ds_scr[...] = ds

    c = pl.BlockSpec((CH, BR), lambda i: (nc - 1 - i, 0))
    c128 = pl.BlockSpec((CH, 128), lambda i: (nc - 1 - i, 0))
    return pl.pallas_call(
        body, name="dn_core_bwd", grid=(nc,),
        in_specs=[c, c, c, c128, pl.BlockSpec((CH, BR), lambda i: (nc - 1 - i, 12)),
                  pl.BlockSpec((1, HD), lambda i: (0, 0)),
                  pl.BlockSpec((1, NH, HD, HD), lambda i: (nc - 1 - i, 0, 0, 0)), c],
        out_specs=[c, c, c, c128, c, pl.BlockSpec((1, HD), lambda i: (0, 0))],
        out_shape=[SDS((L, BR), F32)] * 3 + [SDS((L, 128), F32), SDS((L, BR), BF16), SDS((1, HD), F32)],
        scratch_shapes=[pltpu.VMEM((NH, HD, HD), F32)],
        compiler_params=_cparams(("arbitrary",)),
    )(q, k, v, gb, proj, ng, ssave, dyd)


def _outproj_bwd_call(dx, ys, w):
    L = dx.shape[0]

    def body(dx_ref, a_ref, b_ref, c_ref, d_ref, w_ref, da, db, dc, dd, dw_ref):
        @pl.when(pl.program_id(0) == 0)
        def _():
            dw_ref[...] = jnp.zeros_like(dw_ref)

        dxb = dx_ref[...].astype(BF16)
        for b, (y_ref, o_ref) in enumerate(zip((a_ref, b_ref, c_ref, d_ref), (da, db, dc, dd))):
            o_ref[...] = _dot(dxb, w_ref[b * BR:(b + 1) * BR, :], NT)
            dw_ref[b * BR:(b + 1) * BR, :] += _dot(y_ref[...].astype(BF16), dxb, TN)

    yspec = pl.BlockSpec((TL, BR), lambda i: (i, 0))
    return pl.pallas_call(
        body, name="outproj_bwd", grid=(L // TL,),
        in_specs=[pl.BlockSpec((TL, D), lambda i: (i, 0)), yspec, yspec, yspec, yspec,
                  pl.BlockSpec((D, D), lambda i: (0, 0))],
        out_specs=[yspec] * 4 + [pl.BlockSpec((D, D), lambda i: (0, 0))],
        out_shape=[SDS((L, BR), F32)] * 4 + [SDS((D, D), F32)],
        compiler_params=_cparams(("arbitrary",)),
    )(dx, *ys, w)


def _inproj_bwd_x_call(dproj, w, x, g, dx_next):
    L = x.shape[0]

    def body(dp_ref, w_ref, x_ref, g_ref, dxn_ref, dx_ref, dg_ref):
        @pl.when(pl.program_id(0) == 0)
        def _():
            dg_ref[...] = jnp.zeros_like(dg_ref)

        dh = _dot(dp_ref[...], w_ref[...], NT)
        _, vjp = jax.vjp(_rms, x_ref[...], g_ref[...])
        dx, dg = vjp(dh)
        dx_ref[...] = dx + dxn_ref[...]
        dg_ref[...] += dg

    row = lambda w_: pl.BlockSpec((TL, w_), lambda i: (i, 0))
    return pl.pallas_call(
        body, name="inproj_bwd_x", grid=(L // TL,),
        in_specs=[row(PW), pl.BlockSpec((D, PW), lambda i: (0, 0)), row(D), pl.BlockSpec((1, D), lambda i: (0, 0)), row(D)],
        out_specs=[row(D), pl.BlockSpec((1, D), lambda i: (0, 0))],
        out_shape=[SDS((L, D), F32), SDS((1, D), F32)],
        compiler_params=_cparams(("arbitrary",)),
    )(dproj, w, x, g, dx_next)


def _inproj_bwd_w_call(h, dproj):
    L = h.shape[0]
    tn = 1152

    def body(h_ref, dp_ref, dw_ref):
        @pl.when(pl.program_id(1) == 0)
        def _():
            dw_ref[...] = jnp.zeros_like(dw_ref)

        dw_ref[...] += _dot(h_ref[...], dp_ref[...], TN)

    return pl.pallas_call(
        body, name="inproj_bwd_w", grid=(PW // tn, L // TL),
        in_specs=[pl.BlockSpec((TL, D), lambda j, i: (i, 0)), pl.BlockSpec((TL, tn), lambda j, i: (i, j))],
        out_specs=pl.BlockSpec((D, tn), lambda j, i: (0, j)),
        out_shape=SDS((D, PW), F32),
        compiler_params=_cparams(("parallel", "arbitrary")),
    )(h, dproj)


def _exchange_call(src, per_peer, name):
    blk = src.shape[1:] if per_peer else src.shape

    def body(src_ref, dst_ref, send_sems, recv_sems, local_sem):
        x, y, c = lax.axis_index("x"), lax.axis_index("y"), lax.axis_index("c")
        me = 4 * x + 2 * y + c
        copies = []
        for mask in range(1, N_DEV):
            px = 1 - x if mask & 4 else x
            py = 1 - y if mask & 2 else y
            pc = 1 - c if mask & 1 else c
            s = src_ref.at[4 * px + 2 * py + pc] if per_peer else src_ref
            cp = pltpu.make_async_remote_copy(
                src_ref=s, dst_ref=dst_ref.at[me], send_sem=send_sems.at[mask - 1], recv_sem=recv_sems.at[mask - 1],
                device_id=(px, py, pc), device_id_type=pl.DeviceIdType.MESH)
            cp.start()
            copies.append(cp)
        mine = pltpu.make_async_copy(src_ref.at[me] if per_peer else src_ref, dst_ref.at[me], local_sem)
        mine.start()
        for cp in copies:
            cp.wait()
        mine.wait()

    return pl.pallas_call(
        body, name=name,
        in_specs=[pl.BlockSpec(memory_space=pl.ANY)],
        out_specs=pl.BlockSpec(memory_space=pl.ANY),
        out_shape=SDS((N_DEV,) + tuple(blk), src.dtype),
        scratch_shapes=[pltpu.SemaphoreType.DMA((N_DEV - 1,)), pltpu.SemaphoreType.DMA((N_DEV - 1,)),
                        pltpu.SemaphoreType.DMA],
    )(src)


def _reduce_adamw_call(parts, w, m, v, name):
    R = w.shape[0]
    tr = 1000 if R % 1000 == 0 else R
    c1 = 1.0 - ADAM_B1 ** ADAM_STEP
    c2 = 1.0 - ADAM_B2 ** ADAM_STEP

    def body(p_ref, w_ref, m_ref, v_ref, g_ref, d_ref, nm_ref, nv_ref):
        g = p_ref[0]
        for k in range(1, N_DEV):
            g = g + p_ref[k]
        nm = ADAM_B1 * m_ref[...] + (1.0 - ADAM_B1) * g
        nv = ADAM_B2 * v_ref[...] + (1.0 - ADAM_B2) * (g * g)
        g_ref[...] = g
        nm_ref[...] = nm
        nv_ref[...] = nv
        d_ref[...] = -ADAM_LR * ((nm / c1) / (jnp.sqrt(nv / c2) + ADAM_EPS) + ADAM_WD * w_ref[...])

    row = pl.BlockSpec((tr, 128), lambda i: (i, 0))
    return pl.pallas_call(
        body, name=name, grid=(R // tr,),
        in_specs=[pl.BlockSpec((N_DEV, tr, 128), lambda i: (0, i, 0)), row, row, row],
        out_specs=[row] * 4,
        out_shape=[SDS((R, 128), F32)] * 4,
        compiler_params=_cparams(("parallel",)),
    )(parts, w, m, v)


_SHARDED = ("w_in", "w_out", "a_pw_w", "s5_glu_w", "a_conv_w", "c_conv_w", "d_conv_w")
_SHARD_SHAPES = {"w_in": (DEPTH, D, IN_COLS // N_DEV), "w_out": (DEPTH, D // N_DEV, D),
                 "a_pw_w": (DEPTH, BR // N_DEV, BR), "s5_glu_w": (DEPTH, BR // N_DEV, BR),
                 "a_conv_w": (DEPTH, KA, BR // N_DEV), "c_conv_w": (DEPTH, KC, BR // N_DEV),
                 "d_conv_w": (DEPTH, KD, 3 * BR // N_DEV)}
_REPLICATED = ("norm_g", "a_conv_b", "a_ln_g", "a_ln_b", "a_pw_b", "s5_lambda_re", "s5_lambda_im", "s5_b_re", "s5_b_im",
               "s5_c_re", "s5_c_im", "s5_d", "s5_log_dt", "s5_glu_b", "d_a_log", "d_dt_bias", "d_norm_g", "final_g")
_WEIGHTS = ("norm_g", "w_in", "a_conv_w", "a_conv_b", "a_ln_g", "a_ln_b", "a_pw_w", "a_pw_b", "s5_lambda_re",
            "s5_lambda_im", "s5_b_re", "s5_b_im", "s5_c_re", "s5_c_im", "s5_d", "s5_log_dt", "s5_glu_w", "s5_glu_b",
            "c_conv_w", "d_conv_w", "d_a_log", "d_dt_bias", "d_norm_g", "w_out", "final_g")


def _size(shape):
    n = 1
    for s in shape:
        n *= s
    return n


def _pack_rows(pieces, row_mult):
    flat = jnp.concatenate([p.reshape(-1) for p in pieces])
    per = 128 * row_mult
    pad = (-flat.shape[0]) % per
    return jnp.pad(flat, (0, pad)).reshape(-1, 128)


def _unpack(flat, shapes):
    out, off = [], 0
    for s in shapes:
        n = _size(s)
        out.append(flat[off:off + n].reshape(s))
        off += n
    return out


def _gather_weights(shards):
    pieces = [shards[n].astype(BF16) for n in _SHARDED[:4]]
    pieces += [lax.bitcast_convert_type(shards[n], BF16) for n in _SHARDED[4:]]
    got = _exchange_call(_pack_rows(pieces, 16), False, "gather_weights").reshape(N_DEV, -1)
    shapes = [_SHARD_SHAPES[n] for n in _SHARDED[:4]] + [_SHARD_SHAPES[n] + (2,) for n in _SHARDED[4:]]
    per_dev = [_unpack(got[k], shapes) for k in range(N_DEV)]
    st = {n: jnp.stack([per_dev[k][i] for k in range(N_DEV)]) for i, n in enumerate(_SHARDED)}
    for n in _SHARDED[4:]:
        st[n] = lax.bitcast_convert_type(st[n], F32)
    cols = lambda a: jnp.moveaxis(a, 0, -2).reshape(a.shape[1:-1] + (N_DEV * a.shape[-1],))
    rows = lambda a: jnp.moveaxis(a, 0, 1).reshape((a.shape[1], N_DEV * a.shape[2]) + a.shape[3:])
    w_in = cols(st["w_in"])
    w_in = jnp.concatenate([w_in[..., :3072], w_in[..., 3080:], w_in[..., 3072:3080],
                            jnp.zeros((DEPTH, D, PW - IN_COLS), BF16)], axis=-1)
    padk = lambda a, k: jnp.pad(a, ((0, 0), (0, k - a.shape[1]), (0, 0)))
    return {"w_in": w_in, "w_out": rows(st["w_out"]), "a_pw_w": rows(st["a_pw_w"]), "s5_glu_w": rows(st["s5_glu_w"]),
            "a_conv_w": padk(cols(st["a_conv_w"]), HALO), "c_conv_w": padk(cols(st["c_conv_w"]), HALO_S),
            "d_conv_w": padk(cols(st["d_conv_w"]), HALO_S)}


def _scatter_layout(full):
    w_in = full["w_in"]
    w_in = jnp.concatenate([w_in[..., :3072], w_in[..., 3328:3336], w_in[..., 3072:3328]], axis=-1)
    cols = lambda a: jnp.moveaxis(a.reshape(a.shape[:-1] + (N_DEV, a.shape[-1] // N_DEV)), -2, 0)
    rows = lambda a: jnp.moveaxis(a.reshape((a.shape[0], N_DEV, a.shape[1] // N_DEV) + a.shape[2:]), 1, 0)
    by_dev = [cols(w_in), rows(full["w_out"]), rows(full["a_pw_w"]), rows(full["s5_glu_w"]),
              cols(full["a_conv_w"][:, :KA]), cols(full["c_conv_w"][:, :KC]), cols(full["d_conv_w"][:, :KD])]
    flat = jnp.concatenate([a.reshape(N_DEV, -1) for a in by_dev], axis=1)
    pad = (-flat.shape[1]) % (128 * 1000)
    return jnp.pad(flat, ((0, 0), (0, pad))).reshape(N_DEV, -1, 128)


def _s5_inputs(p):
    return {"lam_re": p["s5_lambda_re"].reshape(1, NSTATE), "lam_im": p["s5_lambda_im"].reshape(1, NSTATE),
            "log_dt": p["s5_log_dt"].reshape(1, 16), "b_re": p["s5_b_re"].reshape(NSTATE, 16),
            "b_im": p["s5_b_im"].reshape(NSTATE, 16), "c_re": p["s5_c_re"].reshape(BR, 64),
            "c_im": p["s5_c_im"].reshape(BR, 64)}


def _row(a, width=None):
    a = a.reshape(1, -1)
    return a if width is None else jnp.pad(a, ((0, 0), (0, width - a.shape[1])))


def _layer_params(p):
    q = dict(p)
    for n in ("norm_g", "a_conv_b", "a_ln_g", "a_ln_b", "a_pw_b", "s5_d", "s5_glu_b", "d_norm_g"):
        q[n] = _row(p[n])
    q["d_a_log"] = _row(p["d_a_log"], 128)
    q["d_dt_bias"] = _row(p["d_dt_bias"], 128)
    return q


def _layer_fwd(x, p):
    q = _layer_params(p)
    proj, h = _inproj_call(x, q["norm_g"], q["w_in"])
    ya, yc = _ac_fwd_call(proj, q)
    prep = _s5_prep_call(_s5_inputs(p))
    yb, cin_r, cin_i = _s5_fwd_call(proj, prep, q)
    dq, dk, dv, dgb = _dn_pre_fwd_call(proj, q)
    yd, ssave = _dn_core_fwd_call(proj, dq, dk, dv, dgb, q["d_norm_g"])
    x_next = _outproj_call(x, (ya, yb, yc, yd), q["w_out"])
    saved = dict(x=x, proj=proj, h=h, ya=ya, yb=yb, yc=yc, yd=yd, cin_r=cin_r, cin_i=cin_i,
                 q=dq, k=dk, v=dv, gb=dgb, ssave=ssave)
    return x_next, saved


def _layer_bwd(dx, p, sv):
    q = _layer_params(p)
    proj = sv["proj"]
    dya, dyb, dyc, dyd, g_wout = _outproj_bwd_call(dx, (sv["ya"], sv["yb"], sv["yc"], sv["yd"]), q["w_out"])
    dpa, dpc, g_acw, g_acb, g_alg, g_alb, g_apw, g_apb, g_ccw = _ac_bwd_call(proj, q, dya, dyc)
    sp = _s5_inputs(p)
    prep = _s5_prep_call(sp)
    dpb, *s5g = _s5_bwd_call(proj, prep, q, sv["cin_r"], sv["cin_i"], dyb)
    g_sd, g_gw, g_gb = s5g[6:]
    g_lre, g_lim, g_ldt, g_bre, g_bim, g_cre, g_cim = _s5_prep_bwd_call(sp, s5g[:6])
    dq, dk, dv, dgb, dz, g_ng = _dn_core_bwd_call(proj, sv["q"], sv["k"], sv["v"], sv["gb"], q["d_norm_g"], sv["ssave"], dyd)
    dqkv, dab, g_dcw, g_alog, g_dtb = _dn_pre_bwd_call(proj, q, dq, dk, dv, dgb)
    dproj = jnp.concatenate([dpa, dpb, dpc, dqkv, dz, dab], axis=1)
    g_win = _inproj_bwd_w_call(sv["h"], dproj)
    dx_prev, g_ng0 = _inproj_bwd_x_call(dproj, q["w_in"], sv["x"], q["norm_g"], dx)
    grads = {"norm_g": g_ng0.reshape(D), "w_in": g_win, "a_conv_w": g_acw, "a_conv_b": g_acb.reshape(BR),
             "a_ln_g": g_alg.reshape(BR), "a_ln_b": g_alb.reshape(BR), "a_pw_w": g_apw, "a_pw_b": g_apb.reshape(BR),
             "s5_lambda_re": g_lre.reshape(16, 64), "s5_lambda_im": g_lim.reshape(16, 64),
             "s5_b_re": g_bre.reshape(16, 64, 16), "s5_b_im": g_bim.reshape(16, 64, 16),
             "s5_c_re": g_cre.reshape(16, 16, 64), "s5_c_im": g_cim.reshape(16, 16, 64),
             "s5_d": g_sd.reshape(BR), "s5_log_dt": g_ldt.reshape(16), "s5_glu_w": g_gw, "s5_glu_b": g_gb.reshape(BR),
             "c_conv_w": g_ccw, "d_conv_w": g_dcw, "d_a_log": g_alog[0, :NH], "d_dt_bias": g_dtb[0, :NH],
             "d_norm_g": g_ng.reshape(HD), "w_out": g_wout}
    return dx_prev, grads


def _step(x, target, weights, moments_m, moments_v):
    full = _gather_weights({n: weights[n] for n in _SHARDED})
    layer_names = [n for n in _WEIGHTS if n != "final_g"]
    stacked = {n: (full[n] if n in full else weights[n]) for n in layer_names}

    x_out, saved = lax.scan(lambda xc, p: _layer_fwd(xc, p), x, stacked)
    dx, g_final, loss_part = _loss_call(x_out, _row(weights["final_g"]), target)
    dx0, grads = lax.scan(lambda dxc, ps: _layer_bwd(dxc, ps[0], ps[1]), dx, (stacked, saved), reverse=True)
    grads["final_g"] = g_final.reshape(D)
    loss = lax.psum(loss_part[0, 0], ("x", "y", "c"))

    parts = _exchange_call(_scatter_layout({n: grads[n] for n in _SHARDED}), True, "scatter_grads")
    own = lambda d: _pack_rows([d[n] for n in _SHARDED], 1000)
    res_s = _reduce_adamw_call(parts, own(weights), own(moments_m), own(moments_v), "adamw_sharded")
    res_s = [dict(zip(_SHARDED, _unpack(r.reshape(-1), [_SHARD_SHAPES[n] for n in _SHARDED]))) for r in res_s]

    rep_shapes = [weights[n].shape for n in _REPLICATED]
    pack = lambda d: _pack_rows([d[n] for n in _REPLICATED], 8)
    gathered = _exchange_call(pack(grads), False, "gather_small_grads")
    res_r = _reduce_adamw_call(gathered, pack(weights), pack(moments_m), pack(moments_v), "adamw_replicated")
    res_r = [dict(zip(_REPLICATED, _unpack(r.reshape(-1), rep_shapes))) for r in res_r]

    outs = [loss, dx0]
    for kind in range(4):
        outs += [res_s[kind][n] if n in _SHARDED else res_r[kind][n] for n in _WEIGHTS]
    return tuple(outs)


def kernel(x, norm_g, w_in, a_conv_w, a_conv_b, a_ln_g, a_ln_b, a_pw_w, a_pw_b, s5_lambda_re, s5_lambda_im, s5_b_re, s5_b_im, s5_c_re, s5_c_im, s5_d, s5_log_dt, s5_glu_w, s5_glu_b, c_conv_w, d_conv_w, d_a_log, d_dt_bias, d_norm_g, w_out, final_g, loss_target, m_norm_g, m_w_in, m_a_conv_w, m_a_conv_b, m_a_ln_g, m_a_ln_b, m_a_pw_w, m_a_pw_b, m_s5_lambda_re, m_s5_lambda_im, m_s5_b_re, m_s5_b_im, m_s5_c_re, m_s5_c_im, m_s5_d, m_s5_log_dt, m_s5_glu_w, m_s5_glu_b, m_c_conv_w, m_d_conv_w, m_d_a_log, m_d_dt_bias, m_d_norm_g, m_w_out, m_final_g, v_norm_g, v_w_in, v_a_conv_w, v_a_conv_b, v_a_ln_g, v_a_ln_b, v_a_pw_w, v_a_pw_b, v_s5_lambda_re, v_s5_lambda_im, v_s5_b_re, v_s5_b_im, v_s5_c_re, v_s5_c_im, v_s5_d, v_s5_log_dt, v_s5_glu_w, v_s5_glu_b, v_c_conv_w, v_d_conv_w, v_d_a_log, v_d_dt_bias, v_d_norm_g, v_w_out, v_final_g):
    weights = dict(norm_g=norm_g, w_in=w_in, a_conv_w=a_conv_w, a_conv_b=a_conv_b, a_ln_g=a_ln_g, a_ln_b=a_ln_b, a_pw_w=a_pw_w, a_pw_b=a_pw_b, s5_lambda_re=s5_lambda_re, s5_lambda_im=s5_lambda_im, s5_b_re=s5_b_re, s5_b_im=s5_b_im, s5_c_re=s5_c_re, s5_c_im=s5_c_im, s5_d=s5_d, s5_log_dt=s5_log_dt, s5_glu_w=s5_glu_w, s5_glu_b=s5_glu_b, c_conv_w=c_conv_w, d_conv_w=d_conv_w, d_a_log=d_a_log, d_dt_bias=d_dt_bias, d_norm_g=d_norm_g, w_out=w_out, final_g=final_g)
    mom_m = dict(norm_g=m_norm_g, w_in=m_w_in, a_conv_w=m_a_conv_w, a_conv_b=m_a_conv_b, a_ln_g=m_a_ln_g, a_ln_b=m_a_ln_b, a_pw_w=m_a_pw_w, a_pw_b=m_a_pw_b, s5_lambda_re=m_s5_lambda_re, s5_lambda_im=m_s5_lambda_im, s5_b_re=m_s5_b_re, s5_b_im=m_s5_b_im, s5_c_re=m_s5_c_re, s5_c_im=m_s5_c_im, s5_d=m_s5_d, s5_log_dt=m_s5_log_dt, s5_glu_w=m_s5_glu_w, s5_glu_b=m_s5_glu_b, c_conv_w=m_c_conv_w, d_conv_w=m_d_conv_w, d_a_log=m_d_a_log, d_dt_bias=m_d_dt_bias, d_norm_g=m_d_norm_g, w_out=m_w_out, final_g=m_final_g)
    mom_v = dict(norm_g=v_norm_g, w_in=v_w_in, a_conv_w=v_a_conv_w, a_conv_b=v_a_conv_b, a_ln_g=v_a_ln_g, a_ln_b=v_a_ln_b, a_pw_w=v_a_pw_w, a_pw_b=v_a_pw_b, s5_lambda_re=v_s5_lambda_re, s5_lambda_im=v_s5_lambda_im, s5_b_re=v_s5_b_re, s5_b_im=v_s5_b_im, s5_c_re=v_s5_c_re, s5_c_im=v_s5_c_im, s5_d=v_s5_d, s5_log_dt=v_s5_log_dt, s5_glu_w=v_s5_glu_w, s5_glu_b=v_s5_glu_b, c_conv_w=v_c_conv_w, d_conv_w=v_d_conv_w, d_a_log=v_d_a_log, d_dt_bias=v_d_dt_bias, d_norm_g=v_d_norm_g, w_out=v_w_out, final_g=v_final_g)
    outs = _step(x[0], loss_target[0], weights, mom_m, mom_v)
    return (outs[0], outs[1][None]) + outs[2:]
```

```python
import functools

import jax
import jax.numpy as jnp
from jax import lax
from jax.experimental import pallas as pl
from jax.experimental.pallas import tpu as pltpu

F32 = jnp.float32
BF16 = jnp.bfloat16
HI = lax.Precision.HIGHEST
SDS = jax.ShapeDtypeStruct

N_DEV = 8
D = 1024
BR = 256
DEPTH = 4
IN_COLS = 3336
PW = 3456
AB_COL = 3328
EPS = 1e-6
TL = 512
SEG = TL // 8
HALO = 32
HALO_S = 8
KA, KC, KD = 31, 3, 4
CH = 64
DN_GROUP = 4
NH, HD = 4, 64
NSTATE = 1024
VMEM_LIMIT = 56 * 1024 * 1024

ADAM_LR, ADAM_B1, ADAM_B2, ADAM_EPS, ADAM_WD, ADAM_STEP = 0.001, 0.9, 0.999, 1e-08, 0.01, 10

NN = ((1,), (0,))
NT = ((1,), (1,))
TN = ((0,), (0,))


def _dot(a, b, dims, prec=None):
    return lax.dot_general(a, b, (dims, ((), ())), precision=prec, preferred_element_type=F32)


def _make_mm(cast, prec, fwd_dims):
    def prep(t):
        return t.astype(cast) if cast is not None else t

    @jax.custom_vjp
    def mm(a, w):
        return _dot(prep(a), prep(w), fwd_dims, prec)

    def fwd(a, w):
        return mm(a, w), (a, w)

    def bwd(res, dy):
        a, w = res
        a, w, dy = prep(a), prep(w), prep(dy)
        if fwd_dims == NN:
            return _dot(dy, w, NT, prec), _dot(a, dy, TN, prec)
        if fwd_dims == NT:
            return _dot(dy, w, NN, prec), _dot(dy, a, TN, prec)
        return _dot(w, dy, NT, prec), _dot(a, dy, NN, prec)

    mm.defvjp(fwd, bwd)
    return mm


mm = _make_mm(BF16, None, NN)
mm_nt = _make_mm(BF16, None, NT)
mm_tn = _make_mm(BF16, None, TN)
mmh = _make_mm(None, HI, NN)
mmh_nt = _make_mm(None, HI, NT)


def _sigmoid(x):
    return jax.nn.sigmoid(x)


def _silu(x):
    return x * jax.nn.sigmoid(x)


def _gelu(x):
    return 0.5 * x * (1.0 + jnp.tanh(0.7978845608028654 * (x + 0.044715 * (x * x * x))))


def _softplus(x):
    return jnp.maximum(x, 0.0) + jnp.log1p(jnp.exp(-jnp.abs(x)))


def _rms(x, g):
    return x * lax.rsqrt(jnp.mean(x * x, axis=-1, keepdims=True) + EPS) * g


def _cparams(sem):
    return pltpu.CompilerParams(dimension_semantics=sem, vmem_limit_bytes=VMEM_LIMIT)


def _tap_offsets(halo, taps):
    return [halo - (taps - 1) + k for k in range(taps)]


def _conv_fwd_impl(acat, w, tile, halo, taps):
    n = tile + halo
    out = None
    for k, off in enumerate(_tap_offsets(halo, taps)):
        src = jnp.roll(acat, n - off, axis=0)[:tile, :] if off != halo else acat[halo:, :]
        term = src * w[k:k + 1, :]
        out = term if out is None else out + term
    return out


def _make_conv(tile, halo, taps):
    @jax.custom_vjp
    def conv(acat, w):
        return _conv_fwd_impl(acat, w, tile, halo, taps)

    def fwd(acat, w):
        return conv(acat, w), (acat, w)

    def bwd(res, dy):
        acat, w = res
        n = tile + halo
        dyp = jnp.concatenate([dy, jnp.zeros((halo, dy.shape[1]), F32)], axis=0)
        rows = lax.broadcasted_iota(jnp.int32, w.shape, 0)
        dacat = None
        dw = jnp.zeros(w.shape, F32)
        for k, off in enumerate(_tap_offsets(halo, taps)):
            term = jnp.roll(dyp, off, axis=0) * w[k:k + 1, :]
            dacat = term if dacat is None else dacat + term
            src = jnp.roll(acat, n - off, axis=0)[:tile, :] if off != halo else acat[halo:, :]
            dw = dw + jnp.where(rows == k, jnp.sum(dy * src, axis=0, keepdims=True), 0.0)
        return dacat, dw

    conv.defvjp(fwd, bwd)
    return conv


def _halo_spec(tile, halo, width, col):
    per = tile // halo
    return pl.BlockSpec((halo, width), lambda i: (jnp.maximum(i * per - 1, 0), col))


def _halo_spec_rev(nt, tile, halo, width, col):
    per = tile // halo
    return pl.BlockSpec((halo, width), lambda i: (jnp.maximum((nt - 1 - i) * per - 1, 0), col))


def _inproj_call(x, g, w):
    L = x.shape[0]
    tn = 1152

    def body(x_ref, g_ref, w_ref, p_ref, h_ref):
        h = _rms(x_ref[...], g_ref[...]).astype(BF16)
        h_ref[...] = h
        p_ref[...] = _dot(h, w_ref[...], NN)

    return pl.pallas_call(
        body, name="inproj", grid=(L // TL, PW // tn),
        in_specs=[pl.BlockSpec((TL, D), lambda i, j: (i, 0)), pl.BlockSpec((1, D), lambda i, j: (0, 0)),
                  pl.BlockSpec((D, tn), lambda i, j: (0, j))],
        out_specs=[pl.BlockSpec((TL, tn), lambda i, j: (i, j)), pl.BlockSpec((TL, D), lambda i, j: (i, 0))],
        out_shape=[SDS((L, PW), F32), SDS((L, D), BF16)],
        compiler_params=_cparams(("parallel", "arbitrary")),
    )(x, g, w)


def _outproj_call(x, ys, w):
    L = x.shape[0]

    def body(x_ref, a_ref, b_ref, c_ref, d_ref, w_ref, o_ref):
        acc = x_ref[...]
        for b, y_ref in enumerate((a_ref, b_ref, c_ref, d_ref)):
            acc = acc + _dot(y_ref[...].astype(BF16), w_ref[b * BR:(b + 1) * BR, :], NN)
        o_ref[...] = acc

    yspec = pl.BlockSpec((TL, BR), lambda i: (i, 0))
    return pl.pallas_call(
        body, name="outproj", grid=(L // TL,),
        in_specs=[pl.BlockSpec((TL, D), lambda i: (i, 0)), yspec, yspec, yspec, yspec,
                  pl.BlockSpec((D, D), lambda i: (0, 0))],
        out_specs=pl.BlockSpec((TL, D), lambda i: (i, 0)),
        out_shape=SDS((L, D), F32),
        compiler_params=_cparams(("parallel",)),
    )(x, *ys, w)


def _loss_call(x, g, target):
    L = x.shape[0]

    def body(x_ref, g_ref, t_ref, dx_ref, dg_ref, loss_ref):
        @pl.when(pl.program_id(0) == 0)
        def _():
            dg_ref[...] = jnp.zeros_like(dg_ref)
            loss_ref[...] = jnp.zeros_like(loss_ref)

        y, vjp = jax.vjp(_rms, x_ref[...], g_ref[...])
        err = y - t_ref[...]
        dx, dg = vjp(err * (1.0 / D))
        dx_ref[...] = dx
        dg_ref[...] += dg
        tot = jnp.sum(jnp.sum(err * err, axis=1, keepdims=True), axis=0, keepdims=True)
        loss_ref[...] += jnp.broadcast_to(tot * (0.5 / D), loss_ref.shape)

    return pl.pallas_call(
        body, name="loss_head", grid=(L // TL,),
        in_specs=[pl.BlockSpec((TL, D), lambda i: (i, 0)), pl.BlockSpec((1, D), lambda i: (0, 0)),
                  pl.BlockSpec((TL, D), lambda i: (i, 0))],
        out_specs=[pl.BlockSpec((TL, D), lambda i: (i, 0)), pl.BlockSpec((1, D), lambda i: (0, 0)),
                   pl.BlockSpec((1, 128), lambda i: (0, 0))],
        out_shape=[SDS((L, D), F32), SDS((1, D), F32), SDS((1, 128), F32)],
        compiler_params=_cparams(("arbitrary",)),
    )(x, g, target)


def _branch_a(valw, gatew, z, cw, cb, lg, lb, pw, pb, conv):
    a = conv(valw * _sigmoid(gatew), cw) + cb
    mu = jnp.mean(a, axis=-1, keepdims=True)
    xc = a - mu
    y = xc * lax.rsqrt(jnp.mean(xc * xc, axis=-1, keepdims=True) + EPS) * lg + lb
    y = mm(_silu(y), pw) + pb
    return y * _silu(z)


def _branch_c(bg, cw_, xw, z, w3, conv):
    return bg * conv(cw_ * xw, w3) * _silu(z)


def _ac_fwd_call(proj, p):
    L = proj.shape[0]

    def body(val, gate, za, hval, hgate, cb_, cc, cx, cz, hcc, hcx,
             acw, acb, alg, alb, apw, apb, ccw, ya_ref, yc_ref):
        nf = (pl.program_id(0) > 0).astype(F32)
        win = lambda h, m: jnp.concatenate([h[...] * nf, m[...]], axis=0)
        conv_a = functools.partial(_conv_fwd_impl, tile=TL, halo=HALO, taps=KA)
        conv_c = functools.partial(_conv_fwd_impl, tile=TL, halo=HALO, taps=KC)
        ya_ref[...] = _branch_a(win(hval, val), win(hgate, gate), za[...], acw[...], acb[...], alg[...], alb[...],
                                apw[...], apb[...], conv_a)
        yc_ref[...] = _branch_c(cb_[...], win(hcc, cc), win(hcx, cx), cz[...], ccw[...], conv_c)

    col = lambda j: pl.BlockSpec((TL, BR), lambda i: (i, j))
    hal = lambda j: _halo_spec(TL, HALO, BR, j)
    full = lambda a: pl.BlockSpec(a.shape, lambda i: (0,) * a.ndim)
    params = (p["a_conv_w"], p["a_conv_b"], p["a_ln_g"], p["a_ln_b"], p["a_pw_w"], p["a_pw_b"], p["c_conv_w"])
    return pl.pallas_call(
        body, name="ac_fwd", grid=(L // TL,),
        in_specs=[col(0), col(1), col(2), hal(0), hal(1), col(5), col(6), col(7), col(8), hal(6), hal(7)]
        + [full(a) for a in params],
        out_specs=[pl.BlockSpec((TL, BR), lambda i: (i, 0))] * 2,
        out_shape=[SDS((L, BR), F32)] * 2,
        compiler_params=_cparams(("parallel",)),
    )(*([proj] * 11), *params)


def _ac_bwd_call(proj, p, dya, dyc):
    L = proj.shape[0]
    nt = L // TL

    def body(val, gate, za, hval, hgate, cb_, cc, cx, cz, hcc, hcx,
             acw, acb, alg, alb, apw, apb, ccw, dya_ref, dyc_ref,
             da_ref, dc_ref, g_acw, g_acb, g_alg, g_alb, g_apw, g_apb, g_ccw, carry):
        i = pl.program_id(0)
        gouts = (g_acw, g_acb, g_alg, g_alb, g_apw, g_apb, g_ccw)

        @pl.when(i == 0)
        def _():
            carry[...] = jnp.zeros_like(carry)
            for r in gouts:
                r[...] = jnp.zeros_like(r)

        nf = (i < nt - 1).astype(F32)
        win = lambda h, m: jnp.concatenate([h[...] * nf, m[...]], axis=0)
        conv_a = _make_conv(TL, HALO, KA)
        conv_c = _make_conv(TL, HALO, KC)

        def f(valw, gatew, z, bg, ccw_, cxw, czv, w1, b1, lg, lb, pw, pb, w3):
            return (_branch_a(valw, gatew, z, w1, b1, lg, lb, pw, pb, conv_a),
                    _branch_c(bg, ccw_, cxw, czv, w3, conv_c))

        _, vjp = jax.vjp(f, win(hval, val), win(hgate, gate), za[...], cb_[...], win(hcc, cc), win(hcx, cx), cz[...],
                         acw[...], acb[...], alg[...], alb[...], apw[...].astype(F32), apb[...], ccw[...])
        (dvalw, dgatew, dz, dbg, dccw, dcxw, dczv, d1, d2, d3, d4, d5, d6, d7) = vjp((dya_ref[...], dyc_ref[...]))

        def settle(slot, dwin):
            tail = jnp.concatenate([jnp.zeros((TL - HALO, BR), F32), carry[slot]], axis=0)
            carry[slot] = dwin[:HALO, :]
            return (dwin[HALO:, :] + tail).astype(BF16)

        da_ref[:, 0:BR] = settle(0, dvalw)
        da_ref[:, BR:2 * BR] = settle(1, dgatew)
        da_ref[:, 2 * BR:3 * BR] = dz.astype(BF16)
        dc_ref[:, 0:BR] = dbg.astype(BF16)
        dc_ref[:, BR:2 * BR] = settle(2, dccw)
        dc_ref[:, 2 * BR:3 * BR] = settle(3, dcxw)
        dc_ref[:, 3 * BR:4 * BR] = dczv.astype(BF16)
        for r, g in zip(gouts, (d1, d2, d3, d4, d5, d6, d7)):
            r[...] += g

    col = lambda j: pl.BlockSpec((TL, BR), lambda i: (nt - 1 - i, j))
    hal = lambda j: _halo_spec_rev(nt, TL, HALO, BR, j)
    full = lambda a: pl.BlockSpec(a.shape, lambda i: (0,) * a.ndim)
    params = (p["a_conv_w"], p["a_conv_b"], p["a_ln_g"], p["a_ln_b"], p["a_pw_w"], p["a_pw_b"], p["c_conv_w"])
    rev = lambda w: pl.BlockSpec((TL, w), lambda i: (nt - 1 - i, 0))
    return pl.pallas_call(
        body, name="ac_bwd", grid=(nt,),
        in_specs=[col(0), col(1), col(2), hal(0), hal(1), col(5), col(6), col(7), col(8), hal(6), hal(7)]
        + [full(a) for a in params] + [rev(BR), rev(BR)],
        out_specs=[rev(3 * BR), rev(4 * BR)] + [full(a) for a in params],
        out_shape=[SDS((L, 3 * BR), BF16), SDS((L, 4 * BR), BF16)] + [SDS(a.shape, F32) for a in params],
        scratch_shapes=[pltpu.VMEM((4, HALO, BR), F32)],
        compiler_params=_cparams(("arbitrary",)),
    )(*([proj] * 11), *params, dya, dyc)


def _iota2(shape, dim):
    return lax.broadcasted_iota(jnp.int32, shape, dim)


def _s5_params(lam_re, lam_im, logdt, b_re, b_im, c_re, c_im):
    eg = (_iota2((16, NSTATE), 1) >> 6 == _iota2((16, NSTATE), 0)).astype(F32)
    dt = jnp.exp(mmh(jnp.broadcast_to(logdt, (8, 16)), eg)[0:1, :])
    lr = jnp.minimum(lam_re, -1e-4)
    li = lam_im
    mag = jnp.exp(lr * dt)
    lbr = mag * jnp.cos(li * dt)
    lbi = mag * jnp.sin(li * dt)
    den = lr * lr + li * li
    nr = lbr - 1.0
    fr = (nr * lr + lbi * li) / den
    fi = (lbi * lr - nr * li) / den
    row = _iota2((8, NSTATE), 0)
    f8 = jnp.where(row == 0, fr, jnp.where(row == 1, fi, 0.0))
    eye = (_iota2((NSTATE, NSTATE), 0) == _iota2((NSTATE, NSTATE), 1)).astype(F32)
    fcol = mmh_nt(eye, f8)
    frc, fic = fcol[:, 0:1], fcol[:, 1:2]
    bbr = frc * b_re - fic * b_im
    bbi = frc * b_im + fic * b_re
    e1 = ((_iota2((16, BR), 1) & 15) == _iota2((16, BR), 0)).astype(F32)
    m1 = ((_iota2((NSTATE, BR), 0) >> 6) == (_iota2((NSTATE, BR), 1) >> 4)).astype(F32)
    wbr = mmh(bbr, e1) * m1
    wbi = mmh(bbi, e1) * m1
    e2 = ((_iota2((64, NSTATE), 1) & 63) == _iota2((64, NSTATE), 0)).astype(F32)
    m2 = ((_iota2((BR, NSTATE), 0) >> 4) == (_iota2((BR, NSTATE), 1) >> 6)).astype(F32)
    wcr = mmh(c_re, e2) * m2
    wci = mmh(c_im, e2) * m2
    return lbr, lbi, wbr, wbi, wcr, wci


_S5_OUT = [(1, NSTATE), (1, NSTATE), (NSTATE, BR), (NSTATE, BR), (BR, NSTATE), (BR, NSTATE)]


def _s5_prep_call(sp):
    def body(lre, lim, ldt, bre, bim, cre, cim, o_lbr, o_lbi, o_wbr, o_wbi, o_wcr, o_wci, pwr, pwi, qwr, qwi):
        lbr, lbi, wbr, wbi, wcr, wci = _s5_params(lre[...], lim[...], ldt[...], bre[...], bim[...], cre[...], cim[...])
        o_lbr[...], o_lbi[...], o_wbr[...], o_wbi[...], o_wcr[...], o_wci[...] = lbr, lbi, wbr, wbi, wcr, wci
        pr, pi = lbr, lbi
        for i in range(SEG):
            pwr[i:i + 1, :] = pr
            pwi[i:i + 1, :] = pi
            qwr[SEG - 1 - i:SEG - i, :] = pr
            qwi[SEG - 1 - i:SEG - i, :] = -pi
            pr, pi = pr * lbr - pi * lbi, pr * lbi + pi * lbr

    args = (sp["lam_re"], sp["lam_im"], sp["log_dt"], sp["b_re"], sp["b_im"], sp["c_re"], sp["c_im"])
    return pl.pallas_call(
        body, name="s5_prep",
        out_shape=[SDS(s, F32) for s in _S5_OUT] + [SDS((SEG, NSTATE), F32)] * 4,
        compiler_params=pltpu.CompilerParams(vmem_limit_bytes=VMEM_LIMIT),
    )(*args)


def _s5_prep_bwd_call(sp, cots):
    def body(lre, lim, ldt, bre, bim, cre, cim, c0, c1, c2, c3, c4, c5, *outs):
        _, vjp = jax.vjp(_s5_params, lre[...], lim[...], ldt[...], bre[...], bim[...], cre[...], cim[...])
        grads = vjp((c0[...], c1[...], c2[...], c3[...], c4[...], c5[...]))
        for o, g in zip(outs, grads):
            o[...] = g

    args = (sp["lam_re"], sp["lam_im"], sp["log_dt"], sp["b_re"], sp["b_im"], sp["c_re"], sp["c_im"])
    return pl.pallas_call(
        body, name="s5_prep_bwd",
        out_shape=[SDS(a.shape, F32) for a in args],
        compiler_params=pltpu.CompilerParams(vmem_limit_bytes=VMEM_LIMIT),
    )(*args, *cots)


def _lanes(v, j):
    return v[:, j * 128:(j + 1) * 128]


def _s5_scan(sre, sim, lbr, lbi, pwr, pwi, cin_r, cin_i, reverse):
    lr = [jnp.broadcast_to(_lanes(lbr, j), (8, 128)) for j in range(8)]
    li = [jnp.broadcast_to(_lanes(lbi, j), (8, 128)) for j in range(8)]

    def step(t, st):
        i = SEG - 1 - t if reverse else t
        rows = pl.ds(i, 8, stride=SEG)
        new = []
        for j in range(8):
            sr, si = st[2 * j], st[2 * j + 1]
            vr, vi = sre.at[j], sim.at[j]
            nr = lr[j] * sr - li[j] * si + vr[rows, :]
            ni = lr[j] * si + li[j] * sr + vi[rows, :]
            vr[rows, :] = nr
            vi[rows, :] = ni
            new += [nr, ni]
        return tuple(new)

    ends = lax.fori_loop(0, SEG, step, tuple(jnp.zeros((8, 128), F32) for _ in range(16)))
    e_r = jnp.concatenate([ends[2 * j] for j in range(8)], axis=1)
    e_i = jnp.concatenate([ends[2 * j + 1] for j in range(8)], axis=1)
    l64r, l64i = (pwr[0:1, :], pwi[0:1, :]) if reverse else (pwr[SEG - 1:SEG, :], pwi[SEG - 1:SEG, :])
    order = range(7, -1, -1) if reverse else range(8)
    c_r, c_i = cin_r, cin_i
    carries = {}
    for k in order:
        carries[k] = (c_r, c_i)
        er, ei = e_r[k:k + 1, :], e_i[k:k + 1, :]
        c_r, c_i = er + l64r * c_r - l64i * c_i, ei + l64r * c_i + l64i * c_r
    for j in range(8):
        pr, pi = pwr[:, j * 128:(j + 1) * 128], pwi[:, j * 128:(j + 1) * 128]
        for k in range(8):
            cr, ci = _lanes(carries[k][0], j), _lanes(carries[k][1], j)
            blk = slice(k * SEG, (k + 1) * SEG)
            sre[j, blk, :] = sre[j, blk, :] + pr * cr - pi * ci
            sim[j, blk, :] = sim[j, blk, :] + pr * ci + pi * cr
    return c_r, c_i


def _bdot(a, b, dims):
    return _dot(a.astype(BF16), b.astype(BF16), dims)


def _s5_states(u, wbr, wbi, sre, sim):
    bur = _bdot(u, wbr, NT)
    bui = _bdot(u, wbi, NT)
    for j in range(8):
        sre[j] = _lanes(bur, j)
        sim[j] = _lanes(bui, j)


def _gather_lanes(s):
    return jnp.concatenate([s[j] for j in range(8)], axis=1)


def _s5_post(s_re, s_im, u, z, wcr, wci, dsk, gw, gb):
    y = mm_nt(s_re, wcr) - mm_nt(s_im, wci) + dsk * u
    yg = _gelu(y)
    return yg * _sigmoid(mm(yg, gw) + gb) * _silu(z)


def _s5_fwd_call(proj, prep, p):
    L = proj.shape[0]
    nt = L // TL
    lbr, lbi, wbr, wbi, wcr, wci, pwr, pwi, _, _ = prep

    def body(u_ref, z_ref, lbr_r, lbi_r, wbr_r, wbi_r, wcr_r, wci_r, pwr_r, pwi_r, d_r, gw_r, gb_r,
             yb_ref, cinr_ref, cini_ref, sre, sim, car, cai):
        @pl.when(pl.program_id(0) == 0)
        def _():
            car[...] = jnp.zeros_like(car)
            cai[...] = jnp.zeros_like(cai)

        u = u_ref[...]
        cinr_ref[0] = car[...]
        cini_ref[0] = cai[...]
        _s5_states(u, wbr_r[...], wbi_r[...], sre, sim)
        nr, ni = _s5_scan(sre, sim, lbr_r[...], lbi_r[...], pwr_r, pwi_r, car[...], cai[...], False)
        car[...] = nr
        cai[...] = ni
        yb_ref[...] = _s5_post(_gather_lanes(sre), _gather_lanes(sim), u, z_ref[...], wcr_r[...], wci_r[...],
                               d_r[...], gw_r[...], gb_r[...])

    full = lambda a: pl.BlockSpec(a.shape, lambda i: (0,) * a.ndim)
    consts = (lbr, lbi, wbr, wbi, wcr, wci, pwr, pwi, p["s5_d"], p["s5_glu_w"], p["s5_glu_b"])
    cspec = pl.BlockSpec((1, 1, NSTATE), lambda i: (i, 0, 0))
    return pl.pallas_call(
        body, name="s5_fwd", grid=(nt,),
        in_specs=[pl.BlockSpec((TL, BR), lambda i: (i, 3)), pl.BlockSpec((TL, BR), lambda i: (i, 4))]
        + [full(a) for a in consts],
        out_specs=[pl.BlockSpec((TL, BR), lambda i: (i, 0)), cspec, cspec],
        out_shape=[SDS((L, BR), F32), SDS((nt, 1, NSTATE), F32), SDS((nt, 1, NSTATE), F32)],
        scratch_shapes=[pltpu.VMEM((8, TL, 128), F32), pltpu.VMEM((8, TL, 128), F32),
                        pltpu.VMEM((1, NSTATE), F32), pltpu.VMEM((1, NSTATE), F32)],
        compiler_params=_cparams(("arbitrary",)),
    )(proj, proj, *consts)


def _s5_bwd_call(proj, prep, p, cin_r, cin_i, dyb):
    L = proj.shape[0]
    nt = L // TL
    lbr, lbi, wbr, wbi, wcr, wci, pwr, pwi, qwr, qwi = prep

    def body(u_ref, z_ref, lbr_r, lbi_r, wbr_r, wbi_r, wcr_r, wci_r, pwr_r, pwi_r, qwr_r, qwi_r, d_r, gw_r, gb_r,
             cinr_ref, cini_ref, dy_ref,
             db_ref, g_lbr, g_lbi, g_wbr, g_wbi, g_wcr, g_wci, g_d, g_gw, g_gb, sre, sim, gre, gim, car, cai):
        gouts = (g_lbr, g_lbi, g_wbr, g_wbi, g_wcr, g_wci, g_d, g_gw, g_gb)

        @pl.when(pl.program_id(0) == 0)
        def _():
            car[...] = jnp.zeros_like(car)
            cai[...] = jnp.zeros_like(cai)
            for r in gouts:
                r[...] = jnp.zeros_like(r)

        u = u_ref[...]
        lr, li = lbr_r[...], lbi_r[...]
        c0r, c0i = cinr_ref[0], cini_ref[0]
        _s5_states(u, wbr_r[...], wbi_r[...], sre, sim)
        _s5_scan(sre, sim, lr, li, pwr_r, pwi_r, c0r, c0i, False)
        s_re, s_im = _gather_lanes(sre), _gather_lanes(sim)
        _, vjp = jax.vjp(_s5_post, s_re, s_im, u, z_ref[...], wcr_r[...], wci_r[...], d_r[...],
                         gw_r[...].astype(F32), gb_r[...])
        ds_re, ds_im, du, dz, dwcr, dwci, dd, dgw, dgb = vjp(dy_ref[...])
        for j in range(8):
            gre[j] = _lanes(ds_re, j)
            gim[j] = _lanes(ds_im, j)
        nr, ni = _s5_scan(gre, gim, lr, -li, qwr_r, qwi_r, car[...], cai[...], True)
        car[...] = nr
        cai[...] = ni
        a_re, a_im = _gather_lanes(gre), _gather_lanes(gim)
        first = _iota2((TL, NSTATE), 0) == 0
        p_re = jnp.where(first, c0r, jnp.roll(s_re, 1, axis=0))
        p_im = jnp.where(first, c0i, jnp.roll(s_im, 1, axis=0))
        g_lbr[...] += jnp.sum(a_re * p_re + a_im * p_im, axis=0, keepdims=True)
        g_lbi[...] += jnp.sum(a_im * p_re - a_re * p_im, axis=0, keepdims=True)
        du = du + _bdot(a_re, wbr_r[...], NN) + _bdot(a_im, wbi_r[...], NN)
        g_wbr[...] += _bdot(a_re, u, TN)
        g_wbi[...] += _bdot(a_im, u, TN)
        g_wcr[...] += dwcr
        g_wci[...] += dwci
        g_d[...] += dd
        g_gw[...] += dgw
        g_gb[...] += dgb
        db_ref[:, 0:BR] = du.astype(BF16)
        db_ref[:, BR:2 * BR] = dz.astype(BF16)

    full = lambda a: pl.BlockSpec(a.shape, lambda i: (0,) * a.ndim)
    consts = (lbr, lbi, wbr, wbi, wcr, wci, pwr, pwi, qwr, qwi, p["s5_d"], p["s5_glu_w"], p["s5_glu_b"])
    cspec = pl.BlockSpec((1, 1, NSTATE), lambda i: (nt - 1 - i, 0, 0))
    gshapes = _S5_OUT + [(1, BR), (BR, BR), (1, BR)]
    return pl.pallas_call(
        body, name="s5_bwd", grid=(nt,),
        in_specs=[pl.BlockSpec((TL, BR), lambda i: (nt - 1 - i, 3)), pl.BlockSpec((TL, BR), lambda i: (nt - 1 - i, 4))]
        + [full(a) for a in consts] + [cspec, cspec, pl.BlockSpec((TL, BR), lambda i: (nt - 1 - i, 0))],
        out_specs=[pl.BlockSpec((TL, 2 * BR), lambda i: (nt - 1 - i, 0))]
        + [pl.BlockSpec(s, lambda i: (0, 0)) for s in gshapes],
        out_shape=[SDS((L, 2 * BR), BF16)] + [SDS(s, F32) for s in gshapes],
        scratch_shapes=[pltpu.VMEM((8, TL, 128), F32)] * 4 + [pltpu.VMEM((1, NSTATE), F32)] * 2,
        compiler_params=_cparams(("arbitrary",)),
    )(proj, proj, *consts, cin_r, cin_i, dyb)


def _heads(x):
    return [x[:, h * HD:(h + 1) * HD] for h in range(NH)]


def _l2n(x, scale):
    return jnp.concatenate([xh * (lax.rsqrt(jnp.sum(xh * xh, axis=-1, keepdims=True) + EPS) * scale)
                            for xh in _heads(x)], axis=1)


def _dn_pre(qkvw, ab, cw, alog, dtb, conv, rows):
    c = _silu(conv(qkvw, cw))
    q = _l2n(c[:, 0:BR], HD ** -0.5)
    k = _l2n(c[:, BR:2 * BR], 1.0)
    v = c[:, 2 * BR:3 * BR]
    g = -jnp.exp(alog) * _softplus(ab + dtb)
    ri, ci = _iota2((rows, rows), 0), _iota2((rows, rows), 1)
    tri = ((ri >= ci) & ((ri >> 6) == (ci >> 6))).astype(F32)
    gc = mmh(tri, g)
    lane = _iota2(ab.shape, 1)
    return q, k, v, jnp.where(lane < NH, gc, jnp.where(lane < 2 * NH, _sigmoid(ab), 0.0))


def _dn_pre_fwd_call(proj, p):
    L = proj.shape[0]

    def body(m_ref, h_ref, ab_ref, cw, alog, dtb, q_ref, k_ref, v_ref, gb_ref):
        nf = (pl.program_id(0) > 0).astype(F32)
        qkvw = jnp.concatenate([h_ref[...] * nf, m_ref[...]], axis=0)
        conv = functools.partial(_conv_fwd_impl, tile=TL, halo=HALO_S, taps=KD)
        q_ref[...], k_ref[...], v_ref[...], gb_ref[...] = _dn_pre(qkvw, ab_ref[...], cw[...], alog[...], dtb[...], conv, TL)

    full = lambda a: pl.BlockSpec(a.shape, lambda i: (0,) * a.ndim)
    params = (p["d_conv_w"], p["d_a_log"], p["d_dt_bias"])
    o = pl.BlockSpec((TL, BR), lambda i: (i, 0))
    return pl.pallas_call(
        body, name="dn_pre_fwd", grid=(L // TL,),
        in_specs=[pl.BlockSpec((TL, 3 * BR), lambda i: (i, 3)), _halo_spec(TL, HALO_S, 3 * BR, 3),
                  pl.BlockSpec((TL, 128), lambda i: (i, AB_COL // 128))] + [full(a) for a in params],
        out_specs=[o, o, o, pl.BlockSpec((TL, 128), lambda i: (i, 0))],
        out_shape=[SDS((L, BR), F32)] * 3 + [SDS((L, 128), F32)],
        compiler_params=_cparams(("parallel",)),
    )(proj, proj, proj, *params)


def _dn_pre_bwd_call(proj, p, dq, dk, dv, dgb):
    L = proj.shape[0]
    nt = L // TL

    def body(m_ref, h_ref, ab_ref, cw, alog, dtb, dq_r, dk_r, dv_r, dgb_r,
             dqkv_ref, dab_ref, g_cw, g_alog, g_dtb, carry):
        i = pl.program_id(0)

        @pl.when(i == 0)
        def _():
            carry[...] = jnp.zeros_like(carry)
            for r in (g_cw, g_alog, g_dtb):
                r[...] = jnp.zeros_like(r)

        nf = (i < nt - 1).astype(F32)
        qkvw = jnp.concatenate([h_ref[...] * nf, m_ref[...]], axis=0)
        conv = _make_conv(TL, HALO_S, KD)
        _, vjp = jax.vjp(lambda a, b, c, d, e: _dn_pre(a, b, c, d, e, conv, TL),
                         qkvw, ab_ref[...], cw[...], alog[...], dtb[...])
        dwin, dab, dcw, dalog, ddtb = vjp((dq_r[...], dk_r[...], dv_r[...], dgb_r[...]))
        tail = jnp.concatenate([jnp.zeros((TL - HALO_S, 3 * BR), F32), carry[...]], axis=0)
        carry[...] = dwin[:HALO_S, :]
        dqkv_ref[...] = (dwin[HALO_S:, :] + tail).astype(BF16)
        dab_ref[...] = dab.astype(BF16)
        g_cw[...] += dcw
        g_alog[...] += dalog
        g_dtb[...] += ddtb

    full = lambda a: pl.BlockSpec(a.shape, lambda i: (0,) * a.ndim)
    params = (p["d_conv_w"], p["d_a_log"], p["d_dt_bias"])
    rev = lambda w: pl.BlockSpec((TL, w), lambda i: (nt - 1 - i, 0))
    return pl.pallas_call(
        body, name="dn_pre_bwd", grid=(nt,),
        in_specs=[pl.BlockSpec((TL, 3 * BR), lambda i: (nt - 1 - i, 3)), _halo_spec_rev(nt, TL, HALO_S, 3 * BR, 3),
                  pl.BlockSpec((TL, 128), lambda i: (nt - 1 - i, AB_COL // 128))] + [full(a) for a in params]
        + [rev(BR), rev(BR), rev(BR), rev(128)],
        out_specs=[rev(3 * BR), rev(128)] + [full(a) for a in params],
        out_shape=[SDS((L, 3 * BR), BF16), SDS((L, 128), BF16)] + [SDS(a.shape, F32) for a in params],
        scratch_shapes=[pltpu.VMEM((HALO_S, 3 * BR), F32)],
        compiler_params=_cparams(("arbitrary",)),
    )(proj, proj, proj, *params, dq, dk, dv, dgb)


def _tri_inverse(lm, eye):
    t = eye - lm
    pw = lm
    for _ in range(5):
        pw = mm(pw, pw)
        t = mm(t, eye + pw)
    return t


def _dn_group(q, k, v, gb, z, ng, *s):
    ri, ci = _iota2((CH, CH), 0), _iota2((CH, CH), 1)
    causal, strict = ri >= ci, ri > ci
    eye = (ri == ci).astype(F32)
    s = list(s)
    out_rows = []
    for c in range(DN_GROUP):
        rows = slice(c * CH, (c + 1) * CH)
        gbc = gb[rows, :]
        outs = []
        for h, (qh, kh, vh, zh) in enumerate(zip(_heads(q[rows, :]), _heads(k[rows, :]), _heads(v[rows, :]),
                                                 _heads(z[rows, :]))):
            gc = jnp.broadcast_to(gbc[:, h:h + 1], (CH, HD))
            beta = gbc[:, NH + h:NH + h + 1]
            diff = gc - gc.T
            decay = jnp.where(causal, jnp.exp(jnp.where(causal, diff, 0.0)), 0.0)
            egc = jnp.exp(gc)
            glast = gc[CH - 1:CH, :]
            kb = kh * beta
            lmat = jnp.where(strict, mm_nt(kb, kh) * decay, 0.0)
            uw = mm(_tri_inverse(lmat, eye), jnp.concatenate([vh * beta, kb * egc], axis=1))
            u, w = uw[:, :HD], uw[:, HD:]
            attn = mm_nt(qh, kh) * decay
            ws = mm(jnp.concatenate([w, qh * egc], axis=0), s[h])
            v_new = u - ws[:CH, :]
            o = ws[CH:, :] + mm(attn, v_new)
            s[h] = s[h] * jnp.exp(glast[:, 0:1]) + mm_tn(kh * jnp.exp(glast - gc), v_new)
            o = o * lax.rsqrt(jnp.mean(o * o, axis=-1, keepdims=True) + EPS) * ng
            outs.append(o * _silu(zh))
        out_rows.append(jnp.concatenate(outs, axis=1))
    return (jnp.concatenate(out_rows, axis=0), *s)


def _dn_core_fwd_call(proj, q, k, v, gb, ng):
    L = q.shape[0]
    rows = DN_GROUP * CH
    ng_ = L // rows

    def body(q_r, k_r, v_r, gb_r, z_r, ng_r, yd_ref, ssave_ref, s_scr):
        @pl.when(pl.program_id(0) == 0)
        def _():
            s_scr[...] = jnp.zeros_like(s_scr)

        ssave_ref[0] = s_scr[...]
        yd, *s2 = _dn_group(q_r[...], k_r[...], v_r[...], gb_r[...], z_r[...], ng_r[...], *[s_scr[h] for h in range(NH)])
        yd_ref[...] = yd
        for h in range(NH):
            s_scr[h] = s2[h]

    c = pl.BlockSpec((rows, BR), lambda i: (i, 0))
    return pl.pallas_call(
        body, name="dn_core_fwd", grid=(ng_,),
        in_specs=[c, c, c, pl.BlockSpec((rows, 128), lambda i: (i, 0)), pl.BlockSpec((rows, BR), lambda i: (i, 12)),
                  pl.BlockSpec((1, HD), lambda i: (0, 0))],
        out_specs=[c, pl.BlockSpec((1, NH, HD, HD), lambda i: (i, 0, 0, 0))],
        out_shape=[SDS((L, BR), F32), SDS((ng_, NH, HD, HD), F32)],
        scratch_shapes=[pltpu.VMEM((NH, HD, HD), F32)],
        compiler_params=_cparams(("arbitrary",)),
    )(q, k, v, gb, proj, ng)


def _dn_core_bwd_call(proj, q, k, v, gb, ng, ssave, dyd):
    L = q.shape[0]
    rows = DN_GROUP * CH
    ng_ = L // rows

    def body(q_r, k_r, v_r, gb_r, z_r, ng_r, s_r, dy_r, dq_ref, dk_ref, dv_ref, dgb_ref, dz_ref, g_ng, ds_scr):
        @pl.when(pl.program_id(0) == 0)
        def _():
            ds_scr[...] = jnp.zeros_like(ds_scr)
            g_ng[...] = jnp.zeros_like(g_ng)

        _, vjp = jax.vjp(_dn_group, q_r[...], k_r[...], v_r[...], gb_r[...], z_r[...], ng_r[...],
                         *[s_r[0, h] for h in range(NH)])
        dq, dk, dv, dgb, dz, dng, *ds = vjp((dy_r[...], *[ds_scr[h] for h in range(NH)]))
        dq_ref[...], dk_ref[...], dv_ref[...], dgb_ref[...] = dq, dk, dv, dgb
        dz_ref[...] = dz.astype(BF16)
        g_ng[...] += dng
        for h in range(NH):
            ds_scr[h] = ds[h]

    c = pl.BlockSpec((rows, BR), lambda i: (ng_ - 1 - i, 0))
    c128 = pl.BlockSpec((rows, 128), lambda i: (ng_ - 1 - i, 0))
    return pl.pallas_call(
        body, name="dn_core_bwd", grid=(ng_,),
        in_specs=[c, c, c, c128, pl.BlockSpec((rows, BR), lambda i: (ng_ - 1 - i, 12)),
                  pl.BlockSpec((1, HD), lambda i: (0, 0)),
                  pl.BlockSpec((1, NH, HD, HD), lambda i: (ng_ - 1 - i, 0, 0, 0)), c],
        out_specs=[c, c, c, c128, c, pl.BlockSpec((1, HD), lambda i: (0, 0))],
        out_shape=[SDS((L, BR), F32)] * 3 + [SDS((L, 128), F32), SDS((L, BR), BF16), SDS((1, HD), F32)],
        scratch_shapes=[pltpu.VMEM((NH, HD, HD), F32)],
        compiler_params=_cparams(("arbitrary",)),
    )(q, k, v, gb, proj, ng, ssave, dyd)


def _outproj_bwd_call(dx, ys, w):
    L = dx.shape[0]

    def body(dx_ref, a_ref, b_ref, c_ref, d_ref, w_ref, da, db, dc, dd, dw_ref):
        @pl.when(pl.program_id(0) == 0)
        def _():
            dw_ref[...] = jnp.zeros_like(dw_ref)

        dxb = dx_ref[...].astype(BF16)
        for b, (y_ref, o_ref) in enumerate(zip((a_ref, b_ref, c_ref, d_ref), (da, db, dc, dd))):
            o_ref[...] = _dot(dxb, w_ref[b * BR:(b + 1) * BR, :], NT)
            dw_ref[b * BR:(b + 1) * BR, :] += _dot(y_ref[...].astype(BF16), dxb, TN)

    yspec = pl.BlockSpec((TL, BR), lambda i: (i, 0))
    return pl.pallas_call(
        body, name="outproj_bwd", grid=(L // TL,),
        in_specs=[pl.BlockSpec((TL, D), lambda i: (i, 0)), yspec, yspec, yspec, yspec,
                  pl.BlockSpec((D, D), lambda i: (0, 0))],
        out_specs=[yspec] * 4 + [pl.BlockSpec((D, D), lambda i: (0, 0))],
        out_shape=[SDS((L, BR), F32)] * 4 + [SDS((D, D), F32)],
        compiler_params=_cparams(("arbitrary",)),
    )(dx, *ys, w)


def _slab_cols(slabs):
    widths = [s.shape[1] for s in slabs]
    starts = [sum(widths[:i]) for i in range(len(widths))]
    assert starts[-1] + widths[-1] == PW
    return list(zip(starts, widths))


def _inproj_bwd_x_call(slabs, w, x, g, dx_next):
    L = x.shape[0]
    cols = _slab_cols(slabs)
    n = len(slabs)

    def body(*refs):
        dp_refs, (w_ref, x_ref, g_ref, dxn_ref, dx_ref, dg_ref) = refs[:n], refs[n:]

        @pl.when(pl.program_id(0) == 0)
        def _():
            dg_ref[...] = jnp.zeros_like(dg_ref)

        dh = None
        for dp_ref, (c0, cw) in zip(dp_refs, cols):
            part = _dot(dp_ref[...], w_ref[:, c0:c0 + cw], NT)
            dh = part if dh is None else dh + part
        _, vjp = jax.vjp(_rms, x_ref[...], g_ref[...])
        dx, dg = vjp(dh)
        dx_ref[...] = dx + dxn_ref[...]
        dg_ref[...] += dg

    row = lambda w_: pl.BlockSpec((TL, w_), lambda i: (i, 0))
    return pl.pallas_call(
        body, name="inproj_bwd_x", grid=(L // TL,),
        in_specs=[row(cw) for _, cw in cols]
        + [pl.BlockSpec((D, PW), lambda i: (0, 0)), row(D), pl.BlockSpec((1, D), lambda i: (0, 0)), row(D)],
        out_specs=[row(D), pl.BlockSpec((1, D), lambda i: (0, 0))],
        out_shape=[SDS((L, D), F32), SDS((1, D), F32)],
        compiler_params=_cparams(("arbitrary",)),
    )(*slabs, w, x, g, dx_next)


def _inproj_bwd_w_call(h, slabs):
    L = h.shape[0]
    cols = _slab_cols(slabs)
    n = len(slabs)

    def body(*refs):
        h_ref, dp_refs, dw_ref = refs[0], refs[1:1 + n], refs[1 + n]

        @pl.when(pl.program_id(0) == 0)
        def _():
            dw_ref[...] = jnp.zeros_like(dw_ref)

        hv = h_ref[...]
        for dp_ref, (c0, cw) in zip(dp_refs, cols):
            dw_ref[:, c0:c0 + cw] += _dot(hv, dp_ref[...], TN)

    row = lambda w_: pl.BlockSpec((TL, w_), lambda i: (i, 0))
    return pl.pallas_call(
        body, name="inproj_bwd_w", grid=(L // TL,),
        in_specs=[row(D)] + [row(cw) for _, cw in cols],
        out_specs=pl.BlockSpec((D, PW), lambda i: (0, 0)),
        out_shape=SDS((D, PW), F32),
        compiler_params=_cparams(("arbitrary",)),
    )(h, *slabs)


def _exchange_call(src, per_peer, name):
    blk = src.shape[1:] if per_peer else src.shape

    def body(src_ref, dst_ref, send_sems, recv_sems, local_sem):
        x, y, c = lax.axis_index("x"), lax.axis_index("y"), lax.axis_index("c")
        me = 4 * x + 2 * y + c
        copies = []
        for mask in range(1, N_DEV):
            px = 1 - x if mask & 4 else x
            py = 1 - y if mask & 2 else y
            pc = 1 - c if mask & 1 else c
            s = src_ref.at[4 * px + 2 * py + pc] if per_peer else src_ref
            cp = pltpu.make_async_remote_copy(
                src_ref=s, dst_ref=dst_ref.at[me], send_sem=send_sems.at[mask - 1], recv_sem=recv_sems.at[mask - 1],
                device_id=(px, py, pc), device_id_type=pl.DeviceIdType.MESH)
            cp.start()
            copies.append(cp)
        mine = pltpu.make_async_copy(src_ref.at[me] if per_peer else src_ref, dst_ref.at[me], local_sem)
        mine.start()
        for cp in copies:
            cp.wait()
        mine.wait()

    return pl.pallas_call(
        body, name=name,
        in_specs=[pl.BlockSpec(memory_space=pl.ANY)],
        out_specs=pl.BlockSpec(memory_space=pl.ANY),
        out_shape=SDS((N_DEV,) + tuple(blk), src.dtype),
        scratch_shapes=[pltpu.SemaphoreType.DMA((N_DEV - 1,)), pltpu.SemaphoreType.DMA((N_DEV - 1,)),
                        pltpu.SemaphoreType.DMA],
    )(src)


def _reduce_adamw_call(parts, w, m, v, name):
    R = w.shape[0]
    tr = 1000 if R % 1000 == 0 else R
    c1 = 1.0 - ADAM_B1 ** ADAM_STEP
    c2 = 1.0 - ADAM_B2 ** ADAM_STEP

    def body(p_ref, w_ref, m_ref, v_ref, g_ref, d_ref, nm_ref, nv_ref):
        g = p_ref[0]
        for k in range(1, N_DEV):
            g = g + p_ref[k]
        nm = ADAM_B1 * m_ref[...] + (1.0 - ADAM_B1) * g
        nv = ADAM_B2 * v_ref[...] + (1.0 - ADAM_B2) * (g * g)
        g_ref[...] = g
        nm_ref[...] = nm
        nv_ref[...] = nv
        d_ref[...] = -ADAM_LR * ((nm / c1) / (jnp.sqrt(nv / c2) + ADAM_EPS) + ADAM_WD * w_ref[...])

    row = pl.BlockSpec((tr, 128), lambda i: (i, 0))
    return pl.pallas_call(
        body, name=name, grid=(R // tr,),
        in_specs=[pl.BlockSpec((N_DEV, tr, 128), lambda i: (0, i, 0)), row, row, row],
        out_specs=[row] * 4,
        out_shape=[SDS((R, 128), F32)] * 4,
        compiler_params=_cparams(("parallel",)),
    )(parts, w, m, v)


_SHARDED = ("w_in", "w_out", "a_pw_w", "s5_glu_w", "a_conv_w", "c_conv_w", "d_conv_w")
_SHARD_SHAPES = {"w_in": (DEPTH, D, IN_COLS // N_DEV), "w_out": (DEPTH, D // N_DEV, D),
                 "a_pw_w": (DEPTH, BR // N_DEV, BR), "s5_glu_w": (DEPTH, BR // N_DEV, BR),
                 "a_conv_w": (DEPTH, KA, BR // N_DEV), "c_conv_w": (DEPTH, KC, BR // N_DEV),
                 "d_conv_w": (DEPTH, KD, 3 * BR // N_DEV)}
_REPLICATED = ("norm_g", "a_conv_b", "a_ln_g", "a_ln_b", "a_pw_b", "s5_lambda_re", "s5_lambda_im", "s5_b_re", "s5_b_im",
               "s5_c_re", "s5_c_im", "s5_d", "s5_log_dt", "s5_glu_b", "d_a_log", "d_dt_bias", "d_norm_g", "final_g")
_WEIGHTS = ("norm_g", "w_in", "a_conv_w", "a_conv_b", "a_ln_g", "a_ln_b", "a_pw_w", "a_pw_b", "s5_lambda_re",
            "s5_lambda_im", "s5_b_re", "s5_b_im", "s5_c_re", "s5_c_im", "s5_d", "s5_log_dt", "s5_glu_w", "s5_glu_b",
            "c_conv_w", "d_conv_w", "d_a_log", "d_dt_bias", "d_norm_g", "w_out", "final_g")


def _size(shape):
    n = 1
    for s in shape:
        n *= s
    return n


def _pack_rows(pieces, row_mult):
    flat = jnp.concatenate([p.reshape(-1) for p in pieces])
    per = 128 * row_mult
    pad = (-flat.shape[0]) % per
    return jnp.pad(flat, (0, pad)).reshape(-1, 128)


def _unpack(flat, shapes):
    out, off = [], 0
    for s in shapes:
        n = _size(s)
        out.append(flat[off:off + n].reshape(s))
        off += n
    return out


def _gather_weights(shards):
    pieces = [shards[n].astype(BF16) for n in _SHARDED[:4]]
    pieces += [lax.bitcast_convert_type(shards[n], BF16) for n in _SHARDED[4:]]
    got = _exchange_call(_pack_rows(pieces, 16), False, "gather_weights").reshape(N_DEV, -1)
    shapes = [_SHARD_SHAPES[n] for n in _SHARDED[:4]] + [_SHARD_SHAPES[n] + (2,) for n in _SHARDED[4:]]
    per_dev = [_unpack(got[k], shapes) for k in range(N_DEV)]
    st = {n: jnp.stack([per_dev[k][i] for k in range(N_DEV)]) for i, n in enumerate(_SHARDED)}
    for n in _SHARDED[4:]:
        st[n] = lax.bitcast_convert_type(st[n], F32)
    cols = lambda a: jnp.moveaxis(a, 0, -2).reshape(a.shape[1:-1] + (N_DEV * a.shape[-1],))
    rows = lambda a: jnp.moveaxis(a, 0, 1).reshape((a.shape[1], N_DEV * a.shape[2]) + a.shape[3:])
    w_in = cols(st["w_in"])
    w_in = jnp.concatenate([w_in[..., :3072], w_in[..., 3080:], w_in[..., 3072:3080],
                            jnp.zeros((DEPTH, D, PW - IN_COLS), BF16)], axis=-1)
    padk = lambda a, k: jnp.pad(a, ((0, 0), (0, k - a.shape[1]), (0, 0)))
    return {"w_in": w_in, "w_out": rows(st["w_out"]), "a_pw_w": rows(st["a_pw_w"]), "s5_glu_w": rows(st["s5_glu_w"]),
            "a_conv_w": padk(cols(st["a_conv_w"]), HALO), "c_conv_w": padk(cols(st["c_conv_w"]), HALO_S),
            "d_conv_w": padk(cols(st["d_conv_w"]), HALO_S)}


def _scatter_layout(full):
    w_in = full["w_in"]
    w_in = jnp.concatenate([w_in[..., :3072], w_in[..., 3328:3336], w_in[..., 3072:3328]], axis=-1)
    cols = lambda a: jnp.moveaxis(a.reshape(a.shape[:-1] + (N_DEV, a.shape[-1] // N_DEV)), -2, 0)
    rows = lambda a: jnp.moveaxis(a.reshape((a.shape[0], N_DEV, a.shape[1] // N_DEV) + a.shape[2:]), 1, 0)
    by_dev = [cols(w_in), rows(full["w_out"]), rows(full["a_pw_w"]), rows(full["s5_glu_w"]),
              cols(full["a_conv_w"][:, :KA]), cols(full["c_conv_w"][:, :KC]), cols(full["d_conv_w"][:, :KD])]
    flat = jnp.concatenate([a.reshape(N_DEV, -1) for a in by_dev], axis=1)
    pad = (-flat.shape[1]) % (128 * 1000)
    return jnp.pad(flat, ((0, 0), (0, pad))).reshape(N_DEV, -1, 128)


def _s5_inputs(p):
    return {"lam_re": p["s5_lambda_re"].reshape(1, NSTATE), "lam_im": p["s5_lambda_im"].reshape(1, NSTATE),
            "log_dt": p["s5_log_dt"].reshape(1, 16), "b_re": p["s5_b_re"].reshape(NSTATE, 16),
            "b_im": p["s5_b_im"].reshape(NSTATE, 16), "c_re": p["s5_c_re"].reshape(BR, 64),
            "c_im": p["s5_c_im"].reshape(BR, 64)}


def _row(a, width=None):
    a = a.reshape(1, -1)
    return a if width is None else jnp.pad(a, ((0, 0), (0, width - a.shape[1])))


def _layer_params(p):
    q = dict(p)
    for n in ("norm_g", "a_conv_b", "a_ln_g", "a_ln_b", "a_pw_b", "s5_d", "s5_glu_b", "d_norm_g"):
        q[n] = _row(p[n])
    q["d_a_log"] = _row(p["d_a_log"], 128)
    q["d_dt_bias"] = _row(p["d_dt_bias"], 128)
    return q


def _layer_fwd(x, p):
    q = _layer_params(p)
    proj, h = _inproj_call(x, q["norm_g"], q["w_in"])
    ya, yc = _ac_fwd_call(proj, q)
    prep = _s5_prep_call(_s5_inputs(p))
    yb, cin_r, cin_i = _s5_fwd_call(proj, prep, q)
    dq, dk, dv, dgb = _dn_pre_fwd_call(proj, q)
    yd, ssave = _dn_core_fwd_call(proj, dq, dk, dv, dgb, q["d_norm_g"])
    x_next = _outproj_call(x, (ya, yb, yc, yd), q["w_out"])
    saved = dict(x=x, proj=proj, h=h, ya=ya, yb=yb, yc=yc, yd=yd, cin_r=cin_r, cin_i=cin_i,
                 q=dq, k=dk, v=dv, gb=dgb, ssave=ssave)
    return x_next, saved


def _layer_bwd(dx, p, sv):
    q = _layer_params(p)
    proj = sv["proj"]
    dya, dyb, dyc, dyd, g_wout = _outproj_bwd_call(dx, (sv["ya"], sv["yb"], sv["yc"], sv["yd"]), q["w_out"])
    dpa, dpc, g_acw, g_acb, g_alg, g_alb, g_apw, g_apb, g_ccw = _ac_bwd_call(proj, q, dya, dyc)
    sp = _s5_inputs(p)
    prep = _s5_prep_call(sp)
    dpb, *s5g = _s5_bwd_call(proj, prep, q, sv["cin_r"], sv["cin_i"], dyb)
    g_sd, g_gw, g_gb = s5g[6:]
    g_lre, g_lim, g_ldt, g_bre, g_bim, g_cre, g_cim = _s5_prep_bwd_call(sp, s5g[:6])
    dq, dk, dv, dgb, dz, g_ng = _dn_core_bwd_call(proj, sv["q"], sv["k"], sv["v"], sv["gb"], q["d_norm_g"], sv["ssave"], dyd)
    dqkv, dab, g_dcw, g_alog, g_dtb = _dn_pre_bwd_call(proj, q, dq, dk, dv, dgb)
    slabs = (dpa, dpb, dpc, dqkv, dz, dab)
    g_win = _inproj_bwd_w_call(sv["h"], slabs)
    dx_prev, g_ng0 = _inproj_bwd_x_call(slabs, q["w_in"], sv["x"], q["norm_g"], dx)
    grads = {"norm_g": g_ng0.reshape(D), "w_in": g_win, "a_conv_w": g_acw, "a_conv_b": g_acb.reshape(BR),
             "a_ln_g": g_alg.reshape(BR), "a_ln_b": g_alb.reshape(BR), "a_pw_w": g_apw, "a_pw_b": g_apb.reshape(BR),
             "s5_lambda_re": g_lre.reshape(16, 64), "s5_lambda_im": g_lim.reshape(16, 64),
             "s5_b_re": g_bre.reshape(16, 64, 16), "s5_b_im": g_bim.reshape(16, 64, 16),
             "s5_c_re": g_cre.reshape(16, 16, 64), "s5_c_im": g_cim.reshape(16, 16, 64),
             "s5_d": g_sd.reshape(BR), "s5_log_dt": g_ldt.reshape(16), "s5_glu_w": g_gw, "s5_glu_b": g_gb.reshape(BR),
             "c_conv_w": g_ccw, "d_conv_w": g_dcw, "d_a_log": g_alog[0, :NH], "d_dt_bias": g_dtb[0, :NH],
             "d_norm_g": g_ng.reshape(HD), "w_out": g_wout}
    return dx_prev, grads


def _step(x, target, weights, moments_m, moments_v):
    full = _gather_weights({n: weights[n] for n in _SHARDED})
    layer_names = [n for n in _WEIGHTS if n != "final_g"]
    stacked = {n: (full[n] if n in full else weights[n]) for n in layer_names}

    layers = [{n: a[l] for n, a in stacked.items()} for l in range(DEPTH)]
    saved, x_out = [], x
    for p in layers:
        x_out, sv = _layer_fwd(x_out, p)
        saved.append(sv)
    dx0, g_final, loss_part = _loss_call(x_out, _row(weights["final_g"]), target)
    per_layer = [None] * DEPTH
    for l in range(DEPTH - 1, -1, -1):
        dx0, per_layer[l] = _layer_bwd(dx0, layers[l], saved[l])
    grads = {n: jnp.stack([g[n] for g in per_layer]) for n in layer_names}
    grads["final_g"] = g_final.reshape(D)
    loss = lax.psum(loss_part[0, 0], ("x", "y", "c"))

    parts = _exchange_call(_scatter_layout({n: grads[n] for n in _SHARDED}), True, "scatter_grads")
    own = lambda d: _pack_rows([d[n] for n in _SHARDED], 1000)
    res_s = _reduce_adamw_call(parts, own(weights), own(moments_m), own(moments_v), "adamw_sharded")
    res_s = [dict(zip(_SHARDED, _unpack(r.reshape(-1), [_SHARD_SHAPES[n] for n in _SHARDED]))) for r in res_s]

    rep_shapes = [weights[n].shape for n in _REPLICATED]
    pack = lambda d: _pack_rows([d[n] for n in _REPLICATED], 8)
    gathered = _exchange_call(pack(grads), False, "gather_small_grads")
    res_r = _reduce_adamw_call(gathered, pack(weights), pack(moments_m), pack(moments_v), "adamw_replicated")
    res_r = [dict(zip(_REPLICATED, _unpack(r.reshape(-1), rep_shapes))) for r in res_r]

    outs = [loss, dx0]
    for kind in range(4):
        outs += [res_s[kind][n] if n in _SHARDED else res_r[kind][n] for n in _WEIGHTS]
    return tuple(outs)


def kernel(x, norm_g, w_in, a_conv_w, a_conv_b, a_ln_g, a_ln_b, a_pw_w, a_pw_b, s5_lambda_re, s5_lambda_im, s5_b_re, s5_b_im, s5_c_re, s5_c_im, s5_d, s5_log_dt, s5_glu_w, s5_glu_b, c_conv_w, d_conv_w, d_a_log, d_dt_bias, d_norm_g, w_out, final_g, loss_target, m_norm_g, m_w_in, m_a_conv_w, m_a_conv_b, m_a_ln_g, m_a_ln_b, m_a_pw_w, m_a_pw_b, m_s5_lambda_re, m_s5_lambda_im, m_s5_b_re, m_s5_b_im, m_s5_c_re, m_s5_c_im, m_s5_d, m_s5_log_dt, m_s5_glu_w, m_s5_glu_b, m_c_conv_w, m_d_conv_w, m_d_a_log, m_d_dt_bias, m_d_norm_g, m_w_out, m_final_g, v_norm_g, v_w_in, v_a_conv_w, v_a_conv_b, v_a_ln_g, v_a_ln_b, v_a_pw_w, v_a_pw_b, v_s5_lambda_re, v_s5_lambda_im, v_s5_b_re, v_s5_b_im, v_s5_c_re, v_s5_c_im, v_s5_d, v_s5_log_dt, v_s5_glu_w, v_s5_glu_b, v_c_conv_w, v_d_conv_w, v_d_a_log, v_d_dt_bias, v_d_norm_g, v_w_out, v_final_g):
    weights = dict(norm_g=norm_g, w_in=w_in, a_conv_w=a_conv_w, a_conv_b=a_conv_b, a_ln_g=a_ln_g, a_ln_b=a_ln_b, a_pw_w=a_pw_w, a_pw_b=a_pw_b, s5_lambda_re=s5_lambda_re, s5_lambda_im=s5_lambda_im, s5_b_re=s5_b_re, s5_b_im=s5_b_im, s5_c_re=s5_c_re, s5_c_im=s5_c_im, s5_d=s5_d, s5_log_dt=s5_log_dt, s5_glu_w=s5_glu_w, s5_glu_b=s5_glu_b, c_conv_w=c_conv_w, d_conv_w=d_conv_w, d_a_log=d_a_log, d_dt_bias=d_dt_bias, d_norm_g=d_norm_g, w_out=w_out, final_g=final_g)
    mom_m = dict(norm_g=m_norm_g, w_in=m_w_in, a_conv_w=m_a_conv_w, a_conv_b=m_a_conv_b, a_ln_g=m_a_ln_g, a_ln_b=m_a_ln_b, a_pw_w=m_a_pw_w, a_pw_b=m_a_pw_b, s5_lambda_re=m_s5_lambda_re, s5_lambda_im=m_s5_lambda_im, s5_b_re=m_s5_b_re, s5_b_im=m_s5_b_im, s5_c_re=m_s5_c_re, s5_c_im=m_s5_c_im, s5_d=m_s5_d, s5_log_dt=m_s5_log_dt, s5_glu_w=m_s5_glu_w, s5_glu_b=m_s5_glu_b, c_conv_w=m_c_conv_w, d_conv_w=m_d_conv_w, d_a_log=m_d_a_log, d_dt_bias=m_d_dt_bias, d_norm_g=m_d_norm_g, w_out=m_w_out, final_g=m_final_g)
    mom_v = dict(norm_g=v_norm_g, w_in=v_w_in, a_conv_w=v_a_conv_w, a_conv_b=v_a_conv_b, a_ln_g=v_a_ln_g, a_ln_b=v_a_ln_b, a_pw_w=v_a_pw_w, a_pw_b=v_a_pw_b, s5_lambda_re=v_s5_lambda_re, s5_lambda_im=v_s5_lambda_im, s5_b_re=v_s5_b_re, s5_b_im=v_s5_b_im, s5_c_re=v_s5_c_re, s5_c_im=v_s5_c_im, s5_d=v_s5_d, s5_log_dt=v_s5_log_dt, s5_glu_w=v_s5_glu_w, s5_glu_b=v_s5_glu_b, c_conv_w=v_c_conv_w, d_conv_w=v_d_conv_w, d_a_log=v_d_a_log, d_dt_bias=v_d_dt_bias, d_norm_g=v_d_norm_g, w_out=v_w_out, final_g=v_final_g)
    outs = _step(x[0], loss_target[0], weights, mom_m, mom_v)
    return (outs[0], outs[1][None]) + outs[2:]
```

```python
import functools

import jax
import jax.numpy as jnp
from jax import lax
from jax.experimental import pallas as pl
from jax.experimental.pallas import tpu as pltpu

F32 = jnp.float32
BF16 = jnp.bfloat16
HI = lax.Precision.HIGHEST
SDS = jax.ShapeDtypeStruct

N_DEV = 8
D = 1024
BR = 256
DEPTH = 4
IN_COLS = 3336
PW = 3456
AB_COL = 3328
EPS = 1e-6
TL = 512
SEG = TL // 8
HALO = 32
HALO_S = 8
KA, KC, KD = 31, 3, 4
CH = 64
DN_GROUP = 4
NH, HD = 4, 64
NSTATE = 1024
VMEM_LIMIT = 56 * 1024 * 1024

ADAM_LR, ADAM_B1, ADAM_B2, ADAM_EPS, ADAM_WD, ADAM_STEP = 0.001, 0.9, 0.999, 1e-08, 0.01, 10

NN = ((1,), (0,))
NT = ((1,), (1,))
TN = ((0,), (0,))


def _dot(a, b, dims, prec=None):
    return lax.dot_general(a, b, (dims, ((), ())), precision=prec, preferred_element_type=F32)


def _make_mm(cast, prec, fwd_dims):
    def prep(t):
        return t.astype(cast) if cast is not None else t

    @jax.custom_vjp
    def mm(a, w):
        return _dot(prep(a), prep(w), fwd_dims, prec)

    def fwd(a, w):
        return mm(a, w), (a, w)

    def bwd(res, dy):
        a, w = res
        a, w, dy = prep(a), prep(w), prep(dy)
        if fwd_dims == NN:
            return _dot(dy, w, NT, prec), _dot(a, dy, TN, prec)
        if fwd_dims == NT:
            return _dot(dy, w, NN, prec), _dot(dy, a, TN, prec)
        return _dot(w, dy, NT, prec), _dot(a, dy, NN, prec)

    mm.defvjp(fwd, bwd)
    return mm


mm = _make_mm(BF16, None, NN)
mm_nt = _make_mm(BF16, None, NT)
mm_tn = _make_mm(BF16, None, TN)
mmh = _make_mm(None, HI, NN)
mmh_nt = _make_mm(None, HI, NT)


def _sigmoid(x):
    return jax.nn.sigmoid(x)


def _silu(x):
    return x * jax.nn.sigmoid(x)


def _gelu(x):
    return 0.5 * x * (1.0 + jnp.tanh(0.7978845608028654 * (x + 0.044715 * (x * x * x))))


def _softplus(x):
    return jnp.maximum(x, 0.0) + jnp.log1p(jnp.exp(-jnp.abs(x)))


def _rms(x, g):
    return x * lax.rsqrt(jnp.mean(x * x, axis=-1, keepdims=True) + EPS) * g


def _cparams(sem):
    return pltpu.CompilerParams(dimension_semantics=sem, vmem_limit_bytes=VMEM_LIMIT)


def _tap_offsets(halo, taps):
    return [halo - (taps - 1) + k for k in range(taps)]


def _conv_fwd_impl(acat, w, tile, halo, taps):
    n = tile + halo
    out = None
    for k, off in enumerate(_tap_offsets(halo, taps)):
        src = jnp.roll(acat, n - off, axis=0)[:tile, :] if off != halo else acat[halo:, :]
        term = src * w[k:k + 1, :]
        out = term if out is None else out + term
    return out


def _make_conv(tile, halo, taps):
    @jax.custom_vjp
    def conv(acat, w):
        return _conv_fwd_impl(acat, w, tile, halo, taps)

    def fwd(acat, w):
        return conv(acat, w), (acat, w)

    def bwd(res, dy):
        acat, w = res
        n = tile + halo
        dyp = jnp.concatenate([dy, jnp.zeros((halo, dy.shape[1]), F32)], axis=0)
        rows = lax.broadcasted_iota(jnp.int32, w.shape, 0)
        dacat = None
        dw = jnp.zeros(w.shape, F32)
        for k, off in enumerate(_tap_offsets(halo, taps)):
            term = jnp.roll(dyp, off, axis=0) * w[k:k + 1, :]
            dacat = term if dacat is None else dacat + term
            src = jnp.roll(acat, n - off, axis=0)[:tile, :] if off != halo else acat[halo:, :]
            dw = dw + jnp.where(rows == k, jnp.sum(dy * src, axis=0, keepdims=True), 0.0)
        return dacat, dw

    conv.defvjp(fwd, bwd)
    return conv


def _halo_spec(tile, halo, width, col):
    per = tile // halo
    return pl.BlockSpec((halo, width), lambda i: (jnp.maximum(i * per - 1, 0), col))


def _halo_spec_rev(nt, tile, halo, width, col):
    per = tile // halo
    return pl.BlockSpec((halo, width), lambda i: (jnp.maximum((nt - 1 - i) * per - 1, 0), col))


def _inproj_call(x, g, w):
    L = x.shape[0]
    tn = 1152

    def body(x_ref, g_ref, w_ref, p_ref, h_ref):
        h = _rms(x_ref[...], g_ref[...]).astype(BF16)
        h_ref[...] = h
        p_ref[...] = _dot(h, w_ref[...], NN)

    return pl.pallas_call(
        body, name="inproj", grid=(L // TL, PW // tn),
        in_specs=[pl.BlockSpec((TL, D), lambda i, j: (i, 0)), pl.BlockSpec((1, D), lambda i, j: (0, 0)),
                  pl.BlockSpec((D, tn), lambda i, j: (0, j))],
        out_specs=[pl.BlockSpec((TL, tn), lambda i, j: (i, j)), pl.BlockSpec((TL, D), lambda i, j: (i, 0))],
        out_shape=[SDS((L, PW), F32), SDS((L, D), BF16)],
        compiler_params=_cparams(("parallel", "arbitrary")),
    )(x, g, w)


def _outproj_call(x, ys, w):
    L = x.shape[0]

    def body(x_ref, a_ref, b_ref, c_ref, d_ref, w_ref, o_ref):
        acc = x_ref[...]
        for b, y_ref in enumerate((a_ref, b_ref, c_ref, d_ref)):
            acc = acc + _dot(y_ref[...].astype(BF16), w_ref[b * BR:(b + 1) * BR, :], NN)
        o_ref[...] = acc

    yspec = pl.BlockSpec((TL, BR), lambda i: (i, 0))
    return pl.pallas_call(
        body, name="outproj", grid=(L // TL,),
        in_specs=[pl.BlockSpec((TL, D), lambda i: (i, 0)), yspec, yspec, yspec, yspec,
                  pl.BlockSpec((D, D), lambda i: (0, 0))],
        out_specs=pl.BlockSpec((TL, D), lambda i: (i, 0)),
        out_shape=SDS((L, D), F32),
        compiler_params=_cparams(("parallel",)),
    )(x, *ys, w)


def _loss_call(x, g, target):
    L = x.shape[0]

    def body(x_ref, g_ref, t_ref, dx_ref, dg_ref, loss_ref):
        @pl.when(pl.program_id(0) == 0)
        def _():
            dg_ref[...] = jnp.zeros_like(dg_ref)
            loss_ref[...] = jnp.zeros_like(loss_ref)

        y, vjp = jax.vjp(_rms, x_ref[...], g_ref[...])
        err = y - t_ref[...]
        dx, dg = vjp(err * (1.0 / D))
        dx_ref[...] = dx
        dg_ref[...] += dg
        tot = jnp.sum(jnp.sum(err * err, axis=1, keepdims=True), axis=0, keepdims=True)
        loss_ref[...] += jnp.broadcast_to(tot * (0.5 / D), loss_ref.shape)

    return pl.pallas_call(
        body, name="loss_head", grid=(L // TL,),
        in_specs=[pl.BlockSpec((TL, D), lambda i: (i, 0)), pl.BlockSpec((1, D), lambda i: (0, 0)),
                  pl.BlockSpec((TL, D), lambda i: (i, 0))],
        out_specs=[pl.BlockSpec((TL, D), lambda i: (i, 0)), pl.BlockSpec((1, D), lambda i: (0, 0)),
                   pl.BlockSpec((1, 128), lambda i: (0, 0))],
        out_shape=[SDS((L, D), F32), SDS((1, D), F32), SDS((1, 128), F32)],
        compiler_params=_cparams(("arbitrary",)),
    )(x, g, target)


def _branch_a(valw, gatew, z, cw, cb, lg, lb, pw, pb, conv):
    a = conv(valw * _sigmoid(gatew), cw) + cb
    mu = jnp.mean(a, axis=-1, keepdims=True)
    xc = a - mu
    y = xc * lax.rsqrt(jnp.mean(xc * xc, axis=-1, keepdims=True) + EPS) * lg + lb
    y = mm(_silu(y), pw) + pb
    return y * _silu(z)


def _branch_c(bg, cw_, xw, z, w3, conv):
    return bg * conv(cw_ * xw, w3) * _silu(z)


def _ac_fwd_call(proj, p):
    L = proj.shape[0]

    def body(val, gate, za, hval, hgate, cb_, cc, cx, cz, hcc, hcx,
             acw, acb, alg, alb, apw, apb, ccw, ya_ref, yc_ref):
        nf = (pl.program_id(0) > 0).astype(F32)
        win = lambda h, m: jnp.concatenate([h[...] * nf, m[...]], axis=0)
        conv_a = functools.partial(_conv_fwd_impl, tile=TL, halo=HALO, taps=KA)
        conv_c = functools.partial(_conv_fwd_impl, tile=TL, halo=HALO, taps=KC)
        ya_ref[...] = _branch_a(win(hval, val), win(hgate, gate), za[...], acw[...], acb[...], alg[...], alb[...],
                                apw[...], apb[...], conv_a)
        yc_ref[...] = _branch_c(cb_[...], win(hcc, cc), win(hcx, cx), cz[...], ccw[...], conv_c)

    col = lambda j: pl.BlockSpec((TL, BR), lambda i: (i, j))
    hal = lambda j: _halo_spec(TL, HALO, BR, j)
    full = lambda a: pl.BlockSpec(a.shape, lambda i: (0,) * a.ndim)
    params = (p["a_conv_w"], p["a_conv_b"], p["a_ln_g"], p["a_ln_b"], p["a_pw_w"], p["a_pw_b"], p["c_conv_w"])
    return pl.pallas_call(
        body, name="ac_fwd", grid=(L // TL,),
        in_specs=[col(0), col(1), col(2), hal(0), hal(1), col(5), col(6), col(7), col(8), hal(6), hal(7)]
        + [full(a) for a in params],
        out_specs=[pl.BlockSpec((TL, BR), lambda i: (i, 0))] * 2,
        out_shape=[SDS((L, BR), F32)] * 2,
        compiler_params=_cparams(("parallel",)),
    )(*([proj] * 11), *params)


def _ac_bwd_call(proj, p, dya, dyc):
    L = proj.shape[0]
    nt = L // TL

    def body(val, gate, za, hval, hgate, cb_, cc, cx, cz, hcc, hcx,
             acw, acb, alg, alb, apw, apb, ccw, dya_ref, dyc_ref,
             da_ref, dc_ref, g_acw, g_acb, g_alg, g_alb, g_apw, g_apb, g_ccw, carry):
        i = pl.program_id(0)
        gouts = (g_acw, g_acb, g_alg, g_alb, g_apw, g_apb, g_ccw)

        @pl.when(i == 0)
        def _():
            carry[...] = jnp.zeros_like(carry)
            for r in gouts:
                r[...] = jnp.zeros_like(r)

        nf = (i < nt - 1).astype(F32)
        win = lambda h, m: jnp.concatenate([h[...] * nf, m[...]], axis=0)
        conv_a = _make_conv(TL, HALO, KA)
        conv_c = _make_conv(TL, HALO, KC)

        def f(valw, gatew, z, bg, ccw_, cxw, czv, w1, b1, lg, lb, pw, pb, w3):
            return (_branch_a(valw, gatew, z, w1, b1, lg, lb, pw, pb, conv_a),
                    _branch_c(bg, ccw_, cxw, czv, w3, conv_c))

        _, vjp = jax.vjp(f, win(hval, val), win(hgate, gate), za[...], cb_[...], win(hcc, cc), win(hcx, cx), cz[...],
                         acw[...], acb[...], alg[...], alb[...], apw[...].astype(F32), apb[...], ccw[...])
        (dvalw, dgatew, dz, dbg, dccw, dcxw, dczv, d1, d2, d3, d4, d5, d6, d7) = vjp((dya_ref[...], dyc_ref[...]))

        def settle(slot, dwin):
            tail = jnp.concatenate([jnp.zeros((TL - HALO, BR), F32), carry[slot]], axis=0)
            carry[slot] = dwin[:HALO, :]
            return (dwin[HALO:, :] + tail).astype(BF16)

        da_ref[:, 0:BR] = settle(0, dvalw)
        da_ref[:, BR:2 * BR] = settle(1, dgatew)
        da_ref[:, 2 * BR:3 * BR] = dz.astype(BF16)
        dc_ref[:, 0:BR] = dbg.astype(BF16)
        dc_ref[:, BR:2 * BR] = settle(2, dccw)
        dc_ref[:, 2 * BR:3 * BR] = settle(3, dcxw)
        dc_ref[:, 3 * BR:4 * BR] = dczv.astype(BF16)
        for r, g in zip(gouts, (d1, d2, d3, d4, d5, d6, d7)):
            r[...] += g

    col = lambda j: pl.BlockSpec((TL, BR), lambda i: (nt - 1 - i, j))
    hal = lambda j: _halo_spec_rev(nt, TL, HALO, BR, j)
    full = lambda a: pl.BlockSpec(a.shape, lambda i: (0,) * a.ndim)
    params = (p["a_conv_w"], p["a_conv_b"], p["a_ln_g"], p["a_ln_b"], p["a_pw_w"], p["a_pw_b"], p["c_conv_w"])
    rev = lambda w: pl.BlockSpec((TL, w), lambda i: (nt - 1 - i, 0))
    return pl.pallas_call(
        body, name="ac_bwd", grid=(nt,),
        in_specs=[col(0), col(1), col(2), hal(0), hal(1), col(5), col(6), col(7), col(8), hal(6), hal(7)]
        + [full(a) for a in params] + [rev(BR), rev(BR)],
        out_specs=[rev(3 * BR), rev(4 * BR)] + [full(a) for a in params],
        out_shape=[SDS((L, 3 * BR), BF16), SDS((L, 4 * BR), BF16)] + [SDS(a.shape, F32) for a in params],
        scratch_shapes=[pltpu.VMEM((4, HALO, BR), F32)],
        compiler_params=_cparams(("arbitrary",)),
    )(*([proj] * 11), *params, dya, dyc)


def _iota2(shape, dim):
    return lax.broadcasted_iota(jnp.int32, shape, dim)


def _s5_params(lam_re, lam_im, logdt, b_re, b_im, c_re, c_im):
    eg = (_iota2((16, NSTATE), 1) >> 6 == _iota2((16, NSTATE), 0)).astype(F32)
    dt = jnp.exp(mmh(jnp.broadcast_to(logdt, (8, 16)), eg)[0:1, :])
    lr = jnp.minimum(lam_re, -1e-4)
    li = lam_im
    mag = jnp.exp(lr * dt)
    lbr = mag * jnp.cos(li * dt)
    lbi = mag * jnp.sin(li * dt)
    den = lr * lr + li * li
    nr = lbr - 1.0
    fr = (nr * lr + lbi * li) / den
    fi = (lbi * lr - nr * li) / den
    row = _iota2((8, NSTATE), 0)
    f8 = jnp.where(row == 0, fr, jnp.where(row == 1, fi, 0.0))
    eye = (_iota2((NSTATE, NSTATE), 0) == _iota2((NSTATE, NSTATE), 1)).astype(F32)
    fcol = mmh_nt(eye, f8)
    frc, fic = fcol[:, 0:1], fcol[:, 1:2]
    bbr = frc * b_re - fic * b_im
    bbi = frc * b_im + fic * b_re
    e1 = ((_iota2((16, BR), 1) & 15) == _iota2((16, BR), 0)).astype(F32)
    m1 = ((_iota2((NSTATE, BR), 0) >> 6) == (_iota2((NSTATE, BR), 1) >> 4)).astype(F32)
    wbr = mmh(bbr, e1) * m1
    wbi = mmh(bbi, e1) * m1
    e2 = ((_iota2((64, NSTATE), 1) & 63) == _iota2((64, NSTATE), 0)).astype(F32)
    m2 = ((_iota2((BR, NSTATE), 0) >> 4) == (_iota2((BR, NSTATE), 1) >> 6)).astype(F32)
    wcr = mmh(c_re, e2) * m2
    wci = mmh(c_im, e2) * m2
    return lbr, lbi, wbr, wbi, wcr, wci


_S5_OUT = [(1, NSTATE), (1, NSTATE), (NSTATE, BR), (NSTATE, BR), (BR, NSTATE), (BR, NSTATE)]


def _s5_prep_call(sp):
    def body(lre, lim, ldt, bre, bim, cre, cim, o_lbr, o_lbi, o_wbr, o_wbi, o_wcr, o_wci, pwr, pwi, qwr, qwi):
        lbr, lbi, wbr, wbi, wcr, wci = _s5_params(lre[...], lim[...], ldt[...], bre[...], bim[...], cre[...], cim[...])
        o_lbr[...], o_lbi[...], o_wbr[...], o_wbi[...], o_wcr[...], o_wci[...] = lbr, lbi, wbr, wbi, wcr, wci
        pr, pi = lbr, lbi
        for i in range(SEG):
            pwr[i:i + 1, :] = pr
            pwi[i:i + 1, :] = pi
            qwr[SEG - 1 - i:SEG - i, :] = pr
            qwi[SEG - 1 - i:SEG - i, :] = -pi
            pr, pi = pr * lbr - pi * lbi, pr * lbi + pi * lbr

    args = (sp["lam_re"], sp["lam_im"], sp["log_dt"], sp["b_re"], sp["b_im"], sp["c_re"], sp["c_im"])
    return pl.pallas_call(
        body, name="s5_prep",
        out_shape=[SDS(s, F32) for s in _S5_OUT] + [SDS((SEG, NSTATE), F32)] * 4,
        compiler_params=pltpu.CompilerParams(vmem_limit_bytes=VMEM_LIMIT),
    )(*args)


def _s5_prep_bwd_call(sp, cots):
    def body(lre, lim, ldt, bre, bim, cre, cim, c0, c1, c2, c3, c4, c5, *outs):
        _, vjp = jax.vjp(_s5_params, lre[...], lim[...], ldt[...], bre[...], bim[...], cre[...], cim[...])
        grads = vjp((c0[...], c1[...], c2[...], c3[...], c4[...], c5[...]))
        for o, g in zip(outs, grads):
            o[...] = g

    args = (sp["lam_re"], sp["lam_im"], sp["log_dt"], sp["b_re"], sp["b_im"], sp["c_re"], sp["c_im"])
    return pl.pallas_call(
        body, name="s5_prep_bwd",
        out_shape=[SDS(a.shape, F32) for a in args],
        compiler_params=pltpu.CompilerParams(vmem_limit_bytes=VMEM_LIMIT),
    )(*args, *cots)


def _lanes(v, j):
    return v[:, j * 128:(j + 1) * 128]


def _s5_scan(sre, sim, lbr, lbi, pwr, pwi, cin_r, cin_i, reverse):
    lr = [jnp.broadcast_to(_lanes(lbr, j), (8, 128)) for j in range(8)]
    li = [jnp.broadcast_to(_lanes(lbi, j), (8, 128)) for j in range(8)]

    def step(t, st):
        i = SEG - 1 - t if reverse else t
        rows = pl.ds(i, 8, stride=SEG)
        new = []
        for j in range(8):
            sr, si = st[2 * j], st[2 * j + 1]
            vr, vi = sre.at[j], sim.at[j]
            nr = lr[j] * sr - li[j] * si + vr[rows, :]
            ni = lr[j] * si + li[j] * sr + vi[rows, :]
            vr[rows, :] = nr
            vi[rows, :] = ni
            new += [nr, ni]
        return tuple(new)

    ends = lax.fori_loop(0, SEG, step, tuple(jnp.zeros((8, 128), F32) for _ in range(16)))
    e_r = jnp.concatenate([ends[2 * j] for j in range(8)], axis=1)
    e_i = jnp.concatenate([ends[2 * j + 1] for j in range(8)], axis=1)
    l64r, l64i = (pwr[0:1, :], pwi[0:1, :]) if reverse else (pwr[SEG - 1:SEG, :], pwi[SEG - 1:SEG, :])
    order = range(7, -1, -1) if reverse else range(8)
    c_r, c_i = cin_r, cin_i
    carries = {}
    for k in order:
        carries[k] = (c_r, c_i)
        er, ei = e_r[k:k + 1, :], e_i[k:k + 1, :]
        c_r, c_i = er + l64r * c_r - l64i * c_i, ei + l64r * c_i + l64i * c_r
    for j in range(8):
        pr, pi = pwr[:, j * 128:(j + 1) * 128], pwi[:, j * 128:(j + 1) * 128]
        for k in range(8):
            cr, ci = _lanes(carries[k][0], j), _lanes(carries[k][1], j)
            blk = slice(k * SEG, (k + 1) * SEG)
            sre[j, blk, :] = sre[j, blk, :] + pr * cr - pi * ci
            sim[j, blk, :] = sim[j, blk, :] + pr * ci + pi * cr
    return c_r, c_i


def _bdot(a, b, dims):
    return _dot(a.astype(BF16), b.astype(BF16), dims)


def _s5_states(u, wbr, wbi, sre, sim):
    bur = _bdot(u, wbr, NT)
    bui = _bdot(u, wbi, NT)
    for j in range(8):
        sre[j] = _lanes(bur, j)
        sim[j] = _lanes(bui, j)


def _gather_lanes(s):
    return jnp.concatenate([s[j] for j in range(8)], axis=1)


def _s5_post(s_re, s_im, u, z, wcr, wci, dsk, gw, gb):
    y = mm_nt(s_re, wcr) - mm_nt(s_im, wci) + dsk * u
    yg = _gelu(y)
    return yg * _sigmoid(mm(yg, gw) + gb) * _silu(z)


def _s5_fwd_call(proj, prep, p):
    L = proj.shape[0]
    nt = L // TL
    lbr, lbi, wbr, wbi, wcr, wci, pwr, pwi, _, _ = prep

    def body(u_ref, z_ref, lbr_r, lbi_r, wbr_r, wbi_r, wcr_r, wci_r, pwr_r, pwi_r, d_r, gw_r, gb_r,
             yb_ref, cinr_ref, cini_ref, sre, sim, car, cai):
        @pl.when(pl.program_id(0) == 0)
        def _():
            car[...] = jnp.zeros_like(car)
            cai[...] = jnp.zeros_like(cai)

        u = u_ref[...]
        cinr_ref[0] = car[...]
        cini_ref[0] = cai[...]
        _s5_states(u, wbr_r[...], wbi_r[...], sre, sim)
        nr, ni = _s5_scan(sre, sim, lbr_r[...], lbi_r[...], pwr_r, pwi_r, car[...], cai[...], False)
        car[...] = nr
        cai[...] = ni
        yb_ref[...] = _s5_post(_gather_lanes(sre), _gather_lanes(sim), u, z_ref[...], wcr_r[...], wci_r[...],
                               d_r[...], gw_r[...], gb_r[...])

    full = lambda a: pl.BlockSpec(a.shape, lambda i: (0,) * a.ndim)
    consts = (lbr, lbi, wbr, wbi, wcr, wci, pwr, pwi, p["s5_d"], p["s5_glu_w"], p["s5_glu_b"])
    cspec = pl.BlockSpec((1, 1, NSTATE), lambda i: (i, 0, 0))
    return pl.pallas_call(
        body, name="s5_fwd", grid=(nt,),
        in_specs=[pl.BlockSpec((TL, BR), lambda i: (i, 3)), pl.BlockSpec((TL, BR), lambda i: (i, 4))]
        + [full(a) for a in consts],
        out_specs=[pl.BlockSpec((TL, BR), lambda i: (i, 0)), cspec, cspec],
        out_shape=[SDS((L, BR), F32), SDS((nt, 1, NSTATE), F32), SDS((nt, 1, NSTATE), F32)],
        scratch_shapes=[pltpu.VMEM((8, TL, 128), F32), pltpu.VMEM((8, TL, 128), F32),
                        pltpu.VMEM((1, NSTATE), F32), pltpu.VMEM((1, NSTATE), F32)],
        compiler_params=_cparams(("arbitrary",)),
    )(proj, proj, *consts)


def _s5_bwd_call(proj, prep, p, cin_r, cin_i, dyb):
    L = proj.shape[0]
    nt = L // TL
    lbr, lbi, wbr, wbi, wcr, wci, pwr, pwi, qwr, qwi = prep

    def body(u_ref, z_ref, lbr_r, lbi_r, wbr_r, wbi_r, wcr_r, wci_r, pwr_r, pwi_r, qwr_r, qwi_r, d_r, gw_r, gb_r,
             cinr_ref, cini_ref, dy_ref,
             db_ref, g_lbr, g_lbi, g_wbr, g_wbi, g_wcr, g_wci, g_d, g_gw, g_gb, sre, sim, gre, gim, car, cai):
        gouts = (g_lbr, g_lbi, g_wbr, g_wbi, g_wcr, g_wci, g_d, g_gw, g_gb)

        @pl.when(pl.program_id(0) == 0)
        def _():
            car[...] = jnp.zeros_like(car)
            cai[...] = jnp.zeros_like(cai)
            for r in gouts:
                r[...] = jnp.zeros_like(r)

        u = u_ref[...]
        lr, li = lbr_r[...], lbi_r[...]
        c0r, c0i = cinr_ref[0], cini_ref[0]
        _s5_states(u, wbr_r[...], wbi_r[...], sre, sim)
        _s5_scan(sre, sim, lr, li, pwr_r, pwi_r, c0r, c0i, False)
        s_re, s_im = _gather_lanes(sre), _gather_lanes(sim)
        _, vjp = jax.vjp(_s5_post, s_re, s_im, u, z_ref[...], wcr_r[...], wci_r[...], d_r[...],
                         gw_r[...].astype(F32), gb_r[...])
        ds_re, ds_im, du, dz, dwcr, dwci, dd, dgw, dgb = vjp(dy_ref[...])
        for j in range(8):
            gre[j] = _lanes(ds_re, j)
            gim[j] = _lanes(ds_im, j)
        nr, ni = _s5_scan(gre, gim, lr, -li, qwr_r, qwi_r, car[...], cai[...], True)
        car[...] = nr
        cai[...] = ni
        a_re, a_im = _gather_lanes(gre), _gather_lanes(gim)
        first = _iota2((TL, NSTATE), 0) == 0
        p_re = jnp.where(first, c0r, jnp.roll(s_re, 1, axis=0))
        p_im = jnp.where(first, c0i, jnp.roll(s_im, 1, axis=0))
        g_lbr[...] += jnp.sum(a_re * p_re + a_im * p_im, axis=0, keepdims=True)
        g_lbi[...] += jnp.sum(a_im * p_re - a_re * p_im, axis=0, keepdims=True)
        du = du + _bdot(a_re, wbr_r[...], NN) + _bdot(a_im, wbi_r[...], NN)
        g_wbr[...] += _bdot(a_re, u, TN)
        g_wbi[...] += _bdot(a_im, u, TN)
        g_wcr[...] += dwcr
        g_wci[...] += dwci
        g_d[...] += dd
        g_gw[...] += dgw
        g_gb[...] += dgb
        db_ref[:, 0:BR] = du.astype(BF16)
        db_ref[:, BR:2 * BR] = dz.astype(BF16)

    full = lambda a: pl.BlockSpec(a.shape, lambda i: (0,) * a.ndim)
    consts = (lbr, lbi, wbr, wbi, wcr, wci, pwr, pwi, qwr, qwi, p["s5_d"], p["s5_glu_w"], p["s5_glu_b"])
    cspec = pl.BlockSpec((1, 1, NSTATE), lambda i: (nt - 1 - i, 0, 0))
    gshapes = _S5_OUT + [(1, BR), (BR, BR), (1, BR)]
    return pl.pallas_call(
        body, name="s5_bwd", grid=(nt,),
        in_specs=[pl.BlockSpec((TL, BR), lambda i: (nt - 1 - i, 3)), pl.BlockSpec((TL, BR), lambda i: (nt - 1 - i, 4))]
        + [full(a) for a in consts] + [cspec, cspec, pl.BlockSpec((TL, BR), lambda i: (nt - 1 - i, 0))],
        out_specs=[pl.BlockSpec((TL, 2 * BR), lambda i: (nt - 1 - i, 0))]
        + [pl.BlockSpec(s, lambda i: (0, 0)) for s in gshapes],
        out_shape=[SDS((L, 2 * BR), BF16)] + [SDS(s, F32) for s in gshapes],
        scratch_shapes=[pltpu.VMEM((8, TL, 128), F32)] * 4 + [pltpu.VMEM((1, NSTATE), F32)] * 2,
        compiler_params=_cparams(("arbitrary",)),
    )(proj, proj, *consts, cin_r, cin_i, dyb)


def _heads(x):
    return [x[:, h * HD:(h + 1) * HD] for h in range(NH)]


def _l2n(x, scale):
    return jnp.concatenate([xh * (lax.rsqrt(jnp.sum(xh * xh, axis=-1, keepdims=True) + EPS) * scale)
                            for xh in _heads(x)], axis=1)


def _dn_pre(qkvw, ab, cw, alog, dtb, conv, rows):
    c = _silu(conv(qkvw, cw))
    q = _l2n(c[:, 0:BR], HD ** -0.5)
    k = _l2n(c[:, BR:2 * BR], 1.0)
    v = c[:, 2 * BR:3 * BR]
    g = -jnp.exp(alog) * _softplus(ab + dtb)
    ri, ci = _iota2((rows, rows), 0), _iota2((rows, rows), 1)
    tri = ((ri >= ci) & ((ri >> 6) == (ci >> 6))).astype(F32)
    gc = mmh(tri, g)
    lane = _iota2(ab.shape, 1)
    return q, k, v, jnp.where(lane < NH, gc, jnp.where(lane < 2 * NH, _sigmoid(ab), 0.0))


def _dn_pre_fwd_call(proj, p):
    L = proj.shape[0]

    def body(m_ref, h_ref, ab_ref, cw, alog, dtb, q_ref, k_ref, v_ref, gb_ref):
        nf = (pl.program_id(0) > 0).astype(F32)
        qkvw = jnp.concatenate([h_ref[...] * nf, m_ref[...]], axis=0)
        conv = functools.partial(_conv_fwd_impl, tile=TL, halo=HALO_S, taps=KD)
        q_ref[...], k_ref[...], v_ref[...], gb_ref[...] = _dn_pre(qkvw, ab_ref[...], cw[...], alog[...], dtb[...], conv, TL)

    full = lambda a: pl.BlockSpec(a.shape, lambda i: (0,) * a.ndim)
    params = (p["d_conv_w"], p["d_a_log"], p["d_dt_bias"])
    o = pl.BlockSpec((TL, BR), lambda i: (i, 0))
    return pl.pallas_call(
        body, name="dn_pre_fwd", grid=(L // TL,),
        in_specs=[pl.BlockSpec((TL, 3 * BR), lambda i: (i, 3)), _halo_spec(TL, HALO_S, 3 * BR, 3),
                  pl.BlockSpec((TL, 128), lambda i: (i, AB_COL // 128))] + [full(a) for a in params],
        out_specs=[o, o, o, pl.BlockSpec((TL, 128), lambda i: (i, 0))],
        out_shape=[SDS((L, BR), F32)] * 3 + [SDS((L, 128), F32)],
        compiler_params=_cparams(("parallel",)),
    )(proj, proj, proj, *params)


def _dn_pre_bwd_call(proj, p, dq, dk, dv, dgb):
    L = proj.shape[0]
    nt = L // TL

    def body(m_ref, h_ref, ab_ref, cw, alog, dtb, dq_r, dk_r, dv_r, dgb_r,
             dqkv_ref, dab_ref, g_cw, g_alog, g_dtb, carry):
        i = pl.program_id(0)

        @pl.when(i == 0)
        def _():
            carry[...] = jnp.zeros_like(carry)
            for r in (g_cw, g_alog, g_dtb):
                r[...] = jnp.zeros_like(r)

        nf = (i < nt - 1).astype(F32)
        qkvw = jnp.concatenate([h_ref[...] * nf, m_ref[...]], axis=0)
        conv = _make_conv(TL, HALO_S, KD)
        _, vjp = jax.vjp(lambda a, b, c, d, e: _dn_pre(a, b, c, d, e, conv, TL),
                         qkvw, ab_ref[...], cw[...], alog[...], dtb[...])
        dwin, dab, dcw, dalog, ddtb = vjp((dq_r[...], dk_r[...], dv_r[...], dgb_r[...]))
        tail = jnp.concatenate([jnp.zeros((TL - HALO_S, 3 * BR), F32), carry[...]], axis=0)
        carry[...] = dwin[:HALO_S, :]
        dqkv_ref[...] = (dwin[HALO_S:, :] + tail).astype(BF16)
        dab_ref[...] = dab.astype(BF16)
        g_cw[...] += dcw
        g_alog[...] += dalog
        g_dtb[...] += ddtb

    full = lambda a: pl.BlockSpec(a.shape, lambda i: (0,) * a.ndim)
    params = (p["d_conv_w"], p["d_a_log"], p["d_dt_bias"])
    rev = lambda w: pl.BlockSpec((TL, w), lambda i: (nt - 1 - i, 0))
    return pl.pallas_call(
        body, name="dn_pre_bwd", grid=(nt,),
        in_specs=[pl.BlockSpec((TL, 3 * BR), lambda i: (nt - 1 - i, 3)), _halo_spec_rev(nt, TL, HALO_S, 3 * BR, 3),
                  pl.BlockSpec((TL, 128), lambda i: (nt - 1 - i, AB_COL // 128))] + [full(a) for a in params]
        + [rev(BR), rev(BR), rev(BR), rev(128)],
        out_specs=[rev(3 * BR), rev(128)] + [full(a) for a in params],
        out_shape=[SDS((L, 3 * BR), BF16), SDS((L, 128), BF16)] + [SDS(a.shape, F32) for a in params],
        scratch_shapes=[pltpu.VMEM((HALO_S, 3 * BR), F32)],
        compiler_params=_cparams(("arbitrary",)),
    )(proj, proj, proj, *params, dq, dk, dv, dgb)


def _dn_group(q, k, v, gb, z, ng, *s):
    ri, ci = _iota2((CH, CH), 0), _iota2((CH, CH), 1)
    causal, strict = ri >= ci, ri > ci
    eye = (ri == ci).astype(F32)
    s = list(s)
    pairs = []
    for c in range(DN_GROUP):
        rows = slice(c * CH, (c + 1) * CH)
        gbc = gb[rows, :]
        for h, (qh, kh, vh, zh) in enumerate(zip(_heads(q[rows, :]), _heads(k[rows, :]), _heads(v[rows, :]),
                                                 _heads(z[rows, :]))):
            gc = jnp.broadcast_to(gbc[:, h:h + 1], (CH, HD))
            beta = gbc[:, NH + h:NH + h + 1]
            decay = jnp.where(causal, jnp.exp(jnp.where(causal, gc - gc.T, 0.0)), 0.0)
            egc = jnp.exp(gc)
            glast = gc[CH - 1:CH, :]
            kb = kh * beta
            pairs.append(dict(q=qh, k=kh, z=zh, decay=decay, qe=qh * egc, kd=kh * jnp.exp(glast - gc),
                              sdec=jnp.exp(glast[:, 0:1]), kb=kb, rhs=jnp.concatenate([vh * beta, kb * egc], axis=1)))
    for p in pairs:
        p["pw"] = jnp.where(strict, mm_nt(p["kb"], p["k"]) * p["decay"], 0.0)
        p["t"] = eye - p["pw"]
    for _ in range(5):
        for p in pairs:
            p["pw"] = mm(p["pw"], p["pw"])
        for p in pairs:
            p["t"] = mm(p["t"], eye + p["pw"])
    for p in pairs:
        p["uw"] = mm(p["t"], p["rhs"])
    for p in pairs:
        p["attn"] = mm_nt(p["q"], p["k"]) * p["decay"]
    out_rows = []
    for c in range(DN_GROUP):
        grp = pairs[c * NH:(c + 1) * NH]
        ws = [mm(jnp.concatenate([p["uw"][:, HD:], p["qe"]], axis=0), s[h]) for h, p in enumerate(grp)]
        v_new = [p["uw"][:, :HD] - w_[:CH, :] for p, w_ in zip(grp, ws)]
        o = [w_[CH:, :] + mm(p["attn"], vn) for p, w_, vn in zip(grp, ws, v_new)]
        s = [s[h] * p["sdec"] + mm_tn(p["kd"], vn) for h, (p, vn) in enumerate(zip(grp, v_new))]
        o = [oh * lax.rsqrt(jnp.mean(oh * oh, axis=-1, keepdims=True) + EPS) * ng * _silu(p["z"]) for oh, p in zip(o, grp)]
        out_rows.append(jnp.concatenate(o, axis=1))
    return (jnp.concatenate(out_rows, axis=0), *s)


def _dn_core_fwd_call(proj, q, k, v, gb, ng):
    L = q.shape[0]
    rows = DN_GROUP * CH
    ng_ = L // rows

    def body(q_r, k_r, v_r, gb_r, z_r, ng_r, yd_ref, ssave_ref, s_scr):
        @pl.when(pl.program_id(0) == 0)
        def _():
            s_scr[...] = jnp.zeros_like(s_scr)

        ssave_ref[0] = s_scr[...]
        yd, *s2 = _dn_group(q_r[...], k_r[...], v_r[...], gb_r[...], z_r[...], ng_r[...], *[s_scr[h] for h in range(NH)])
        yd_ref[...] = yd
        for h in range(NH):
            s_scr[h] = s2[h]

    c = pl.BlockSpec((rows, BR), lambda i: (i, 0))
    return pl.pallas_call(
        body, name="dn_core_fwd", grid=(ng_,),
        in_specs=[c, c, c, pl.BlockSpec((rows, 128), lambda i: (i, 0)), pl.BlockSpec((rows, BR), lambda i: (i, 12)),
                  pl.BlockSpec((1, HD), lambda i: (0, 0))],
        out_specs=[c, pl.BlockSpec((1, NH, HD, HD), lambda i: (i, 0, 0, 0))],
        out_shape=[SDS((L, BR), F32), SDS((ng_, NH, HD, HD), F32)],
        scratch_shapes=[pltpu.VMEM((NH, HD, HD), F32)],
        compiler_params=_cparams(("arbitrary",)),
    )(q, k, v, gb, proj, ng)


def _dn_core_bwd_call(proj, q, k, v, gb, ng, ssave, dyd):
    L = q.shape[0]
    rows = DN_GROUP * CH
    ng_ = L // rows

    def body(q_r, k_r, v_r, gb_r, z_r, ng_r, s_r, dy_r, dq_ref, dk_ref, dv_ref, dgb_ref, dz_ref, g_ng, ds_scr):
        @pl.when(pl.program_id(0) == 0)
        def _():
            ds_scr[...] = jnp.zeros_like(ds_scr)
            g_ng[...] = jnp.zeros_like(g_ng)

        _, vjp = jax.vjp(_dn_group, q_r[...], k_r[...], v_r[...], gb_r[...], z_r[...], ng_r[...],
                         *[s_r[0, h] for h in range(NH)])
        dq, dk, dv, dgb, dz, dng, *ds = vjp((dy_r[...], *[ds_scr[h] for h in range(NH)]))
        dq_ref[...], dk_ref[...], dv_ref[...], dgb_ref[...] = dq, dk, dv, dgb
        dz_ref[...] = dz.astype(BF16)
        g_ng[...] += dng
        for h in range(NH):
            ds_scr[h] = ds[h]

    c = pl.BlockSpec((rows, BR), lambda i: (ng_ - 1 - i, 0))
    c128 = pl.BlockSpec((rows, 128), lambda i: (ng_ - 1 - i, 0))
    return pl.pallas_call(
        body, name="dn_core_bwd", grid=(ng_,),
        in_specs=[c, c, c, c128, pl.BlockSpec((rows, BR), lambda i: (ng_ - 1 - i, 12)),
                  pl.BlockSpec((1, HD), lambda i: (0, 0)),
                  pl.BlockSpec((1, NH, HD, HD), lambda i: (ng_ - 1 - i, 0, 0, 0)), c],
        out_specs=[c, c, c, c128, c, pl.BlockSpec((1, HD), lambda i: (0, 0))],
        out_shape=[SDS((L, BR), F32)] * 3 + [SDS((L, 128), F32), SDS((L, BR), BF16), SDS((1, HD), F32)],
        scratch_shapes=[pltpu.VMEM((NH, HD, HD), F32)],
        compiler_params=_cparams(("arbitrary",)),
    )(q, k, v, gb, proj, ng, ssave, dyd)


def _outproj_bwd_call(dx, ys, w):
    L = dx.shape[0]

    def body(dx_ref, a_ref, b_ref, c_ref, d_ref, w_ref, da, db, dc, dd, dw_ref):
        @pl.when(pl.program_id(0) == 0)
        def _():
            dw_ref[...] = jnp.zeros_like(dw_ref)

        dxb = dx_ref[...].astype(BF16)
        for b, (y_ref, o_ref) in enumerate(zip((a_ref, b_ref, c_ref, d_ref), (da, db, dc, dd))):
            o_ref[...] = _dot(dxb, w_ref[b * BR:(b + 1) * BR, :], NT)
            dw_ref[b * BR:(b + 1) * BR, :] += _dot(y_ref[...].astype(BF16), dxb, TN)

    yspec = pl.BlockSpec((TL, BR), lambda i: (i, 0))
    return pl.pallas_call(
        body, name="outproj_bwd", grid=(L // TL,),
        in_specs=[pl.BlockSpec((TL, D), lambda i: (i, 0)), yspec, yspec, yspec, yspec,
                  pl.BlockSpec((D, D), lambda i: (0, 0))],
        out_specs=[yspec] * 4 + [pl.BlockSpec((D, D), lambda i: (0, 0))],
        out_shape=[SDS((L, BR), F32)] * 4 + [SDS((D, D), F32)],
        compiler_params=_cparams(("arbitrary",)),
    )(dx, *ys, w)


def _slab_cols(slabs):
    widths = [s.shape[1] for s in slabs]
    starts = [sum(widths[:i]) for i in range(len(widths))]
    assert starts[-1] + widths[-1] == PW
    return list(zip(starts, widths))


def _inproj_bwd_x_call(slabs, w, x, g, dx_next):
    L = x.shape[0]
    cols = _slab_cols(slabs)
    n = len(slabs)

    def body(*refs):
        dp_refs, (w_ref, x_ref, g_ref, dxn_ref, dx_ref, dg_ref) = refs[:n], refs[n:]

        @pl.when(pl.program_id(0) == 0)
        def _():
            dg_ref[...] = jnp.zeros_like(dg_ref)

        dh = None
        for dp_ref, (c0, cw) in zip(dp_refs, cols):
            part = _dot(dp_ref[...], w_ref[:, c0:c0 + cw], NT)
            dh = part if dh is None else dh + part
        _, vjp = jax.vjp(_rms, x_ref[...], g_ref[...])
        dx, dg = vjp(dh)
        dx_ref[...] = dx + dxn_ref[...]
        dg_ref[...] += dg

    row = lambda w_: pl.BlockSpec((TL, w_), lambda i: (i, 0))
    return pl.pallas_call(
        body, name="inproj_bwd_x", grid=(L // TL,),
        in_specs=[row(cw) for _, cw in cols]
        + [pl.BlockSpec((D, PW), lambda i: (0, 0)), row(D), pl.BlockSpec((1, D), lambda i: (0, 0)), row(D)],
        out_specs=[row(D), pl.BlockSpec((1, D), lambda i: (0, 0))],
        out_shape=[SDS((L, D), F32), SDS((1, D), F32)],
        compiler_params=_cparams(("arbitrary",)),
    )(*slabs, w, x, g, dx_next)


def _inproj_bwd_w_call(h, slabs):
    L = h.shape[0]
    cols = _slab_cols(slabs)
    n = len(slabs)

    def body(*refs):
        h_ref, dp_refs, dw_ref = refs[0], refs[1:1 + n], refs[1 + n]

        @pl.when(pl.program_id(0) == 0)
        def _():
            dw_ref[...] = jnp.zeros_like(dw_ref)

        hv = h_ref[...]
        for dp_ref, (c0, cw) in zip(dp_refs, cols):
            dw_ref[:, c0:c0 + cw] += _dot(hv, dp_ref[...], TN)

    row = lambda w_: pl.BlockSpec((TL, w_), lambda i: (i, 0))
    return pl.pallas_call(
        body, name="inproj_bwd_w", grid=(L // TL,),
        in_specs=[row(D)] + [row(cw) for _, cw in cols],
        out_specs=pl.BlockSpec((D, PW), lambda i: (0, 0)),
        out_shape=SDS((D, PW), F32),
        compiler_params=_cparams(("arbitrary",)),
    )(h, *slabs)


def _exchange_call(name, flows):
    n = len(flows)

    def body(*refs):
        srcs, dsts = refs[:n], refs[n:2 * n]
        send_sems, recv_sems, local_sems = refs[2 * n:]
        x, y, c = lax.axis_index("x"), lax.axis_index("y"), lax.axis_index("c")
        me = 4 * x + 2 * y + c
        copies = []
        for mask in range(1, N_DEV):
            px = 1 - x if mask & 4 else x
            py = 1 - y if mask & 2 else y
            pc = 1 - c if mask & 1 else c
            for f, (_, src_at, _, dst_at) in enumerate(flows):
                cp = pltpu.make_async_remote_copy(
                    src_ref=src_at(srcs[f], 4 * px + 2 * py + pc), dst_ref=dst_at(dsts[f], me),
                    send_sem=send_sems.at[mask - 1, f], recv_sem=recv_sems.at[mask - 1, f],
                    device_id=(px, py, pc), device_id_type=pl.DeviceIdType.MESH)
                cp.start()
                copies.append(cp)
        mine = [pltpu.make_async_copy(src_at(srcs[f], me), dst_at(dsts[f], me), local_sems.at[f])
                for f, (_, src_at, _, dst_at) in enumerate(flows)]
        for cp in mine:
            cp.start()
        for cp in copies + mine:
            cp.wait()

    return pl.pallas_call(
        body, name=name,
        in_specs=[pl.BlockSpec(memory_space=pl.ANY)] * n,
        out_specs=[pl.BlockSpec(memory_space=pl.ANY)] * n,
        out_shape=[SDS(tuple(shape), src.dtype) for src, _, shape, _ in flows],
        scratch_shapes=[pltpu.SemaphoreType.DMA((N_DEV - 1, n)), pltpu.SemaphoreType.DMA((N_DEV - 1, n)),
                        pltpu.SemaphoreType.DMA((n,))],
    )(*[f[0] for f in flows])


def _whole(ref, _):
    return ref


def _slot(ref, k):
    return ref.at[k]


def _reduce_adamw_call(parts, w, m, v, block, name):
    nsrc = parts.shape[0]
    grid = tuple(s // b for s, b in zip(w.shape, block))
    c1 = 1.0 - ADAM_B1 ** ADAM_STEP
    c2 = 1.0 - ADAM_B2 ** ADAM_STEP

    def body(p_ref, w_ref, m_ref, v_ref, g_ref, d_ref, nm_ref, nv_ref):
        g = p_ref[0]
        for k in range(1, nsrc):
            g = g + p_ref[k]
        nm = ADAM_B1 * m_ref[...] + (1.0 - ADAM_B1) * g
        nv = ADAM_B2 * v_ref[...] + (1.0 - ADAM_B2) * (g * g)
        g_ref[...] = g
        nm_ref[...] = nm
        nv_ref[...] = nv
        d_ref[...] = -ADAM_LR * ((nm / c1) / (jnp.sqrt(nv / c2) + ADAM_EPS) + ADAM_WD * w_ref[...])

    own = pl.BlockSpec(tuple(block), lambda *i: i)
    return pl.pallas_call(
        body, name=name, grid=grid,
        in_specs=[pl.BlockSpec((nsrc,) + tuple(block), lambda *i: (0,) + i), own, own, own],
        out_specs=[own] * 4,
        out_shape=[SDS(w.shape, F32)] * 4,
        compiler_params=_cparams(("parallel",) * len(grid)),
    )(parts, w, m, v)


RELAYOUT_ROWS = 256
SHARD_COLS = IN_COLS // N_DEV


def _win_gather_layout_call(shards, layer):
    def body(w_ref, o_ref):
        nat = jnp.concatenate([w_ref[k, 0].astype(F32) for k in range(N_DEV)], axis=1)
        out = jnp.concatenate([nat[:, :3072], nat[:, 3080:], nat[:, 3072:3080],
                               jnp.zeros((RELAYOUT_ROWS, PW - IN_COLS), F32)], axis=1)
        o_ref[...] = out.astype(BF16)

    return pl.pallas_call(
        body, name="w_in_layout", grid=(D // RELAYOUT_ROWS,),
        in_specs=[pl.BlockSpec((N_DEV, 1, RELAYOUT_ROWS, SHARD_COLS), lambda i: (0, layer, i, 0))],
        out_specs=pl.BlockSpec((RELAYOUT_ROWS, PW), lambda i: (i, 0)),
        out_shape=SDS((D, PW), BF16),
        compiler_params=_cparams(("parallel",)),
    )(shards)


def _win_scatter_layout_call(grads):
    def body(*refs):
        g_refs, o_ref = refs[:DEPTH], refs[DEPTH]
        for l in range(DEPTH):
            @pl.when(pl.program_id(0) == l)
            def _(l=l):
                g = g_refs[l][...]
                nat = jnp.concatenate([g[:, :3072], g[:, AB_COL:AB_COL + 8], g[:, 3072:AB_COL]], axis=1)
                for k in range(N_DEV):
                    o_ref[k, 0] = nat[:, SHARD_COLS * k:SHARD_COLS * (k + 1)]

    def in_spec(l):
        return pl.BlockSpec((RELAYOUT_ROWS, PW), lambda m, i: (jnp.where(m == l, i, 0), 0))

    return pl.pallas_call(
        body, name="w_in_grad_layout", grid=(DEPTH, D // RELAYOUT_ROWS),
        in_specs=[in_spec(l) for l in range(DEPTH)],
        out_specs=pl.BlockSpec((N_DEV, 1, RELAYOUT_ROWS, SHARD_COLS), lambda m, i: (0, m, i, 0)),
        out_shape=SDS((N_DEV, DEPTH, D, SHARD_COLS), F32),
        compiler_params=_cparams(("arbitrary", "arbitrary")),
    )(*grads)


_BIG = ("w_in", "w_out", "a_pw_w", "s5_glu_w")
_CONV = ("a_conv_w", "c_conv_w", "d_conv_w")
_CONV_TAPS = {"a_conv_w": KA, "c_conv_w": KC, "d_conv_w": KD}
_CONV_ROWS = {"a_conv_w": HALO, "c_conv_w": HALO_S, "d_conv_w": HALO_S}
_CONV_WIDTH = {"a_conv_w": BR, "c_conv_w": BR, "d_conv_w": 3 * BR}
_REPLICATED = ("norm_g", "a_conv_b", "a_ln_g", "a_ln_b", "a_pw_b", "s5_lambda_re", "s5_lambda_im", "s5_b_re", "s5_b_im",
               "s5_c_re", "s5_c_im", "s5_d", "s5_log_dt", "s5_glu_b", "d_a_log", "d_dt_bias", "d_norm_g", "final_g")
_WEIGHTS = ("norm_g", "w_in", "a_conv_w", "a_conv_b", "a_ln_g", "a_ln_b", "a_pw_w", "a_pw_b", "s5_lambda_re",
            "s5_lambda_im", "s5_b_re", "s5_b_im", "s5_c_re", "s5_c_im", "s5_d", "s5_log_dt", "s5_glu_w", "s5_glu_b",
            "c_conv_w", "d_conv_w", "d_a_log", "d_dt_bias", "d_norm_g", "w_out", "final_g")


def _size(shape):
    n = 1
    for s in shape:
        n *= s
    return n


def _pack_rows(pieces, row_mult):
    flat = jnp.concatenate([p.reshape(-1) for p in pieces])
    per = 128 * row_mult
    pad = (-flat.shape[0]) % per
    return jnp.pad(flat, (0, pad)).reshape(-1, 128)


def _unpack(flat, shapes):
    out, off = [], 0
    for s in shapes:
        n = _size(s)
        out.append(flat[off:off + n].reshape(s))
        off += n
    return out


def _rows_of_peer(rows):
    return lambda ref, k: ref.at[:, pl.ds(k * rows, rows), :]


def _row_slot(ref, k):
    return ref.at[:, k]


def _gather_weights(weights):
    bf = {n: weights[n].astype(BF16) for n in _BIG}
    conv = _pack_rows([weights[n] for n in _CONV], 8)
    flows = [(bf["w_in"], _whole, (N_DEV,) + bf["w_in"].shape, _slot),
             (bf["w_out"], _whole, (DEPTH, N_DEV, D // N_DEV, D), _row_slot),
             (bf["a_pw_w"], _whole, (DEPTH, N_DEV, BR // N_DEV, BR), _row_slot),
             (bf["s5_glu_w"], _whole, (DEPTH, N_DEV, BR // N_DEV, BR), _row_slot),
             (conv, _whole, (N_DEV,) + conv.shape, _slot)]
    w_in, w_out, a_pw, glu, conv_all = _exchange_call("gather_weights", flows)
    full = {"w_in": [_win_gather_layout_call(w_in, l) for l in range(DEPTH)],
            "w_out": w_out.reshape(DEPTH, D, D), "a_pw_w": a_pw.reshape(DEPTH, BR, BR),
            "s5_glu_w": glu.reshape(DEPTH, BR, BR)}
    shapes = [weights[n].shape for n in _CONV]
    per_dev = [_unpack(conv_all[k].reshape(-1), shapes) for k in range(N_DEV)]
    for i, n in enumerate(_CONV):
        whole = jnp.concatenate([per_dev[k][i] for k in range(N_DEV)], axis=-1)
        full[n] = jnp.pad(whole, ((0, 0), (0, _CONV_ROWS[n] - _CONV_TAPS[n]), (0, 0)))
    return full


def _s5_inputs(p):
    return {"lam_re": p["s5_lambda_re"].reshape(1, NSTATE), "lam_im": p["s5_lambda_im"].reshape(1, NSTATE),
            "log_dt": p["s5_log_dt"].reshape(1, 16), "b_re": p["s5_b_re"].reshape(NSTATE, 16),
            "b_im": p["s5_b_im"].reshape(NSTATE, 16), "c_re": p["s5_c_re"].reshape(BR, 64),
            "c_im": p["s5_c_im"].reshape(BR, 64)}


def _row(a, width=None):
    a = a.reshape(1, -1)
    return a if width is None else jnp.pad(a, ((0, 0), (0, width - a.shape[1])))


def _layer_params(p):
    q = dict(p)
    for n in ("norm_g", "a_conv_b", "a_ln_g", "a_ln_b", "a_pw_b", "s5_d", "s5_glu_b", "d_norm_g"):
        q[n] = _row(p[n])
    q["d_a_log"] = _row(p["d_a_log"], 128)
    q["d_dt_bias"] = _row(p["d_dt_bias"], 128)
    return q


def _layer_fwd(x, p):
    q = _layer_params(p)
    proj, h = _inproj_call(x, q["norm_g"], q["w_in"])
    ya, yc = _ac_fwd_call(proj, q)
    prep = _s5_prep_call(_s5_inputs(p))
    yb, cin_r, cin_i = _s5_fwd_call(proj, prep, q)
    dq, dk, dv, dgb = _dn_pre_fwd_call(proj, q)
    yd, ssave = _dn_core_fwd_call(proj, dq, dk, dv, dgb, q["d_norm_g"])
    x_next = _outproj_call(x, (ya, yb, yc, yd), q["w_out"])
    saved = dict(x=x, proj=proj, h=h, ya=ya, yb=yb, yc=yc, yd=yd, cin_r=cin_r, cin_i=cin_i,
                 q=dq, k=dk, v=dv, gb=dgb, ssave=ssave)
    return x_next, saved


def _layer_bwd(dx, p, sv):
    q = _layer_params(p)
    proj = sv["proj"]
    dya, dyb, dyc, dyd, g_wout = _outproj_bwd_call(dx, (sv["ya"], sv["yb"], sv["yc"], sv["yd"]), q["w_out"])
    dpa, dpc, g_acw, g_acb, g_alg, g_alb, g_apw, g_apb, g_ccw = _ac_bwd_call(proj, q, dya, dyc)
    sp = _s5_inputs(p)
    prep = _s5_prep_call(sp)
    dpb, *s5g = _s5_bwd_call(proj, prep, q, sv["cin_r"], sv["cin_i"], dyb)
    g_sd, g_gw, g_gb = s5g[6:]
    g_lre, g_lim, g_ldt, g_bre, g_bim, g_cre, g_cim = _s5_prep_bwd_call(sp, s5g[:6])
    dq, dk, dv, dgb, dz, g_ng = _dn_core_bwd_call(proj, sv["q"], sv["k"], sv["v"], sv["gb"], q["d_norm_g"], sv["ssave"], dyd)
    dqkv, dab, g_dcw, g_alog, g_dtb = _dn_pre_bwd_call(proj, q, dq, dk, dv, dgb)
    slabs = (dpa, dpb, dpc, dqkv, dz, dab)
    g_win = _inproj_bwd_w_call(sv["h"], slabs)
    dx_prev, g_ng0 = _inproj_bwd_x_call(slabs, q["w_in"], sv["x"], q["norm_g"], dx)
    grads = {"norm_g": g_ng0.reshape(D), "w_in": g_win, "a_conv_w": g_acw, "a_conv_b": g_acb.reshape(BR),
             "a_ln_g": g_alg.reshape(BR), "a_ln_b": g_alb.reshape(BR), "a_pw_w": g_apw, "a_pw_b": g_apb.reshape(BR),
             "s5_lambda_re": g_lre.reshape(16, 64), "s5_lambda_im": g_lim.reshape(16, 64),
             "s5_b_re": g_bre.reshape(16, 64, 16), "s5_b_im": g_bim.reshape(16, 64, 16),
             "s5_c_re": g_cre.reshape(16, 16, 64), "s5_c_im": g_cim.reshape(16, 16, 64),
             "s5_d": g_sd.reshape(BR), "s5_log_dt": g_ldt.reshape(16), "s5_glu_w": g_gw, "s5_glu_b": g_gb.reshape(BR),
             "c_conv_w": g_ccw, "d_conv_w": g_dcw, "d_a_log": g_alog[0, :NH], "d_dt_bias": g_dtb[0, :NH],
             "d_norm_g": g_ng.reshape(HD), "w_out": g_wout}
    return dx_prev, grads


def _step(x, target, weights, moments_m, moments_v):
    full = _gather_weights(weights)
    layer_names = [n for n in _WEIGHTS if n != "final_g"]
    layers = [{n: (full[n][l] if n in full else weights[n][l]) for n in layer_names} for l in range(DEPTH)]
    saved, x_out = [], x
    for p in layers:
        x_out, sv = _layer_fwd(x_out, p)
        saved.append(sv)
    dx0, g_final, loss_part = _loss_call(x_out, _row(weights["final_g"]), target)
    per_layer = [None] * DEPTH
    for l in range(DEPTH - 1, -1, -1):
        dx0, per_layer[l] = _layer_bwd(dx0, layers[l], saved[l])
    stack = lambda n: jnp.stack([g[n] for g in per_layer])
    loss = lax.psum(loss_part[0, 0], ("x", "y", "c"))
    results = {}

    g_win = _win_scatter_layout_call([g["w_in"] for g in per_layer])
    flows = [(g_win, _slot, g_win.shape, _slot),
             (stack("w_out"), _rows_of_peer(D // N_DEV), (N_DEV, DEPTH, D // N_DEV, D), _slot),
             (stack("a_pw_w"), _rows_of_peer(BR // N_DEV), (N_DEV, DEPTH, BR // N_DEV, BR), _slot),
             (stack("s5_glu_w"), _rows_of_peer(BR // N_DEV), (N_DEV, DEPTH, BR // N_DEV, BR), _slot)]
    parts = _exchange_call("scatter_grads", flows)
    blocks = {"w_in": (1, RELAYOUT_ROWS, SHARD_COLS), "w_out": (1, D // N_DEV, D),
              "a_pw_w": (DEPTH, BR // N_DEV, BR), "s5_glu_w": (DEPTH, BR // N_DEV, BR)}
    for n, part in zip(_BIG, parts):
        results[n] = _reduce_adamw_call(part, weights[n], moments_m[n], moments_v[n], blocks[n], "adamw_" + n)

    grads = {n: stack(n) for n in _REPLICATED if n != "final_g"}
    grads["final_g"] = g_final.reshape(D)
    conv_g = [stack(n) for n in _CONV]
    pack = lambda d: _pack_rows([d[n] for n in _REPLICATED] + [jnp.zeros_like(c) for c in conv_g], 512)
    packed_g = _pack_rows([grads[n] for n in _REPLICATED] + conv_g, 512)
    gathered, = _exchange_call("gather_small_grads", [(packed_g, _whole, (N_DEV,) + packed_g.shape, _slot)])
    res = _reduce_adamw_call(gathered, pack(weights), pack(moments_m), pack(moments_v), (512, 128), "adamw_replicated")
    shapes = [weights[n].shape for n in _REPLICATED] + [c.shape for c in conv_g]
    res = [_unpack(r.reshape(-1), shapes) for r in res]
    for i, n in enumerate(_REPLICATED):
        results[n] = tuple(r[i] for r in res)

    me = 4 * lax.axis_index("x") + 2 * lax.axis_index("y") + lax.axis_index("c")
    own_g = []
    for i, n in enumerate(_CONV):
        width = _CONV_WIDTH[n] // N_DEV
        summed = res[0][len(_REPLICATED) + i][:, :_CONV_TAPS[n], :]
        own_g.append(lax.dynamic_slice_in_dim(summed, me * width, width, axis=2))
    packc = lambda arrs: _pack_rows(arrs, 8)
    res = _reduce_adamw_call(packc(own_g)[None], packc([weights[n] for n in _CONV]), packc([moments_m[n] for n in _CONV]),
                             packc([moments_v[n] for n in _CONV]), packc(own_g).shape, "adamw_conv")
    res = [_unpack(r.reshape(-1), [weights[n].shape for n in _CONV]) for r in res]
    for i, n in enumerate(_CONV):
        results[n] = tuple(r[i] for r in res)

    outs = [loss, dx0]
    for kind in range(4):
        outs += [results[n][kind] for n in _WEIGHTS]
    return tuple(outs)


def kernel(x, norm_g, w_in, a_conv_w, a_conv_b, a_ln_g, a_ln_b, a_pw_w, a_pw_b, s5_lambda_re, s5_lambda_im, s5_b_re, s5_b_im, s5_c_re, s5_c_im, s5_d, s5_log_dt, s5_glu_w, s5_glu_b, c_conv_w, d_conv_w, d_a_log, d_dt_bias, d_norm_g, w_out, final_g, loss_target, m_norm_g, m_w_in, m_a_conv_w, m_a_conv_b, m_a_ln_g, m_a_ln_b, m_a_pw_w, m_a_pw_b, m_s5_lambda_re, m_s5_lambda_im, m_s5_b_re, m_s5_b_im, m_s5_c_re, m_s5_c_im, m_s5_d, m_s5_log_dt, m_s5_glu_w, m_s5_glu_b, m_c_conv_w, m_d_conv_w, m_d_a_log, m_d_dt_bias, m_d_norm_g, m_w_out, m_final_g, v_norm_g, v_w_in, v_a_conv_w, v_a_conv_b, v_a_ln_g, v_a_ln_b, v_a_pw_w, v_a_pw_b, v_s5_lambda_re, v_s5_lambda_im, v_s5_b_re, v_s5_b_im, v_s5_c_re, v_s5_c_im, v_s5_d, v_s5_log_dt, v_s5_glu_w, v_s5_glu_b, v_c_conv_w, v_d_conv_w, v_d_a_log, v_d_dt_bias, v_d_norm_g, v_w_out, v_final_g):
    weights = dict(norm_g=norm_g, w_in=w_in, a_conv_w=a_conv_w, a_conv_b=a_conv_b, a_ln_g=a_ln_g, a_ln_b=a_ln_b, a_pw_w=a_pw_w, a_pw_b=a_pw_b, s5_lambda_re=s5_lambda_re, s5_lambda_im=s5_lambda_im, s5_b_re=s5_b_re, s5_b_im=s5_b_im, s5_c_re=s5_c_re, s5_c_im=s5_c_im, s5_d=s5_d, s5_log_dt=s5_log_dt, s5_glu_w=s5_glu_w, s5_glu_b=s5_glu_b, c_conv_w=c_conv_w, d_conv_w=d_conv_w, d_a_log=d_a_log, d_dt_bias=d_dt_bias, d_norm_g=d_norm_g, w_out=w_out, final_g=final_g)
    mom_m = dict(norm_g=m_norm_g, w_in=m_w_in, a_conv_w=m_a_conv_w, a_conv_b=m_a_conv_b, a_ln_g=m_a_ln_g, a_ln_b=m_a_ln_b, a_pw_w=m_a_pw_w, a_pw_b=m_a_pw_b, s5_lambda_re=m_s5_lambda_re, s5_lambda_im=m_s5_lambda_im, s5_b_re=m_s5_b_re, s5_b_im=m_s5_b_im, s5_c_re=m_s5_c_re, s5_c_im=m_s5_c_im, s5_d=m_s5_d, s5_log_dt=m_s5_log_dt, s5_glu_w=m_s5_glu_w, s5_glu_b=m_s5_glu_b, c_conv_w=m_c_conv_w, d_conv_w=m_d_conv_w, d_a_log=m_d_a_log, d_dt_bias=m_d_dt_bias, d_norm_g=m_d_norm_g, w_out=m_w_out, final_g=m_final_g)
    mom_v = dict(norm_g=v_norm_g, w_in=v_w_in, a_conv_w=v_a_conv_w, a_conv_b=v_a_conv_b, a_ln_g=v_a_ln_g, a_ln_b=v_a_ln_b, a_pw_w=v_a_pw_w, a_pw_b=v_a_pw_b, s5_lambda_re=v_s5_lambda_re, s5_lambda_im=v_s5_lambda_im, s5_b_re=v_s5_b_re, s5_b_im=v_s5_b_im, s5_c_re=v_s5_c_re, s5_c_im=v_s5_c_im, s5_d=v_s5_d, s5_log_dt=v_s5_log_dt, s5_glu_w=v_s5_glu_w, s5_glu_b=v_s5_glu_b, c_conv_w=v_c_conv_w, d_conv_w=v_d_conv_w, d_a_log=v_d_a_log, d_dt_bias=v_d_dt_bias, d_norm_g=v_d_norm_g, w_out=v_w_out, final_g=v_final_g)
    outs = _step(x[0], loss_target[0], weights, mom_m, mom_v)
    return (outs[0], outs[1][None]) + outs[2:]
```

```python
import functools

import jax
import jax.numpy as jnp
from jax import lax
from jax.experimental import pallas as pl
from jax.experimental.pallas import tpu as pltpu

F32 = jnp.float32
BF16 = jnp.bfloat16
HI = lax.Precision.HIGHEST
SDS = jax.ShapeDtypeStruct

N_DEV = 8
D = 1024
BR = 256
DEPTH = 4
IN_COLS = 3336
PW = 3456
AB_COL = 3328
EPS = 1e-6
TL = 512
SEG = TL // 8
HALO = 32
HALO_S = 8
KA, KC, KD = 31, 3, 4
CH = 64
DN_GROUP = 4
NH, HD = 4, 64
NSTATE = 1024
VMEM_LIMIT = 56 * 1024 * 1024

ADAM_LR, ADAM_B1, ADAM_B2, ADAM_EPS, ADAM_WD, ADAM_STEP = 0.001, 0.9, 0.999, 1e-08, 0.01, 10

NN = ((1,), (0,))
NT = ((1,), (1,))
TN = ((0,), (0,))


def _dot(a, b, dims, prec=None):
    return lax.dot_general(a, b, (dims, ((), ())), precision=prec, preferred_element_type=F32)


def _make_mm(cast, prec, fwd_dims):
    def prep(t):
        return t.astype(cast) if cast is not None else t

    @jax.custom_vjp
    def mm(a, w):
        return _dot(prep(a), prep(w), fwd_dims, prec)

    def fwd(a, w):
        return mm(a, w), (a, w)

    def bwd(res, dy):
        a, w = res
        a, w, dy = prep(a), prep(w), prep(dy)
        if fwd_dims == NN:
            return _dot(dy, w, NT, prec), _dot(a, dy, TN, prec)
        if fwd_dims == NT:
            return _dot(dy, w, NN, prec), _dot(dy, a, TN, prec)
        return _dot(w, dy, NT, prec), _dot(a, dy, NN, prec)

    mm.defvjp(fwd, bwd)
    return mm


mm = _make_mm(BF16, None, NN)
mm_nt = _make_mm(BF16, None, NT)
mm_tn = _make_mm(BF16, None, TN)
mmh = _make_mm(None, HI, NN)
mmh_nt = _make_mm(None, HI, NT)


def _sigmoid(x):
    return jax.nn.sigmoid(x)


def _silu(x):
    return x * jax.nn.sigmoid(x)


def _gelu(x):
    return 0.5 * x * (1.0 + jnp.tanh(0.7978845608028654 * (x + 0.044715 * (x * x * x))))


def _softplus(x):
    return jnp.maximum(x, 0.0) + jnp.log1p(jnp.exp(-jnp.abs(x)))


def _rms(x, g):
    return x * lax.rsqrt(jnp.mean(x * x, axis=-1, keepdims=True) + EPS) * g


def _cparams(sem):
    return pltpu.CompilerParams(dimension_semantics=sem, vmem_limit_bytes=VMEM_LIMIT)


def _tap_offsets(halo, taps):
    return [halo - (taps - 1) + k for k in range(taps)]


def _conv_fwd_impl(acat, w, tile, halo, taps):
    n = tile + halo
    out = None
    for k, off in enumerate(_tap_offsets(halo, taps)):
        src = jnp.roll(acat, n - off, axis=0)[:tile, :] if off != halo else acat[halo:, :]
        term = src * w[k:k + 1, :]
        out = term if out is None else out + term
    return out


def _make_conv(tile, halo, taps):
    @jax.custom_vjp
    def conv(acat, w):
        return _conv_fwd_impl(acat, w, tile, halo, taps)

    def fwd(acat, w):
        return conv(acat, w), (acat, w)

    def bwd(res, dy):
        acat, w = res
        n = tile + halo
        dyp = jnp.concatenate([dy, jnp.zeros((halo, dy.shape[1]), F32)], axis=0)
        rows = lax.broadcasted_iota(jnp.int32, w.shape, 0)
        dacat = None
        dw = jnp.zeros(w.shape, F32)
        for k, off in enumerate(_tap_offsets(halo, taps)):
            term = jnp.roll(dyp, off, axis=0) * w[k:k + 1, :]
            dacat = term if dacat is None else dacat + term
            src = jnp.roll(acat, n - off, axis=0)[:tile, :] if off != halo else acat[halo:, :]
            dw = dw + jnp.where(rows == k, jnp.sum(dy * src, axis=0, keepdims=True), 0.0)
        return dacat, dw

    conv.defvjp(fwd, bwd)
    return conv


def _halo_spec(tile, halo, width, col):
    per = tile // halo
    return pl.BlockSpec((halo, width), lambda i: (jnp.maximum(i * per - 1, 0), col))


def _halo_spec_rev(nt, tile, halo, width, col):
    per = tile // halo
    return pl.BlockSpec((halo, width), lambda i: (jnp.maximum((nt - 1 - i) * per - 1, 0), col))


def _inproj_call(x, g, w):
    L = x.shape[0]
    tn = 1152

    def body(x_ref, g_ref, w_ref, p_ref, h_ref):
        h = _rms(x_ref[...], g_ref[...]).astype(BF16)
        h_ref[...] = h
        p_ref[...] = _dot(h, w_ref[...], NN)

    return pl.pallas_call(
        body, name="inproj", grid=(L // TL, PW // tn),
        in_specs=[pl.BlockSpec((TL, D), lambda i, j: (i, 0)), pl.BlockSpec((1, D), lambda i, j: (0, 0)),
                  pl.BlockSpec((D, tn), lambda i, j: (0, j))],
        out_specs=[pl.BlockSpec((TL, tn), lambda i, j: (i, j)), pl.BlockSpec((TL, D), lambda i, j: (i, 0))],
        out_shape=[SDS((L, PW), F32), SDS((L, D), BF16)],
        compiler_params=_cparams(("parallel", "arbitrary")),
    )(x, g, w)


def _outproj_call(x, ys, w):
    L = x.shape[0]

    def body(x_ref, a_ref, b_ref, c_ref, d_ref, w_ref, o_ref):
        acc = x_ref[...]
        for b, y_ref in enumerate((a_ref, b_ref, c_ref, d_ref)):
            acc = acc + _dot(y_ref[...].astype(BF16), w_ref[b * BR:(b + 1) * BR, :], NN)
        o_ref[...] = acc

    yspec = pl.BlockSpec((TL, BR), lambda i: (i, 0))
    return pl.pallas_call(
        body, name="outproj", grid=(L // TL,),
        in_specs=[pl.BlockSpec((TL, D), lambda i: (i, 0)), yspec, yspec, yspec, yspec,
                  pl.BlockSpec((D, D), lambda i: (0, 0))],
        out_specs=pl.BlockSpec((TL, D), lambda i: (i, 0)),
        out_shape=SDS((L, D), F32),
        compiler_params=_cparams(("parallel",)),
    )(x, *ys, w)


def _loss_call(x, g, target):
    L = x.shape[0]

    def body(x_ref, g_ref, t_ref, dx_ref, dg_ref, loss_ref):
        @pl.when(pl.program_id(0) == 0)
        def _():
            dg_ref[...] = jnp.zeros_like(dg_ref)
            loss_ref[...] = jnp.zeros_like(loss_ref)

        y, vjp = jax.vjp(_rms, x_ref[...], g_ref[...])
        err = y - t_ref[...]
        dx, dg = vjp(err * (1.0 / D))
        dx_ref[...] = dx
        dg_ref[...] += dg
        tot = jnp.sum(jnp.sum(err * err, axis=1, keepdims=True), axis=0, keepdims=True)
        loss_ref[...] += jnp.broadcast_to(tot * (0.5 / D), loss_ref.shape)

    return pl.pallas_call(
        body, name="loss_head", grid=(L // TL,),
        in_specs=[pl.BlockSpec((TL, D), lambda i: (i, 0)), pl.BlockSpec((1, D), lambda i: (0, 0)),
                  pl.BlockSpec((TL, D), lambda i: (i, 0))],
        out_specs=[pl.BlockSpec((TL, D), lambda i: (i, 0)), pl.BlockSpec((1, D), lambda i: (0, 0)),
                   pl.BlockSpec((1, 128), lambda i: (0, 0))],
        out_shape=[SDS((L, D), F32), SDS((1, D), F32), SDS((1, 128), F32)],
        compiler_params=_cparams(("arbitrary",)),
    )(x, g, target)


def _branch_a(valw, gatew, z, cw, cb, lg, lb, pw, pb, conv):
    a = conv(valw * _sigmoid(gatew), cw) + cb
    mu = jnp.mean(a, axis=-1, keepdims=True)
    xc = a - mu
    y = xc * lax.rsqrt(jnp.mean(xc * xc, axis=-1, keepdims=True) + EPS) * lg + lb
    y = mm(_silu(y), pw) + pb
    return y * _silu(z)


def _branch_c(bg, cw_, xw, z, w3, conv):
    return bg * conv(cw_ * xw, w3) * _silu(z)


def _ac_fwd_call(proj, p):
    L = proj.shape[0]

    def body(val, gate, za, hval, hgate, cb_, cc, cx, cz, hcc, hcx,
             acw, acb, alg, alb, apw, apb, ccw, ya_ref, yc_ref):
        nf = (pl.program_id(0) > 0).astype(F32)
        win = lambda h, m: jnp.concatenate([h[...] * nf, m[...]], axis=0)
        conv_a = functools.partial(_conv_fwd_impl, tile=TL, halo=HALO, taps=KA)
        conv_c = functools.partial(_conv_fwd_impl, tile=TL, halo=HALO, taps=KC)
        ya_ref[...] = _branch_a(win(hval, val), win(hgate, gate), za[...], acw[...], acb[...], alg[...], alb[...],
                                apw[...], apb[...], conv_a)
        yc_ref[...] = _branch_c(cb_[...], win(hcc, cc), win(hcx, cx), cz[...], ccw[...], conv_c)

    col = lambda j: pl.BlockSpec((TL, BR), lambda i: (i, j))
    hal = lambda j: _halo_spec(TL, HALO, BR, j)
    full = lambda a: pl.BlockSpec(a.shape, lambda i: (0,) * a.ndim)
    params = (p["a_conv_w"], p["a_conv_b"], p["a_ln_g"], p["a_ln_b"], p["a_pw_w"], p["a_pw_b"], p["c_conv_w"])
    return pl.pallas_call(
        body, name="ac_fwd", grid=(L // TL,),
        in_specs=[col(0), col(1), col(2), hal(0), hal(1), col(5), col(6), col(7), col(8), hal(6), hal(7)]
        + [full(a) for a in params],
        out_specs=[pl.BlockSpec((TL, BR), lambda i: (i, 0))] * 2,
        out_shape=[SDS((L, BR), F32)] * 2,
        compiler_params=_cparams(("parallel",)),
    )(*([proj] * 11), *params)


def _ac_bwd_call(proj, p, dya, dyc):
    L = proj.shape[0]
    nt = L // TL

    def body(val, gate, za, hval, hgate, cb_, cc, cx, cz, hcc, hcx,
             acw, acb, alg, alb, apw, apb, ccw, dya_ref, dyc_ref,
             da_ref, dc_ref, g_acw, g_acb, g_alg, g_alb, g_apw, g_apb, g_ccw, carry):
        i = pl.program_id(0)
        gouts = (g_acw, g_acb, g_alg, g_alb, g_apw, g_apb, g_ccw)

        @pl.when(i == 0)
        def _():
            carry[...] = jnp.zeros_like(carry)
            for r in gouts:
                r[...] = jnp.zeros_like(r)

        nf = (i < nt - 1).astype(F32)
        win = lambda h, m: jnp.concatenate([h[...] * nf, m[...]], axis=0)
        conv_a = _make_conv(TL, HALO, KA)
        conv_c = _make_conv(TL, HALO, KC)

        def f(valw, gatew, z, bg, ccw_, cxw, czv, w1, b1, lg, lb, pw, pb, w3):
            return (_branch_a(valw, gatew, z, w1, b1, lg, lb, pw, pb, conv_a),
                    _branch_c(bg, ccw_, cxw, czv, w3, conv_c))

        _, vjp = jax.vjp(f, win(hval, val), win(hgate, gate), za[...], cb_[...], win(hcc, cc), win(hcx, cx), cz[...],
                         acw[...], acb[...], alg[...], alb[...], apw[...].astype(F32), apb[...], ccw[...])
        (dvalw, dgatew, dz, dbg, dccw, dcxw, dczv, d1, d2, d3, d4, d5, d6, d7) = vjp((dya_ref[...], dyc_ref[...]))

        def settle(slot, dwin):
            tail = jnp.concatenate([jnp.zeros((TL - HALO, BR), F32), carry[slot]], axis=0)
            carry[slot] = dwin[:HALO, :]
            return (dwin[HALO:, :] + tail).astype(BF16)

        da_ref[:, 0:BR] = settle(0, dvalw)
        da_ref[:, BR:2 * BR] = settle(1, dgatew)
        da_ref[:, 2 * BR:3 * BR] = dz.astype(BF16)
        dc_ref[:, 0:BR] = dbg.astype(BF16)
        dc_ref[:, BR:2 * BR] = settle(2, dccw)
        dc_ref[:, 2 * BR:3 * BR] = settle(3, dcxw)
        dc_ref[:, 3 * BR:4 * BR] = dczv.astype(BF16)
        for r, g in zip(gouts, (d1, d2, d3, d4, d5, d6, d7)):
            r[...] += g

    col = lambda j: pl.BlockSpec((TL, BR), lambda i: (nt - 1 - i, j))
    hal = lambda j: _halo_spec_rev(nt, TL, HALO, BR, j)
    full = lambda a: pl.BlockSpec(a.shape, lambda i: (0,) * a.ndim)
    params = (p["a_conv_w"], p["a_conv_b"], p["a_ln_g"], p["a_ln_b"], p["a_pw_w"], p["a_pw_b"], p["c_conv_w"])
    rev = lambda w: pl.BlockSpec((TL, w), lambda i: (nt - 1 - i, 0))
    return pl.pallas_call(
        body, name="ac_bwd", grid=(nt,),
        in_specs=[col(0), col(1), col(2), hal(0), hal(1), col(5), col(6), col(7), col(8), hal(6), hal(7)]
        + [full(a) for a in params] + [rev(BR), rev(BR)],
        out_specs=[rev(3 * BR), rev(4 * BR)] + [full(a) for a in params],
        out_shape=[SDS((L, 3 * BR), BF16), SDS((L, 4 * BR), BF16)] + [SDS(a.shape, F32) for a in params],
        scratch_shapes=[pltpu.VMEM((4, HALO, BR), F32)],
        compiler_params=_cparams(("arbitrary",)),
    )(*([proj] * 11), *params, dya, dyc)


def _iota2(shape, dim):
    return lax.broadcasted_iota(jnp.int32, shape, dim)


def _s5_params(lam_re, lam_im, logdt, b_re, b_im, c_re, c_im):
    eg = (_iota2((16, NSTATE), 1) >> 6 == _iota2((16, NSTATE), 0)).astype(F32)
    dt = jnp.exp(mmh(jnp.broadcast_to(logdt, (8, 16)), eg)[0:1, :])
    lr = jnp.minimum(lam_re, -1e-4)
    li = lam_im
    mag = jnp.exp(lr * dt)
    lbr = mag * jnp.cos(li * dt)
    lbi = mag * jnp.sin(li * dt)
    den = lr * lr + li * li
    nr = lbr - 1.0
    fr = (nr * lr + lbi * li) / den
    fi = (lbi * lr - nr * li) / den
    row = _iota2((8, NSTATE), 0)
    f8 = jnp.where(row == 0, fr, jnp.where(row == 1, fi, 0.0))
    eye = (_iota2((NSTATE, NSTATE), 0) == _iota2((NSTATE, NSTATE), 1)).astype(F32)
    fcol = mmh_nt(eye, f8)
    frc, fic = fcol[:, 0:1], fcol[:, 1:2]
    bbr = frc * b_re - fic * b_im
    bbi = frc * b_im + fic * b_re
    e1 = ((_iota2((16, BR), 1) & 15) == _iota2((16, BR), 0)).astype(F32)
    m1 = ((_iota2((NSTATE, BR), 0) >> 6) == (_iota2((NSTATE, BR), 1) >> 4)).astype(F32)
    wbr = mmh(bbr, e1) * m1
    wbi = mmh(bbi, e1) * m1
    e2 = ((_iota2((64, NSTATE), 1) & 63) == _iota2((64, NSTATE), 0)).astype(F32)
    m2 = ((_iota2((BR, NSTATE), 0) >> 4) == (_iota2((BR, NSTATE), 1) >> 6)).astype(F32)
    wcr = mmh(c_re, e2) * m2
    wci = mmh(c_im, e2) * m2
    return lbr, lbi, wbr, wbi, wcr, wci


_S5_OUT = [(1, NSTATE), (1, NSTATE), (NSTATE, BR), (NSTATE, BR), (BR, NSTATE), (BR, NSTATE)]


def _s5_prep_call(sp):
    def body(lre, lim, ldt, bre, bim, cre, cim, o_lbr, o_lbi, o_wbr, o_wbi, o_wcr, o_wci, pwr, pwi, qwr, qwi):
        lbr, lbi, wbr, wbi, wcr, wci = _s5_params(lre[...], lim[...], ldt[...], bre[...], bim[...], cre[...], cim[...])
        o_lbr[...], o_lbi[...], o_wbr[...], o_wbi[...], o_wcr[...], o_wci[...] = lbr, lbi, wbr, wbi, wcr, wci
        pr, pi = lbr, lbi
        for i in range(SEG):
            pwr[i:i + 1, :] = pr
            pwi[i:i + 1, :] = pi
            qwr[SEG - 1 - i:SEG - i, :] = pr
            qwi[SEG - 1 - i:SEG - i, :] = -pi
            pr, pi = pr * lbr - pi * lbi, pr * lbi + pi * lbr

    args = (sp["lam_re"], sp["lam_im"], sp["log_dt"], sp["b_re"], sp["b_im"], sp["c_re"], sp["c_im"])
    return pl.pallas_call(
        body, name="s5_prep",
        out_shape=[SDS(s, F32) for s in _S5_OUT] + [SDS((SEG, NSTATE), F32)] * 4,
        compiler_params=pltpu.CompilerParams(vmem_limit_bytes=VMEM_LIMIT),
    )(*args)


def _s5_prep_bwd_call(sp, cots):
    def body(lre, lim, ldt, bre, bim, cre, cim, c0, c1, c2, c3, c4, c5, *outs):
        _, vjp = jax.vjp(_s5_params, lre[...], lim[...], ldt[...], bre[...], bim[...], cre[...], cim[...])
        grads = vjp((c0[...], c1[...], c2[...], c3[...], c4[...], c5[...]))
        for o, g in zip(outs, grads):
            o[...] = g

    args = (sp["lam_re"], sp["lam_im"], sp["log_dt"], sp["b_re"], sp["b_im"], sp["c_re"], sp["c_im"])
    return pl.pallas_call(
        body, name="s5_prep_bwd",
        out_shape=[SDS(a.shape, F32) for a in args],
        compiler_params=pltpu.CompilerParams(vmem_limit_bytes=VMEM_LIMIT),
    )(*args, *cots)


def _lanes(v, j):
    return v[:, j * 128:(j + 1) * 128]


def _s5_scan(sre, sim, lbr, lbi, pwr, pwi, cin_r, cin_i, reverse):
    lr = [jnp.broadcast_to(_lanes(lbr, j), (8, 128)) for j in range(8)]
    li = [jnp.broadcast_to(_lanes(lbi, j), (8, 128)) for j in range(8)]

    def step(t, st):
        i = SEG - 1 - t if reverse else t
        rows = pl.ds(i, 8, stride=SEG)
        new = []
        for j in range(8):
            sr, si = st[2 * j], st[2 * j + 1]
            vr, vi = sre.at[j], sim.at[j]
            nr = lr[j] * sr - li[j] * si + vr[rows, :]
            ni = lr[j] * si + li[j] * sr + vi[rows, :]
            vr[rows, :] = nr
            vi[rows, :] = ni
            new += [nr, ni]
        return tuple(new)

    ends = lax.fori_loop(0, SEG, step, tuple(jnp.zeros((8, 128), F32) for _ in range(16)))
    e_r = jnp.concatenate([ends[2 * j] for j in range(8)], axis=1)
    e_i = jnp.concatenate([ends[2 * j + 1] for j in range(8)], axis=1)
    l64r, l64i = (pwr[0:1, :], pwi[0:1, :]) if reverse else (pwr[SEG - 1:SEG, :], pwi[SEG - 1:SEG, :])
    order = range(7, -1, -1) if reverse else range(8)
    c_r, c_i = cin_r, cin_i
    carries = {}
    for k in order:
        carries[k] = (c_r, c_i)
        er, ei = e_r[k:k + 1, :], e_i[k:k + 1, :]
        c_r, c_i = er + l64r * c_r - l64i * c_i, ei + l64r * c_i + l64i * c_r
    for j in range(8):
        pr, pi = pwr[:, j * 128:(j + 1) * 128], pwi[:, j * 128:(j + 1) * 128]
        for k in range(8):
            cr, ci = _lanes(carries[k][0], j), _lanes(carries[k][1], j)
            blk = slice(k * SEG, (k + 1) * SEG)
            sre[j, blk, :] = sre[j, blk, :] + pr * cr - pi * ci
            sim[j, blk, :] = sim[j, blk, :] + pr * ci + pi * cr
    return c_r, c_i


def _bdot(a, b, dims):
    return _dot(a.astype(BF16), b.astype(BF16), dims)


def _s5_states(u, wbr, wbi, sre, sim):
    bur = _bdot(u, wbr, NT)
    bui = _bdot(u, wbi, NT)
    for j in range(8):
        sre[j] = _lanes(bur, j)
        sim[j] = _lanes(bui, j)


def _gather_lanes(s):
    return jnp.concatenate([s[j] for j in range(8)], axis=1)


def _s5_post(s_re, s_im, u, z, wcr, wci, dsk, gw, gb):
    y = mm_nt(s_re, wcr) - mm_nt(s_im, wci) + dsk * u
    yg = _gelu(y)
    return yg * _sigmoid(mm(yg, gw) + gb) * _silu(z)


def _s5_fwd_call(proj, prep, p):
    L = proj.shape[0]
    nt = L // TL
    lbr, lbi, wbr, wbi, wcr, wci, pwr, pwi, _, _ = prep

    def body(u_ref, z_ref, lbr_r, lbi_r, wbr_r, wbi_r, wcr_r, wci_r, pwr_r, pwi_r, d_r, gw_r, gb_r,
             yb_ref, cinr_ref, cini_ref, sre, sim, car, cai):
        @pl.when(pl.program_id(0) == 0)
        def _():
            car[...] = jnp.zeros_like(car)
            cai[...] = jnp.zeros_like(cai)

        u = u_ref[...]
        cinr_ref[0] = car[...]
        cini_ref[0] = cai[...]
        _s5_states(u, wbr_r[...], wbi_r[...], sre, sim)
        nr, ni = _s5_scan(sre, sim, lbr_r[...], lbi_r[...], pwr_r, pwi_r, car[...], cai[...], False)
        car[...] = nr
        cai[...] = ni
        yb_ref[...] = _s5_post(_gather_lanes(sre), _gather_lanes(sim), u, z_ref[...], wcr_r[...], wci_r[...],
                               d_r[...], gw_r[...], gb_r[...])

    full = lambda a: pl.BlockSpec(a.shape, lambda i: (0,) * a.ndim)
    consts = (lbr, lbi, wbr, wbi, wcr, wci, pwr, pwi, p["s5_d"], p["s5_glu_w"], p["s5_glu_b"])
    cspec = pl.BlockSpec((1, 1, NSTATE), lambda i: (i, 0, 0))
    return pl.pallas_call(
        body, name="s5_fwd", grid=(nt,),
        in_specs=[pl.BlockSpec((TL, BR), lambda i: (i, 3)), pl.BlockSpec((TL, BR), lambda i: (i, 4))]
        + [full(a) for a in consts],
        out_specs=[pl.BlockSpec((TL, BR), lambda i: (i, 0)), cspec, cspec],
        out_shape=[SDS((L, BR), F32), SDS((nt, 1, NSTATE), F32), SDS((nt, 1, NSTATE), F32)],
        scratch_shapes=[pltpu.VMEM((8, TL, 128), F32), pltpu.VMEM((8, TL, 128), F32),
                        pltpu.VMEM((1, NSTATE), F32), pltpu.VMEM((1, NSTATE), F32)],
        compiler_params=_cparams(("arbitrary",)),
    )(proj, proj, *consts)


def _s5_bwd_call(proj, prep, p, cin_r, cin_i, dyb):
    L = proj.shape[0]
    nt = L // TL
    lbr, lbi, wbr, wbi, wcr, wci, pwr, pwi, qwr, qwi = prep

    def body(u_ref, z_ref, lbr_r, lbi_r, wbr_r, wbi_r, wcr_r, wci_r, pwr_r, pwi_r, qwr_r, qwi_r, d_r, gw_r, gb_r,
             cinr_ref, cini_ref, dy_ref,
             db_ref, g_lbr, g_lbi, g_wbr, g_wbi, g_wcr, g_wci, g_d, g_gw, g_gb, sre, sim, gre, gim, car, cai):
        gouts = (g_lbr, g_lbi, g_wbr, g_wbi, g_wcr, g_wci, g_d, g_gw, g_gb)

        @pl.when(pl.program_id(0) == 0)
        def _():
            car[...] = jnp.zeros_like(car)
            cai[...] = jnp.zeros_like(cai)
            for r in gouts:
                r[...] = jnp.zeros_like(r)

        u = u_ref[...]
        lr, li = lbr_r[...], lbi_r[...]
        c0r, c0i = cinr_ref[0], cini_ref[0]
        _s5_states(u, wbr_r[...], wbi_r[...], sre, sim)
        _s5_scan(sre, sim, lr, li, pwr_r, pwi_r, c0r, c0i, False)
        s_re, s_im = _gather_lanes(sre), _gather_lanes(sim)
        _, vjp = jax.vjp(_s5_post, s_re, s_im, u, z_ref[...], wcr_r[...], wci_r[...], d_r[...],
                         gw_r[...].astype(F32), gb_r[...])
        ds_re, ds_im, du, dz, dwcr, dwci, dd, dgw, dgb = vjp(dy_ref[...])
        for j in range(8):
            gre[j] = _lanes(ds_re, j)
            gim[j] = _lanes(ds_im, j)
        nr, ni = _s5_scan(gre, gim, lr, -li, qwr_r, qwi_r, car[...], cai[...], True)
        car[...] = nr
        cai[...] = ni
        a_re, a_im = _gather_lanes(gre), _gather_lanes(gim)
        first = _iota2((TL, NSTATE), 0) == 0
        p_re = jnp.where(first, c0r, jnp.roll(s_re, 1, axis=0))
        p_im = jnp.where(first, c0i, jnp.roll(s_im, 1, axis=0))
        g_lbr[...] += jnp.sum(a_re * p_re + a_im * p_im, axis=0, keepdims=True)
        g_lbi[...] += jnp.sum(a_im * p_re - a_re * p_im, axis=0, keepdims=True)
        du = du + _bdot(a_re, wbr_r[...], NN) + _bdot(a_im, wbi_r[...], NN)
        g_wbr[...] += _bdot(a_re, u, TN)
        g_wbi[...] += _bdot(a_im, u, TN)
        g_wcr[...] += dwcr
        g_wci[...] += dwci
        g_d[...] += dd
        g_gw[...] += dgw
        g_gb[...] += dgb
        db_ref[:, 0:BR] = du.astype(BF16)
        db_ref[:, BR:2 * BR] = dz.astype(BF16)

    full = lambda a: pl.BlockSpec(a.shape, lambda i: (0,) * a.ndim)
    consts = (lbr, lbi, wbr, wbi, wcr, wci, pwr, pwi, qwr, qwi, p["s5_d"], p["s5_glu_w"], p["s5_glu_b"])
    cspec = pl.BlockSpec((1, 1, NSTATE), lambda i: (nt - 1 - i, 0, 0))
    gshapes = _S5_OUT + [(1, BR), (BR, BR), (1, BR)]
    return pl.pallas_call(
        body, name="s5_bwd", grid=(nt,),
        in_specs=[pl.BlockSpec((TL, BR), lambda i: (nt - 1 - i, 3)), pl.BlockSpec((TL, BR), lambda i: (nt - 1 - i, 4))]
        + [full(a) for a in consts] + [cspec, cspec, pl.BlockSpec((TL, BR), lambda i: (nt - 1 - i, 0))],
        out_specs=[pl.BlockSpec((TL, 2 * BR), lambda i: (nt - 1 - i, 0))]
        + [pl.BlockSpec(s, lambda i: (0, 0)) for s in gshapes],
        out_shape=[SDS((L, 2 * BR), BF16)] + [SDS(s, F32) for s in gshapes],
        scratch_shapes=[pltpu.VMEM((8, TL, 128), F32)] * 4 + [pltpu.VMEM((1, NSTATE), F32)] * 2,
        compiler_params=_cparams(("arbitrary",)),
    )(proj, proj, *consts, cin_r, cin_i, dyb)


def _heads(x):
    return [x[:, h * HD:(h + 1) * HD] for h in range(NH)]


def _l2n(x, scale):
    return jnp.concatenate([xh * (lax.rsqrt(jnp.sum(xh * xh, axis=-1, keepdims=True) + EPS) * scale)
                            for xh in _heads(x)], axis=1)


def _dn_pre(qkvw, ab, cw, alog, dtb, conv, rows):
    c = _silu(conv(qkvw, cw))
    q = _l2n(c[:, 0:BR], HD ** -0.5)
    k = _l2n(c[:, BR:2 * BR], 1.0)
    v = c[:, 2 * BR:3 * BR]
    g = -jnp.exp(alog) * _softplus(ab + dtb)
    ri, ci = _iota2((rows, rows), 0), _iota2((rows, rows), 1)
    tri = ((ri >= ci) & ((ri >> 6) == (ci >> 6))).astype(F32)
    gc = mmh(tri, g)
    lane = _iota2(ab.shape, 1)
    return q, k, v, jnp.where(lane < NH, gc, jnp.where(lane < 2 * NH, _sigmoid(ab), 0.0))


def _dn_pre_fwd_call(proj, p):
    L = proj.shape[0]

    def body(m_ref, h_ref, ab_ref, cw, alog, dtb, q_ref, k_ref, v_ref, gb_ref):
        nf = (pl.program_id(0) > 0).astype(F32)
        qkvw = jnp.concatenate([h_ref[...] * nf, m_ref[...]], axis=0)
        conv = functools.partial(_conv_fwd_impl, tile=TL, halo=HALO_S, taps=KD)
        q_ref[...], k_ref[...], v_ref[...], gb_ref[...] = _dn_pre(qkvw, ab_ref[...], cw[...], alog[...], dtb[...], conv, TL)

    full = lambda a: pl.BlockSpec(a.shape, lambda i: (0,) * a.ndim)
    params = (p["d_conv_w"], p["d_a_log"], p["d_dt_bias"])
    o = pl.BlockSpec((TL, BR), lambda i: (i, 0))
    return pl.pallas_call(
        body, name="dn_pre_fwd", grid=(L // TL,),
        in_specs=[pl.BlockSpec((TL, 3 * BR), lambda i: (i, 3)), _halo_spec(TL, HALO_S, 3 * BR, 3),
                  pl.BlockSpec((TL, 128), lambda i: (i, AB_COL // 128))] + [full(a) for a in params],
        out_specs=[o, o, o, pl.BlockSpec((TL, 128), lambda i: (i, 0))],
        out_shape=[SDS((L, BR), F32)] * 3 + [SDS((L, 128), F32)],
        compiler_params=_cparams(("parallel",)),
    )(proj, proj, proj, *params)


def _dn_pre_bwd_call(proj, p, dq, dk, dv, dgb):
    L = proj.shape[0]
    nt = L // TL

    def body(m_ref, h_ref, ab_ref, cw, alog, dtb, dq_r, dk_r, dv_r, dgb_r,
             dqkv_ref, dab_ref, g_cw, g_alog, g_dtb, carry):
        i = pl.program_id(0)

        @pl.when(i == 0)
        def _():
            carry[...] = jnp.zeros_like(carry)
            for r in (g_cw, g_alog, g_dtb):
                r[...] = jnp.zeros_like(r)

        nf = (i < nt - 1).astype(F32)
        qkvw = jnp.concatenate([h_ref[...] * nf, m_ref[...]], axis=0)
        conv = _make_conv(TL, HALO_S, KD)
        _, vjp = jax.vjp(lambda a, b, c, d, e: _dn_pre(a, b, c, d, e, conv, TL),
                         qkvw, ab_ref[...], cw[...], alog[...], dtb[...])
        dwin, dab, dcw, dalog, ddtb = vjp((dq_r[...], dk_r[...], dv_r[...], dgb_r[...]))
        tail = jnp.concatenate([jnp.zeros((TL - HALO_S, 3 * BR), F32), carry[...]], axis=0)
        carry[...] = dwin[:HALO_S, :]
        dqkv_ref[...] = (dwin[HALO_S:, :] + tail).astype(BF16)
        dab_ref[...] = dab.astype(BF16)
        g_cw[...] += dcw
        g_alog[...] += dalog
        g_dtb[...] += ddtb

    full = lambda a: pl.BlockSpec(a.shape, lambda i: (0,) * a.ndim)
    params = (p["d_conv_w"], p["d_a_log"], p["d_dt_bias"])
    rev = lambda w: pl.BlockSpec((TL, w), lambda i: (nt - 1 - i, 0))
    return pl.pallas_call(
        body, name="dn_pre_bwd", grid=(nt,),
        in_specs=[pl.BlockSpec((TL, 3 * BR), lambda i: (nt - 1 - i, 3)), _halo_spec_rev(nt, TL, HALO_S, 3 * BR, 3),
                  pl.BlockSpec((TL, 128), lambda i: (nt - 1 - i, AB_COL // 128))] + [full(a) for a in params]
        + [rev(BR), rev(BR), rev(BR), rev(128)],
        out_specs=[rev(3 * BR), rev(128)] + [full(a) for a in params],
        out_shape=[SDS((L, 3 * BR), BF16), SDS((L, 128), BF16)] + [SDS(a.shape, F32) for a in params],
        scratch_shapes=[pltpu.VMEM((HALO_S, 3 * BR), F32)],
        compiler_params=_cparams(("arbitrary",)),
    )(proj, proj, proj, *params, dq, dk, dv, dgb)


def _dn_group(q, k, v, gb, z, ng, *s):
    ri, ci = _iota2((CH, CH), 0), _iota2((CH, CH), 1)
    causal, strict = ri >= ci, ri > ci
    eye = (ri == ci).astype(F32)
    s = list(s)
    pairs = []
    for c in range(DN_GROUP):
        rows = slice(c * CH, (c + 1) * CH)
        gbc = gb[rows, :]
        for h, (qh, kh, vh, zh) in enumerate(zip(_heads(q[rows, :]), _heads(k[rows, :]), _heads(v[rows, :]),
                                                 _heads(z[rows, :]))):
            gc = jnp.broadcast_to(gbc[:, h:h + 1], (CH, HD))
            beta = gbc[:, NH + h:NH + h + 1]
            decay = jnp.where(causal, jnp.exp(jnp.where(causal, gc - gc.T, 0.0)), 0.0)
            egc = jnp.exp(gc)
            glast = gc[CH - 1:CH, :]
            kb = kh * beta
            pairs.append(dict(q=qh, k=kh, z=zh, decay=decay, qe=qh * egc, kd=kh * jnp.exp(glast - gc),
                              sdec=jnp.exp(glast[:, 0:1]), kb=kb, rhs=jnp.concatenate([vh * beta, kb * egc], axis=1)))
    for p in pairs:
        p["pw"] = jnp.where(strict, mm_nt(p["kb"], p["k"]) * p["decay"], 0.0)
        p["t"] = eye - p["pw"]
    for _ in range(5):
        for p in pairs:
            p["pw"] = mm(p["pw"], p["pw"])
        for p in pairs:
            p["t"] = mm(p["t"], eye + p["pw"])
    for p in pairs:
        p["uw"] = mm(p["t"], p["rhs"])
    for p in pairs:
        p["attn"] = mm_nt(p["q"], p["k"]) * p["decay"]
    out_rows = []
    for c in range(DN_GROUP):
        grp = pairs[c * NH:(c + 1) * NH]
        ws = [mm(jnp.concatenate([p["uw"][:, HD:], p["qe"]], axis=0), s[h]) for h, p in enumerate(grp)]
        v_new = [p["uw"][:, :HD] - w_[:CH, :] for p, w_ in zip(grp, ws)]
        o = [w_[CH:, :] + mm(p["attn"], vn) for p, w_, vn in zip(grp, ws, v_new)]
        s = [s[h] * p["sdec"] + mm_tn(p["kd"], vn) for h, (p, vn) in enumerate(zip(grp, v_new))]
        o = [oh * lax.rsqrt(jnp.mean(oh * oh, axis=-1, keepdims=True) + EPS) * ng * _silu(p["z"]) for oh, p in zip(o, grp)]
        out_rows.append(jnp.concatenate(o, axis=1))
    return (jnp.concatenate(out_rows, axis=0), *s)


def _dn_core_fwd_call(proj, q, k, v, gb, ng):
    L = q.shape[0]
    rows = DN_GROUP * CH
    ng_ = L // rows

    def body(q_r, k_r, v_r, gb_r, z_r, ng_r, yd_ref, ssave_ref, s_scr):
        @pl.when(pl.program_id(0) == 0)
        def _():
            s_scr[...] = jnp.zeros_like(s_scr)

        ssave_ref[0] = s_scr[...]
        yd, *s2 = _dn_group(q_r[...], k_r[...], v_r[...], gb_r[...], z_r[...], ng_r[...], *[s_scr[h] for h in range(NH)])
        yd_ref[...] = yd
        for h in range(NH):
            s_scr[h] = s2[h]

    c = pl.BlockSpec((rows, BR), lambda i: (i, 0))
    return pl.pallas_call(
        body, name="dn_core_fwd", grid=(ng_,),
        in_specs=[c, c, c, pl.BlockSpec((rows, 128), lambda i: (i, 0)), pl.BlockSpec((rows, BR), lambda i: (i, 12)),
                  pl.BlockSpec((1, HD), lambda i: (0, 0))],
        out_specs=[c, pl.BlockSpec((1, NH, HD, HD), lambda i: (i, 0, 0, 0))],
        out_shape=[SDS((L, BR), F32), SDS((ng_, NH, HD, HD), F32)],
        scratch_shapes=[pltpu.VMEM((NH, HD, HD), F32)],
        compiler_params=_cparams(("arbitrary",)),
    )(q, k, v, gb, proj, ng)


def _dn_core_bwd_call(proj, q, k, v, gb, ng, ssave, dyd):
    L = q.shape[0]
    rows = DN_GROUP * CH
    ng_ = L // rows

    def body(q_r, k_r, v_r, gb_r, z_r, ng_r, s_r, dy_r, dq_ref, dk_ref, dv_ref, dgb_ref, dz_ref, g_ng, ds_scr):
        @pl.when(pl.program_id(0) == 0)
        def _():
            ds_scr[...] = jnp.zeros_like(ds_scr)
            g_ng[...] = jnp.zeros_like(g_ng)

        _, vjp = jax.vjp(_dn_group, q_r[...], k_r[...], v_r[...], gb_r[...], z_r[...], ng_r[...],
                         *[s_r[0, h] for h in range(NH)])
        dq, dk, dv, dgb, dz, dng, *ds = vjp((dy_r[...], *[ds_scr[h] for h in range(NH)]))
        dq_ref[...], dk_ref[...], dv_ref[...], dgb_ref[...] = dq, dk, dv, dgb
        dz_ref[...] = dz.astype(BF16)
        g_ng[...] += dng
        for h in range(NH):
            ds_scr[h] = ds[h]

    c = pl.BlockSpec((rows, BR), lambda i: (ng_ - 1 - i, 0))
    c128 = pl.BlockSpec((rows, 128), lambda i: (ng_ - 1 - i, 0))
    return pl.pallas_call(
        body, name="dn_core_bwd", grid=(ng_,),
        in_specs=[c, c, c, c128, pl.BlockSpec((rows, BR), lambda i: (ng_ - 1 - i, 12)),
                  pl.BlockSpec((1, HD), lambda i: (0, 0)),
                  pl.BlockSpec((1, NH, HD, HD), lambda i: (ng_ - 1 - i, 0, 0, 0)), c],
        out_specs=[c, c, c, c128, c, pl.BlockSpec((1, HD), lambda i: (0, 0))],
        out_shape=[SDS((L, BR), F32)] * 3 + [SDS((L, 128), F32), SDS((L, BR), BF16), SDS((1, HD), F32)],
        scratch_shapes=[pltpu.VMEM((NH, HD, HD), F32)],
        compiler_params=_cparams(("arbitrary",)),
    )(q, k, v, gb, proj, ng, ssave, dyd)


def _outproj_bwd_call(dx, ys, w):
    L = dx.shape[0]

    def body(dx_ref, a_ref, b_ref, c_ref, d_ref, w_ref, da, db, dc, dd, dw_ref):
        @pl.when(pl.program_id(0) == 0)
        def _():
            dw_ref[...] = jnp.zeros_like(dw_ref)

        dxb = dx_ref[...].astype(BF16)
        for b, (y_ref, o_ref) in enumerate(zip((a_ref, b_ref, c_ref, d_ref), (da, db, dc, dd))):
            o_ref[...] = _dot(dxb, w_ref[b * BR:(b + 1) * BR, :], NT)
            dw_ref[b * BR:(b + 1) * BR, :] += _dot(y_ref[...].astype(BF16), dxb, TN)

    yspec = pl.BlockSpec((TL, BR), lambda i: (i, 0))
    return pl.pallas_call(
        body, name="outproj_bwd", grid=(L // TL,),
        in_specs=[pl.BlockSpec((TL, D), lambda i: (i, 0)), yspec, yspec, yspec, yspec,
                  pl.BlockSpec((D, D), lambda i: (0, 0))],
        out_specs=[yspec] * 4 + [pl.BlockSpec((D, D), lambda i: (0, 0))],
        out_shape=[SDS((L, BR), F32)] * 4 + [SDS((D, D), F32)],
        compiler_params=_cparams(("arbitrary",)),
    )(dx, *ys, w)


def _slab_cols(slabs):
    widths = [s.shape[1] for s in slabs]
    starts = [sum(widths[:i]) for i in range(len(widths))]
    assert starts[-1] + widths[-1] == PW
    return list(zip(starts, widths))


def _inproj_bwd_x_call(slabs, w, x, g, dx_next):
    L = x.shape[0]
    cols = _slab_cols(slabs)
    n = len(slabs)

    def body(*refs):
        dp_refs, (w_ref, x_ref, g_ref, dxn_ref, dx_ref, dg_ref) = refs[:n], refs[n:]

        @pl.when(pl.program_id(0) == 0)
        def _():
            dg_ref[...] = jnp.zeros_like(dg_ref)

        dh = None
        for dp_ref, (c0, cw) in zip(dp_refs, cols):
            part = _dot(dp_ref[...], w_ref[:, c0:c0 + cw], NT)
            dh = part if dh is None else dh + part
        _, vjp = jax.vjp(_rms, x_ref[...], g_ref[...])
        dx, dg = vjp(dh)
        dx_ref[...] = dx + dxn_ref[...]
        dg_ref[...] += dg

    row = lambda w_: pl.BlockSpec((TL, w_), lambda i: (i, 0))
    return pl.pallas_call(
        body, name="inproj_bwd_x", grid=(L // TL,),
        in_specs=[row(cw) for _, cw in cols]
        + [pl.BlockSpec((D, PW), lambda i: (0, 0)), row(D), pl.BlockSpec((1, D), lambda i: (0, 0)), row(D)],
        out_specs=[row(D), pl.BlockSpec((1, D), lambda i: (0, 0))],
        out_shape=[SDS((L, D), F32), SDS((1, D), F32)],
        compiler_params=_cparams(("arbitrary",)),
    )(*slabs, w, x, g, dx_next)


def _inproj_bwd_w_call(h, slabs):
    L = h.shape[0]
    cols = _slab_cols(slabs)
    n = len(slabs)

    def body(*refs):
        h_ref, dp_refs, dw_ref = refs[0], refs[1:1 + n], refs[1 + n]

        @pl.when(pl.program_id(0) == 0)
        def _():
            dw_ref[...] = jnp.zeros_like(dw_ref)

        hv = h_ref[...]
        for dp_ref, (c0, cw) in zip(dp_refs, cols):
            dw_ref[:, c0:c0 + cw] += _dot(hv, dp_ref[...], TN)

    row = lambda w_: pl.BlockSpec((TL, w_), lambda i: (i, 0))
    return pl.pallas_call(
        body, name="inproj_bwd_w", grid=(L // TL,),
        in_specs=[row(D)] + [row(cw) for _, cw in cols],
        out_specs=pl.BlockSpec((D, PW), lambda i: (0, 0)),
        out_shape=SDS((D, PW), F32),
        compiler_params=_cparams(("arbitrary",)),
    )(h, *slabs)


def _exchange_call(name, flows):
    n = len(flows)

    def body(*refs):
        srcs, dsts = refs[:n], refs[n:2 * n]
        send_sems, recv_sems, local_sems = refs[2 * n:]
        x, y, c = lax.axis_index("x"), lax.axis_index("y"), lax.axis_index("c")
        me = 4 * x + 2 * y + c
        copies = []
        for mask in range(1, N_DEV):
            px = 1 - x if mask & 4 else x
            py = 1 - y if mask & 2 else y
            pc = 1 - c if mask & 1 else c
            for f, (_, src_at, _, dst_at) in enumerate(flows):
                cp = pltpu.make_async_remote_copy(
                    src_ref=src_at(srcs[f], 4 * px + 2 * py + pc), dst_ref=dst_at(dsts[f], me),
                    send_sem=send_sems.at[mask - 1, f], recv_sem=recv_sems.at[mask - 1, f],
                    device_id=(px, py, pc), device_id_type=pl.DeviceIdType.MESH)
                cp.start()
                copies.append(cp)
        mine = [pltpu.make_async_copy(src_at(srcs[f], me), dst_at(dsts[f], me), local_sems.at[f])
                for f, (_, src_at, _, dst_at) in enumerate(flows)]
        for cp in mine:
            cp.start()
        for cp in copies + mine:
            cp.wait()

    return pl.pallas_call(
        body, name=name,
        in_specs=[pl.BlockSpec(memory_space=pl.ANY)] * n,
        out_specs=[pl.BlockSpec(memory_space=pl.ANY)] * n,
        out_shape=[SDS(tuple(shape), src.dtype) for src, _, shape, _ in flows],
        scratch_shapes=[pltpu.SemaphoreType.DMA((N_DEV - 1, n)), pltpu.SemaphoreType.DMA((N_DEV - 1, n)),
                        pltpu.SemaphoreType.DMA((n,))],
    )(*[f[0] for f in flows])


def _whole(ref, _):
    return ref


def _slot(ref, k):
    return ref.at[k]


def _reduce_adamw_call(parts, w, m, v, block, name):
    nsrc = parts.shape[0]
    grid = tuple(s // b for s, b in zip(w.shape, block))
    c1 = 1.0 - ADAM_B1 ** ADAM_STEP
    c2 = 1.0 - ADAM_B2 ** ADAM_STEP

    def body(p_ref, w_ref, m_ref, v_ref, g_ref, d_ref, nm_ref, nv_ref):
        g = p_ref[0].astype(F32)
        for k in range(1, nsrc):
            g = g + p_ref[k].astype(F32)
        nm = ADAM_B1 * m_ref[...] + (1.0 - ADAM_B1) * g
        nv = ADAM_B2 * v_ref[...] + (1.0 - ADAM_B2) * (g * g)
        g_ref[...] = g
        nm_ref[...] = nm
        nv_ref[...] = nv
        d_ref[...] = -ADAM_LR * ((nm / c1) / (jnp.sqrt(nv / c2) + ADAM_EPS) + ADAM_WD * w_ref[...])

    own = pl.BlockSpec(tuple(block), lambda *i: i)
    return pl.pallas_call(
        body, name=name, grid=grid,
        in_specs=[pl.BlockSpec((nsrc,) + tuple(block), lambda *i: (0,) + i), own, own, own],
        out_specs=[own] * 4,
        out_shape=[SDS(w.shape, F32)] * 4,
        compiler_params=_cparams(("parallel",) * len(grid)),
    )(parts, w, m, v)


RELAYOUT_ROWS = 256
SHARD_COLS = IN_COLS // N_DEV


def _win_gather_layout_call(shards, layer):
    def body(w_ref, o_ref):
        nat = jnp.concatenate([w_ref[k, 0].astype(F32) for k in range(N_DEV)], axis=1)
        out = jnp.concatenate([nat[:, :3072], nat[:, 3080:], nat[:, 3072:3080],
                               jnp.zeros((RELAYOUT_ROWS, PW - IN_COLS), F32)], axis=1)
        o_ref[...] = out.astype(BF16)

    return pl.pallas_call(
        body, name="w_in_layout", grid=(D // RELAYOUT_ROWS,),
        in_specs=[pl.BlockSpec((N_DEV, 1, RELAYOUT_ROWS, SHARD_COLS), lambda i: (0, layer, i, 0))],
        out_specs=pl.BlockSpec((RELAYOUT_ROWS, PW), lambda i: (i, 0)),
        out_shape=SDS((D, PW), BF16),
        compiler_params=_cparams(("parallel",)),
    )(shards)


def _win_scatter_layout_call(grads):
    def body(*refs):
        g_refs, o_ref = refs[:DEPTH], refs[DEPTH]
        for l in range(DEPTH):
            @pl.when(pl.program_id(0) == l)
            def _(l=l):
                g = g_refs[l][...]
                nat = jnp.concatenate([g[:, :3072], g[:, AB_COL:AB_COL + 8], g[:, 3072:AB_COL]], axis=1)
                for k in range(N_DEV):
                    o_ref[k, 0] = nat[:, SHARD_COLS * k:SHARD_COLS * (k + 1)].astype(BF16)

    def in_spec(l):
        return pl.BlockSpec((RELAYOUT_ROWS, PW), lambda m, i: (jnp.where(m == l, i, 0), 0))

    return pl.pallas_call(
        body, name="w_in_grad_layout", grid=(DEPTH, D // RELAYOUT_ROWS),
        in_specs=[in_spec(l) for l in range(DEPTH)],
        out_specs=pl.BlockSpec((N_DEV, 1, RELAYOUT_ROWS, SHARD_COLS), lambda m, i: (0, m, i, 0)),
        out_shape=SDS((N_DEV, DEPTH, D, SHARD_COLS), BF16),
        compiler_params=_cparams(("arbitrary", "arbitrary")),
    )(*grads)


_BIG = ("w_in", "w_out", "a_pw_w", "s5_glu_w")
_CONV = ("a_conv_w", "c_conv_w", "d_conv_w")
_CONV_TAPS = {"a_conv_w": KA, "c_conv_w": KC, "d_conv_w": KD}
_CONV_ROWS = {"a_conv_w": HALO, "c_conv_w": HALO_S, "d_conv_w": HALO_S}
_CONV_WIDTH = {"a_conv_w": BR, "c_conv_w": BR, "d_conv_w": 3 * BR}
_REPLICATED = ("norm_g", "a_conv_b", "a_ln_g", "a_ln_b", "a_pw_b", "s5_lambda_re", "s5_lambda_im", "s5_b_re", "s5_b_im",
               "s5_c_re", "s5_c_im", "s5_d", "s5_log_dt", "s5_glu_b", "d_a_log", "d_dt_bias", "d_norm_g", "final_g")
_WEIGHTS = ("norm_g", "w_in", "a_conv_w", "a_conv_b", "a_ln_g", "a_ln_b", "a_pw_w", "a_pw_b", "s5_lambda_re",
            "s5_lambda_im", "s5_b_re", "s5_b_im", "s5_c_re", "s5_c_im", "s5_d", "s5_log_dt", "s5_glu_w", "s5_glu_b",
            "c_conv_w", "d_conv_w", "d_a_log", "d_dt_bias", "d_norm_g", "w_out", "final_g")


def _size(shape):
    n = 1
    for s in shape:
        n *= s
    return n


def _pack_rows(pieces, row_mult):
    flat = jnp.concatenate([p.reshape(-1) for p in pieces])
    per = 128 * row_mult
    pad = (-flat.shape[0]) % per
    return jnp.pad(flat, (0, pad)).reshape(-1, 128)


def _unpack(flat, shapes):
    out, off = [], 0
    for s in shapes:
        n = _size(s)
        out.append(flat[off:off + n].reshape(s))
        off += n
    return out


def _rows_of_peer(rows):
    return lambda ref, k: ref.at[:, pl.ds(k * rows, rows), :]


def _row_slot(ref, k):
    return ref.at[:, k]


def _gather_weights(weights):
    bf = {n: weights[n].astype(BF16) for n in _BIG}
    conv = _pack_rows([weights[n] for n in _CONV], 8)
    flows = [(bf["w_in"], _whole, (N_DEV,) + bf["w_in"].shape, _slot),
             (bf["w_out"], _whole, (DEPTH, N_DEV, D // N_DEV, D), _row_slot),
             (bf["a_pw_w"], _whole, (DEPTH, N_DEV, BR // N_DEV, BR), _row_slot),
             (bf["s5_glu_w"], _whole, (DEPTH, N_DEV, BR // N_DEV, BR), _row_slot),
             (conv, _whole, (N_DEV,) + conv.shape, _slot)]
    w_in, w_out, a_pw, glu, conv_all = _exchange_call("gather_weights", flows)
    full = {"w_in": [_win_gather_layout_call(w_in, l) for l in range(DEPTH)],
            "w_out": w_out.reshape(DEPTH, D, D), "a_pw_w": a_pw.reshape(DEPTH, BR, BR),
            "s5_glu_w": glu.reshape(DEPTH, BR, BR)}
    shapes = [weights[n].shape for n in _CONV]
    per_dev = [_unpack(conv_all[k].reshape(-1), shapes) for k in range(N_DEV)]
    for i, n in enumerate(_CONV):
        whole = jnp.concatenate([per_dev[k][i] for k in range(N_DEV)], axis=-1)
        full[n] = jnp.pad(whole, ((0, 0), (0, _CONV_ROWS[n] - _CONV_TAPS[n]), (0, 0)))
    return full


_S5_KERNEL_SHAPES = {"s5_lambda_re": (1, NSTATE), "s5_lambda_im": (1, NSTATE), "s5_log_dt": (1, 16),
                     "s5_b_re": (NSTATE, 16), "s5_b_im": (NSTATE, 16), "s5_c_re": (BR, 64), "s5_c_im": (BR, 64)}
_S5_KEYS = {"s5_lambda_re": "lam_re", "s5_lambda_im": "lam_im", "s5_log_dt": "log_dt", "s5_b_re": "b_re",
            "s5_b_im": "b_im", "s5_c_re": "c_re", "s5_c_im": "c_im"}


def _s5_inputs_all(weights):
    return {n: weights[n].reshape((DEPTH,) + s) for n, s in _S5_KERNEL_SHAPES.items()}


def _s5_inputs(p):
    return {_S5_KEYS[n]: p["s5_in"][n] for n in _S5_KERNEL_SHAPES}


def _row(a, width=None):
    a = a.reshape(1, -1)
    return a if width is None else jnp.pad(a, ((0, 0), (0, width - a.shape[1])))


def _layer_params(p):
    q = dict(p)
    for n in ("norm_g", "a_conv_b", "a_ln_g", "a_ln_b", "a_pw_b", "s5_d", "s5_glu_b", "d_norm_g"):
        q[n] = _row(p[n])
    q["d_a_log"] = _row(p["d_a_log"], 128)
    q["d_dt_bias"] = _row(p["d_dt_bias"], 128)
    return q


def _layer_fwd(x, p):
    q = _layer_params(p)
    proj, h = _inproj_call(x, q["norm_g"], q["w_in"])
    ya, yc = _ac_fwd_call(proj, q)
    prep = _s5_prep_call(_s5_inputs(p))
    yb, cin_r, cin_i = _s5_fwd_call(proj, prep, q)
    dq, dk, dv, dgb = _dn_pre_fwd_call(proj, q)
    yd, ssave = _dn_core_fwd_call(proj, dq, dk, dv, dgb, q["d_norm_g"])
    x_next = _outproj_call(x, (ya, yb, yc, yd), q["w_out"])
    saved = dict(x=x, proj=proj, h=h, ya=ya, yb=yb, yc=yc, yd=yd, cin_r=cin_r, cin_i=cin_i,
                 q=dq, k=dk, v=dv, gb=dgb, ssave=ssave, prep=prep)
    return x_next, saved


def _layer_bwd(dx, p, sv):
    q = _layer_params(p)
    proj = sv["proj"]
    dya, dyb, dyc, dyd, g_wout = _outproj_bwd_call(dx, (sv["ya"], sv["yb"], sv["yc"], sv["yd"]), q["w_out"])
    dpa, dpc, g_acw, g_acb, g_alg, g_alb, g_apw, g_apb, g_ccw = _ac_bwd_call(proj, q, dya, dyc)
    dpb, *s5g = _s5_bwd_call(proj, sv["prep"], q, sv["cin_r"], sv["cin_i"], dyb)
    g_sd, g_gw, g_gb = s5g[6:]
    g_lre, g_lim, g_ldt, g_bre, g_bim, g_cre, g_cim = _s5_prep_bwd_call(_s5_inputs(p), s5g[:6])
    dq, dk, dv, dgb, dz, g_ng = _dn_core_bwd_call(proj, sv["q"], sv["k"], sv["v"], sv["gb"], q["d_norm_g"], sv["ssave"], dyd)
    dqkv, dab, g_dcw, g_alog, g_dtb = _dn_pre_bwd_call(proj, q, dq, dk, dv, dgb)
    slabs = (dpa, dpb, dpc, dqkv, dz, dab)
    g_win = _inproj_bwd_w_call(sv["h"], slabs)
    dx_prev, g_ng0 = _inproj_bwd_x_call(slabs, q["w_in"], sv["x"], q["norm_g"], dx)
    grads = {"norm_g": g_ng0, "w_in": g_win, "a_conv_w": g_acw, "a_conv_b": g_acb, "a_ln_g": g_alg, "a_ln_b": g_alb,
             "a_pw_w": g_apw, "a_pw_b": g_apb, "s5_lambda_re": g_lre, "s5_lambda_im": g_lim, "s5_b_re": g_bre,
             "s5_b_im": g_bim, "s5_c_re": g_cre, "s5_c_im": g_cim, "s5_d": g_sd, "s5_log_dt": g_ldt, "s5_glu_w": g_gw,
             "s5_glu_b": g_gb, "c_conv_w": g_ccw, "d_conv_w": g_dcw, "d_a_log": g_alog[:, :NH], "d_dt_bias": g_dtb[:, :NH],
             "d_norm_g": g_ng, "w_out": g_wout}
    return dx_prev, grads


def _step(x, target, weights, moments_m, moments_v):
    full = _gather_weights(weights)
    layer_names = [n for n in _WEIGHTS if n != "final_g"]
    layers = [{n: (full[n][l] if n in full else weights[n][l]) for n in layer_names} for l in range(DEPTH)]
    s5_all = _s5_inputs_all(weights)
    for l in range(DEPTH):
        layers[l]["s5_in"] = {n: a[l] for n, a in s5_all.items()}
    saved, x_out = [], x
    for p in layers:
        x_out, sv = _layer_fwd(x_out, p)
        saved.append(sv)
    dx0, g_final, loss_part = _loss_call(x_out, _row(weights["final_g"]), target)
    per_layer = [None] * DEPTH
    for l in range(DEPTH - 1, -1, -1):
        dx0, per_layer[l] = _layer_bwd(dx0, layers[l], saved[l])
    stack = lambda n: jnp.stack([g[n] for g in per_layer])
    loss = lax.psum(loss_part[0, 0], ("x", "y", "c"))
    results = {}

    g_win = _win_scatter_layout_call([g["w_in"] for g in per_layer])
    half = lambda n: stack(n).astype(BF16)
    flows = [(g_win, _slot, g_win.shape, _slot),
             (half("w_out"), _rows_of_peer(D // N_DEV), (N_DEV, DEPTH, D // N_DEV, D), _slot),
             (half("a_pw_w"), _rows_of_peer(BR // N_DEV), (N_DEV, DEPTH, BR // N_DEV, BR), _slot),
             (half("s5_glu_w"), _rows_of_peer(BR // N_DEV), (N_DEV, DEPTH, BR // N_DEV, BR), _slot)]
    parts = _exchange_call("scatter_grads", flows)
    blocks = {"w_in": (1, RELAYOUT_ROWS, SHARD_COLS), "w_out": (1, D // N_DEV, D),
              "a_pw_w": (DEPTH, BR // N_DEV, BR), "s5_glu_w": (DEPTH, BR // N_DEV, BR)}
    for n, part in zip(_BIG, parts):
        results[n] = _reduce_adamw_call(part, weights[n], moments_m[n], moments_v[n], blocks[n], "adamw_" + n)

    grads = {n: stack(n).reshape(weights[n].shape) for n in _REPLICATED if n != "final_g"}
    grads["final_g"] = g_final.reshape(D)
    conv_g = [stack(n) for n in _CONV]
    pack = lambda d: _pack_rows([d[n] for n in _REPLICATED] + [jnp.zeros_like(c) for c in conv_g], 512)
    packed_g = _pack_rows([grads[n] for n in _REPLICATED] + conv_g, 512)
    gathered, = _exchange_call("gather_small_grads", [(packed_g, _whole, (N_DEV,) + packed_g.shape, _slot)])
    res = _reduce_adamw_call(gathered, pack(weights), pack(moments_m), pack(moments_v), (512, 128), "adamw_replicated")
    shapes = [weights[n].shape for n in _REPLICATED] + [c.shape for c in conv_g]
    res = [_unpack(r.reshape(-1), shapes) for r in res]
    for i, n in enumerate(_REPLICATED):
        results[n] = tuple(r[i] for r in res)

    me = 4 * lax.axis_index("x") + 2 * lax.axis_index("y") + lax.axis_index("c")
    own_g = []
    for i, n in enumerate(_CONV):
        width = _CONV_WIDTH[n] // N_DEV
        summed = res[0][len(_REPLICATED) + i][:, :_CONV_TAPS[n], :]
        own_g.append(lax.dynamic_slice_in_dim(summed, me * width, width, axis=2))
    packc = lambda arrs: _pack_rows(arrs, 8)
    res = _reduce_adamw_call(packc(own_g)[None], packc([weights[n] for n in _CONV]), packc([moments_m[n] for n in _CONV]),
                             packc([moments_v[n] for n in _CONV]), packc(own_g).shape, "adamw_conv")
    res = [_unpack(r.reshape(-1), [weights[n].shape for n in _CONV]) for r in res]
    for i, n in enumerate(_CONV):
        results[n] = tuple(r[i] for r in res)

    outs = [loss, dx0]
    for kind in range(4):
        outs += [results[n][kind] for n in _WEIGHTS]
    return tuple(outs)


def kernel(x, norm_g, w_in, a_conv_w, a_conv_b, a_ln_g, a_ln_b, a_pw_w, a_pw_b, s5_lambda_re, s5_lambda_im, s5_b_re, s5_b_im, s5_c_re, s5_c_im, s5_d, s5_log_dt, s5_glu_w, s5_glu_b, c_conv_w, d_conv_w, d_a_log, d_dt_bias, d_norm_g, w_out, final_g, loss_target, m_norm_g, m_w_in, m_a_conv_w, m_a_conv_b, m_a_ln_g, m_a_ln_b, m_a_pw_w, m_a_pw_b, m_s5_lambda_re, m_s5_lambda_im, m_s5_b_re, m_s5_b_im, m_s5_c_re, m_s5_c_im, m_s5_d, m_s5_log_dt, m_s5_glu_w, m_s5_glu_b, m_c_conv_w, m_d_conv_w, m_d_a_log, m_d_dt_bias, m_d_norm_g, m_w_out, m_final_g, v_norm_g, v_w_in, v_a_conv_w, v_a_conv_b, v_a_ln_g, v_a_ln_b, v_a_pw_w, v_a_pw_b, v_s5_lambda_re, v_s5_lambda_im, v_s5_b_re, v_s5_b_im, v_s5_c_re, v_s5_c_im, v_s5_d, v_s5_log_dt, v_s5_glu_w, v_s5_glu_b, v_c_conv_w, v_d_conv_w, v_d_a_log, v_d_dt_bias, v_d_norm_g, v_w_out, v_final_g):
    weights = dict(norm_g=norm_g, w_in=w_in, a_conv_w=a_conv_w, a_conv_b=a_conv_b, a_ln_g=a_ln_g, a_ln_b=a_ln_b, a_pw_w=a_pw_w, a_pw_b=a_pw_b, s5_lambda_re=s5_lambda_re, s5_lambda_im=s5_lambda_im, s5_b_re=s5_b_re, s5_b_im=s5_b_im, s5_c_re=s5_c_re, s5_c_im=s5_c_im, s5_d=s5_d, s5_log_dt=s5_log_dt, s5_glu_w=s5_glu_w, s5_glu_b=s5_glu_b, c_conv_w=c_conv_w, d_conv_w=d_conv_w, d_a_log=d_a_log, d_dt_bias=d_dt_bias, d_norm_g=d_norm_g, w_out=w_out, final_g=final_g)
    mom_m = dict(norm_g=m_norm_g, w_in=m_w_in, a_conv_w=m_a_conv_w, a_conv_b=m_a_conv_b, a_ln_g=m_a_ln_g, a_ln_b=m_a_ln_b, a_pw_w=m_a_pw_w, a_pw_b=m_a_pw_b, s5_lambda_re=m_s5_lambda_re, s5_lambda_im=m_s5_lambda_im, s5_b_re=m_s5_b_re, s5_b_im=m_s5_b_im, s5_c_re=m_s5_c_re, s5_c_im=m_s5_c_im, s5_d=m_s5_d, s5_log_dt=m_s5_log_dt, s5_glu_w=m_s5_glu_w, s5_glu_b=m_s5_glu_b, c_conv_w=m_c_conv_w, d_conv_w=m_d_conv_w, d_a_log=m_d_a_log, d_dt_bias=m_d_dt_bias, d_norm_g=m_d_norm_g, w_out=m_w_out, final_g=m_final_g)
    mom_v = dict(norm_g=v_norm_g, w_in=v_w_in, a_conv_w=v_a_conv_w, a_conv_b=v_a_conv_b, a_ln_g=v_a_ln_g, a_ln_b=v_a_ln_b, a_pw_w=v_a_pw_w, a_pw_b=v_a_pw_b, s5_lambda_re=v_s5_lambda_re, s5_lambda_im=v_s5_lambda_im, s5_b_re=v_s5_b_re, s5_b_im=v_s5_b_im, s5_c_re=v_s5_c_re, s5_c_im=v_s5_c_im, s5_d=v_s5_d, s5_log_dt=v_s5_log_dt, s5_glu_w=v_s5_glu_w, s5_glu_b=v_s5_glu_b, c_conv_w=v_c_conv_w, d_conv_w=v_d_conv_w, d_a_log=v_d_a_log, d_dt_bias=v_d_dt_bias, d_norm_g=v_d_norm_g, w_out=v_w_out, final_g=v_final_g)
    outs = _step(x[0], loss_target[0], weights, mom_m, mom_v)
    return (outs[0], outs[1][None]) + outs[2:]
```

```python
import functools

import jax
import jax.numpy as jnp
from jax import lax
from jax.experimental import pallas as pl
from jax.experimental.pallas import tpu as pltpu

F32 = jnp.float32
BF16 = jnp.bfloat16
HI = lax.Precision.HIGHEST
SDS = jax.ShapeDtypeStruct

N_DEV = 8
D = 1024
BR = 256
DEPTH = 4
IN_COLS = 3336
PW = 3456
AB_COL = 3328
EPS = 1e-6
TL = 512
SEG = TL // 8
HALO = 32
HALO_S = 8
KA, KC, KD = 31, 3, 4
CH = 64
DN_GROUP = 4
NH, HD = 4, 64
NSTATE = 1024
VMEM_LIMIT = 56 * 1024 * 1024

ADAM_LR, ADAM_B1, ADAM_B2, ADAM_EPS, ADAM_WD, ADAM_STEP = 0.001, 0.9, 0.999, 1e-08, 0.01, 10

NN = ((1,), (0,))
NT = ((1,), (1,))
TN = ((0,), (0,))


def _dot(a, b, dims, prec=None):
    return lax.dot_general(a, b, (dims, ((), ())), precision=prec, preferred_element_type=F32)


def _make_mm(cast, prec, fwd_dims):
    def prep(t):
        return t.astype(cast) if cast is not None else t

    @jax.custom_vjp
    def mm(a, w):
        return _dot(prep(a), prep(w), fwd_dims, prec)

    def fwd(a, w):
        return mm(a, w), (a, w)

    def bwd(res, dy):
        a, w = res
        a, w, dy = prep(a), prep(w), prep(dy)
        if fwd_dims == NN:
            return _dot(dy, w, NT, prec), _dot(a, dy, TN, prec)
        if fwd_dims == NT:
            return _dot(dy, w, NN, prec), _dot(dy, a, TN, prec)
        return _dot(w, dy, NT, prec), _dot(a, dy, NN, prec)

    mm.defvjp(fwd, bwd)
    return mm


mm = _make_mm(BF16, None, NN)
mm_nt = _make_mm(BF16, None, NT)
mm_tn = _make_mm(BF16, None, TN)
mmh = _make_mm(None, HI, NN)
mmh_nt = _make_mm(None, HI, NT)


def _sigmoid(x):
    return jax.nn.sigmoid(x)


def _silu(x):
    return x * jax.nn.sigmoid(x)


def _gelu(x):
    return 0.5 * x * (1.0 + jnp.tanh(0.7978845608028654 * (x + 0.044715 * (x * x * x))))


def _softplus(x):
    return jnp.maximum(x, 0.0) + jnp.log1p(jnp.exp(-jnp.abs(x)))


def _rms(x, g):
    return x * lax.rsqrt(jnp.mean(x * x, axis=-1, keepdims=True) + EPS) * g


def _cparams(sem):
    return pltpu.CompilerParams(dimension_semantics=sem, vmem_limit_bytes=VMEM_LIMIT)


def _tap_offsets(halo, taps):
    return [halo - (taps - 1) + k for k in range(taps)]


def _conv_fwd_impl(acat, w, tile, halo, taps):
    n = tile + halo
    out = None
    for k, off in enumerate(_tap_offsets(halo, taps)):
        src = jnp.roll(acat, n - off, axis=0)[:tile, :] if off != halo else acat[halo:, :]
        term = src * w[k:k + 1, :]
        out = term if out is None else out + term
    return out


def _make_conv(tile, halo, taps):
    @jax.custom_vjp
    def conv(acat, w):
        return _conv_fwd_impl(acat, w, tile, halo, taps)

    def fwd(acat, w):
        return conv(acat, w), (acat, w)

    def bwd(res, dy):
        acat, w = res
        n = tile + halo
        dyp = jnp.concatenate([dy, jnp.zeros((halo, dy.shape[1]), F32)], axis=0)
        rows = lax.broadcasted_iota(jnp.int32, w.shape, 0)
        dacat = None
        dw = jnp.zeros(w.shape, F32)
        for k, off in enumerate(_tap_offsets(halo, taps)):
            term = jnp.roll(dyp, off, axis=0) * w[k:k + 1, :]
            dacat = term if dacat is None else dacat + term
            src = jnp.roll(acat, n - off, axis=0)[:tile, :] if off != halo else acat[halo:, :]
            dw = dw + jnp.where(rows == k, jnp.sum(dy * src, axis=0, keepdims=True), 0.0)
        return dacat, dw

    conv.defvjp(fwd, bwd)
    return conv


def _halo_spec(tile, halo, width, col):
    per = tile // halo
    return pl.BlockSpec((halo, width), lambda i: (jnp.maximum(i * per - 1, 0), col))


def _halo_spec_rev(nt, tile, halo, width, col):
    per = tile // halo
    return pl.BlockSpec((halo, width), lambda i: (jnp.maximum((nt - 1 - i) * per - 1, 0), col))


def _inproj_call(x, g, w):
    L = x.shape[0]
    tn = 1152

    def body(x_ref, g_ref, w_ref, p_ref, h_ref):
        h = _rms(x_ref[...], g_ref[...]).astype(BF16)
        h_ref[...] = h
        p_ref[...] = _dot(h, w_ref[...], NN)

    return pl.pallas_call(
        body, name="inproj", grid=(L // TL, PW // tn),
        in_specs=[pl.BlockSpec((TL, D), lambda i, j: (i, 0)), pl.BlockSpec((1, D), lambda i, j: (0, 0)),
                  pl.BlockSpec((D, tn), lambda i, j: (0, j))],
        out_specs=[pl.BlockSpec((TL, tn), lambda i, j: (i, j)), pl.BlockSpec((TL, D), lambda i, j: (i, 0))],
        out_shape=[SDS((L, PW), F32), SDS((L, D), BF16)],
        compiler_params=_cparams(("parallel", "arbitrary")),
    )(x, g, w)


def _outproj_call(x, ys, w):
    L = x.shape[0]

    def body(x_ref, a_ref, b_ref, c_ref, d_ref, w_ref, o_ref):
        acc = x_ref[...]
        for b, y_ref in enumerate((a_ref, b_ref, c_ref, d_ref)):
            acc = acc + _dot(y_ref[...].astype(BF16), w_ref[b * BR:(b + 1) * BR, :], NN)
        o_ref[...] = acc

    yspec = pl.BlockSpec((TL, BR), lambda i: (i, 0))
    return pl.pallas_call(
        body, name="outproj", grid=(L // TL,),
        in_specs=[pl.BlockSpec((TL, D), lambda i: (i, 0)), yspec, yspec, yspec, yspec,
                  pl.BlockSpec((D, D), lambda i: (0, 0))],
        out_specs=pl.BlockSpec((TL, D), lambda i: (i, 0)),
        out_shape=SDS((L, D), F32),
        compiler_params=_cparams(("parallel",)),
    )(x, *ys, w)


def _loss_call(x, g, target):
    L = x.shape[0]

    def body(x_ref, g_ref, t_ref, dx_ref, dg_ref, loss_ref):
        @pl.when(pl.program_id(0) == 0)
        def _():
            dg_ref[...] = jnp.zeros_like(dg_ref)
            loss_ref[...] = jnp.zeros_like(loss_ref)

        y, vjp = jax.vjp(_rms, x_ref[...], g_ref[...])
        err = y - t_ref[...]
        dx, dg = vjp(err * (1.0 / D))
        dx_ref[...] = dx
        dg_ref[...] += dg
        tot = jnp.sum(jnp.sum(err * err, axis=1, keepdims=True), axis=0, keepdims=True)
        loss_ref[...] += jnp.broadcast_to(tot * (0.5 / D), loss_ref.shape)

    return pl.pallas_call(
        body, name="loss_head", grid=(L // TL,),
        in_specs=[pl.BlockSpec((TL, D), lambda i: (i, 0)), pl.BlockSpec((1, D), lambda i: (0, 0)),
                  pl.BlockSpec((TL, D), lambda i: (i, 0))],
        out_specs=[pl.BlockSpec((TL, D), lambda i: (i, 0)), pl.BlockSpec((1, D), lambda i: (0, 0)),
                   pl.BlockSpec((1, 128), lambda i: (0, 0))],
        out_shape=[SDS((L, D), F32), SDS((1, D), F32), SDS((1, 128), F32)],
        compiler_params=_cparams(("arbitrary",)),
    )(x, g, target)


def _branch_a(valw, gatew, z, cw, cb, lg, lb, pw, pb, conv):
    a = conv(valw * _sigmoid(gatew), cw) + cb
    mu = jnp.mean(a, axis=-1, keepdims=True)
    xc = a - mu
    y = xc * lax.rsqrt(jnp.mean(xc * xc, axis=-1, keepdims=True) + EPS) * lg + lb
    y = mm(_silu(y), pw) + pb
    return y * _silu(z)


def _branch_c(bg, cw_, xw, z, w3, conv):
    return bg * conv(cw_ * xw, w3) * _silu(z)


def _ac_fwd_call(proj, p):
    L = proj.shape[0]

    def body(val, gate, za, hval, hgate, cb_, cc, cx, cz, hcc, hcx,
             acw, acb, alg, alb, apw, apb, ccw, ya_ref, yc_ref):
        nf = (pl.program_id(0) > 0).astype(F32)
        win = lambda h, m: jnp.concatenate([h[...] * nf, m[...]], axis=0)
        conv_a = functools.partial(_conv_fwd_impl, tile=TL, halo=HALO, taps=KA)
        conv_c = functools.partial(_conv_fwd_impl, tile=TL, halo=HALO, taps=KC)
        ya_ref[...] = _branch_a(win(hval, val), win(hgate, gate), za[...], acw[...], acb[...], alg[...], alb[...],
                                apw[...], apb[...], conv_a)
        yc_ref[...] = _branch_c(cb_[...], win(hcc, cc), win(hcx, cx), cz[...], ccw[...], conv_c)

    col = lambda j: pl.BlockSpec((TL, BR), lambda i: (i, j))
    hal = lambda j: _halo_spec(TL, HALO, BR, j)
    full = lambda a: pl.BlockSpec(a.shape, lambda i: (0,) * a.ndim)
    params = (p["a_conv_w"], p["a_conv_b"], p["a_ln_g"], p["a_ln_b"], p["a_pw_w"], p["a_pw_b"], p["c_conv_w"])
    return pl.pallas_call(
        body, name="ac_fwd", grid=(L // TL,),
        in_specs=[col(0), col(1), col(2), hal(0), hal(1), col(5), col(6), col(7), col(8), hal(6), hal(7)]
        + [full(a) for a in params],
        out_specs=[pl.BlockSpec((TL, BR), lambda i: (i, 0))] * 2,
        out_shape=[SDS((L, BR), F32)] * 2,
        compiler_params=_cparams(("parallel",)),
    )(*([proj] * 11), *params)


def _ac_bwd_call(proj, p, dya, dyc):
    L = proj.shape[0]
    nt = L // TL

    def body(val, gate, za, hval, hgate, cb_, cc, cx, cz, hcc, hcx,
             acw, acb, alg, alb, apw, apb, ccw, dya_ref, dyc_ref,
             da_ref, dc_ref, g_acw, g_acb, g_alg, g_alb, g_apw, g_apb, g_ccw, carry):
        i = pl.program_id(0)
        gouts = (g_acw, g_acb, g_alg, g_alb, g_apw, g_apb, g_ccw)

        @pl.when(i == 0)
        def _():
            carry[...] = jnp.zeros_like(carry)
            for r in gouts:
                r[...] = jnp.zeros_like(r)

        nf = (i < nt - 1).astype(F32)
        win = lambda h, m: jnp.concatenate([h[...] * nf, m[...]], axis=0)
        conv_a = _make_conv(TL, HALO, KA)
        conv_c = _make_conv(TL, HALO, KC)

        def f(valw, gatew, z, bg, ccw_, cxw, czv, w1, b1, lg, lb, pw, pb, w3):
            return (_branch_a(valw, gatew, z, w1, b1, lg, lb, pw, pb, conv_a),
                    _branch_c(bg, ccw_, cxw, czv, w3, conv_c))

        _, vjp = jax.vjp(f, win(hval, val), win(hgate, gate), za[...], cb_[...], win(hcc, cc), win(hcx, cx), cz[...],
                         acw[...], acb[...], alg[...], alb[...], apw[...].astype(F32), apb[...], ccw[...])
        (dvalw, dgatew, dz, dbg, dccw, dcxw, dczv, d1, d2, d3, d4, d5, d6, d7) = vjp((dya_ref[...], dyc_ref[...]))

        def settle(slot, dwin):
            tail = jnp.concatenate([jnp.zeros((TL - HALO, BR), F32), carry[slot]], axis=0)
            carry[slot] = dwin[:HALO, :]
            return (dwin[HALO:, :] + tail).astype(BF16)

        da_ref[:, 0:BR] = settle(0, dvalw)
        da_ref[:, BR:2 * BR] = settle(1, dgatew)
        da_ref[:, 2 * BR:3 * BR] = dz.astype(BF16)
        dc_ref[:, 0:BR] = dbg.astype(BF16)
        dc_ref[:, BR:2 * BR] = settle(2, dccw)
        dc_ref[:, 2 * BR:3 * BR] = settle(3, dcxw)
        dc_ref[:, 3 * BR:4 * BR] = dczv.astype(BF16)
        for r, g in zip(gouts, (d1, d2, d3, d4, d5, d6, d7)):
            r[...] += g

    col = lambda j: pl.BlockSpec((TL, BR), lambda i: (nt - 1 - i, j))
    hal = lambda j: _halo_spec_rev(nt, TL, HALO, BR, j)
    full = lambda a: pl.BlockSpec(a.shape, lambda i: (0,) * a.ndim)
    params = (p["a_conv_w"], p["a_conv_b"], p["a_ln_g"], p["a_ln_b"], p["a_pw_w"], p["a_pw_b"], p["c_conv_w"])
    rev = lambda w: pl.BlockSpec((TL, w), lambda i: (nt - 1 - i, 0))
    return pl.pallas_call(
        body, name="ac_bwd", grid=(nt,),
        in_specs=[col(0), col(1), col(2), hal(0), hal(1), col(5), col(6), col(7), col(8), hal(6), hal(7)]
        + [full(a) for a in params] + [rev(BR), rev(BR)],
        out_specs=[rev(3 * BR), rev(4 * BR)] + [full(a) for a in params],
        out_shape=[SDS((L, 3 * BR), BF16), SDS((L, 4 * BR), BF16)] + [SDS(a.shape, F32) for a in params],
        scratch_shapes=[pltpu.VMEM((4, HALO, BR), F32)],
        compiler_params=_cparams(("arbitrary",)),
    )(*([proj] * 11), *params, dya, dyc)


def _iota2(shape, dim):
    return lax.broadcasted_iota(jnp.int32, shape, dim)


def _s5_params(lam_re, lam_im, logdt, b_re, b_im, c_re, c_im):
    eg = (_iota2((16, NSTATE), 1) >> 6 == _iota2((16, NSTATE), 0)).astype(F32)
    dt = jnp.exp(mmh(jnp.broadcast_to(logdt, (8, 16)), eg)[0:1, :])
    lr = jnp.minimum(lam_re, -1e-4)
    li = lam_im
    mag = jnp.exp(lr * dt)
    lbr = mag * jnp.cos(li * dt)
    lbi = mag * jnp.sin(li * dt)
    den = lr * lr + li * li
    nr = lbr - 1.0
    fr = (nr * lr + lbi * li) / den
    fi = (lbi * lr - nr * li) / den
    row = _iota2((8, NSTATE), 0)
    f8 = jnp.where(row == 0, fr, jnp.where(row == 1, fi, 0.0))
    eye = (_iota2((NSTATE, NSTATE), 0) == _iota2((NSTATE, NSTATE), 1)).astype(F32)
    fcol = mmh_nt(eye, f8)
    frc, fic = fcol[:, 0:1], fcol[:, 1:2]
    bbr = frc * b_re - fic * b_im
    bbi = frc * b_im + fic * b_re
    e1 = ((_iota2((16, BR), 1) & 15) == _iota2((16, BR), 0)).astype(F32)
    m1 = ((_iota2((NSTATE, BR), 0) >> 6) == (_iota2((NSTATE, BR), 1) >> 4)).astype(F32)
    wbr = mmh(bbr, e1) * m1
    wbi = mmh(bbi, e1) * m1
    e2 = ((_iota2((64, NSTATE), 1) & 63) == _iota2((64, NSTATE), 0)).astype(F32)
    m2 = ((_iota2((BR, NSTATE), 0) >> 4) == (_iota2((BR, NSTATE), 1) >> 6)).astype(F32)
    wcr = mmh(c_re, e2) * m2
    wci = mmh(c_im, e2) * m2
    return lbr, lbi, wbr, wbi, wcr, wci


_S5_OUT = [(1, NSTATE), (1, NSTATE), (NSTATE, BR), (NSTATE, BR), (BR, NSTATE), (BR, NSTATE)]


def _s5_prep_call(sp):
    def body(lre, lim, ldt, bre, bim, cre, cim, o_lbr, o_lbi, o_wbr, o_wbi, o_wcr, o_wci, pwr, pwi, qwr, qwi):
        lbr, lbi, wbr, wbi, wcr, wci = _s5_params(lre[...], lim[...], ldt[...], bre[...], bim[...], cre[...], cim[...])
        o_lbr[...], o_lbi[...], o_wbr[...], o_wbi[...], o_wcr[...], o_wci[...] = lbr, lbi, wbr, wbi, wcr, wci
        pr, pi = lbr, lbi
        for i in range(SEG):
            pwr[i:i + 1, :] = pr
            pwi[i:i + 1, :] = pi
            qwr[SEG - 1 - i:SEG - i, :] = pr
            qwi[SEG - 1 - i:SEG - i, :] = -pi
            pr, pi = pr * lbr - pi * lbi, pr * lbi + pi * lbr

    args = (sp["lam_re"], sp["lam_im"], sp["log_dt"], sp["b_re"], sp["b_im"], sp["c_re"], sp["c_im"])
    return pl.pallas_call(
        body, name="s5_prep",
        out_shape=[SDS(s, F32) for s in _S5_OUT] + [SDS((SEG, NSTATE), F32)] * 4,
        compiler_params=pltpu.CompilerParams(vmem_limit_bytes=VMEM_LIMIT),
    )(*args)


def _s5_prep_bwd_call(sp, cots):
    def body(lre, lim, ldt, bre, bim, cre, cim, c0, c1, c2, c3, c4, c5, *outs):
        _, vjp = jax.vjp(_s5_params, lre[...], lim[...], ldt[...], bre[...], bim[...], cre[...], cim[...])
        grads = vjp((c0[...], c1[...], c2[...], c3[...], c4[...], c5[...]))
        for o, g in zip(outs, grads):
            o[...] = g

    args = (sp["lam_re"], sp["lam_im"], sp["log_dt"], sp["b_re"], sp["b_im"], sp["c_re"], sp["c_im"])
    return pl.pallas_call(
        body, name="s5_prep_bwd",
        out_shape=[SDS(a.shape, F32) for a in args],
        compiler_params=pltpu.CompilerParams(vmem_limit_bytes=VMEM_LIMIT),
    )(*args, *cots)


def _lanes(v, j):
    return v[:, j * 128:(j + 1) * 128]


def _s5_scan(sre, sim, lbr, lbi, pwr, pwi, cin_r, cin_i, reverse):
    lr = [jnp.broadcast_to(_lanes(lbr, j), (8, 128)) for j in range(8)]
    li = [jnp.broadcast_to(_lanes(lbi, j), (8, 128)) for j in range(8)]

    def step(t, st):
        i = SEG - 1 - t if reverse else t
        rows = pl.ds(i, 8, stride=SEG)
        new = []
        for j in range(8):
            sr, si = st[2 * j], st[2 * j + 1]
            vr, vi = sre.at[j], sim.at[j]
            nr = lr[j] * sr - li[j] * si + vr[rows, :]
            ni = lr[j] * si + li[j] * sr + vi[rows, :]
            vr[rows, :] = nr
            vi[rows, :] = ni
            new += [nr, ni]
        return tuple(new)

    ends = lax.fori_loop(0, SEG, step, tuple(jnp.zeros((8, 128), F32) for _ in range(16)))
    e_r = jnp.concatenate([ends[2 * j] for j in range(8)], axis=1)
    e_i = jnp.concatenate([ends[2 * j + 1] for j in range(8)], axis=1)
    l64r, l64i = (pwr[0:1, :], pwi[0:1, :]) if reverse else (pwr[SEG - 1:SEG, :], pwi[SEG - 1:SEG, :])
    order = range(7, -1, -1) if reverse else range(8)
    c_r, c_i = cin_r, cin_i
    carries = {}
    for k in order:
        carries[k] = (c_r, c_i)
        er, ei = e_r[k:k + 1, :], e_i[k:k + 1, :]
        c_r, c_i = er + l64r * c_r - l64i * c_i, ei + l64r * c_i + l64i * c_r
    for j in range(8):
        pr, pi = pwr[:, j * 128:(j + 1) * 128], pwi[:, j * 128:(j + 1) * 128]
        for k in range(8):
            cr, ci = _lanes(carries[k][0], j), _lanes(carries[k][1], j)
            blk = slice(k * SEG, (k + 1) * SEG)
            sre[j, blk, :] = sre[j, blk, :] + pr * cr - pi * ci
            sim[j, blk, :] = sim[j, blk, :] + pr * ci + pi * cr
    return c_r, c_i


def _bdot(a, b, dims):
    return _dot(a.astype(BF16), b.astype(BF16), dims)


def _s5_states(u, wbr, wbi, sre, sim):
    bur = _bdot(u, wbr, NT)
    bui = _bdot(u, wbi, NT)
    for j in range(8):
        sre[j] = _lanes(bur, j)
        sim[j] = _lanes(bui, j)


def _gather_lanes(s):
    return jnp.concatenate([s[j] for j in range(8)], axis=1)


def _s5_post(s_re, s_im, u, z, wcr, wci, dsk, gw, gb):
    y = mm_nt(s_re, wcr) - mm_nt(s_im, wci) + dsk * u
    yg = _gelu(y)
    return yg * _sigmoid(mm(yg, gw) + gb) * _silu(z)


def _s5_fwd_call(proj, prep, p):
    L = proj.shape[0]
    nt = L // TL
    lbr, lbi, wbr, wbi, wcr, wci, pwr, pwi, _, _ = prep

    def body(u_ref, z_ref, lbr_r, lbi_r, wbr_r, wbi_r, wcr_r, wci_r, pwr_r, pwi_r, d_r, gw_r, gb_r,
             yb_ref, cinr_ref, cini_ref, sre, sim, car, cai):
        @pl.when(pl.program_id(0) == 0)
        def _():
            car[...] = jnp.zeros_like(car)
            cai[...] = jnp.zeros_like(cai)

        u = u_ref[...]
        cinr_ref[0] = car[...]
        cini_ref[0] = cai[...]
        _s5_states(u, wbr_r[...], wbi_r[...], sre, sim)
        nr, ni = _s5_scan(sre, sim, lbr_r[...], lbi_r[...], pwr_r, pwi_r, car[...], cai[...], False)
        car[...] = nr
        cai[...] = ni
        yb_ref[...] = _s5_post(_gather_lanes(sre), _gather_lanes(sim), u, z_ref[...], wcr_r[...], wci_r[...],
                               d_r[...], gw_r[...], gb_r[...])

    full = lambda a: pl.BlockSpec(a.shape, lambda i: (0,) * a.ndim)
    consts = (lbr, lbi, wbr, wbi, wcr, wci, pwr, pwi, p["s5_d"], p["s5_glu_w"], p["s5_glu_b"])
    cspec = pl.BlockSpec((1, 1, NSTATE), lambda i: (i, 0, 0))
    return pl.pallas_call(
        body, name="s5_fwd", grid=(nt,),
        in_specs=[pl.BlockSpec((TL, BR), lambda i: (i, 3)), pl.BlockSpec((TL, BR), lambda i: (i, 4))]
        + [full(a) for a in consts],
        out_specs=[pl.BlockSpec((TL, BR), lambda i: (i, 0)), cspec, cspec],
        out_shape=[SDS((L, BR), F32), SDS((nt, 1, NSTATE), F32), SDS((nt, 1, NSTATE), F32)],
        scratch_shapes=[pltpu.VMEM((8, TL, 128), F32), pltpu.VMEM((8, TL, 128), F32),
                        pltpu.VMEM((1, NSTATE), F32), pltpu.VMEM((1, NSTATE), F32)],
        compiler_params=_cparams(("arbitrary",)),
    )(proj, proj, *consts)


def _s5_bwd_call(proj, prep, p, cin_r, cin_i, dyb):
    L = proj.shape[0]
    nt = L // TL
    lbr, lbi, wbr, wbi, wcr, wci, pwr, pwi, qwr, qwi = prep

    def body(u_ref, z_ref, lbr_r, lbi_r, wbr_r, wbi_r, wcr_r, wci_r, pwr_r, pwi_r, qwr_r, qwi_r, d_r, gw_r, gb_r,
             cinr_ref, cini_ref, dy_ref,
             db_ref, g_lbr, g_lbi, g_wbr, g_wbi, g_wcr, g_wci, g_d, g_gw, g_gb, sre, sim, gre, gim, car, cai):
        gouts = (g_lbr, g_lbi, g_wbr, g_wbi, g_wcr, g_wci, g_d, g_gw, g_gb)

        @pl.when(pl.program_id(0) == 0)
        def _():
            car[...] = jnp.zeros_like(car)
            cai[...] = jnp.zeros_like(cai)
            for r in gouts:
                r[...] = jnp.zeros_like(r)

        u = u_ref[...]
        lr, li = lbr_r[...], lbi_r[...]
        c0r, c0i = cinr_ref[0], cini_ref[0]
        _s5_states(u, wbr_r[...], wbi_r[...], sre, sim)
        _s5_scan(sre, sim, lr, li, pwr_r, pwi_r, c0r, c0i, False)
        s_re, s_im = _gather_lanes(sre), _gather_lanes(sim)
        _, vjp = jax.vjp(_s5_post, s_re, s_im, u, z_ref[...], wcr_r[...], wci_r[...], d_r[...],
                         gw_r[...].astype(F32), gb_r[...])
        ds_re, ds_im, du, dz, dwcr, dwci, dd, dgw, dgb = vjp(dy_ref[...])
        for j in range(8):
            gre[j] = _lanes(ds_re, j)
            gim[j] = _lanes(ds_im, j)
        nr, ni = _s5_scan(gre, gim, lr, -li, qwr_r, qwi_r, car[...], cai[...], True)
        car[...] = nr
        cai[...] = ni
        a_re, a_im = _gather_lanes(gre), _gather_lanes(gim)
        first = _iota2((TL, NSTATE), 0) == 0
        p_re = jnp.where(first, c0r, jnp.roll(s_re, 1, axis=0))
        p_im = jnp.where(first, c0i, jnp.roll(s_im, 1, axis=0))
        g_lbr[...] += jnp.sum(a_re * p_re + a_im * p_im, axis=0, keepdims=True)
        g_lbi[...] += jnp.sum(a_im * p_re - a_re * p_im, axis=0, keepdims=True)
        du = du + _bdot(a_re, wbr_r[...], NN) + _bdot(a_im, wbi_r[...], NN)
        g_wbr[...] += _bdot(a_re, u, TN)
        g_wbi[...] += _bdot(a_im, u, TN)
        g_wcr[...] += dwcr
        g_wci[...] += dwci
        g_d[...] += dd
        g_gw[...] += dgw
        g_gb[...] += dgb
        db_ref[:, 0:BR] = du.astype(BF16)
        db_ref[:, BR:2 * BR] = dz.astype(BF16)

    full = lambda a: pl.BlockSpec(a.shape, lambda i: (0,) * a.ndim)
    consts = (lbr, lbi, wbr, wbi, wcr, wci, pwr, pwi, qwr, qwi, p["s5_d"], p["s5_glu_w"], p["s5_glu_b"])
    cspec = pl.BlockSpec((1, 1, NSTATE), lambda i: (nt - 1 - i, 0, 0))
    gshapes = _S5_OUT + [(1, BR), (BR, BR), (1, BR)]
    return pl.pallas_call(
        body, name="s5_bwd", grid=(nt,),
        in_specs=[pl.BlockSpec((TL, BR), lambda i: (nt - 1 - i, 3)), pl.BlockSpec((TL, BR), lambda i: (nt - 1 - i, 4))]
        + [full(a) for a in consts] + [cspec, cspec, pl.BlockSpec((TL, BR), lambda i: (nt - 1 - i, 0))],
        out_specs=[pl.BlockSpec((TL, 2 * BR), lambda i: (nt - 1 - i, 0))]
        + [pl.BlockSpec(s, lambda i: (0, 0)) for s in gshapes],
        out_shape=[SDS((L, 2 * BR), BF16)] + [SDS(s, F32) for s in gshapes],
        scratch_shapes=[pltpu.VMEM((8, TL, 128), F32)] * 4 + [pltpu.VMEM((1, NSTATE), F32)] * 2,
        compiler_params=_cparams(("arbitrary",)),
    )(proj, proj, *consts, cin_r, cin_i, dyb)


def _heads(x):
    return [x[:, h * HD:(h + 1) * HD] for h in range(NH)]


def _l2n(x, scale):
    return jnp.concatenate([xh * (lax.rsqrt(jnp.sum(xh * xh, axis=-1, keepdims=True) + EPS) * scale)
                            for xh in _heads(x)], axis=1)


def _dn_pre(qkvw, ab, cw, alog, dtb, conv, rows):
    c = _silu(conv(qkvw, cw))
    q = _l2n(c[:, 0:BR], HD ** -0.5)
    k = _l2n(c[:, BR:2 * BR], 1.0)
    v = c[:, 2 * BR:3 * BR]
    g = -jnp.exp(alog) * _softplus(ab + dtb)
    ri, ci = _iota2((rows, rows), 0), _iota2((rows, rows), 1)
    tri = ((ri >= ci) & ((ri >> 6) == (ci >> 6))).astype(F32)
    gc = mmh(tri, g)
    lane = _iota2(ab.shape, 1)
    return q, k, v, jnp.where(lane < NH, gc, jnp.where(lane < 2 * NH, _sigmoid(ab), 0.0))


def _dn_pre_fwd_call(proj, p):
    L = proj.shape[0]

    def body(m_ref, h_ref, ab_ref, cw, alog, dtb, q_ref, k_ref, v_ref, gb_ref):
        nf = (pl.program_id(0) > 0).astype(F32)
        qkvw = jnp.concatenate([h_ref[...] * nf, m_ref[...]], axis=0)
        conv = functools.partial(_conv_fwd_impl, tile=TL, halo=HALO_S, taps=KD)
        q_ref[...], k_ref[...], v_ref[...], gb_ref[...] = _dn_pre(qkvw, ab_ref[...], cw[...], alog[...], dtb[...], conv, TL)

    full = lambda a: pl.BlockSpec(a.shape, lambda i: (0,) * a.ndim)
    params = (p["d_conv_w"], p["d_a_log"], p["d_dt_bias"])
    o = pl.BlockSpec((TL, BR), lambda i: (i, 0))
    return pl.pallas_call(
        body, name="dn_pre_fwd", grid=(L // TL,),
        in_specs=[pl.BlockSpec((TL, 3 * BR), lambda i: (i, 3)), _halo_spec(TL, HALO_S, 3 * BR, 3),
                  pl.BlockSpec((TL, 128), lambda i: (i, AB_COL // 128))] + [full(a) for a in params],
        out_specs=[o, o, o, pl.BlockSpec((TL, 128), lambda i: (i, 0))],
        out_shape=[SDS((L, BR), F32)] * 3 + [SDS((L, 128), F32)],
        compiler_params=_cparams(("parallel",)),
    )(proj, proj, proj, *params)


def _dn_pre_bwd_call(proj, p, dq, dk, dv, dgb):
    L = proj.shape[0]
    nt = L // TL

    def body(m_ref, h_ref, ab_ref, cw, alog, dtb, dq_r, dk_r, dv_r, dgb_r,
             dqkv_ref, dab_ref, g_cw, g_alog, g_dtb, carry):
        i = pl.program_id(0)

        @pl.when(i == 0)
        def _():
            carry[...] = jnp.zeros_like(carry)
            for r in (g_cw, g_alog, g_dtb):
                r[...] = jnp.zeros_like(r)

        nf = (i < nt - 1).astype(F32)
        qkvw = jnp.concatenate([h_ref[...] * nf, m_ref[...]], axis=0)
        conv = _make_conv(TL, HALO_S, KD)
        _, vjp = jax.vjp(lambda a, b, c, d, e: _dn_pre(a, b, c, d, e, conv, TL),
                         qkvw, ab_ref[...], cw[...], alog[...], dtb[...])
        dwin, dab, dcw, dalog, ddtb = vjp((dq_r[...], dk_r[...], dv_r[...], dgb_r[...]))
        tail = jnp.concatenate([jnp.zeros((TL - HALO_S, 3 * BR), F32), carry[...]], axis=0)
        carry[...] = dwin[:HALO_S, :]
        dqkv_ref[...] = (dwin[HALO_S:, :] + tail).astype(BF16)
        dab_ref[...] = dab.astype(BF16)
        g_cw[...] += dcw
        g_alog[...] += dalog
        g_dtb[...] += ddtb

    full = lambda a: pl.BlockSpec(a.shape, lambda i: (0,) * a.ndim)
    params = (p["d_conv_w"], p["d_a_log"], p["d_dt_bias"])
    rev = lambda w: pl.BlockSpec((TL, w), lambda i: (nt - 1 - i, 0))
    return pl.pallas_call(
        body, name="dn_pre_bwd", grid=(nt,),
        in_specs=[pl.BlockSpec((TL, 3 * BR), lambda i: (nt - 1 - i, 3)), _halo_spec_rev(nt, TL, HALO_S, 3 * BR, 3),
                  pl.BlockSpec((TL, 128), lambda i: (nt - 1 - i, AB_COL // 128))] + [full(a) for a in params]
        + [rev(BR), rev(BR), rev(BR), rev(128)],
        out_specs=[rev(3 * BR), rev(128)] + [full(a) for a in params],
        out_shape=[SDS((L, 3 * BR), BF16), SDS((L, 128), BF16)] + [SDS(a.shape, F32) for a in params],
        scratch_shapes=[pltpu.VMEM((HALO_S, 3 * BR), F32)],
        compiler_params=_cparams(("arbitrary",)),
    )(proj, proj, proj, *params, dq, dk, dv, dgb)


def _dn_group(q, k, v, gb, z, ng, *s):
    ri, ci = _iota2((CH, CH), 0), _iota2((CH, CH), 1)
    causal, strict = ri >= ci, ri > ci
    eye = (ri == ci).astype(F32)
    s = list(s)
    pairs = []
    for c in range(DN_GROUP):
        rows = slice(c * CH, (c + 1) * CH)
        gbc = gb[rows, :]
        for h, (qh, kh, vh, zh) in enumerate(zip(_heads(q[rows, :]), _heads(k[rows, :]), _heads(v[rows, :]),
                                                 _heads(z[rows, :]))):
            gc = jnp.broadcast_to(gbc[:, h:h + 1], (CH, HD))
            beta = gbc[:, NH + h:NH + h + 1]
            decay = jnp.where(causal, jnp.exp(jnp.where(causal, gc - gc.T, 0.0)), 0.0)
            egc = jnp.exp(gc)
            glast = gc[CH - 1:CH, :]
            kb = kh * beta
            pairs.append(dict(q=qh, k=kh, z=zh, decay=decay, qe=qh * egc, kd=kh * jnp.exp(glast - gc),
                              sdec=jnp.exp(glast[:, 0:1]), kb=kb, rhs=jnp.concatenate([vh * beta, kb * egc], axis=1)))
    for p in pairs:
        p["pw"] = jnp.where(strict, mm_nt(p["kb"], p["k"]) * p["decay"], 0.0)
        p["t"] = eye - p["pw"]
    for _ in range(5):
        for p in pairs:
            p["pw"] = mm(p["pw"], p["pw"])
        for p in pairs:
            p["t"] = mm(p["t"], eye + p["pw"])
    for p in pairs:
        p["uw"] = mm(p["t"], p["rhs"])
    for p in pairs:
        p["attn"] = mm_nt(p["q"], p["k"]) * p["decay"]
    out_rows = []
    for c in range(DN_GROUP):
        grp = pairs[c * NH:(c + 1) * NH]
        ws = [mm(jnp.concatenate([p["uw"][:, HD:], p["qe"]], axis=0), s[h]) for h, p in enumerate(grp)]
        v_new = [p["uw"][:, :HD] - w_[:CH, :] for p, w_ in zip(grp, ws)]
        o = [w_[CH:, :] + mm(p["attn"], vn) for p, w_, vn in zip(grp, ws, v_new)]
        s = [s[h] * p["sdec"] + mm_tn(p["kd"], vn) for h, (p, vn) in enumerate(zip(grp, v_new))]
        o = [oh * lax.rsqrt(jnp.mean(oh * oh, axis=-1, keepdims=True) + EPS) * ng * _silu(p["z"]) for oh, p in zip(o, grp)]
        out_rows.append(jnp.concatenate(o, axis=1))
    return (jnp.concatenate(out_rows, axis=0), *s)


def _dn_core_fwd_call(proj, q, k, v, gb, ng):
    L = q.shape[0]
    rows = DN_GROUP * CH
    ng_ = L // rows

    def body(q_r, k_r, v_r, gb_r, z_r, ng_r, yd_ref, ssave_ref, s_scr):
        @pl.when(pl.program_id(0) == 0)
        def _():
            s_scr[...] = jnp.zeros_like(s_scr)

        ssave_ref[0] = s_scr[...]
        yd, *s2 = _dn_group(q_r[...], k_r[...], v_r[...], gb_r[...], z_r[...], ng_r[...], *[s_scr[h] for h in range(NH)])
        yd_ref[...] = yd
        for h in range(NH):
            s_scr[h] = s2[h]

    c = pl.BlockSpec((rows, BR), lambda i: (i, 0))
    return pl.pallas_call(
        body, name="dn_core_fwd", grid=(ng_,),
        in_specs=[c, c, c, pl.BlockSpec((rows, 128), lambda i: (i, 0)), pl.BlockSpec((rows, BR), lambda i: (i, 12)),
                  pl.BlockSpec((1, HD), lambda i: (0, 0))],
        out_specs=[c, pl.BlockSpec((1, NH, HD, HD), lambda i: (i, 0, 0, 0))],
        out_shape=[SDS((L, BR), F32), SDS((ng_, NH, HD, HD), F32)],
        scratch_shapes=[pltpu.VMEM((NH, HD, HD), F32)],
        compiler_params=_cparams(("arbitrary",)),
    )(q, k, v, gb, proj, ng)


def _dn_core_bwd_call(proj, q, k, v, gb, ng, ssave, dyd):
    L = q.shape[0]
    rows = DN_GROUP * CH
    ng_ = L // rows

    def body(q_r, k_r, v_r, gb_r, z_r, ng_r, s_r, dy_r, dq_ref, dk_ref, dv_ref, dgb_ref, dz_ref, g_ng, ds_scr):
        @pl.when(pl.program_id(0) == 0)
        def _():
            ds_scr[...] = jnp.zeros_like(ds_scr)
            g_ng[...] = jnp.zeros_like(g_ng)

        _, vjp = jax.vjp(_dn_group, q_r[...], k_r[...], v_r[...], gb_r[...], z_r[...], ng_r[...],
                         *[s_r[0, h] for h in range(NH)])
        dq, dk, dv, dgb, dz, dng, *ds = vjp((dy_r[...], *[ds_scr[h] for h in range(NH)]))
        dq_ref[...], dk_ref[...], dv_ref[...], dgb_ref[...] = dq, dk, dv, dgb
        dz_ref[...] = dz.astype(BF16)
        g_ng[...] += dng
        for h in range(NH):
            ds_scr[h] = ds[h]

    c = pl.BlockSpec((rows, BR), lambda i: (ng_ - 1 - i, 0))
    c128 = pl.BlockSpec((rows, 128), lambda i: (ng_ - 1 - i, 0))
    return pl.pallas_call(
        body, name="dn_core_bwd", grid=(ng_,),
        in_specs=[c, c, c, c128, pl.BlockSpec((rows, BR), lambda i: (ng_ - 1 - i, 12)),
                  pl.BlockSpec((1, HD), lambda i: (0, 0)),
                  pl.BlockSpec((1, NH, HD, HD), lambda i: (ng_ - 1 - i, 0, 0, 0)), c],
        out_specs=[c, c, c, c128, c, pl.BlockSpec((1, HD), lambda i: (0, 0))],
        out_shape=[SDS((L, BR), F32)] * 3 + [SDS((L, 128), F32), SDS((L, BR), BF16), SDS((1, HD), F32)],
        scratch_shapes=[pltpu.VMEM((NH, HD, HD), F32)],
        compiler_params=_cparams(("arbitrary",)),
    )(q, k, v, gb, proj, ng, ssave, dyd)


def _outproj_bwd_call(dx, ys, w):
    L = dx.shape[0]

    def body(dx_ref, a_ref, b_ref, c_ref, d_ref, w_ref, da, db, dc, dd, dw_ref):
        @pl.when(pl.program_id(0) == 0)
        def _():
            dw_ref[...] = jnp.zeros_like(dw_ref)

        dxb = dx_ref[...].astype(BF16)
        for b, (y_ref, o_ref) in enumerate(zip((a_ref, b_ref, c_ref, d_ref), (da, db, dc, dd))):
            o_ref[...] = _dot(dxb, w_ref[b * BR:(b + 1) * BR, :], NT)
            dw_ref[b * BR:(b + 1) * BR, :] += _dot(y_ref[...].astype(BF16), dxb, TN)

    yspec = pl.BlockSpec((TL, BR), lambda i: (i, 0))
    return pl.pallas_call(
        body, name="outproj_bwd", grid=(L // TL,),
        in_specs=[pl.BlockSpec((TL, D), lambda i: (i, 0)), yspec, yspec, yspec, yspec,
                  pl.BlockSpec((D, D), lambda i: (0, 0))],
        out_specs=[yspec] * 4 + [pl.BlockSpec((D, D), lambda i: (0, 0))],
        out_shape=[SDS((L, BR), F32)] * 4 + [SDS((D, D), F32)],
        compiler_params=_cparams(("arbitrary",)),
    )(dx, *ys, w)


def _slab_cols(slabs):
    widths = [s.shape[1] for s in slabs]
    starts = [sum(widths[:i]) for i in range(len(widths))]
    assert starts[-1] + widths[-1] == PW
    return list(zip(starts, widths))


def _inproj_bwd_x_call(slabs, w, x, g, dx_next):
    L = x.shape[0]
    cols = _slab_cols(slabs)
    n = len(slabs)

    def body(*refs):
        dp_refs, (w_ref, x_ref, g_ref, dxn_ref, dx_ref, dg_ref) = refs[:n], refs[n:]

        @pl.when(pl.program_id(0) == 0)
        def _():
            dg_ref[...] = jnp.zeros_like(dg_ref)

        dh = None
        for dp_ref, (c0, cw) in zip(dp_refs, cols):
            part = _dot(dp_ref[...], w_ref[:, c0:c0 + cw], NT)
            dh = part if dh is None else dh + part
        _, vjp = jax.vjp(_rms, x_ref[...], g_ref[...])
        dx, dg = vjp(dh)
        dx_ref[...] = dx + dxn_ref[...]
        dg_ref[...] += dg

    row = lambda w_: pl.BlockSpec((TL, w_), lambda i: (i, 0))
    return pl.pallas_call(
        body, name="inproj_bwd_x", grid=(L // TL,),
        in_specs=[row(cw) for _, cw in cols]
        + [pl.BlockSpec((D, PW), lambda i: (0, 0)), row(D), pl.BlockSpec((1, D), lambda i: (0, 0)), row(D)],
        out_specs=[row(D), pl.BlockSpec((1, D), lambda i: (0, 0))],
        out_shape=[SDS((L, D), F32), SDS((1, D), F32)],
        compiler_params=_cparams(("arbitrary",)),
    )(*slabs, w, x, g, dx_next)


def _inproj_bwd_w_call(h, slabs):
    L = h.shape[0]
    cols = _slab_cols(slabs)
    n = len(slabs)

    def body(*refs):
        h_ref, dp_refs, dw_ref = refs[0], refs[1:1 + n], refs[1 + n]

        @pl.when(pl.program_id(0) == 0)
        def _():
            dw_ref[...] = jnp.zeros_like(dw_ref)

        hv = h_ref[...]
        for dp_ref, (c0, cw) in zip(dp_refs, cols):
            dw_ref[:, c0:c0 + cw] += _dot(hv, dp_ref[...], TN)

    row = lambda w_: pl.BlockSpec((TL, w_), lambda i: (i, 0))
    return pl.pallas_call(
        body, name="inproj_bwd_w", grid=(L // TL,),
        in_specs=[row(D)] + [row(cw) for _, cw in cols],
        out_specs=pl.BlockSpec((D, PW), lambda i: (0, 0)),
        out_shape=SDS((D, PW), F32),
        compiler_params=_cparams(("arbitrary",)),
    )(h, *slabs)


def _exchange_call(name, flows):
    n = len(flows)

    def body(*refs):
        srcs, dsts = refs[:n], refs[n:2 * n]
        send_sems, recv_sems, local_sems = refs[2 * n:]
        x, y, c = lax.axis_index("x"), lax.axis_index("y"), lax.axis_index("c")
        me = 4 * x + 2 * y + c
        copies = []
        for mask in range(1, N_DEV):
            px = 1 - x if mask & 4 else x
            py = 1 - y if mask & 2 else y
            pc = 1 - c if mask & 1 else c
            for f, (_, src_at, _, dst_at) in enumerate(flows):
                cp = pltpu.make_async_remote_copy(
                    src_ref=src_at(srcs[f], 4 * px + 2 * py + pc), dst_ref=dst_at(dsts[f], me),
                    send_sem=send_sems.at[mask - 1, f], recv_sem=recv_sems.at[mask - 1, f],
                    device_id=(px, py, pc), device_id_type=pl.DeviceIdType.MESH)
                cp.start()
                copies.append(cp)
        mine = [pltpu.make_async_copy(src_at(srcs[f], me), dst_at(dsts[f], me), local_sems.at[f])
                for f, (_, src_at, _, dst_at) in enumerate(flows)]
        for cp in mine:
            cp.start()
        for cp in copies + mine:
            cp.wait()

    return pl.pallas_call(
        body, name=name,
        in_specs=[pl.BlockSpec(memory_space=pl.ANY)] * n,
        out_specs=[pl.BlockSpec(memory_space=pl.ANY)] * n,
        out_shape=[SDS(tuple(shape), src.dtype) for src, _, shape, _ in flows],
        scratch_shapes=[pltpu.SemaphoreType.DMA((N_DEV - 1, n)), pltpu.SemaphoreType.DMA((N_DEV - 1, n)),
                        pltpu.SemaphoreType.DMA((n,))],
    )(*[f[0] for f in flows])


def _whole(ref, _):
    return ref


def _slot(ref, k):
    return ref.at[k]


_HBM_SPEC = pl.BlockSpec(memory_space=pltpu.HBM)
_SEM_SPEC = pl.BlockSpec(memory_space=pltpu.SEMAPHORE)
_DATAFLOW = pltpu.SideEffectType.DATAFLOW_SIDE_EFFECTING


def _split_copies(views, src_refs, land_refs, send_sems, recv_sems):
    x, y, c = lax.axis_index("x"), lax.axis_index("y"), lax.axis_index("c")
    me = 4 * x + 2 * y + c
    copies = []
    for mask in range(1, N_DEV):
        px = 1 - x if mask & 4 else x
        py = 1 - y if mask & 2 else y
        pc = 1 - c if mask & 1 else c
        for f, (src_at, dst_at) in enumerate(views):
            pair = (mask - 1) * len(views) + f
            copies.append(pltpu.make_async_remote_copy(
                src_ref=src_at(src_refs[f], 4 * px + 2 * py + pc), dst_ref=dst_at(land_refs[f], me),
                send_sem=send_sems.at[pair], recv_sem=recv_sems.at[pair],
                device_id=(px, py, pc), device_id_type=pl.DeviceIdType.MESH))
    return copies


def _split_start_call(name, srcs, lands, views):
    n = len(srcs)

    def body(*refs):
        src_refs, land_refs = refs[:n], refs[n:2 * n]
        send_sems, recv_sems, token = refs[2 * n], refs[2 * n + 1], refs[-1]
        for cp in _split_copies(views, src_refs, land_refs, send_sems, recv_sems):
            cp.start()
        token[...] = jnp.zeros_like(token)

    arrays = list(srcs) + list(lands)
    outs = pl.pallas_call(
        body, name=name,
        out_shape=(pltpu.SemaphoreType.DMA(((N_DEV - 1) * n,)), pltpu.SemaphoreType.DMA(((N_DEV - 1) * n,)),
                   *[pltpu.HBM(a.shape, a.dtype) for a in arrays], SDS((8, 128), F32)),
        in_specs=[_HBM_SPEC] * (2 * n),
        out_specs=(_SEM_SPEC, _SEM_SPEC, *[_HBM_SPEC] * (2 * n), pl.BlockSpec(memory_space=pltpu.VMEM)),
        input_output_aliases={i: 2 + i for i in range(2 * n)},
        compiler_params=pltpu.CompilerParams(has_side_effects=_DATAFLOW),
    )(*[pltpu.with_memory_space_constraint(a, pltpu.HBM) for a in arrays])
    return outs[0], outs[1], list(outs[2:2 + 2 * n]), outs[-1]


def _split_wait_call(name, send_sems, recv_sems, thru, views, after):
    n = len(views)

    def body(*refs):
        src_refs, land_refs = refs[:n], refs[n:2 * n]
        send, recv = refs[2 * n], refs[2 * n + 1]
        for cp in _split_copies(views, src_refs, land_refs, send, recv):
            cp.wait_send()
            cp.wait_recv()

    outs = pl.pallas_call(
        body, name=name,
        out_shape=tuple(pltpu.HBM(a.shape, a.dtype) for a in thru),
        in_specs=[_HBM_SPEC] * (2 * n) + [_SEM_SPEC, _SEM_SPEC, pl.BlockSpec(memory_space=pl.ANY)],
        out_specs=tuple([_HBM_SPEC] * (2 * n)),
        input_output_aliases={i: i for i in range(2 * n)},
        compiler_params=pltpu.CompilerParams(has_side_effects=_DATAFLOW),
    )(*thru, send_sems, recv_sems, after)
    return list(outs[n:])


def _own_slot(block, me):
    zone = jnp.zeros((N_DEV,) + block.shape, block.dtype)
    return lax.dynamic_update_slice(zone, block[None], (me,) + (0,) * block.ndim)


def _reduce_adamw_call(parts, w, m, v, block, name):
    nsrc = parts.shape[0]
    grid = tuple(s // b for s, b in zip(w.shape, block))
    c1 = 1.0 - ADAM_B1 ** ADAM_STEP
    c2 = 1.0 - ADAM_B2 ** ADAM_STEP

    def body(p_ref, w_ref, m_ref, v_ref, g_ref, d_ref, nm_ref, nv_ref):
        g = p_ref[0].astype(F32)
        for k in range(1, nsrc):
            g = g + p_ref[k].astype(F32)
        nm = ADAM_B1 * m_ref[...] + (1.0 - ADAM_B1) * g
        nv = ADAM_B2 * v_ref[...] + (1.0 - ADAM_B2) * (g * g)
        g_ref[...] = g
        nm_ref[...] = nm
        nv_ref[...] = nv
        d_ref[...] = -ADAM_LR * ((nm / c1) / (jnp.sqrt(nv / c2) + ADAM_EPS) + ADAM_WD * w_ref[...])

    own = pl.BlockSpec(tuple(block), lambda *i: i)
    return pl.pallas_call(
        body, name=name, grid=grid,
        in_specs=[pl.BlockSpec((nsrc,) + tuple(block), lambda *i: (0,) + i), own, own, own],
        out_specs=[own] * 4,
        out_shape=[SDS(w.shape, F32)] * 4,
        compiler_params=_cparams(("parallel",) * len(grid)),
    )(parts, w, m, v)


RELAYOUT_ROWS = 256
SHARD_COLS = IN_COLS // N_DEV


def _win_gather_layout_call(shards):
    def body(w_ref, o_ref):
        nat = jnp.concatenate([w_ref[k].astype(F32) for k in range(N_DEV)], axis=1)
        out = jnp.concatenate([nat[:, :3072], nat[:, 3080:], nat[:, 3072:3080],
                               jnp.zeros((RELAYOUT_ROWS, PW - IN_COLS), F32)], axis=1)
        o_ref[...] = out.astype(BF16)

    return pl.pallas_call(
        body, name="w_in_layout", grid=(D // RELAYOUT_ROWS,),
        in_specs=[pl.BlockSpec((N_DEV, RELAYOUT_ROWS, SHARD_COLS), lambda i: (0, i, 0))],
        out_specs=pl.BlockSpec((RELAYOUT_ROWS, PW), lambda i: (i, 0)),
        out_shape=SDS((D, PW), BF16),
        compiler_params=_cparams(("parallel",)),
    )(shards)


def _win_scatter_layout_call(grad):
    def body(g_ref, o_ref):
        g = g_ref[...]
        nat = jnp.concatenate([g[:, :3072], g[:, AB_COL:AB_COL + 8], g[:, 3072:AB_COL]], axis=1)
        for k in range(N_DEV):
            o_ref[k] = nat[:, SHARD_COLS * k:SHARD_COLS * (k + 1)].astype(BF16)

    return pl.pallas_call(
        body, name="w_in_grad_layout", grid=(D // RELAYOUT_ROWS,),
        in_specs=[pl.BlockSpec((RELAYOUT_ROWS, PW), lambda i: (i, 0))],
        out_specs=pl.BlockSpec((N_DEV, RELAYOUT_ROWS, SHARD_COLS), lambda i: (0, i, 0)),
        out_shape=SDS((N_DEV, D, SHARD_COLS), BF16),
        compiler_params=_cparams(("parallel",)),
    )(grad)


def _reduce_adamw_layers_call(parts, w, m, v, rows, name):
    _, R, C = w.shape
    c1 = 1.0 - ADAM_B1 ** ADAM_STEP
    c2 = 1.0 - ADAM_B2 ** ADAM_STEP

    def body(*refs):
        p_refs = refs[:DEPTH]
        w_ref, m_ref, v_ref, g_ref, d_ref, nm_ref, nv_ref = refs[DEPTH:]
        for l in range(DEPTH):
            @pl.when(pl.program_id(0) == l)
            def _(l=l):
                g = p_refs[l][0].astype(F32)
                for k in range(1, N_DEV):
                    g = g + p_refs[l][k].astype(F32)
                nm = ADAM_B1 * m_ref[0] + (1.0 - ADAM_B1) * g
                nv = ADAM_B2 * v_ref[0] + (1.0 - ADAM_B2) * (g * g)
                g_ref[0] = g
                nm_ref[0] = nm
                nv_ref[0] = nv
                d_ref[0] = -ADAM_LR * ((nm / c1) / (jnp.sqrt(nv / c2) + ADAM_EPS) + ADAM_WD * w_ref[0])

    def part_spec(l):
        return pl.BlockSpec((N_DEV, rows, C), lambda j, i: (0, jnp.where(j == l, i, 0), 0))

    own = pl.BlockSpec((1, rows, C), lambda j, i: (j, i, 0))
    return pl.pallas_call(
        body, name=name, grid=(DEPTH, R // rows),
        in_specs=[part_spec(l) for l in range(DEPTH)] + [own, own, own],
        out_specs=[own] * 4,
        out_shape=[SDS(w.shape, F32)] * 4,
        compiler_params=_cparams(("arbitrary", "arbitrary")),
    )(*parts, w, m, v)


_BIG = ("w_in", "w_out", "a_pw_w", "s5_glu_w")
_CONV = ("a_conv_w", "c_conv_w", "d_conv_w")
_CONV_TAPS = {"a_conv_w": KA, "c_conv_w": KC, "d_conv_w": KD}
_CONV_ROWS = {"a_conv_w": HALO, "c_conv_w": HALO_S, "d_conv_w": HALO_S}
_CONV_WIDTH = {"a_conv_w": BR, "c_conv_w": BR, "d_conv_w": 3 * BR}
_REPLICATED = ("norm_g", "a_conv_b", "a_ln_g", "a_ln_b", "a_pw_b", "s5_lambda_re", "s5_lambda_im", "s5_b_re", "s5_b_im",
               "s5_c_re", "s5_c_im", "s5_d", "s5_log_dt", "s5_glu_b", "d_a_log", "d_dt_bias", "d_norm_g", "final_g")
_WEIGHTS = ("norm_g", "w_in", "a_conv_w", "a_conv_b", "a_ln_g", "a_ln_b", "a_pw_w", "a_pw_b", "s5_lambda_re",
            "s5_lambda_im", "s5_b_re", "s5_b_im", "s5_c_re", "s5_c_im", "s5_d", "s5_log_dt", "s5_glu_w", "s5_glu_b",
            "c_conv_w", "d_conv_w", "d_a_log", "d_dt_bias", "d_norm_g", "w_out", "final_g")


def _size(shape):
    n = 1
    for s in shape:
        n *= s
    return n


PACK_ALIGN = 1024


def _piece_rows(n):
    return -(-n // PACK_ALIGN) * (PACK_ALIGN // 128)


def _pack_rows(pieces, row_mult):
    rows = []
    for p in pieces:
        flat = p.reshape(-1)
        rows.append(jnp.pad(flat, (0, _piece_rows(flat.shape[0]) * 128 - flat.shape[0])).reshape(-1, 128))
    out = jnp.concatenate(rows, axis=0)
    return jnp.pad(out, ((0, (-out.shape[0]) % row_mult), (0, 0)))


def _unpack(packed, shapes):
    out, row = [], 0
    for s in shapes:
        n = _size(s)
        nr = _piece_rows(n)
        out.append(packed[row:row + nr].reshape(-1)[:n].reshape(s))
        row += nr
    return out


_GATHER_VIEWS = [(_whole, _slot)] * 5


def _gather_start(weights, layer, me):
    srcs = [weights[n][layer].astype(BF16) for n in _BIG]
    srcs.append(_pack_rows([weights[n][layer] for n in _CONV], 8))
    lands = [_own_slot(s, me) for s in srcs]
    return _split_start_call("gather_start_%d" % layer, srcs, lands, _GATHER_VIEWS)


def _gather_finish(weights, layer, started, after):
    send, recv, thru, _ = started
    w_in, w_out, a_pw, glu, conv_all = _split_wait_call("gather_wait_%d" % layer, send, recv, thru, _GATHER_VIEWS, after)
    full = {"w_in": _win_gather_layout_call(w_in), "w_out": w_out.reshape(D, D), "a_pw_w": a_pw.reshape(BR, BR),
            "s5_glu_w": glu.reshape(BR, BR)}
    shapes = [weights[n].shape[1:] for n in _CONV]
    per_dev = [_unpack(conv_all[k], shapes) for k in range(N_DEV)]
    for i, n in enumerate(_CONV):
        whole = jnp.concatenate([per_dev[k][i] for k in range(N_DEV)], axis=-1)
        full[n] = jnp.pad(whole, ((0, _CONV_ROWS[n] - _CONV_TAPS[n]), (0, 0)))
    return full


def _rows_view(rows):
    return lambda ref, k: ref.at[pl.ds(k * rows, rows), :]


_SCATTER_VIEWS = [(_slot, _slot), (_rows_view(D // N_DEV), _slot), (_rows_view(BR // N_DEV), _slot),
                  (_rows_view(BR // N_DEV), _slot)]


def _scatter_start(grads, layer, me):
    srcs = [_win_scatter_layout_call(grads["w_in"])] + [grads[n].astype(BF16) for n in _BIG[1:]]
    own = [lax.dynamic_index_in_dim(srcs[0], me, 0, keepdims=False)]
    for s, rows in zip(srcs[1:], (D // N_DEV, BR // N_DEV, BR // N_DEV)):
        own.append(lax.dynamic_slice_in_dim(s, me * rows, rows, axis=0))
    lands = [_own_slot(o, me) for o in own]
    return _split_start_call("scatter_start_%d" % layer, srcs, lands, _SCATTER_VIEWS)


_S5_KERNEL_SHAPES = {"s5_lambda_re": (1, NSTATE), "s5_lambda_im": (1, NSTATE), "s5_log_dt": (1, 16),
                     "s5_b_re": (NSTATE, 16), "s5_b_im": (NSTATE, 16), "s5_c_re": (BR, 64), "s5_c_im": (BR, 64)}
_S5_KEYS = {"s5_lambda_re": "lam_re", "s5_lambda_im": "lam_im", "s5_log_dt": "log_dt", "s5_b_re": "b_re",
            "s5_b_im": "b_im", "s5_c_re": "c_re", "s5_c_im": "c_im"}


def _s5_inputs_all(weights):
    return {n: weights[n].reshape((DEPTH,) + s) for n, s in _S5_KERNEL_SHAPES.items()}


def _s5_inputs(p):
    return {_S5_KEYS[n]: p["s5_in"][n] for n in _S5_KERNEL_SHAPES}


def _row(a, width=None):
    a = a.reshape(1, -1)
    return a if width is None else jnp.pad(a, ((0, 0), (0, width - a.shape[1])))


def _layer_params(p):
    q = dict(p)
    for n in ("norm_g", "a_conv_b", "a_ln_g", "a_ln_b", "a_pw_b", "s5_d", "s5_glu_b", "d_norm_g"):
        q[n] = _row(p[n])
    q["d_a_log"] = _row(p["d_a_log"], 128)
    q["d_dt_bias"] = _row(p["d_dt_bias"], 128)
    return q


def _layer_fwd(x, p):
    q = _layer_params(p)
    proj, h = _inproj_call(x, q["norm_g"], q["w_in"])
    ya, yc = _ac_fwd_call(proj, q)
    prep = _s5_prep_call(_s5_inputs(p))
    yb, cin_r, cin_i = _s5_fwd_call(proj, prep, q)
    dq, dk, dv, dgb = _dn_pre_fwd_call(proj, q)
    yd, ssave = _dn_core_fwd_call(proj, dq, dk, dv, dgb, q["d_norm_g"])
    x_next = _outproj_call(x, (ya, yb, yc, yd), q["w_out"])
    saved = dict(x=x, proj=proj, h=h, ya=ya, yb=yb, yc=yc, yd=yd, cin_r=cin_r, cin_i=cin_i,
                 q=dq, k=dk, v=dv, gb=dgb, ssave=ssave, prep=prep)
    return x_next, saved


def _layer_bwd(dx, p, sv):
    q = _layer_params(p)
    proj = sv["proj"]
    dya, dyb, dyc, dyd, g_wout = _outproj_bwd_call(dx, (sv["ya"], sv["yb"], sv["yc"], sv["yd"]), q["w_out"])
    dpa, dpc, g_acw, g_acb, g_alg, g_alb, g_apw, g_apb, g_ccw = _ac_bwd_call(proj, q, dya, dyc)
    dpb, *s5g = _s5_bwd_call(proj, sv["prep"], q, sv["cin_r"], sv["cin_i"], dyb)
    g_sd, g_gw, g_gb = s5g[6:]
    g_lre, g_lim, g_ldt, g_bre, g_bim, g_cre, g_cim = _s5_prep_bwd_call(_s5_inputs(p), s5g[:6])
    dq, dk, dv, dgb, dz, g_ng = _dn_core_bwd_call(proj, sv["q"], sv["k"], sv["v"], sv["gb"], q["d_norm_g"], sv["ssave"], dyd)
    dqkv, dab, g_dcw, g_alog, g_dtb = _dn_pre_bwd_call(proj, q, dq, dk, dv, dgb)
    slabs = (dpa, dpb, dpc, dqkv, dz, dab)
    g_win = _inproj_bwd_w_call(sv["h"], slabs)
    dx_prev, g_ng0 = _inproj_bwd_x_call(slabs, q["w_in"], sv["x"], q["norm_g"], dx)
    grads = {"norm_g": g_ng0, "w_in": g_win, "a_conv_w": g_acw, "a_conv_b": g_acb, "a_ln_g": g_alg, "a_ln_b": g_alb,
             "a_pw_w": g_apw, "a_pw_b": g_apb, "s5_lambda_re": g_lre, "s5_lambda_im": g_lim, "s5_b_re": g_bre,
             "s5_b_im": g_bim, "s5_c_re": g_cre, "s5_c_im": g_cim, "s5_d": g_sd, "s5_log_dt": g_ldt, "s5_glu_w": g_gw,
             "s5_glu_b": g_gb, "c_conv_w": g_ccw, "d_conv_w": g_dcw, "d_a_log": g_alog[:, :NH], "d_dt_bias": g_dtb[:, :NH],
             "d_norm_g": g_ng, "w_out": g_wout}
    return dx_prev, grads


def _step(x, target, weights, moments_m, moments_v):
    me = 4 * lax.axis_index("x") + 2 * lax.axis_index("y") + lax.axis_index("c")
    layer_names = [n for n in _WEIGHTS if n != "final_g"]
    s5_all = _s5_inputs_all(weights)

    gathers = [_gather_start(weights, l, me) for l in range(DEPTH)]
    x_out = x + sum(g[3][0, 0] for g in gathers)
    layers, saved = [], []
    for l in range(DEPTH):
        full = _gather_finish(weights, l, gathers[l], x_out)
        p = {n: (full[n] if n in full else weights[n][l]) for n in layer_names}
        p["s5_in"] = {n: a[l] for n, a in s5_all.items()}
        layers.append(p)
        x_out, sv = _layer_fwd(x_out, p)
        saved.append(sv)
    dx0, g_final, loss_part = _loss_call(x_out, _row(weights["final_g"]), target)

    per_layer, scatters = [None] * DEPTH, [None] * DEPTH
    for l in range(DEPTH - 1, -1, -1):
        dx0, per_layer[l] = _layer_bwd(dx0, layers[l], saved[l])
        scatters[l] = _scatter_start(per_layer[l], l, me)
        dx0 = dx0 + scatters[l][3][0, 0]
    stack = lambda n: jnp.stack([g[n] for g in per_layer])
    loss = lax.psum(loss_part[0, 0], ("x", "y", "c"))
    results = {}

    parts = [_split_wait_call("scatter_wait_%d" % l, scatters[l][0], scatters[l][1], scatters[l][2], _SCATTER_VIEWS, dx0)
             for l in range(DEPTH - 1, -1, -1)][::-1]
    rows = {"w_in": RELAYOUT_ROWS, "w_out": D // N_DEV, "a_pw_w": BR // N_DEV, "s5_glu_w": BR // N_DEV}
    for i, n in enumerate(_BIG):
        results[n] = _reduce_adamw_layers_call([parts[l][i] for l in range(DEPTH)], weights[n], moments_m[n],
                                               moments_v[n], rows[n], "adamw_" + n)

    grads = {n: stack(n).reshape(weights[n].shape) for n in _REPLICATED if n != "final_g"}
    grads["final_g"] = g_final.reshape(D)
    conv_g = [stack(n) for n in _CONV]
    pack = lambda d: _pack_rows([d[n] for n in _REPLICATED] + [jnp.zeros_like(c) for c in conv_g], 512)
    packed_g = _pack_rows([grads[n] for n in _REPLICATED] + conv_g, 512)
    gathered, = _exchange_call("gather_small_grads", [(packed_g, _whole, (N_DEV,) + packed_g.shape, _slot)])
    res = _reduce_adamw_call(gathered, pack(weights), pack(moments_m), pack(moments_v), (512, 128), "adamw_replicated")
    shapes = [weights[n].shape for n in _REPLICATED] + [c.shape for c in conv_g]
    res = [_unpack(r, shapes) for r in res]
    for i, n in enumerate(_REPLICATED):
        results[n] = tuple(r[i] for r in res)

    own_g = []
    for i, n in enumerate(_CONV):
        width = _CONV_WIDTH[n] // N_DEV
        summed = res[0][len(_REPLICATED) + i][:, :_CONV_TAPS[n], :]
        own_g.append(lax.dynamic_slice_in_dim(summed, me * width, width, axis=2))
    packc = lambda arrs: _pack_rows(arrs, 8)
    res = _reduce_adamw_call(packc(own_g)[None], packc([weights[n] for n in _CONV]), packc([moments_m[n] for n in _CONV]),
                             packc([moments_v[n] for n in _CONV]), packc(own_g).shape, "adamw_conv")
    res = [_unpack(r, [weights[n].shape for n in _CONV]) for r in res]
    for i, n in enumerate(_CONV):
        results[n] = tuple(r[i] for r in res)

    outs = [loss, dx0]
    for kind in range(4):
        outs += [results[n][kind] for n in _WEIGHTS]
    return tuple(outs)


def kernel(x, norm_g, w_in, a_conv_w, a_conv_b, a_ln_g, a_ln_b, a_pw_w, a_pw_b, s5_lambda_re, s5_lambda_im, s5_b_re, s5_b_im, s5_c_re, s5_c_im, s5_d, s5_log_dt, s5_glu_w, s5_glu_b, c_conv_w, d_conv_w, d_a_log, d_dt_bias, d_norm_g, w_out, final_g, loss_target, m_norm_g, m_w_in, m_a_conv_w, m_a_conv_b, m_a_ln_g, m_a_ln_b, m_a_pw_w, m_a_pw_b, m_s5_lambda_re, m_s5_lambda_im, m_s5_b_re, m_s5_b_im, m_s5_c_re, m_s5_c_im, m_s5_d, m_s5_log_dt, m_s5_glu_w, m_s5_glu_b, m_c_conv_w, m_d_conv_w, m_d_a_log, m_d_dt_bias, m_d_norm_g, m_w_out, m_final_g, v_norm_g, v_w_in, v_a_conv_w, v_a_conv_b, v_a_ln_g, v_a_ln_b, v_a_pw_w, v_a_pw_b, v_s5_lambda_re, v_s5_lambda_im, v_s5_b_re, v_s5_b_im, v_s5_c_re, v_s5_c_im, v_s5_d, v_s5_log_dt, v_s5_glu_w, v_s5_glu_b, v_c_conv_w, v_d_conv_w, v_d_a_log, v_d_dt_bias, v_d_norm_g, v_w_out, v_final_g):
    weights = dict(norm_g=norm_g, w_in=w_in, a_conv_w=a_conv_w, a_conv_b=a_conv_b, a_ln_g=a_ln_g, a_ln_b=a_ln_b, a_pw_w=a_pw_w, a_pw_b=a_pw_b, s5_lambda_re=s5_lambda_re, s5_lambda_im=s5_lambda_im, s5_b_re=s5_b_re, s5_b_im=s5_b_im, s5_c_re=s5_c_re, s5_c_im=s5_c_im, s5_d=s5_d, s5_log_dt=s5_log_dt, s5_glu_w=s5_glu_w, s5_glu_b=s5_glu_b, c_conv_w=c_conv_w, d_conv_w=d_conv_w, d_a_log=d_a_log, d_dt_bias=d_dt_bias, d_norm_g=d_norm_g, w_out=w_out, final_g=final_g)
    mom_m = dict(norm_g=m_norm_g, w_in=m_w_in, a_conv_w=m_a_conv_w, a_conv_b=m_a_conv_b, a_ln_g=m_a_ln_g, a_ln_b=m_a_ln_b, a_pw_w=m_a_pw_w, a_pw_b=m_a_pw_b, s5_lambda_re=m_s5_lambda_re, s5_lambda_im=m_s5_lambda_im, s5_b_re=m_s5_b_re, s5_b_im=m_s5_b_im, s5_c_re=m_s5_c_re, s5_c_im=m_s5_c_im, s5_d=m_s5_d, s5_log_dt=m_s5_log_dt, s5_glu_w=m_s5_glu_w, s5_glu_b=m_s5_glu_b, c_conv_w=m_c_conv_w, d_conv_w=m_d_conv_w, d_a_log=m_d_a_log, d_dt_bias=m_d_dt_bias, d_norm_g=m_d_norm_g, w_out=m_w_out, final_g=m_final_g)
    mom_v = dict(norm_g=v_norm_g, w_in=v_w_in, a_conv_w=v_a_conv_w, a_conv_b=v_a_conv_b, a_ln_g=v_a_ln_g, a_ln_b=v_a_ln_b, a_pw_w=v_a_pw_w, a_pw_b=v_a_pw_b, s5_lambda_re=v_s5_lambda_re, s5_lambda_im=v_s5_lambda_im, s5_b_re=v_s5_b_re, s5_b_im=v_s5_b_im, s5_c_re=v_s5_c_re, s5_c_im=v_s5_c_im, s5_d=v_s5_d, s5_log_dt=v_s5_log_dt, s5_glu_w=v_s5_glu_w, s5_glu_b=v_s5_glu_b, c_conv_w=v_c_conv_w, d_conv_w=v_d_conv_w, d_a_log=v_d_a_log, d_dt_bias=v_d_dt_bias, d_norm_g=v_d_norm_g, w_out=v_w_out, final_g=v_final_g)
    outs = _step(x[0], loss_target[0], weights, mom_m, mom_v)
    return (outs[0], outs[1][None]) + outs[2:]
```

```python
import functools

import jax
import jax.numpy as jnp
from jax import lax
from jax.experimental import pallas as pl
from jax.experimental.pallas import tpu as pltpu

F32 = jnp.float32
BF16 = jnp.bfloat16
HI = lax.Precision.HIGHEST
SDS = jax.ShapeDtypeStruct

N_DEV = 8
D = 1024
BR = 256
DEPTH = 4
IN_COLS = 3336
PW = 3456
AB_COL = 3328
EPS = 1e-6
TL = 512
SEG = TL // 8
HALO = 32
HALO_S = 8
KA, KC, KD = 31, 3, 4
CH = 64
DN_GROUP = 4
NH, HD = 4, 64
NSTATE = 1024
VMEM_LIMIT = 56 * 1024 * 1024

ADAM_LR, ADAM_B1, ADAM_B2, ADAM_EPS, ADAM_WD, ADAM_STEP = 0.001, 0.9, 0.999, 1e-08, 0.01, 10

NN = ((1,), (0,))
NT = ((1,), (1,))
TN = ((0,), (0,))


def _dot(a, b, dims, prec=None):
    return lax.dot_general(a, b, (dims, ((), ())), precision=prec, preferred_element_type=F32)


def _make_mm(cast, prec, fwd_dims):
    def prep(t):
        return t.astype(cast) if cast is not None else t

    @jax.custom_vjp
    def mm(a, w):
        return _dot(prep(a), prep(w), fwd_dims, prec)

    def fwd(a, w):
        return mm(a, w), (a, w)

    def bwd(res, dy):
        a, w = res
        a, w, dy = prep(a), prep(w), prep(dy)
        if fwd_dims == NN:
            return _dot(dy, w, NT, prec), _dot(a, dy, TN, prec)
        if fwd_dims == NT:
            return _dot(dy, w, NN, prec), _dot(dy, a, TN, prec)
        return _dot(w, dy, NT, prec), _dot(a, dy, NN, prec)

    mm.defvjp(fwd, bwd)
    return mm


mm = _make_mm(BF16, None, NN)
mm_nt = _make_mm(BF16, None, NT)
mm_tn = _make_mm(BF16, None, TN)
mmh = _make_mm(None, HI, NN)
mmh_nt = _make_mm(None, HI, NT)


def _sigmoid(x):
    return jax.nn.sigmoid(x)


def _silu(x):
    return x * jax.nn.sigmoid(x)


def _gelu(x):
    return 0.5 * x * (1.0 + jnp.tanh(0.7978845608028654 * (x + 0.044715 * (x * x * x))))


def _softplus(x):
    return jnp.maximum(x, 0.0) + jnp.log1p(jnp.exp(-jnp.abs(x)))


def _rms(x, g):
    return x * lax.rsqrt(jnp.mean(x * x, axis=-1, keepdims=True) + EPS) * g


def _cparams(sem):
    return pltpu.CompilerParams(dimension_semantics=sem, vmem_limit_bytes=VMEM_LIMIT)


def _tap_offsets(halo, taps):
    return [halo - (taps - 1) + k for k in range(taps)]


def _conv_fwd_impl(acat, w, tile, halo, taps):
    n = tile + halo
    out = None
    for k, off in enumerate(_tap_offsets(halo, taps)):
        src = jnp.roll(acat, n - off, axis=0)[:tile, :] if off != halo else acat[halo:, :]
        term = src * w[k:k + 1, :]
        out = term if out is None else out + term
    return out


def _make_conv(tile, halo, taps):
    @jax.custom_vjp
    def conv(acat, w):
        return _conv_fwd_impl(acat, w, tile, halo, taps)

    def fwd(acat, w):
        return conv(acat, w), (acat, w)

    def bwd(res, dy):
        acat, w = res
        n = tile + halo
        dyp = jnp.concatenate([dy, jnp.zeros((halo, dy.shape[1]), F32)], axis=0)
        rows = lax.broadcasted_iota(jnp.int32, w.shape, 0)
        dacat = None
        dw = jnp.zeros(w.shape, F32)
        for k, off in enumerate(_tap_offsets(halo, taps)):
            term = jnp.roll(dyp, off, axis=0) * w[k:k + 1, :]
            dacat = term if dacat is None else dacat + term
            src = jnp.roll(acat, n - off, axis=0)[:tile, :] if off != halo else acat[halo:, :]
            dw = dw + jnp.where(rows == k, jnp.sum(dy * src, axis=0, keepdims=True), 0.0)
        return dacat, dw

    conv.defvjp(fwd, bwd)
    return conv


def _halo_spec(tile, halo, width, col):
    per = tile // halo
    return pl.BlockSpec((halo, width), lambda i: (jnp.maximum(i * per - 1, 0), col))


def _halo_spec_rev(nt, tile, halo, width, col):
    per = tile // halo
    return pl.BlockSpec((halo, width), lambda i: (jnp.maximum((nt - 1 - i) * per - 1, 0), col))


def _dep_specs(deps):
    return [pl.BlockSpec((8, 128), lambda *_: (0, 0)) for _ in deps]


def _inproj_call(x, g, w, deps=()):
    L = x.shape[0]

    def body(x_ref, g_ref, w_ref, *rest):
        p_ref, h_ref = rest[len(deps):]
        h = _rms(x_ref[...], g_ref[...]).astype(BF16)
        h_ref[...] = h
        p_ref[...] = _dot(h, w_ref[...], NN)

    return pl.pallas_call(
        body, name="inproj", grid=(L // TL,),
        in_specs=[pl.BlockSpec((TL, D), lambda i: (i, 0)), pl.BlockSpec((1, D), lambda i: (0, 0)),
                  pl.BlockSpec((D, PW), lambda i: (0, 0))] + _dep_specs(deps),
        out_specs=[pl.BlockSpec((TL, PW), lambda i: (i, 0)), pl.BlockSpec((TL, D), lambda i: (i, 0))],
        out_shape=[SDS((L, PW), F32), SDS((L, D), BF16)],
        compiler_params=_cparams(("parallel",)),
    )(x, g, w, *deps)


def _outproj_call(x, ys, w):
    L = x.shape[0]

    def body(x_ref, a_ref, b_ref, c_ref, d_ref, w_ref, o_ref):
        acc = x_ref[...]
        for b, y_ref in enumerate((a_ref, b_ref, c_ref, d_ref)):
            acc = acc + _dot(y_ref[...].astype(BF16), w_ref[b * BR:(b + 1) * BR, :], NN)
        o_ref[...] = acc

    yspec = pl.BlockSpec((TL, BR), lambda i: (i, 0))
    return pl.pallas_call(
        body, name="outproj", grid=(L // TL,),
        in_specs=[pl.BlockSpec((TL, D), lambda i: (i, 0)), yspec, yspec, yspec, yspec,
                  pl.BlockSpec((D, D), lambda i: (0, 0))],
        out_specs=pl.BlockSpec((TL, D), lambda i: (i, 0)),
        out_shape=SDS((L, D), F32),
        compiler_params=_cparams(("parallel",)),
    )(x, *ys, w)


def _loss_call(x, g, target):
    L = x.shape[0]

    def body(x_ref, g_ref, t_ref, dx_ref, dg_ref, loss_ref):
        @pl.when(pl.program_id(0) == 0)
        def _():
            dg_ref[...] = jnp.zeros_like(dg_ref)
            loss_ref[...] = jnp.zeros_like(loss_ref)

        y, vjp = jax.vjp(_rms, x_ref[...], g_ref[...])
        err = y - t_ref[...]
        dx, dg = vjp(err * (1.0 / D))
        dx_ref[...] = dx
        dg_ref[...] += dg
        tot = jnp.sum(jnp.sum(err * err, axis=1, keepdims=True), axis=0, keepdims=True)
        loss_ref[...] += jnp.broadcast_to(tot * (0.5 / D), loss_ref.shape)

    return pl.pallas_call(
        body, name="loss_head", grid=(L // TL,),
        in_specs=[pl.BlockSpec((TL, D), lambda i: (i, 0)), pl.BlockSpec((1, D), lambda i: (0, 0)),
                  pl.BlockSpec((TL, D), lambda i: (i, 0))],
        out_specs=[pl.BlockSpec((TL, D), lambda i: (i, 0)), pl.BlockSpec((1, D), lambda i: (0, 0)),
                   pl.BlockSpec((1, 128), lambda i: (0, 0))],
        out_shape=[SDS((L, D), F32), SDS((1, D), F32), SDS((1, 128), F32)],
        compiler_params=_cparams(("arbitrary",)),
    )(x, g, target)


def _branch_a(valw, gatew, z, cw, cb, lg, lb, pw, pb, conv):
    a = conv(valw * _sigmoid(gatew), cw) + cb
    mu = jnp.mean(a, axis=-1, keepdims=True)
    xc = a - mu
    y = xc * lax.rsqrt(jnp.mean(xc * xc, axis=-1, keepdims=True) + EPS) * lg + lb
    y = mm(_silu(y), pw) + pb
    return y * _silu(z)


def _branch_c(bg, cw_, xw, z, w3, conv):
    return bg * conv(cw_ * xw, w3) * _silu(z)


def _ac_fwd_call(proj, p):
    L = proj.shape[0]

    def body(val, gate, za, hval, hgate, cb_, cc, cx, cz, hcc, hcx,
             acw, acb, alg, alb, apw, apb, ccw, ya_ref, yc_ref):
        nf = (pl.program_id(0) > 0).astype(F32)
        win = lambda h, m: jnp.concatenate([h[...] * nf, m[...]], axis=0)
        conv_a = functools.partial(_conv_fwd_impl, tile=TL, halo=HALO, taps=KA)
        conv_c = functools.partial(_conv_fwd_impl, tile=TL, halo=HALO, taps=KC)
        ya_ref[...] = _branch_a(win(hval, val), win(hgate, gate), za[...], acw[...], acb[...], alg[...], alb[...],
                                apw[...], apb[...], conv_a)
        yc_ref[...] = _branch_c(cb_[...], win(hcc, cc), win(hcx, cx), cz[...], ccw[...], conv_c)

    col = lambda j: pl.BlockSpec((TL, BR), lambda i: (i, j))
    hal = lambda j: _halo_spec(TL, HALO, BR, j)
    full = lambda a: pl.BlockSpec(a.shape, lambda i: (0,) * a.ndim)
    params = (p["a_conv_w"], p["a_conv_b"], p["a_ln_g"], p["a_ln_b"], p["a_pw_w"], p["a_pw_b"], p["c_conv_w"])
    return pl.pallas_call(
        body, name="ac_fwd", grid=(L // TL,),
        in_specs=[col(0), col(1), col(2), hal(0), hal(1), col(5), col(6), col(7), col(8), hal(6), hal(7)]
        + [full(a) for a in params],
        out_specs=[pl.BlockSpec((TL, BR), lambda i: (i, 0))] * 2,
        out_shape=[SDS((L, BR), F32)] * 2,
        compiler_params=_cparams(("parallel",)),
    )(*([proj] * 11), *params)


def _ac_bwd_call(proj, p, dya, dyc):
    L = proj.shape[0]
    nt = L // TL

    def body(val, gate, za, hval, hgate, cb_, cc, cx, cz, hcc, hcx,
             acw, acb, alg, alb, apw, apb, ccw, dya_ref, dyc_ref,
             da_ref, dc_ref, g_acw, g_acb, g_alg, g_alb, g_apw, g_apb, g_ccw, carry):
        i = pl.program_id(0)
        gouts = (g_acw, g_acb, g_alg, g_alb, g_apw, g_apb, g_ccw)

        @pl.when(i == 0)
        def _():
            carry[...] = jnp.zeros_like(carry)
            for r in gouts:
                r[...] = jnp.zeros_like(r)

        nf = (i < nt - 1).astype(F32)
        win = lambda h, m: jnp.concatenate([h[...] * nf, m[...]], axis=0)
        conv_a = _make_conv(TL, HALO, KA)
        conv_c = _make_conv(TL, HALO, KC)

        def f(valw, gatew, z, bg, ccw_, cxw, czv, w1, b1, lg, lb, pw, pb, w3):
            return (_branch_a(valw, gatew, z, w1, b1, lg, lb, pw, pb, conv_a),
                    _branch_c(bg, ccw_, cxw, czv, w3, conv_c))

        _, vjp = jax.vjp(f, win(hval, val), win(hgate, gate), za[...], cb_[...], win(hcc, cc), win(hcx, cx), cz[...],
                         acw[...], acb[...], alg[...], alb[...], apw[...].astype(F32), apb[...], ccw[...])
        (dvalw, dgatew, dz, dbg, dccw, dcxw, dczv, d1, d2, d3, d4, d5, d6, d7) = vjp((dya_ref[...], dyc_ref[...]))

        def settle(slot, dwin):
            tail = jnp.concatenate([jnp.zeros((TL - HALO, BR), F32), carry[slot]], axis=0)
            carry[slot] = dwin[:HALO, :]
            return (dwin[HALO:, :] + tail).astype(BF16)

        da_ref[:, 0:BR] = settle(0, dvalw)
        da_ref[:, BR:2 * BR] = settle(1, dgatew)
        da_ref[:, 2 * BR:3 * BR] = dz.astype(BF16)
        dc_ref[:, 0:BR] = dbg.astype(BF16)
        dc_ref[:, BR:2 * BR] = settle(2, dccw)
        dc_ref[:, 2 * BR:3 * BR] = settle(3, dcxw)
        dc_ref[:, 3 * BR:4 * BR] = dczv.astype(BF16)
        for r, g in zip(gouts, (d1, d2, d3, d4, d5, d6, d7)):
            r[...] += g

    col = lambda j: pl.BlockSpec((TL, BR), lambda i: (nt - 1 - i, j))
    hal = lambda j: _halo_spec_rev(nt, TL, HALO, BR, j)
    full = lambda a: pl.BlockSpec(a.shape, lambda i: (0,) * a.ndim)
    params = (p["a_conv_w"], p["a_conv_b"], p["a_ln_g"], p["a_ln_b"], p["a_pw_w"], p["a_pw_b"], p["c_conv_w"])
    rev = lambda w: pl.BlockSpec((TL, w), lambda i: (nt - 1 - i, 0))
    return pl.pallas_call(
        body, name="ac_bwd", grid=(nt,),
        in_specs=[col(0), col(1), col(2), hal(0), hal(1), col(5), col(6), col(7), col(8), hal(6), hal(7)]
        + [full(a) for a in params] + [rev(BR), rev(BR)],
        out_specs=[rev(3 * BR), rev(4 * BR)] + [full(a) for a in params],
        out_shape=[SDS((L, 3 * BR), BF16), SDS((L, 4 * BR), BF16)] + [SDS(a.shape, F32) for a in params],
        scratch_shapes=[pltpu.VMEM((4, HALO, BR), F32)],
        compiler_params=_cparams(("arbitrary",)),
    )(*([proj] * 11), *params, dya, dyc)


def _iota2(shape, dim):
    return lax.broadcasted_iota(jnp.int32, shape, dim)


def _s5_params(lam_re, lam_im, logdt, b_re, b_im, c_re, c_im):
    eg = (_iota2((16, NSTATE), 1) >> 6 == _iota2((16, NSTATE), 0)).astype(F32)
    dt = jnp.exp(mmh(jnp.broadcast_to(logdt, (8, 16)), eg)[0:1, :])
    lr = jnp.minimum(lam_re, -1e-4)
    li = lam_im
    mag = jnp.exp(lr * dt)
    lbr = mag * jnp.cos(li * dt)
    lbi = mag * jnp.sin(li * dt)
    den = lr * lr + li * li
    nr = lbr - 1.0
    fr = (nr * lr + lbi * li) / den
    fi = (lbi * lr - nr * li) / den
    row = _iota2((8, NSTATE), 0)
    f8 = jnp.where(row == 0, fr, jnp.where(row == 1, fi, 0.0))
    eye = (_iota2((NSTATE, NSTATE), 0) == _iota2((NSTATE, NSTATE), 1)).astype(F32)
    fcol = mmh_nt(eye, f8)
    frc, fic = fcol[:, 0:1], fcol[:, 1:2]
    bbr = frc * b_re - fic * b_im
    bbi = frc * b_im + fic * b_re
    e1 = ((_iota2((16, BR), 1) & 15) == _iota2((16, BR), 0)).astype(F32)
    m1 = ((_iota2((NSTATE, BR), 0) >> 6) == (_iota2((NSTATE, BR), 1) >> 4)).astype(F32)
    wbr = mmh(bbr, e1) * m1
    wbi = mmh(bbi, e1) * m1
    e2 = ((_iota2((64, NSTATE), 1) & 63) == _iota2((64, NSTATE), 0)).astype(F32)
    m2 = ((_iota2((BR, NSTATE), 0) >> 4) == (_iota2((BR, NSTATE), 1) >> 6)).astype(F32)
    wcr = mmh(c_re, e2) * m2
    wci = mmh(c_im, e2) * m2
    return lbr, lbi, wbr, wbi, wcr, wci


_S5_OUT = [(1, NSTATE), (1, NSTATE), (NSTATE, BR), (NSTATE, BR), (BR, NSTATE), (BR, NSTATE)]


def _s5_prep_call(sp):
    def body(lre, lim, ldt, bre, bim, cre, cim, o_lbr, o_lbi, o_wbr, o_wbi, o_wcr, o_wci, pwr, pwi, qwr, qwi):
        lbr, lbi, wbr, wbi, wcr, wci = _s5_params(lre[...], lim[...], ldt[...], bre[...], bim[...], cre[...], cim[...])
        o_lbr[...], o_lbi[...], o_wbr[...], o_wbi[...], o_wcr[...], o_wci[...] = lbr, lbi, wbr, wbi, wcr, wci
        pr, pi = lbr, lbi
        for i in range(SEG):
            pwr[i:i + 1, :] = pr
            pwi[i:i + 1, :] = pi
            qwr[SEG - 1 - i:SEG - i, :] = pr
            qwi[SEG - 1 - i:SEG - i, :] = -pi
            pr, pi = pr * lbr - pi * lbi, pr * lbi + pi * lbr

    args = (sp["lam_re"], sp["lam_im"], sp["log_dt"], sp["b_re"], sp["b_im"], sp["c_re"], sp["c_im"])
    return pl.pallas_call(
        body, name="s5_prep",
        out_shape=[SDS(s, F32) for s in _S5_OUT] + [SDS((SEG, NSTATE), F32)] * 4,
        compiler_params=pltpu.CompilerParams(vmem_limit_bytes=VMEM_LIMIT),
    )(*args)


def _s5_prep_bwd_call(sp, cots):
    def body(lre, lim, ldt, bre, bim, cre, cim, c0, c1, c2, c3, c4, c5, *outs):
        _, vjp = jax.vjp(_s5_params, lre[...], lim[...], ldt[...], bre[...], bim[...], cre[...], cim[...])
        grads = vjp((c0[...], c1[...], c2[...], c3[...], c4[...], c5[...]))
        for o, g in zip(outs, grads):
            o[...] = g

    args = (sp["lam_re"], sp["lam_im"], sp["log_dt"], sp["b_re"], sp["b_im"], sp["c_re"], sp["c_im"])
    return pl.pallas_call(
        body, name="s5_prep_bwd",
        out_shape=[SDS(a.shape, F32) for a in args],
        compiler_params=pltpu.CompilerParams(vmem_limit_bytes=VMEM_LIMIT),
    )(*args, *cots)


def _lanes(v, j):
    return v[:, j * 128:(j + 1) * 128]


def _s5_scan(sre, sim, pwr, pwi, cin_r, cin_i, reverse):
    row = _iota2((8, 128), 0)
    steps = (1, 2, 4)

    def lane_consts(j):
        lanes = slice(j * 128, (j + 1) * 128)
        if reverse:
            mult = [(jnp.broadcast_to(pwr[SEG - d:SEG - d + 1, lanes], (8, 128)),
                     jnp.broadcast_to(pwi[SEG - d:SEG - d + 1, lanes], (8, 128))) for d in steps]
            return mult, pwr[SEG - 8:SEG, lanes], pwi[SEG - 8:SEG, lanes]
        mult = [(jnp.broadcast_to(pwr[d - 1:d, lanes], (8, 128)),
                 jnp.broadcast_to(pwi[d - 1:d, lanes], (8, 128))) for d in steps]
        return mult, pwr[0:8, lanes], pwi[0:8, lanes]

    consts = [lane_consts(j) for j in range(8)]
    nblk = TL // 8

    def block(t, carry):
        b = nblk - 1 - t if reverse else t
        rows = pl.ds(pl.multiple_of(b * 8, 8), 8)
        new = []
        for j in range(8):
            mult, p8r, p8i = consts[j]
            vr, vi = sre.at[j], sim.at[j]
            sr, si = vr[rows, :], vi[rows, :]
            for d, (mr, mi) in zip(steps, mult):
                if reverse:
                    hr = jnp.where(row < 8 - d, pltpu.roll(sr, 8 - d, 0), 0.0)
                    hi = jnp.where(row < 8 - d, pltpu.roll(si, 8 - d, 0), 0.0)
                else:
                    hr = jnp.where(row >= d, pltpu.roll(sr, d, 0), 0.0)
                    hi = jnp.where(row >= d, pltpu.roll(si, d, 0), 0.0)
                sr, si = sr + mr * hr - mi * hi, si + mr * hi + mi * hr
            cr, ci = carry[2 * j], carry[2 * j + 1]
            sr, si = sr + p8r * cr - p8i * ci, si + p8r * ci + p8i * cr
            vr[rows, :] = sr
            vi[rows, :] = si
            edge = slice(0, 1) if reverse else slice(7, 8)
            new += [sr[edge, :], si[edge, :]]
        return tuple(new)

    init = []
    for j in range(8):
        init += [_lanes(cin_r, j), _lanes(cin_i, j)]
    ends = lax.fori_loop(0, nblk, block, tuple(init))
    return (jnp.concatenate([ends[2 * j] for j in range(8)], axis=1),
            jnp.concatenate([ends[2 * j + 1] for j in range(8)], axis=1))


def _bdot(a, b, dims):
    return _dot(a.astype(BF16), b.astype(BF16), dims)


def _s5_states(u, wbr, wbi, sre, sim):
    bur = _bdot(u, wbr, NT)
    bui = _bdot(u, wbi, NT)
    for j in range(8):
        sre[j] = _lanes(bur, j)
        sim[j] = _lanes(bui, j)


def _gather_lanes(s):
    return jnp.concatenate([s[j] for j in range(8)], axis=1)


def _s5_post(s_re, s_im, u, z, wcr, wci, dsk, gw, gb):
    y = mm_nt(s_re, wcr) - mm_nt(s_im, wci) + dsk * u
    yg = _gelu(y)
    return yg * _sigmoid(mm(yg, gw) + gb) * _silu(z)


def _s5_fwd_call(proj, prep, p):
    L = proj.shape[0]
    nt = L // TL
    lbr, lbi, wbr, wbi, wcr, wci, pwr, pwi, _, _ = prep

    def body(u_ref, z_ref, lbr_r, lbi_r, wbr_r, wbi_r, wcr_r, wci_r, pwr_r, pwi_r, d_r, gw_r, gb_r,
             yb_ref, cinr_ref, cini_ref, sre, sim, car, cai):
        @pl.when(pl.program_id(0) == 0)
        def _():
            car[...] = jnp.zeros_like(car)
            cai[...] = jnp.zeros_like(cai)

        u = u_ref[...]
        cinr_ref[0] = car[...]
        cini_ref[0] = cai[...]
        _s5_states(u, wbr_r[...], wbi_r[...], sre, sim)
        nr, ni = _s5_scan(sre, sim, pwr_r, pwi_r, car[...], cai[...], False)
        car[...] = nr
        cai[...] = ni
        yb_ref[...] = _s5_post(_gather_lanes(sre), _gather_lanes(sim), u, z_ref[...], wcr_r[...], wci_r[...],
                               d_r[...], gw_r[...], gb_r[...])

    full = lambda a: pl.BlockSpec(a.shape, lambda i: (0,) * a.ndim)
    consts = (lbr, lbi, wbr, wbi, wcr, wci, pwr, pwi, p["s5_d"], p["s5_glu_w"], p["s5_glu_b"])
    cspec = pl.BlockSpec((1, 1, NSTATE), lambda i: (i, 0, 0))
    return pl.pallas_call(
        body, name="s5_fwd", grid=(nt,),
        in_specs=[pl.BlockSpec((TL, BR), lambda i: (i, 3)), pl.BlockSpec((TL, BR), lambda i: (i, 4))]
        + [full(a) for a in consts],
        out_specs=[pl.BlockSpec((TL, BR), lambda i: (i, 0)), cspec, cspec],
        out_shape=[SDS((L, BR), F32), SDS((nt, 1, NSTATE), F32), SDS((nt, 1, NSTATE), F32)],
        scratch_shapes=[pltpu.VMEM((8, TL, 128), F32), pltpu.VMEM((8, TL, 128), F32),
                        pltpu.VMEM((1, NSTATE), F32), pltpu.VMEM((1, NSTATE), F32)],
        compiler_params=_cparams(("arbitrary",)),
    )(proj, proj, *consts)


def _s5_bwd_call(proj, prep, p, cin_r, cin_i, dyb):
    L = proj.shape[0]
    nt = L // TL
    lbr, lbi, wbr, wbi, wcr, wci, pwr, pwi, qwr, qwi = prep

    def body(u_ref, z_ref, lbr_r, lbi_r, wbr_r, wbi_r, wcr_r, wci_r, pwr_r, pwi_r, qwr_r, qwi_r, d_r, gw_r, gb_r,
             cinr_ref, cini_ref, dy_ref,
             db_ref, g_lbr, g_lbi, g_wbr, g_wbi, g_wcr, g_wci, g_d, g_gw, g_gb, sre, sim, gre, gim, car, cai):
        gouts = (g_lbr, g_lbi, g_wbr, g_wbi, g_wcr, g_wci, g_d, g_gw, g_gb)

        @pl.when(pl.program_id(0) == 0)
        def _():
            car[...] = jnp.zeros_like(car)
            cai[...] = jnp.zeros_like(cai)
            for r in gouts:
                r[...] = jnp.zeros_like(r)

        u = u_ref[...]
        lr, li = lbr_r[...], lbi_r[...]
        c0r, c0i = cinr_ref[0], cini_ref[0]
        _s5_states(u, wbr_r[...], wbi_r[...], sre, sim)
        _s5_scan(sre, sim, pwr_r, pwi_r, c0r, c0i, False)
        s_re, s_im = _gather_lanes(sre), _gather_lanes(sim)
        _, vjp = jax.vjp(_s5_post, s_re, s_im, u, z_ref[...], wcr_r[...], wci_r[...], d_r[...],
                         gw_r[...].astype(F32), gb_r[...])
        ds_re, ds_im, du, dz, dwcr, dwci, dd, dgw, dgb = vjp(dy_ref[...])
        for j in range(8):
            gre[j] = _lanes(ds_re, j)
            gim[j] = _lanes(ds_im, j)
        nr, ni = _s5_scan(gre, gim, qwr_r, qwi_r, car[...], cai[...], True)
        car[...] = nr
        cai[...] = ni
        a_re, a_im = _gather_lanes(gre), _gather_lanes(gim)
        first = _iota2((TL, NSTATE), 0) == 0
        p_re = jnp.where(first, c0r, jnp.roll(s_re, 1, axis=0))
        p_im = jnp.where(first, c0i, jnp.roll(s_im, 1, axis=0))
        g_lbr[...] += jnp.sum(a_re * p_re + a_im * p_im, axis=0, keepdims=True)
        g_lbi[...] += jnp.sum(a_im * p_re - a_re * p_im, axis=0, keepdims=True)
        du = du + _bdot(a_re, wbr_r[...], NN) + _bdot(a_im, wbi_r[...], NN)
        g_wbr[...] += _bdot(a_re, u, TN)
        g_wbi[...] += _bdot(a_im, u, TN)
        g_wcr[...] += dwcr
        g_wci[...] += dwci
        g_d[...] += dd
        g_gw[...] += dgw
        g_gb[...] += dgb
        db_ref[:, 0:BR] = du.astype(BF16)
        db_ref[:, BR:2 * BR] = dz.astype(BF16)

    full = lambda a: pl.BlockSpec(a.shape, lambda i: (0,) * a.ndim)
    consts = (lbr, lbi, wbr, wbi, wcr, wci, pwr, pwi, qwr, qwi, p["s5_d"], p["s5_glu_w"], p["s5_glu_b"])
    cspec = pl.BlockSpec((1, 1, NSTATE), lambda i: (nt - 1 - i, 0, 0))
    gshapes = _S5_OUT + [(1, BR), (BR, BR), (1, BR)]
    return pl.pallas_call(
        body, name="s5_bwd", grid=(nt,),
        in_specs=[pl.BlockSpec((TL, BR), lambda i: (nt - 1 - i, 3)), pl.BlockSpec((TL, BR), lambda i: (nt - 1 - i, 4))]
        + [full(a) for a in consts] + [cspec, cspec, pl.BlockSpec((TL, BR), lambda i: (nt - 1 - i, 0))],
        out_specs=[pl.BlockSpec((TL, 2 * BR), lambda i: (nt - 1 - i, 0))]
        + [pl.BlockSpec(s, lambda i: (0, 0)) for s in gshapes],
        out_shape=[SDS((L, 2 * BR), BF16)] + [SDS(s, F32) for s in gshapes],
        scratch_shapes=[pltpu.VMEM((8, TL, 128), F32)] * 4 + [pltpu.VMEM((1, NSTATE), F32)] * 2,
        compiler_params=_cparams(("arbitrary",)),
    )(proj, proj, *consts, cin_r, cin_i, dyb)


def _heads(x):
    return [x[:, h * HD:(h + 1) * HD] for h in range(NH)]


def _l2n(x, scale):
    return jnp.concatenate([xh * (lax.rsqrt(jnp.sum(xh * xh, axis=-1, keepdims=True) + EPS) * scale)
                            for xh in _heads(x)], axis=1)


def _dn_pre(qkvw, ab, cw, alog, dtb, conv, rows):
    c = _silu(conv(qkvw, cw))
    q = _l2n(c[:, 0:BR], HD ** -0.5)
    k = _l2n(c[:, BR:2 * BR], 1.0)
    v = c[:, 2 * BR:3 * BR]
    g = -jnp.exp(alog) * _softplus(ab + dtb)
    ri, ci = _iota2((rows, rows), 0), _iota2((rows, rows), 1)
    tri = ((ri >= ci) & ((ri >> 6) == (ci >> 6))).astype(F32)
    gc = mmh(tri, g)
    lane = _iota2(ab.shape, 1)
    return q, k, v, jnp.where(lane < NH, gc, jnp.where(lane < 2 * NH, _sigmoid(ab), 0.0))


def _dn_pre_fwd_call(proj, p):
    L = proj.shape[0]

    def body(m_ref, h_ref, ab_ref, cw, alog, dtb, q_ref, k_ref, v_ref, gb_ref):
        nf = (pl.program_id(0) > 0).astype(F32)
        qkvw = jnp.concatenate([h_ref[...] * nf, m_ref[...]], axis=0)
        conv = functools.partial(_conv_fwd_impl, tile=TL, halo=HALO_S, taps=KD)
        q_ref[...], k_ref[...], v_ref[...], gb_ref[...] = _dn_pre(qkvw, ab_ref[...], cw[...], alog[...], dtb[...], conv, TL)

    full = lambda a: pl.BlockSpec(a.shape, lambda i: (0,) * a.ndim)
    params = (p["d_conv_w"], p["d_a_log"], p["d_dt_bias"])
    o = pl.BlockSpec((TL, BR), lambda i: (i, 0))
    return pl.pallas_call(
        body, name="dn_pre_fwd", grid=(L // TL,),
        in_specs=[pl.BlockSpec((TL, 3 * BR), lambda i: (i, 3)), _halo_spec(TL, HALO_S, 3 * BR, 3),
                  pl.BlockSpec((TL, 128), lambda i: (i, AB_COL // 128))] + [full(a) for a in params],
        out_specs=[o, o, o, pl.BlockSpec((TL, 128), lambda i: (i, 0))],
        out_shape=[SDS((L, BR), F32)] * 3 + [SDS((L, 128), F32)],
        compiler_params=_cparams(("parallel",)),
    )(proj, proj, proj, *params)


def _dn_pre_bwd_call(proj, p, dq, dk, dv, dgb):
    L = proj.shape[0]
    nt = L // TL

    def body(m_ref, h_ref, ab_ref, cw, alog, dtb, dq_r, dk_r, dv_r, dgb_r,
             dqkv_ref, dab_ref, g_cw, g_alog, g_dtb, carry):
        i = pl.program_id(0)

        @pl.when(i == 0)
        def _():
            carry[...] = jnp.zeros_like(carry)
            for r in (g_cw, g_alog, g_dtb):
                r[...] = jnp.zeros_like(r)

        nf = (i < nt - 1).astype(F32)
        qkvw = jnp.concatenate([h_ref[...] * nf, m_ref[...]], axis=0)
        conv = _make_conv(TL, HALO_S, KD)
        _, vjp = jax.vjp(lambda a, b, c, d, e: _dn_pre(a, b, c, d, e, conv, TL),
                         qkvw, ab_ref[...], cw[...], alog[...], dtb[...])
        dwin, dab, dcw, dalog, ddtb = vjp((dq_r[...], dk_r[...], dv_r[...], dgb_r[...]))
        tail = jnp.concatenate([jnp.zeros((TL - HALO_S, 3 * BR), F32), carry[...]], axis=0)
        carry[...] = dwin[:HALO_S, :]
        dqkv_ref[...] = (dwin[HALO_S:, :] + tail).astype(BF16)
        dab_ref[...] = dab.astype(BF16)
        g_cw[...] += dcw
        g_alog[...] += dalog
        g_dtb[...] += ddtb

    full = lambda a: pl.BlockSpec(a.shape, lambda i: (0,) * a.ndim)
    params = (p["d_conv_w"], p["d_a_log"], p["d_dt_bias"])
    rev = lambda w: pl.BlockSpec((TL, w), lambda i: (nt - 1 - i, 0))
    return pl.pallas_call(
        body, name="dn_pre_bwd", grid=(nt,),
        in_specs=[pl.BlockSpec((TL, 3 * BR), lambda i: (nt - 1 - i, 3)), _halo_spec_rev(nt, TL, HALO_S, 3 * BR, 3),
                  pl.BlockSpec((TL, 128), lambda i: (nt - 1 - i, AB_COL // 128))] + [full(a) for a in params]
        + [rev(BR), rev(BR), rev(BR), rev(128)],
        out_specs=[rev(3 * BR), rev(128)] + [full(a) for a in params],
        out_shape=[SDS((L, 3 * BR), BF16), SDS((L, 128), BF16)] + [SDS(a.shape, F32) for a in params],
        scratch_shapes=[pltpu.VMEM((HALO_S, 3 * BR), F32)],
        compiler_params=_cparams(("arbitrary",)),
    )(proj, proj, proj, *params, dq, dk, dv, dgb)


def _dn_group(q, k, v, gb, z, ng, *s):
    ri, ci = _iota2((CH, CH), 0), _iota2((CH, CH), 1)
    causal, strict = ri >= ci, ri > ci
    eye = (ri == ci).astype(F32)
    s = list(s)
    pairs = []
    for c in range(DN_GROUP):
        rows = slice(c * CH, (c + 1) * CH)
        gbc = gb[rows, :]
        for h, (qh, kh, vh, zh) in enumerate(zip(_heads(q[rows, :]), _heads(k[rows, :]), _heads(v[rows, :]),
                                                 _heads(z[rows, :]))):
            gc = jnp.broadcast_to(gbc[:, h:h + 1], (CH, HD))
            beta = gbc[:, NH + h:NH + h + 1]
            decay = jnp.where(causal, jnp.exp(jnp.where(causal, gc - gc.T, 0.0)), 0.0)
            egc = jnp.exp(gc)
            glast = gc[CH - 1:CH, :]
            kb = kh * beta
            pairs.append(dict(q=qh, k=kh, z=zh, decay=decay, qe=qh * egc, kd=kh * jnp.exp(glast - gc),
                              sdec=jnp.exp(glast[:, 0:1]), kb=kb, rhs=jnp.concatenate([vh * beta, kb * egc], axis=1)))
    for p in pairs:
        p["pw"] = jnp.where(strict, mm_nt(p["kb"], p["k"]) * p["decay"], 0.0)
        p["t"] = eye - p["pw"]
    for _ in range(5):
        for p in pairs:
            p["pw"] = mm(p["pw"], p["pw"])
        for p in pairs:
            p["t"] = mm(p["t"], eye + p["pw"])
    for p in pairs:
        p["uw"] = mm(p["t"], p["rhs"])
    for p in pairs:
        p["attn"] = mm_nt(p["q"], p["k"]) * p["decay"]
    out_rows = []
    for c in range(DN_GROUP):
        grp = pairs[c * NH:(c + 1) * NH]
        ws = [mm(jnp.concatenate([p["uw"][:, HD:], p["qe"]], axis=0), s[h]) for h, p in enumerate(grp)]
        v_new = [p["uw"][:, :HD] - w_[:CH, :] for p, w_ in zip(grp, ws)]
        o = [w_[CH:, :] + mm(p["attn"], vn) for p, w_, vn in zip(grp, ws, v_new)]
        s = [s[h] * p["sdec"] + mm_tn(p["kd"], vn) for h, (p, vn) in enumerate(zip(grp, v_new))]
        o = [oh * lax.rsqrt(jnp.mean(oh * oh, axis=-1, keepdims=True) + EPS) * ng * _silu(p["z"]) for oh, p in zip(o, grp)]
        out_rows.append(jnp.concatenate(o, axis=1))
    return (jnp.concatenate(out_rows, axis=0), *s)


def _dn_core_fwd_call(proj, q, k, v, gb, ng):
    L = q.shape[0]
    rows = DN_GROUP * CH
    ng_ = L // rows

    def body(q_r, k_r, v_r, gb_r, z_r, ng_r, yd_ref, ssave_ref, s_scr):
        @pl.when(pl.program_id(0) == 0)
        def _():
            s_scr[...] = jnp.zeros_like(s_scr)

        ssave_ref[0] = s_scr[...]
        yd, *s2 = _dn_group(q_r[...], k_r[...], v_r[...], gb_r[...], z_r[...], ng_r[...], *[s_scr[h] for h in range(NH)])
        yd_ref[...] = yd
        for h in range(NH):
            s_scr[h] = s2[h]

    c = pl.BlockSpec((rows, BR), lambda i: (i, 0))
    return pl.pallas_call(
        body, name="dn_core_fwd", grid=(ng_,),
        in_specs=[c, c, c, pl.BlockSpec((rows, 128), lambda i: (i, 0)), pl.BlockSpec((rows, BR), lambda i: (i, 12)),
                  pl.BlockSpec((1, HD), lambda i: (0, 0))],
        out_specs=[c, pl.BlockSpec((1, NH, HD, HD), lambda i: (i, 0, 0, 0))],
        out_shape=[SDS((L, BR), F32), SDS((ng_, NH, HD, HD), F32)],
        scratch_shapes=[pltpu.VMEM((NH, HD, HD), F32)],
        compiler_params=_cparams(("arbitrary",)),
    )(q, k, v, gb, proj, ng)


def _dn_core_bwd_call(proj, q, k, v, gb, ng, ssave, dyd):
    L = q.shape[0]
    rows = DN_GROUP * CH
    ng_ = L // rows

    def body(q_r, k_r, v_r, gb_r, z_r, ng_r, s_r, dy_r, dq_ref, dk_ref, dv_ref, dgb_ref, dz_ref, g_ng, ds_scr):
        @pl.when(pl.program_id(0) == 0)
        def _():
            ds_scr[...] = jnp.zeros_like(ds_scr)
            g_ng[...] = jnp.zeros_like(g_ng)

        _, vjp = jax.vjp(_dn_group, q_r[...], k_r[...], v_r[...], gb_r[...], z_r[...], ng_r[...],
                         *[s_r[0, h] for h in range(NH)])
        dq, dk, dv, dgb, dz, dng, *ds = vjp((dy_r[...], *[ds_scr[h] for h in range(NH)]))
        dq_ref[...], dk_ref[...], dv_ref[...], dgb_ref[...] = dq, dk, dv, dgb
        dz_ref[...] = dz.astype(BF16)
        g_ng[...] += dng
        for h in range(NH):
            ds_scr[h] = ds[h]

    c = pl.BlockSpec((rows, BR), lambda i: (ng_ - 1 - i, 0))
    c128 = pl.BlockSpec((rows, 128), lambda i: (ng_ - 1 - i, 0))
    return pl.pallas_call(
        body, name="dn_core_bwd", grid=(ng_,),
        in_specs=[c, c, c, c128, pl.BlockSpec((rows, BR), lambda i: (ng_ - 1 - i, 12)),
                  pl.BlockSpec((1, HD), lambda i: (0, 0)),
                  pl.BlockSpec((1, NH, HD, HD), lambda i: (ng_ - 1 - i, 0, 0, 0)), c],
        out_specs=[c, c, c, c128, c, pl.BlockSpec((1, HD), lambda i: (0, 0))],
        out_shape=[SDS((L, BR), F32)] * 3 + [SDS((L, 128), F32), SDS((L, BR), BF16), SDS((1, HD), F32)],
        scratch_shapes=[pltpu.VMEM((NH, HD, HD), F32)],
        compiler_params=_cparams(("arbitrary",)),
    )(q, k, v, gb, proj, ng, ssave, dyd)


def _outproj_bwd_call(dx, ys, w, deps=()):
    L = dx.shape[0]

    def body(dx_ref, a_ref, b_ref, c_ref, d_ref, w_ref, *rest):
        da, db, dc, dd, dw_ref = rest[len(deps):]

        @pl.when(pl.program_id(0) == 0)
        def _():
            dw_ref[...] = jnp.zeros_like(dw_ref)

        dxb = dx_ref[...].astype(BF16)
        for b, (y_ref, o_ref) in enumerate(zip((a_ref, b_ref, c_ref, d_ref), (da, db, dc, dd))):
            o_ref[...] = _dot(dxb, w_ref[b * BR:(b + 1) * BR, :], NT)
            dw_ref[b * BR:(b + 1) * BR, :] += _dot(y_ref[...].astype(BF16), dxb, TN)

    yspec = pl.BlockSpec((TL, BR), lambda i: (i, 0))
    return pl.pallas_call(
        body, name="outproj_bwd", grid=(L // TL,),
        in_specs=[pl.BlockSpec((TL, D), lambda i: (i, 0)), yspec, yspec, yspec, yspec,
                  pl.BlockSpec((D, D), lambda i: (0, 0))] + _dep_specs(deps),
        out_specs=[yspec] * 4 + [pl.BlockSpec((D, D), lambda i: (0, 0))],
        out_shape=[SDS((L, BR), F32)] * 4 + [SDS((D, D), F32)],
        compiler_params=_cparams(("arbitrary",)),
    )(dx, *ys, w, *deps)


def _slab_cols(slabs):
    widths = [s.shape[1] for s in slabs]
    starts = [sum(widths[:i]) for i in range(len(widths))]
    assert starts[-1] + widths[-1] == PW
    return list(zip(starts, widths))


def _inproj_bwd_x_call(slabs, w, x, g, dx_next):
    L = x.shape[0]
    cols = _slab_cols(slabs)
    n = len(slabs)

    def body(*refs):
        dp_refs, (w_ref, x_ref, g_ref, dxn_ref, dx_ref, dg_ref) = refs[:n], refs[n:]

        @pl.when(pl.program_id(0) == 0)
        def _():
            dg_ref[...] = jnp.zeros_like(dg_ref)

        dh = None
        for dp_ref, (c0, cw) in zip(dp_refs, cols):
            part = _dot(dp_ref[...], w_ref[:, c0:c0 + cw], NT)
            dh = part if dh is None else dh + part
        _, vjp = jax.vjp(_rms, x_ref[...], g_ref[...])
        dx, dg = vjp(dh)
        dx_ref[...] = dx + dxn_ref[...]
        dg_ref[...] += dg

    row = lambda w_: pl.BlockSpec((TL, w_), lambda i: (i, 0))
    return pl.pallas_call(
        body, name="inproj_bwd_x", grid=(L // TL,),
        in_specs=[row(cw) for _, cw in cols]
        + [pl.BlockSpec((D, PW), lambda i: (0, 0)), row(D), pl.BlockSpec((1, D), lambda i: (0, 0)), row(D)],
        out_specs=[row(D), pl.BlockSpec((1, D), lambda i: (0, 0))],
        out_shape=[SDS((L, D), F32), SDS((1, D), F32)],
        compiler_params=_cparams(("arbitrary",)),
    )(*slabs, w, x, g, dx_next)


def _inproj_bwd_w_call(h, slabs):
    L = h.shape[0]
    cols = _slab_cols(slabs)
    n = len(slabs)

    def body(*refs):
        h_ref, dp_refs, dw_ref = refs[0], refs[1:1 + n], refs[1 + n]

        @pl.when(pl.program_id(0) == 0)
        def _():
            dw_ref[...] = jnp.zeros_like(dw_ref)

        hv = h_ref[...]
        for dp_ref, (c0, cw) in zip(dp_refs, cols):
            dw_ref[:, c0:c0 + cw] += _dot(hv, dp_ref[...], TN)

    row = lambda w_: pl.BlockSpec((TL, w_), lambda i: (i, 0))
    return pl.pallas_call(
        body, name="inproj_bwd_w", grid=(L // TL,),
        in_specs=[row(D)] + [row(cw) for _, cw in cols],
        out_specs=pl.BlockSpec((D, PW), lambda i: (0, 0)),
        out_shape=SDS((D, PW), F32),
        compiler_params=_cparams(("arbitrary",)),
    )(h, *slabs)


def _exchange_call(name, flows):
    n = len(flows)

    def body(*refs):
        srcs, dsts = refs[:n], refs[n:2 * n]
        send_sems, recv_sems, local_sems = refs[2 * n:]
        x, y, c = lax.axis_index("x"), lax.axis_index("y"), lax.axis_index("c")
        me = 4 * x + 2 * y + c
        copies = []
        for mask in range(1, N_DEV):
            px = 1 - x if mask & 4 else x
            py = 1 - y if mask & 2 else y
            pc = 1 - c if mask & 1 else c
            for f, (_, src_at, _, dst_at) in enumerate(flows):
                cp = pltpu.make_async_remote_copy(
                    src_ref=src_at(srcs[f], 4 * px + 2 * py + pc), dst_ref=dst_at(dsts[f], me),
                    send_sem=send_sems.at[mask - 1, f], recv_sem=recv_sems.at[mask - 1, f],
                    device_id=(px, py, pc), device_id_type=pl.DeviceIdType.MESH)
                cp.start()
                copies.append(cp)
        mine = [pltpu.make_async_copy(src_at(srcs[f], me), dst_at(dsts[f], me), local_sems.at[f])
                for f, (_, src_at, _, dst_at) in enumerate(flows)]
        for cp in mine:
            cp.start()
        for cp in copies + mine:
            cp.wait()

    return pl.pallas_call(
        body, name=name,
        in_specs=[pl.BlockSpec(memory_space=pl.ANY)] * n,
        out_specs=[pl.BlockSpec(memory_space=pl.ANY)] * n,
        out_shape=[SDS(tuple(shape), src.dtype) for src, _, shape, _ in flows],
        scratch_shapes=[pltpu.SemaphoreType.DMA((N_DEV - 1, n)), pltpu.SemaphoreType.DMA((N_DEV - 1, n)),
                        pltpu.SemaphoreType.DMA((n,))],
    )(*[f[0] for f in flows])


def _whole(ref, _):
    return ref


def _slot(ref, k):
    return ref.at[k]


_HBM_SPEC = pl.BlockSpec(memory_space=pltpu.HBM)
_SEM_SPEC = pl.BlockSpec(memory_space=pltpu.SEMAPHORE)
_DATAFLOW = pltpu.SideEffectType.DATAFLOW_SIDE_EFFECTING


def _split_copies(views, src_refs, land_refs, send_sems, recv_sems):
    x, y, c = lax.axis_index("x"), lax.axis_index("y"), lax.axis_index("c")
    me = 4 * x + 2 * y + c
    copies = []
    for mask in range(1, N_DEV):
        px = 1 - x if mask & 4 else x
        py = 1 - y if mask & 2 else y
        pc = 1 - c if mask & 1 else c
        for f, (src_at, dst_at) in enumerate(views):
            pair = (mask - 1) * len(views) + f
            copies.append(pltpu.make_async_remote_copy(
                src_ref=src_at(src_refs[f], 4 * px + 2 * py + pc), dst_ref=dst_at(land_refs[f], me),
                send_sem=send_sems.at[pair], recv_sem=recv_sems.at[pair],
                device_id=(px, py, pc), device_id_type=pl.DeviceIdType.MESH))
    return copies


def _split_start_call(name, srcs, lands, views):
    n = len(srcs)

    def body(*refs):
        src_refs, land_refs = refs[:n], refs[n:2 * n]
        send_sems, recv_sems, token = refs[2 * n], refs[2 * n + 1], refs[-1]
        for cp in _split_copies(views, src_refs, land_refs, send_sems, recv_sems):
            cp.start()
        token[...] = jnp.zeros_like(token)

    arrays = list(srcs) + list(lands)
    outs = pl.pallas_call(
        body, name=name,
        out_shape=(pltpu.SemaphoreType.DMA(((N_DEV - 1) * n,)), pltpu.SemaphoreType.DMA(((N_DEV - 1) * n,)),
                   *[pltpu.HBM(a.shape, a.dtype) for a in arrays], SDS((8, 128), F32)),
        in_specs=[_HBM_SPEC] * (2 * n),
        out_specs=(_SEM_SPEC, _SEM_SPEC, *[_HBM_SPEC] * (2 * n), pl.BlockSpec(memory_space=pltpu.VMEM)),
        input_output_aliases={i: 2 + i for i in range(2 * n)},
        compiler_params=pltpu.CompilerParams(has_side_effects=_DATAFLOW),
    )(*[pltpu.with_memory_space_constraint(a, pltpu.HBM) for a in arrays])
    return outs[0], outs[1], list(outs[2:2 + 2 * n]), outs[-1]


def _split_wait_call(name, send_sems, recv_sems, thru, views, after):
    n = len(views)

    def body(*refs):
        src_refs, land_refs = refs[:n], refs[n:2 * n]
        send, recv = refs[2 * n], refs[2 * n + 1]
        for cp in _split_copies(views, src_refs, land_refs, send, recv):
            cp.wait_send()
            cp.wait_recv()

    outs = pl.pallas_call(
        body, name=name,
        out_shape=tuple(pltpu.HBM(a.shape, a.dtype) for a in thru),
        in_specs=[_HBM_SPEC] * (2 * n) + [_SEM_SPEC, _SEM_SPEC, pl.BlockSpec(memory_space=pl.ANY)],
        out_specs=tuple([_HBM_SPEC] * (2 * n)),
        input_output_aliases={i: i for i in range(2 * n)},
        compiler_params=pltpu.CompilerParams(has_side_effects=_DATAFLOW),
    )(*thru, send_sems, recv_sems, after)
    return list(outs[n:])


def _own_slot(block, me):
    zone = jnp.zeros((N_DEV,) + block.shape, block.dtype)
    return lax.dynamic_update_slice(zone, block[None], (me,) + (0,) * block.ndim)


def _reduce_adamw_call(parts, w, m, v, block, name):
    nsrc = parts.shape[0]
    grid = tuple(s // b for s, b in zip(w.shape, block))
    c1 = 1.0 - ADAM_B1 ** ADAM_STEP
    c2 = 1.0 - ADAM_B2 ** ADAM_STEP

    def body(p_ref, w_ref, m_ref, v_ref, g_ref, d_ref, nm_ref, nv_ref):
        g = p_ref[0].astype(F32)
        for k in range(1, nsrc):
            g = g + p_ref[k].astype(F32)
        nm = ADAM_B1 * m_ref[...] + (1.0 - ADAM_B1) * g
        nv = ADAM_B2 * v_ref[...] + (1.0 - ADAM_B2) * (g * g)
        g_ref[...] = g
        nm_ref[...] = nm
        nv_ref[...] = nv
        d_ref[...] = -ADAM_LR * ((nm / c1) / (jnp.sqrt(nv / c2) + ADAM_EPS) + ADAM_WD * w_ref[...])

    own = pl.BlockSpec(tuple(block), lambda *i: i)
    return pl.pallas_call(
        body, name=name, grid=grid,
        in_specs=[pl.BlockSpec((nsrc,) + tuple(block), lambda *i: (0,) + i), own, own, own],
        out_specs=[own] * 4,
        out_shape=[SDS(w.shape, F32)] * 4,
        compiler_params=_cparams(("parallel",) * len(grid)),
    )(parts, w, m, v)


RELAYOUT_ROWS = 256
SHARD_COLS = IN_COLS // N_DEV


def _win_gather_layout_call(shards):
    def body(w_ref, o_ref):
        nat = jnp.concatenate([w_ref[k].astype(F32) for k in range(N_DEV)], axis=1)
        out = jnp.concatenate([nat[:, :3072], nat[:, 3080:], nat[:, 3072:3080],
                               jnp.zeros((RELAYOUT_ROWS, PW - IN_COLS), F32)], axis=1)
        o_ref[...] = out.astype(BF16)

    return pl.pallas_call(
        body, name="w_in_layout", grid=(D // RELAYOUT_ROWS,),
        in_specs=[pl.BlockSpec((N_DEV, RELAYOUT_ROWS, SHARD_COLS), lambda i: (0, i, 0))],
        out_specs=pl.BlockSpec((RELAYOUT_ROWS, PW), lambda i: (i, 0)),
        out_shape=SDS((D, PW), BF16),
        compiler_params=_cparams(("parallel",)),
    )(shards)


def _win_scatter_layout_call(grad):
    def body(g_ref, o_ref):
        g = g_ref[...]
        nat = jnp.concatenate([g[:, :3072], g[:, AB_COL:AB_COL + 8], g[:, 3072:AB_COL]], axis=1)
        for k in range(N_DEV):
            o_ref[k] = nat[:, SHARD_COLS * k:SHARD_COLS * (k + 1)].astype(BF16)

    return pl.pallas_call(
        body, name="w_in_grad_layout", grid=(D // RELAYOUT_ROWS,),
        in_specs=[pl.BlockSpec((RELAYOUT_ROWS, PW), lambda i: (i, 0))],
        out_specs=pl.BlockSpec((N_DEV, RELAYOUT_ROWS, SHARD_COLS), lambda i: (0, i, 0)),
        out_shape=SDS((N_DEV, D, SHARD_COLS), BF16),
        compiler_params=_cparams(("parallel",)),
    )(grad)


def _reduce_adamw_layers_call(parts, w, m, v, rows, name):
    _, R, C = w.shape
    c1 = 1.0 - ADAM_B1 ** ADAM_STEP
    c2 = 1.0 - ADAM_B2 ** ADAM_STEP

    def body(*refs):
        p_refs = refs[:DEPTH]
        w_ref, m_ref, v_ref, g_ref, d_ref, nm_ref, nv_ref = refs[DEPTH:]
        for l in range(DEPTH):
            @pl.when(pl.program_id(0) == l)
            def _(l=l):
                g = p_refs[l][0].astype(F32)
                for k in range(1, N_DEV):
                    g = g + p_refs[l][k].astype(F32)
                nm = ADAM_B1 * m_ref[0] + (1.0 - ADAM_B1) * g
                nv = ADAM_B2 * v_ref[0] + (1.0 - ADAM_B2) * (g * g)
                g_ref[0] = g
                nm_ref[0] = nm
                nv_ref[0] = nv
                d_ref[0] = -ADAM_LR * ((nm / c1) / (jnp.sqrt(nv / c2) + ADAM_EPS) + ADAM_WD * w_ref[0])

    def part_spec(l):
        return pl.BlockSpec((N_DEV, rows, C), lambda j, i: (0, jnp.where(j == l, i, 0), 0))

    own = pl.BlockSpec((1, rows, C), lambda j, i: (j, i, 0))
    return pl.pallas_call(
        body, name=name, grid=(DEPTH, R // rows),
        in_specs=[part_spec(l) for l in range(DEPTH)] + [own, own, own],
        out_specs=[own] * 4,
        out_shape=[SDS(w.shape, F32)] * 4,
        compiler_params=_cparams(("arbitrary", "arbitrary")),
    )(*parts, w, m, v)


_BIG = ("w_in", "w_out", "a_pw_w", "s5_glu_w")
_CONV = ("a_conv_w", "c_conv_w", "d_conv_w")
_CONV_TAPS = {"a_conv_w": KA, "c_conv_w": KC, "d_conv_w": KD}
_CONV_ROWS = {"a_conv_w": HALO, "c_conv_w": HALO_S, "d_conv_w": HALO_S}
_CONV_WIDTH = {"a_conv_w": BR, "c_conv_w": BR, "d_conv_w": 3 * BR}
_REPLICATED = ("norm_g", "a_conv_b", "a_ln_g", "a_ln_b", "a_pw_b", "s5_lambda_re", "s5_lambda_im", "s5_b_re", "s5_b_im",
               "s5_c_re", "s5_c_im", "s5_d", "s5_log_dt", "s5_glu_b", "d_a_log", "d_dt_bias", "d_norm_g", "final_g")
_WEIGHTS = ("norm_g", "w_in", "a_conv_w", "a_conv_b", "a_ln_g", "a_ln_b", "a_pw_w", "a_pw_b", "s5_lambda_re",
            "s5_lambda_im", "s5_b_re", "s5_b_im", "s5_c_re", "s5_c_im", "s5_d", "s5_log_dt", "s5_glu_w", "s5_glu_b",
            "c_conv_w", "d_conv_w", "d_a_log", "d_dt_bias", "d_norm_g", "w_out", "final_g")


def _size(shape):
    n = 1
    for s in shape:
        n *= s
    return n


PACK_ALIGN = 1024


def _piece_rows(n):
    return -(-n // PACK_ALIGN) * (PACK_ALIGN // 128)


def _pack_rows(pieces, row_mult):
    rows = []
    for p in pieces:
        flat = p.reshape(-1)
        rows.append(jnp.pad(flat, (0, _piece_rows(flat.shape[0]) * 128 - flat.shape[0])).reshape(-1, 128))
    out = jnp.concatenate(rows, axis=0)
    return jnp.pad(out, ((0, (-out.shape[0]) % row_mult), (0, 0)))


def _unpack(packed, shapes):
    out, row = [], 0
    for s in shapes:
        n = _size(s)
        nr = _piece_rows(n)
        out.append(packed[row:row + nr].reshape(-1)[:n].reshape(s))
        row += nr
    return out


_GATHER_VIEWS = [(_whole, _slot)] * 5


def _gather_start(shards, layer, me):
    srcs = [shards[n].astype(BF16) for n in _BIG]
    srcs.append(_pack_rows([shards[n] for n in _CONV], 8))
    lands = [_own_slot(s, me) for s in srcs]
    return _split_start_call("gather_start_%d" % layer, srcs, lands, _GATHER_VIEWS)


def _gather_finish(weights, layer, started, after):
    send, recv, thru, _ = started
    w_in, w_out, a_pw, glu, conv_all = _split_wait_call("gather_wait_%d" % layer, send, recv, thru, _GATHER_VIEWS, after)
    full = {"w_in": _win_gather_layout_call(w_in), "w_out": w_out.reshape(D, D), "a_pw_w": a_pw.reshape(BR, BR),
            "s5_glu_w": glu.reshape(BR, BR)}
    shapes = [weights[n].shape[1:] for n in _CONV]
    per_dev = [_unpack(conv_all[k], shapes) for k in range(N_DEV)]
    for i, n in enumerate(_CONV):
        whole = jnp.concatenate([per_dev[k][i] for k in range(N_DEV)], axis=-1)
        full[n] = jnp.pad(whole, ((0, _CONV_ROWS[n] - _CONV_TAPS[n]), (0, 0)))
    return full


def _rows_view(rows):
    return lambda ref, k: ref.at[pl.ds(k * rows, rows), :]


_SCATTER_VIEWS = [(_slot, _slot), (_rows_view(D // N_DEV), _slot), (_rows_view(BR // N_DEV), _slot),
                  (_rows_view(BR // N_DEV), _slot)]


def _scatter_start(grads, layer, me):
    srcs = [_win_scatter_layout_call(grads["w_in"])] + [grads[n].astype(BF16) for n in _BIG[1:]]
    own = [lax.dynamic_index_in_dim(srcs[0], me, 0, keepdims=False)]
    for s, rows in zip(srcs[1:], (D // N_DEV, BR // N_DEV, BR // N_DEV)):
        own.append(lax.dynamic_slice_in_dim(s, me * rows, rows, axis=0))
    lands = [_own_slot(o, me) for o in own]
    return _split_start_call("scatter_start_%d" % layer, srcs, lands, _SCATTER_VIEWS)


_S5_KERNEL_SHAPES = {"s5_lambda_re": (1, NSTATE), "s5_lambda_im": (1, NSTATE), "s5_log_dt": (1, 16),
                     "s5_b_re": (NSTATE, 16), "s5_b_im": (NSTATE, 16), "s5_c_re": (BR, 64), "s5_c_im": (BR, 64)}
_S5_KEYS = {"s5_lambda_re": "lam_re", "s5_lambda_im": "lam_im", "s5_log_dt": "log_dt", "s5_b_re": "b_re",
            "s5_b_im": "b_im", "s5_c_re": "c_re", "s5_c_im": "c_im"}


def _s5_inputs_all(weights):
    return {n: weights[n].reshape((DEPTH,) + s) for n, s in _S5_KERNEL_SHAPES.items()}


def _s5_inputs(p):
    return {_S5_KEYS[n]: p["s5_in"][n] for n in _S5_KERNEL_SHAPES}


def _row(a, width=None):
    a = a.reshape(1, -1)
    return a if width is None else jnp.pad(a, ((0, 0), (0, width - a.shape[1])))


def _layer_params(p):
    q = dict(p)
    for n in ("norm_g", "a_conv_b", "a_ln_g", "a_ln_b", "a_pw_b", "s5_d", "s5_glu_b", "d_norm_g"):
        q[n] = _row(p[n])
    q["d_a_log"] = _row(p["d_a_log"], 128)
    q["d_dt_bias"] = _row(p["d_dt_bias"], 128)
    return q


def _layer_fwd(x, p, deps=()):
    q = _layer_params(p)
    proj, h = _inproj_call(x, q["norm_g"], q["w_in"], deps)
    ya, yc = _ac_fwd_call(proj, q)
    prep = _s5_prep_call(_s5_inputs(p))
    yb, cin_r, cin_i = _s5_fwd_call(proj, prep, q)
    dq, dk, dv, dgb = _dn_pre_fwd_call(proj, q)
    yd, ssave = _dn_core_fwd_call(proj, dq, dk, dv, dgb, q["d_norm_g"])
    x_next = _outproj_call(x, (ya, yb, yc, yd), q["w_out"])
    saved = dict(x=x, proj=proj, h=h, ya=ya, yb=yb, yc=yc, yd=yd, cin_r=cin_r, cin_i=cin_i,
                 q=dq, k=dk, v=dv, gb=dgb, ssave=ssave, prep=prep)
    return x_next, saved


def _layer_bwd(dx, p, sv, deps=()):
    q = _layer_params(p)
    proj = sv["proj"]
    dya, dyb, dyc, dyd, g_wout = _outproj_bwd_call(dx, (sv["ya"], sv["yb"], sv["yc"], sv["yd"]), q["w_out"], deps)
    dpa, dpc, g_acw, g_acb, g_alg, g_alb, g_apw, g_apb, g_ccw = _ac_bwd_call(proj, q, dya, dyc)
    dpb, *s5g = _s5_bwd_call(proj, sv["prep"], q, sv["cin_r"], sv["cin_i"], dyb)
    g_sd, g_gw, g_gb = s5g[6:]
    g_lre, g_lim, g_ldt, g_bre, g_bim, g_cre, g_cim = _s5_prep_bwd_call(_s5_inputs(p), s5g[:6])
    dq, dk, dv, dgb, dz, g_ng = _dn_core_bwd_call(proj, sv["q"], sv["k"], sv["v"], sv["gb"], q["d_norm_g"], sv["ssave"], dyd)
    dqkv, dab, g_dcw, g_alog, g_dtb = _dn_pre_bwd_call(proj, q, dq, dk, dv, dgb)
    slabs = (dpa, dpb, dpc, dqkv, dz, dab)
    g_win = _inproj_bwd_w_call(sv["h"], slabs)
    dx_prev, g_ng0 = _inproj_bwd_x_call(slabs, q["w_in"], sv["x"], q["norm_g"], dx)
    grads = {"norm_g": g_ng0, "w_in": g_win, "a_conv_w": g_acw, "a_conv_b": g_acb, "a_ln_g": g_alg, "a_ln_b": g_alb,
             "a_pw_w": g_apw, "a_pw_b": g_apb, "s5_lambda_re": g_lre, "s5_lambda_im": g_lim, "s5_b_re": g_bre,
             "s5_b_im": g_bim, "s5_c_re": g_cre, "s5_c_im": g_cim, "s5_d": g_sd, "s5_log_dt": g_ldt, "s5_glu_w": g_gw,
             "s5_glu_b": g_gb, "c_conv_w": g_ccw, "d_conv_w": g_dcw, "d_a_log": g_alog[:, :NH], "d_dt_bias": g_dtb[:, :NH],
             "d_norm_g": g_ng, "w_out": g_wout}
    return dx_prev, grads


def _step(x, target, weights, moments_m, moments_v):
    me = 4 * lax.axis_index("x") + 2 * lax.axis_index("y") + lax.axis_index("c")
    layer_names = [n for n in _WEIGHTS if n != "final_g"]
    s5_all = _s5_inputs_all(weights)

    sharded = _BIG + _CONV
    gathers = [_gather_start({n: weights[n][0] for n in sharded}, 0, me)]
    first = _gather_finish(weights, 0, gathers[0], x)
    later, first["w_out"] = lax.optimization_barrier(({n: weights[n][1:] for n in sharded}, first["w_out"]))
    gathers += [_gather_start({n: later[n][l - 1] for n in sharded}, l, me) for l in range(1, DEPTH)]
    x_out, layers, saved = x, [], []
    for l in range(DEPTH):
        full = first if l == 0 else _gather_finish(weights, l, gathers[l], x_out)
        p = {n: (full[n] if n in full else weights[n][l]) for n in layer_names}
        p["s5_in"] = {n: a[l] for n, a in s5_all.items()}
        layers.append(p)
        x_out, sv = _layer_fwd(x_out, p, [g[3] for g in gathers[1:]] if l == 0 else ())
        saved.append(sv)
    dx0, g_final, loss_part = _loss_call(x_out, _row(weights["final_g"]), target)

    per_layer, scatters = [None] * DEPTH, [None] * DEPTH
    for l in range(DEPTH - 1, -1, -1):
        dx0, per_layer[l] = _layer_bwd(dx0, layers[l], saved[l], [scatters[l + 1][3]] if l + 1 < DEPTH else ())
        scatters[l] = _scatter_start(per_layer[l], l, me)
    stack = lambda n: jnp.stack([g[n] for g in per_layer])
    loss = lax.psum(loss_part[0, 0], ("x", "y", "c"))
    results = {}

    parts = [_split_wait_call("scatter_wait_%d" % l, scatters[l][0], scatters[l][1], scatters[l][2], _SCATTER_VIEWS, dx0)
             for l in range(DEPTH - 1, -1, -1)][::-1]
    rows = {"w_in": RELAYOUT_ROWS, "w_out": D // N_DEV, "a_pw_w": BR // N_DEV, "s5_glu_w": BR // N_DEV}
    for i, n in enumerate(_BIG):
        results[n] = _reduce_adamw_layers_call([parts[l][i] for l in range(DEPTH)], weights[n], moments_m[n],
                                               moments_v[n], rows[n], "adamw_" + n)

    grads = {n: stack(n).reshape(weights[n].shape) for n in _REPLICATED if n != "final_g"}
    grads["final_g"] = g_final.reshape(D)
    conv_g = [stack(n) for n in _CONV]
    pack = lambda d: _pack_rows([d[n] for n in _REPLICATED] + [jnp.zeros_like(c) for c in conv_g], 512)
    packed_g = _pack_rows([grads[n] for n in _REPLICATED] + conv_g, 512)
    gathered, = _exchange_call("gather_small_grads", [(packed_g, _whole, (N_DEV,) + packed_g.shape, _slot)])
    res = _reduce_adamw_call(gathered, pack(weights), pack(moments_m), pack(moments_v), (512, 128), "adamw_replicated")
    shapes = [weights[n].shape for n in _REPLICATED] + [c.shape for c in conv_g]
    res = [_unpack(r, shapes) for r in res]
    for i, n in enumerate(_REPLICATED):
        results[n] = tuple(r[i] for r in res)

    own_g = []
    for i, n in enumerate(_CONV):
        width = _CONV_WIDTH[n] // N_DEV
        summed = res[0][len(_REPLICATED) + i][:, :_CONV_TAPS[n], :]
        own_g.append(lax.dynamic_slice_in_dim(summed, me * width, width, axis=2))
    packc = lambda arrs: _pack_rows(arrs, 8)
    res = _reduce_adamw_call(packc(own_g)[None], packc([weights[n] for n in _CONV]), packc([moments_m[n] for n in _CONV]),
                             packc([moments_v[n] for n in _CONV]), packc(own_g).shape, "adamw_conv")
    res = [_unpack(r, [weights[n].shape for n in _CONV]) for r in res]
    for i, n in enumerate(_CONV):
        results[n] = tuple(r[i] for r in res)

    outs = [loss, dx0]
    for kind in range(4):
        outs += [results[n][kind] for n in _WEIGHTS]
    return tuple(outs)


def kernel(x, norm_g, w_in, a_conv_w, a_conv_b, a_ln_g, a_ln_b, a_pw_w, a_pw_b, s5_lambda_re, s5_lambda_im, s5_b_re, s5_b_im, s5_c_re, s5_c_im, s5_d, s5_log_dt, s5_glu_w, s5_glu_b, c_conv_w, d_conv_w, d_a_log, d_dt_bias, d_norm_g, w_out, final_g, loss_target, m_norm_g, m_w_in, m_a_conv_w, m_a_conv_b, m_a_ln_g, m_a_ln_b, m_a_pw_w, m_a_pw_b, m_s5_lambda_re, m_s5_lambda_im, m_s5_b_re, m_s5_b_im, m_s5_c_re, m_s5_c_im, m_s5_d, m_s5_log_dt, m_s5_glu_w, m_s5_glu_b, m_c_conv_w, m_d_conv_w, m_d_a_log, m_d_dt_bias, m_d_norm_g, m_w_out, m_final_g, v_norm_g, v_w_in, v_a_conv_w, v_a_conv_b, v_a_ln_g, v_a_ln_b, v_a_pw_w, v_a_pw_b, v_s5_lambda_re, v_s5_lambda_im, v_s5_b_re, v_s5_b_im, v_s5_c_re, v_s5_c_im, v_s5_d, v_s5_log_dt, v_s5_glu_w, v_s5_glu_b, v_c_conv_w, v_d_conv_w, v_d_a_log, v_d_dt_bias, v_d_norm_g, v_w_out, v_final_g):
    weights = dict(norm_g=norm_g, w_in=w_in, a_conv_w=a_conv_w, a_conv_b=a_conv_b, a_ln_g=a_ln_g, a_ln_b=a_ln_b, a_pw_w=a_pw_w, a_pw_b=a_pw_b, s5_lambda_re=s5_lambda_re, s5_lambda_im=s5_lambda_im, s5_b_re=s5_b_re, s5_b_im=s5_b_im, s5_c_re=s5_c_re, s5_c_im=s5_c_im, s5_d=s5_d, s5_log_dt=s5_log_dt, s5_glu_w=s5_glu_w, s5_glu_b=s5_glu_b, c_conv_w=c_conv_w, d_conv_w=d_conv_w, d_a_log=d_a_log, d_dt_bias=d_dt_bias, d_norm_g=d_norm_g, w_out=w_out, final_g=final_g)
    mom_m = dict(norm_g=m_norm_g, w_in=m_w_in, a_conv_w=m_a_conv_w, a_conv_b=m_a_conv_b, a_ln_g=m_a_ln_g, a_ln_b=m_a_ln_b, a_pw_w=m_a_pw_w, a_pw_b=m_a_pw_b, s5_lambda_re=m_s5_lambda_re, s5_lambda_im=m_s5_lambda_im, s5_b_re=m_s5_b_re, s5_b_im=m_s5_b_im, s5_c_re=m_s5_c_re, s5_c_im=m_s5_c_im, s5_d=m_s5_d, s5_log_dt=m_s5_log_dt, s5_glu_w=m_s5_glu_w, s5_glu_b=m_s5_glu_b, c_conv_w=m_c_conv_w, d_conv_w=m_d_conv_w, d_a_log=m_d_a_log, d_dt_bias=m_d_dt_bias, d_norm_g=m_d_norm_g, w_out=m_w_out, final_g=m_final_g)
    mom_v = dict(norm_g=v_norm_g, w_in=v_w_in, a_conv_w=v_a_conv_w, a_conv_b=v_a_conv_b, a_ln_g=v_a_ln_g, a_ln_b=v_a_ln_b, a_pw_w=v_a_pw_w, a_pw_b=v_a_pw_b, s5_lambda_re=v_s5_lambda_re, s5_lambda_im=v_s5_lambda_im, s5_b_re=v_s5_b_re, s5_b_im=v_s5_b_im, s5_c_re=v_s5_c_re, s5_c_im=v_s5_c_im, s5_d=v_s5_d, s5_log_dt=v_s5_log_dt, s5_glu_w=v_s5_glu_w, s5_glu_b=v_s5_glu_b, c_conv_w=v_c_conv_w, d_conv_w=v_d_conv_w, d_a_log=v_d_a_log, d_dt_bias=v_d_dt_bias, d_norm_g=v_d_norm_g, w_out=v_w_out, final_g=v_final_g)
    outs = _step(x[0], loss_target[0], weights, mom_m, mom_v)
    return (outs[0], outs[1][None]) + outs[2:]
```

```python
import functools

import jax
import jax.numpy as jnp
from jax import lax
from jax.experimental import pallas as pl
from jax.experimental.pallas import tpu as pltpu

F32 = jnp.float32
BF16 = jnp.bfloat16
HI = lax.Precision.HIGHEST
SDS = jax.ShapeDtypeStruct

N_DEV = 8
D = 1024
BR = 256
DEPTH = 4
IN_COLS = 3336
PW = 3456
AB_COL = 3328
EPS = 1e-6
TL = 512
SEG = TL // 8
HALO = 32
HALO_S = 8
KA, KC, KD = 31, 3, 4
CH = 64
DN_GROUP = 4
NH, HD = 4, 64
NSTATE = 1024
VMEM_LIMIT = 56 * 1024 * 1024

ADAM_LR, ADAM_B1, ADAM_B2, ADAM_EPS, ADAM_WD, ADAM_STEP = 0.001, 0.9, 0.999, 1e-08, 0.01, 10

NN = ((1,), (0,))
NT = ((1,), (1,))
TN = ((0,), (0,))


def _dot(a, b, dims, prec=None):
    return lax.dot_general(a, b, (dims, ((), ())), precision=prec, preferred_element_type=F32)


def _make_mm(cast, prec, fwd_dims):
    def prep(t):
        return t.astype(cast) if cast is not None else t

    @jax.custom_vjp
    def mm(a, w):
        return _dot(prep(a), prep(w), fwd_dims, prec)

    def fwd(a, w):
        return mm(a, w), (a, w)

    def bwd(res, dy):
        a, w = res
        a, w, dy = prep(a), prep(w), prep(dy)
        if fwd_dims == NN:
            return _dot(dy, w, NT, prec), _dot(a, dy, TN, prec)
        if fwd_dims == NT:
            return _dot(dy, w, NN, prec), _dot(dy, a, TN, prec)
        return _dot(w, dy, NT, prec), _dot(a, dy, NN, prec)

    mm.defvjp(fwd, bwd)
    return mm


mm = _make_mm(BF16, None, NN)
mm_nt = _make_mm(BF16, None, NT)
mm_tn = _make_mm(BF16, None, TN)
mmh = _make_mm(None, HI, NN)
mmh_nt = _make_mm(None, HI, NT)


def _sigmoid(x):
    return jax.nn.sigmoid(x)


def _silu(x):
    return x * jax.nn.sigmoid(x)


def _gelu(x):
    return 0.5 * x * (1.0 + jnp.tanh(0.7978845608028654 * (x + 0.044715 * (x * x * x))))


def _softplus(x):
    return jnp.maximum(x, 0.0) + jnp.log1p(jnp.exp(-jnp.abs(x)))


def _rms(x, g):
    return x * lax.rsqrt(jnp.mean(x * x, axis=-1, keepdims=True) + EPS) * g


def _cparams(sem):
    return pltpu.CompilerParams(dimension_semantics=sem, vmem_limit_bytes=VMEM_LIMIT)


def _tap_offsets(halo, taps):
    return [halo - (taps - 1) + k for k in range(taps)]


def _conv_fwd_impl(acat, w, tile, halo, taps):
    n = tile + halo
    out = None
    for k, off in enumerate(_tap_offsets(halo, taps)):
        src = jnp.roll(acat, n - off, axis=0)[:tile, :] if off != halo else acat[halo:, :]
        term = src * w[k:k + 1, :]
        out = term if out is None else out + term
    return out


def _make_conv(tile, halo, taps):
    @jax.custom_vjp
    def conv(acat, w):
        return _conv_fwd_impl(acat, w, tile, halo, taps)

    def fwd(acat, w):
        return conv(acat, w), (acat, w)

    def bwd(res, dy):
        acat, w = res
        n = tile + halo
        dyp = jnp.concatenate([dy, jnp.zeros((halo, dy.shape[1]), F32)], axis=0)
        rows = lax.broadcasted_iota(jnp.int32, w.shape, 0)
        dacat = None
        dw = jnp.zeros(w.shape, F32)
        for k, off in enumerate(_tap_offsets(halo, taps)):
            term = jnp.roll(dyp, off, axis=0) * w[k:k + 1, :]
            dacat = term if dacat is None else dacat + term
            src = jnp.roll(acat, n - off, axis=0)[:tile, :] if off != halo else acat[halo:, :]
            dw = dw + jnp.where(rows == k, jnp.sum(dy * src, axis=0, keepdims=True), 0.0)
        return dacat, dw

    conv.defvjp(fwd, bwd)
    return conv


def _halo_spec(tile, halo, width, col):
    per = tile // halo
    return pl.BlockSpec((halo, width), lambda i: (jnp.maximum(i * per - 1, 0), col))


def _halo_spec_rev(nt, tile, halo, width, col):
    per = tile // halo
    return pl.BlockSpec((halo, width), lambda i: (jnp.maximum((nt - 1 - i) * per - 1, 0), col))


def _dep_specs(deps):
    return [pl.BlockSpec((8, 128), lambda *_: (0, 0)) for _ in deps]


def _inproj_call(x, g, w, deps=()):
    L = x.shape[0]

    def body(x_ref, g_ref, w_ref, *rest):
        p_ref, h_ref = rest[len(deps):]
        h = _rms(x_ref[...], g_ref[...]).astype(BF16)
        h_ref[...] = h
        p_ref[...] = _dot(h, w_ref[...], NN)

    return pl.pallas_call(
        body, name="inproj", grid=(L // TL,),
        in_specs=[pl.BlockSpec((TL, D), lambda i: (i, 0)), pl.BlockSpec((1, D), lambda i: (0, 0)),
                  pl.BlockSpec((D, PW), lambda i: (0, 0))] + _dep_specs(deps),
        out_specs=[pl.BlockSpec((TL, PW), lambda i: (i, 0)), pl.BlockSpec((TL, D), lambda i: (i, 0))],
        out_shape=[SDS((L, PW), F32), SDS((L, D), BF16)],
        compiler_params=_cparams(("parallel",)),
    )(x, g, w, *deps)


def _outproj_call(x, ys, w):
    L = x.shape[0]

    def body(x_ref, a_ref, b_ref, c_ref, d_ref, w_ref, o_ref):
        acc = x_ref[...]
        for b, y_ref in enumerate((a_ref, b_ref, c_ref, d_ref)):
            acc = acc + _dot(y_ref[...].astype(BF16), w_ref[b * BR:(b + 1) * BR, :], NN)
        o_ref[...] = acc

    yspec = pl.BlockSpec((TL, BR), lambda i: (i, 0))
    return pl.pallas_call(
        body, name="outproj", grid=(L // TL,),
        in_specs=[pl.BlockSpec((TL, D), lambda i: (i, 0)), yspec, yspec, yspec, yspec,
                  pl.BlockSpec((D, D), lambda i: (0, 0))],
        out_specs=pl.BlockSpec((TL, D), lambda i: (i, 0)),
        out_shape=SDS((L, D), F32),
        compiler_params=_cparams(("parallel",)),
    )(x, *ys, w)


def _loss_call(x, g, target):
    L = x.shape[0]

    def body(x_ref, g_ref, t_ref, dx_ref, dg_ref, loss_ref):
        @pl.when(pl.program_id(0) == 0)
        def _():
            dg_ref[...] = jnp.zeros_like(dg_ref)
            loss_ref[...] = jnp.zeros_like(loss_ref)

        y, vjp = jax.vjp(_rms, x_ref[...], g_ref[...])
        err = y - t_ref[...]
        dx, dg = vjp(err * (1.0 / D))
        dx_ref[...] = dx
        dg_ref[...] += dg
        tot = jnp.sum(jnp.sum(err * err, axis=1, keepdims=True), axis=0, keepdims=True)
        loss_ref[...] += jnp.broadcast_to(tot * (0.5 / D), loss_ref.shape)

    return pl.pallas_call(
        body, name="loss_head", grid=(L // TL,),
        in_specs=[pl.BlockSpec((TL, D), lambda i: (i, 0)), pl.BlockSpec((1, D), lambda i: (0, 0)),
                  pl.BlockSpec((TL, D), lambda i: (i, 0))],
        out_specs=[pl.BlockSpec((TL, D), lambda i: (i, 0)), pl.BlockSpec((1, D), lambda i: (0, 0)),
                   pl.BlockSpec((1, 128), lambda i: (0, 0))],
        out_shape=[SDS((L, D), F32), SDS((1, D), F32), SDS((1, 128), F32)],
        compiler_params=_cparams(("arbitrary",)),
    )(x, g, target)


def _branch_a(valw, gatew, z, cw, cb, lg, lb, pw, pb, conv):
    a = conv(valw * _sigmoid(gatew), cw) + cb
    mu = jnp.mean(a, axis=-1, keepdims=True)
    xc = a - mu
    y = xc * lax.rsqrt(jnp.mean(xc * xc, axis=-1, keepdims=True) + EPS) * lg + lb
    y = mm(_silu(y), pw) + pb
    return y * _silu(z)


def _branch_c(bg, cw_, xw, z, w3, conv):
    return bg * conv(cw_ * xw, w3) * _silu(z)


def _ac_fwd_call(proj, p):
    L = proj.shape[0]

    def body(val, gate, za, hval, hgate, cb_, cc, cx, cz, hcc, hcx,
             acw, acb, alg, alb, apw, apb, ccw, ya_ref, yc_ref):
        nf = (pl.program_id(0) > 0).astype(F32)
        win = lambda h, m: jnp.concatenate([h[...] * nf, m[...]], axis=0)
        conv_a = functools.partial(_conv_fwd_impl, tile=TL, halo=HALO, taps=KA)
        conv_c = functools.partial(_conv_fwd_impl, tile=TL, halo=HALO, taps=KC)
        ya_ref[...] = _branch_a(win(hval, val), win(hgate, gate), za[...], acw[...], acb[...], alg[...], alb[...],
                                apw[...], apb[...], conv_a)
        yc_ref[...] = _branch_c(cb_[...], win(hcc, cc), win(hcx, cx), cz[...], ccw[...], conv_c)

    col = lambda j: pl.BlockSpec((TL, BR), lambda i: (i, j))
    hal = lambda j: _halo_spec(TL, HALO, BR, j)
    full = lambda a: pl.BlockSpec(a.shape, lambda i: (0,) * a.ndim)
    params = (p["a_conv_w"], p["a_conv_b"], p["a_ln_g"], p["a_ln_b"], p["a_pw_w"], p["a_pw_b"], p["c_conv_w"])
    return pl.pallas_call(
        body, name="ac_fwd", grid=(L // TL,),
        in_specs=[col(0), col(1), col(2), hal(0), hal(1), col(5), col(6), col(7), col(8), hal(6), hal(7)]
        + [full(a) for a in params],
        out_specs=[pl.BlockSpec((TL, BR), lambda i: (i, 0))] * 2,
        out_shape=[SDS((L, BR), F32)] * 2,
        compiler_params=_cparams(("parallel",)),
    )(*([proj] * 11), *params)


def _ac_bwd_call(proj, p, dya, dyc):
    L = proj.shape[0]
    nt = L // TL

    def body(val, gate, za, hval, hgate, cb_, cc, cx, cz, hcc, hcx,
             acw, acb, alg, alb, apw, apb, ccw, dya_ref, dyc_ref,
             da_ref, dc_ref, g_acw, g_acb, g_alg, g_alb, g_apw, g_apb, g_ccw, carry):
        i = pl.program_id(0)
        gouts = (g_acw, g_acb, g_alg, g_alb, g_apw, g_apb, g_ccw)

        @pl.when(i == 0)
        def _():
            carry[...] = jnp.zeros_like(carry)
            for r in gouts:
                r[...] = jnp.zeros_like(r)

        nf = (i < nt - 1).astype(F32)
        win = lambda h, m: jnp.concatenate([h[...] * nf, m[...]], axis=0)
        conv_a = _make_conv(TL, HALO, KA)
        conv_c = _make_conv(TL, HALO, KC)

        def f(valw, gatew, z, bg, ccw_, cxw, czv, w1, b1, lg, lb, pw, pb, w3):
            return (_branch_a(valw, gatew, z, w1, b1, lg, lb, pw, pb, conv_a),
                    _branch_c(bg, ccw_, cxw, czv, w3, conv_c))

        _, vjp = jax.vjp(f, win(hval, val), win(hgate, gate), za[...], cb_[...], win(hcc, cc), win(hcx, cx), cz[...],
                         acw[...], acb[...], alg[...], alb[...], apw[...].astype(F32), apb[...], ccw[...])
        (dvalw, dgatew, dz, dbg, dccw, dcxw, dczv, d1, d2, d3, d4, d5, d6, d7) = vjp((dya_ref[...], dyc_ref[...]))

        def settle(slot, dwin):
            tail = jnp.concatenate([jnp.zeros((TL - HALO, BR), F32), carry[slot]], axis=0)
            carry[slot] = dwin[:HALO, :]
            return (dwin[HALO:, :] + tail).astype(BF16)

        da_ref[:, 0:BR] = settle(0, dvalw)
        da_ref[:, BR:2 * BR] = settle(1, dgatew)
        da_ref[:, 2 * BR:3 * BR] = dz.astype(BF16)
        dc_ref[:, 0:BR] = dbg.astype(BF16)
        dc_ref[:, BR:2 * BR] = settle(2, dccw)
        dc_ref[:, 2 * BR:3 * BR] = settle(3, dcxw)
        dc_ref[:, 3 * BR:4 * BR] = dczv.astype(BF16)
        for r, g in zip(gouts, (d1, d2, d3, d4, d5, d6, d7)):
            r[...] += g

    col = lambda j: pl.BlockSpec((TL, BR), lambda i: (nt - 1 - i, j))
    hal = lambda j: _halo_spec_rev(nt, TL, HALO, BR, j)
    full = lambda a: pl.BlockSpec(a.shape, lambda i: (0,) * a.ndim)
    params = (p["a_conv_w"], p["a_conv_b"], p["a_ln_g"], p["a_ln_b"], p["a_pw_w"], p["a_pw_b"], p["c_conv_w"])
    rev = lambda w: pl.BlockSpec((TL, w), lambda i: (nt - 1 - i, 0))
    return pl.pallas_call(
        body, name="ac_bwd", grid=(nt,),
        in_specs=[col(0), col(1), col(2), hal(0), hal(1), col(5), col(6), col(7), col(8), hal(6), hal(7)]
        + [full(a) for a in params] + [rev(BR), rev(BR)],
        out_specs=[rev(3 * BR), rev(4 * BR)] + [full(a) for a in params],
        out_shape=[SDS((L, 3 * BR), BF16), SDS((L, 4 * BR), BF16)] + [SDS(a.shape, F32) for a in params],
        scratch_shapes=[pltpu.VMEM((4, HALO, BR), F32)],
        compiler_params=_cparams(("arbitrary",)),
    )(*([proj] * 11), *params, dya, dyc)


def _iota2(shape, dim):
    return lax.broadcasted_iota(jnp.int32, shape, dim)


def _s5_params(lam_re, lam_im, logdt, b_re, b_im, c_re, c_im):
    eg = (_iota2((16, NSTATE), 1) >> 6 == _iota2((16, NSTATE), 0)).astype(F32)
    dt = jnp.exp(mmh(jnp.broadcast_to(logdt, (8, 16)), eg)[0:1, :])
    lr = jnp.minimum(lam_re, -1e-4)
    li = lam_im
    mag = jnp.exp(lr * dt)
    lbr = mag * jnp.cos(li * dt)
    lbi = mag * jnp.sin(li * dt)
    den = lr * lr + li * li
    nr = lbr - 1.0
    fr = (nr * lr + lbi * li) / den
    fi = (lbi * lr - nr * li) / den
    row = _iota2((8, NSTATE), 0)
    f8 = jnp.where(row == 0, fr, jnp.where(row == 1, fi, 0.0))
    eye = (_iota2((NSTATE, NSTATE), 0) == _iota2((NSTATE, NSTATE), 1)).astype(F32)
    fcol = mmh_nt(eye, f8)
    frc, fic = fcol[:, 0:1], fcol[:, 1:2]
    bbr = frc * b_re - fic * b_im
    bbi = frc * b_im + fic * b_re
    e1 = ((_iota2((16, BR), 1) & 15) == _iota2((16, BR), 0)).astype(F32)
    m1 = ((_iota2((NSTATE, BR), 0) >> 6) == (_iota2((NSTATE, BR), 1) >> 4)).astype(F32)
    wbr = mmh(bbr, e1) * m1
    wbi = mmh(bbi, e1) * m1
    e2 = ((_iota2((64, NSTATE), 1) & 63) == _iota2((64, NSTATE), 0)).astype(F32)
    m2 = ((_iota2((BR, NSTATE), 0) >> 4) == (_iota2((BR, NSTATE), 1) >> 6)).astype(F32)
    wcr = mmh(c_re, e2) * m2
    wci = mmh(c_im, e2) * m2
    return lbr, lbi, wbr, wbi, wcr, wci


_S5_OUT = [(1, NSTATE), (1, NSTATE), (NSTATE, BR), (NSTATE, BR), (BR, NSTATE), (BR, NSTATE)]


def _s5_prep_call(sp, deps=()):
    def body(lre, lim, ldt, bre, bim, cre, cim, *rest):
        o_lbr, o_lbi, o_wbr, o_wbi, o_wcr, o_wci, pwr, pwi, qwr, qwi = rest[len(deps):]
        lbr, lbi, wbr, wbi, wcr, wci = _s5_params(lre[...], lim[...], ldt[...], bre[...], bim[...], cre[...], cim[...])
        o_lbr[...], o_lbi[...], o_wbr[...], o_wbi[...], o_wcr[...], o_wci[...] = lbr, lbi, wbr, wbi, wcr, wci
        pr, pi = lbr, lbi
        for i in range(SEG):
            pwr[i:i + 1, :] = pr
            pwi[i:i + 1, :] = pi
            qwr[SEG - 1 - i:SEG - i, :] = pr
            qwi[SEG - 1 - i:SEG - i, :] = -pi
            pr, pi = pr * lbr - pi * lbi, pr * lbi + pi * lbr

    args = (sp["lam_re"], sp["lam_im"], sp["log_dt"], sp["b_re"], sp["b_im"], sp["c_re"], sp["c_im"])
    return pl.pallas_call(
        body, name="s5_prep",
        out_shape=[SDS(s, F32) for s in _S5_OUT] + [SDS((SEG, NSTATE), F32)] * 4,
        compiler_params=pltpu.CompilerParams(vmem_limit_bytes=VMEM_LIMIT),
    )(*args, *deps)


def _s5_prep_bwd_call(sp, cots):
    def body(lre, lim, ldt, bre, bim, cre, cim, c0, c1, c2, c3, c4, c5, *outs):
        _, vjp = jax.vjp(_s5_params, lre[...], lim[...], ldt[...], bre[...], bim[...], cre[...], cim[...])
        grads = vjp((c0[...], c1[...], c2[...], c3[...], c4[...], c5[...]))
        for o, g in zip(outs, grads):
            o[...] = g

    args = (sp["lam_re"], sp["lam_im"], sp["log_dt"], sp["b_re"], sp["b_im"], sp["c_re"], sp["c_im"])
    return pl.pallas_call(
        body, name="s5_prep_bwd",
        out_shape=[SDS(a.shape, F32) for a in args],
        compiler_params=pltpu.CompilerParams(vmem_limit_bytes=VMEM_LIMIT),
    )(*args, *cots)


def _lanes(v, j):
    return v[:, j * 128:(j + 1) * 128]


def _s5_scan(sre, sim, pwr, pwi, cin_r, cin_i, reverse):
    row = _iota2((8, 128), 0)
    steps = (1, 2, 4)

    def lane_consts(j):
        lanes = slice(j * 128, (j + 1) * 128)
        if reverse:
            mult = [(jnp.broadcast_to(pwr[SEG - d:SEG - d + 1, lanes], (8, 128)),
                     jnp.broadcast_to(pwi[SEG - d:SEG - d + 1, lanes], (8, 128))) for d in steps]
            return mult, pwr[SEG - 8:SEG, lanes], pwi[SEG - 8:SEG, lanes]
        mult = [(jnp.broadcast_to(pwr[d - 1:d, lanes], (8, 128)),
                 jnp.broadcast_to(pwi[d - 1:d, lanes], (8, 128))) for d in steps]
        return mult, pwr[0:8, lanes], pwi[0:8, lanes]

    consts = [lane_consts(j) for j in range(8)]
    nblk = TL // 8

    def block(t, carry):
        b = nblk - 1 - t if reverse else t
        rows = pl.ds(pl.multiple_of(b * 8, 8), 8)
        new = []
        for j in range(8):
            mult, p8r, p8i = consts[j]
            vr, vi = sre.at[j], sim.at[j]
            sr, si = vr[rows, :], vi[rows, :]
            for d, (mr, mi) in zip(steps, mult):
                if reverse:
                    hr = jnp.where(row < 8 - d, pltpu.roll(sr, 8 - d, 0), 0.0)
                    hi = jnp.where(row < 8 - d, pltpu.roll(si, 8 - d, 0), 0.0)
                else:
                    hr = jnp.where(row >= d, pltpu.roll(sr, d, 0), 0.0)
                    hi = jnp.where(row >= d, pltpu.roll(si, d, 0), 0.0)
                sr, si = sr + mr * hr - mi * hi, si + mr * hi + mi * hr
            cr, ci = carry[2 * j], carry[2 * j + 1]
            sr, si = sr + p8r * cr - p8i * ci, si + p8r * ci + p8i * cr
            vr[rows, :] = sr
            vi[rows, :] = si
            edge = slice(0, 1) if reverse else slice(7, 8)
            new += [sr[edge, :], si[edge, :]]
        return tuple(new)

    init = []
    for j in range(8):
        init += [_lanes(cin_r, j), _lanes(cin_i, j)]
    ends = lax.fori_loop(0, nblk, block, tuple(init))
    return (jnp.concatenate([ends[2 * j] for j in range(8)], axis=1),
            jnp.concatenate([ends[2 * j + 1] for j in range(8)], axis=1))


def _bdot(a, b, dims):
    return _dot(a.astype(BF16), b.astype(BF16), dims)


def _s5_states(u, wbr, wbi, sre, sim):
    bur = _bdot(u, wbr, NT)
    bui = _bdot(u, wbi, NT)
    for j in range(8):
        sre[j] = _lanes(bur, j)
        sim[j] = _lanes(bui, j)


def _gather_lanes(s):
    return jnp.concatenate([s[j] for j in range(8)], axis=1)


def _s5_post(s_re, s_im, u, z, wcr, wci, dsk, gw, gb):
    y = mm_nt(s_re, wcr) - mm_nt(s_im, wci) + dsk * u
    yg = _gelu(y)
    return yg * _sigmoid(mm(yg, gw) + gb) * _silu(z)


def _s5_fwd_call(proj, prep, p):
    L = proj.shape[0]
    nt = L // TL
    lbr, lbi, wbr, wbi, wcr, wci, pwr, pwi, _, _ = prep

    def body(u_ref, z_ref, lbr_r, lbi_r, wbr_r, wbi_r, wcr_r, wci_r, pwr_r, pwi_r, d_r, gw_r, gb_r,
             yb_ref, cinr_ref, cini_ref, sre, sim, car, cai):
        @pl.when(pl.program_id(0) == 0)
        def _():
            car[...] = jnp.zeros_like(car)
            cai[...] = jnp.zeros_like(cai)

        u = u_ref[...]
        cinr_ref[0] = car[...]
        cini_ref[0] = cai[...]
        _s5_states(u, wbr_r[...], wbi_r[...], sre, sim)
        nr, ni = _s5_scan(sre, sim, pwr_r, pwi_r, car[...], cai[...], False)
        car[...] = nr
        cai[...] = ni
        yb_ref[...] = _s5_post(_gather_lanes(sre), _gather_lanes(sim), u, z_ref[...], wcr_r[...], wci_r[...],
                               d_r[...], gw_r[...], gb_r[...])

    full = lambda a: pl.BlockSpec(a.shape, lambda i: (0,) * a.ndim)
    consts = (lbr, lbi, wbr, wbi, wcr, wci, pwr, pwi, p["s5_d"], p["s5_glu_w"], p["s5_glu_b"])
    cspec = pl.BlockSpec((1, 1, NSTATE), lambda i: (i, 0, 0))
    return pl.pallas_call(
        body, name="s5_fwd", grid=(nt,),
        in_specs=[pl.BlockSpec((TL, BR), lambda i: (i, 3)), pl.BlockSpec((TL, BR), lambda i: (i, 4))]
        + [full(a) for a in consts],
        out_specs=[pl.BlockSpec((TL, BR), lambda i: (i, 0)), cspec, cspec],
        out_shape=[SDS((L, BR), F32), SDS((nt, 1, NSTATE), F32), SDS((nt, 1, NSTATE), F32)],
        scratch_shapes=[pltpu.VMEM((8, TL, 128), F32), pltpu.VMEM((8, TL, 128), F32),
                        pltpu.VMEM((1, NSTATE), F32), pltpu.VMEM((1, NSTATE), F32)],
        compiler_params=_cparams(("arbitrary",)),
    )(proj, proj, *consts)


def _s5_bwd_call(proj, prep, p, cin_r, cin_i, dyb):
    L = proj.shape[0]
    nt = L // TL
    lbr, lbi, wbr, wbi, wcr, wci, pwr, pwi, qwr, qwi = prep

    def body(u_ref, z_ref, lbr_r, lbi_r, wbr_r, wbi_r, wcr_r, wci_r, pwr_r, pwi_r, qwr_r, qwi_r, d_r, gw_r, gb_r,
             cinr_ref, cini_ref, dy_ref,
             db_ref, g_lbr, g_lbi, g_wbr, g_wbi, g_wcr, g_wci, g_d, g_gw, g_gb, sre, sim, gre, gim, car, cai):
        gouts = (g_lbr, g_lbi, g_wbr, g_wbi, g_wcr, g_wci, g_d, g_gw, g_gb)

        @pl.when(pl.program_id(0) == 0)
        def _():
            car[...] = jnp.zeros_like(car)
            cai[...] = jnp.zeros_like(cai)
            for r in gouts:
                r[...] = jnp.zeros_like(r)

        u = u_ref[...]
        lr, li = lbr_r[...], lbi_r[...]
        c0r, c0i = cinr_ref[0], cini_ref[0]
        _s5_states(u, wbr_r[...], wbi_r[...], sre, sim)
        _s5_scan(sre, sim, pwr_r, pwi_r, c0r, c0i, False)
        s_re, s_im = _gather_lanes(sre), _gather_lanes(sim)
        _, vjp = jax.vjp(_s5_post, s_re, s_im, u, z_ref[...], wcr_r[...], wci_r[...], d_r[...],
                         gw_r[...].astype(F32), gb_r[...])
        ds_re, ds_im, du, dz, dwcr, dwci, dd, dgw, dgb = vjp(dy_ref[...])
        for j in range(8):
            gre[j] = _lanes(ds_re, j)
            gim[j] = _lanes(ds_im, j)
        nr, ni = _s5_scan(gre, gim, qwr_r, qwi_r, car[...], cai[...], True)
        car[...] = nr
        cai[...] = ni
        a_re, a_im = _gather_lanes(gre), _gather_lanes(gim)
        first = _iota2((TL, NSTATE), 0) == 0
        p_re = jnp.where(first, c0r, jnp.roll(s_re, 1, axis=0))
        p_im = jnp.where(first, c0i, jnp.roll(s_im, 1, axis=0))
        g_lbr[...] += jnp.sum(a_re * p_re + a_im * p_im, axis=0, keepdims=True)
        g_lbi[...] += jnp.sum(a_im * p_re - a_re * p_im, axis=0, keepdims=True)
        du = du + _bdot(a_re, wbr_r[...], NN) + _bdot(a_im, wbi_r[...], NN)
        g_wbr[...] += _bdot(a_re, u, TN)
        g_wbi[...] += _bdot(a_im, u, TN)
        g_wcr[...] += dwcr
        g_wci[...] += dwci
        g_d[...] += dd
        g_gw[...] += dgw
        g_gb[...] += dgb
        db_ref[:, 0:BR] = du.astype(BF16)
        db_ref[:, BR:2 * BR] = dz.astype(BF16)

    full = lambda a: pl.BlockSpec(a.shape, lambda i: (0,) * a.ndim)
    consts = (lbr, lbi, wbr, wbi, wcr, wci, pwr, pwi, qwr, qwi, p["s5_d"], p["s5_glu_w"], p["s5_glu_b"])
    cspec = pl.BlockSpec((1, 1, NSTATE), lambda i: (nt - 1 - i, 0, 0))
    gshapes = _S5_OUT + [(1, BR), (BR, BR), (1, BR)]
    return pl.pallas_call(
        body, name="s5_bwd", grid=(nt,),
        in_specs=[pl.BlockSpec((TL, BR), lambda i: (nt - 1 - i, 3)), pl.BlockSpec((TL, BR), lambda i: (nt - 1 - i, 4))]
        + [full(a) for a in consts] + [cspec, cspec, pl.BlockSpec((TL, BR), lambda i: (nt - 1 - i, 0))],
        out_specs=[pl.BlockSpec((TL, 2 * BR), lambda i: (nt - 1 - i, 0))]
        + [pl.BlockSpec(s, lambda i: (0, 0)) for s in gshapes],
        out_shape=[SDS((L, 2 * BR), BF16)] + [SDS(s, F32) for s in gshapes],
        scratch_shapes=[pltpu.VMEM((8, TL, 128), F32)] * 4 + [pltpu.VMEM((1, NSTATE), F32)] * 2,
        compiler_params=_cparams(("arbitrary",)),
    )(proj, proj, *consts, cin_r, cin_i, dyb)


def _heads(x):
    return [x[:, h * HD:(h + 1) * HD] for h in range(NH)]


def _l2n(x, scale):
    return jnp.concatenate([xh * (lax.rsqrt(jnp.sum(xh * xh, axis=-1, keepdims=True) + EPS) * scale)
                            for xh in _heads(x)], axis=1)


def _dn_pre(qkvw, ab, cw, alog, dtb, conv, rows):
    c = _silu(conv(qkvw, cw))
    q = _l2n(c[:, 0:BR], HD ** -0.5)
    k = _l2n(c[:, BR:2 * BR], 1.0)
    v = c[:, 2 * BR:3 * BR]
    g = -jnp.exp(alog) * _softplus(ab + dtb)
    ri, ci = _iota2((rows, rows), 0), _iota2((rows, rows), 1)
    tri = ((ri >= ci) & ((ri >> 6) == (ci >> 6))).astype(F32)
    gc = mmh(tri, g)
    lane = _iota2(ab.shape, 1)
    return q, k, v, jnp.where(lane < NH, gc, jnp.where(lane < 2 * NH, _sigmoid(ab), 0.0))


def _dn_pre_fwd_call(proj, p):
    L = proj.shape[0]

    def body(m_ref, h_ref, ab_ref, cw, alog, dtb, q_ref, k_ref, v_ref, gb_ref):
        nf = (pl.program_id(0) > 0).astype(F32)
        qkvw = jnp.concatenate([h_ref[...] * nf, m_ref[...]], axis=0)
        conv = functools.partial(_conv_fwd_impl, tile=TL, halo=HALO_S, taps=KD)
        q_ref[...], k_ref[...], v_ref[...], gb_ref[...] = _dn_pre(qkvw, ab_ref[...], cw[...], alog[...], dtb[...], conv, TL)

    full = lambda a: pl.BlockSpec(a.shape, lambda i: (0,) * a.ndim)
    params = (p["d_conv_w"], p["d_a_log"], p["d_dt_bias"])
    o = pl.BlockSpec((TL, BR), lambda i: (i, 0))
    return pl.pallas_call(
        body, name="dn_pre_fwd", grid=(L // TL,),
        in_specs=[pl.BlockSpec((TL, 3 * BR), lambda i: (i, 3)), _halo_spec(TL, HALO_S, 3 * BR, 3),
                  pl.BlockSpec((TL, 128), lambda i: (i, AB_COL // 128))] + [full(a) for a in params],
        out_specs=[o, o, o, pl.BlockSpec((TL, 128), lambda i: (i, 0))],
        out_shape=[SDS((L, BR), F32)] * 3 + [SDS((L, 128), F32)],
        compiler_params=_cparams(("parallel",)),
    )(proj, proj, proj, *params)


def _dn_pre_bwd_call(proj, p, dq, dk, dv, dgb):
    L = proj.shape[0]
    nt = L // TL

    def body(m_ref, h_ref, ab_ref, cw, alog, dtb, dq_r, dk_r, dv_r, dgb_r,
             dqkv_ref, dab_ref, g_cw, g_alog, g_dtb, carry):
        i = pl.program_id(0)

        @pl.when(i == 0)
        def _():
            carry[...] = jnp.zeros_like(carry)
            for r in (g_cw, g_alog, g_dtb):
                r[...] = jnp.zeros_like(r)

        nf = (i < nt - 1).astype(F32)
        qkvw = jnp.concatenate([h_ref[...] * nf, m_ref[...]], axis=0)
        conv = _make_conv(TL, HALO_S, KD)
        _, vjp = jax.vjp(lambda a, b, c, d, e: _dn_pre(a, b, c, d, e, conv, TL),
                         qkvw, ab_ref[...], cw[...], alog[...], dtb[...])
        dwin, dab, dcw, dalog, ddtb = vjp((dq_r[...], dk_r[...], dv_r[...], dgb_r[...]))
        tail = jnp.concatenate([jnp.zeros((TL - HALO_S, 3 * BR), F32), carry[...]], axis=0)
        carry[...] = dwin[:HALO_S, :]
        dqkv_ref[...] = (dwin[HALO_S:, :] + tail).astype(BF16)
        dab_ref[...] = dab.astype(BF16)
        g_cw[...] += dcw
        g_alog[...] += dalog
        g_dtb[...] += ddtb

    full = lambda a: pl.BlockSpec(a.shape, lambda i: (0,) * a.ndim)
    params = (p["d_conv_w"], p["d_a_log"], p["d_dt_bias"])
    rev = lambda w: pl.BlockSpec((TL, w), lambda i: (nt - 1 - i, 0))
    return pl.pallas_call(
        body, name="dn_pre_bwd", grid=(nt,),
        in_specs=[pl.BlockSpec((TL, 3 * BR), lambda i: (nt - 1 - i, 3)), _halo_spec_rev(nt, TL, HALO_S, 3 * BR, 3),
                  pl.BlockSpec((TL, 128), lambda i: (nt - 1 - i, AB_COL // 128))] + [full(a) for a in params]
        + [rev(BR), rev(BR), rev(BR), rev(128)],
        out_specs=[rev(3 * BR), rev(128)] + [full(a) for a in params],
        out_shape=[SDS((L, 3 * BR), BF16), SDS((L, 128), BF16)] + [SDS(a.shape, F32) for a in params],
        scratch_shapes=[pltpu.VMEM((HALO_S, 3 * BR), F32)],
        compiler_params=_cparams(("arbitrary",)),
    )(proj, proj, proj, *params, dq, dk, dv, dgb)


@jax.custom_vjp
def _inverse_given(lmat, t):
    return t


def _inverse_given_fwd(lmat, t):
    return t, t


def _inverse_given_bwd(t, dt):
    return -mm_nt(mm_tn(t, dt), t), jnp.zeros_like(t)


_inverse_given.defvjp(_inverse_given_fwd, _inverse_given_bwd)


def _dn_group(q, k, v, gb, z, ng, *s, inverses=None, keep=None):
    ri, ci = _iota2((CH, CH), 0), _iota2((CH, CH), 1)
    causal, strict = ri >= ci, ri > ci
    eye = (ri == ci).astype(F32)
    s = list(s)
    pairs = []
    for c in range(DN_GROUP):
        rows = slice(c * CH, (c + 1) * CH)
        gbc = gb[rows, :]
        for h, (qh, kh, vh, zh) in enumerate(zip(_heads(q[rows, :]), _heads(k[rows, :]), _heads(v[rows, :]),
                                                 _heads(z[rows, :]))):
            gc = jnp.broadcast_to(gbc[:, h:h + 1], (CH, HD))
            beta = gbc[:, NH + h:NH + h + 1]
            decay = jnp.where(causal, jnp.exp(jnp.where(causal, gc - gc.T, 0.0)), 0.0)
            egc = jnp.exp(gc)
            glast = gc[CH - 1:CH, :]
            kb = kh * beta
            pairs.append(dict(q=qh, k=kh, z=zh, decay=decay, qe=qh * egc, kd=kh * jnp.exp(glast - gc),
                              sdec=jnp.exp(glast[:, 0:1]), kb=kb, rhs=jnp.concatenate([vh * beta, kb * egc], axis=1)))
    for p in pairs:
        p["pw"] = jnp.where(strict, mm_nt(p["kb"], p["k"]) * p["decay"], 0.0)
    if inverses is not None:
        for p, t in zip(pairs, inverses):
            p["t"] = _inverse_given(p["pw"], t)
    else:
        for p in pairs:
            p["t"] = eye - p["pw"]
        for _ in range(5):
            for p in pairs:
                p["pw"] = mm(p["pw"], p["pw"])
            for p in pairs:
                p["t"] = mm(p["t"], eye + p["pw"])
        if keep is not None:
            keep.extend(p["t"] for p in pairs)
    for p in pairs:
        p["uw"] = mm(p["t"], p["rhs"])
    for p in pairs:
        p["attn"] = mm_nt(p["q"], p["k"]) * p["decay"]
    out_rows = []
    for c in range(DN_GROUP):
        grp = pairs[c * NH:(c + 1) * NH]
        ws = [mm(jnp.concatenate([p["uw"][:, HD:], p["qe"]], axis=0), s[h]) for h, p in enumerate(grp)]
        v_new = [p["uw"][:, :HD] - w_[:CH, :] for p, w_ in zip(grp, ws)]
        o = [w_[CH:, :] + mm(p["attn"], vn) for p, w_, vn in zip(grp, ws, v_new)]
        s = [s[h] * p["sdec"] + mm_tn(p["kd"], vn) for h, (p, vn) in enumerate(zip(grp, v_new))]
        o = [oh * lax.rsqrt(jnp.mean(oh * oh, axis=-1, keepdims=True) + EPS) * ng * _silu(p["z"]) for oh, p in zip(o, grp)]
        out_rows.append(jnp.concatenate(o, axis=1))
    return (jnp.concatenate(out_rows, axis=0), *s)


def _dn_core_fwd_call(proj, q, k, v, gb, ng):
    L = q.shape[0]
    rows = DN_GROUP * CH
    ng_ = L // rows

    def body(q_r, k_r, v_r, gb_r, z_r, ng_r, yd_ref, ssave_ref, tsave_ref, s_scr):
        @pl.when(pl.program_id(0) == 0)
        def _():
            s_scr[...] = jnp.zeros_like(s_scr)

        ssave_ref[0] = s_scr[...]
        inverses = []
        yd, *s2 = _dn_group(q_r[...], k_r[...], v_r[...], gb_r[...], z_r[...], ng_r[...],
                            *[s_scr[h] for h in range(NH)], keep=inverses)
        yd_ref[...] = yd
        for h in range(NH):
            s_scr[h] = s2[h]
        for i, t in enumerate(inverses):
            tsave_ref[0, i] = t

    c = pl.BlockSpec((rows, BR), lambda i: (i, 0))
    return pl.pallas_call(
        body, name="dn_core_fwd", grid=(ng_,),
        in_specs=[c, c, c, pl.BlockSpec((rows, 128), lambda i: (i, 0)), pl.BlockSpec((rows, BR), lambda i: (i, 12)),
                  pl.BlockSpec((1, HD), lambda i: (0, 0))],
        out_specs=[c, pl.BlockSpec((1, NH, HD, HD), lambda i: (i, 0, 0, 0)),
                   pl.BlockSpec((1, DN_GROUP * NH, CH, CH), lambda i: (i, 0, 0, 0))],
        out_shape=[SDS((L, BR), F32), SDS((ng_, NH, HD, HD), F32), SDS((ng_, DN_GROUP * NH, CH, CH), F32)],
        scratch_shapes=[pltpu.VMEM((NH, HD, HD), F32)],
        compiler_params=_cparams(("arbitrary",)),
    )(q, k, v, gb, proj, ng)


def _dn_core_bwd_call(proj, q, k, v, gb, ng, ssave, tsave, dyd):
    L = q.shape[0]
    rows = DN_GROUP * CH
    ng_ = L // rows

    def body(q_r, k_r, v_r, gb_r, z_r, ng_r, s_r, t_r, dy_r, dq_ref, dk_ref, dv_ref, dgb_ref, dz_ref, g_ng, ds_scr):
        @pl.when(pl.program_id(0) == 0)
        def _():
            ds_scr[...] = jnp.zeros_like(ds_scr)
            g_ng[...] = jnp.zeros_like(g_ng)

        inverses = [t_r[0, i] for i in range(DN_GROUP * NH)]
        _, vjp = jax.vjp(functools.partial(_dn_group, inverses=inverses),
                         q_r[...], k_r[...], v_r[...], gb_r[...], z_r[...], ng_r[...],
                         *[s_r[0, h] for h in range(NH)])
        dq, dk, dv, dgb, dz, dng, *ds = vjp((dy_r[...], *[ds_scr[h] for h in range(NH)]))
        dq_ref[...], dk_ref[...], dv_ref[...], dgb_ref[...] = dq, dk, dv, dgb
        dz_ref[...] = dz.astype(BF16)
        g_ng[...] += dng
        for h in range(NH):
            ds_scr[h] = ds[h]

    c = pl.BlockSpec((rows, BR), lambda i: (ng_ - 1 - i, 0))
    c128 = pl.BlockSpec((rows, 128), lambda i: (ng_ - 1 - i, 0))
    return pl.pallas_call(
        body, name="dn_core_bwd", grid=(ng_,),
        in_specs=[c, c, c, c128, pl.BlockSpec((rows, BR), lambda i: (ng_ - 1 - i, 12)),
                  pl.BlockSpec((1, HD), lambda i: (0, 0)),
                  pl.BlockSpec((1, NH, HD, HD), lambda i: (ng_ - 1 - i, 0, 0, 0)),
                  pl.BlockSpec((1, DN_GROUP * NH, CH, CH), lambda i: (ng_ - 1 - i, 0, 0, 0)), c],
        out_specs=[c, c, c, c128, c, pl.BlockSpec((1, HD), lambda i: (0, 0))],
        out_shape=[SDS((L, BR), F32)] * 3 + [SDS((L, 128), F32), SDS((L, BR), BF16), SDS((1, HD), F32)],
        scratch_shapes=[pltpu.VMEM((NH, HD, HD), F32)],
        compiler_params=_cparams(("arbitrary",)),
    )(q, k, v, gb, proj, ng, ssave, tsave, dyd)


def _outproj_bwd_call(dx, ys, w, deps=()):
    L = dx.shape[0]

    def body(dx_ref, a_ref, b_ref, c_ref, d_ref, w_ref, *rest):
        da, db, dc, dd, dw_ref = rest[len(deps):]

        @pl.when(pl.program_id(0) == 0)
        def _():
            dw_ref[...] = jnp.zeros_like(dw_ref)

        dxb = dx_ref[...].astype(BF16)
        for b, (y_ref, o_ref) in enumerate(zip((a_ref, b_ref, c_ref, d_ref), (da, db, dc, dd))):
            o_ref[...] = _dot(dxb, w_ref[b * BR:(b + 1) * BR, :], NT)
            dw_ref[b * BR:(b + 1) * BR, :] += _dot(y_ref[...].astype(BF16), dxb, TN)

    yspec = pl.BlockSpec((TL, BR), lambda i: (i, 0))
    return pl.pallas_call(
        body, name="outproj_bwd", grid=(L // TL,),
        in_specs=[pl.BlockSpec((TL, D), lambda i: (i, 0)), yspec, yspec, yspec, yspec,
                  pl.BlockSpec((D, D), lambda i: (0, 0))] + _dep_specs(deps),
        out_specs=[yspec] * 4 + [pl.BlockSpec((D, D), lambda i: (0, 0))],
        out_shape=[SDS((L, BR), F32)] * 4 + [SDS((D, D), F32)],
        compiler_params=_cparams(("arbitrary",)),
    )(dx, *ys, w, *deps)


def _slab_cols(slabs):
    widths = [s.shape[1] for s in slabs]
    starts = [sum(widths[:i]) for i in range(len(widths))]
    assert starts[-1] + widths[-1] == PW
    return list(zip(starts, widths))


def _inproj_bwd_x_call(slabs, w, x, g, dx_next, deps=()):
    L = x.shape[0]
    cols = _slab_cols(slabs)
    n = len(slabs)

    def body(*refs):
        dp_refs, (w_ref, x_ref, g_ref, dxn_ref) = refs[:n], refs[n:n + 4]
        dx_ref, dg_ref = refs[n + 4 + len(deps):]

        @pl.when(pl.program_id(0) == 0)
        def _():
            dg_ref[...] = jnp.zeros_like(dg_ref)

        dh = None
        for dp_ref, (c0, cw) in zip(dp_refs, cols):
            part = _dot(dp_ref[...], w_ref[:, c0:c0 + cw], NT)
            dh = part if dh is None else dh + part
        _, vjp = jax.vjp(_rms, x_ref[...], g_ref[...])
        dx, dg = vjp(dh)
        dx_ref[...] = dx + dxn_ref[...]
        dg_ref[...] += dg

    row = lambda w_: pl.BlockSpec((TL, w_), lambda i: (i, 0))
    return pl.pallas_call(
        body, name="inproj_bwd_x", grid=(L // TL,),
        in_specs=[row(cw) for _, cw in cols]
        + [pl.BlockSpec((D, PW), lambda i: (0, 0)), row(D), pl.BlockSpec((1, D), lambda i: (0, 0)), row(D)]
        + _dep_specs(deps),
        out_specs=[row(D), pl.BlockSpec((1, D), lambda i: (0, 0))],
        out_shape=[SDS((L, D), F32), SDS((1, D), F32)],
        compiler_params=_cparams(("arbitrary",)),
    )(*slabs, w, x, g, dx_next, *deps)


def _inproj_bwd_w_call(h, slabs):
    L = h.shape[0]
    cols = _slab_cols(slabs)
    n = len(slabs)

    def body(*refs):
        h_ref, dp_refs, dw_ref = refs[0], refs[1:1 + n], refs[1 + n]

        @pl.when(pl.program_id(0) == 0)
        def _():
            dw_ref[...] = jnp.zeros_like(dw_ref)

        hv = h_ref[...]
        for dp_ref, (c0, cw) in zip(dp_refs, cols):
            dw_ref[:, c0:c0 + cw] += _dot(hv, dp_ref[...], TN)

    row = lambda w_: pl.BlockSpec((TL, w_), lambda i: (i, 0))
    return pl.pallas_call(
        body, name="inproj_bwd_w", grid=(L // TL,),
        in_specs=[row(D)] + [row(cw) for _, cw in cols],
        out_specs=pl.BlockSpec((D, PW), lambda i: (0, 0)),
        out_shape=SDS((D, PW), F32),
        compiler_params=_cparams(("arbitrary",)),
    )(h, *slabs)


def _exchange_call(name, flows):
    n = len(flows)

    def body(*refs):
        srcs, dsts = refs[:n], refs[n:2 * n]
        send_sems, recv_sems, local_sems = refs[2 * n:]
        x, y, c = lax.axis_index("x"), lax.axis_index("y"), lax.axis_index("c")
        me = 4 * x + 2 * y + c
        copies = []
        for mask in range(1, N_DEV):
            px = 1 - x if mask & 4 else x
            py = 1 - y if mask & 2 else y
            pc = 1 - c if mask & 1 else c
            for f, (_, src_at, _, dst_at) in enumerate(flows):
                cp = pltpu.make_async_remote_copy(
                    src_ref=src_at(srcs[f], 4 * px + 2 * py + pc), dst_ref=dst_at(dsts[f], me),
                    send_sem=send_sems.at[mask - 1, f], recv_sem=recv_sems.at[mask - 1, f],
                    device_id=(px, py, pc), device_id_type=pl.DeviceIdType.MESH)
                cp.start()
                copies.append(cp)
        mine = [pltpu.make_async_copy(src_at(srcs[f], me), dst_at(dsts[f], me), local_sems.at[f])
                for f, (_, src_at, _, dst_at) in enumerate(flows)]
        for cp in mine:
            cp.start()
        for cp in copies + mine:
            cp.wait()

    return pl.pallas_call(
        body, name=name,
        in_specs=[pl.BlockSpec(memory_space=pl.ANY)] * n,
        out_specs=[pl.BlockSpec(memory_space=pl.ANY)] * n,
        out_shape=[SDS(tuple(shape), src.dtype) for src, _, shape, _ in flows],
        scratch_shapes=[pltpu.SemaphoreType.DMA((N_DEV - 1, n)), pltpu.SemaphoreType.DMA((N_DEV - 1, n)),
                        pltpu.SemaphoreType.DMA((n,))],
    )(*[f[0] for f in flows])


def _whole(ref, _):
    return ref


def _slot(ref, k):
    return ref.at[k]


_HBM_SPEC = pl.BlockSpec(memory_space=pltpu.HBM)
_SEM_SPEC = pl.BlockSpec(memory_space=pltpu.SEMAPHORE)
_DATAFLOW = pltpu.SideEffectType.DATAFLOW_SIDE_EFFECTING


def _split_copies(views, src_refs, land_refs, send_sems, recv_sems):
    x, y, c = lax.axis_index("x"), lax.axis_index("y"), lax.axis_index("c")
    me = 4 * x + 2 * y + c
    copies = []
    for mask in range(1, N_DEV):
        px = 1 - x if mask & 4 else x
        py = 1 - y if mask & 2 else y
        pc = 1 - c if mask & 1 else c
        for f, (src_at, dst_at) in enumerate(views):
            pair = (mask - 1) * len(views) + f
            copies.append(pltpu.make_async_remote_copy(
                src_ref=src_at(src_refs[f], 4 * px + 2 * py + pc), dst_ref=dst_at(land_refs[f], me),
                send_sem=send_sems.at[pair], recv_sem=recv_sems.at[pair],
                device_id=(px, py, pc), device_id_type=pl.DeviceIdType.MESH))
    return copies


def _split_start_call(name, srcs, lands, views):
    n = len(srcs)

    def body(*refs):
        src_refs, land_refs = refs[:n], refs[n:2 * n]
        send_sems, recv_sems, token = refs[2 * n], refs[2 * n + 1], refs[-1]
        for cp in _split_copies(views, src_refs, land_refs, send_sems, recv_sems):
            cp.start()
        token[...] = jnp.zeros_like(token)

    arrays = list(srcs) + list(lands)
    outs = pl.pallas_call(
        body, name=name,
        out_shape=(pltpu.SemaphoreType.DMA(((N_DEV - 1) * n,)), pltpu.SemaphoreType.DMA(((N_DEV - 1) * n,)),
                   *[pltpu.HBM(a.shape, a.dtype) for a in arrays], SDS((8, 128), F32)),
        in_specs=[_HBM_SPEC] * (2 * n),
        out_specs=(_SEM_SPEC, _SEM_SPEC, *[_HBM_SPEC] * (2 * n), pl.BlockSpec(memory_space=pltpu.VMEM)),
        input_output_aliases={i: 2 + i for i in range(2 * n)},
        compiler_params=pltpu.CompilerParams(has_side_effects=_DATAFLOW),
    )(*[pltpu.with_memory_space_constraint(a, pltpu.HBM) for a in arrays])
    return outs[0], outs[1], list(outs[2:2 + 2 * n]), outs[-1]


def _split_wait_call(name, send_sems, recv_sems, thru, views, after):
    n = len(views)

    def body(*refs):
        src_refs, land_refs = refs[:n], refs[n:2 * n]
        send, recv = refs[2 * n], refs[2 * n + 1]
        for cp in _split_copies(views, src_refs, land_refs, send, recv):
            cp.wait_send()
            cp.wait_recv()

    outs = pl.pallas_call(
        body, name=name,
        out_shape=tuple(pltpu.HBM(a.shape, a.dtype) for a in thru),
        in_specs=[_HBM_SPEC] * (2 * n) + [_SEM_SPEC, _SEM_SPEC, pl.BlockSpec(memory_space=pl.ANY)],
        out_specs=tuple([_HBM_SPEC] * (2 * n)),
        input_output_aliases={i: i for i in range(2 * n)},
        compiler_params=pltpu.CompilerParams(has_side_effects=_DATAFLOW),
    )(*thru, send_sems, recv_sems, after)
    return list(outs[n:])


def _own_slot(block, me):
    zone = jnp.zeros((N_DEV,) + block.shape, block.dtype)
    return lax.dynamic_update_slice(zone, block[None], (me,) + (0,) * block.ndim)


def _reduce_adamw_call(parts, w, m, v, block, name):
    nsrc = parts.shape[0]
    grid = tuple(s // b for s, b in zip(w.shape, block))
    c1 = 1.0 - ADAM_B1 ** ADAM_STEP
    c2 = 1.0 - ADAM_B2 ** ADAM_STEP

    def body(p_ref, w_ref, m_ref, v_ref, g_ref, d_ref, nm_ref, nv_ref):
        g = p_ref[0].astype(F32)
        for k in range(1, nsrc):
            g = g + p_ref[k].astype(F32)
        nm = ADAM_B1 * m_ref[...] + (1.0 - ADAM_B1) * g
        nv = ADAM_B2 * v_ref[...] + (1.0 - ADAM_B2) * (g * g)
        g_ref[...] = g
        nm_ref[...] = nm
        nv_ref[...] = nv
        d_ref[...] = -ADAM_LR * ((nm / c1) / (jnp.sqrt(nv / c2) + ADAM_EPS) + ADAM_WD * w_ref[...])

    own = pl.BlockSpec(tuple(block), lambda *i: i)
    return pl.pallas_call(
        body, name=name, grid=grid,
        in_specs=[pl.BlockSpec((nsrc,) + tuple(block), lambda *i: (0,) + i), own, own, own],
        out_specs=[own] * 4,
        out_shape=[SDS(w.shape, F32)] * 4,
        compiler_params=_cparams(("parallel",) * len(grid)),
    )(parts, w, m, v)


RELAYOUT_ROWS = 256
SHARD_COLS = IN_COLS // N_DEV


def _win_gather_layout_call(shards):
    def body(w_ref, o_ref):
        nat = jnp.concatenate([w_ref[k].astype(F32) for k in range(N_DEV)], axis=1)
        out = jnp.concatenate([nat[:, :3072], nat[:, 3080:], nat[:, 3072:3080],
                               jnp.zeros((RELAYOUT_ROWS, PW - IN_COLS), F32)], axis=1)
        o_ref[...] = out.astype(BF16)

    return pl.pallas_call(
        body, name="w_in_layout", grid=(D // RELAYOUT_ROWS,),
        in_specs=[pl.BlockSpec((N_DEV, RELAYOUT_ROWS, SHARD_COLS), lambda i: (0, i, 0))],
        out_specs=pl.BlockSpec((RELAYOUT_ROWS, PW), lambda i: (i, 0)),
        out_shape=SDS((D, PW), BF16),
        compiler_params=_cparams(("parallel",)),
    )(shards)


def _win_scatter_layout_call(grad):
    def body(g_ref, o_ref):
        g = g_ref[...]
        nat = jnp.concatenate([g[:, :3072], g[:, AB_COL:AB_COL + 8], g[:, 3072:AB_COL]], axis=1)
        for k in range(N_DEV):
            o_ref[k] = nat[:, SHARD_COLS * k:SHARD_COLS * (k + 1)].astype(BF16)

    return pl.pallas_call(
        body, name="w_in_grad_layout", grid=(D // RELAYOUT_ROWS,),
        in_specs=[pl.BlockSpec((RELAYOUT_ROWS, PW), lambda i: (i, 0))],
        out_specs=pl.BlockSpec((N_DEV, RELAYOUT_ROWS, SHARD_COLS), lambda i: (0, i, 0)),
        out_shape=SDS((N_DEV, D, SHARD_COLS), BF16),
        compiler_params=_cparams(("parallel",)),
    )(grad)


def _reduce_adamw_layers_call(parts, w, m, v, rows, name):
    _, R, C = w.shape
    c1 = 1.0 - ADAM_B1 ** ADAM_STEP
    c2 = 1.0 - ADAM_B2 ** ADAM_STEP

    def body(*refs):
        p_refs = refs[:DEPTH]
        w_ref, m_ref, v_ref, g_ref, d_ref, nm_ref, nv_ref = refs[DEPTH:]
        for l in range(DEPTH):
            @pl.when(pl.program_id(0) == l)
            def _(l=l):
                g = p_refs[l][0].astype(F32)
                for k in range(1, N_DEV):
                    g = g + p_refs[l][k].astype(F32)
                nm = ADAM_B1 * m_ref[0] + (1.0 - ADAM_B1) * g
                nv = ADAM_B2 * v_ref[0] + (1.0 - ADAM_B2) * (g * g)
                g_ref[0] = g
                nm_ref[0] = nm
                nv_ref[0] = nv
                d_ref[0] = -ADAM_LR * ((nm / c1) / (jnp.sqrt(nv / c2) + ADAM_EPS) + ADAM_WD * w_ref[0])

    def part_spec(l):
        return pl.BlockSpec((N_DEV, rows, C), lambda j, i: (0, jnp.where(j == l, i, 0), 0))

    own = pl.BlockSpec((1, rows, C), lambda j, i: (j, i, 0))
    return pl.pallas_call(
        body, name=name, grid=(DEPTH, R // rows),
        in_specs=[part_spec(l) for l in range(DEPTH)] + [own, own, own],
        out_specs=[own] * 4,
        out_shape=[SDS(w.shape, F32)] * 4,
        compiler_params=_cparams(("arbitrary", "arbitrary")),
    )(*parts, w, m, v)


_BIG = ("w_in", "w_out", "a_pw_w", "s5_glu_w")
_CONV = ("a_conv_w", "c_conv_w", "d_conv_w")
_CONV_TAPS = {"a_conv_w": KA, "c_conv_w": KC, "d_conv_w": KD}
_CONV_ROWS = {"a_conv_w": HALO, "c_conv_w": HALO_S, "d_conv_w": HALO_S}
_CONV_WIDTH = {"a_conv_w": BR, "c_conv_w": BR, "d_conv_w": 3 * BR}
_REPLICATED = ("norm_g", "a_conv_b", "a_ln_g", "a_ln_b", "a_pw_b", "s5_lambda_re", "s5_lambda_im", "s5_b_re", "s5_b_im",
               "s5_c_re", "s5_c_im", "s5_d", "s5_log_dt", "s5_glu_b", "d_a_log", "d_dt_bias", "d_norm_g", "final_g")
_WEIGHTS = ("norm_g", "w_in", "a_conv_w", "a_conv_b", "a_ln_g", "a_ln_b", "a_pw_w", "a_pw_b", "s5_lambda_re",
            "s5_lambda_im", "s5_b_re", "s5_b_im", "s5_c_re", "s5_c_im", "s5_d", "s5_log_dt", "s5_glu_w", "s5_glu_b",
            "c_conv_w", "d_conv_w", "d_a_log", "d_dt_bias", "d_norm_g", "w_out", "final_g")


def _size(shape):
    n = 1
    for s in shape:
        n *= s
    return n


PACK_ALIGN = 1024


def _piece_rows(n):
    return -(-n // PACK_ALIGN) * (PACK_ALIGN // 128)


def _pack_rows(pieces, row_mult):
    rows = []
    for p in pieces:
        flat = p.reshape(-1)
        rows.append(jnp.pad(flat, (0, _piece_rows(flat.shape[0]) * 128 - flat.shape[0])).reshape(-1, 128))
    out = jnp.concatenate(rows, axis=0)
    return jnp.pad(out, ((0, (-out.shape[0]) % row_mult), (0, 0)))


def _unpack(packed, shapes):
    out, row = [], 0
    for s in shapes:
        n = _size(s)
        nr = _piece_rows(n)
        out.append(packed[row:row + nr].reshape(-1)[:n].reshape(s))
        row += nr
    return out


_GATHER_VIEWS = [(_whole, _slot)] * 5


def _gather_start(shards, layer, me):
    srcs = [shards[n].astype(BF16) for n in _BIG]
    srcs.append(_pack_rows([shards[n] for n in _CONV], 8))
    lands = [_own_slot(s, me) for s in srcs]
    return _split_start_call("gather_start_%d" % layer, srcs, lands, _GATHER_VIEWS)


def _gather_finish(weights, layer, started, after):
    send, recv, thru, _ = started
    w_in, w_out, a_pw, glu, conv_all = _split_wait_call("gather_wait_%d" % layer, send, recv, thru, _GATHER_VIEWS, after)
    full = {"w_in": _win_gather_layout_call(w_in), "w_out": w_out.reshape(D, D), "a_pw_w": a_pw.reshape(BR, BR),
            "s5_glu_w": glu.reshape(BR, BR)}
    shapes = [weights[n].shape[1:] for n in _CONV]
    per_dev = [_unpack(conv_all[k], shapes) for k in range(N_DEV)]
    for i, n in enumerate(_CONV):
        whole = jnp.concatenate([per_dev[k][i] for k in range(N_DEV)], axis=-1)
        full[n] = jnp.pad(whole, ((0, _CONV_ROWS[n] - _CONV_TAPS[n]), (0, 0)))
    return full


def _rows_view(rows):
    return lambda ref, k: ref.at[pl.ds(k * rows, rows), :]


_SCATTER_VIEWS = [(_slot, _slot), (_rows_view(D // N_DEV), _slot), (_rows_view(BR // N_DEV), _slot),
                  (_rows_view(BR // N_DEV), _slot)]


def _scatter_start(grads, layer, me):
    srcs = [_win_scatter_layout_call(grads["w_in"])] + [grads[n].astype(BF16) for n in _BIG[1:]]
    own = [lax.dynamic_index_in_dim(srcs[0], me, 0, keepdims=False)]
    for s, rows in zip(srcs[1:], (D // N_DEV, BR // N_DEV, BR // N_DEV)):
        own.append(lax.dynamic_slice_in_dim(s, me * rows, rows, axis=0))
    lands = [_own_slot(o, me) for o in own]
    return _split_start_call("scatter_start_%d" % layer, srcs, lands, _SCATTER_VIEWS)


_S5_KERNEL_SHAPES = {"s5_lambda_re": (1, NSTATE), "s5_lambda_im": (1, NSTATE), "s5_log_dt": (1, 16),
                     "s5_b_re": (NSTATE, 16), "s5_b_im": (NSTATE, 16), "s5_c_re": (BR, 64), "s5_c_im": (BR, 64)}
_S5_KEYS = {"s5_lambda_re": "lam_re", "s5_lambda_im": "lam_im", "s5_log_dt": "log_dt", "s5_b_re": "b_re",
            "s5_b_im": "b_im", "s5_c_re": "c_re", "s5_c_im": "c_im"}


def _s5_inputs_all(weights):
    return {n: weights[n].reshape((DEPTH,) + s) for n, s in _S5_KERNEL_SHAPES.items()}


def _s5_inputs(p):
    return {_S5_KEYS[n]: p["s5_in"][n] for n in _S5_KERNEL_SHAPES}


def _row(a, width=None):
    a = a.reshape(1, -1)
    return a if width is None else jnp.pad(a, ((0, 0), (0, width - a.shape[1])))


def _layer_params(p):
    q = dict(p)
    for n in ("norm_g", "a_conv_b", "a_ln_g", "a_ln_b", "a_pw_b", "s5_d", "s5_glu_b", "d_norm_g"):
        q[n] = _row(p[n])
    q["d_a_log"] = _row(p["d_a_log"], 128)
    q["d_dt_bias"] = _row(p["d_dt_bias"], 128)
    return q


def _layer_fwd(x, p, deps=()):
    q = _layer_params(p)
    proj, h = _inproj_call(x, q["norm_g"], q["w_in"], deps)
    ya, yc = _ac_fwd_call(proj, q)
    prep = p["s5_prep"]
    yb, cin_r, cin_i = _s5_fwd_call(proj, prep, q)
    dq, dk, dv, dgb = _dn_pre_fwd_call(proj, q)
    yd, ssave, tsave = _dn_core_fwd_call(proj, dq, dk, dv, dgb, q["d_norm_g"])
    x_next = _outproj_call(x, (ya, yb, yc, yd), q["w_out"])
    saved = dict(x=x, proj=proj, h=h, ya=ya, yb=yb, yc=yc, yd=yd, cin_r=cin_r, cin_i=cin_i,
                 q=dq, k=dk, v=dv, gb=dgb, ssave=ssave, tsave=tsave, prep=prep)
    return x_next, saved


def _layer_bwd(dx, p, sv, deps=(), on_weight_grads=None):
    q = _layer_params(p)
    proj = sv["proj"]
    dya, dyb, dyc, dyd, g_wout = _outproj_bwd_call(dx, (sv["ya"], sv["yb"], sv["yc"], sv["yd"]), q["w_out"], deps)
    dpa, dpc, g_acw, g_acb, g_alg, g_alb, g_apw, g_apb, g_ccw = _ac_bwd_call(proj, q, dya, dyc)
    dpb, *s5g = _s5_bwd_call(proj, sv["prep"], q, sv["cin_r"], sv["cin_i"], dyb)
    g_sd, g_gw, g_gb = s5g[6:]
    g_lre, g_lim, g_ldt, g_bre, g_bim, g_cre, g_cim = _s5_prep_bwd_call(_s5_inputs(p), s5g[:6])
    dq, dk, dv, dgb, dz, g_ng = _dn_core_bwd_call(proj, sv["q"], sv["k"], sv["v"], sv["gb"], q["d_norm_g"], sv["ssave"],
                                                  sv["tsave"], dyd)
    dqkv, dab, g_dcw, g_alog, g_dtb = _dn_pre_bwd_call(proj, q, dq, dk, dv, dgb)
    slabs = (dpa, dpb, dpc, dqkv, dz, dab)
    g_win = _inproj_bwd_w_call(sv["h"], slabs)
    big = {"w_in": g_win, "w_out": g_wout, "a_pw_w": g_apw, "s5_glu_w": g_gw}
    tokens = on_weight_grads(big) if on_weight_grads is not None else ()
    dx_prev, g_ng0 = _inproj_bwd_x_call(slabs, q["w_in"], sv["x"], q["norm_g"], dx, tokens)
    grads = {"norm_g": g_ng0, "w_in": g_win, "a_conv_w": g_acw, "a_conv_b": g_acb, "a_ln_g": g_alg, "a_ln_b": g_alb,
             "a_pw_w": g_apw, "a_pw_b": g_apb, "s5_lambda_re": g_lre, "s5_lambda_im": g_lim, "s5_b_re": g_bre,
             "s5_b_im": g_bim, "s5_c_re": g_cre, "s5_c_im": g_cim, "s5_d": g_sd, "s5_log_dt": g_ldt, "s5_glu_w": g_gw,
             "s5_glu_b": g_gb, "c_conv_w": g_ccw, "d_conv_w": g_dcw, "d_a_log": g_alog[:, :NH], "d_dt_bias": g_dtb[:, :NH],
             "d_norm_g": g_ng, "w_out": g_wout}
    return dx_prev, grads


def _step(x, target, weights, moments_m, moments_v):
    me = 4 * lax.axis_index("x") + 2 * lax.axis_index("y") + lax.axis_index("c")
    layer_names = [n for n in _WEIGHTS if n != "final_g"]
    s5_all = _s5_inputs_all(weights)

    sharded = _BIG + _CONV
    gather = _gather_start({n: weights[n][0] for n in sharded}, 0, me)
    preps = [_s5_prep_call({_S5_KEYS[n]: a[l] for n, a in s5_all.items()}, [gather[3]]) for l in range(DEPTH)]
    x_out, after, layers, saved = x, preps[-1][0], [], []
    for l in range(DEPTH):
        full = _gather_finish(weights, l, gather, after)
        deps = ()
        if l + 1 < DEPTH:
            nxt, full["w_out"] = lax.optimization_barrier(({n: weights[n][l + 1] for n in sharded}, full["w_out"]))
            gather = _gather_start(nxt, l + 1, me)
            deps = [gather[3]]
        p = {n: (full[n] if n in full else weights[n][l]) for n in layer_names}
        p["s5_in"] = {n: a[l] for n, a in s5_all.items()}
        p["s5_prep"] = preps[l]
        layers.append(p)
        x_out, sv = _layer_fwd(x_out, p, deps)
        after = x_out
        saved.append(sv)
    dx0, g_final, loss_part = _loss_call(x_out, _row(weights["final_g"]), target)

    per_layer, scatters = [None] * DEPTH, [None] * DEPTH
    for l in range(DEPTH - 1, -1, -1):
        def start(big, l=l):
            scatters[l] = _scatter_start(big, l, me)
            return [scatters[l][3]]
        dx0, per_layer[l] = _layer_bwd(dx0, layers[l], saved[l], [scatters[l + 1][3]] if l + 1 < DEPTH else (), start)
    stack = lambda n: jnp.stack([g[n] for g in per_layer])
    loss = lax.psum(loss_part[0, 0], ("x", "y", "c"))
    results = {}

    parts = [_split_wait_call("scatter_wait_%d" % l, scatters[l][0], scatters[l][1], scatters[l][2], _SCATTER_VIEWS, dx0)
             for l in range(DEPTH - 1, -1, -1)][::-1]
    rows = {"w_in": RELAYOUT_ROWS, "w_out": D // N_DEV, "a_pw_w": BR // N_DEV, "s5_glu_w": BR // N_DEV}
    for i, n in enumerate(_BIG):
        results[n] = _reduce_adamw_layers_call([parts[l][i] for l in range(DEPTH)], weights[n], moments_m[n],
                                               moments_v[n], rows[n], "adamw_" + n)

    grads = {n: stack(n).reshape(weights[n].shape) for n in _REPLICATED if n != "final_g"}
    grads["final_g"] = g_final.reshape(D)
    conv_g = [stack(n) for n in _CONV]
    pack = lambda d: _pack_rows([d[n] for n in _REPLICATED] + [jnp.zeros_like(c) for c in conv_g], 512)
    packed_g = _pack_rows([grads[n] for n in _REPLICATED] + conv_g, 512)
    gathered, = _exchange_call("gather_small_grads", [(packed_g, _whole, (N_DEV,) + packed_g.shape, _slot)])
    res = _reduce_adamw_call(gathered, pack(weights), pack(moments_m), pack(moments_v), (512, 128), "adamw_replicated")
    shapes = [weights[n].shape for n in _REPLICATED] + [c.shape for c in conv_g]
    res = [_unpack(r, shapes) for r in res]
    for i, n in enumerate(_REPLICATED):
        results[n] = tuple(r[i] for r in res)

    own_g = []
    for i, n in enumerate(_CONV):
        width = _CONV_WIDTH[n] // N_DEV
        summed = res[0][len(_REPLICATED) + i][:, :_CONV_TAPS[n], :]
        own_g.append(lax.dynamic_slice_in_dim(summed, me * width, width, axis=2))
    packc = lambda arrs: _pack_rows(arrs, 8)
    res = _reduce_adamw_call(packc(own_g)[None], packc([weights[n] for n in _CONV]), packc([moments_m[n] for n in _CONV]),
                             packc([moments_v[n] for n in _CONV]), packc(own_g).shape, "adamw_conv")
    res = [_unpack(r, [weights[n].shape for n in _CONV]) for r in res]
    for i, n in enumerate(_CONV):
        results[n] = tuple(r[i] for r in res)

    outs = [loss, dx0]
    for kind in range(4):
        outs += [results[n][kind] for n in _WEIGHTS]
    return tuple(outs)


def kernel(x, norm_g, w_in, a_conv_w, a_conv_b, a_ln_g, a_ln_b, a_pw_w, a_pw_b, s5_lambda_re, s5_lambda_im, s5_b_re, s5_b_im, s5_c_re, s5_c_im, s5_d, s5_log_dt, s5_glu_w, s5_glu_b, c_conv_w, d_conv_w, d_a_log, d_dt_bias, d_norm_g, w_out, final_g, loss_target, m_norm_g, m_w_in, m_a_conv_w, m_a_conv_b, m_a_ln_g, m_a_ln_b, m_a_pw_w, m_a_pw_b, m_s5_lambda_re, m_s5_lambda_im, m_s5_b_re, m_s5_b_im, m_s5_c_re, m_s5_c_im, m_s5_d, m_s5_log_dt, m_s5_glu_w, m_s5_glu_b, m_c_conv_w, m_d_conv_w, m_d_a_log, m_d_dt_bias, m_d_norm_g, m_w_out, m_final_g, v_norm_g, v_w_in, v_a_conv_w, v_a_conv_b, v_a_ln_g, v_a_ln_b, v_a_pw_w, v_a_pw_b, v_s5_lambda_re, v_s5_lambda_im, v_s5_b_re, v_s5_b_im, v_s5_c_re, v_s5_c_im, v_s5_d, v_s5_log_dt, v_s5_glu_w, v_s5_glu_b, v_c_conv_w, v_d_conv_w, v_d_a_log, v_d_dt_bias, v_d_norm_g, v_w_out, v_final_g):
    weights = dict(norm_g=norm_g, w_in=w_in, a_conv_w=a_conv_w, a_conv_b=a_conv_b, a_ln_g=a_ln_g, a_ln_b=a_ln_b, a_pw_w=a_pw_w, a_pw_b=a_pw_b, s5_lambda_re=s5_lambda_re, s5_lambda_im=s5_lambda_im, s5_b_re=s5_b_re, s5_b_im=s5_b_im, s5_c_re=s5_c_re, s5_c_im=s5_c_im, s5_d=s5_d, s5_log_dt=s5_log_dt, s5_glu_w=s5_glu_w, s5_glu_b=s5_glu_b, c_conv_w=c_conv_w, d_conv_w=d_conv_w, d_a_log=d_a_log, d_dt_bias=d_dt_bias, d_norm_g=d_norm_g, w_out=w_out, final_g=final_g)
    mom_m = dict(norm_g=m_norm_g, w_in=m_w_in, a_conv_w=m_a_conv_w, a_conv_b=m_a_conv_b, a_ln_g=m_a_ln_g, a_ln_b=m_a_ln_b, a_pw_w=m_a_pw_w, a_pw_b=m_a_pw_b, s5_lambda_re=m_s5_lambda_re, s5_lambda_im=m_s5_lambda_im, s5_b_re=m_s5_b_re, s5_b_im=m_s5_b_im, s5_c_re=m_s5_c_re, s5_c_im=m_s5_c_im, s5_d=m_s5_d, s5_log_dt=m_s5_log_dt, s5_glu_w=m_s5_glu_w, s5_glu_b=m_s5_glu_b, c_conv_w=m_c_conv_w, d_conv_w=m_d_conv_w, d_a_log=m_d_a_log, d_dt_bias=m_d_dt_bias, d_norm_g=m_d_norm_g, w_out=m_w_out, final_g=m_final_g)
    mom_v = dict(norm_g=v_norm_g, w_in=v_w_in, a_conv_w=v_a_conv_w, a_conv_b=v_a_conv_b, a_ln_g=v_a_ln_g, a_ln_b=v_a_ln_b, a_pw_w=v_a_pw_w, a_pw_b=v_a_pw_b, s5_lambda_re=v_s5_lambda_re, s5_lambda_im=v_s5_lambda_im, s5_b_re=v_s5_b_re, s5_b_im=v_s5_b_im, s5_c_re=v_s5_c_re, s5_c_im=v_s5_c_im, s5_d=v_s5_d, s5_log_dt=v_s5_log_dt, s5_glu_w=v_s5_glu_w, s5_glu_b=v_s5_glu_b, c_conv_w=v_c_conv_w, d_conv_w=v_d_conv_w, d_a_log=v_d_a_log, d_dt_bias=v_d_dt_bias, d_norm_g=v_d_norm_g, w_out=v_w_out, final_g=v_final_g)
    outs = _step(x[0], loss_target[0], weights, mom_m, mom_v)
    return (outs[0], outs[1][None]) + outs[2:]
```

```python
import functools

import jax
import jax.numpy as jnp
from jax import lax
from jax.experimental import pallas as pl
from jax.experimental.pallas import tpu as pltpu

F32 = jnp.float32
BF16 = jnp.bfloat16
HI = lax.Precision.HIGHEST
SDS = jax.ShapeDtypeStruct

N_DEV = 8
D = 1024
BR = 256
DEPTH = 4
IN_COLS = 3336
PW = 3456
AB_COL = 3328
EPS = 1e-6
TL = 512
SEG = TL // 8
HALO = 32
HALO_S = 8
KA, KC, KD = 31, 3, 4
CH = 64
DN_GROUP = 4
NH, HD = 4, 64
NSTATE = 1024
VMEM_LIMIT = 56 * 1024 * 1024

ADAM_LR, ADAM_B1, ADAM_B2, ADAM_EPS, ADAM_WD, ADAM_STEP = 0.001, 0.9, 0.999, 1e-08, 0.01, 10

NN = ((1,), (0,))
NT = ((1,), (1,))
TN = ((0,), (0,))


def _dot(a, b, dims, prec=None):
    return lax.dot_general(a, b, (dims, ((), ())), precision=prec, preferred_element_type=F32)


def _make_mm(cast, prec, fwd_dims):
    def prep(t):
        return t.astype(cast) if cast is not None else t

    @jax.custom_vjp
    def mm(a, w):
        return _dot(prep(a), prep(w), fwd_dims, prec)

    def fwd(a, w):
        return mm(a, w), (a, w)

    def bwd(res, dy):
        a, w = res
        a, w, dy = prep(a), prep(w), prep(dy)
        if fwd_dims == NN:
            return _dot(dy, w, NT, prec), _dot(a, dy, TN, prec)
        if fwd_dims == NT:
            return _dot(dy, w, NN, prec), _dot(dy, a, TN, prec)
        return _dot(w, dy, NT, prec), _dot(a, dy, NN, prec)

    mm.defvjp(fwd, bwd)
    return mm


mm = _make_mm(BF16, None, NN)
mm_nt = _make_mm(BF16, None, NT)
mm_tn = _make_mm(BF16, None, TN)
mmh = _make_mm(None, HI, NN)
mmh_nt = _make_mm(None, HI, NT)


def _sigmoid(x):
    return jax.nn.sigmoid(x)


def _silu(x):
    return x * jax.nn.sigmoid(x)


def _gelu(x):
    return 0.5 * x * (1.0 + jnp.tanh(0.7978845608028654 * (x + 0.044715 * (x * x * x))))


def _softplus(x):
    return jnp.maximum(x, 0.0) + jnp.log1p(jnp.exp(-jnp.abs(x)))


def _rms(x, g):
    return x * lax.rsqrt(jnp.mean(x * x, axis=-1, keepdims=True) + EPS) * g


def _cparams(sem):
    return pltpu.CompilerParams(dimension_semantics=sem, vmem_limit_bytes=VMEM_LIMIT)


def _tap_offsets(halo, taps):
    return [halo - (taps - 1) + k for k in range(taps)]


def _conv_fwd_impl(acat, w, tile, halo, taps):
    n = tile + halo
    out = None
    for k, off in enumerate(_tap_offsets(halo, taps)):
        src = jnp.roll(acat, n - off, axis=0)[:tile, :] if off != halo else acat[halo:, :]
        term = src * w[k:k + 1, :]
        out = term if out is None else out + term
    return out


def _make_conv(tile, halo, taps):
    @jax.custom_vjp
    def conv(acat, w):
        return _conv_fwd_impl(acat, w, tile, halo, taps)

    def fwd(acat, w):
        return conv(acat, w), (acat, w)

    def bwd(res, dy):
        acat, w = res
        n = tile + halo
        dyp = jnp.concatenate([dy, jnp.zeros((halo, dy.shape[1]), F32)], axis=0)
        rows = lax.broadcasted_iota(jnp.int32, w.shape, 0)
        dacat = None
        dw = jnp.zeros(w.shape, F32)
        for k, off in enumerate(_tap_offsets(halo, taps)):
            term = jnp.roll(dyp, off, axis=0) * w[k:k + 1, :]
            dacat = term if dacat is None else dacat + term
            src = jnp.roll(acat, n - off, axis=0)[:tile, :] if off != halo else acat[halo:, :]
            dw = dw + jnp.where(rows == k, jnp.sum(dy * src, axis=0, keepdims=True), 0.0)
        return dacat, dw

    conv.defvjp(fwd, bwd)
    return conv


def _halo_spec(tile, halo, width, col):
    per = tile // halo
    return pl.BlockSpec((halo, width), lambda i: (jnp.maximum(i * per - 1, 0), col))


def _halo_spec_rev(nt, tile, halo, width, col):
    per = tile // halo
    return pl.BlockSpec((halo, width), lambda i: (jnp.maximum((nt - 1 - i) * per - 1, 0), col))


def _dep_specs(deps):
    return [pl.BlockSpec((8, 128), lambda *_: (0, 0)) for _ in deps]


def _inproj_call(x, g, w, deps=()):
    L = x.shape[0]

    def body(x_ref, g_ref, w_ref, *rest):
        p_ref, h_ref = rest[len(deps):]
        h = _rms(x_ref[...], g_ref[...]).astype(BF16)
        h_ref[...] = h
        p_ref[...] = _dot(h, w_ref[...], NN)

    return pl.pallas_call(
        body, name="inproj", grid=(L // TL,),
        in_specs=[pl.BlockSpec((TL, D), lambda i: (i, 0)), pl.BlockSpec((1, D), lambda i: (0, 0)),
                  pl.BlockSpec((D, PW), lambda i: (0, 0))] + _dep_specs(deps),
        out_specs=[pl.BlockSpec((TL, PW), lambda i: (i, 0)), pl.BlockSpec((TL, D), lambda i: (i, 0))],
        out_shape=[SDS((L, PW), F32), SDS((L, D), BF16)],
        compiler_params=_cparams(("parallel",)),
    )(x, g, w, *deps)


def _outproj_call(x, ys, w):
    L = x.shape[0]

    def body(x_ref, a_ref, b_ref, c_ref, d_ref, w_ref, o_ref):
        acc = x_ref[...]
        for b, y_ref in enumerate((a_ref, b_ref, c_ref, d_ref)):
            acc = acc + _dot(y_ref[...].astype(BF16), w_ref[b * BR:(b + 1) * BR, :], NN)
        o_ref[...] = acc

    yspec = pl.BlockSpec((TL, BR), lambda i: (i, 0))
    return pl.pallas_call(
        body, name="outproj", grid=(L // TL,),
        in_specs=[pl.BlockSpec((TL, D), lambda i: (i, 0)), yspec, yspec, yspec, yspec,
                  pl.BlockSpec((D, D), lambda i: (0, 0))],
        out_specs=pl.BlockSpec((TL, D), lambda i: (i, 0)),
        out_shape=SDS((L, D), F32),
        compiler_params=_cparams(("parallel",)),
    )(x, *ys, w)


def _loss_call(x, g, target):
    L = x.shape[0]

    def body(x_ref, g_ref, t_ref, dx_ref, dg_ref, loss_ref):
        @pl.when(pl.program_id(0) == 0)
        def _():
            dg_ref[...] = jnp.zeros_like(dg_ref)
            loss_ref[...] = jnp.zeros_like(loss_ref)

        y, vjp = jax.vjp(_rms, x_ref[...], g_ref[...])
        err = y - t_ref[...]
        dx, dg = vjp(err * (1.0 / D))
        dx_ref[...] = dx
        dg_ref[...] += dg
        tot = jnp.sum(jnp.sum(err * err, axis=1, keepdims=True), axis=0, keepdims=True)
        loss_ref[...] += jnp.broadcast_to(tot * (0.5 / D), loss_ref.shape)

    return pl.pallas_call(
        body, name="loss_head", grid=(L // TL,),
        in_specs=[pl.BlockSpec((TL, D), lambda i: (i, 0)), pl.BlockSpec((1, D), lambda i: (0, 0)),
                  pl.BlockSpec((TL, D), lambda i: (i, 0))],
        out_specs=[pl.BlockSpec((TL, D), lambda i: (i, 0)), pl.BlockSpec((1, D), lambda i: (0, 0)),
                   pl.BlockSpec((1, 128), lambda i: (0, 0))],
        out_shape=[SDS((L, D), F32), SDS((1, D), F32), SDS((1, 128), F32)],
        compiler_params=_cparams(("arbitrary",)),
    )(x, g, target)


def _branch_a(valw, gatew, z, cw, cb, lg, lb, pw, pb, conv):
    a = conv(valw * _sigmoid(gatew), cw) + cb
    mu = jnp.mean(a, axis=-1, keepdims=True)
    xc = a - mu
    y = xc * lax.rsqrt(jnp.mean(xc * xc, axis=-1, keepdims=True) + EPS) * lg + lb
    y = mm(_silu(y), pw) + pb
    return y * _silu(z)


def _branch_c(bg, cw_, xw, z, w3, conv):
    return bg * conv(cw_ * xw, w3) * _silu(z)


def _ac_fwd_call(proj, p):
    L = proj.shape[0]

    def body(val, gate, za, hval, hgate, cb_, cc, cx, cz, hcc, hcx,
             acw, acb, alg, alb, apw, apb, ccw, ya_ref, yc_ref):
        nf = (pl.program_id(0) > 0).astype(F32)
        win = lambda h, m: jnp.concatenate([h[...] * nf, m[...]], axis=0)
        conv_a = functools.partial(_conv_fwd_impl, tile=TL, halo=HALO, taps=KA)
        conv_c = functools.partial(_conv_fwd_impl, tile=TL, halo=HALO, taps=KC)
        ya_ref[...] = _branch_a(win(hval, val), win(hgate, gate), za[...], acw[...], acb[...], alg[...], alb[...],
                                apw[...], apb[...], conv_a)
        yc_ref[...] = _branch_c(cb_[...], win(hcc, cc), win(hcx, cx), cz[...], ccw[...], conv_c)

    col = lambda j: pl.BlockSpec((TL, BR), lambda i: (i, j))
    hal = lambda j: _halo_spec(TL, HALO, BR, j)
    full = lambda a: pl.BlockSpec(a.shape, lambda i: (0,) * a.ndim)
    params = (p["a_conv_w"], p["a_conv_b"], p["a_ln_g"], p["a_ln_b"], p["a_pw_w"], p["a_pw_b"], p["c_conv_w"])
    return pl.pallas_call(
        body, name="ac_fwd", grid=(L // TL,),
        in_specs=[col(0), col(1), col(2), hal(0), hal(1), col(5), col(6), col(7), col(8), hal(6), hal(7)]
        + [full(a) for a in params],
        out_specs=[pl.BlockSpec((TL, BR), lambda i: (i, 0))] * 2,
        out_shape=[SDS((L, BR), F32)] * 2,
        compiler_params=_cparams(("parallel",)),
    )(*([proj] * 11), *params)


def _ac_bwd_call(proj, p, dya, dyc):
    L = proj.shape[0]
    nt = L // TL

    def body(val, gate, za, hval, hgate, cb_, cc, cx, cz, hcc, hcx,
             acw, acb, alg, alb, apw, apb, ccw, dya_ref, dyc_ref,
             da_ref, dc_ref, g_acw, g_acb, g_alg, g_alb, g_apw, g_apb, g_ccw, carry):
        i = pl.program_id(0)
        gouts = (g_acw, g_acb, g_alg, g_alb, g_apw, g_apb, g_ccw)

        @pl.when(i == 0)
        def _():
            carry[...] = jnp.zeros_like(carry)
            for r in gouts:
                r[...] = jnp.zeros_like(r)

        nf = (i < nt - 1).astype(F32)
        win = lambda h, m: jnp.concatenate([h[...] * nf, m[...]], axis=0)
        conv_a = _make_conv(TL, HALO, KA)
        conv_c = _make_conv(TL, HALO, KC)

        def f(valw, gatew, z, bg, ccw_, cxw, czv, w1, b1, lg, lb, pw, pb, w3):
            return (_branch_a(valw, gatew, z, w1, b1, lg, lb, pw, pb, conv_a),
                    _branch_c(bg, ccw_, cxw, czv, w3, conv_c))

        _, vjp = jax.vjp(f, win(hval, val), win(hgate, gate), za[...], cb_[...], win(hcc, cc), win(hcx, cx), cz[...],
                         acw[...], acb[...], alg[...], alb[...], apw[...].astype(F32), apb[...], ccw[...])
        (dvalw, dgatew, dz, dbg, dccw, dcxw, dczv, d1, d2, d3, d4, d5, d6, d7) = vjp((dya_ref[...], dyc_ref[...]))

        def settle(slot, dwin):
            tail = jnp.concatenate([jnp.zeros((TL - HALO, BR), F32), carry[slot]], axis=0)
            carry[slot] = dwin[:HALO, :]
            return (dwin[HALO:, :] + tail).astype(BF16)

        da_ref[:, 0:BR] = settle(0, dvalw)
        da_ref[:, BR:2 * BR] = settle(1, dgatew)
        da_ref[:, 2 * BR:3 * BR] = dz.astype(BF16)
        dc_ref[:, 0:BR] = dbg.astype(BF16)
        dc_ref[:, BR:2 * BR] = settle(2, dccw)
        dc_ref[:, 2 * BR:3 * BR] = settle(3, dcxw)
        dc_ref[:, 3 * BR:4 * BR] = dczv.astype(BF16)
        for r, g in zip(gouts, (d1, d2, d3, d4, d5, d6, d7)):
            r[...] += g

    col = lambda j: pl.BlockSpec((TL, BR), lambda i: (nt - 1 - i, j))
    hal = lambda j: _halo_spec_rev(nt, TL, HALO, BR, j)
    full = lambda a: pl.BlockSpec(a.shape, lambda i: (0,) * a.ndim)
    params = (p["a_conv_w"], p["a_conv_b"], p["a_ln_g"], p["a_ln_b"], p["a_pw_w"], p["a_pw_b"], p["c_conv_w"])
    rev = lambda w: pl.BlockSpec((TL, w), lambda i: (nt - 1 - i, 0))
    return pl.pallas_call(
        body, name="ac_bwd", grid=(nt,),
        in_specs=[col(0), col(1), col(2), hal(0), hal(1), col(5), col(6), col(7), col(8), hal(6), hal(7)]
        + [full(a) for a in params] + [rev(BR), rev(BR)],
        out_specs=[rev(3 * BR), rev(4 * BR)] + [full(a) for a in params],
        out_shape=[SDS((L, 3 * BR), BF16), SDS((L, 4 * BR), BF16)] + [SDS(a.shape, F32) for a in params],
        scratch_shapes=[pltpu.VMEM((4, HALO, BR), F32)],
        compiler_params=_cparams(("arbitrary",)),
    )(*([proj] * 11), *params, dya, dyc)


def _iota2(shape, dim):
    return lax.broadcasted_iota(jnp.int32, shape, dim)


def _s5_params(lam_re, lam_im, logdt, b_re, b_im, c_re, c_im):
    eg = (_iota2((16, NSTATE), 1) >> 6 == _iota2((16, NSTATE), 0)).astype(F32)
    dt = jnp.exp(mmh(jnp.broadcast_to(logdt, (8, 16)), eg)[0:1, :])
    lr = jnp.minimum(lam_re, -1e-4)
    li = lam_im
    mag = jnp.exp(lr * dt)
    lbr = mag * jnp.cos(li * dt)
    lbi = mag * jnp.sin(li * dt)
    den = lr * lr + li * li
    nr = lbr - 1.0
    fr = (nr * lr + lbi * li) / den
    fi = (lbi * lr - nr * li) / den
    row = _iota2((8, NSTATE), 0)
    f8 = jnp.where(row == 0, fr, jnp.where(row == 1, fi, 0.0))
    eye = (_iota2((NSTATE, NSTATE), 0) == _iota2((NSTATE, NSTATE), 1)).astype(F32)
    fcol = mmh_nt(eye, f8)
    frc, fic = fcol[:, 0:1], fcol[:, 1:2]
    bbr = frc * b_re - fic * b_im
    bbi = frc * b_im + fic * b_re
    e1 = ((_iota2((16, BR), 1) & 15) == _iota2((16, BR), 0)).astype(F32)
    m1 = ((_iota2((NSTATE, BR), 0) >> 6) == (_iota2((NSTATE, BR), 1) >> 4)).astype(F32)
    wbr = mmh(bbr, e1) * m1
    wbi = mmh(bbi, e1) * m1
    e2 = ((_iota2((64, NSTATE), 1) & 63) == _iota2((64, NSTATE), 0)).astype(F32)
    m2 = ((_iota2((BR, NSTATE), 0) >> 4) == (_iota2((BR, NSTATE), 1) >> 6)).astype(F32)
    wcr = mmh(c_re, e2) * m2
    wci = mmh(c_im, e2) * m2
    return lbr, lbi, wbr, wbi, wcr, wci


_S5_OUT = [(1, NSTATE), (1, NSTATE), (NSTATE, BR), (NSTATE, BR), (BR, NSTATE), (BR, NSTATE)]


def _s5_prep_call(sp, deps=()):
    def body(lre, lim, ldt, bre, bim, cre, cim, *rest):
        o_lbr, o_lbi, o_wbr, o_wbi, o_wcr, o_wci, pwr, pwi, qwr, qwi = rest[len(deps):]
        lbr, lbi, wbr, wbi, wcr, wci = _s5_params(lre[...], lim[...], ldt[...], bre[...], bim[...], cre[...], cim[...])
        o_lbr[...], o_lbi[...], o_wbr[...], o_wbi[...], o_wcr[...], o_wci[...] = lbr, lbi, wbr, wbi, wcr, wci
        pr, pi = lbr, lbi
        for i in range(SEG):
            pwr[i:i + 1, :] = pr
            pwi[i:i + 1, :] = pi
            qwr[SEG - 1 - i:SEG - i, :] = pr
            qwi[SEG - 1 - i:SEG - i, :] = -pi
            pr, pi = pr * lbr - pi * lbi, pr * lbi + pi * lbr

    args = (sp["lam_re"], sp["lam_im"], sp["log_dt"], sp["b_re"], sp["b_im"], sp["c_re"], sp["c_im"])
    return pl.pallas_call(
        body, name="s5_prep",
        out_shape=[SDS(s, F32) for s in _S5_OUT] + [SDS((SEG, NSTATE), F32)] * 4,
        compiler_params=pltpu.CompilerParams(vmem_limit_bytes=VMEM_LIMIT),
    )(*args, *deps)


def _s5_prep_bwd_call(sp, cots):
    def body(lre, lim, ldt, bre, bim, cre, cim, c0, c1, c2, c3, c4, c5, *outs):
        _, vjp = jax.vjp(_s5_params, lre[...], lim[...], ldt[...], bre[...], bim[...], cre[...], cim[...])
        grads = vjp((c0[...], c1[...], c2[...], c3[...], c4[...], c5[...]))
        for o, g in zip(outs, grads):
            o[...] = g

    args = (sp["lam_re"], sp["lam_im"], sp["log_dt"], sp["b_re"], sp["b_im"], sp["c_re"], sp["c_im"])
    return pl.pallas_call(
        body, name="s5_prep_bwd",
        out_shape=[SDS(a.shape, F32) for a in args],
        compiler_params=pltpu.CompilerParams(vmem_limit_bytes=VMEM_LIMIT),
    )(*args, *cots)


def _lanes(v, j):
    return v[:, j * 128:(j + 1) * 128]


def _s5_scan(sre, sim, pwr, pwi, cin_r, cin_i, reverse):
    row = _iota2((8, 128), 0)
    steps = (1, 2, 4)

    def lane_consts(j):
        lanes = slice(j * 128, (j + 1) * 128)
        if reverse:
            mult = [(jnp.broadcast_to(pwr[SEG - d:SEG - d + 1, lanes], (8, 128)),
                     jnp.broadcast_to(pwi[SEG - d:SEG - d + 1, lanes], (8, 128))) for d in steps]
            return mult, pwr[SEG - 8:SEG, lanes], pwi[SEG - 8:SEG, lanes]
        mult = [(jnp.broadcast_to(pwr[d - 1:d, lanes], (8, 128)),
                 jnp.broadcast_to(pwi[d - 1:d, lanes], (8, 128))) for d in steps]
        return mult, pwr[0:8, lanes], pwi[0:8, lanes]

    consts = [lane_consts(j) for j in range(8)]
    nblk = TL // 8

    def block(t, carry):
        b = nblk - 1 - t if reverse else t
        rows = pl.ds(pl.multiple_of(b * 8, 8), 8)
        new = []
        for j in range(8):
            mult, p8r, p8i = consts[j]
            vr, vi = sre.at[j], sim.at[j]
            sr, si = vr[rows, :], vi[rows, :]
            for d, (mr, mi) in zip(steps, mult):
                if reverse:
                    hr = jnp.where(row < 8 - d, pltpu.roll(sr, 8 - d, 0), 0.0)
                    hi = jnp.where(row < 8 - d, pltpu.roll(si, 8 - d, 0), 0.0)
                else:
                    hr = jnp.where(row >= d, pltpu.roll(sr, d, 0), 0.0)
                    hi = jnp.where(row >= d, pltpu.roll(si, d, 0), 0.0)
                sr, si = sr + mr * hr - mi * hi, si + mr * hi + mi * hr
            cr, ci = carry[2 * j], carry[2 * j + 1]
            sr, si = sr + p8r * cr - p8i * ci, si + p8r * ci + p8i * cr
            vr[rows, :] = sr
            vi[rows, :] = si
            edge = slice(0, 1) if reverse else slice(7, 8)
            new += [sr[edge, :], si[edge, :]]
        return tuple(new)

    init = []
    for j in range(8):
        init += [_lanes(cin_r, j), _lanes(cin_i, j)]
    ends = lax.fori_loop(0, nblk, block, tuple(init))
    return (jnp.concatenate([ends[2 * j] for j in range(8)], axis=1),
            jnp.concatenate([ends[2 * j + 1] for j in range(8)], axis=1))


def _bdot(a, b, dims):
    return _dot(a.astype(BF16), b.astype(BF16), dims)


def _s5_states(u, wbr, wbi, sre, sim):
    bur = _bdot(u, wbr, NT)
    bui = _bdot(u, wbi, NT)
    for j in range(8):
        sre[j] = _lanes(bur, j)
        sim[j] = _lanes(bui, j)


def _gather_lanes(s):
    return jnp.concatenate([s[j] for j in range(8)], axis=1)


def _s5_post(s_re, s_im, u, z, wcr, wci, dsk, gw, gb):
    y = mm_nt(s_re, wcr) - mm_nt(s_im, wci) + dsk * u
    yg = _gelu(y)
    return yg * _sigmoid(mm(yg, gw) + gb) * _silu(z)


def _s5_fwd_call(proj, prep, p):
    L = proj.shape[0]
    nt = L // TL
    lbr, lbi, wbr, wbi, wcr, wci, pwr, pwi, _, _ = prep

    def body(u_ref, z_ref, lbr_r, lbi_r, wbr_r, wbi_r, wcr_r, wci_r, pwr_r, pwi_r, d_r, gw_r, gb_r,
             yb_ref, cinr_ref, cini_ref, sre, sim, car, cai):
        @pl.when(pl.program_id(0) == 0)
        def _():
            car[...] = jnp.zeros_like(car)
            cai[...] = jnp.zeros_like(cai)

        u = u_ref[...]
        cinr_ref[0] = car[...]
        cini_ref[0] = cai[...]
        _s5_states(u, wbr_r[...], wbi_r[...], sre, sim)
        nr, ni = _s5_scan(sre, sim, pwr_r, pwi_r, car[...], cai[...], False)
        car[...] = nr
        cai[...] = ni
        yb_ref[...] = _s5_post(_gather_lanes(sre), _gather_lanes(sim), u, z_ref[...], wcr_r[...], wci_r[...],
                               d_r[...], gw_r[...], gb_r[...])

    full = lambda a: pl.BlockSpec(a.shape, lambda i: (0,) * a.ndim)
    consts = (lbr, lbi, wbr, wbi, wcr, wci, pwr, pwi, p["s5_d"], p["s5_glu_w"], p["s5_glu_b"])
    cspec = pl.BlockSpec((1, 1, NSTATE), lambda i: (i, 0, 0))
    return pl.pallas_call(
        body, name="s5_fwd", grid=(nt,),
        in_specs=[pl.BlockSpec((TL, BR), lambda i: (i, 3)), pl.BlockSpec((TL, BR), lambda i: (i, 4))]
        + [full(a) for a in consts],
        out_specs=[pl.BlockSpec((TL, BR), lambda i: (i, 0)), cspec, cspec],
        out_shape=[SDS((L, BR), F32), SDS((nt, 1, NSTATE), F32), SDS((nt, 1, NSTATE), F32)],
        scratch_shapes=[pltpu.VMEM((8, TL, 128), F32), pltpu.VMEM((8, TL, 128), F32),
                        pltpu.VMEM((1, NSTATE), F32), pltpu.VMEM((1, NSTATE), F32)],
        compiler_params=_cparams(("arbitrary",)),
    )(proj, proj, *consts)


def _s5_bwd_call(proj, prep, p, cin_r, cin_i, dyb):
    L = proj.shape[0]
    nt = L // TL
    lbr, lbi, wbr, wbi, wcr, wci, pwr, pwi, qwr, qwi = prep

    def body(u_ref, z_ref, lbr_r, lbi_r, wbr_r, wbi_r, wcr_r, wci_r, pwr_r, pwi_r, qwr_r, qwi_r, d_r, gw_r, gb_r,
             cinr_ref, cini_ref, dy_ref,
             db_ref, g_lbr, g_lbi, g_wbr, g_wbi, g_wcr, g_wci, g_d, g_gw, g_gb, sre, sim, gre, gim, car, cai):
        gouts = (g_lbr, g_lbi, g_wbr, g_wbi, g_wcr, g_wci, g_d, g_gw, g_gb)

        @pl.when(pl.program_id(0) == 0)
        def _():
            car[...] = jnp.zeros_like(car)
            cai[...] = jnp.zeros_like(cai)
            for r in gouts:
                r[...] = jnp.zeros_like(r)

        u = u_ref[...]
        lr, li = lbr_r[...], lbi_r[...]
        c0r, c0i = cinr_ref[0], cini_ref[0]
        _s5_states(u, wbr_r[...], wbi_r[...], sre, sim)
        _s5_scan(sre, sim, pwr_r, pwi_r, c0r, c0i, False)
        s_re, s_im = _gather_lanes(sre), _gather_lanes(sim)
        _, vjp = jax.vjp(_s5_post, s_re, s_im, u, z_ref[...], wcr_r[...], wci_r[...], d_r[...],
                         gw_r[...].astype(F32), gb_r[...])
        ds_re, ds_im, du, dz, dwcr, dwci, dd, dgw, dgb = vjp(dy_ref[...])
        for j in range(8):
            gre[j] = _lanes(ds_re, j)
            gim[j] = _lanes(ds_im, j)
        nr, ni = _s5_scan(gre, gim, qwr_r, qwi_r, car[...], cai[...], True)
        car[...] = nr
        cai[...] = ni
        a_re, a_im = _gather_lanes(gre), _gather_lanes(gim)
        first = _iota2((TL, NSTATE), 0) == 0
        p_re = jnp.where(first, c0r, jnp.roll(s_re, 1, axis=0))
        p_im = jnp.where(first, c0i, jnp.roll(s_im, 1, axis=0))
        g_lbr[...] += jnp.sum(a_re * p_re + a_im * p_im, axis=0, keepdims=True)
        g_lbi[...] += jnp.sum(a_im * p_re - a_re * p_im, axis=0, keepdims=True)
        du = du + _bdot(a_re, wbr_r[...], NN) + _bdot(a_im, wbi_r[...], NN)
        g_wbr[...] += _bdot(a_re, u, TN)
        g_wbi[...] += _bdot(a_im, u, TN)
        g_wcr[...] += dwcr
        g_wci[...] += dwci
        g_d[...] += dd
        g_gw[...] += dgw
        g_gb[...] += dgb
        db_ref[:, 0:BR] = du.astype(BF16)
        db_ref[:, BR:2 * BR] = dz.astype(BF16)

    full = lambda a: pl.BlockSpec(a.shape, lambda i: (0,) * a.ndim)
    consts = (lbr, lbi, wbr, wbi, wcr, wci, pwr, pwi, qwr, qwi, p["s5_d"], p["s5_glu_w"], p["s5_glu_b"])
    cspec = pl.BlockSpec((1, 1, NSTATE), lambda i: (nt - 1 - i, 0, 0))
    gshapes = _S5_OUT + [(1, BR), (BR, BR), (1, BR)]
    return pl.pallas_call(
        body, name="s5_bwd", grid=(nt,),
        in_specs=[pl.BlockSpec((TL, BR), lambda i: (nt - 1 - i, 3)), pl.BlockSpec((TL, BR), lambda i: (nt - 1 - i, 4))]
        + [full(a) for a in consts] + [cspec, cspec, pl.BlockSpec((TL, BR), lambda i: (nt - 1 - i, 0))],
        out_specs=[pl.BlockSpec((TL, 2 * BR), lambda i: (nt - 1 - i, 0))]
        + [pl.BlockSpec(s, lambda i: (0, 0)) for s in gshapes],
        out_shape=[SDS((L, 2 * BR), BF16)] + [SDS(s, F32) for s in gshapes],
        scratch_shapes=[pltpu.VMEM((8, TL, 128), F32)] * 4 + [pltpu.VMEM((1, NSTATE), F32)] * 2,
        compiler_params=_cparams(("arbitrary",)),
    )(proj, proj, *consts, cin_r, cin_i, dyb)


def _heads(x):
    return [x[:, h * HD:(h + 1) * HD] for h in range(NH)]


def _l2n(x, scale):
    return jnp.concatenate([xh * (lax.rsqrt(jnp.sum(xh * xh, axis=-1, keepdims=True) + EPS) * scale)
                            for xh in _heads(x)], axis=1)


def _dn_pre(qkvw, ab, cw, alog, dtb, conv, rows):
    c = _silu(conv(qkvw, cw))
    q = _l2n(c[:, 0:BR], HD ** -0.5)
    k = _l2n(c[:, BR:2 * BR], 1.0)
    v = c[:, 2 * BR:3 * BR]
    g = -jnp.exp(alog) * _softplus(ab + dtb)
    ri, ci = _iota2((rows, rows), 0), _iota2((rows, rows), 1)
    tri = ((ri >= ci) & ((ri >> 6) == (ci >> 6))).astype(F32)
    gc = mmh(tri, g)
    lane = _iota2(ab.shape, 1)
    return q, k, v, jnp.where(lane < NH, gc, jnp.where(lane < 2 * NH, _sigmoid(ab), 0.0))


def _dn_pre_fwd_call(proj, p):
    L = proj.shape[0]

    def body(m_ref, h_ref, ab_ref, cw, alog, dtb, q_ref, k_ref, v_ref, gb_ref):
        nf = (pl.program_id(0) > 0).astype(F32)
        qkvw = jnp.concatenate([h_ref[...] * nf, m_ref[...]], axis=0)
        conv = functools.partial(_conv_fwd_impl, tile=TL, halo=HALO_S, taps=KD)
        q_ref[...], k_ref[...], v_ref[...], gb_ref[...] = _dn_pre(qkvw, ab_ref[...], cw[...], alog[...], dtb[...], conv, TL)

    full = lambda a: pl.BlockSpec(a.shape, lambda i: (0,) * a.ndim)
    params = (p["d_conv_w"], p["d_a_log"], p["d_dt_bias"])
    o = pl.BlockSpec((TL, BR), lambda i: (i, 0))
    return pl.pallas_call(
        body, name="dn_pre_fwd", grid=(L // TL,),
        in_specs=[pl.BlockSpec((TL, 3 * BR), lambda i: (i, 3)), _halo_spec(TL, HALO_S, 3 * BR, 3),
                  pl.BlockSpec((TL, 128), lambda i: (i, AB_COL // 128))] + [full(a) for a in params],
        out_specs=[o, o, o, pl.BlockSpec((TL, 128), lambda i: (i, 0))],
        out_shape=[SDS((L, BR), F32)] * 3 + [SDS((L, 128), F32)],
        compiler_params=_cparams(("parallel",)),
    )(proj, proj, proj, *params)


def _dn_pre_bwd_call(proj, p, dq, dk, dv, dgb):
    L = proj.shape[0]
    nt = L // TL

    def body(m_ref, h_ref, ab_ref, cw, alog, dtb, dq_r, dk_r, dv_r, dgb_r,
             dqkv_ref, dab_ref, g_cw, g_alog, g_dtb, carry):
        i = pl.program_id(0)

        @pl.when(i == 0)
        def _():
            carry[...] = jnp.zeros_like(carry)
            for r in (g_cw, g_alog, g_dtb):
                r[...] = jnp.zeros_like(r)

        nf = (i < nt - 1).astype(F32)
        qkvw = jnp.concatenate([h_ref[...] * nf, m_ref[...]], axis=0)
        conv = _make_conv(TL, HALO_S, KD)
        _, vjp = jax.vjp(lambda a, b, c, d, e: _dn_pre(a, b, c, d, e, conv, TL),
                         qkvw, ab_ref[...], cw[...], alog[...], dtb[...])
        dwin, dab, dcw, dalog, ddtb = vjp((dq_r[...], dk_r[...], dv_r[...], dgb_r[...]))
        tail = jnp.concatenate([jnp.zeros((TL - HALO_S, 3 * BR), F32), carry[...]], axis=0)
        carry[...] = dwin[:HALO_S, :]
        dqkv_ref[...] = (dwin[HALO_S:, :] + tail).astype(BF16)
        dab_ref[...] = dab.astype(BF16)
        g_cw[...] += dcw
        g_alog[...] += dalog
        g_dtb[...] += ddtb

    full = lambda a: pl.BlockSpec(a.shape, lambda i: (0,) * a.ndim)
    params = (p["d_conv_w"], p["d_a_log"], p["d_dt_bias"])
    rev = lambda w: pl.BlockSpec((TL, w), lambda i: (nt - 1 - i, 0))
    return pl.pallas_call(
        body, name="dn_pre_bwd", grid=(nt,),
        in_specs=[pl.BlockSpec((TL, 3 * BR), lambda i: (nt - 1 - i, 3)), _halo_spec_rev(nt, TL, HALO_S, 3 * BR, 3),
                  pl.BlockSpec((TL, 128), lambda i: (nt - 1 - i, AB_COL // 128))] + [full(a) for a in params]
        + [rev(BR), rev(BR), rev(BR), rev(128)],
        out_specs=[rev(3 * BR), rev(128)] + [full(a) for a in params],
        out_shape=[SDS((L, 3 * BR), BF16), SDS((L, 128), BF16)] + [SDS(a.shape, F32) for a in params],
        scratch_shapes=[pltpu.VMEM((HALO_S, 3 * BR), F32)],
        compiler_params=_cparams(("arbitrary",)),
    )(proj, proj, proj, *params, dq, dk, dv, dgb)


def _dn_group(q, k, v, gb, z, ng, *s):
    ri, ci = _iota2((CH, CH), 0), _iota2((CH, CH), 1)
    causal, strict = ri >= ci, ri > ci
    eye = (ri == ci).astype(F32)
    s = list(s)
    pairs = []
    for c in range(DN_GROUP):
        rows = slice(c * CH, (c + 1) * CH)
        gbc = gb[rows, :]
        for h, (qh, kh, vh, zh) in enumerate(zip(_heads(q[rows, :]), _heads(k[rows, :]), _heads(v[rows, :]),
                                                 _heads(z[rows, :]))):
            gc = jnp.broadcast_to(gbc[:, h:h + 1], (CH, HD))
            beta = gbc[:, NH + h:NH + h + 1]
            decay = jnp.where(causal, jnp.exp(jnp.where(causal, gc - gc.T, 0.0)), 0.0)
            egc = jnp.exp(gc)
            glast = gc[CH - 1:CH, :]
            kb = kh * beta
            pairs.append(dict(q=qh, k=kh, z=zh, decay=decay, qe=qh * egc, kd=kh * jnp.exp(glast - gc),
                              sdec=jnp.exp(glast[:, 0:1]), kb=kb, rhs=jnp.concatenate([vh * beta, kb * egc], axis=1)))
    for p in pairs:
        p["pw"] = jnp.where(strict, mm_nt(p["kb"], p["k"]) * p["decay"], 0.0)
    for p in pairs:
        p["t"] = eye - p["pw"]
    for _ in range(5):
        for p in pairs:
            p["pw"] = mm(p["pw"], p["pw"])
        for p in pairs:
            p["t"] = mm(p["t"], eye + p["pw"])
    for p in pairs:
        p["uw"] = mm(p["t"], p["rhs"])
    for p in pairs:
        p["attn"] = mm_nt(p["q"], p["k"]) * p["decay"]
    out_rows = []
    for c in range(DN_GROUP):
        grp = pairs[c * NH:(c + 1) * NH]
        ws = [mm(jnp.concatenate([p["uw"][:, HD:], p["qe"]], axis=0), s[h]) for h, p in enumerate(grp)]
        v_new = [p["uw"][:, :HD] - w_[:CH, :] for p, w_ in zip(grp, ws)]
        o = [w_[CH:, :] + mm(p["attn"], vn) for p, w_, vn in zip(grp, ws, v_new)]
        s = [s[h] * p["sdec"] + mm_tn(p["kd"], vn) for h, (p, vn) in enumerate(zip(grp, v_new))]
        o = [oh * lax.rsqrt(jnp.mean(oh * oh, axis=-1, keepdims=True) + EPS) * ng * _silu(p["z"]) for oh, p in zip(o, grp)]
        out_rows.append(jnp.concatenate(o, axis=1))
    return (jnp.concatenate(out_rows, axis=0), *s)


def _dn_core_fwd_call(proj, q, k, v, gb, ng):
    L = q.shape[0]
    rows = DN_GROUP * CH
    ng_ = L // rows

    def body(q_r, k_r, v_r, gb_r, z_r, ng_r, yd_ref, ssave_ref, s_scr):
        @pl.when(pl.program_id(0) == 0)
        def _():
            s_scr[...] = jnp.zeros_like(s_scr)

        ssave_ref[0] = s_scr[...]
        yd, *s2 = _dn_group(q_r[...], k_r[...], v_r[...], gb_r[...], z_r[...], ng_r[...], *[s_scr[h] for h in range(NH)])
        yd_ref[...] = yd
        for h in range(NH):
            s_scr[h] = s2[h]

    c = pl.BlockSpec((rows, BR), lambda i: (i, 0))
    return pl.pallas_call(
        body, name="dn_core_fwd", grid=(ng_,),
        in_specs=[c, c, c, pl.BlockSpec((rows, 128), lambda i: (i, 0)), pl.BlockSpec((rows, BR), lambda i: (i, 12)),
                  pl.BlockSpec((1, HD), lambda i: (0, 0))],
        out_specs=[c, pl.BlockSpec((1, NH, HD, HD), lambda i: (i, 0, 0, 0))],
        out_shape=[SDS((L, BR), F32), SDS((ng_, NH, HD, HD), F32)],
        scratch_shapes=[pltpu.VMEM((NH, HD, HD), F32)],
        compiler_params=_cparams(("arbitrary",)),
    )(q, k, v, gb, proj, ng)


def _dn_core_bwd_call(proj, q, k, v, gb, ng, ssave, dyd):
    L = q.shape[0]
    rows = DN_GROUP * CH
    ng_ = L // rows

    def body(q_r, k_r, v_r, gb_r, z_r, ng_r, s_r, dy_r, dq_ref, dk_ref, dv_ref, dgb_ref, dz_ref, g_ng, ds_scr):
        @pl.when(pl.program_id(0) == 0)
        def _():
            ds_scr[...] = jnp.zeros_like(ds_scr)
            g_ng[...] = jnp.zeros_like(g_ng)

        _, vjp = jax.vjp(_dn_group, q_r[...], k_r[...], v_r[...], gb_r[...], z_r[...], ng_r[...],
                         *[s_r[0, h] for h in range(NH)])
        dq, dk, dv, dgb, dz, dng, *ds = vjp((dy_r[...], *[ds_scr[h] for h in range(NH)]))
        dq_ref[...], dk_ref[...], dv_ref[...], dgb_ref[...] = dq, dk, dv, dgb
        dz_ref[...] = dz.astype(BF16)
        g_ng[...] += dng
        for h in range(NH):
            ds_scr[h] = ds[h]

    c = pl.BlockSpec((rows, BR), lambda i: (ng_ - 1 - i, 0))
    c128 = pl.BlockSpec((rows, 128), lambda i: (ng_ - 1 - i, 0))
    return pl.pallas_call(
        body, name="dn_core_bwd", grid=(ng_,),
        in_specs=[c, c, c, c128, pl.BlockSpec((rows, BR), lambda i: (ng_ - 1 - i, 12)),
                  pl.BlockSpec((1, HD), lambda i: (0, 0)),
                  pl.BlockSpec((1, NH, HD, HD), lambda i: (ng_ - 1 - i, 0, 0, 0)), c],
        out_specs=[c, c, c, c128, c, pl.BlockSpec((1, HD), lambda i: (0, 0))],
        out_shape=[SDS((L, BR), F32)] * 3 + [SDS((L, 128), F32), SDS((L, BR), BF16), SDS((1, HD), F32)],
        scratch_shapes=[pltpu.VMEM((NH, HD, HD), F32)],
        compiler_params=_cparams(("arbitrary",)),
    )(q, k, v, gb, proj, ng, ssave, dyd)


def _outproj_bwd_call(dx, ys, w, deps=()):
    L = dx.shape[0]

    def body(dx_ref, a_ref, b_ref, c_ref, d_ref, w_ref, *rest):
        da, db, dc, dd, dw_ref = rest[len(deps):]

        @pl.when(pl.program_id(0) == 0)
        def _():
            dw_ref[...] = jnp.zeros_like(dw_ref)

        dxb = dx_ref[...].astype(BF16)
        for b, (y_ref, o_ref) in enumerate(zip((a_ref, b_ref, c_ref, d_ref), (da, db, dc, dd))):
            o_ref[...] = _dot(dxb, w_ref[b * BR:(b + 1) * BR, :], NT)
            dw_ref[b * BR:(b + 1) * BR, :] += _dot(y_ref[...].astype(BF16), dxb, TN)

    yspec = pl.BlockSpec((TL, BR), lambda i: (i, 0))
    return pl.pallas_call(
        body, name="outproj_bwd", grid=(L // TL,),
        in_specs=[pl.BlockSpec((TL, D), lambda i: (i, 0)), yspec, yspec, yspec, yspec,
                  pl.BlockSpec((D, D), lambda i: (0, 0))] + _dep_specs(deps),
        out_specs=[yspec] * 4 + [pl.BlockSpec((D, D), lambda i: (0, 0))],
        out_shape=[SDS((L, BR), F32)] * 4 + [SDS((D, D), F32)],
        compiler_params=_cparams(("arbitrary",)),
    )(dx, *ys, w, *deps)


def _slab_cols(slabs):
    widths = [s.shape[1] for s in slabs]
    starts = [sum(widths[:i]) for i in range(len(widths))]
    assert starts[-1] + widths[-1] == PW
    return list(zip(starts, widths))


def _inproj_bwd_x_call(slabs, w, x, g, dx_next, deps=()):
    L = x.shape[0]
    cols = _slab_cols(slabs)
    n = len(slabs)

    def body(*refs):
        dp_refs, (w_ref, x_ref, g_ref, dxn_ref) = refs[:n], refs[n:n + 4]
        dx_ref, dg_ref = refs[n + 4 + len(deps):]

        @pl.when(pl.program_id(0) == 0)
        def _():
            dg_ref[...] = jnp.zeros_like(dg_ref)

        dh = None
        for dp_ref, (c0, cw) in zip(dp_refs, cols):
            part = _dot(dp_ref[...], w_ref[:, c0:c0 + cw], NT)
            dh = part if dh is None else dh + part
        _, vjp = jax.vjp(_rms, x_ref[...], g_ref[...])
        dx, dg = vjp(dh)
        dx_ref[...] = dx + dxn_ref[...]
        dg_ref[...] += dg

    row = lambda w_: pl.BlockSpec((TL, w_), lambda i: (i, 0))
    return pl.pallas_call(
        body, name="inproj_bwd_x", grid=(L // TL,),
        in_specs=[row(cw) for _, cw in cols]
        + [pl.BlockSpec((D, PW), lambda i: (0, 0)), row(D), pl.BlockSpec((1, D), lambda i: (0, 0)), row(D)]
        + _dep_specs(deps),
        out_specs=[row(D), pl.BlockSpec((1, D), lambda i: (0, 0))],
        out_shape=[SDS((L, D), F32), SDS((1, D), F32)],
        compiler_params=_cparams(("arbitrary",)),
    )(*slabs, w, x, g, dx_next, *deps)


def _inproj_bwd_w_call(h, slabs):
    L = h.shape[0]
    cols = _slab_cols(slabs)
    n = len(slabs)

    def body(*refs):
        h_ref, dp_refs, dw_ref = refs[0], refs[1:1 + n], refs[1 + n]

        @pl.when(pl.program_id(0) == 0)
        def _():
            dw_ref[...] = jnp.zeros_like(dw_ref)

        hv = h_ref[...]
        for dp_ref, (c0, cw) in zip(dp_refs, cols):
            dw_ref[:, c0:c0 + cw] += _dot(hv, dp_ref[...], TN)

    row = lambda w_: pl.BlockSpec((TL, w_), lambda i: (i, 0))
    return pl.pallas_call(
        body, name="inproj_bwd_w", grid=(L // TL,),
        in_specs=[row(D)] + [row(cw) for _, cw in cols],
        out_specs=pl.BlockSpec((D, PW), lambda i: (0, 0)),
        out_shape=SDS((D, PW), F32),
        compiler_params=_cparams(("arbitrary",)),
    )(h, *slabs)


def _exchange_call(name, flows):
    n = len(flows)

    def body(*refs):
        srcs, dsts = refs[:n], refs[n:2 * n]
        send_sems, recv_sems, local_sems = refs[2 * n:]
        x, y, c = lax.axis_index("x"), lax.axis_index("y"), lax.axis_index("c")
        me = 4 * x + 2 * y + c
        copies = []
        for mask in range(1, N_DEV):
            px = 1 - x if mask & 4 else x
            py = 1 - y if mask & 2 else y
            pc = 1 - c if mask & 1 else c
            for f, (_, src_at, _, dst_at) in enumerate(flows):
                cp = pltpu.make_async_remote_copy(
                    src_ref=src_at(srcs[f], 4 * px + 2 * py + pc), dst_ref=dst_at(dsts[f], me),
                    send_sem=send_sems.at[mask - 1, f], recv_sem=recv_sems.at[mask - 1, f],
                    device_id=(px, py, pc), device_id_type=pl.DeviceIdType.MESH)
                cp.start()
                copies.append(cp)
        mine = [pltpu.make_async_copy(src_at(srcs[f], me), dst_at(dsts[f], me), local_sems.at[f])
                for f, (_, src_at, _, dst_at) in enumerate(flows)]
        for cp in mine:
            cp.start()
        for cp in copies + mine:
            cp.wait()

    return pl.pallas_call(
        body, name=name,
        in_specs=[pl.BlockSpec(memory_space=pl.ANY)] * n,
        out_specs=[pl.BlockSpec(memory_space=pl.ANY)] * n,
        out_shape=[SDS(tuple(shape), src.dtype) for src, _, shape, _ in flows],
        scratch_shapes=[pltpu.SemaphoreType.DMA((N_DEV - 1, n)), pltpu.SemaphoreType.DMA((N_DEV - 1, n)),
                        pltpu.SemaphoreType.DMA((n,))],
    )(*[f[0] for f in flows])


def _whole(ref, _):
    return ref


def _slot(ref, k):
    return ref.at[k]


_HBM_SPEC = pl.BlockSpec(memory_space=pltpu.HBM)
_SEM_SPEC = pl.BlockSpec(memory_space=pltpu.SEMAPHORE)
_DATAFLOW = pltpu.SideEffectType.DATAFLOW_SIDE_EFFECTING


def _split_copies(views, src_refs, land_refs, send_sems, recv_sems):
    x, y, c = lax.axis_index("x"), lax.axis_index("y"), lax.axis_index("c")
    me = 4 * x + 2 * y + c
    copies = []
    for mask in range(1, N_DEV):
        px = 1 - x if mask & 4 else x
        py = 1 - y if mask & 2 else y
        pc = 1 - c if mask & 1 else c
        for f, (src_at, dst_at) in enumerate(views):
            pair = (mask - 1) * len(views) + f
            copies.append(pltpu.make_async_remote_copy(
                src_ref=src_at(src_refs[f], 4 * px + 2 * py + pc), dst_ref=dst_at(land_refs[f], me),
                send_sem=send_sems.at[pair], recv_sem=recv_sems.at[pair],
                device_id=(px, py, pc), device_id_type=pl.DeviceIdType.MESH))
    return copies


def _split_start_call(name, srcs, lands, views):
    n = len(srcs)

    def body(*refs):
        src_refs, land_refs = refs[:n], refs[n:2 * n]
        send_sems, recv_sems, token = refs[2 * n], refs[2 * n + 1], refs[-1]
        for cp in _split_copies(views, src_refs, land_refs, send_sems, recv_sems):
            cp.start()
        token[...] = jnp.zeros_like(token)

    arrays = list(srcs) + list(lands)
    outs = pl.pallas_call(
        body, name=name,
        out_shape=(pltpu.SemaphoreType.DMA(((N_DEV - 1) * n,)), pltpu.SemaphoreType.DMA(((N_DEV - 1) * n,)),
                   *[pltpu.HBM(a.shape, a.dtype) for a in arrays], SDS((8, 128), F32)),
        in_specs=[_HBM_SPEC] * (2 * n),
        out_specs=(_SEM_SPEC, _SEM_SPEC, *[_HBM_SPEC] * (2 * n), pl.BlockSpec(memory_space=pltpu.VMEM)),
        input_output_aliases={i: 2 + i for i in range(2 * n)},
        compiler_params=pltpu.CompilerParams(has_side_effects=_DATAFLOW),
    )(*[pltpu.with_memory_space_constraint(a, pltpu.HBM) for a in arrays])
    return outs[0], outs[1], list(outs[2:2 + 2 * n]), outs[-1]


def _split_wait_call(name, send_sems, recv_sems, thru, views, after):
    n = len(views)

    def body(*refs):
        src_refs, land_refs = refs[:n], refs[n:2 * n]
        send, recv = refs[2 * n], refs[2 * n + 1]
        for cp in _split_copies(views, src_refs, land_refs, send, recv):
            cp.wait_send()
            cp.wait_recv()

    outs = pl.pallas_call(
        body, name=name,
        out_shape=tuple(pltpu.HBM(a.shape, a.dtype) for a in thru),
        in_specs=[_HBM_SPEC] * (2 * n) + [_SEM_SPEC, _SEM_SPEC, pl.BlockSpec(memory_space=pl.ANY)],
        out_specs=tuple([_HBM_SPEC] * (2 * n)),
        input_output_aliases={i: i for i in range(2 * n)},
        compiler_params=pltpu.CompilerParams(has_side_effects=_DATAFLOW),
    )(*thru, send_sems, recv_sems, after)
    return list(outs[n:])


def _own_slot(block, me):
    zone = jnp.zeros((N_DEV,) + block.shape, block.dtype)
    return lax.dynamic_update_slice(zone, block[None], (me,) + (0,) * block.ndim)


def _reduce_adamw_call(parts, w, m, v, block, name):
    nsrc = parts.shape[0]
    grid = tuple(s // b for s, b in zip(w.shape, block))
    c1 = 1.0 - ADAM_B1 ** ADAM_STEP
    c2 = 1.0 - ADAM_B2 ** ADAM_STEP

    def body(p_ref, w_ref, m_ref, v_ref, g_ref, d_ref, nm_ref, nv_ref):
        g = p_ref[0].astype(F32)
        for k in range(1, nsrc):
            g = g + p_ref[k].astype(F32)
        nm = ADAM_B1 * m_ref[...] + (1.0 - ADAM_B1) * g
        nv = ADAM_B2 * v_ref[...] + (1.0 - ADAM_B2) * (g * g)
        g_ref[...] = g
        nm_ref[...] = nm
        nv_ref[...] = nv
        d_ref[...] = -ADAM_LR * ((nm / c1) / (jnp.sqrt(nv / c2) + ADAM_EPS) + ADAM_WD * w_ref[...])

    own = pl.BlockSpec(tuple(block), lambda *i: i)
    return pl.pallas_call(
        body, name=name, grid=grid,
        in_specs=[pl.BlockSpec((nsrc,) + tuple(block), lambda *i: (0,) + i), own, own, own],
        out_specs=[own] * 4,
        out_shape=[SDS(w.shape, F32)] * 4,
        compiler_params=_cparams(("parallel",) * len(grid)),
    )(parts, w, m, v)


RELAYOUT_ROWS = 256
SHARD_COLS = IN_COLS // N_DEV


def _win_gather_layout_call(shards):
    def body(w_ref, o_ref):
        nat = jnp.concatenate([w_ref[k].astype(F32) for k in range(N_DEV)], axis=1)
        out = jnp.concatenate([nat[:, :3072], nat[:, 3080:], nat[:, 3072:3080],
                               jnp.zeros((RELAYOUT_ROWS, PW - IN_COLS), F32)], axis=1)
        o_ref[...] = out.astype(BF16)

    return pl.pallas_call(
        body, name="w_in_layout", grid=(D // RELAYOUT_ROWS,),
        in_specs=[pl.BlockSpec((N_DEV, RELAYOUT_ROWS, SHARD_COLS), lambda i: (0, i, 0))],
        out_specs=pl.BlockSpec((RELAYOUT_ROWS, PW), lambda i: (i, 0)),
        out_shape=SDS((D, PW), BF16),
        compiler_params=_cparams(("parallel",)),
    )(shards)


def _win_scatter_layout_call(grad):
    def body(g_ref, o_ref):
        g = g_ref[...]
        nat = jnp.concatenate([g[:, :3072], g[:, AB_COL:AB_COL + 8], g[:, 3072:AB_COL]], axis=1)
        for k in range(N_DEV):
            o_ref[k] = nat[:, SHARD_COLS * k:SHARD_COLS * (k + 1)].astype(BF16)

    return pl.pallas_call(
        body, name="w_in_grad_layout", grid=(D // RELAYOUT_ROWS,),
        in_specs=[pl.BlockSpec((RELAYOUT_ROWS, PW), lambda i: (i, 0))],
        out_specs=pl.BlockSpec((N_DEV, RELAYOUT_ROWS, SHARD_COLS), lambda i: (0, i, 0)),
        out_shape=SDS((N_DEV, D, SHARD_COLS), BF16),
        compiler_params=_cparams(("parallel",)),
    )(grad)


def _reduce_adamw_layers_call(parts, w, m, v, rows, name):
    _, R, C = w.shape
    c1 = 1.0 - ADAM_B1 ** ADAM_STEP
    c2 = 1.0 - ADAM_B2 ** ADAM_STEP

    def body(*refs):
        p_refs = refs[:DEPTH]
        w_ref, m_ref, v_ref, g_ref, d_ref, nm_ref, nv_ref = refs[DEPTH:]
        for l in range(DEPTH):
            @pl.when(pl.program_id(0) == l)
            def _(l=l):
                g = p_refs[l][0].astype(F32)
                for k in range(1, N_DEV):
                    g = g + p_refs[l][k].astype(F32)
                nm = ADAM_B1 * m_ref[0] + (1.0 - ADAM_B1) * g
                nv = ADAM_B2 * v_ref[0] + (1.0 - ADAM_B2) * (g * g)
                g_ref[0] = g
                nm_ref[0] = nm
                nv_ref[0] = nv
                d_ref[0] = -ADAM_LR * ((nm / c1) / (jnp.sqrt(nv / c2) + ADAM_EPS) + ADAM_WD * w_ref[0])

    def part_spec(l):
        return pl.BlockSpec((N_DEV, rows, C), lambda j, i: (0, jnp.where(j == l, i, 0), 0))

    own = pl.BlockSpec((1, rows, C), lambda j, i: (j, i, 0))
    return pl.pallas_call(
        body, name=name, grid=(DEPTH, R // rows),
        in_specs=[part_spec(l) for l in range(DEPTH)] + [own, own, own],
        out_specs=[own] * 4,
        out_shape=[SDS(w.shape, F32)] * 4,
        compiler_params=_cparams(("arbitrary", "arbitrary")),
    )(*parts, w, m, v)


_BIG = ("w_in", "w_out", "a_pw_w", "s5_glu_w")
_CONV = ("a_conv_w", "c_conv_w", "d_conv_w")
_CONV_TAPS = {"a_conv_w": KA, "c_conv_w": KC, "d_conv_w": KD}
_CONV_ROWS = {"a_conv_w": HALO, "c_conv_w": HALO_S, "d_conv_w": HALO_S}
_CONV_WIDTH = {"a_conv_w": BR, "c_conv_w": BR, "d_conv_w": 3 * BR}
_REPLICATED = ("norm_g", "a_conv_b", "a_ln_g", "a_ln_b", "a_pw_b", "s5_lambda_re", "s5_lambda_im", "s5_b_re", "s5_b_im",
               "s5_c_re", "s5_c_im", "s5_d", "s5_log_dt", "s5_glu_b", "d_a_log", "d_dt_bias", "d_norm_g", "final_g")
_WEIGHTS = ("norm_g", "w_in", "a_conv_w", "a_conv_b", "a_ln_g", "a_ln_b", "a_pw_w", "a_pw_b", "s5_lambda_re",
            "s5_lambda_im", "s5_b_re", "s5_b_im", "s5_c_re", "s5_c_im", "s5_d", "s5_log_dt", "s5_glu_w", "s5_glu_b",
            "c_conv_w", "d_conv_w", "d_a_log", "d_dt_bias", "d_norm_g", "w_out", "final_g")


def _size(shape):
    n = 1
    for s in shape:
        n *= s
    return n


PACK_ALIGN = 1024


def _piece_rows(n):
    return -(-n // PACK_ALIGN) * (PACK_ALIGN // 128)


def _pack_rows(pieces, row_mult):
    rows = []
    for p in pieces:
        flat = p.reshape(-1)
        rows.append(jnp.pad(flat, (0, _piece_rows(flat.shape[0]) * 128 - flat.shape[0])).reshape(-1, 128))
    out = jnp.concatenate(rows, axis=0)
    return jnp.pad(out, ((0, (-out.shape[0]) % row_mult), (0, 0)))


def _unpack(packed, shapes):
    out, row = [], 0
    for s in shapes:
        n = _size(s)
        nr = _piece_rows(n)
        out.append(packed[row:row + nr].reshape(-1)[:n].reshape(s))
        row += nr
    return out


_GATHER_VIEWS = [(_whole, _slot)] * 5


def _gather_start(shards, layer, me):
    srcs = [shards[n].astype(BF16) for n in _BIG]
    srcs.append(_pack_rows([shards[n] for n in _CONV], 8))
    lands = [_own_slot(s, me) for s in srcs]
    return _split_start_call("gather_start_%d" % layer, srcs, lands, _GATHER_VIEWS)


def _gather_finish(weights, layer, started, after):
    send, recv, thru, _ = started
    w_in, w_out, a_pw, glu, conv_all = _split_wait_call("gather_wait_%d" % layer, send, recv, thru, _GATHER_VIEWS, after)
    full = {"w_in": _win_gather_layout_call(w_in), "w_out": w_out.reshape(D, D), "a_pw_w": a_pw.reshape(BR, BR),
            "s5_glu_w": glu.reshape(BR, BR)}
    shapes = [weights[n].shape[1:] for n in _CONV]
    per_dev = [_unpack(conv_all[k], shapes) for k in range(N_DEV)]
    for i, n in enumerate(_CONV):
        whole = jnp.concatenate([per_dev[k][i] for k in range(N_DEV)], axis=-1)
        full[n] = jnp.pad(whole, ((0, _CONV_ROWS[n] - _CONV_TAPS[n]), (0, 0)))
    return full


def _rows_view(rows):
    return lambda ref, k: ref.at[pl.ds(k * rows, rows), :]


_SCATTER_VIEWS = [(_slot, _slot), (_rows_view(D // N_DEV), _slot), (_rows_view(BR // N_DEV), _slot),
                  (_rows_view(BR // N_DEV), _slot)]


def _scatter_start(grads, layer, me):
    srcs = [_win_scatter_layout_call(grads["w_in"])] + [grads[n].astype(BF16) for n in _BIG[1:]]
    own = [lax.dynamic_index_in_dim(srcs[0], me, 0, keepdims=False)]
    for s, rows in zip(srcs[1:], (D // N_DEV, BR // N_DEV, BR // N_DEV)):
        own.append(lax.dynamic_slice_in_dim(s, me * rows, rows, axis=0))
    lands = [_own_slot(o, me) for o in own]
    return _split_start_call("scatter_start_%d" % layer, srcs, lands, _SCATTER_VIEWS)


_S5_KERNEL_SHAPES = {"s5_lambda_re": (1, NSTATE), "s5_lambda_im": (1, NSTATE), "s5_log_dt": (1, 16),
                     "s5_b_re": (NSTATE, 16), "s5_b_im": (NSTATE, 16), "s5_c_re": (BR, 64), "s5_c_im": (BR, 64)}
_S5_KEYS = {"s5_lambda_re": "lam_re", "s5_lambda_im": "lam_im", "s5_log_dt": "log_dt", "s5_b_re": "b_re",
            "s5_b_im": "b_im", "s5_c_re": "c_re", "s5_c_im": "c_im"}


def _s5_inputs_all(weights):
    return {n: weights[n].reshape((DEPTH,) + s) for n, s in _S5_KERNEL_SHAPES.items()}


def _s5_inputs(p):
    return {_S5_KEYS[n]: p["s5_in"][n] for n in _S5_KERNEL_SHAPES}


def _row(a, width=None):
    a = a.reshape(1, -1)
    return a if width is None else jnp.pad(a, ((0, 0), (0, width - a.shape[1])))


def _layer_params(p):
    q = dict(p)
    for n in ("norm_g", "a_conv_b", "a_ln_g", "a_ln_b", "a_pw_b", "s5_d", "s5_glu_b", "d_norm_g"):
        q[n] = _row(p[n])
    q["d_a_log"] = _row(p["d_a_log"], 128)
    q["d_dt_bias"] = _row(p["d_dt_bias"], 128)
    return q


def _layer_fwd(x, p, deps=()):
    q = _layer_params(p)
    proj, h = _inproj_call(x, q["norm_g"], q["w_in"], deps)
    ya, yc = _ac_fwd_call(proj, q)
    prep = p["s5_prep"]
    yb, cin_r, cin_i = _s5_fwd_call(proj, prep, q)
    dq, dk, dv, dgb = _dn_pre_fwd_call(proj, q)
    yd, ssave = _dn_core_fwd_call(proj, dq, dk, dv, dgb, q["d_norm_g"])
    x_next = _outproj_call(x, (ya, yb, yc, yd), q["w_out"])
    saved = dict(x=x, proj=proj, h=h, ya=ya, yb=yb, yc=yc, yd=yd, cin_r=cin_r, cin_i=cin_i,
                 q=dq, k=dk, v=dv, gb=dgb, ssave=ssave, prep=prep)
    return x_next, saved


def _layer_bwd(dx, p, sv, deps=(), on_weight_grads=None):
    q = _layer_params(p)
    proj = sv["proj"]
    dya, dyb, dyc, dyd, g_wout = _outproj_bwd_call(dx, (sv["ya"], sv["yb"], sv["yc"], sv["yd"]), q["w_out"], deps)
    dpa, dpc, g_acw, g_acb, g_alg, g_alb, g_apw, g_apb, g_ccw = _ac_bwd_call(proj, q, dya, dyc)
    dpb, *s5g = _s5_bwd_call(proj, sv["prep"], q, sv["cin_r"], sv["cin_i"], dyb)
    g_sd, g_gw, g_gb = s5g[6:]
    g_lre, g_lim, g_ldt, g_bre, g_bim, g_cre, g_cim = _s5_prep_bwd_call(_s5_inputs(p), s5g[:6])
    dq, dk, dv, dgb, dz, g_ng = _dn_core_bwd_call(proj, sv["q"], sv["k"], sv["v"], sv["gb"], q["d_norm_g"], sv["ssave"], dyd)
    dqkv, dab, g_dcw, g_alog, g_dtb = _dn_pre_bwd_call(proj, q, dq, dk, dv, dgb)
    slabs = (dpa, dpb, dpc, dqkv, dz, dab)
    g_win = _inproj_bwd_w_call(sv["h"], slabs)
    big = {"w_in": g_win, "w_out": g_wout, "a_pw_w": g_apw, "s5_glu_w": g_gw}
    tokens = on_weight_grads(big) if on_weight_grads is not None else ()
    dx_prev, g_ng0 = _inproj_bwd_x_call(slabs, q["w_in"], sv["x"], q["norm_g"], dx, tokens)
    grads = {"norm_g": g_ng0, "w_in": g_win, "a_conv_w": g_acw, "a_conv_b": g_acb, "a_ln_g": g_alg, "a_ln_b": g_alb,
             "a_pw_w": g_apw, "a_pw_b": g_apb, "s5_lambda_re": g_lre, "s5_lambda_im": g_lim, "s5_b_re": g_bre,
             "s5_b_im": g_bim, "s5_c_re": g_cre, "s5_c_im": g_cim, "s5_d": g_sd, "s5_log_dt": g_ldt, "s5_glu_w": g_gw,
             "s5_glu_b": g_gb, "c_conv_w": g_ccw, "d_conv_w": g_dcw, "d_a_log": g_alog[:, :NH], "d_dt_bias": g_dtb[:, :NH],
             "d_norm_g": g_ng, "w_out": g_wout}
    return dx_prev, grads


def _step(x, target, weights, moments_m, moments_v):
    me = 4 * lax.axis_index("x") + 2 * lax.axis_index("y") + lax.axis_index("c")
    layer_names = [n for n in _WEIGHTS if n != "final_g"]
    s5_all = _s5_inputs_all(weights)

    sharded = _BIG + _CONV
    gather = _gather_start({n: weights[n][0] for n in sharded}, 0, me)
    preps = [_s5_prep_call({_S5_KEYS[n]: a[l] for n, a in s5_all.items()}, [gather[3]]) for l in range(DEPTH)]
    x_out, after, layers, saved = x, preps[-1][0], [], []
    for l in range(DEPTH):
        full = _gather_finish(weights, l, gather, after)
        deps = ()
        if l + 1 < DEPTH:
            nxt, full["w_out"] = lax.optimization_barrier(({n: weights[n][l + 1] for n in sharded}, full["w_out"]))
            gather = _gather_start(nxt, l + 1, me)
            deps = [gather[3]]
        p = {n: (full[n] if n in full else weights[n][l]) for n in layer_names}
        p["s5_in"] = {n: a[l] for n, a in s5_all.items()}
        p["s5_prep"] = preps[l]
        layers.append(p)
        x_out, sv = _layer_fwd(x_out, p, deps)
        after = x_out
        saved.append(sv)
    dx0, g_final, loss_part = _loss_call(x_out, _row(weights["final_g"]), target)

    per_layer, scatters = [None] * DEPTH, [None] * DEPTH
    for l in range(DEPTH - 1, -1, -1):
        def start(big, l=l):
            scatters[l] = _scatter_start(big, l, me)
            return [scatters[l][3]]
        dx0, per_layer[l] = _layer_bwd(dx0, layers[l], saved[l], [scatters[l + 1][3]] if l + 1 < DEPTH else (), start)
    stack = lambda n: jnp.stack([g[n] for g in per_layer])
    loss = lax.psum(loss_part[0, 0], ("x", "y", "c"))
    results = {}

    parts = [_split_wait_call("scatter_wait_%d" % l, scatters[l][0], scatters[l][1], scatters[l][2], _SCATTER_VIEWS, dx0)
             for l in range(DEPTH - 1, -1, -1)][::-1]
    rows = {"w_in": RELAYOUT_ROWS, "w_out": D // N_DEV, "a_pw_w": BR // N_DEV, "s5_glu_w": BR // N_DEV}
    for i, n in enumerate(_BIG):
        results[n] = _reduce_adamw_layers_call([parts[l][i] for l in range(DEPTH)], weights[n], moments_m[n],
                                               moments_v[n], rows[n], "adamw_" + n)

    grads = {n: stack(n).reshape(weights[n].shape) for n in _REPLICATED if n != "final_g"}
    grads["final_g"] = g_final.reshape(D)
    conv_g = [stack(n) for n in _CONV]
    pack = lambda d: _pack_rows([d[n] for n in _REPLICATED] + [jnp.zeros_like(c) for c in conv_g], 512)
    packed_g = _pack_rows([grads[n] for n in _REPLICATED] + conv_g, 512)
    gathered, = _exchange_call("gather_small_grads", [(packed_g, _whole, (N_DEV,) + packed_g.shape, _slot)])
    res = _reduce_adamw_call(gathered, pack(weights), pack(moments_m), pack(moments_v), (512, 128), "adamw_replicated")
    shapes = [weights[n].shape for n in _REPLICATED] + [c.shape for c in conv_g]
    res = [_unpack(r, shapes) for r in res]
    for i, n in enumerate(_REPLICATED):
        results[n] = tuple(r[i] for r in res)

    own_g = []
    for i, n in enumerate(_CONV):
        width = _CONV_WIDTH[n] // N_DEV
        summed = res[0][len(_REPLICATED) + i][:, :_CONV_TAPS[n], :]
        own_g.append(lax.dynamic_slice_in_dim(summed, me * width, width, axis=2))
    packc = lambda arrs: _pack_rows(arrs, 8)
    res = _reduce_adamw_call(packc(own_g)[None], packc([weights[n] for n in _CONV]), packc([moments_m[n] for n in _CONV]),
                             packc([moments_v[n] for n in _CONV]), packc(own_g).shape, "adamw_conv")
    res = [_unpack(r, [weights[n].shape for n in _CONV]) for r in res]
    for i, n in enumerate(_CONV):
        results[n] = tuple(r[i] for r in res)

    outs = [loss, dx0]
    for kind in range(4):
        outs += [results[n][kind] for n in _WEIGHTS]
    return tuple(outs)


def kernel(x, norm_g, w_in, a_conv_w, a_conv_b, a_ln_g, a_ln_b, a_pw_w, a_pw_b, s5_lambda_re, s5_lambda_im, s5_b_re, s5_b_im, s5_c_re, s5_c_im, s5_d, s5_log_dt, s5_glu_w, s5_glu_b, c_conv_w, d_conv_w, d_a_log, d_dt_bias, d_norm_g, w_out, final_g, loss_target, m_norm_g, m_w_in, m_a_conv_w, m_a_conv_b, m_a_ln_g, m_a_ln_b, m_a_pw_w, m_a_pw_b, m_s5_lambda_re, m_s5_lambda_im, m_s5_b_re, m_s5_b_im, m_s5_c_re, m_s5_c_im, m_s5_d, m_s5_log_dt, m_s5_glu_w, m_s5_glu_b, m_c_conv_w, m_d_conv_w, m_d_a_log, m_d_dt_bias, m_d_norm_g, m_w_out, m_final_g, v_norm_g, v_w_in, v_a_conv_w, v_a_conv_b, v_a_ln_g, v_a_ln_b, v_a_pw_w, v_a_pw_b, v_s5_lambda_re, v_s5_lambda_im, v_s5_b_re, v_s5_b_im, v_s5_c_re, v_s5_c_im, v_s5_d, v_s5_log_dt, v_s5_glu_w, v_s5_glu_b, v_c_conv_w, v_d_conv_w, v_d_a_log, v_d_dt_bias, v_d_norm_g, v_w_out, v_final_g):
    weights = dict(norm_g=norm_g, w_in=w_in, a_conv_w=a_conv_w, a_conv_b=a_conv_b, a_ln_g=a_ln_g, a_ln_b=a_ln_b, a_pw_w=a_pw_w, a_pw_b=a_pw_b, s5_lambda_re=s5_lambda_re, s5_lambda_im=s5_lambda_im, s5_b_re=s5_b_re, s5_b_im=s5_b_im, s5_c_re=s5_c_re, s5_c_im=s5_c_im, s5_d=s5_d, s5_log_dt=s5_log_dt, s5_glu_w=s5_glu_w, s5_glu_b=s5_glu_b, c_conv_w=c_conv_w, d_conv_w=d_conv_w, d_a_log=d_a_log, d_dt_bias=d_dt_bias, d_norm_g=d_norm_g, w_out=w_out, final_g=final_g)
    mom_m = dict(norm_g=m_norm_g, w_in=m_w_in, a_conv_w=m_a_conv_w, a_conv_b=m_a_conv_b, a_ln_g=m_a_ln_g, a_ln_b=m_a_ln_b, a_pw_w=m_a_pw_w, a_pw_b=m_a_pw_b, s5_lambda_re=m_s5_lambda_re, s5_lambda_im=m_s5_lambda_im, s5_b_re=m_s5_b_re, s5_b_im=m_s5_b_im, s5_c_re=m_s5_c_re, s5_c_im=m_s5_c_im, s5_d=m_s5_d, s5_log_dt=m_s5_log_dt, s5_glu_w=m_s5_glu_w, s5_glu_b=m_s5_glu_b, c_conv_w=m_c_conv_w, d_conv_w=m_d_conv_w, d_a_log=m_d_a_log, d_dt_bias=m_d_dt_bias, d_norm_g=m_d_norm_g, w_out=m_w_out, final_g=m_final_g)
    mom_v = dict(norm_g=v_norm_g, w_in=v_w_in, a_conv_w=v_a_conv_w, a_conv_b=v_a_conv_b, a_ln_g=v_a_ln_g, a_ln_b=v_a_ln_b, a_pw_w=v_a_pw_w, a_pw_b=v_a_pw_b, s5_lambda_re=v_s5_lambda_re, s5_lambda_im=v_s5_lambda_im, s5_b_re=v_s5_b_re, s5_b_im=v_s5_b_im, s5_c_re=v_s5_c_re, s5_c_im=v_s5_c_im, s5_d=v_s5_d, s5_log_dt=v_s5_log_dt, s5_glu_w=v_s5_glu_w, s5_glu_b=v_s5_glu_b, c_conv_w=v_c_conv_w, d_conv_w=v_d_conv_w, d_a_log=v_d_a_log, d_dt_bias=v_d_dt_bias, d_norm_g=v_d_norm_g, w_out=v_w_out, final_g=v_final_g)
    outs = _step(x[0], loss_target[0], weights, mom_m, mom_v)
    return (outs[0], outs[1][None]) + outs[2:]
```

```python
import functools

import jax
import jax.numpy as jnp
from jax import lax
from jax.experimental import pallas as pl
from jax.experimental.pallas import tpu as pltpu

F32 = jnp.float32
BF16 = jnp.bfloat16
HI = lax.Precision.HIGHEST
SDS = jax.ShapeDtypeStruct

N_DEV = 8
D = 1024
BR = 256
DEPTH = 4
IN_COLS = 3336
PW = 3456
AB_COL = 3328
EPS = 1e-6
TL = 512
SEG = TL // 8
HALO = 32
HALO_S = 8
KA, KC, KD = 31, 3, 4
CH = 64
DN_GROUP = 4
NH, HD = 4, 64
NSTATE = 1024
VMEM_LIMIT = 56 * 1024 * 1024

ADAM_LR, ADAM_B1, ADAM_B2, ADAM_EPS, ADAM_WD, ADAM_STEP = 0.001, 0.9, 0.999, 1e-08, 0.01, 10

NN = ((1,), (0,))
NT = ((1,), (1,))
TN = ((0,), (0,))


def _dot(a, b, dims, prec=None):
    return lax.dot_general(a, b, (dims, ((), ())), precision=prec, preferred_element_type=F32)


def _make_mm(cast, prec, fwd_dims):
    def prep(t):
        return t.astype(cast) if cast is not None else t

    @jax.custom_vjp
    def mm(a, w):
        return _dot(prep(a), prep(w), fwd_dims, prec)

    def fwd(a, w):
        return mm(a, w), (a, w)

    def bwd(res, dy):
        a, w = res
        a, w, dy = prep(a), prep(w), prep(dy)
        if fwd_dims == NN:
            return _dot(dy, w, NT, prec), _dot(a, dy, TN, prec)
        if fwd_dims == NT:
            return _dot(dy, w, NN, prec), _dot(dy, a, TN, prec)
        return _dot(w, dy, NT, prec), _dot(a, dy, NN, prec)

    mm.defvjp(fwd, bwd)
    return mm


mm = _make_mm(BF16, None, NN)
mm_nt = _make_mm(BF16, None, NT)
mm_tn = _make_mm(BF16, None, TN)
mmh = _make_mm(None, HI, NN)
mmh_nt = _make_mm(None, HI, NT)


def _sigmoid(x):
    return jax.nn.sigmoid(x)


def _silu(x):
    return x * jax.nn.sigmoid(x)


def _gelu(x):
    return 0.5 * x * (1.0 + jnp.tanh(0.7978845608028654 * (x + 0.044715 * (x * x * x))))


def _softplus(x):
    return jnp.maximum(x, 0.0) + jnp.log1p(jnp.exp(-jnp.abs(x)))


def _rms(x, g):
    return x * lax.rsqrt(jnp.mean(x * x, axis=-1, keepdims=True) + EPS) * g


def _cparams(sem):
    return pltpu.CompilerParams(dimension_semantics=sem, vmem_limit_bytes=VMEM_LIMIT)


def _tap_offsets(halo, taps):
    return [halo - (taps - 1) + k for k in range(taps)]


def _conv_fwd_impl(acat, w, tile, halo, taps):
    n = tile + halo
    out = None
    for k, off in enumerate(_tap_offsets(halo, taps)):
        src = jnp.roll(acat, n - off, axis=0)[:tile, :] if off != halo else acat[halo:, :]
        term = src * w[k:k + 1, :]
        out = term if out is None else out + term
    return out


def _make_conv(tile, halo, taps):
    @jax.custom_vjp
    def conv(acat, w):
        return _conv_fwd_impl(acat, w, tile, halo, taps)

    def fwd(acat, w):
        return conv(acat, w), (acat, w)

    def bwd(res, dy):
        acat, w = res
        n = tile + halo
        dyp = jnp.concatenate([dy, jnp.zeros((halo, dy.shape[1]), F32)], axis=0)
        rows = lax.broadcasted_iota(jnp.int32, w.shape, 0)
        dacat = None
        dw = jnp.zeros(w.shape, F32)
        for k, off in enumerate(_tap_offsets(halo, taps)):
            term = jnp.roll(dyp, off, axis=0) * w[k:k + 1, :]
            dacat = term if dacat is None else dacat + term
            src = jnp.roll(acat, n - off, axis=0)[:tile, :] if off != halo else acat[halo:, :]
            dw = dw + jnp.where(rows == k, jnp.sum(dy * src, axis=0, keepdims=True), 0.0)
        return dacat, dw

    conv.defvjp(fwd, bwd)
    return conv


def _halo_spec(tile, halo, width, col):
    per = tile // halo
    return pl.BlockSpec((halo, width), lambda i: (jnp.maximum(i * per - 1, 0), col))


def _halo_spec_rev(nt, tile, halo, width, col):
    per = tile // halo
    return pl.BlockSpec((halo, width), lambda i: (jnp.maximum((nt - 1 - i) * per - 1, 0), col))


def _dep_specs(deps):
    return [pl.BlockSpec((8, 128), lambda *_: (0, 0)) for _ in deps]


def _inproj_call(x, g, w, deps=()):
    L = x.shape[0]

    def body(x_ref, g_ref, w_ref, *rest):
        p_ref, h_ref = rest[len(deps):]
        h = _rms(x_ref[...], g_ref[...]).astype(BF16)
        h_ref[...] = h
        p_ref[...] = _dot(h, w_ref[...], NN)

    return pl.pallas_call(
        body, name="inproj", grid=(L // TL,),
        in_specs=[pl.BlockSpec((TL, D), lambda i: (i, 0)), pl.BlockSpec((1, D), lambda i: (0, 0)),
                  pl.BlockSpec((D, PW), lambda i: (0, 0))] + _dep_specs(deps),
        out_specs=[pl.BlockSpec((TL, PW), lambda i: (i, 0)), pl.BlockSpec((TL, D), lambda i: (i, 0))],
        out_shape=[SDS((L, PW), F32), SDS((L, D), BF16)],
        compiler_params=_cparams(("parallel",)),
    )(x, g, w, *deps)


def _outproj_call(x, ys, w):
    L = x.shape[0]

    def body(x_ref, a_ref, b_ref, c_ref, d_ref, w_ref, o_ref):
        acc = x_ref[...]
        for b, y_ref in enumerate((a_ref, b_ref, c_ref, d_ref)):
            acc = acc + _dot(y_ref[...].astype(BF16), w_ref[b * BR:(b + 1) * BR, :], NN)
        o_ref[...] = acc

    yspec = pl.BlockSpec((TL, BR), lambda i: (i, 0))
    return pl.pallas_call(
        body, name="outproj", grid=(L // TL,),
        in_specs=[pl.BlockSpec((TL, D), lambda i: (i, 0)), yspec, yspec, yspec, yspec,
                  pl.BlockSpec((D, D), lambda i: (0, 0))],
        out_specs=pl.BlockSpec((TL, D), lambda i: (i, 0)),
        out_shape=SDS((L, D), F32),
        compiler_params=_cparams(("parallel",)),
    )(x, *ys, w)


def _loss_call(x, g, target):
    L = x.shape[0]

    def body(x_ref, g_ref, t_ref, dx_ref, dg_ref, loss_ref):
        @pl.when(pl.program_id(0) == 0)
        def _():
            dg_ref[...] = jnp.zeros_like(dg_ref)
            loss_ref[...] = jnp.zeros_like(loss_ref)

        y, vjp = jax.vjp(_rms, x_ref[...], g_ref[...])
        err = y - t_ref[...]
        dx, dg = vjp(err * (1.0 / D))
        dx_ref[...] = dx
        dg_ref[...] += dg
        tot = jnp.sum(jnp.sum(err * err, axis=1, keepdims=True), axis=0, keepdims=True)
        loss_ref[...] += jnp.broadcast_to(tot * (0.5 / D), loss_ref.shape)

    return pl.pallas_call(
        body, name="loss_head", grid=(L // TL,),
        in_specs=[pl.BlockSpec((TL, D), lambda i: (i, 0)), pl.BlockSpec((1, D), lambda i: (0, 0)),
                  pl.BlockSpec((TL, D), lambda i: (i, 0))],
        out_specs=[pl.BlockSpec((TL, D), lambda i: (i, 0)), pl.BlockSpec((1, D), lambda i: (0, 0)),
                   pl.BlockSpec((1, 128), lambda i: (0, 0))],
        out_shape=[SDS((L, D), F32), SDS((1, D), F32), SDS((1, 128), F32)],
        compiler_params=_cparams(("arbitrary",)),
    )(x, g, target)


def _branch_a(valw, gatew, z, cw, cb, lg, lb, pw, pb, conv):
    a = conv(valw * _sigmoid(gatew), cw) + cb
    mu = jnp.mean(a, axis=-1, keepdims=True)
    xc = a - mu
    y = xc * lax.rsqrt(jnp.mean(xc * xc, axis=-1, keepdims=True) + EPS) * lg + lb
    y = mm(_silu(y), pw) + pb
    return y * _silu(z)


def _branch_c(bg, cw_, xw, z, w3, conv):
    return bg * conv(cw_ * xw, w3) * _silu(z)


def _ac_fwd_call(proj, p):
    L = proj.shape[0]

    def body(val, gate, za, hval, hgate, cb_, cc, cx, cz, hcc, hcx,
             acw, acb, alg, alb, apw, apb, ccw, ya_ref, yc_ref):
        nf = (pl.program_id(0) > 0).astype(F32)
        win = lambda h, m: jnp.concatenate([h[...] * nf, m[...]], axis=0)
        conv_a = functools.partial(_conv_fwd_impl, tile=TL, halo=HALO, taps=KA)
        conv_c = functools.partial(_conv_fwd_impl, tile=TL, halo=HALO, taps=KC)
        ya_ref[...] = _branch_a(win(hval, val), win(hgate, gate), za[...], acw[...], acb[...], alg[...], alb[...],
                                apw[...], apb[...], conv_a)
        yc_ref[...] = _branch_c(cb_[...], win(hcc, cc), win(hcx, cx), cz[...], ccw[...], conv_c)

    col = lambda j: pl.BlockSpec((TL, BR), lambda i: (i, j))
    hal = lambda j: _halo_spec(TL, HALO, BR, j)
    full = lambda a: pl.BlockSpec(a.shape, lambda i: (0,) * a.ndim)
    params = (p["a_conv_w"], p["a_conv_b"], p["a_ln_g"], p["a_ln_b"], p["a_pw_w"], p["a_pw_b"], p["c_conv_w"])
    return pl.pallas_call(
        body, name="ac_fwd", grid=(L // TL,),
        in_specs=[col(0), col(1), col(2), hal(0), hal(1), col(5), col(6), col(7), col(8), hal(6), hal(7)]
        + [full(a) for a in params],
        out_specs=[pl.BlockSpec((TL, BR), lambda i: (i, 0))] * 2,
        out_shape=[SDS((L, BR), F32)] * 2,
        compiler_params=_cparams(("parallel",)),
    )(*([proj] * 11), *params)


def _ac_bwd_call(proj, p, dya, dyc):
    L = proj.shape[0]
    nt = L // TL

    def body(val, gate, za, hval, hgate, cb_, cc, cx, cz, hcc, hcx,
             acw, acb, alg, alb, apw, apb, ccw, dya_ref, dyc_ref,
             da_ref, dc_ref, g_acw, g_acb, g_alg, g_alb, g_apw, g_apb, g_ccw, carry):
        i = pl.program_id(0)
        gouts = (g_acw, g_acb, g_alg, g_alb, g_apw, g_apb, g_ccw)

        @pl.when(i == 0)
        def _():
            carry[...] = jnp.zeros_like(carry)
            for r in gouts:
                r[...] = jnp.zeros_like(r)

        nf = (i < nt - 1).astype(F32)
        win = lambda h, m: jnp.concatenate([h[...] * nf, m[...]], axis=0)
        conv_a = _make_conv(TL, HALO, KA)
        conv_c = _make_conv(TL, HALO, KC)

        def f(valw, gatew, z, bg, ccw_, cxw, czv, w1, b1, lg, lb, pw, pb, w3):
            return (_branch_a(valw, gatew, z, w1, b1, lg, lb, pw, pb, conv_a),
                    _branch_c(bg, ccw_, cxw, czv, w3, conv_c))

        _, vjp = jax.vjp(f, win(hval, val), win(hgate, gate), za[...], cb_[...], win(hcc, cc), win(hcx, cx), cz[...],
                         acw[...], acb[...], alg[...], alb[...], apw[...].astype(F32), apb[...], ccw[...])
        (dvalw, dgatew, dz, dbg, dccw, dcxw, dczv, d1, d2, d3, d4, d5, d6, d7) = vjp((dya_ref[...], dyc_ref[...]))

        def settle(slot, dwin):
            tail = jnp.concatenate([jnp.zeros((TL - HALO, BR), F32), carry[slot]], axis=0)
            carry[slot] = dwin[:HALO, :]
            return (dwin[HALO:, :] + tail).astype(BF16)

        da_ref[:, 0:BR] = settle(0, dvalw)
        da_ref[:, BR:2 * BR] = settle(1, dgatew)
        da_ref[:, 2 * BR:3 * BR] = dz.astype(BF16)
        dc_ref[:, 0:BR] = dbg.astype(BF16)
        dc_ref[:, BR:2 * BR] = settle(2, dccw)
        dc_ref[:, 2 * BR:3 * BR] = settle(3, dcxw)
        dc_ref[:, 3 * BR:4 * BR] = dczv.astype(BF16)
        for r, g in zip(gouts, (d1, d2, d3, d4, d5, d6, d7)):
            r[...] += g

    col = lambda j: pl.BlockSpec((TL, BR), lambda i: (nt - 1 - i, j))
    hal = lambda j: _halo_spec_rev(nt, TL, HALO, BR, j)
    full = lambda a: pl.BlockSpec(a.shape, lambda i: (0,) * a.ndim)
    params = (p["a_conv_w"], p["a_conv_b"], p["a_ln_g"], p["a_ln_b"], p["a_pw_w"], p["a_pw_b"], p["c_conv_w"])
    rev = lambda w: pl.BlockSpec((TL, w), lambda i: (nt - 1 - i, 0))
    return pl.pallas_call(
        body, name="ac_bwd", grid=(nt,),
        in_specs=[col(0), col(1), col(2), hal(0), hal(1), col(5), col(6), col(7), col(8), hal(6), hal(7)]
        + [full(a) for a in params] + [rev(BR), rev(BR)],
        out_specs=[rev(3 * BR), rev(4 * BR)] + [full(a) for a in params],
        out_shape=[SDS((L, 3 * BR), BF16), SDS((L, 4 * BR), BF16)] + [SDS(a.shape, F32) for a in params],
        scratch_shapes=[pltpu.VMEM((4, HALO, BR), F32)],
        compiler_params=_cparams(("arbitrary",)),
    )(*([proj] * 11), *params, dya, dyc)


def _iota2(shape, dim):
    return lax.broadcasted_iota(jnp.int32, shape, dim)


def _s5_params(lam_re, lam_im, logdt, b_re, b_im, c_re, c_im):
    eg = (_iota2((16, NSTATE), 1) >> 6 == _iota2((16, NSTATE), 0)).astype(F32)
    dt = jnp.exp(mmh(jnp.broadcast_to(logdt, (8, 16)), eg)[0:1, :])
    lr = jnp.minimum(lam_re, -1e-4)
    li = lam_im
    mag = jnp.exp(lr * dt)
    lbr = mag * jnp.cos(li * dt)
    lbi = mag * jnp.sin(li * dt)
    den = lr * lr + li * li
    nr = lbr - 1.0
    fr = (nr * lr + lbi * li) / den
    fi = (lbi * lr - nr * li) / den
    row = _iota2((8, NSTATE), 0)
    f8 = jnp.where(row == 0, fr, jnp.where(row == 1, fi, 0.0))
    eye = (_iota2((NSTATE, NSTATE), 0) == _iota2((NSTATE, NSTATE), 1)).astype(F32)
    fcol = mmh_nt(eye, f8)
    frc, fic = fcol[:, 0:1], fcol[:, 1:2]
    bbr = frc * b_re - fic * b_im
    bbi = frc * b_im + fic * b_re
    e1 = ((_iota2((16, BR), 1) & 15) == _iota2((16, BR), 0)).astype(F32)
    m1 = ((_iota2((NSTATE, BR), 0) >> 6) == (_iota2((NSTATE, BR), 1) >> 4)).astype(F32)
    wbr = mmh(bbr, e1) * m1
    wbi = mmh(bbi, e1) * m1
    e2 = ((_iota2((64, NSTATE), 1) & 63) == _iota2((64, NSTATE), 0)).astype(F32)
    m2 = ((_iota2((BR, NSTATE), 0) >> 4) == (_iota2((BR, NSTATE), 1) >> 6)).astype(F32)
    wcr = mmh(c_re, e2) * m2
    wci = mmh(c_im, e2) * m2
    return lbr, lbi, wbr, wbi, wcr, wci


_S5_OUT = [(1, NSTATE), (1, NSTATE), (NSTATE, BR), (NSTATE, BR), (BR, NSTATE), (BR, NSTATE)]


def _s5_prep_call(sp, deps=()):
    def body(lre, lim, ldt, bre, bim, cre, cim, *rest):
        o_lbr, o_lbi, o_wbr, o_wbi, o_wcr, o_wci, pwr, pwi, qwr, qwi = rest[len(deps):]
        lbr, lbi, wbr, wbi, wcr, wci = _s5_params(lre[...], lim[...], ldt[...], bre[...], bim[...], cre[...], cim[...])
        o_lbr[...], o_lbi[...], o_wbr[...], o_wbi[...], o_wcr[...], o_wci[...] = lbr, lbi, wbr, wbi, wcr, wci
        pr, pi = lbr, lbi
        for i in range(SEG):
            pwr[i:i + 1, :] = pr
            pwi[i:i + 1, :] = pi
            qwr[SEG - 1 - i:SEG - i, :] = pr
            qwi[SEG - 1 - i:SEG - i, :] = -pi
            pr, pi = pr * lbr - pi * lbi, pr * lbi + pi * lbr

    args = (sp["lam_re"], sp["lam_im"], sp["log_dt"], sp["b_re"], sp["b_im"], sp["c_re"], sp["c_im"])
    return pl.pallas_call(
        body, name="s5_prep",
        out_shape=[SDS(s, F32) for s in _S5_OUT] + [SDS((SEG, NSTATE), F32)] * 4,
        compiler_params=pltpu.CompilerParams(vmem_limit_bytes=VMEM_LIMIT),
    )(*args, *deps)


def _s5_prep_bwd_call(sp, cots):
    def body(lre, lim, ldt, bre, bim, cre, cim, c0, c1, c2, c3, c4, c5, *outs):
        _, vjp = jax.vjp(_s5_params, lre[...], lim[...], ldt[...], bre[...], bim[...], cre[...], cim[...])
        grads = vjp((c0[...], c1[...], c2[...], c3[...], c4[...], c5[...]))
        for o, g in zip(outs, grads):
            o[...] = g

    args = (sp["lam_re"], sp["lam_im"], sp["log_dt"], sp["b_re"], sp["b_im"], sp["c_re"], sp["c_im"])
    return pl.pallas_call(
        body, name="s5_prep_bwd",
        out_shape=[SDS(a.shape, F32) for a in args],
        compiler_params=pltpu.CompilerParams(vmem_limit_bytes=VMEM_LIMIT),
    )(*args, *cots)


def _lanes(v, j):
    return v[:, j * 128:(j + 1) * 128]


def _s5_scan(sre, sim, pwr, pwi, cin_r, cin_i, reverse):
    row = _iota2((8, 128), 0)
    steps = (1, 2, 4)

    def lane_consts(j):
        lanes = slice(j * 128, (j + 1) * 128)
        if reverse:
            mult = [(jnp.broadcast_to(pwr[SEG - d:SEG - d + 1, lanes], (8, 128)),
                     jnp.broadcast_to(pwi[SEG - d:SEG - d + 1, lanes], (8, 128))) for d in steps]
            return mult, pwr[SEG - 8:SEG, lanes], pwi[SEG - 8:SEG, lanes]
        mult = [(jnp.broadcast_to(pwr[d - 1:d, lanes], (8, 128)),
                 jnp.broadcast_to(pwi[d - 1:d, lanes], (8, 128))) for d in steps]
        return mult, pwr[0:8, lanes], pwi[0:8, lanes]

    consts = [lane_consts(j) for j in range(8)]
    nblk = TL // 8

    def block(t, carry):
        b = nblk - 1 - t if reverse else t
        rows = pl.ds(pl.multiple_of(b * 8, 8), 8)
        new = []
        for j in range(8):
            mult, p8r, p8i = consts[j]
            vr, vi = sre.at[j], sim.at[j]
            sr, si = vr[rows, :], vi[rows, :]
            for d, (mr, mi) in zip(steps, mult):
                if reverse:
                    hr = jnp.where(row < 8 - d, pltpu.roll(sr, 8 - d, 0), 0.0)
                    hi = jnp.where(row < 8 - d, pltpu.roll(si, 8 - d, 0), 0.0)
                else:
                    hr = jnp.where(row >= d, pltpu.roll(sr, d, 0), 0.0)
                    hi = jnp.where(row >= d, pltpu.roll(si, d, 0), 0.0)
                sr, si = sr + mr * hr - mi * hi, si + mr * hi + mi * hr
            cr, ci = carry[2 * j], carry[2 * j + 1]
            sr, si = sr + p8r * cr - p8i * ci, si + p8r * ci + p8i * cr
            vr[rows, :] = sr
            vi[rows, :] = si
            edge = slice(0, 1) if reverse else slice(7, 8)
            new += [sr[edge, :], si[edge, :]]
        return tuple(new)

    init = []
    for j in range(8):
        init += [_lanes(cin_r, j), _lanes(cin_i, j)]
    ends = lax.fori_loop(0, nblk, block, tuple(init))
    return (jnp.concatenate([ends[2 * j] for j in range(8)], axis=1),
            jnp.concatenate([ends[2 * j + 1] for j in range(8)], axis=1))


def _bdot(a, b, dims):
    return _dot(a.astype(BF16), b.astype(BF16), dims)


def _s5_states(u, wbr, wbi, sre, sim):
    bur = _bdot(u, wbr, NT)
    bui = _bdot(u, wbi, NT)
    for j in range(8):
        sre[j] = _lanes(bur, j)
        sim[j] = _lanes(bui, j)


def _gather_lanes(s):
    return jnp.concatenate([s[j] for j in range(8)], axis=1)


def _s5_post(s_re, s_im, u, z, wcr, wci, dsk, gw, gb):
    y = mm_nt(s_re, wcr) - mm_nt(s_im, wci) + dsk * u
    yg = _gelu(y)
    return yg * _sigmoid(mm(yg, gw) + gb) * _silu(z)


def _s5_fwd_call(proj, prep, p):
    L = proj.shape[0]
    nt = L // TL
    lbr, lbi, wbr, wbi, wcr, wci, pwr, pwi, _, _ = prep

    def body(u_ref, z_ref, lbr_r, lbi_r, wbr_r, wbi_r, wcr_r, wci_r, pwr_r, pwi_r, d_r, gw_r, gb_r,
             yb_ref, cinr_ref, cini_ref, sre, sim, car, cai):
        @pl.when(pl.program_id(0) == 0)
        def _():
            car[...] = jnp.zeros_like(car)
            cai[...] = jnp.zeros_like(cai)

        u = u_ref[...]
        cinr_ref[0] = car[...]
        cini_ref[0] = cai[...]
        _s5_states(u, wbr_r[...], wbi_r[...], sre, sim)
        nr, ni = _s5_scan(sre, sim, pwr_r, pwi_r, car[...], cai[...], False)
        car[...] = nr
        cai[...] = ni
        yb_ref[...] = _s5_post(_gather_lanes(sre), _gather_lanes(sim), u, z_ref[...], wcr_r[...], wci_r[...],
                               d_r[...], gw_r[...], gb_r[...])

    full = lambda a: pl.BlockSpec(a.shape, lambda i: (0,) * a.ndim)
    consts = (lbr, lbi, wbr, wbi, wcr, wci, pwr, pwi, p["s5_d"], p["s5_glu_w"], p["s5_glu_b"])
    cspec = pl.BlockSpec((1, 1, NSTATE), lambda i: (i, 0, 0))
    return pl.pallas_call(
        body, name="s5_fwd", grid=(nt,),
        in_specs=[pl.BlockSpec((TL, BR), lambda i: (i, 3)), pl.BlockSpec((TL, BR), lambda i: (i, 4))]
        + [full(a) for a in consts],
        out_specs=[pl.BlockSpec((TL, BR), lambda i: (i, 0)), cspec, cspec],
        out_shape=[SDS((L, BR), F32), SDS((nt, 1, NSTATE), F32), SDS((nt, 1, NSTATE), F32)],
        scratch_shapes=[pltpu.VMEM((8, TL, 128), F32), pltpu.VMEM((8, TL, 128), F32),
                        pltpu.VMEM((1, NSTATE), F32), pltpu.VMEM((1, NSTATE), F32)],
        compiler_params=_cparams(("arbitrary",)),
    )(proj, proj, *consts)


def _s5_bwd_call(proj, prep, p, cin_r, cin_i, dyb):
    L = proj.shape[0]
    nt = L // TL
    lbr, lbi, wbr, wbi, wcr, wci, pwr, pwi, qwr, qwi = prep

    def body(u_ref, z_ref, lbr_r, lbi_r, wbr_r, wbi_r, wcr_r, wci_r, pwr_r, pwi_r, qwr_r, qwi_r, d_r, gw_r, gb_r,
             cinr_ref, cini_ref, dy_ref,
             db_ref, g_lbr, g_lbi, g_wbr, g_wbi, g_wcr, g_wci, g_d, g_gw, g_gb, sre, sim, gre, gim, car, cai):
        gouts = (g_lbr, g_lbi, g_wbr, g_wbi, g_wcr, g_wci, g_d, g_gw, g_gb)

        @pl.when(pl.program_id(0) == 0)
        def _():
            car[...] = jnp.zeros_like(car)
            cai[...] = jnp.zeros_like(cai)
            for r in gouts:
                r[...] = jnp.zeros_like(r)

        u = u_ref[...]
        lr, li = lbr_r[...], lbi_r[...]
        c0r, c0i = cinr_ref[0], cini_ref[0]
        _s5_states(u, wbr_r[...], wbi_r[...], sre, sim)
        _s5_scan(sre, sim, pwr_r, pwi_r, c0r, c0i, False)
        s_re, s_im = _gather_lanes(sre), _gather_lanes(sim)
        _, vjp = jax.vjp(_s5_post, s_re, s_im, u, z_ref[...], wcr_r[...], wci_r[...], d_r[...],
                         gw_r[...].astype(F32), gb_r[...])
        ds_re, ds_im, du, dz, dwcr, dwci, dd, dgw, dgb = vjp(dy_ref[...])
        for j in range(8):
            gre[j] = _lanes(ds_re, j)
            gim[j] = _lanes(ds_im, j)
        nr, ni = _s5_scan(gre, gim, qwr_r, qwi_r, car[...], cai[...], True)
        car[...] = nr
        cai[...] = ni
        a_re, a_im = _gather_lanes(gre), _gather_lanes(gim)
        first = _iota2((TL, NSTATE), 0) == 0
        p_re = jnp.where(first, c0r, jnp.roll(s_re, 1, axis=0))
        p_im = jnp.where(first, c0i, jnp.roll(s_im, 1, axis=0))
        g_lbr[...] += jnp.sum(a_re * p_re + a_im * p_im, axis=0, keepdims=True)
        g_lbi[...] += jnp.sum(a_im * p_re - a_re * p_im, axis=0, keepdims=True)
        du = du + _bdot(a_re, wbr_r[...], NN) + _bdot(a_im, wbi_r[...], NN)
        g_wbr[...] += _bdot(a_re, u, TN)
        g_wbi[...] += _bdot(a_im, u, TN)
        g_wcr[...] += dwcr
        g_wci[...] += dwci
        g_d[...] += dd
        g_gw[...] += dgw
        g_gb[...] += dgb
        db_ref[:, 0:BR] = du.astype(BF16)
        db_ref[:, BR:2 * BR] = dz.astype(BF16)

    full = lambda a: pl.BlockSpec(a.shape, lambda i: (0,) * a.ndim)
    consts = (lbr, lbi, wbr, wbi, wcr, wci, pwr, pwi, qwr, qwi, p["s5_d"], p["s5_glu_w"], p["s5_glu_b"])
    cspec = pl.BlockSpec((1, 1, NSTATE), lambda i: (nt - 1 - i, 0, 0))
    gshapes = _S5_OUT + [(1, BR), (BR, BR), (1, BR)]
    return pl.pallas_call(
        body, name="s5_bwd", grid=(nt,),
        in_specs=[pl.BlockSpec((TL, BR), lambda i: (nt - 1 - i, 3)), pl.BlockSpec((TL, BR), lambda i: (nt - 1 - i, 4))]
        + [full(a) for a in consts] + [cspec, cspec, pl.BlockSpec((TL, BR), lambda i: (nt - 1 - i, 0))],
        out_specs=[pl.BlockSpec((TL, 2 * BR), lambda i: (nt - 1 - i, 0))]
        + [pl.BlockSpec(s, lambda i: (0, 0)) for s in gshapes],
        out_shape=[SDS((L, 2 * BR), BF16)] + [SDS(s, F32) for s in gshapes],
        scratch_shapes=[pltpu.VMEM((8, TL, 128), F32)] * 4 + [pltpu.VMEM((1, NSTATE), F32)] * 2,
        compiler_params=_cparams(("arbitrary",)),
    )(proj, proj, *consts, cin_r, cin_i, dyb)


def _heads(x):
    return [x[:, h * HD:(h + 1) * HD] for h in range(NH)]


def _l2n(x, scale):
    return jnp.concatenate([xh * (lax.rsqrt(jnp.sum(xh * xh, axis=-1, keepdims=True) + EPS) * scale)
                            for xh in _heads(x)], axis=1)


def _dn_pre(qkvw, ab, cw, alog, dtb, conv, rows):
    c = _silu(conv(qkvw, cw))
    q = _l2n(c[:, 0:BR], HD ** -0.5)
    k = _l2n(c[:, BR:2 * BR], 1.0)
    v = c[:, 2 * BR:3 * BR]
    g = -jnp.exp(alog) * _softplus(ab + dtb)
    ri, ci = _iota2((rows, rows), 0), _iota2((rows, rows), 1)
    tri = ((ri >= ci) & ((ri >> 6) == (ci >> 6))).astype(F32)
    gc = mmh(tri, g)
    lane = _iota2(ab.shape, 1)
    return q, k, v, jnp.where(lane < NH, gc, jnp.where(lane < 2 * NH, _sigmoid(ab), 0.0))


def _dn_pre_fwd_call(proj, p):
    L = proj.shape[0]

    def body(m_ref, h_ref, ab_ref, cw, alog, dtb, q_ref, k_ref, v_ref, gb_ref):
        nf = (pl.program_id(0) > 0).astype(F32)
        qkvw = jnp.concatenate([h_ref[...] * nf, m_ref[...]], axis=0)
        conv = functools.partial(_conv_fwd_impl, tile=TL, halo=HALO_S, taps=KD)
        q_ref[...], k_ref[...], v_ref[...], gb_ref[...] = _dn_pre(qkvw, ab_ref[...], cw[...], alog[...], dtb[...], conv, TL)

    full = lambda a: pl.BlockSpec(a.shape, lambda i: (0,) * a.ndim)
    params = (p["d_conv_w"], p["d_a_log"], p["d_dt_bias"])
    o = pl.BlockSpec((TL, BR), lambda i: (i, 0))
    return pl.pallas_call(
        body, name="dn_pre_fwd", grid=(L // TL,),
        in_specs=[pl.BlockSpec((TL, 3 * BR), lambda i: (i, 3)), _halo_spec(TL, HALO_S, 3 * BR, 3),
                  pl.BlockSpec((TL, 128), lambda i: (i, AB_COL // 128))] + [full(a) for a in params],
        out_specs=[o, o, o, pl.BlockSpec((TL, 128), lambda i: (i, 0))],
        out_shape=[SDS((L, BR), F32)] * 3 + [SDS((L, 128), F32)],
        compiler_params=_cparams(("parallel",)),
    )(proj, proj, proj, *params)


def _dn_pre_bwd_call(proj, p, dq, dk, dv, dgb):
    L = proj.shape[0]
    nt = L // TL

    def body(m_ref, h_ref, ab_ref, cw, alog, dtb, dq_r, dk_r, dv_r, dgb_r,
             dqkv_ref, dab_ref, g_cw, g_alog, g_dtb, carry):
        i = pl.program_id(0)

        @pl.when(i == 0)
        def _():
            carry[...] = jnp.zeros_like(carry)
            for r in (g_cw, g_alog, g_dtb):
                r[...] = jnp.zeros_like(r)

        nf = (i < nt - 1).astype(F32)
        qkvw = jnp.concatenate([h_ref[...] * nf, m_ref[...]], axis=0)
        conv = _make_conv(TL, HALO_S, KD)
        _, vjp = jax.vjp(lambda a, b, c, d, e: _dn_pre(a, b, c, d, e, conv, TL),
                         qkvw, ab_ref[...], cw[...], alog[...], dtb[...])
        dwin, dab, dcw, dalog, ddtb = vjp((dq_r[...], dk_r[...], dv_r[...], dgb_r[...]))
        tail = jnp.concatenate([jnp.zeros((TL - HALO_S, 3 * BR), F32), carry[...]], axis=0)
        carry[...] = dwin[:HALO_S, :]
        dqkv_ref[...] = (dwin[HALO_S:, :] + tail).astype(BF16)
        dab_ref[...] = dab.astype(BF16)
        g_cw[...] += dcw
        g_alog[...] += dalog
        g_dtb[...] += ddtb

    full = lambda a: pl.BlockSpec(a.shape, lambda i: (0,) * a.ndim)
    params = (p["d_conv_w"], p["d_a_log"], p["d_dt_bias"])
    rev = lambda w: pl.BlockSpec((TL, w), lambda i: (nt - 1 - i, 0))
    return pl.pallas_call(
        body, name="dn_pre_bwd", grid=(nt,),
        in_specs=[pl.BlockSpec((TL, 3 * BR), lambda i: (nt - 1 - i, 3)), _halo_spec_rev(nt, TL, HALO_S, 3 * BR, 3),
                  pl.BlockSpec((TL, 128), lambda i: (nt - 1 - i, AB_COL // 128))] + [full(a) for a in params]
        + [rev(BR), rev(BR), rev(BR), rev(128)],
        out_specs=[rev(3 * BR), rev(128)] + [full(a) for a in params],
        out_shape=[SDS((L, 3 * BR), BF16), SDS((L, 128), BF16)] + [SDS(a.shape, F32) for a in params],
        scratch_shapes=[pltpu.VMEM((HALO_S, 3 * BR), F32)],
        compiler_params=_cparams(("arbitrary",)),
    )(proj, proj, proj, *params, dq, dk, dv, dgb)


def _dn_group(q, k, v, gb, z, ng, *s):
    ri, ci = _iota2((CH, CH), 0), _iota2((CH, CH), 1)
    causal, strict = ri >= ci, ri > ci
    eye = (ri == ci).astype(F32)
    s = list(s)
    pairs = []
    for c in range(DN_GROUP):
        rows = slice(c * CH, (c + 1) * CH)
        gbc = gb[rows, :]
        for h, (qh, kh, vh, zh) in enumerate(zip(_heads(q[rows, :]), _heads(k[rows, :]), _heads(v[rows, :]),
                                                 _heads(z[rows, :]))):
            gc = jnp.broadcast_to(gbc[:, h:h + 1], (CH, HD))
            beta = gbc[:, NH + h:NH + h + 1]
            decay = jnp.where(causal, jnp.exp(jnp.where(causal, gc - gc.T, 0.0)), 0.0)
            egc = jnp.exp(gc)
            glast = gc[CH - 1:CH, :]
            kb = kh * beta
            pairs.append(dict(q=qh, k=kh, z=zh, decay=decay, qe=qh * egc, kd=kh * jnp.exp(glast - gc),
                              sdec=jnp.exp(glast[:, 0:1]), kb=kb, rhs=jnp.concatenate([vh * beta, kb * egc], axis=1)))
    for p in pairs:
        p["pw"] = jnp.where(strict, mm_nt(p["kb"], p["k"]) * p["decay"], 0.0)
    for p in pairs:
        p["t"] = eye - p["pw"]
    for _ in range(5):
        for p in pairs:
            p["pw"] = mm(p["pw"], p["pw"])
        for p in pairs:
            p["t"] = mm(p["t"], eye + p["pw"])
    for p in pairs:
        p["uw"] = mm(p["t"], p["rhs"])
    for p in pairs:
        p["attn"] = mm_nt(p["q"], p["k"]) * p["decay"]
    out_rows = []
    for c in range(DN_GROUP):
        grp = pairs[c * NH:(c + 1) * NH]
        ws = [mm(jnp.concatenate([p["uw"][:, HD:], p["qe"]], axis=0), s[h]) for h, p in enumerate(grp)]
        v_new = [p["uw"][:, :HD] - w_[:CH, :] for p, w_ in zip(grp, ws)]
        o = [w_[CH:, :] + mm(p["attn"], vn) for p, w_, vn in zip(grp, ws, v_new)]
        s = [s[h] * p["sdec"] + mm_tn(p["kd"], vn) for h, (p, vn) in enumerate(zip(grp, v_new))]
        o = [oh * lax.rsqrt(jnp.mean(oh * oh, axis=-1, keepdims=True) + EPS) * ng * _silu(p["z"]) for oh, p in zip(o, grp)]
        out_rows.append(jnp.concatenate(o, axis=1))
    return (jnp.concatenate(out_rows, axis=0), *s)


def _dn_core_fwd_call(proj, q, k, v, gb, ng):
    L = q.shape[0]
    rows = DN_GROUP * CH
    ng_ = L // rows

    def body(q_r, k_r, v_r, gb_r, z_r, ng_r, yd_ref, ssave_ref, s_scr):
        @pl.when(pl.program_id(0) == 0)
        def _():
            s_scr[...] = jnp.zeros_like(s_scr)

        ssave_ref[0] = s_scr[...]
        yd, *s2 = _dn_group(q_r[...], k_r[...], v_r[...], gb_r[...], z_r[...], ng_r[...], *[s_scr[h] for h in range(NH)])
        yd_ref[...] = yd
        for h in range(NH):
            s_scr[h] = s2[h]

    c = pl.BlockSpec((rows, BR), lambda i: (i, 0))
    return pl.pallas_call(
        body, name="dn_core_fwd", grid=(ng_,),
        in_specs=[c, c, c, pl.BlockSpec((rows, 128), lambda i: (i, 0)), pl.BlockSpec((rows, BR), lambda i: (i, 12)),
                  pl.BlockSpec((1, HD), lambda i: (0, 0))],
        out_specs=[c, pl.BlockSpec((1, NH, HD, HD), lambda i: (i, 0, 0, 0))],
        out_shape=[SDS((L, BR), F32), SDS((ng_, NH, HD, HD), F32)],
        scratch_shapes=[pltpu.VMEM((NH, HD, HD), F32)],
        compiler_params=_cparams(("arbitrary",)),
    )(q, k, v, gb, proj, ng)


def _dn_core_bwd_call(proj, q, k, v, gb, ng, ssave, dyd):
    L = q.shape[0]
    rows = DN_GROUP * CH
    ng_ = L // rows

    def body(q_r, k_r, v_r, gb_r, z_r, ng_r, s_r, dy_r, dq_ref, dk_ref, dv_ref, dgb_ref, dz_ref, g_ng, ds_scr):
        @pl.when(pl.program_id(0) == 0)
        def _():
            ds_scr[...] = jnp.zeros_like(ds_scr)
            g_ng[...] = jnp.zeros_like(g_ng)

        _, vjp = jax.vjp(_dn_group, q_r[...], k_r[...], v_r[...], gb_r[...], z_r[...], ng_r[...],
                         *[s_r[0, h] for h in range(NH)])
        dq, dk, dv, dgb, dz, dng, *ds = vjp((dy_r[...], *[ds_scr[h] for h in range(NH)]))
        dq_ref[...], dk_ref[...], dv_ref[...], dgb_ref[...] = dq, dk, dv, dgb
        dz_ref[...] = dz.astype(BF16)
        g_ng[...] += dng
        for h in range(NH):
            ds_scr[h] = ds[h]

    c = pl.BlockSpec((rows, BR), lambda i: (ng_ - 1 - i, 0))
    c128 = pl.BlockSpec((rows, 128), lambda i: (ng_ - 1 - i, 0))
    return pl.pallas_call(
        body, name="dn_core_bwd", grid=(ng_,),
        in_specs=[c, c, c, c128, pl.BlockSpec((rows, BR), lambda i: (ng_ - 1 - i, 12)),
                  pl.BlockSpec((1, HD), lambda i: (0, 0)),
                  pl.BlockSpec((1, NH, HD, HD), lambda i: (ng_ - 1 - i, 0, 0, 0)), c],
        out_specs=[c, c, c, c128, c, pl.BlockSpec((1, HD), lambda i: (0, 0))],
        out_shape=[SDS((L, BR), F32)] * 3 + [SDS((L, 128), F32), SDS((L, BR), BF16), SDS((1, HD), F32)],
        scratch_shapes=[pltpu.VMEM((NH, HD, HD), F32)],
        compiler_params=_cparams(("arbitrary",)),
    )(q, k, v, gb, proj, ng, ssave, dyd)


def _outproj_bwd_call(dx, ys, w, deps=()):
    L = dx.shape[0]

    def body(dx_ref, a_ref, b_ref, c_ref, d_ref, w_ref, *rest):
        da, db, dc, dd, dw_ref = rest[len(deps):]

        @pl.when(pl.program_id(0) == 0)
        def _():
            dw_ref[...] = jnp.zeros_like(dw_ref)

        dxb = dx_ref[...].astype(BF16)
        for b, (y_ref, o_ref) in enumerate(zip((a_ref, b_ref, c_ref, d_ref), (da, db, dc, dd))):
            o_ref[...] = _dot(dxb, w_ref[b * BR:(b + 1) * BR, :], NT)
            dw_ref[b * BR:(b + 1) * BR, :] += _dot(y_ref[...].astype(BF16), dxb, TN)

    yspec = pl.BlockSpec((TL, BR), lambda i: (i, 0))
    return pl.pallas_call(
        body, name="outproj_bwd", grid=(L // TL,),
        in_specs=[pl.BlockSpec((TL, D), lambda i: (i, 0)), yspec, yspec, yspec, yspec,
                  pl.BlockSpec((D, D), lambda i: (0, 0))] + _dep_specs(deps),
        out_specs=[yspec] * 4 + [pl.BlockSpec((D, D), lambda i: (0, 0))],
        out_shape=[SDS((L, BR), F32)] * 4 + [SDS((D, D), F32)],
        compiler_params=_cparams(("arbitrary",)),
    )(dx, *ys, w, *deps)


def _slab_cols(slabs):
    widths = [s.shape[1] for s in slabs]
    starts = [sum(widths[:i]) for i in range(len(widths))]
    assert starts[-1] + widths[-1] == PW
    return list(zip(starts, widths))


def _inproj_bwd_x_call(slabs, w, x, g, dx_next, deps=()):
    L = x.shape[0]
    cols = _slab_cols(slabs)
    n = len(slabs)

    def body(*refs):
        dp_refs, (w_ref, x_ref, g_ref, dxn_ref) = refs[:n], refs[n:n + 4]
        dx_ref, dg_ref = refs[n + 4 + len(deps):]

        @pl.when(pl.program_id(0) == 0)
        def _():
            dg_ref[...] = jnp.zeros_like(dg_ref)

        dh = None
        for dp_ref, (c0, cw) in zip(dp_refs, cols):
            part = _dot(dp_ref[...], w_ref[:, c0:c0 + cw], NT)
            dh = part if dh is None else dh + part
        _, vjp = jax.vjp(_rms, x_ref[...], g_ref[...])
        dx, dg = vjp(dh)
        dx_ref[...] = dx + dxn_ref[...]
        dg_ref[...] += dg

    row = lambda w_: pl.BlockSpec((TL, w_), lambda i: (i, 0))
    return pl.pallas_call(
        body, name="inproj_bwd_x", grid=(L // TL,),
        in_specs=[row(cw) for _, cw in cols]
        + [pl.BlockSpec((D, PW), lambda i: (0, 0)), row(D), pl.BlockSpec((1, D), lambda i: (0, 0)), row(D)]
        + _dep_specs(deps),
        out_specs=[row(D), pl.BlockSpec((1, D), lambda i: (0, 0))],
        out_shape=[SDS((L, D), F32), SDS((1, D), F32)],
        compiler_params=_cparams(("arbitrary",)),
    )(*slabs, w, x, g, dx_next, *deps)


def _inproj_bwd_w_call(h, slabs):
    L = h.shape[0]
    cols = _slab_cols(slabs)
    n = len(slabs)

    def body(*refs):
        h_ref, dp_refs, dw_ref = refs[0], refs[1:1 + n], refs[1 + n]

        @pl.when(pl.program_id(0) == 0)
        def _():
            dw_ref[...] = jnp.zeros_like(dw_ref)

        hv = h_ref[...]
        for dp_ref, (c0, cw) in zip(dp_refs, cols):
            dw_ref[:, c0:c0 + cw] += _dot(hv, dp_ref[...], TN)

    row = lambda w_: pl.BlockSpec((TL, w_), lambda i: (i, 0))
    return pl.pallas_call(
        body, name="inproj_bwd_w", grid=(L // TL,),
        in_specs=[row(D)] + [row(cw) for _, cw in cols],
        out_specs=pl.BlockSpec((D, PW), lambda i: (0, 0)),
        out_shape=SDS((D, PW), F32),
        compiler_params=_cparams(("arbitrary",)),
    )(h, *slabs)


def _exchange_call(name, flows):
    n = len(flows)

    def body(*refs):
        srcs, dsts = refs[:n], refs[n:2 * n]
        send_sems, recv_sems, local_sems = refs[2 * n:]
        x, y, c = lax.axis_index("x"), lax.axis_index("y"), lax.axis_index("c")
        me = 4 * x + 2 * y + c
        copies = []
        for mask in range(1, N_DEV):
            px = 1 - x if mask & 4 else x
            py = 1 - y if mask & 2 else y
            pc = 1 - c if mask & 1 else c
            for f, (_, src_at, _, dst_at) in enumerate(flows):
                cp = pltpu.make_async_remote_copy(
                    src_ref=src_at(srcs[f], 4 * px + 2 * py + pc), dst_ref=dst_at(dsts[f], me),
                    send_sem=send_sems.at[mask - 1, f], recv_sem=recv_sems.at[mask - 1, f],
                    device_id=(px, py, pc), device_id_type=pl.DeviceIdType.MESH)
                cp.start()
                copies.append(cp)
        mine = [pltpu.make_async_copy(src_at(srcs[f], me), dst_at(dsts[f], me), local_sems.at[f])
                for f, (_, src_at, _, dst_at) in enumerate(flows)]
        for cp in mine:
            cp.start()
        for cp in copies + mine:
            cp.wait()

    return pl.pallas_call(
        body, name=name,
        in_specs=[pl.BlockSpec(memory_space=pl.ANY)] * n,
        out_specs=[pl.BlockSpec(memory_space=pl.ANY)] * n,
        out_shape=[SDS(tuple(shape), src.dtype) for src, _, shape, _ in flows],
        scratch_shapes=[pltpu.SemaphoreType.DMA((N_DEV - 1, n)), pltpu.SemaphoreType.DMA((N_DEV - 1, n)),
                        pltpu.SemaphoreType.DMA((n,))],
    )(*[f[0] for f in flows])


def _whole(ref, _):
    return ref


def _slot(ref, k):
    return ref.at[k]


_HBM_SPEC = pl.BlockSpec(memory_space=pltpu.HBM)
_SEM_SPEC = pl.BlockSpec(memory_space=pltpu.SEMAPHORE)
_DATAFLOW = pltpu.SideEffectType.DATAFLOW_SIDE_EFFECTING


def _split_copies(views, src_refs, land_refs, send_sems, recv_sems):
    x, y, c = lax.axis_index("x"), lax.axis_index("y"), lax.axis_index("c")
    me = 4 * x + 2 * y + c
    copies = []
    for mask in range(1, N_DEV):
        px = 1 - x if mask & 4 else x
        py = 1 - y if mask & 2 else y
        pc = 1 - c if mask & 1 else c
        for f, (src_at, dst_at) in enumerate(views):
            pair = (mask - 1) * len(views) + f
            copies.append(pltpu.make_async_remote_copy(
                src_ref=src_at(src_refs[f], 4 * px + 2 * py + pc), dst_ref=dst_at(land_refs[f], me),
                send_sem=send_sems.at[pair], recv_sem=recv_sems.at[pair],
                device_id=(px, py, pc), device_id_type=pl.DeviceIdType.MESH))
    return copies


def _split_start_call(name, srcs, lands, views):
    n = len(srcs)

    def body(*refs):
        src_refs, land_refs = refs[:n], refs[n:2 * n]
        send_sems, recv_sems, token = refs[2 * n], refs[2 * n + 1], refs[-1]
        for cp in _split_copies(views, src_refs, land_refs, send_sems, recv_sems):
            cp.start()
        token[...] = jnp.zeros_like(token)

    arrays = list(srcs) + list(lands)
    outs = pl.pallas_call(
        body, name=name,
        out_shape=(pltpu.SemaphoreType.DMA(((N_DEV - 1) * n,)), pltpu.SemaphoreType.DMA(((N_DEV - 1) * n,)),
                   *[pltpu.HBM(a.shape, a.dtype) for a in arrays], SDS((8, 128), F32)),
        in_specs=[_HBM_SPEC] * (2 * n),
        out_specs=(_SEM_SPEC, _SEM_SPEC, *[_HBM_SPEC] * (2 * n), pl.BlockSpec(memory_space=pltpu.VMEM)),
        input_output_aliases={i: 2 + i for i in range(2 * n)},
        compiler_params=pltpu.CompilerParams(has_side_effects=_DATAFLOW),
    )(*[pltpu.with_memory_space_constraint(a, pltpu.HBM) for a in arrays])
    return outs[0], outs[1], list(outs[2:2 + 2 * n]), outs[-1]


def _split_wait_call(name, send_sems, recv_sems, thru, views, after):
    n = len(views)

    def body(*refs):
        src_refs, land_refs = refs[:n], refs[n:2 * n]
        send, recv = refs[2 * n], refs[2 * n + 1]
        for cp in _split_copies(views, src_refs, land_refs, send, recv):
            cp.wait_send()
            cp.wait_recv()

    outs = pl.pallas_call(
        body, name=name,
        out_shape=tuple(pltpu.HBM(a.shape, a.dtype) for a in thru),
        in_specs=[_HBM_SPEC] * (2 * n) + [_SEM_SPEC, _SEM_SPEC, pl.BlockSpec(memory_space=pl.ANY)],
        out_specs=tuple([_HBM_SPEC] * (2 * n)),
        input_output_aliases={i: i for i in range(2 * n)},
        compiler_params=pltpu.CompilerParams(has_side_effects=_DATAFLOW),
    )(*thru, send_sems, recv_sems, after)
    return list(outs[n:])


def _own_slot(block, me):
    zone = jnp.zeros((N_DEV,) + block.shape, block.dtype)
    return lax.dynamic_update_slice(zone, block[None], (me,) + (0,) * block.ndim)


def _reduce_adamw_call(parts, w, m, v, block, name):
    nsrc = parts.shape[0]
    grid = tuple(s // b for s, b in zip(w.shape, block))
    c1 = 1.0 - ADAM_B1 ** ADAM_STEP
    c2 = 1.0 - ADAM_B2 ** ADAM_STEP

    def body(p_ref, w_ref, m_ref, v_ref, g_ref, d_ref, nm_ref, nv_ref):
        g = p_ref[0].astype(F32)
        for k in range(1, nsrc):
            g = g + p_ref[k].astype(F32)
        nm = ADAM_B1 * m_ref[...] + (1.0 - ADAM_B1) * g
        nv = ADAM_B2 * v_ref[...] + (1.0 - ADAM_B2) * (g * g)
        g_ref[...] = g
        nm_ref[...] = nm
        nv_ref[...] = nv
        d_ref[...] = -ADAM_LR * ((nm / c1) / (jnp.sqrt(nv / c2) + ADAM_EPS) + ADAM_WD * w_ref[...])

    own = pl.BlockSpec(tuple(block), lambda *i: i)
    return pl.pallas_call(
        body, name=name, grid=grid,
        in_specs=[pl.BlockSpec((nsrc,) + tuple(block), lambda *i: (0,) + i), own, own, own],
        out_specs=[own] * 4,
        out_shape=[SDS(w.shape, F32)] * 4,
        compiler_params=_cparams(("parallel",) * len(grid)),
    )(parts, w, m, v)


RELAYOUT_ROWS = 256
SHARD_COLS = IN_COLS // N_DEV


def _win_gather_layout_call(shards):
    def body(w_ref, o_ref):
        nat = jnp.concatenate([w_ref[k].astype(F32) for k in range(N_DEV)], axis=1)
        out = jnp.concatenate([nat[:, :3072], nat[:, 3080:], nat[:, 3072:3080],
                               jnp.zeros((RELAYOUT_ROWS, PW - IN_COLS), F32)], axis=1)
        o_ref[...] = out.astype(BF16)

    return pl.pallas_call(
        body, name="w_in_layout", grid=(D // RELAYOUT_ROWS,),
        in_specs=[pl.BlockSpec((N_DEV, RELAYOUT_ROWS, SHARD_COLS), lambda i: (0, i, 0))],
        out_specs=pl.BlockSpec((RELAYOUT_ROWS, PW), lambda i: (i, 0)),
        out_shape=SDS((D, PW), BF16),
        compiler_params=_cparams(("parallel",)),
    )(shards)


def _win_scatter_layout_call(grad):
    def body(g_ref, o_ref):
        g = g_ref[...]
        nat = jnp.concatenate([g[:, :3072], g[:, AB_COL:AB_COL + 8], g[:, 3072:AB_COL]], axis=1)
        for k in range(N_DEV):
            o_ref[k] = nat[:, SHARD_COLS * k:SHARD_COLS * (k + 1)].astype(BF16)

    return pl.pallas_call(
        body, name="w_in_grad_layout", grid=(D // RELAYOUT_ROWS,),
        in_specs=[pl.BlockSpec((RELAYOUT_ROWS, PW), lambda i: (i, 0))],
        out_specs=pl.BlockSpec((N_DEV, RELAYOUT_ROWS, SHARD_COLS), lambda i: (0, i, 0)),
        out_shape=SDS((N_DEV, D, SHARD_COLS), BF16),
        compiler_params=_cparams(("parallel",)),
    )(grad)


def _reduce_adamw_layers_call(parts, w, m, v, rows, name):
    _, R, C = w.shape
    c1 = 1.0 - ADAM_B1 ** ADAM_STEP
    c2 = 1.0 - ADAM_B2 ** ADAM_STEP

    def body(*refs):
        p_refs = refs[:DEPTH]
        w_ref, m_ref, v_ref, g_ref, d_ref, nm_ref, nv_ref = refs[DEPTH:]
        for l in range(DEPTH):
            @pl.when(pl.program_id(0) == l)
            def _(l=l):
                g = p_refs[l][0].astype(F32)
                for k in range(1, N_DEV):
                    g = g + p_refs[l][k].astype(F32)
                nm = ADAM_B1 * m_ref[0] + (1.0 - ADAM_B1) * g
                nv = ADAM_B2 * v_ref[0] + (1.0 - ADAM_B2) * (g * g)
                g_ref[0] = g
                nm_ref[0] = nm
                nv_ref[0] = nv
                d_ref[0] = -ADAM_LR * ((nm / c1) / (jnp.sqrt(nv / c2) + ADAM_EPS) + ADAM_WD * w_ref[0])

    def part_spec(l):
        return pl.BlockSpec((N_DEV, rows, C), lambda j, i: (0, jnp.where(j == l, i, 0), 0))

    own = pl.BlockSpec((1, rows, C), lambda j, i: (j, i, 0))
    return pl.pallas_call(
        body, name=name, grid=(DEPTH, R // rows),
        in_specs=[part_spec(l) for l in range(DEPTH)] + [own, own, own],
        out_specs=[own] * 4,
        out_shape=[SDS(w.shape, F32)] * 4,
        compiler_params=_cparams(("arbitrary", "arbitrary")),
    )(*parts, w, m, v)


_BIG = ("w_in", "w_out", "a_pw_w", "s5_glu_w")
_CONV = ("a_conv_w", "c_conv_w", "d_conv_w")
_CONV_TAPS = {"a_conv_w": KA, "c_conv_w": KC, "d_conv_w": KD}
_CONV_ROWS = {"a_conv_w": HALO, "c_conv_w": HALO_S, "d_conv_w": HALO_S}
_CONV_WIDTH = {"a_conv_w": BR, "c_conv_w": BR, "d_conv_w": 3 * BR}
_REPLICATED = ("norm_g", "a_conv_b", "a_ln_g", "a_ln_b", "a_pw_b", "s5_lambda_re", "s5_lambda_im", "s5_b_re", "s5_b_im",
               "s5_c_re", "s5_c_im", "s5_d", "s5_log_dt", "s5_glu_b", "d_a_log", "d_dt_bias", "d_norm_g", "final_g")
_REP_TAIL = ("norm_g", "final_g")
_REP_LAYER = tuple(n for n in _REPLICATED if n not in _REP_TAIL)
_WEIGHTS = ("norm_g", "w_in", "a_conv_w", "a_conv_b", "a_ln_g", "a_ln_b", "a_pw_w", "a_pw_b", "s5_lambda_re",
            "s5_lambda_im", "s5_b_re", "s5_b_im", "s5_c_re", "s5_c_im", "s5_d", "s5_log_dt", "s5_glu_w", "s5_glu_b",
            "c_conv_w", "d_conv_w", "d_a_log", "d_dt_bias", "d_norm_g", "w_out", "final_g")


def _size(shape):
    n = 1
    for s in shape:
        n *= s
    return n


PACK_ALIGN = 1024


def _piece_rows(n):
    return -(-n // PACK_ALIGN) * (PACK_ALIGN // 128)


def _pack_rows(pieces, row_mult):
    rows = []
    for p in pieces:
        flat = p.reshape(-1)
        rows.append(jnp.pad(flat, (0, _piece_rows(flat.shape[0]) * 128 - flat.shape[0])).reshape(-1, 128))
    out = jnp.concatenate(rows, axis=0)
    return jnp.pad(out, ((0, (-out.shape[0]) % row_mult), (0, 0)))


def _pack_layers(pieces):
    rows = []
    for p in pieces:
        flat = p.reshape(DEPTH, -1)
        nr = _piece_rows(flat.shape[1])
        rows.append(jnp.pad(flat, ((0, 0), (0, nr * 128 - flat.shape[1]))).reshape(DEPTH, nr, 128))
    return jnp.concatenate(rows, axis=1)


def _unpack_layers(packed, shapes):
    out, row = [], 0
    for s in shapes:
        n = _size(s[1:])
        nr = _piece_rows(n)
        out.append(packed[:, row:row + nr].reshape(DEPTH, -1)[:, :n].reshape(s))
        row += nr
    return out


def _unpack(packed, shapes):
    out, row = [], 0
    for s in shapes:
        n = _size(s)
        nr = _piece_rows(n)
        out.append(packed[row:row + nr].reshape(-1)[:n].reshape(s))
        row += nr
    return out


_GATHER_VIEWS = [(_whole, _slot)] * 5


def _gather_start(shards, layer, me):
    srcs = [shards[n].astype(BF16) for n in _BIG]
    srcs.append(_pack_rows([shards[n] for n in _CONV], 8))
    lands = [_own_slot(s, me) for s in srcs]
    return _split_start_call("gather_start_%d" % layer, srcs, lands, _GATHER_VIEWS)


def _gather_finish(weights, layer, started, after):
    send, recv, thru, _ = started
    w_in, w_out, a_pw, glu, conv_all = _split_wait_call("gather_wait_%d" % layer, send, recv, thru, _GATHER_VIEWS, after)
    full = {"w_in": _win_gather_layout_call(w_in), "w_out": w_out.reshape(D, D), "a_pw_w": a_pw.reshape(BR, BR),
            "s5_glu_w": glu.reshape(BR, BR)}
    shapes = [weights[n].shape[1:] for n in _CONV]
    per_dev = [_unpack(conv_all[k], shapes) for k in range(N_DEV)]
    for i, n in enumerate(_CONV):
        whole = jnp.concatenate([per_dev[k][i] for k in range(N_DEV)], axis=-1)
        full[n] = jnp.pad(whole, ((0, _CONV_ROWS[n] - _CONV_TAPS[n]), (0, 0)))
    return full


def _rows_view(rows):
    return lambda ref, k: ref.at[pl.ds(k * rows, rows), :]


_SCATTER_VIEWS = [(_slot, _slot), (_rows_view(D // N_DEV), _slot), (_rows_view(BR // N_DEV), _slot),
                  (_rows_view(BR // N_DEV), _slot), (_whole, _slot)]


def _scatter_start(grads, small, layer, me):
    srcs = [_win_scatter_layout_call(grads["w_in"])] + [grads[n].astype(BF16) for n in _BIG[1:]]
    own = [lax.dynamic_index_in_dim(srcs[0], me, 0, keepdims=False)]
    for s, rows in zip(srcs[1:], (D // N_DEV, BR // N_DEV, BR // N_DEV)):
        own.append(lax.dynamic_slice_in_dim(s, me * rows, rows, axis=0))
    lands = [_own_slot(o, me) for o in own + [small]]
    return _split_start_call("scatter_start_%d" % layer, srcs + [small], lands, _SCATTER_VIEWS)


_S5_KERNEL_SHAPES = {"s5_lambda_re": (1, NSTATE), "s5_lambda_im": (1, NSTATE), "s5_log_dt": (1, 16),
                     "s5_b_re": (NSTATE, 16), "s5_b_im": (NSTATE, 16), "s5_c_re": (BR, 64), "s5_c_im": (BR, 64)}
_S5_KEYS = {"s5_lambda_re": "lam_re", "s5_lambda_im": "lam_im", "s5_log_dt": "log_dt", "s5_b_re": "b_re",
            "s5_b_im": "b_im", "s5_c_re": "c_re", "s5_c_im": "c_im"}


def _s5_inputs_all(weights):
    return {n: weights[n].reshape((DEPTH,) + s) for n, s in _S5_KERNEL_SHAPES.items()}


def _s5_inputs(p):
    return {_S5_KEYS[n]: p["s5_in"][n] for n in _S5_KERNEL_SHAPES}


def _row(a, width=None):
    a = a.reshape(1, -1)
    return a if width is None else jnp.pad(a, ((0, 0), (0, width - a.shape[1])))


def _layer_params(p):
    q = dict(p)
    for n in ("norm_g", "a_conv_b", "a_ln_g", "a_ln_b", "a_pw_b", "s5_d", "s5_glu_b", "d_norm_g"):
        q[n] = _row(p[n])
    q["d_a_log"] = _row(p["d_a_log"], 128)
    q["d_dt_bias"] = _row(p["d_dt_bias"], 128)
    return q


def _layer_fwd(x, p, deps=()):
    q = _layer_params(p)
    proj, h = _inproj_call(x, q["norm_g"], q["w_in"], deps)
    ya, yc = _ac_fwd_call(proj, q)
    prep = p["s5_prep"]
    yb, cin_r, cin_i = _s5_fwd_call(proj, prep, q)
    dq, dk, dv, dgb = _dn_pre_fwd_call(proj, q)
    yd, ssave = _dn_core_fwd_call(proj, dq, dk, dv, dgb, q["d_norm_g"])
    x_next = _outproj_call(x, (ya, yb, yc, yd), q["w_out"])
    saved = dict(x=x, proj=proj, h=h, ya=ya, yb=yb, yc=yc, yd=yd, cin_r=cin_r, cin_i=cin_i,
                 q=dq, k=dk, v=dv, gb=dgb, ssave=ssave, prep=prep)
    return x_next, saved


def _layer_bwd(dx, p, sv, deps=(), on_weight_grads=None):
    q = _layer_params(p)
    proj = sv["proj"]
    dya, dyb, dyc, dyd, g_wout = _outproj_bwd_call(dx, (sv["ya"], sv["yb"], sv["yc"], sv["yd"]), q["w_out"], deps)
    dpa, dpc, g_acw, g_acb, g_alg, g_alb, g_apw, g_apb, g_ccw = _ac_bwd_call(proj, q, dya, dyc)
    dpb, *s5g = _s5_bwd_call(proj, sv["prep"], q, sv["cin_r"], sv["cin_i"], dyb)
    g_sd, g_gw, g_gb = s5g[6:]
    g_lre, g_lim, g_ldt, g_bre, g_bim, g_cre, g_cim = _s5_prep_bwd_call(_s5_inputs(p), s5g[:6])
    dq, dk, dv, dgb, dz, g_ng = _dn_core_bwd_call(proj, sv["q"], sv["k"], sv["v"], sv["gb"], q["d_norm_g"], sv["ssave"], dyd)
    dqkv, dab, g_dcw, g_alog, g_dtb = _dn_pre_bwd_call(proj, q, dq, dk, dv, dgb)
    slabs = (dpa, dpb, dpc, dqkv, dz, dab)
    g_win = _inproj_bwd_w_call(sv["h"], slabs)
    grads = {"w_in": g_win, "a_conv_w": g_acw, "a_conv_b": g_acb, "a_ln_g": g_alg, "a_ln_b": g_alb,
             "a_pw_w": g_apw, "a_pw_b": g_apb, "s5_lambda_re": g_lre, "s5_lambda_im": g_lim, "s5_b_re": g_bre,
             "s5_b_im": g_bim, "s5_c_re": g_cre, "s5_c_im": g_cim, "s5_d": g_sd, "s5_log_dt": g_ldt, "s5_glu_w": g_gw,
             "s5_glu_b": g_gb, "c_conv_w": g_ccw, "d_conv_w": g_dcw, "d_a_log": g_alog[:, :NH], "d_dt_bias": g_dtb[:, :NH],
             "d_norm_g": g_ng, "w_out": g_wout}
    tokens = ()
    if on_weight_grads is not None:
        small = _pack_rows([grads[n] for n in _REP_LAYER + _CONV], 8)
        tokens = on_weight_grads({n: grads[n] for n in _BIG}, small)
    dx_prev, grads["norm_g"] = _inproj_bwd_x_call(slabs, q["w_in"], sv["x"], q["norm_g"], dx, tokens)
    return dx_prev, grads


def _step(x, target, weights, moments_m, moments_v):
    me = 4 * lax.axis_index("x") + 2 * lax.axis_index("y") + lax.axis_index("c")
    layer_names = [n for n in _WEIGHTS if n != "final_g"]
    s5_all = _s5_inputs_all(weights)

    sharded = _BIG + _CONV
    gather = _gather_start({n: weights[n][0] for n in sharded}, 0, me)
    preps = [_s5_prep_call({_S5_KEYS[n]: a[l] for n, a in s5_all.items()}, [gather[3]]) for l in range(DEPTH)]
    x_out, after, layers, saved = x, preps[-1][0], [], []
    for l in range(DEPTH):
        full = _gather_finish(weights, l, gather, after)
        deps = ()
        if l + 1 < DEPTH:
            nxt, full["w_out"] = lax.optimization_barrier(({n: weights[n][l + 1] for n in sharded}, full["w_out"]))
            gather = _gather_start(nxt, l + 1, me)
            deps = [gather[3]]
        p = {n: (full[n] if n in full else weights[n][l]) for n in layer_names}
        p["s5_in"] = {n: a[l] for n, a in s5_all.items()}
        p["s5_prep"] = preps[l]
        layers.append(p)
        x_out, sv = _layer_fwd(x_out, p, deps)
        after = x_out
        saved.append(sv)
    dx0, g_final, loss_part = _loss_call(x_out, _row(weights["final_g"]), target)

    per_layer, scatters = [None] * DEPTH, [None] * DEPTH
    for l in range(DEPTH - 1, -1, -1):
        def start(big, small, l=l):
            scatters[l] = _scatter_start(big, small, l, me)
            return [scatters[l][3]]
        dx0, per_layer[l] = _layer_bwd(dx0, layers[l], saved[l], [scatters[l + 1][3]] if l + 1 < DEPTH else (), start)
    loss = lax.psum(loss_part[0, 0], ("x", "y", "c"))
    results = {}

    parts = [_split_wait_call("scatter_wait_%d" % l, scatters[l][0], scatters[l][1], scatters[l][2], _SCATTER_VIEWS, dx0)
             for l in range(DEPTH - 1, -1, -1)][::-1]
    rows = {"w_in": RELAYOUT_ROWS, "w_out": D // N_DEV, "a_pw_w": BR // N_DEV, "s5_glu_w": BR // N_DEV}
    for i, n in enumerate(_BIG):
        results[n] = _reduce_adamw_layers_call([parts[l][i] for l in range(DEPTH)], weights[n], moments_m[n],
                                               moments_v[n], rows[n], "adamw_" + n)

    conv_shapes = [(DEPTH, _CONV_ROWS[n], _CONV_WIDTH[n]) for n in _CONV]
    pack = lambda d: _pack_layers([d[n] for n in _REP_LAYER] + [jnp.zeros(s, F32) for s in conv_shapes])
    wpack = pack(weights)
    res = _reduce_adamw_layers_call([parts[l][len(_BIG)] for l in range(DEPTH)], wpack, pack(moments_m), pack(moments_v),
                                    wpack.shape[1], "adamw_replicated")
    shapes = [weights[n].shape for n in _REP_LAYER] + conv_shapes
    res = [_unpack_layers(r, shapes) for r in res]
    for i, n in enumerate(_REP_LAYER):
        results[n] = tuple(r[i] for r in res)

    tail_g = [jnp.stack([g["norm_g"] for g in per_layer]).reshape(DEPTH, D), g_final.reshape(D)]
    packt = lambda arrs: _pack_rows(arrs, 8)
    gathered, = _exchange_call("gather_tail_grads", [(packt(tail_g), _whole, (N_DEV,) + packt(tail_g).shape, _slot)])
    tail = _reduce_adamw_call(gathered, packt([weights[n] for n in _REP_TAIL]), packt([moments_m[n] for n in _REP_TAIL]),
                              packt([moments_v[n] for n in _REP_TAIL]), packt(tail_g).shape, "adamw_tail")
    tail = [_unpack(r, [weights[n].shape for n in _REP_TAIL]) for r in tail]
    for i, n in enumerate(_REP_TAIL):
        results[n] = tuple(r[i] for r in tail)

    own_g = []
    for i, n in enumerate(_CONV):
        width = _CONV_WIDTH[n] // N_DEV
        summed = res[0][len(_REP_LAYER) + i][:, :_CONV_TAPS[n], :]
        own_g.append(lax.dynamic_slice_in_dim(summed, me * width, width, axis=2))
    packc = lambda arrs: _pack_rows(arrs, 8)
    res = _reduce_adamw_call(packc(own_g)[None], packc([weights[n] for n in _CONV]), packc([moments_m[n] for n in _CONV]),
                             packc([moments_v[n] for n in _CONV]), packc(own_g).shape, "adamw_conv")
    res = [_unpack(r, [weights[n].shape for n in _CONV]) for r in res]
    for i, n in enumerate(_CONV):
        results[n] = tuple(r[i] for r in res)

    outs = [loss, dx0]
    for kind in range(4):
        outs += [results[n][kind] for n in _WEIGHTS]
    return tuple(outs)


def kernel(x, norm_g, w_in, a_conv_w, a_conv_b, a_ln_g, a_ln_b, a_pw_w, a_pw_b, s5_lambda_re, s5_lambda_im, s5_b_re, s5_b_im, s5_c_re, s5_c_im, s5_d, s5_log_dt, s5_glu_w, s5_glu_b, c_conv_w, d_conv_w, d_a_log, d_dt_bias, d_norm_g, w_out, final_g, loss_target, m_norm_g, m_w_in, m_a_conv_w, m_a_conv_b, m_a_ln_g, m_a_ln_b, m_a_pw_w, m_a_pw_b, m_s5_lambda_re, m_s5_lambda_im, m_s5_b_re, m_s5_b_im, m_s5_c_re, m_s5_c_im, m_s5_d, m_s5_log_dt, m_s5_glu_w, m_s5_glu_b, m_c_conv_w, m_d_conv_w, m_d_a_log, m_d_dt_bias, m_d_norm_g, m_w_out, m_final_g, v_norm_g, v_w_in, v_a_conv_w, v_a_conv_b, v_a_ln_g, v_a_ln_b, v_a_pw_w, v_a_pw_b, v_s5_lambda_re, v_s5_lambda_im, v_s5_b_re, v_s5_b_im, v_s5_c_re, v_s5_c_im, v_s5_d, v_s5_log_dt, v_s5_glu_w, v_s5_glu_b, v_c_conv_w, v_d_conv_w, v_d_a_log, v_d_dt_bias, v_d_norm_g, v_w_out, v_final_g):
    weights = dict(norm_g=norm_g, w_in=w_in, a_conv_w=a_conv_w, a_conv_b=a_conv_b, a_ln_g=a_ln_g, a_ln_b=a_ln_b, a_pw_w=a_pw_w, a_pw_b=a_pw_b, s5_lambda_re=s5_lambda_re, s5_lambda_im=s5_lambda_im, s5_b_re=s5_b_re, s5_b_im=s5_b_im, s5_c_re=s5_c_re, s5_c_im=s5_c_im, s5_d=s5_d, s5_log_dt=s5_log_dt, s5_glu_w=s5_glu_w, s5_glu_b=s5_glu_b, c_conv_w=c_conv_w, d_conv_w=d_conv_w, d_a_log=d_a_log, d_dt_bias=d_dt_bias, d_norm_g=d_norm_g, w_out=w_out, final_g=final_g)
    mom_m = dict(norm_g=m_norm_g, w_in=m_w_in, a_conv_w=m_a_conv_w, a_conv_b=m_a_conv_b, a_ln_g=m_a_ln_g, a_ln_b=m_a_ln_b, a_pw_w=m_a_pw_w, a_pw_b=m_a_pw_b, s5_lambda_re=m_s5_lambda_re, s5_lambda_im=m_s5_lambda_im, s5_b_re=m_s5_b_re, s5_b_im=m_s5_b_im, s5_c_re=m_s5_c_re, s5_c_im=m_s5_c_im, s5_d=m_s5_d, s5_log_dt=m_s5_log_dt, s5_glu_w=m_s5_glu_w, s5_glu_b=m_s5_glu_b, c_conv_w=m_c_conv_w, d_conv_w=m_d_conv_w, d_a_log=m_d_a_log, d_dt_bias=m_d_dt_bias, d_norm_g=m_d_norm_g, w_out=m_w_out, final_g=m_final_g)
    mom_v = dict(norm_g=v_norm_g, w_in=v_w_in, a_conv_w=v_a_conv_w, a_conv_b=v_a_conv_b, a_ln_g=v_a_ln_g, a_ln_b=v_a_ln_b, a_pw_w=v_a_pw_w, a_pw_b=v_a_pw_b, s5_lambda_re=v_s5_lambda_re, s5_lambda_im=v_s5_lambda_im, s5_b_re=v_s5_b_re, s5_b_im=v_s5_b_im, s5_c_re=v_s5_c_re, s5_c_im=v_s5_c_im, s5_d=v_s5_d, s5_log_dt=v_s5_log_dt, s5_glu_w=v_s5_glu_w, s5_glu_b=v_s5_glu_b, c_conv_w=v_c_conv_w, d_conv_w=v_d_conv_w, d_a_log=v_d_a_log, d_dt_bias=v_d_dt_bias, d_norm_g=v_d_norm_g, w_out=v_w_out, final_g=v_final_g)
    outs = _step(x[0], loss_target[0], weights, mom_m, mom_v)
    return (outs[0], outs[1][None]) + outs[2:]
```

```python
import functools

import jax
import jax.numpy as jnp
from jax import lax
from jax.experimental import pallas as pl
from jax.experimental.pallas import tpu as pltpu

F32 = jnp.float32
BF16 = jnp.bfloat16
HI = lax.Precision.HIGHEST
SDS = jax.ShapeDtypeStruct

N_DEV = 8
D = 1024
BR = 256
DEPTH = 4
IN_COLS = 3336
PW = 3456
AB_COL = 3328
EPS = 1e-6
TL = 512
SEG = TL // 8
HALO = 32
HALO_S = 8
KA, KC, KD = 31, 3, 4
CH = 64
DN_GROUP = 4
DN_FWD_GROUPS = 2
NH, HD = 4, 64
NSTATE = 1024
VMEM_LIMIT = 56 * 1024 * 1024

ADAM_LR, ADAM_B1, ADAM_B2, ADAM_EPS, ADAM_WD, ADAM_STEP = 0.001, 0.9, 0.999, 1e-08, 0.01, 10

NN = ((1,), (0,))
NT = ((1,), (1,))
TN = ((0,), (0,))


def _dot(a, b, dims, prec=None):
    return lax.dot_general(a, b, (dims, ((), ())), precision=prec, preferred_element_type=F32)


def _make_mm(cast, prec, fwd_dims):
    def prep(t):
        return t.astype(cast) if cast is not None else t

    @jax.custom_vjp
    def mm(a, w):
        return _dot(prep(a), prep(w), fwd_dims, prec)

    def fwd(a, w):
        return mm(a, w), (a, w)

    def bwd(res, dy):
        a, w = res
        a, w, dy = prep(a), prep(w), prep(dy)
        if fwd_dims == NN:
            return _dot(dy, w, NT, prec), _dot(a, dy, TN, prec)
        if fwd_dims == NT:
            return _dot(dy, w, NN, prec), _dot(dy, a, TN, prec)
        return _dot(w, dy, NT, prec), _dot(a, dy, NN, prec)

    mm.defvjp(fwd, bwd)
    return mm


mm = _make_mm(BF16, None, NN)
mm_nt = _make_mm(BF16, None, NT)
mm_tn = _make_mm(BF16, None, TN)
mmh = _make_mm(None, HI, NN)
mmh_nt = _make_mm(None, HI, NT)


def _sigmoid(x):
    return jax.nn.sigmoid(x)


def _silu(x):
    return x * jax.nn.sigmoid(x)


def _gelu(x):
    return 0.5 * x * (1.0 + jnp.tanh(0.7978845608028654 * (x + 0.044715 * (x * x * x))))


def _softplus(x):
    return jnp.maximum(x, 0.0) + jnp.log1p(jnp.exp(-jnp.abs(x)))


def _rms(x, g):
    return x * lax.rsqrt(jnp.mean(x * x, axis=-1, keepdims=True) + EPS) * g


def _cparams(sem):
    return pltpu.CompilerParams(dimension_semantics=sem, vmem_limit_bytes=VMEM_LIMIT)


def _tap_offsets(halo, taps):
    return [halo - (taps - 1) + k for k in range(taps)]


def _conv_fwd_impl(acat, w, tile, halo, taps):
    n = tile + halo
    out = None
    for k, off in enumerate(_tap_offsets(halo, taps)):
        src = jnp.roll(acat, n - off, axis=0)[:tile, :] if off != halo else acat[halo:, :]
        term = src * w[k:k + 1, :]
        out = term if out is None else out + term
    return out


def _make_conv(tile, halo, taps):
    @jax.custom_vjp
    def conv(acat, w):
        return _conv_fwd_impl(acat, w, tile, halo, taps)

    def fwd(acat, w):
        return conv(acat, w), (acat, w)

    def bwd(res, dy):
        acat, w = res
        n = tile + halo
        dyp = jnp.concatenate([dy, jnp.zeros((halo, dy.shape[1]), F32)], axis=0)
        rows = lax.broadcasted_iota(jnp.int32, w.shape, 0)
        dacat = None
        dw = jnp.zeros(w.shape, F32)
        for k, off in enumerate(_tap_offsets(halo, taps)):
            term = jnp.roll(dyp, off, axis=0) * w[k:k + 1, :]
            dacat = term if dacat is None else dacat + term
            src = jnp.roll(acat, n - off, axis=0)[:tile, :] if off != halo else acat[halo:, :]
            dw = dw + jnp.where(rows == k, jnp.sum(dy * src, axis=0, keepdims=True), 0.0)
        return dacat, dw

    conv.defvjp(fwd, bwd)
    return conv


def _halo_spec(tile, halo, width, col):
    per = tile // halo
    return pl.BlockSpec((halo, width), lambda i: (jnp.maximum(i * per - 1, 0), col))


def _halo_spec_rev(nt, tile, halo, width, col):
    per = tile // halo
    return pl.BlockSpec((halo, width), lambda i: (jnp.maximum((nt - 1 - i) * per - 1, 0), col))


def _dep_specs(deps):
    return [pl.BlockSpec((8, 128), lambda *_: (0, 0)) for _ in deps]


def _inproj_call(x, g, w, deps=()):
    L = x.shape[0]

    def body(x_ref, g_ref, w_ref, *rest):
        p_ref, h_ref = rest[len(deps):]
        h = _rms(x_ref[...], g_ref[...]).astype(BF16)
        h_ref[...] = h
        p_ref[...] = _dot(h, w_ref[...], NN)

    return pl.pallas_call(
        body, name="inproj", grid=(L // TL,),
        in_specs=[pl.BlockSpec((TL, D), lambda i: (i, 0)), pl.BlockSpec((1, D), lambda i: (0, 0)),
                  pl.BlockSpec((D, PW), lambda i: (0, 0))] + _dep_specs(deps),
        out_specs=[pl.BlockSpec((TL, PW), lambda i: (i, 0)), pl.BlockSpec((TL, D), lambda i: (i, 0))],
        out_shape=[SDS((L, PW), F32), SDS((L, D), BF16)],
        compiler_params=_cparams(("parallel",)),
    )(x, g, w, *deps)


def _outproj_call(x, ys, w):
    L = x.shape[0]

    def body(x_ref, a_ref, b_ref, c_ref, d_ref, w_ref, o_ref):
        acc = x_ref[...]
        for b, y_ref in enumerate((a_ref, b_ref, c_ref, d_ref)):
            acc = acc + _dot(y_ref[...].astype(BF16), w_ref[b * BR:(b + 1) * BR, :], NN)
        o_ref[...] = acc

    yspec = pl.BlockSpec((TL, BR), lambda i: (i, 0))
    return pl.pallas_call(
        body, name="outproj", grid=(L // TL,),
        in_specs=[pl.BlockSpec((TL, D), lambda i: (i, 0)), yspec, yspec, yspec, yspec,
                  pl.BlockSpec((D, D), lambda i: (0, 0))],
        out_specs=pl.BlockSpec((TL, D), lambda i: (i, 0)),
        out_shape=SDS((L, D), F32),
        compiler_params=_cparams(("parallel",)),
    )(x, *ys, w)


def _loss_call(x, g, target):
    L = x.shape[0]

    def body(x_ref, g_ref, t_ref, dx_ref, dg_ref, loss_ref):
        @pl.when(pl.program_id(0) == 0)
        def _():
            dg_ref[...] = jnp.zeros_like(dg_ref)
            loss_ref[...] = jnp.zeros_like(loss_ref)

        y, vjp = jax.vjp(_rms, x_ref[...], g_ref[...])
        err = y - t_ref[...]
        dx, dg = vjp(err * (1.0 / D))
        dx_ref[...] = dx
        dg_ref[...] += dg
        tot = jnp.sum(jnp.sum(err * err, axis=1, keepdims=True), axis=0, keepdims=True)
        loss_ref[...] += jnp.broadcast_to(tot * (0.5 / D), loss_ref.shape)

    return pl.pallas_call(
        body, name="loss_head", grid=(L // TL,),
        in_specs=[pl.BlockSpec((TL, D), lambda i: (i, 0)), pl.BlockSpec((1, D), lambda i: (0, 0)),
                  pl.BlockSpec((TL, D), lambda i: (i, 0))],
        out_specs=[pl.BlockSpec((TL, D), lambda i: (i, 0)), pl.BlockSpec((1, D), lambda i: (0, 0)),
                   pl.BlockSpec((1, 128), lambda i: (0, 0))],
        out_shape=[SDS((L, D), F32), SDS((1, D), F32), SDS((1, 128), F32)],
        compiler_params=_cparams(("arbitrary",)),
    )(x, g, target)


def _branch_a(valw, gatew, z, cw, cb, lg, lb, pw, pb, conv):
    a = conv(valw * _sigmoid(gatew), cw) + cb
    mu = jnp.mean(a, axis=-1, keepdims=True)
    xc = a - mu
    y = xc * lax.rsqrt(jnp.mean(xc * xc, axis=-1, keepdims=True) + EPS) * lg + lb
    y = mm(_silu(y), pw) + pb
    return y * _silu(z)


def _branch_c(bg, cw_, xw, z, w3, conv):
    return bg * conv(cw_ * xw, w3) * _silu(z)


def _ac_fwd_call(proj, p):
    L = proj.shape[0]

    def body(val, gate, za, hval, hgate, cb_, cc, cx, cz, hcc, hcx,
             acw, acb, alg, alb, apw, apb, ccw, ya_ref, yc_ref):
        nf = (pl.program_id(0) > 0).astype(F32)
        win = lambda h, m: jnp.concatenate([h[...] * nf, m[...]], axis=0)
        conv_a = functools.partial(_conv_fwd_impl, tile=TL, halo=HALO, taps=KA)
        conv_c = functools.partial(_conv_fwd_impl, tile=TL, halo=HALO, taps=KC)
        ya_ref[...] = _branch_a(win(hval, val), win(hgate, gate), za[...], acw[...], acb[...], alg[...], alb[...],
                                apw[...], apb[...], conv_a)
        yc_ref[...] = _branch_c(cb_[...], win(hcc, cc), win(hcx, cx), cz[...], ccw[...], conv_c)

    col = lambda j: pl.BlockSpec((TL, BR), lambda i: (i, j))
    hal = lambda j: _halo_spec(TL, HALO, BR, j)
    full = lambda a: pl.BlockSpec(a.shape, lambda i: (0,) * a.ndim)
    params = (p["a_conv_w"], p["a_conv_b"], p["a_ln_g"], p["a_ln_b"], p["a_pw_w"], p["a_pw_b"], p["c_conv_w"])
    return pl.pallas_call(
        body, name="ac_fwd", grid=(L // TL,),
        in_specs=[col(0), col(1), col(2), hal(0), hal(1), col(5), col(6), col(7), col(8), hal(6), hal(7)]
        + [full(a) for a in params],
        out_specs=[pl.BlockSpec((TL, BR), lambda i: (i, 0))] * 2,
        out_shape=[SDS((L, BR), F32)] * 2,
        compiler_params=_cparams(("parallel",)),
    )(*([proj] * 11), *params)


def _ac_bwd_call(proj, p, dya, dyc):
    L = proj.shape[0]
    nt = L // TL

    def body(val, gate, za, hval, hgate, cb_, cc, cx, cz, hcc, hcx,
             acw, acb, alg, alb, apw, apb, ccw, dya_ref, dyc_ref,
             da_ref, dc_ref, g_acw, g_acb, g_alg, g_alb, g_apw, g_apb, g_ccw, carry):
        i = pl.program_id(0)
        gouts = (g_acw, g_acb, g_alg, g_alb, g_apw, g_apb, g_ccw)

        @pl.when(i == 0)
        def _():
            carry[...] = jnp.zeros_like(carry)
            for r in gouts:
                r[...] = jnp.zeros_like(r)

        nf = (i < nt - 1).astype(F32)
        win = lambda h, m: jnp.concatenate([h[...] * nf, m[...]], axis=0)
        conv_a = _make_conv(TL, HALO, KA)
        conv_c = _make_conv(TL, HALO, KC)

        def f(valw, gatew, z, bg, ccw_, cxw, czv, w1, b1, lg, lb, pw, pb, w3):
            return (_branch_a(valw, gatew, z, w1, b1, lg, lb, pw, pb, conv_a),
                    _branch_c(bg, ccw_, cxw, czv, w3, conv_c))

        _, vjp = jax.vjp(f, win(hval, val), win(hgate, gate), za[...], cb_[...], win(hcc, cc), win(hcx, cx), cz[...],
                         acw[...], acb[...], alg[...], alb[...], apw[...].astype(F32), apb[...], ccw[...])
        (dvalw, dgatew, dz, dbg, dccw, dcxw, dczv, d1, d2, d3, d4, d5, d6, d7) = vjp((dya_ref[...], dyc_ref[...]))

        def settle(slot, dwin):
            tail = jnp.concatenate([jnp.zeros((TL - HALO, BR), F32), carry[slot]], axis=0)
            carry[slot] = dwin[:HALO, :]
            return (dwin[HALO:, :] + tail).astype(BF16)

        da_ref[:, 0:BR] = settle(0, dvalw)
        da_ref[:, BR:2 * BR] = settle(1, dgatew)
        da_ref[:, 2 * BR:3 * BR] = dz.astype(BF16)
        dc_ref[:, 0:BR] = dbg.astype(BF16)
        dc_ref[:, BR:2 * BR] = settle(2, dccw)
        dc_ref[:, 2 * BR:3 * BR] = settle(3, dcxw)
        dc_ref[:, 3 * BR:4 * BR] = dczv.astype(BF16)
        for r, g in zip(gouts, (d1, d2, d3, d4, d5, d6, d7)):
            r[...] += g

    col = lambda j: pl.BlockSpec((TL, BR), lambda i: (nt - 1 - i, j))
    hal = lambda j: _halo_spec_rev(nt, TL, HALO, BR, j)
    full = lambda a: pl.BlockSpec(a.shape, lambda i: (0,) * a.ndim)
    params = (p["a_conv_w"], p["a_conv_b"], p["a_ln_g"], p["a_ln_b"], p["a_pw_w"], p["a_pw_b"], p["c_conv_w"])
    rev = lambda w: pl.BlockSpec((TL, w), lambda i: (nt - 1 - i, 0))
    return pl.pallas_call(
        body, name="ac_bwd", grid=(nt,),
        in_specs=[col(0), col(1), col(2), hal(0), hal(1), col(5), col(6), col(7), col(8), hal(6), hal(7)]
        + [full(a) for a in params] + [rev(BR), rev(BR)],
        out_specs=[rev(3 * BR), rev(4 * BR)] + [full(a) for a in params],
        out_shape=[SDS((L, 3 * BR), BF16), SDS((L, 4 * BR), BF16)] + [SDS(a.shape, F32) for a in params],
        scratch_shapes=[pltpu.VMEM((4, HALO, BR), F32)],
        compiler_params=_cparams(("arbitrary",)),
    )(*([proj] * 11), *params, dya, dyc)


def _iota2(shape, dim):
    return lax.broadcasted_iota(jnp.int32, shape, dim)


def _s5_params(lam_re, lam_im, logdt, b_re, b_im, c_re, c_im):
    eg = (_iota2((16, NSTATE), 1) >> 6 == _iota2((16, NSTATE), 0)).astype(F32)
    dt = jnp.exp(mmh(jnp.broadcast_to(logdt, (8, 16)), eg)[0:1, :])
    lr = jnp.minimum(lam_re, -1e-4)
    li = lam_im
    mag = jnp.exp(lr * dt)
    lbr = mag * jnp.cos(li * dt)
    lbi = mag * jnp.sin(li * dt)
    den = lr * lr + li * li
    nr = lbr - 1.0
    fr = (nr * lr + lbi * li) / den
    fi = (lbi * lr - nr * li) / den
    row = _iota2((8, NSTATE), 0)
    f8 = jnp.where(row == 0, fr, jnp.where(row == 1, fi, 0.0))
    eye = (_iota2((NSTATE, NSTATE), 0) == _iota2((NSTATE, NSTATE), 1)).astype(F32)
    fcol = mmh_nt(eye, f8)
    frc, fic = fcol[:, 0:1], fcol[:, 1:2]
    bbr = frc * b_re - fic * b_im
    bbi = frc * b_im + fic * b_re
    e1 = ((_iota2((16, BR), 1) & 15) == _iota2((16, BR), 0)).astype(F32)
    m1 = ((_iota2((NSTATE, BR), 0) >> 6) == (_iota2((NSTATE, BR), 1) >> 4)).astype(F32)
    wbr = mmh(bbr, e1) * m1
    wbi = mmh(bbi, e1) * m1
    e2 = ((_iota2((64, NSTATE), 1) & 63) == _iota2((64, NSTATE), 0)).astype(F32)
    m2 = ((_iota2((BR, NSTATE), 0) >> 4) == (_iota2((BR, NSTATE), 1) >> 6)).astype(F32)
    wcr = mmh(c_re, e2) * m2
    wci = mmh(c_im, e2) * m2
    return lbr, lbi, wbr, wbi, wcr, wci


_S5_OUT = [(1, NSTATE), (1, NSTATE), (NSTATE, BR), (NSTATE, BR), (BR, NSTATE), (BR, NSTATE)]


def _s5_prep_call(sp, deps=()):
    def body(lre, lim, ldt, bre, bim, cre, cim, *rest):
        o_lbr, o_lbi, o_wbr, o_wbi, o_wcr, o_wci, pwr, pwi, qwr, qwi = rest[len(deps):]
        lbr, lbi, wbr, wbi, wcr, wci = _s5_params(lre[...], lim[...], ldt[...], bre[...], bim[...], cre[...], cim[...])
        o_lbr[...], o_lbi[...], o_wbr[...], o_wbi[...], o_wcr[...], o_wci[...] = lbr, lbi, wbr, wbi, wcr, wci
        pr, pi = lbr, lbi
        for i in range(SEG):
            pwr[i:i + 1, :] = pr
            pwi[i:i + 1, :] = pi
            qwr[SEG - 1 - i:SEG - i, :] = pr
            qwi[SEG - 1 - i:SEG - i, :] = -pi
            pr, pi = pr * lbr - pi * lbi, pr * lbi + pi * lbr

    args = (sp["lam_re"], sp["lam_im"], sp["log_dt"], sp["b_re"], sp["b_im"], sp["c_re"], sp["c_im"])
    return pl.pallas_call(
        body, name="s5_prep",
        out_shape=[SDS(s, F32) for s in _S5_OUT] + [SDS((SEG, NSTATE), F32)] * 4,
        compiler_params=pltpu.CompilerParams(vmem_limit_bytes=VMEM_LIMIT),
    )(*args, *deps)


def _s5_prep_bwd_call(sp, cots):
    def body(lre, lim, ldt, bre, bim, cre, cim, c0, c1, c2, c3, c4, c5, *outs):
        _, vjp = jax.vjp(_s5_params, lre[...], lim[...], ldt[...], bre[...], bim[...], cre[...], cim[...])
        grads = vjp((c0[...], c1[...], c2[...], c3[...], c4[...], c5[...]))
        for o, g in zip(outs, grads):
            o[...] = g

    args = (sp["lam_re"], sp["lam_im"], sp["log_dt"], sp["b_re"], sp["b_im"], sp["c_re"], sp["c_im"])
    return pl.pallas_call(
        body, name="s5_prep_bwd",
        out_shape=[SDS(a.shape, F32) for a in args],
        compiler_params=pltpu.CompilerParams(vmem_limit_bytes=VMEM_LIMIT),
    )(*args, *cots)


def _lanes(v, j):
    return v[:, j * 128:(j + 1) * 128]


def _s5_scan(sre, sim, pwr, pwi, cin_r, cin_i, reverse):
    row = _iota2((8, 128), 0)
    steps = (1, 2, 4)

    def lane_consts(j):
        lanes = slice(j * 128, (j + 1) * 128)
        if reverse:
            mult = [(jnp.broadcast_to(pwr[SEG - d:SEG - d + 1, lanes], (8, 128)),
                     jnp.broadcast_to(pwi[SEG - d:SEG - d + 1, lanes], (8, 128))) for d in steps]
            return mult, pwr[SEG - 8:SEG, lanes], pwi[SEG - 8:SEG, lanes]
        mult = [(jnp.broadcast_to(pwr[d - 1:d, lanes], (8, 128)),
                 jnp.broadcast_to(pwi[d - 1:d, lanes], (8, 128))) for d in steps]
        return mult, pwr[0:8, lanes], pwi[0:8, lanes]

    consts = [lane_consts(j) for j in range(8)]
    nblk = TL // 8

    def block(t, carry):
        b = nblk - 1 - t if reverse else t
        rows = pl.ds(pl.multiple_of(b * 8, 8), 8)
        new = []
        for j in range(8):
            mult, p8r, p8i = consts[j]
            vr, vi = sre.at[j], sim.at[j]
            sr, si = vr[rows, :], vi[rows, :]
            for d, (mr, mi) in zip(steps, mult):
                if reverse:
                    hr = jnp.where(row < 8 - d, pltpu.roll(sr, 8 - d, 0), 0.0)
                    hi = jnp.where(row < 8 - d, pltpu.roll(si, 8 - d, 0), 0.0)
                else:
                    hr = jnp.where(row >= d, pltpu.roll(sr, d, 0), 0.0)
                    hi = jnp.where(row >= d, pltpu.roll(si, d, 0), 0.0)
                sr, si = sr + mr * hr - mi * hi, si + mr * hi + mi * hr
            cr, ci = carry[2 * j], carry[2 * j + 1]
            sr, si = sr + p8r * cr - p8i * ci, si + p8r * ci + p8i * cr
            vr[rows, :] = sr
            vi[rows, :] = si
            edge = slice(0, 1) if reverse else slice(7, 8)
            new += [sr[edge, :], si[edge, :]]
        return tuple(new)

    init = []
    for j in range(8):
        init += [_lanes(cin_r, j), _lanes(cin_i, j)]
    ends = lax.fori_loop(0, nblk, block, tuple(init))
    return (jnp.concatenate([ends[2 * j] for j in range(8)], axis=1),
            jnp.concatenate([ends[2 * j + 1] for j in range(8)], axis=1))


def _bdot(a, b, dims):
    return _dot(a.astype(BF16), b.astype(BF16), dims)


def _s5_states(u, wbr, wbi, sre, sim):
    bur = _bdot(u, wbr, NT)
    bui = _bdot(u, wbi, NT)
    for j in range(8):
        sre[j] = _lanes(bur, j)
        sim[j] = _lanes(bui, j)


def _gather_lanes(s):
    return jnp.concatenate([s[j] for j in range(8)], axis=1)


def _s5_post(s_re, s_im, u, z, wcr, wci, dsk, gw, gb):
    y = mm_nt(s_re, wcr) - mm_nt(s_im, wci) + dsk * u
    yg = _gelu(y)
    return yg * _sigmoid(mm(yg, gw) + gb) * _silu(z)


def _s5_fwd_call(proj, prep, p):
    L = proj.shape[0]
    nt = L // TL
    lbr, lbi, wbr, wbi, wcr, wci, pwr, pwi, _, _ = prep

    def body(u_ref, z_ref, lbr_r, lbi_r, wbr_r, wbi_r, wcr_r, wci_r, pwr_r, pwi_r, d_r, gw_r, gb_r,
             yb_ref, cinr_ref, cini_ref, sre, sim, car, cai):
        @pl.when(pl.program_id(0) == 0)
        def _():
            car[...] = jnp.zeros_like(car)
            cai[...] = jnp.zeros_like(cai)

        u = u_ref[...]
        cinr_ref[0] = car[...]
        cini_ref[0] = cai[...]
        _s5_states(u, wbr_r[...], wbi_r[...], sre, sim)
        nr, ni = _s5_scan(sre, sim, pwr_r, pwi_r, car[...], cai[...], False)
        car[...] = nr
        cai[...] = ni
        yb_ref[...] = _s5_post(_gather_lanes(sre), _gather_lanes(sim), u, z_ref[...], wcr_r[...], wci_r[...],
                               d_r[...], gw_r[...], gb_r[...])

    full = lambda a: pl.BlockSpec(a.shape, lambda i: (0,) * a.ndim)
    consts = (lbr, lbi, wbr, wbi, wcr, wci, pwr, pwi, p["s5_d"], p["s5_glu_w"], p["s5_glu_b"])
    cspec = pl.BlockSpec((1, 1, NSTATE), lambda i: (i, 0, 0))
    return pl.pallas_call(
        body, name="s5_fwd", grid=(nt,),
        in_specs=[pl.BlockSpec((TL, BR), lambda i: (i, 3)), pl.BlockSpec((TL, BR), lambda i: (i, 4))]
        + [full(a) for a in consts],
        out_specs=[pl.BlockSpec((TL, BR), lambda i: (i, 0)), cspec, cspec],
        out_shape=[SDS((L, BR), F32), SDS((nt, 1, NSTATE), F32), SDS((nt, 1, NSTATE), F32)],
        scratch_shapes=[pltpu.VMEM((8, TL, 128), F32), pltpu.VMEM((8, TL, 128), F32),
                        pltpu.VMEM((1, NSTATE), F32), pltpu.VMEM((1, NSTATE), F32)],
        compiler_params=_cparams(("arbitrary",)),
    )(proj, proj, *consts)


def _s5_bwd_call(proj, prep, p, cin_r, cin_i, dyb):
    L = proj.shape[0]
    nt = L // TL
    lbr, lbi, wbr, wbi, wcr, wci, pwr, pwi, qwr, qwi = prep

    def body(u_ref, z_ref, lbr_r, lbi_r, wbr_r, wbi_r, wcr_r, wci_r, pwr_r, pwi_r, qwr_r, qwi_r, d_r, gw_r, gb_r,
             cinr_ref, cini_ref, dy_ref,
             db_ref, g_lbr, g_lbi, g_wbr, g_wbi, g_wcr, g_wci, g_d, g_gw, g_gb, sre, sim, gre, gim, car, cai):
        gouts = (g_lbr, g_lbi, g_wbr, g_wbi, g_wcr, g_wci, g_d, g_gw, g_gb)

        @pl.when(pl.program_id(0) == 0)
        def _():
            car[...] = jnp.zeros_like(car)
            cai[...] = jnp.zeros_like(cai)
            for r in gouts:
                r[...] = jnp.zeros_like(r)

        u = u_ref[...]
        lr, li = lbr_r[...], lbi_r[...]
        c0r, c0i = cinr_ref[0], cini_ref[0]
        _s5_states(u, wbr_r[...], wbi_r[...], sre, sim)
        _s5_scan(sre, sim, pwr_r, pwi_r, c0r, c0i, False)
        s_re, s_im = _gather_lanes(sre), _gather_lanes(sim)
        _, vjp = jax.vjp(_s5_post, s_re, s_im, u, z_ref[...], wcr_r[...], wci_r[...], d_r[...],
                         gw_r[...].astype(F32), gb_r[...])
        ds_re, ds_im, du, dz, dwcr, dwci, dd, dgw, dgb = vjp(dy_ref[...])
        for j in range(8):
            gre[j] = _lanes(ds_re, j)
            gim[j] = _lanes(ds_im, j)
        nr, ni = _s5_scan(gre, gim, qwr_r, qwi_r, car[...], cai[...], True)
        car[...] = nr
        cai[...] = ni
        a_re, a_im = _gather_lanes(gre), _gather_lanes(gim)
        first = _iota2((TL, NSTATE), 0) == 0
        p_re = jnp.where(first, c0r, jnp.roll(s_re, 1, axis=0))
        p_im = jnp.where(first, c0i, jnp.roll(s_im, 1, axis=0))
        g_lbr[...] += jnp.sum(a_re * p_re + a_im * p_im, axis=0, keepdims=True)
        g_lbi[...] += jnp.sum(a_im * p_re - a_re * p_im, axis=0, keepdims=True)
        du = du + _bdot(a_re, wbr_r[...], NN) + _bdot(a_im, wbi_r[...], NN)
        g_wbr[...] += _bdot(a_re, u, TN)
        g_wbi[...] += _bdot(a_im, u, TN)
        g_wcr[...] += dwcr
        g_wci[...] += dwci
        g_d[...] += dd
        g_gw[...] += dgw
        g_gb[...] += dgb
        db_ref[:, 0:BR] = du.astype(BF16)
        db_ref[:, BR:2 * BR] = dz.astype(BF16)

    full = lambda a: pl.BlockSpec(a.shape, lambda i: (0,) * a.ndim)
    consts = (lbr, lbi, wbr, wbi, wcr, wci, pwr, pwi, qwr, qwi, p["s5_d"], p["s5_glu_w"], p["s5_glu_b"])
    cspec = pl.BlockSpec((1, 1, NSTATE), lambda i: (nt - 1 - i, 0, 0))
    gshapes = _S5_OUT + [(1, BR), (BR, BR), (1, BR)]
    return pl.pallas_call(
        body, name="s5_bwd", grid=(nt,),
        in_specs=[pl.BlockSpec((TL, BR), lambda i: (nt - 1 - i, 3)), pl.BlockSpec((TL, BR), lambda i: (nt - 1 - i, 4))]
        + [full(a) for a in consts] + [cspec, cspec, pl.BlockSpec((TL, BR), lambda i: (nt - 1 - i, 0))],
        out_specs=[pl.BlockSpec((TL, 2 * BR), lambda i: (nt - 1 - i, 0))]
        + [pl.BlockSpec(s, lambda i: (0, 0)) for s in gshapes],
        out_shape=[SDS((L, 2 * BR), BF16)] + [SDS(s, F32) for s in gshapes],
        scratch_shapes=[pltpu.VMEM((8, TL, 128), F32)] * 4 + [pltpu.VMEM((1, NSTATE), F32)] * 2,
        compiler_params=_cparams(("arbitrary",)),
    )(proj, proj, *consts, cin_r, cin_i, dyb)


def _heads(x):
    return [x[:, h * HD:(h + 1) * HD] for h in range(NH)]


def _l2n(x, scale):
    return jnp.concatenate([xh * (lax.rsqrt(jnp.sum(xh * xh, axis=-1, keepdims=True) + EPS) * scale)
                            for xh in _heads(x)], axis=1)


def _dn_pre(qkvw, ab, cw, alog, dtb, conv, rows):
    c = _silu(conv(qkvw, cw))
    q = _l2n(c[:, 0:BR], HD ** -0.5)
    k = _l2n(c[:, BR:2 * BR], 1.0)
    v = c[:, 2 * BR:3 * BR]
    g = -jnp.exp(alog) * _softplus(ab + dtb)
    ri, ci = _iota2((rows, rows), 0), _iota2((rows, rows), 1)
    tri = ((ri >= ci) & ((ri >> 6) == (ci >> 6))).astype(F32)
    gc = mmh(tri, g)
    lane = _iota2(ab.shape, 1)
    return q, k, v, jnp.where(lane < NH, gc, jnp.where(lane < 2 * NH, _sigmoid(ab), 0.0))


def _dn_pre_fwd_call(proj, p):
    L = proj.shape[0]

    def body(m_ref, h_ref, ab_ref, cw, alog, dtb, q_ref, k_ref, v_ref, gb_ref):
        nf = (pl.program_id(0) > 0).astype(F32)
        qkvw = jnp.concatenate([h_ref[...] * nf, m_ref[...]], axis=0)
        conv = functools.partial(_conv_fwd_impl, tile=TL, halo=HALO_S, taps=KD)
        q_ref[...], k_ref[...], v_ref[...], gb_ref[...] = _dn_pre(qkvw, ab_ref[...], cw[...], alog[...], dtb[...], conv, TL)

    full = lambda a: pl.BlockSpec(a.shape, lambda i: (0,) * a.ndim)
    params = (p["d_conv_w"], p["d_a_log"], p["d_dt_bias"])
    o = pl.BlockSpec((TL, BR), lambda i: (i, 0))
    return pl.pallas_call(
        body, name="dn_pre_fwd", grid=(L // TL,),
        in_specs=[pl.BlockSpec((TL, 3 * BR), lambda i: (i, 3)), _halo_spec(TL, HALO_S, 3 * BR, 3),
                  pl.BlockSpec((TL, 128), lambda i: (i, AB_COL // 128))] + [full(a) for a in params],
        out_specs=[o, o, o, pl.BlockSpec((TL, 128), lambda i: (i, 0))],
        out_shape=[SDS((L, BR), F32)] * 3 + [SDS((L, 128), F32)],
        compiler_params=_cparams(("parallel",)),
    )(proj, proj, proj, *params)


def _dn_pre_bwd_call(proj, p, dq, dk, dv, dgb):
    L = proj.shape[0]
    nt = L // TL

    def body(m_ref, h_ref, ab_ref, cw, alog, dtb, dq_r, dk_r, dv_r, dgb_r,
             dqkv_ref, dab_ref, g_cw, g_alog, g_dtb, carry):
        i = pl.program_id(0)

        @pl.when(i == 0)
        def _():
            carry[...] = jnp.zeros_like(carry)
            for r in (g_cw, g_alog, g_dtb):
                r[...] = jnp.zeros_like(r)

        nf = (i < nt - 1).astype(F32)
        qkvw = jnp.concatenate([h_ref[...] * nf, m_ref[...]], axis=0)
        conv = _make_conv(TL, HALO_S, KD)
        _, vjp = jax.vjp(lambda a, b, c, d, e: _dn_pre(a, b, c, d, e, conv, TL),
                         qkvw, ab_ref[...], cw[...], alog[...], dtb[...])
        dwin, dab, dcw, dalog, ddtb = vjp((dq_r[...], dk_r[...], dv_r[...], dgb_r[...]))
        tail = jnp.concatenate([jnp.zeros((TL - HALO_S, 3 * BR), F32), carry[...]], axis=0)
        carry[...] = dwin[:HALO_S, :]
        dqkv_ref[...] = (dwin[HALO_S:, :] + tail).astype(BF16)
        dab_ref[...] = dab.astype(BF16)
        g_cw[...] += dcw
        g_alog[...] += dalog
        g_dtb[...] += ddtb

    full = lambda a: pl.BlockSpec(a.shape, lambda i: (0,) * a.ndim)
    params = (p["d_conv_w"], p["d_a_log"], p["d_dt_bias"])
    rev = lambda w: pl.BlockSpec((TL, w), lambda i: (nt - 1 - i, 0))
    return pl.pallas_call(
        body, name="dn_pre_bwd", grid=(nt,),
        in_specs=[pl.BlockSpec((TL, 3 * BR), lambda i: (nt - 1 - i, 3)), _halo_spec_rev(nt, TL, HALO_S, 3 * BR, 3),
                  pl.BlockSpec((TL, 128), lambda i: (nt - 1 - i, AB_COL // 128))] + [full(a) for a in params]
        + [rev(BR), rev(BR), rev(BR), rev(128)],
        out_specs=[rev(3 * BR), rev(128)] + [full(a) for a in params],
        out_shape=[SDS((L, 3 * BR), BF16), SDS((L, 128), BF16)] + [SDS(a.shape, F32) for a in params],
        scratch_shapes=[pltpu.VMEM((HALO_S, 3 * BR), F32)],
        compiler_params=_cparams(("arbitrary",)),
    )(proj, proj, proj, *params, dq, dk, dv, dgb)


def _dn_group(q, k, v, gb, z, ng, *s):
    return _dn_chunks(q, k, v, gb, z, ng, s, DN_GROUP, None)


def _dn_chunks(q, k, v, gb, z, ng, s, chunks, entering):
    ri, ci = _iota2((CH, CH), 0), _iota2((CH, CH), 1)
    causal, strict = ri >= ci, ri > ci
    eye = (ri == ci).astype(F32)
    s = list(s)
    pairs = []
    for c in range(chunks):
        rows = slice(c * CH, (c + 1) * CH)
        gbc = gb[rows, :]
        for h, (qh, kh, vh, zh) in enumerate(zip(_heads(q[rows, :]), _heads(k[rows, :]), _heads(v[rows, :]),
                                                 _heads(z[rows, :]))):
            gc = jnp.broadcast_to(gbc[:, h:h + 1], (CH, HD))
            beta = gbc[:, NH + h:NH + h + 1]
            decay = jnp.where(causal, jnp.exp(jnp.where(causal, gc - gc.T, 0.0)), 0.0)
            egc = jnp.exp(gc)
            glast = gc[CH - 1:CH, :]
            kb = kh * beta
            pairs.append(dict(q=qh, k=kh, z=zh, decay=decay, qe=qh * egc, kd=kh * jnp.exp(glast - gc),
                              sdec=jnp.exp(glast[:, 0:1]), kb=kb, rhs=jnp.concatenate([vh * beta, kb * egc], axis=1)))
    for p in pairs:
        p["pw"] = jnp.where(strict, mm_nt(p["kb"], p["k"]) * p["decay"], 0.0)
    for p in pairs:
        p["t"] = eye - p["pw"]
    for _ in range(5):
        for p in pairs:
            p["pw"] = mm(p["pw"], p["pw"])
        for p in pairs:
            p["t"] = mm(p["t"], eye + p["pw"])
    for p in pairs:
        p["uw"] = mm(p["t"], p["rhs"])
    for p in pairs:
        p["attn"] = mm_nt(p["q"], p["k"]) * p["decay"]
    out_rows = []
    for c in range(chunks):
        if entering is not None and c > 0 and c % DN_GROUP == 0:
            entering.append(list(s))
        grp = pairs[c * NH:(c + 1) * NH]
        ws = [mm(jnp.concatenate([p["uw"][:, HD:], p["qe"]], axis=0), s[h]) for h, p in enumerate(grp)]
        v_new = [p["uw"][:, :HD] - w_[:CH, :] for p, w_ in zip(grp, ws)]
        o = [w_[CH:, :] + mm(p["attn"], vn) for p, w_, vn in zip(grp, ws, v_new)]
        s = [s[h] * p["sdec"] + mm_tn(p["kd"], vn) for h, (p, vn) in enumerate(zip(grp, v_new))]
        o = [oh * lax.rsqrt(jnp.mean(oh * oh, axis=-1, keepdims=True) + EPS) * ng * _silu(p["z"]) for oh, p in zip(o, grp)]
        out_rows.append(jnp.concatenate(o, axis=1))
    return (jnp.concatenate(out_rows, axis=0), *s)


def _dn_core_fwd_call(proj, q, k, v, gb, ng):
    L = q.shape[0]
    rows = DN_FWD_GROUPS * DN_GROUP * CH
    steps = L // rows

    def body(q_r, k_r, v_r, gb_r, z_r, ng_r, yd_ref, ssave_ref, s_scr):
        @pl.when(pl.program_id(0) == 0)
        def _():
            s_scr[...] = jnp.zeros_like(s_scr)

        entering = [[s_scr[h] for h in range(NH)]]
        yd, *s2 = _dn_chunks(q_r[...], k_r[...], v_r[...], gb_r[...], z_r[...], ng_r[...], entering[0],
                             DN_FWD_GROUPS * DN_GROUP, entering)
        yd_ref[...] = yd
        for h in range(NH):
            s_scr[h] = s2[h]
            for g, states in enumerate(entering):
                ssave_ref[g, h] = states[h]

    c = pl.BlockSpec((rows, BR), lambda i: (i, 0))
    return pl.pallas_call(
        body, name="dn_core_fwd", grid=(steps,),
        in_specs=[c, c, c, pl.BlockSpec((rows, 128), lambda i: (i, 0)), pl.BlockSpec((rows, BR), lambda i: (i, 12)),
                  pl.BlockSpec((1, HD), lambda i: (0, 0))],
        out_specs=[c, pl.BlockSpec((DN_FWD_GROUPS, NH, HD, HD), lambda i: (i, 0, 0, 0))],
        out_shape=[SDS((L, BR), F32), SDS((steps * DN_FWD_GROUPS, NH, HD, HD), F32)],
        scratch_shapes=[pltpu.VMEM((NH, HD, HD), F32)],
        compiler_params=_cparams(("arbitrary",)),
    )(q, k, v, gb, proj, ng)


def _dn_core_bwd_call(proj, q, k, v, gb, ng, ssave, dyd):
    L = q.shape[0]
    rows = DN_GROUP * CH
    ng_ = L // rows

    def body(q_r, k_r, v_r, gb_r, z_r, ng_r, s_r, dy_r, dq_ref, dk_ref, dv_ref, dgb_ref, dz_ref, g_ng, ds_scr):
        @pl.when(pl.program_id(0) == 0)
        def _():
            ds_scr[...] = jnp.zeros_like(ds_scr)
            g_ng[...] = jnp.zeros_like(g_ng)

        _, vjp = jax.vjp(_dn_group, q_r[...], k_r[...], v_r[...], gb_r[...], z_r[...], ng_r[...],
                         *[s_r[0, h] for h in range(NH)])
        dq, dk, dv, dgb, dz, dng, *ds = vjp((dy_r[...], *[ds_scr[h] for h in range(NH)]))
        dq_ref[...], dk_ref[...], dv_ref[...], dgb_ref[...] = dq, dk, dv, dgb
        dz_ref[...] = dz.astype(BF16)
        g_ng[...] += dng
        for h in range(NH):
            ds_scr[h] = ds[h]

    c = pl.BlockSpec((rows, BR), lambda i: (ng_ - 1 - i, 0))
    c128 = pl.BlockSpec((rows, 128), lambda i: (ng_ - 1 - i, 0))
    return pl.pallas_call(
        body, name="dn_core_bwd", grid=(ng_,),
        in_specs=[c, c, c, c128, pl.BlockSpec((rows, BR), lambda i: (ng_ - 1 - i, 12)),
                  pl.BlockSpec((1, HD), lambda i: (0, 0)),
                  pl.BlockSpec((1, NH, HD, HD), lambda i: (ng_ - 1 - i, 0, 0, 0)), c],
        out_specs=[c, c, c, c128, c, pl.BlockSpec((1, HD), lambda i: (0, 0))],
        out_shape=[SDS((L, BR), F32)] * 3 + [SDS((L, 128), F32), SDS((L, BR), BF16), SDS((1, HD), F32)],
        scratch_shapes=[pltpu.VMEM((NH, HD, HD), F32)],
        compiler_params=_cparams(("arbitrary",)),
    )(q, k, v, gb, proj, ng, ssave, dyd)


def _outproj_bwd_call(dx, ys, w, deps=()):
    L = dx.shape[0]

    def body(dx_ref, a_ref, b_ref, c_ref, d_ref, w_ref, *rest):
        da, db, dc, dd, dw_ref = rest[len(deps):]

        @pl.when(pl.program_id(0) == 0)
        def _():
            dw_ref[...] = jnp.zeros_like(dw_ref)

        dxb = dx_ref[...].astype(BF16)
        for b, (y_ref, o_ref) in enumerate(zip((a_ref, b_ref, c_ref, d_ref), (da, db, dc, dd))):
            o_ref[...] = _dot(dxb, w_ref[b * BR:(b + 1) * BR, :], NT)
            dw_ref[b * BR:(b + 1) * BR, :] += _dot(y_ref[...].astype(BF16), dxb, TN)

    yspec = pl.BlockSpec((TL, BR), lambda i: (i, 0))
    return pl.pallas_call(
        body, name="outproj_bwd", grid=(L // TL,),
        in_specs=[pl.BlockSpec((TL, D), lambda i: (i, 0)), yspec, yspec, yspec, yspec,
                  pl.BlockSpec((D, D), lambda i: (0, 0))] + _dep_specs(deps),
        out_specs=[yspec] * 4 + [pl.BlockSpec((D, D), lambda i: (0, 0))],
        out_shape=[SDS((L, BR), F32)] * 4 + [SDS((D, D), F32)],
        compiler_params=_cparams(("arbitrary",)),
    )(dx, *ys, w, *deps)


def _slab_cols(slabs):
    widths = [s.shape[1] for s in slabs]
    starts = [sum(widths[:i]) for i in range(len(widths))]
    assert starts[-1] + widths[-1] == PW
    return list(zip(starts, widths))


def _inproj_bwd_x_call(slabs, w, x, g, dx_next, deps=()):
    L = x.shape[0]
    cols = _slab_cols(slabs)
    n = len(slabs)

    def body(*refs):
        dp_refs, (w_ref, x_ref, g_ref, dxn_ref) = refs[:n], refs[n:n + 4]
        dx_ref, dg_ref = refs[n + 4 + len(deps):]

        @pl.when(pl.program_id(0) == 0)
        def _():
            dg_ref[...] = jnp.zeros_like(dg_ref)

        dh = None
        for dp_ref, (c0, cw) in zip(dp_refs, cols):
            part = _dot(dp_ref[...], w_ref[:, c0:c0 + cw], NT)
            dh = part if dh is None else dh + part
        _, vjp = jax.vjp(_rms, x_ref[...], g_ref[...])
        dx, dg = vjp(dh)
        dx_ref[...] = dx + dxn_ref[...]
        dg_ref[...] += dg

    row = lambda w_: pl.BlockSpec((TL, w_), lambda i: (i, 0))
    return pl.pallas_call(
        body, name="inproj_bwd_x", grid=(L // TL,),
        in_specs=[row(cw) for _, cw in cols]
        + [pl.BlockSpec((D, PW), lambda i: (0, 0)), row(D), pl.BlockSpec((1, D), lambda i: (0, 0)), row(D)]
        + _dep_specs(deps),
        out_specs=[row(D), pl.BlockSpec((1, D), lambda i: (0, 0))],
        out_shape=[SDS((L, D), F32), SDS((1, D), F32)],
        compiler_params=_cparams(("arbitrary",)),
    )(*slabs, w, x, g, dx_next, *deps)


def _inproj_bwd_w_call(h, slabs):
    L = h.shape[0]
    cols = _slab_cols(slabs)
    n = len(slabs)

    def body(*refs):
        h_ref, dp_refs, dw_ref = refs[0], refs[1:1 + n], refs[1 + n]

        @pl.when(pl.program_id(0) == 0)
        def _():
            dw_ref[...] = jnp.zeros_like(dw_ref)

        hv = h_ref[...]
        for dp_ref, (c0, cw) in zip(dp_refs, cols):
            dw_ref[:, c0:c0 + cw] += _dot(hv, dp_ref[...], TN)

    row = lambda w_: pl.BlockSpec((TL, w_), lambda i: (i, 0))
    return pl.pallas_call(
        body, name="inproj_bwd_w", grid=(L // TL,),
        in_specs=[row(D)] + [row(cw) for _, cw in cols],
        out_specs=pl.BlockSpec((D, PW), lambda i: (0, 0)),
        out_shape=SDS((D, PW), F32),
        compiler_params=_cparams(("arbitrary",)),
    )(h, *slabs)


def _exchange_call(name, flows):
    n = len(flows)

    def body(*refs):
        srcs, dsts = refs[:n], refs[n:2 * n]
        send_sems, recv_sems, local_sems = refs[2 * n:]
        x, y, c = lax.axis_index("x"), lax.axis_index("y"), lax.axis_index("c")
        me = 4 * x + 2 * y + c
        copies = []
        for mask in range(1, N_DEV):
            px = 1 - x if mask & 4 else x
            py = 1 - y if mask & 2 else y
            pc = 1 - c if mask & 1 else c
            for f, (_, src_at, _, dst_at) in enumerate(flows):
                cp = pltpu.make_async_remote_copy(
                    src_ref=src_at(srcs[f], 4 * px + 2 * py + pc), dst_ref=dst_at(dsts[f], me),
                    send_sem=send_sems.at[mask - 1, f], recv_sem=recv_sems.at[mask - 1, f],
                    device_id=(px, py, pc), device_id_type=pl.DeviceIdType.MESH)
                cp.start()
                copies.append(cp)
        mine = [pltpu.make_async_copy(src_at(srcs[f], me), dst_at(dsts[f], me), local_sems.at[f])
                for f, (_, src_at, _, dst_at) in enumerate(flows)]
        for cp in mine:
            cp.start()
        for cp in copies + mine:
            cp.wait()

    return pl.pallas_call(
        body, name=name,
        in_specs=[pl.BlockSpec(memory_space=pl.ANY)] * n,
        out_specs=[pl.BlockSpec(memory_space=pl.ANY)] * n,
        out_shape=[SDS(tuple(shape), src.dtype) for src, _, shape, _ in flows],
        scratch_shapes=[pltpu.SemaphoreType.DMA((N_DEV - 1, n)), pltpu.SemaphoreType.DMA((N_DEV - 1, n)),
                        pltpu.SemaphoreType.DMA((n,))],
    )(*[f[0] for f in flows])


def _whole(ref, _):
    return ref


def _slot(ref, k):
    return ref.at[k]


_HBM_SPEC = pl.BlockSpec(memory_space=pltpu.HBM)
_SEM_SPEC = pl.BlockSpec(memory_space=pltpu.SEMAPHORE)
_DATAFLOW = pltpu.SideEffectType.DATAFLOW_SIDE_EFFECTING


def _split_copies(views, src_refs, land_refs, send_sems, recv_sems):
    x, y, c = lax.axis_index("x"), lax.axis_index("y"), lax.axis_index("c")
    me = 4 * x + 2 * y + c
    copies = []
    for mask in range(1, N_DEV):
        px = 1 - x if mask & 4 else x
        py = 1 - y if mask & 2 else y
        pc = 1 - c if mask & 1 else c
        for f, (src_at, dst_at) in enumerate(views):
            pair = (mask - 1) * len(views) + f
            copies.append(pltpu.make_async_remote_copy(
                src_ref=src_at(src_refs[f], 4 * px + 2 * py + pc), dst_ref=dst_at(land_refs[f], me),
                send_sem=send_sems.at[pair], recv_sem=recv_sems.at[pair],
                device_id=(px, py, pc), device_id_type=pl.DeviceIdType.MESH))
    return copies


def _split_start_call(name, srcs, lands, views):
    n = len(srcs)

    def body(*refs):
        src_refs, land_refs = refs[:n], refs[n:2 * n]
        send_sems, recv_sems, token = refs[2 * n], refs[2 * n + 1], refs[-1]
        for cp in _split_copies(views, src_refs, land_refs, send_sems, recv_sems):
            cp.start()
        token[...] = jnp.zeros_like(token)

    arrays = list(srcs) + list(lands)
    outs = pl.pallas_call(
        body, name=name,
        out_shape=(pltpu.SemaphoreType.DMA(((N_DEV - 1) * n,)), pltpu.SemaphoreType.DMA(((N_DEV - 1) * n,)),
                   *[pltpu.HBM(a.shape, a.dtype) for a in arrays], SDS((8, 128), F32)),
        in_specs=[_HBM_SPEC] * (2 * n),
        out_specs=(_SEM_SPEC, _SEM_SPEC, *[_HBM_SPEC] * (2 * n), pl.BlockSpec(memory_space=pltpu.VMEM)),
        input_output_aliases={i: 2 + i for i in range(2 * n)},
        compiler_params=pltpu.CompilerParams(has_side_effects=_DATAFLOW),
    )(*[pltpu.with_memory_space_constraint(a, pltpu.HBM) for a in arrays])
    return outs[0], outs[1], list(outs[2:2 + 2 * n]), outs[-1]


def _split_wait_call(name, send_sems, recv_sems, thru, views, after):
    n = len(views)

    def body(*refs):
        src_refs, land_refs = refs[:n], refs[n:2 * n]
        send, recv = refs[2 * n], refs[2 * n + 1]
        for cp in _split_copies(views, src_refs, land_refs, send, recv):
            cp.wait_send()
            cp.wait_recv()

    outs = pl.pallas_call(
        body, name=name,
        out_shape=tuple(pltpu.HBM(a.shape, a.dtype) for a in thru),
        in_specs=[_HBM_SPEC] * (2 * n) + [_SEM_SPEC, _SEM_SPEC, pl.BlockSpec(memory_space=pl.ANY)],
        out_specs=tuple([_HBM_SPEC] * (2 * n)),
        input_output_aliases={i: i for i in range(2 * n)},
        compiler_params=pltpu.CompilerParams(has_side_effects=_DATAFLOW),
    )(*thru, send_sems, recv_sems, after)
    return list(outs[n:])


def _own_slot(block, me):
    zone = lax.empty((N_DEV,) + block.shape, block.dtype)
    return lax.dynamic_update_slice(zone, block[None], (me,) + (0,) * block.ndim)


def _reduce_adamw_call(parts, w, m, v, block, name):
    nsrc = parts.shape[0]
    grid = tuple(s // b for s, b in zip(w.shape, block))
    c1 = 1.0 - ADAM_B1 ** ADAM_STEP
    c2 = 1.0 - ADAM_B2 ** ADAM_STEP

    def body(p_ref, w_ref, m_ref, v_ref, g_ref, d_ref, nm_ref, nv_ref):
        g = p_ref[0].astype(F32)
        for k in range(1, nsrc):
            g = g + p_ref[k].astype(F32)
        nm = ADAM_B1 * m_ref[...] + (1.0 - ADAM_B1) * g
        nv = ADAM_B2 * v_ref[...] + (1.0 - ADAM_B2) * (g * g)
        g_ref[...] = g
        nm_ref[...] = nm
        nv_ref[...] = nv
        d_ref[...] = -ADAM_LR * ((nm / c1) / (jnp.sqrt(nv / c2) + ADAM_EPS) + ADAM_WD * w_ref[...])

    own = pl.BlockSpec(tuple(block), lambda *i: i)
    return pl.pallas_call(
        body, name=name, grid=grid,
        in_specs=[pl.BlockSpec((nsrc,) + tuple(block), lambda *i: (0,) + i), own, own, own],
        out_specs=[own] * 4,
        out_shape=[SDS(w.shape, F32)] * 4,
        compiler_params=_cparams(("parallel",) * len(grid)),
    )(parts, w, m, v)


RELAYOUT_ROWS = 256
SHARD_COLS = IN_COLS // N_DEV


def _win_gather_layout_call(shards):
    def body(w_ref, o_ref):
        nat = jnp.concatenate([w_ref[k].astype(F32) for k in range(N_DEV)], axis=1)
        out = jnp.concatenate([nat[:, :3072], nat[:, 3080:], nat[:, 3072:3080],
                               jnp.zeros((RELAYOUT_ROWS, PW - IN_COLS), F32)], axis=1)
        o_ref[...] = out.astype(BF16)

    return pl.pallas_call(
        body, name="w_in_layout", grid=(D // RELAYOUT_ROWS,),
        in_specs=[pl.BlockSpec((N_DEV, RELAYOUT_ROWS, SHARD_COLS), lambda i: (0, i, 0))],
        out_specs=pl.BlockSpec((RELAYOUT_ROWS, PW), lambda i: (i, 0)),
        out_shape=SDS((D, PW), BF16),
        compiler_params=_cparams(("parallel",)),
    )(shards)


def _win_scatter_layout_call(grad):
    def body(g_ref, o_ref):
        g = g_ref[...]
        nat = jnp.concatenate([g[:, :3072], g[:, AB_COL:AB_COL + 8], g[:, 3072:AB_COL]], axis=1)
        for k in range(N_DEV):
            o_ref[k] = nat[:, SHARD_COLS * k:SHARD_COLS * (k + 1)].astype(BF16)

    return pl.pallas_call(
        body, name="w_in_grad_layout", grid=(D // RELAYOUT_ROWS,),
        in_specs=[pl.BlockSpec((RELAYOUT_ROWS, PW), lambda i: (i, 0))],
        out_specs=pl.BlockSpec((N_DEV, RELAYOUT_ROWS, SHARD_COLS), lambda i: (0, i, 0)),
        out_shape=SDS((N_DEV, D, SHARD_COLS), BF16),
        compiler_params=_cparams(("parallel",)),
    )(grad)


def _reduce_adamw_layers_call(parts, w, m, v, rows, name):
    _, R, C = w.shape
    c1 = 1.0 - ADAM_B1 ** ADAM_STEP
    c2 = 1.0 - ADAM_B2 ** ADAM_STEP

    def body(*refs):
        p_refs = refs[:DEPTH]
        w_ref, m_ref, v_ref, g_ref, d_ref, nm_ref, nv_ref = refs[DEPTH:]
        for l in range(DEPTH):
            @pl.when(pl.program_id(0) == l)
            def _(l=l):
                g = p_refs[l][0].astype(F32)
                for k in range(1, N_DEV):
                    g = g + p_refs[l][k].astype(F32)
                nm = ADAM_B1 * m_ref[0] + (1.0 - ADAM_B1) * g
                nv = ADAM_B2 * v_ref[0] + (1.0 - ADAM_B2) * (g * g)
                g_ref[0] = g
                nm_ref[0] = nm
                nv_ref[0] = nv
                d_ref[0] = -ADAM_LR * ((nm / c1) / (jnp.sqrt(nv / c2) + ADAM_EPS) + ADAM_WD * w_ref[0])

    def part_spec(l):
        return pl.BlockSpec((N_DEV, rows, C), lambda j, i: (0, jnp.where(j == l, i, 0), 0))

    own = pl.BlockSpec((1, rows, C), lambda j, i: (j, i, 0))
    return pl.pallas_call(
        body, name=name, grid=(DEPTH, R // rows),
        in_specs=[part_spec(l) for l in range(DEPTH)] + [own, own, own],
        out_specs=[own] * 4,
        out_shape=[SDS(w.shape, F32)] * 4,
        compiler_params=_cparams(("arbitrary", "arbitrary")),
    )(*parts, w, m, v)


_BIG = ("w_in", "w_out", "a_pw_w", "s5_glu_w")
_CONV = ("a_conv_w", "c_conv_w", "d_conv_w")
_CONV_TAPS = {"a_conv_w": KA, "c_conv_w": KC, "d_conv_w": KD}
_CONV_ROWS = {"a_conv_w": HALO, "c_conv_w": HALO_S, "d_conv_w": HALO_S}
_CONV_WIDTH = {"a_conv_w": BR, "c_conv_w": BR, "d_conv_w": 3 * BR}
_REPLICATED = ("norm_g", "a_conv_b", "a_ln_g", "a_ln_b", "a_pw_b", "s5_lambda_re", "s5_lambda_im", "s5_b_re", "s5_b_im",
               "s5_c_re", "s5_c_im", "s5_d", "s5_log_dt", "s5_glu_b", "d_a_log", "d_dt_bias", "d_norm_g", "final_g")
_REP_TAIL = ("norm_g", "final_g")
_REP_LAYER = tuple(n for n in _REPLICATED if n not in _REP_TAIL)
_WEIGHTS = ("norm_g", "w_in", "a_conv_w", "a_conv_b", "a_ln_g", "a_ln_b", "a_pw_w", "a_pw_b", "s5_lambda_re",
            "s5_lambda_im", "s5_b_re", "s5_b_im", "s5_c_re", "s5_c_im", "s5_d", "s5_log_dt", "s5_glu_w", "s5_glu_b",
            "c_conv_w", "d_conv_w", "d_a_log", "d_dt_bias", "d_norm_g", "w_out", "final_g")


def _size(shape):
    n = 1
    for s in shape:
        n *= s
    return n


PACK_ALIGN = 1024


def _piece_rows(n):
    return -(-n // PACK_ALIGN) * (PACK_ALIGN // 128)


def _pack_rows(pieces, row_mult):
    rows = []
    for p in pieces:
        flat = p.reshape(-1)
        rows.append(jnp.pad(flat, (0, _piece_rows(flat.shape[0]) * 128 - flat.shape[0])).reshape(-1, 128))
    out = jnp.concatenate(rows, axis=0)
    return jnp.pad(out, ((0, (-out.shape[0]) % row_mult), (0, 0)))


def _pack_layers(pieces):
    rows = []
    for p in pieces:
        flat = p.reshape(DEPTH, -1)
        nr = _piece_rows(flat.shape[1])
        rows.append(jnp.pad(flat, ((0, 0), (0, nr * 128 - flat.shape[1]))).reshape(DEPTH, nr, 128))
    return jnp.concatenate(rows, axis=1)


def _unpack_layers(packed, shapes):
    out, row = [], 0
    for s in shapes:
        n = _size(s[1:])
        nr = _piece_rows(n)
        out.append(packed[:, row:row + nr].reshape(DEPTH, -1)[:, :n].reshape(s))
        row += nr
    return out


def _unpack(packed, shapes):
    out, row = [], 0
    for s in shapes:
        n = _size(s)
        nr = _piece_rows(n)
        out.append(packed[row:row + nr].reshape(-1)[:n].reshape(s))
        row += nr
    return out


_GATHER_VIEWS = [(_whole, _slot)] * 5


def _gather_start(shards, layer, me):
    srcs = [shards[n].astype(BF16) for n in _BIG]
    srcs.append(_pack_rows([shards[n] for n in _CONV], 8))
    lands = [_own_slot(s, me) for s in srcs]
    return _split_start_call("gather_start_%d" % layer, srcs, lands, _GATHER_VIEWS)


def _gather_finish(weights, layer, started, after):
    send, recv, thru, _ = started
    w_in, w_out, a_pw, glu, conv_all = _split_wait_call("gather_wait_%d" % layer, send, recv, thru, _GATHER_VIEWS, after)
    full = {"w_in": _win_gather_layout_call(w_in), "w_out": w_out.reshape(D, D), "a_pw_w": a_pw.reshape(BR, BR),
            "s5_glu_w": glu.reshape(BR, BR)}
    shapes = [weights[n].shape[1:] for n in _CONV]
    per_dev = [_unpack(conv_all[k], shapes) for k in range(N_DEV)]
    for i, n in enumerate(_CONV):
        whole = jnp.concatenate([per_dev[k][i] for k in range(N_DEV)], axis=-1)
        full[n] = jnp.pad(whole, ((0, _CONV_ROWS[n] - _CONV_TAPS[n]), (0, 0)))
    return full


def _rows_view(rows):
    return lambda ref, k: ref.at[pl.ds(k * rows, rows), :]


_SCATTER_VIEWS = [(_slot, _slot), (_rows_view(D // N_DEV), _slot), (_rows_view(BR // N_DEV), _slot),
                  (_rows_view(BR // N_DEV), _slot), (_whole, _slot)]


def _scatter_start(grads, small, layer, me):
    srcs = [_win_scatter_layout_call(grads["w_in"])] + [grads[n].astype(BF16) for n in _BIG[1:]]
    own = [lax.dynamic_index_in_dim(srcs[0], me, 0, keepdims=False)]
    for s, rows in zip(srcs[1:], (D // N_DEV, BR // N_DEV, BR // N_DEV)):
        own.append(lax.dynamic_slice_in_dim(s, me * rows, rows, axis=0))
    lands = [_own_slot(o, me) for o in own + [small]]
    return _split_start_call("scatter_start_%d" % layer, srcs + [small], lands, _SCATTER_VIEWS)


_S5_KERNEL_SHAPES = {"s5_lambda_re": (1, NSTATE), "s5_lambda_im": (1, NSTATE), "s5_log_dt": (1, 16),
                     "s5_b_re": (NSTATE, 16), "s5_b_im": (NSTATE, 16), "s5_c_re": (BR, 64), "s5_c_im": (BR, 64)}
_S5_KEYS = {"s5_lambda_re": "lam_re", "s5_lambda_im": "lam_im", "s5_log_dt": "log_dt", "s5_b_re": "b_re",
            "s5_b_im": "b_im", "s5_c_re": "c_re", "s5_c_im": "c_im"}


def _s5_inputs_all(weights):
    return {n: weights[n].reshape((DEPTH,) + s) for n, s in _S5_KERNEL_SHAPES.items()}


def _s5_inputs(p):
    return {_S5_KEYS[n]: p["s5_in"][n] for n in _S5_KERNEL_SHAPES}


def _row(a, width=None):
    a = a.reshape(1, -1)
    return a if width is None else jnp.pad(a, ((0, 0), (0, width - a.shape[1])))


def _layer_params(p):
    q = dict(p)
    for n in ("norm_g", "a_conv_b", "a_ln_g", "a_ln_b", "a_pw_b", "s5_d", "s5_glu_b", "d_norm_g"):
        q[n] = _row(p[n])
    q["d_a_log"] = _row(p["d_a_log"], 128)
    q["d_dt_bias"] = _row(p["d_dt_bias"], 128)
    return q


def _layer_fwd(x, p, deps=()):
    q = _layer_params(p)
    proj, h = _inproj_call(x, q["norm_g"], q["w_in"], deps)
    ya, yc = _ac_fwd_call(proj, q)
    prep = p["s5_prep"]
    yb, cin_r, cin_i = _s5_fwd_call(proj, prep, q)
    dq, dk, dv, dgb = _dn_pre_fwd_call(proj, q)
    yd, ssave = _dn_core_fwd_call(proj, dq, dk, dv, dgb, q["d_norm_g"])
    x_next = _outproj_call(x, (ya, yb, yc, yd), q["w_out"])
    saved = dict(x=x, proj=proj, h=h, ya=ya, yb=yb, yc=yc, yd=yd, cin_r=cin_r, cin_i=cin_i,
                 q=dq, k=dk, v=dv, gb=dgb, ssave=ssave, prep=prep)
    return x_next, saved


def _layer_bwd(dx, p, sv, deps=(), on_weight_grads=None):
    q = _layer_params(p)
    proj = sv["proj"]
    dya, dyb, dyc, dyd, g_wout = _outproj_bwd_call(dx, (sv["ya"], sv["yb"], sv["yc"], sv["yd"]), q["w_out"], deps)
    dpa, dpc, g_acw, g_acb, g_alg, g_alb, g_apw, g_apb, g_ccw = _ac_bwd_call(proj, q, dya, dyc)
    dpb, *s5g = _s5_bwd_call(proj, sv["prep"], q, sv["cin_r"], sv["cin_i"], dyb)
    g_sd, g_gw, g_gb = s5g[6:]
    g_lre, g_lim, g_ldt, g_bre, g_bim, g_cre, g_cim = _s5_prep_bwd_call(_s5_inputs(p), s5g[:6])
    dq, dk, dv, dgb, dz, g_ng = _dn_core_bwd_call(proj, sv["q"], sv["k"], sv["v"], sv["gb"], q["d_norm_g"], sv["ssave"], dyd)
    dqkv, dab, g_dcw, g_alog, g_dtb = _dn_pre_bwd_call(proj, q, dq, dk, dv, dgb)
    slabs = (dpa, dpb, dpc, dqkv, dz, dab)
    g_win = _inproj_bwd_w_call(sv["h"], slabs)
    grads = {"w_in": g_win, "a_conv_w": g_acw, "a_conv_b": g_acb, "a_ln_g": g_alg, "a_ln_b": g_alb,
             "a_pw_w": g_apw, "a_pw_b": g_apb, "s5_lambda_re": g_lre, "s5_lambda_im": g_lim, "s5_b_re": g_bre,
             "s5_b_im": g_bim, "s5_c_re": g_cre, "s5_c_im": g_cim, "s5_d": g_sd, "s5_log_dt": g_ldt, "s5_glu_w": g_gw,
             "s5_glu_b": g_gb, "c_conv_w": g_ccw, "d_conv_w": g_dcw, "d_a_log": g_alog[:, :NH], "d_dt_bias": g_dtb[:, :NH],
             "d_norm_g": g_ng, "w_out": g_wout}
    tokens = ()
    if on_weight_grads is not None:
        small = _pack_rows([grads[n] for n in _REP_LAYER + _CONV], 8)
        tokens = on_weight_grads({n: grads[n] for n in _BIG}, small)
    dx_prev, grads["norm_g"] = _inproj_bwd_x_call(slabs, q["w_in"], sv["x"], q["norm_g"], dx, tokens)
    return dx_prev, grads


def _step(x, target, weights, moments_m, moments_v):
    me = 4 * lax.axis_index("x") + 2 * lax.axis_index("y") + lax.axis_index("c")
    layer_names = [n for n in _WEIGHTS if n != "final_g"]
    s5_all = _s5_inputs_all(weights)

    sharded = _BIG + _CONV
    gather = _gather_start({n: weights[n][0] for n in sharded}, 0, me)
    preps = [_s5_prep_call({_S5_KEYS[n]: a[l] for n, a in s5_all.items()}, [gather[3]]) for l in range(DEPTH)]
    x_out, after, layers, saved = x, preps[-1][0], [], []
    for l in range(DEPTH):
        full = _gather_finish(weights, l, gather, after)
        deps = ()
        if l + 1 < DEPTH:
            nxt, full["w_out"] = lax.optimization_barrier(({n: weights[n][l + 1] for n in sharded}, full["w_out"]))
            gather = _gather_start(nxt, l + 1, me)
            deps = [gather[3]]
        p = {n: (full[n] if n in full else weights[n][l]) for n in layer_names}
        p["s5_in"] = {n: a[l] for n, a in s5_all.items()}
        p["s5_prep"] = preps[l]
        layers.append(p)
        x_out, sv = _layer_fwd(x_out, p, deps)
        after = x_out
        saved.append(sv)
    dx0, g_final, loss_part = _loss_call(x_out, _row(weights["final_g"]), target)

    per_layer, scatters = [None] * DEPTH, [None] * DEPTH
    for l in range(DEPTH - 1, -1, -1):
        def start(big, small, l=l):
            scatters[l] = _scatter_start(big, small, l, me)
            return [scatters[l][3]]
        dx0, per_layer[l] = _layer_bwd(dx0, layers[l], saved[l], [scatters[l + 1][3]] if l + 1 < DEPTH else (), start)
    loss = lax.psum(loss_part[0, 0], ("x", "y", "c"))
    results = {}

    parts = [_split_wait_call("scatter_wait_%d" % l, scatters[l][0], scatters[l][1], scatters[l][2], _SCATTER_VIEWS, dx0)
             for l in range(DEPTH - 1, -1, -1)][::-1]
    rows = {"w_in": RELAYOUT_ROWS, "w_out": D // N_DEV, "a_pw_w": BR // N_DEV, "s5_glu_w": BR // N_DEV}
    for i, n in enumerate(_BIG):
        results[n] = _reduce_adamw_layers_call([parts[l][i] for l in range(DEPTH)], weights[n], moments_m[n],
                                               moments_v[n], rows[n], "adamw_" + n)

    conv_shapes = [(DEPTH, _CONV_ROWS[n], _CONV_WIDTH[n]) for n in _CONV]
    pack = lambda d: _pack_layers([d[n] for n in _REP_LAYER] + [jnp.zeros(s, F32) for s in conv_shapes])
    wpack = pack(weights)
    res = _reduce_adamw_layers_call([parts[l][len(_BIG)] for l in range(DEPTH)], wpack, pack(moments_m), pack(moments_v),
                                    wpack.shape[1], "adamw_replicated")
    shapes = [weights[n].shape for n in _REP_LAYER] + conv_shapes
    res = [_unpack_layers(r, shapes) for r in res]
    for i, n in enumerate(_REP_LAYER):
        results[n] = tuple(r[i] for r in res)

    tail_g = [jnp.stack([g["norm_g"] for g in per_layer]).reshape(DEPTH, D), g_final.reshape(D)]
    packt = lambda arrs: _pack_rows(arrs, 8)
    gathered, = _exchange_call("gather_tail_grads", [(packt(tail_g), _whole, (N_DEV,) + packt(tail_g).shape, _slot)])
    tail = _reduce_adamw_call(gathered, packt([weights[n] for n in _REP_TAIL]), packt([moments_m[n] for n in _REP_TAIL]),
                              packt([moments_v[n] for n in _REP_TAIL]), packt(tail_g).shape, "adamw_tail")
    tail = [_unpack(r, [weights[n].shape for n in _REP_TAIL]) for r in tail]
    for i, n in enumerate(_REP_TAIL):
        results[n] = tuple(r[i] for r in tail)

    own_g = []
    for i, n in enumerate(_CONV):
        width = _CONV_WIDTH[n] // N_DEV
        summed = res[0][len(_REP_LAYER) + i][:, :_CONV_TAPS[n], :]
        own_g.append(lax.dynamic_slice_in_dim(summed, me * width, width, axis=2))
    packc = lambda arrs: _pack_rows(arrs, 8)
    res = _reduce_adamw_call(packc(own_g)[None], packc([weights[n] for n in _CONV]), packc([moments_m[n] for n in _CONV]),
                             packc([moments_v[n] for n in _CONV]), packc(own_g).shape, "adamw_conv")
    res = [_unpack(r, [weights[n].shape for n in _CONV]) for r in res]
    for i, n in enumerate(_CONV):
        results[n] = tuple(r[i] for r in res)

    outs = [loss, dx0]
    for kind in range(4):
        outs += [results[n][kind] for n in _WEIGHTS]
    return tuple(outs)


def kernel(x, norm_g, w_in, a_conv_w, a_conv_b, a_ln_g, a_ln_b, a_pw_w, a_pw_b, s5_lambda_re, s5_lambda_im, s5_b_re, s5_b_im, s5_c_re, s5_c_im, s5_d, s5_log_dt, s5_glu_w, s5_glu_b, c_conv_w, d_conv_w, d_a_log, d_dt_bias, d_norm_g, w_out, final_g, loss_target, m_norm_g, m_w_in, m_a_conv_w, m_a_conv_b, m_a_ln_g, m_a_ln_b, m_a_pw_w, m_a_pw_b, m_s5_lambda_re, m_s5_lambda_im, m_s5_b_re, m_s5_b_im, m_s5_c_re, m_s5_c_im, m_s5_d, m_s5_log_dt, m_s5_glu_w, m_s5_glu_b, m_c_conv_w, m_d_conv_w, m_d_a_log, m_d_dt_bias, m_d_norm_g, m_w_out, m_final_g, v_norm_g, v_w_in, v_a_conv_w, v_a_conv_b, v_a_ln_g, v_a_ln_b, v_a_pw_w, v_a_pw_b, v_s5_lambda_re, v_s5_lambda_im, v_s5_b_re, v_s5_b_im, v_s5_c_re, v_s5_c_im, v_s5_d, v_s5_log_dt, v_s5_glu_w, v_s5_glu_b, v_c_conv_w, v_d_conv_w, v_d_a_log, v_d_dt_bias, v_d_norm_g, v_w_out, v_final_g):
    weights = dict(norm_g=norm_g, w_in=w_in, a_conv_w=a_conv_w, a_conv_b=a_conv_b, a_ln_g=a_ln_g, a_ln_b=a_ln_b, a_pw_w=a_pw_w, a_pw_b=a_pw_b, s5_lambda_re=s5_lambda_re, s5_lambda_im=s5_lambda_im, s5_b_re=s5_b_re, s5_b_im=s5_b_im, s5_c_re=s5_c_re, s5_c_im=s5_c_im, s5_d=s5_d, s5_log_dt=s5_log_dt, s5_glu_w=s5_glu_w, s5_glu_b=s5_glu_b, c_conv_w=c_conv_w, d_conv_w=d_conv_w, d_a_log=d_a_log, d_dt_bias=d_dt_bias, d_norm_g=d_norm_g, w_out=w_out, final_g=final_g)
    mom_m = dict(norm_g=m_norm_g, w_in=m_w_in, a_conv_w=m_a_conv_w, a_conv_b=m_a_conv_b, a_ln_g=m_a_ln_g, a_ln_b=m_a_ln_b, a_pw_w=m_a_pw_w, a_pw_b=m_a_pw_b, s5_lambda_re=m_s5_lambda_re, s5_lambda_im=m_s5_lambda_im, s5_b_re=m_s5_b_re, s5_b_im=m_s5_b_im, s5_c_re=m_s5_c_re, s5_c_im=m_s5_c_im, s5_d=m_s5_d, s5_log_dt=m_s5_log_dt, s5_glu_w=m_s5_glu_w, s5_glu_b=m_s5_glu_b, c_conv_w=m_c_conv_w, d_conv_w=m_d_conv_w, d_a_log=m_d_a_log, d_dt_bias=m_d_dt_bias, d_norm_g=m_d_norm_g, w_out=m_w_out, final_g=m_final_g)
    mom_v = dict(norm_g=v_norm_g, w_in=v_w_in, a_conv_w=v_a_conv_w, a_conv_b=v_a_conv_b, a_ln_g=v_a_ln_g, a_ln_b=v_a_ln_b, a_pw_w=v_a_pw_w, a_pw_b=v_a_pw_b, s5_lambda_re=v_s5_lambda_re, s5_lambda_im=v_s5_lambda_im, s5_b_re=v_s5_b_re, s5_b_im=v_s5_b_im, s5_c_re=v_s5_c_re, s5_c_im=v_s5_c_im, s5_d=v_s5_d, s5_log_dt=v_s5_log_dt, s5_glu_w=v_s5_glu_w, s5_glu_b=v_s5_glu_b, c_conv_w=v_c_conv_w, d_conv_w=v_d_conv_w, d_a_log=v_d_a_log, d_dt_bias=v_d_dt_bias, d_norm_g=v_d_norm_g, w_out=v_w_out, final_g=v_final_g)
    outs = _step(x[0], loss_target[0], weights, mom_m, mom_v)
    return (outs[0], outs[1][None]) + outs[2:]
```

```python
import functools

import jax
import jax.numpy as jnp
from jax import lax
from jax.experimental import pallas as pl
from jax.experimental.pallas import tpu as pltpu

F32 = jnp.float32
BF16 = jnp.bfloat16
HI = lax.Precision.HIGHEST
SDS = jax.ShapeDtypeStruct

N_DEV = 8
D = 1024
BR = 256
DEPTH = 4
IN_COLS = 3336
PW = 3456
AB_COL = 3328
EPS = 1e-6
TL = 512
SEG = TL // 8
HALO = 32
HALO_S = 8
KA, KC, KD = 31, 3, 4
CH = 64
DN_GROUP = 4
DN_FWD_GROUPS = 2
NH, HD = 4, 64
NSTATE = 1024
VMEM_LIMIT = 56 * 1024 * 1024

ADAM_LR, ADAM_B1, ADAM_B2, ADAM_EPS, ADAM_WD, ADAM_STEP = 0.001, 0.9, 0.999, 1e-08, 0.01, 10

NN = ((1,), (0,))
NT = ((1,), (1,))
TN = ((0,), (0,))


def _dot(a, b, dims, prec=None):
    return lax.dot_general(a, b, (dims, ((), ())), precision=prec, preferred_element_type=F32)


def _make_mm(cast, prec, fwd_dims):
    def prep(t):
        return t.astype(cast) if cast is not None else t

    @jax.custom_vjp
    def mm(a, w):
        return _dot(prep(a), prep(w), fwd_dims, prec)

    def fwd(a, w):
        return mm(a, w), (a, w)

    def bwd(res, dy):
        a, w = res
        a, w, dy = prep(a), prep(w), prep(dy)
        if fwd_dims == NN:
            return _dot(dy, w, NT, prec), _dot(a, dy, TN, prec)
        if fwd_dims == NT:
            return _dot(dy, w, NN, prec), _dot(dy, a, TN, prec)
        return _dot(w, dy, NT, prec), _dot(a, dy, NN, prec)

    mm.defvjp(fwd, bwd)
    return mm


mm = _make_mm(BF16, None, NN)
mm_nt = _make_mm(BF16, None, NT)
mm_tn = _make_mm(BF16, None, TN)
mmh = _make_mm(None, HI, NN)
mmh_nt = _make_mm(None, HI, NT)


def _sigmoid(x):
    return jax.nn.sigmoid(x)


def _silu(x):
    return x * jax.nn.sigmoid(x)


def _gelu(x):
    return 0.5 * x * (1.0 + jnp.tanh(0.7978845608028654 * (x + 0.044715 * (x * x * x))))


def _softplus(x):
    return jnp.maximum(x, 0.0) + jnp.log1p(jnp.exp(-jnp.abs(x)))


def _rms(x, g):
    return x * lax.rsqrt(jnp.mean(x * x, axis=-1, keepdims=True) + EPS) * g


def _cparams(sem):
    return pltpu.CompilerParams(dimension_semantics=sem, vmem_limit_bytes=VMEM_LIMIT)


def _tap_offsets(halo, taps):
    return [halo - (taps - 1) + k for k in range(taps)]


def _conv_fwd_impl(acat, w, tile, halo, taps):
    n = tile + halo
    out = None
    for k, off in enumerate(_tap_offsets(halo, taps)):
        src = jnp.roll(acat, n - off, axis=0)[:tile, :] if off != halo else acat[halo:, :]
        term = src * w[k:k + 1, :]
        out = term if out is None else out + term
    return out


def _make_conv(tile, halo, taps):
    @jax.custom_vjp
    def conv(acat, w):
        return _conv_fwd_impl(acat, w, tile, halo, taps)

    def fwd(acat, w):
        return conv(acat, w), (acat, w)

    def bwd(res, dy):
        acat, w = res
        n = tile + halo
        dyp = jnp.concatenate([dy, jnp.zeros((halo, dy.shape[1]), F32)], axis=0)
        rows = lax.broadcasted_iota(jnp.int32, w.shape, 0)
        dacat = None
        dw = jnp.zeros(w.shape, F32)
        for k, off in enumerate(_tap_offsets(halo, taps)):
            term = jnp.roll(dyp, off, axis=0) * w[k:k + 1, :]
            dacat = term if dacat is None else dacat + term
            src = jnp.roll(acat, n - off, axis=0)[:tile, :] if off != halo else acat[halo:, :]
            dw = dw + jnp.where(rows == k, jnp.sum(dy * src, axis=0, keepdims=True), 0.0)
        return dacat, dw

    conv.defvjp(fwd, bwd)
    return conv


def _halo_spec(tile, halo, width, col):
    per = tile // halo
    return pl.BlockSpec((halo, width), lambda i: (jnp.maximum(i * per - 1, 0), col))


def _halo_spec_rev(nt, tile, halo, width, col):
    per = tile // halo
    return pl.BlockSpec((halo, width), lambda i: (jnp.maximum((nt - 1 - i) * per - 1, 0), col))


def _dep_specs(deps):
    return [pl.BlockSpec((8, 128), lambda *_: (0, 0)) for _ in deps]


def _inproj_call(x, g, w, deps=()):
    L = x.shape[0]

    def body(x_ref, g_ref, w_ref, *rest):
        p_ref, h_ref = rest[len(deps):]
        h = _rms(x_ref[...], g_ref[...]).astype(BF16)
        h_ref[...] = h
        p_ref[...] = _dot(h, w_ref[...], NN)

    return pl.pallas_call(
        body, name="inproj", grid=(L // TL,),
        in_specs=[pl.BlockSpec((TL, D), lambda i: (i, 0)), pl.BlockSpec((1, D), lambda i: (0, 0)),
                  pl.BlockSpec((D, PW), lambda i: (0, 0))] + _dep_specs(deps),
        out_specs=[pl.BlockSpec((TL, PW), lambda i: (i, 0)), pl.BlockSpec((TL, D), lambda i: (i, 0))],
        out_shape=[SDS((L, PW), F32), SDS((L, D), BF16)],
        compiler_params=_cparams(("parallel",)),
    )(x, g, w, *deps)


def _outproj_call(x, ys, w):
    L = x.shape[0]

    def body(x_ref, a_ref, b_ref, c_ref, d_ref, w_ref, o_ref):
        mixed = jnp.concatenate([y_ref[...].astype(BF16) for y_ref in (a_ref, b_ref, c_ref, d_ref)], axis=1)
        o_ref[...] = x_ref[...] + _dot(mixed, w_ref[...], NN)

    yspec = pl.BlockSpec((TL, BR), lambda i: (i, 0))
    return pl.pallas_call(
        body, name="outproj", grid=(L // TL,),
        in_specs=[pl.BlockSpec((TL, D), lambda i: (i, 0)), yspec, yspec, yspec, yspec,
                  pl.BlockSpec((D, D), lambda i: (0, 0))],
        out_specs=pl.BlockSpec((TL, D), lambda i: (i, 0)),
        out_shape=SDS((L, D), F32),
        compiler_params=_cparams(("parallel",)),
    )(x, *ys, w)


def _loss_call(x, g, target):
    L = x.shape[0]

    def body(x_ref, g_ref, t_ref, dx_ref, dg_ref, loss_ref):
        @pl.when(pl.program_id(0) == 0)
        def _():
            dg_ref[...] = jnp.zeros_like(dg_ref)
            loss_ref[...] = jnp.zeros_like(loss_ref)

        y, vjp = jax.vjp(_rms, x_ref[...], g_ref[...])
        err = y - t_ref[...]
        dx, dg = vjp(err * (1.0 / D))
        dx_ref[...] = dx
        dg_ref[...] += dg
        tot = jnp.sum(jnp.sum(err * err, axis=1, keepdims=True), axis=0, keepdims=True)
        loss_ref[...] += jnp.broadcast_to(tot * (0.5 / D), loss_ref.shape)

    return pl.pallas_call(
        body, name="loss_head", grid=(L // TL,),
        in_specs=[pl.BlockSpec((TL, D), lambda i: (i, 0)), pl.BlockSpec((1, D), lambda i: (0, 0)),
                  pl.BlockSpec((TL, D), lambda i: (i, 0))],
        out_specs=[pl.BlockSpec((TL, D), lambda i: (i, 0)), pl.BlockSpec((1, D), lambda i: (0, 0)),
                   pl.BlockSpec((1, 128), lambda i: (0, 0))],
        out_shape=[SDS((L, D), F32), SDS((1, D), F32), SDS((1, 128), F32)],
        compiler_params=_cparams(("arbitrary",)),
    )(x, g, target)


def _branch_a(valw, gatew, z, cw, cb, lg, lb, pw, pb, conv):
    a = conv(valw * _sigmoid(gatew), cw) + cb
    mu = jnp.mean(a, axis=-1, keepdims=True)
    xc = a - mu
    y = xc * lax.rsqrt(jnp.mean(xc * xc, axis=-1, keepdims=True) + EPS) * lg + lb
    y = mm(_silu(y), pw) + pb
    return y * _silu(z)


def _branch_c(bg, cw_, xw, z, w3, conv):
    return bg * conv(cw_ * xw, w3) * _silu(z)


def _ac_fwd_call(proj, p):
    L = proj.shape[0]

    def body(val, gate, za, hval, hgate, cb_, cc, cx, cz, hcc, hcx,
             acw, acb, alg, alb, apw, apb, ccw, ya_ref, yc_ref):
        nf = (pl.program_id(0) > 0).astype(F32)
        win = lambda h, m: jnp.concatenate([h[...] * nf, m[...]], axis=0)
        conv_a = functools.partial(_conv_fwd_impl, tile=TL, halo=HALO, taps=KA)
        conv_c = functools.partial(_conv_fwd_impl, tile=TL, halo=HALO, taps=KC)
        ya_ref[...] = _branch_a(win(hval, val), win(hgate, gate), za[...], acw[...], acb[...], alg[...], alb[...],
                                apw[...], apb[...], conv_a)
        yc_ref[...] = _branch_c(cb_[...], win(hcc, cc), win(hcx, cx), cz[...], ccw[...], conv_c)

    col = lambda j: pl.BlockSpec((TL, BR), lambda i: (i, j))
    hal = lambda j: _halo_spec(TL, HALO, BR, j)
    full = lambda a: pl.BlockSpec(a.shape, lambda i: (0,) * a.ndim)
    params = (p["a_conv_w"], p["a_conv_b"], p["a_ln_g"], p["a_ln_b"], p["a_pw_w"], p["a_pw_b"], p["c_conv_w"])
    return pl.pallas_call(
        body, name="ac_fwd", grid=(L // TL,),
        in_specs=[col(0), col(1), col(2), hal(0), hal(1), col(5), col(6), col(7), col(8), hal(6), hal(7)]
        + [full(a) for a in params],
        out_specs=[pl.BlockSpec((TL, BR), lambda i: (i, 0))] * 2,
        out_shape=[SDS((L, BR), F32)] * 2,
        compiler_params=_cparams(("parallel",)),
    )(*([proj] * 11), *params)


def _ac_bwd_call(proj, p, dya, dyc):
    L = proj.shape[0]
    nt = L // TL

    def body(val, gate, za, hval, hgate, cb_, cc, cx, cz, hcc, hcx,
             acw, acb, alg, alb, apw, apb, ccw, dya_ref, dyc_ref,
             da_ref, dc_ref, g_acw, g_acb, g_alg, g_alb, g_apw, g_apb, g_ccw, carry):
        i = pl.program_id(0)
        gouts = (g_acw, g_acb, g_alg, g_alb, g_apw, g_apb, g_ccw)

        @pl.when(i == 0)
        def _():
            carry[...] = jnp.zeros_like(carry)
            for r in gouts:
                r[...] = jnp.zeros_like(r)

        nf = (i < nt - 1).astype(F32)
        win = lambda h, m: jnp.concatenate([h[...] * nf, m[...]], axis=0)
        conv_a = _make_conv(TL, HALO, KA)
        conv_c = _make_conv(TL, HALO, KC)

        def f(valw, gatew, z, bg, ccw_, cxw, czv, w1, b1, lg, lb, pw, pb, w3):
            return (_branch_a(valw, gatew, z, w1, b1, lg, lb, pw, pb, conv_a),
                    _branch_c(bg, ccw_, cxw, czv, w3, conv_c))

        _, vjp = jax.vjp(f, win(hval, val), win(hgate, gate), za[...], cb_[...], win(hcc, cc), win(hcx, cx), cz[...],
                         acw[...], acb[...], alg[...], alb[...], apw[...].astype(F32), apb[...], ccw[...])
        (dvalw, dgatew, dz, dbg, dccw, dcxw, dczv, d1, d2, d3, d4, d5, d6, d7) = vjp((dya_ref[...], dyc_ref[...]))

        def settle(slot, dwin):
            tail = jnp.concatenate([jnp.zeros((TL - HALO, BR), F32), carry[slot]], axis=0)
            carry[slot] = dwin[:HALO, :]
            return (dwin[HALO:, :] + tail).astype(BF16)

        da_ref[:, 0:BR] = settle(0, dvalw)
        da_ref[:, BR:2 * BR] = settle(1, dgatew)
        da_ref[:, 2 * BR:3 * BR] = dz.astype(BF16)
        dc_ref[:, 0:BR] = dbg.astype(BF16)
        dc_ref[:, BR:2 * BR] = settle(2, dccw)
        dc_ref[:, 2 * BR:3 * BR] = settle(3, dcxw)
        dc_ref[:, 3 * BR:4 * BR] = dczv.astype(BF16)
        for r, g in zip(gouts, (d1, d2, d3, d4, d5, d6, d7)):
            r[...] += g

    col = lambda j: pl.BlockSpec((TL, BR), lambda i: (nt - 1 - i, j))
    hal = lambda j: _halo_spec_rev(nt, TL, HALO, BR, j)
    full = lambda a: pl.BlockSpec(a.shape, lambda i: (0,) * a.ndim)
    params = (p["a_conv_w"], p["a_conv_b"], p["a_ln_g"], p["a_ln_b"], p["a_pw_w"], p["a_pw_b"], p["c_conv_w"])
    rev = lambda w: pl.BlockSpec((TL, w), lambda i: (nt - 1 - i, 0))
    return pl.pallas_call(
        body, name="ac_bwd", grid=(nt,),
        in_specs=[col(0), col(1), col(2), hal(0), hal(1), col(5), col(6), col(7), col(8), hal(6), hal(7)]
        + [full(a) for a in params] + [rev(BR), rev(BR)],
        out_specs=[rev(3 * BR), rev(4 * BR)] + [full(a) for a in params],
        out_shape=[SDS((L, 3 * BR), BF16), SDS((L, 4 * BR), BF16)] + [SDS(a.shape, F32) for a in params],
        scratch_shapes=[pltpu.VMEM((4, HALO, BR), F32)],
        compiler_params=_cparams(("arbitrary",)),
    )(*([proj] * 11), *params, dya, dyc)


def _iota2(shape, dim):
    return lax.broadcasted_iota(jnp.int32, shape, dim)


def _s5_params(lam_re, lam_im, logdt, b_re, b_im, c_re, c_im):
    eg = (_iota2((16, NSTATE), 1) >> 6 == _iota2((16, NSTATE), 0)).astype(F32)
    dt = jnp.exp(mmh(jnp.broadcast_to(logdt, (8, 16)), eg)[0:1, :])
    lr = jnp.minimum(lam_re, -1e-4)
    li = lam_im
    mag = jnp.exp(lr * dt)
    lbr = mag * jnp.cos(li * dt)
    lbi = mag * jnp.sin(li * dt)
    den = lr * lr + li * li
    nr = lbr - 1.0
    fr = (nr * lr + lbi * li) / den
    fi = (lbi * lr - nr * li) / den
    row = _iota2((8, NSTATE), 0)
    f8 = jnp.where(row == 0, fr, jnp.where(row == 1, fi, 0.0))
    eye = (_iota2((NSTATE, NSTATE), 0) == _iota2((NSTATE, NSTATE), 1)).astype(F32)
    fcol = mmh_nt(eye, f8)
    frc, fic = fcol[:, 0:1], fcol[:, 1:2]
    bbr = frc * b_re - fic * b_im
    bbi = frc * b_im + fic * b_re
    e1 = ((_iota2((16, BR), 1) & 15) == _iota2((16, BR), 0)).astype(F32)
    m1 = ((_iota2((NSTATE, BR), 0) >> 6) == (_iota2((NSTATE, BR), 1) >> 4)).astype(F32)
    wbr = mmh(bbr, e1) * m1
    wbi = mmh(bbi, e1) * m1
    e2 = ((_iota2((64, NSTATE), 1) & 63) == _iota2((64, NSTATE), 0)).astype(F32)
    m2 = ((_iota2((BR, NSTATE), 0) >> 4) == (_iota2((BR, NSTATE), 1) >> 6)).astype(F32)
    wcr = mmh(c_re, e2) * m2
    wci = mmh(c_im, e2) * m2
    return lbr, lbi, wbr, wbi, wcr, wci


_S5_OUT = [(1, NSTATE), (1, NSTATE), (NSTATE, BR), (NSTATE, BR), (BR, NSTATE), (BR, NSTATE)]


def _s5_prep_call(sp, deps=()):
    def body(lre, lim, ldt, bre, bim, cre, cim, *rest):
        o_lbr, o_lbi, o_wbr, o_wbi, o_wcr, o_wci, pwr, pwi, qwr, qwi = rest[len(deps):]
        lbr, lbi, wbr, wbi, wcr, wci = _s5_params(lre[...], lim[...], ldt[...], bre[...], bim[...], cre[...], cim[...])
        o_lbr[...], o_lbi[...], o_wbr[...], o_wbi[...], o_wcr[...], o_wci[...] = lbr, lbi, wbr, wbi, wcr, wci
        pr, pi = lbr, lbi
        for i in range(SEG):
            pwr[i:i + 1, :] = pr
            pwi[i:i + 1, :] = pi
            qwr[SEG - 1 - i:SEG - i, :] = pr
            qwi[SEG - 1 - i:SEG - i, :] = -pi
            pr, pi = pr * lbr - pi * lbi, pr * lbi + pi * lbr

    args = (sp["lam_re"], sp["lam_im"], sp["log_dt"], sp["b_re"], sp["b_im"], sp["c_re"], sp["c_im"])
    return pl.pallas_call(
        body, name="s5_prep",
        out_shape=[SDS(s, F32) for s in _S5_OUT] + [SDS((SEG, NSTATE), F32)] * 4,
        compiler_params=pltpu.CompilerParams(vmem_limit_bytes=VMEM_LIMIT),
    )(*args, *deps)


def _s5_prep_bwd_call(sp, cots):
    def body(lre, lim, ldt, bre, bim, cre, cim, c0, c1, c2, c3, c4, c5, *outs):
        _, vjp = jax.vjp(_s5_params, lre[...], lim[...], ldt[...], bre[...], bim[...], cre[...], cim[...])
        grads = vjp((c0[...], c1[...], c2[...], c3[...], c4[...], c5[...]))
        for o, g in zip(outs, grads):
            o[...] = g

    args = (sp["lam_re"], sp["lam_im"], sp["log_dt"], sp["b_re"], sp["b_im"], sp["c_re"], sp["c_im"])
    return pl.pallas_call(
        body, name="s5_prep_bwd",
        out_shape=[SDS(a.shape, F32) for a in args],
        compiler_params=pltpu.CompilerParams(vmem_limit_bytes=VMEM_LIMIT),
    )(*args, *cots)


def _lanes(v, j):
    return v[:, j * 128:(j + 1) * 128]


def _s5_scan(sre, sim, pwr, pwi, cin_r, cin_i, reverse):
    row = _iota2((8, 128), 0)
    steps = (1, 2, 4)

    def lane_consts(j):
        lanes = slice(j * 128, (j + 1) * 128)
        if reverse:
            mult = [(jnp.broadcast_to(pwr[SEG - d:SEG - d + 1, lanes], (8, 128)),
                     jnp.broadcast_to(pwi[SEG - d:SEG - d + 1, lanes], (8, 128))) for d in steps]
            return mult, pwr[SEG - 8:SEG, lanes], pwi[SEG - 8:SEG, lanes]
        mult = [(jnp.broadcast_to(pwr[d - 1:d, lanes], (8, 128)),
                 jnp.broadcast_to(pwi[d - 1:d, lanes], (8, 128))) for d in steps]
        return mult, pwr[0:8, lanes], pwi[0:8, lanes]

    consts = [lane_consts(j) for j in range(8)]
    nblk = TL // 8

    def block(t, carry):
        b = nblk - 1 - t if reverse else t
        rows = pl.ds(pl.multiple_of(b * 8, 8), 8)
        new = []
        for j in range(8):
            mult, p8r, p8i = consts[j]
            vr, vi = sre.at[j], sim.at[j]
            sr, si = vr[rows, :], vi[rows, :]
            for d, (mr, mi) in zip(steps, mult):
                if reverse:
                    hr = jnp.where(row < 8 - d, pltpu.roll(sr, 8 - d, 0), 0.0)
                    hi = jnp.where(row < 8 - d, pltpu.roll(si, 8 - d, 0), 0.0)
                else:
                    hr = jnp.where(row >= d, pltpu.roll(sr, d, 0), 0.0)
                    hi = jnp.where(row >= d, pltpu.roll(si, d, 0), 0.0)
                sr, si = sr + mr * hr - mi * hi, si + mr * hi + mi * hr
            cr, ci = carry[2 * j], carry[2 * j + 1]
            sr, si = sr + p8r * cr - p8i * ci, si + p8r * ci + p8i * cr
            vr[rows, :] = sr
            vi[rows, :] = si
            edge = slice(0, 1) if reverse else slice(7, 8)
            new += [sr[edge, :], si[edge, :]]
        return tuple(new)

    init = []
    for j in range(8):
        init += [_lanes(cin_r, j), _lanes(cin_i, j)]
    ends = lax.fori_loop(0, nblk, block, tuple(init))
    return (jnp.concatenate([ends[2 * j] for j in range(8)], axis=1),
            jnp.concatenate([ends[2 * j + 1] for j in range(8)], axis=1))


def _bdot(a, b, dims):
    return _dot(a.astype(BF16), b.astype(BF16), dims)


def _s5_states(u, wbr, wbi, sre, sim):
    bur = _bdot(u, wbr, NT)
    bui = _bdot(u, wbi, NT)
    for j in range(8):
        sre[j] = _lanes(bur, j)
        sim[j] = _lanes(bui, j)


def _gather_lanes(s):
    return jnp.concatenate([s[j] for j in range(8)], axis=1)


def _s5_post(s_re, s_im, u, z, wcr, wci, dsk, gw, gb):
    y = mm_nt(s_re, wcr) - mm_nt(s_im, wci) + dsk * u
    yg = _gelu(y)
    return yg * _sigmoid(mm(yg, gw) + gb) * _silu(z)


def _s5_fwd_call(proj, prep, p):
    L = proj.shape[0]
    nt = L // TL
    lbr, lbi, wbr, wbi, wcr, wci, pwr, pwi, _, _ = prep

    def body(u_ref, z_ref, lbr_r, lbi_r, wbr_r, wbi_r, wcr_r, wci_r, pwr_r, pwi_r, d_r, gw_r, gb_r,
             yb_ref, cinr_ref, cini_ref, sre, sim, car, cai):
        @pl.when(pl.program_id(0) == 0)
        def _():
            car[...] = jnp.zeros_like(car)
            cai[...] = jnp.zeros_like(cai)

        u = u_ref[...]
        cinr_ref[0] = car[...]
        cini_ref[0] = cai[...]
        _s5_states(u, wbr_r[...], wbi_r[...], sre, sim)
        nr, ni = _s5_scan(sre, sim, pwr_r, pwi_r, car[...], cai[...], False)
        car[...] = nr
        cai[...] = ni
        yb_ref[...] = _s5_post(_gather_lanes(sre), _gather_lanes(sim), u, z_ref[...], wcr_r[...], wci_r[...],
                               d_r[...], gw_r[...], gb_r[...])

    full = lambda a: pl.BlockSpec(a.shape, lambda i: (0,) * a.ndim)
    consts = (lbr, lbi, wbr, wbi, wcr, wci, pwr, pwi, p["s5_d"], p["s5_glu_w"], p["s5_glu_b"])
    cspec = pl.BlockSpec((1, 1, NSTATE), lambda i: (i, 0, 0))
    return pl.pallas_call(
        body, name="s5_fwd", grid=(nt,),
        in_specs=[pl.BlockSpec((TL, BR), lambda i: (i, 3)), pl.BlockSpec((TL, BR), lambda i: (i, 4))]
        + [full(a) for a in consts],
        out_specs=[pl.BlockSpec((TL, BR), lambda i: (i, 0)), cspec, cspec],
        out_shape=[SDS((L, BR), F32), SDS((nt, 1, NSTATE), F32), SDS((nt, 1, NSTATE), F32)],
        scratch_shapes=[pltpu.VMEM((8, TL, 128), F32), pltpu.VMEM((8, TL, 128), F32),
                        pltpu.VMEM((1, NSTATE), F32), pltpu.VMEM((1, NSTATE), F32)],
        compiler_params=_cparams(("arbitrary",)),
    )(proj, proj, *consts)


def _s5_bwd_call(proj, prep, p, cin_r, cin_i, dyb):
    L = proj.shape[0]
    nt = L // TL
    lbr, lbi, wbr, wbi, wcr, wci, pwr, pwi, qwr, qwi = prep

    def body(u_ref, z_ref, lbr_r, lbi_r, wbr_r, wbi_r, wcr_r, wci_r, pwr_r, pwi_r, qwr_r, qwi_r, d_r, gw_r, gb_r,
             cinr_ref, cini_ref, dy_ref,
             db_ref, g_lbr, g_lbi, g_wbr, g_wbi, g_wcr, g_wci, g_d, g_gw, g_gb, sre, sim, gre, gim, car, cai):
        gouts = (g_lbr, g_lbi, g_wbr, g_wbi, g_wcr, g_wci, g_d, g_gw, g_gb)

        @pl.when(pl.program_id(0) == 0)
        def _():
            car[...] = jnp.zeros_like(car)
            cai[...] = jnp.zeros_like(cai)
            for r in gouts:
                r[...] = jnp.zeros_like(r)

        u = u_ref[...]
        lr, li = lbr_r[...], lbi_r[...]
        c0r, c0i = cinr_ref[0], cini_ref[0]
        _s5_states(u, wbr_r[...], wbi_r[...], sre, sim)
        _s5_scan(sre, sim, pwr_r, pwi_r, c0r, c0i, False)
        s_re, s_im = _gather_lanes(sre), _gather_lanes(sim)
        _, vjp = jax.vjp(_s5_post, s_re, s_im, u, z_ref[...], wcr_r[...], wci_r[...], d_r[...],
                         gw_r[...].astype(F32), gb_r[...])
        ds_re, ds_im, du, dz, dwcr, dwci, dd, dgw, dgb = vjp(dy_ref[...])
        for j in range(8):
            gre[j] = _lanes(ds_re, j)
            gim[j] = _lanes(ds_im, j)
        nr, ni = _s5_scan(gre, gim, qwr_r, qwi_r, car[...], cai[...], True)
        car[...] = nr
        cai[...] = ni
        a_re, a_im = _gather_lanes(gre), _gather_lanes(gim)
        first = _iota2((TL, NSTATE), 0) == 0
        p_re = jnp.where(first, c0r, jnp.roll(s_re, 1, axis=0))
        p_im = jnp.where(first, c0i, jnp.roll(s_im, 1, axis=0))
        g_lbr[...] += jnp.sum(a_re * p_re + a_im * p_im, axis=0, keepdims=True)
        g_lbi[...] += jnp.sum(a_im * p_re - a_re * p_im, axis=0, keepdims=True)
        du = du + _bdot(a_re, wbr_r[...], NN) + _bdot(a_im, wbi_r[...], NN)
        g_wbr[...] += _bdot(a_re, u, TN)
        g_wbi[...] += _bdot(a_im, u, TN)
        g_wcr[...] += dwcr
        g_wci[...] += dwci
        g_d[...] += dd
        g_gw[...] += dgw
        g_gb[...] += dgb
        db_ref[:, 0:BR] = du.astype(BF16)
        db_ref[:, BR:2 * BR] = dz.astype(BF16)

    full = lambda a: pl.BlockSpec(a.shape, lambda i: (0,) * a.ndim)
    consts = (lbr, lbi, wbr, wbi, wcr, wci, pwr, pwi, qwr, qwi, p["s5_d"], p["s5_glu_w"], p["s5_glu_b"])
    cspec = pl.BlockSpec((1, 1, NSTATE), lambda i: (nt - 1 - i, 0, 0))
    gshapes = _S5_OUT + [(1, BR), (BR, BR), (1, BR)]
    return pl.pallas_call(
        body, name="s5_bwd", grid=(nt,),
        in_specs=[pl.BlockSpec((TL, BR), lambda i: (nt - 1 - i, 3)), pl.BlockSpec((TL, BR), lambda i: (nt - 1 - i, 4))]
        + [full(a) for a in consts] + [cspec, cspec, pl.BlockSpec((TL, BR), lambda i: (nt - 1 - i, 0))],
        out_specs=[pl.BlockSpec((TL, 2 * BR), lambda i: (nt - 1 - i, 0))]
        + [pl.BlockSpec(s, lambda i: (0, 0)) for s in gshapes],
        out_shape=[SDS((L, 2 * BR), BF16)] + [SDS(s, F32) for s in gshapes],
        scratch_shapes=[pltpu.VMEM((8, TL, 128), F32)] * 4 + [pltpu.VMEM((1, NSTATE), F32)] * 2,
        compiler_params=_cparams(("arbitrary",)),
    )(proj, proj, *consts, cin_r, cin_i, dyb)


def _heads(x):
    return [x[:, h * HD:(h + 1) * HD] for h in range(NH)]


def _l2n(x, scale):
    return jnp.concatenate([xh * (lax.rsqrt(jnp.sum(xh * xh, axis=-1, keepdims=True) + EPS) * scale)
                            for xh in _heads(x)], axis=1)


def _dn_pre(qkvw, ab, cw, alog, dtb, conv, rows):
    c = _silu(conv(qkvw, cw))
    q = _l2n(c[:, 0:BR], HD ** -0.5)
    k = _l2n(c[:, BR:2 * BR], 1.0)
    v = c[:, 2 * BR:3 * BR]
    g = -jnp.exp(alog) * _softplus(ab + dtb)
    ri, ci = _iota2((rows, rows), 0), _iota2((rows, rows), 1)
    tri = ((ri >= ci) & ((ri >> 6) == (ci >> 6))).astype(F32)
    gc = mmh(tri, g)
    lane = _iota2(ab.shape, 1)
    return q, k, v, jnp.where(lane < NH, gc, jnp.where(lane < 2 * NH, _sigmoid(ab), 0.0))


def _dn_pre_fwd_call(proj, p):
    L = proj.shape[0]

    def body(m_ref, h_ref, ab_ref, cw, alog, dtb, q_ref, k_ref, v_ref, gb_ref):
        nf = (pl.program_id(0) > 0).astype(F32)
        qkvw = jnp.concatenate([h_ref[...] * nf, m_ref[...]], axis=0)
        conv = functools.partial(_conv_fwd_impl, tile=TL, halo=HALO_S, taps=KD)
        q_ref[...], k_ref[...], v_ref[...], gb_ref[...] = _dn_pre(qkvw, ab_ref[...], cw[...], alog[...], dtb[...], conv, TL)

    full = lambda a: pl.BlockSpec(a.shape, lambda i: (0,) * a.ndim)
    params = (p["d_conv_w"], p["d_a_log"], p["d_dt_bias"])
    o = pl.BlockSpec((TL, BR), lambda i: (i, 0))
    return pl.pallas_call(
        body, name="dn_pre_fwd", grid=(L // TL,),
        in_specs=[pl.BlockSpec((TL, 3 * BR), lambda i: (i, 3)), _halo_spec(TL, HALO_S, 3 * BR, 3),
                  pl.BlockSpec((TL, 128), lambda i: (i, AB_COL // 128))] + [full(a) for a in params],
        out_specs=[o, o, o, pl.BlockSpec((TL, 128), lambda i: (i, 0))],
        out_shape=[SDS((L, BR), F32)] * 3 + [SDS((L, 128), F32)],
        compiler_params=_cparams(("parallel",)),
    )(proj, proj, proj, *params)


def _dn_pre_bwd_call(proj, p, dq, dk, dv, dgb):
    L = proj.shape[0]
    nt = L // TL

    def body(m_ref, h_ref, ab_ref, cw, alog, dtb, dq_r, dk_r, dv_r, dgb_r,
             dqkv_ref, dab_ref, g_cw, g_alog, g_dtb, carry):
        i = pl.program_id(0)

        @pl.when(i == 0)
        def _():
            carry[...] = jnp.zeros_like(carry)
            for r in (g_cw, g_alog, g_dtb):
                r[...] = jnp.zeros_like(r)

        nf = (i < nt - 1).astype(F32)
        qkvw = jnp.concatenate([h_ref[...] * nf, m_ref[...]], axis=0)
        conv = _make_conv(TL, HALO_S, KD)
        _, vjp = jax.vjp(lambda a, b, c, d, e: _dn_pre(a, b, c, d, e, conv, TL),
                         qkvw, ab_ref[...], cw[...], alog[...], dtb[...])
        dwin, dab, dcw, dalog, ddtb = vjp((dq_r[...], dk_r[...], dv_r[...], dgb_r[...]))
        tail = jnp.concatenate([jnp.zeros((TL - HALO_S, 3 * BR), F32), carry[...]], axis=0)
        carry[...] = dwin[:HALO_S, :]
        dqkv_ref[...] = (dwin[HALO_S:, :] + tail).astype(BF16)
        dab_ref[...] = dab.astype(BF16)
        g_cw[...] += dcw
        g_alog[...] += dalog
        g_dtb[...] += ddtb

    full = lambda a: pl.BlockSpec(a.shape, lambda i: (0,) * a.ndim)
    params = (p["d_conv_w"], p["d_a_log"], p["d_dt_bias"])
    rev = lambda w: pl.BlockSpec((TL, w), lambda i: (nt - 1 - i, 0))
    return pl.pallas_call(
        body, name="dn_pre_bwd", grid=(nt,),
        in_specs=[pl.BlockSpec((TL, 3 * BR), lambda i: (nt - 1 - i, 3)), _halo_spec_rev(nt, TL, HALO_S, 3 * BR, 3),
                  pl.BlockSpec((TL, 128), lambda i: (nt - 1 - i, AB_COL // 128))] + [full(a) for a in params]
        + [rev(BR), rev(BR), rev(BR), rev(128)],
        out_specs=[rev(3 * BR), rev(128)] + [full(a) for a in params],
        out_shape=[SDS((L, 3 * BR), BF16), SDS((L, 128), BF16)] + [SDS(a.shape, F32) for a in params],
        scratch_shapes=[pltpu.VMEM((HALO_S, 3 * BR), F32)],
        compiler_params=_cparams(("arbitrary",)),
    )(proj, proj, proj, *params, dq, dk, dv, dgb)


def _dn_group(q, k, v, gb, z, ng, *s):
    return _dn_chunks(q, k, v, gb, z, ng, s, DN_GROUP, None)


def _dn_chunks(q, k, v, gb, z, ng, s, chunks, entering):
    ri, ci = _iota2((CH, CH), 0), _iota2((CH, CH), 1)
    causal, strict = ri >= ci, ri > ci
    eye = (ri == ci).astype(F32)
    s = list(s)
    pairs = []
    for c in range(chunks):
        rows = slice(c * CH, (c + 1) * CH)
        gbc = gb[rows, :]
        for h, (qh, kh, vh, zh) in enumerate(zip(_heads(q[rows, :]), _heads(k[rows, :]), _heads(v[rows, :]),
                                                 _heads(z[rows, :]))):
            gc = jnp.broadcast_to(gbc[:, h:h + 1], (CH, HD))
            beta = gbc[:, NH + h:NH + h + 1]
            decay = jnp.where(causal, jnp.exp(jnp.where(causal, gc - gc.T, 0.0)), 0.0)
            egc = jnp.exp(gc)
            glast = gc[CH - 1:CH, :]
            kb = kh * beta
            pairs.append(dict(q=qh, k=kh, z=zh, decay=decay, qe=qh * egc, kd=kh * jnp.exp(glast - gc),
                              sdec=jnp.exp(glast[:, 0:1]), kb=kb, rhs=jnp.concatenate([vh * beta, kb * egc], axis=1)))
    for p in pairs:
        p["pw"] = jnp.where(strict, mm_nt(p["kb"], p["k"]) * p["decay"], 0.0)
    for p in pairs:
        p["t"] = eye - p["pw"]
    for _ in range(5):
        for p in pairs:
            p["pw"] = mm(p["pw"], p["pw"])
        for p in pairs:
            p["t"] = mm(p["t"], eye + p["pw"])
    for p in pairs:
        p["uw"] = mm(p["t"], p["rhs"])
    for p in pairs:
        p["attn"] = mm_nt(p["q"], p["k"]) * p["decay"]
    out_rows = []
    for c in range(chunks):
        if entering is not None and c > 0 and c % DN_GROUP == 0:
            entering.append(list(s))
        grp = pairs[c * NH:(c + 1) * NH]
        ws = [mm(jnp.concatenate([p["uw"][:, HD:], p["qe"]], axis=0), s[h]) for h, p in enumerate(grp)]
        v_new = [p["uw"][:, :HD] - w_[:CH, :] for p, w_ in zip(grp, ws)]
        o = [w_[CH:, :] + mm(p["attn"], vn) for p, w_, vn in zip(grp, ws, v_new)]
        s = [s[h] * p["sdec"] + mm_tn(p["kd"], vn) for h, (p, vn) in enumerate(zip(grp, v_new))]
        o = [oh * lax.rsqrt(jnp.mean(oh * oh, axis=-1, keepdims=True) + EPS) * ng * _silu(p["z"]) for oh, p in zip(o, grp)]
        out_rows.append(jnp.concatenate(o, axis=1))
    return (jnp.concatenate(out_rows, axis=0), *s)


def _dn_core_fwd_call(proj, q, k, v, gb, ng):
    L = q.shape[0]
    rows = DN_FWD_GROUPS * DN_GROUP * CH
    steps = L // rows

    def body(q_r, k_r, v_r, gb_r, z_r, ng_r, yd_ref, ssave_ref, s_scr):
        @pl.when(pl.program_id(0) == 0)
        def _():
            s_scr[...] = jnp.zeros_like(s_scr)

        entering = [[s_scr[h] for h in range(NH)]]
        yd, *s2 = _dn_chunks(q_r[...], k_r[...], v_r[...], gb_r[...], z_r[...], ng_r[...], entering[0],
                             DN_FWD_GROUPS * DN_GROUP, entering)
        yd_ref[...] = yd
        for h in range(NH):
            s_scr[h] = s2[h]
            for g, states in enumerate(entering):
                ssave_ref[g, h] = states[h]

    c = pl.BlockSpec((rows, BR), lambda i: (i, 0))
    return pl.pallas_call(
        body, name="dn_core_fwd", grid=(steps,),
        in_specs=[c, c, c, pl.BlockSpec((rows, 128), lambda i: (i, 0)), pl.BlockSpec((rows, BR), lambda i: (i, 12)),
                  pl.BlockSpec((1, HD), lambda i: (0, 0))],
        out_specs=[c, pl.BlockSpec((DN_FWD_GROUPS, NH, HD, HD), lambda i: (i, 0, 0, 0))],
        out_shape=[SDS((L, BR), F32), SDS((steps * DN_FWD_GROUPS, NH, HD, HD), F32)],
        scratch_shapes=[pltpu.VMEM((NH, HD, HD), F32)],
        compiler_params=_cparams(("arbitrary",)),
    )(q, k, v, gb, proj, ng)


def _dn_core_bwd_call(proj, q, k, v, gb, ng, ssave, dyd):
    L = q.shape[0]
    rows = DN_GROUP * CH
    ng_ = L // rows

    def body(q_r, k_r, v_r, gb_r, z_r, ng_r, s_r, dy_r, dq_ref, dk_ref, dv_ref, dgb_ref, dz_ref, g_ng, ds_scr):
        @pl.when(pl.program_id(0) == 0)
        def _():
            ds_scr[...] = jnp.zeros_like(ds_scr)
            g_ng[...] = jnp.zeros_like(g_ng)

        _, vjp = jax.vjp(_dn_group, q_r[...], k_r[...], v_r[...], gb_r[...], z_r[...], ng_r[...],
                         *[s_r[0, h] for h in range(NH)])
        dq, dk, dv, dgb, dz, dng, *ds = vjp((dy_r[...], *[ds_scr[h] for h in range(NH)]))
        dq_ref[...], dk_ref[...], dv_ref[...], dgb_ref[...] = dq, dk, dv, dgb
        dz_ref[...] = dz.astype(BF16)
        g_ng[...] += dng
        for h in range(NH):
            ds_scr[h] = ds[h]

    c = pl.BlockSpec((rows, BR), lambda i: (ng_ - 1 - i, 0))
    c128 = pl.BlockSpec((rows, 128), lambda i: (ng_ - 1 - i, 0))
    return pl.pallas_call(
        body, name="dn_core_bwd", grid=(ng_,),
        in_specs=[c, c, c, c128, pl.BlockSpec((rows, BR), lambda i: (ng_ - 1 - i, 12)),
                  pl.BlockSpec((1, HD), lambda i: (0, 0)),
                  pl.BlockSpec((1, NH, HD, HD), lambda i: (ng_ - 1 - i, 0, 0, 0)), c],
        out_specs=[c, c, c, c128, c, pl.BlockSpec((1, HD), lambda i: (0, 0))],
        out_shape=[SDS((L, BR), F32)] * 3 + [SDS((L, 128), F32), SDS((L, BR), BF16), SDS((1, HD), F32)],
        scratch_shapes=[pltpu.VMEM((NH, HD, HD), F32)],
        compiler_params=_cparams(("arbitrary",)),
    )(q, k, v, gb, proj, ng, ssave, dyd)


def _outproj_bwd_call(dx, ys, w, deps=()):
    L = dx.shape[0]

    def body(dx_ref, a_ref, b_ref, c_ref, d_ref, w_ref, *rest):
        da, db, dc, dd, dw_ref = rest[len(deps):]

        @pl.when(pl.program_id(0) == 0)
        def _():
            dw_ref[...] = jnp.zeros_like(dw_ref)

        dxb = dx_ref[...].astype(BF16)
        dmixed = _dot(dxb, w_ref[...], NT)
        for b, o_ref in enumerate((da, db, dc, dd)):
            o_ref[...] = dmixed[:, b * BR:(b + 1) * BR]
        mixed = jnp.concatenate([y_ref[...].astype(BF16) for y_ref in (a_ref, b_ref, c_ref, d_ref)], axis=1)
        dw_ref[...] += _dot(mixed, dxb, TN)

    yspec = pl.BlockSpec((TL, BR), lambda i: (i, 0))
    return pl.pallas_call(
        body, name="outproj_bwd", grid=(L // TL,),
        in_specs=[pl.BlockSpec((TL, D), lambda i: (i, 0)), yspec, yspec, yspec, yspec,
                  pl.BlockSpec((D, D), lambda i: (0, 0))] + _dep_specs(deps),
        out_specs=[yspec] * 4 + [pl.BlockSpec((D, D), lambda i: (0, 0))],
        out_shape=[SDS((L, BR), F32)] * 4 + [SDS((D, D), F32)],
        compiler_params=_cparams(("arbitrary",)),
    )(dx, *ys, w, *deps)


def _slab_cols(slabs):
    widths = [s.shape[1] for s in slabs]
    starts = [sum(widths[:i]) for i in range(len(widths))]
    assert starts[-1] + widths[-1] == PW
    return list(zip(starts, widths))


def _inproj_bwd_x_call(slabs, w, x, g, dx_next, deps=()):
    L = x.shape[0]
    cols = _slab_cols(slabs)
    n = len(slabs)

    def body(*refs):
        dp_refs, (w_ref, x_ref, g_ref, dxn_ref) = refs[:n], refs[n:n + 4]
        dx_ref, dg_ref = refs[n + 4 + len(deps):]

        @pl.when(pl.program_id(0) == 0)
        def _():
            dg_ref[...] = jnp.zeros_like(dg_ref)

        dh = _dot(jnp.concatenate([dp_ref[...] for dp_ref in dp_refs], axis=1), w_ref[...], NT)
        _, vjp = jax.vjp(_rms, x_ref[...], g_ref[...])
        dx, dg = vjp(dh)
        dx_ref[...] = dx + dxn_ref[...]
        dg_ref[...] += dg

    row = lambda w_: pl.BlockSpec((TL, w_), lambda i: (i, 0))
    return pl.pallas_call(
        body, name="inproj_bwd_x", grid=(L // TL,),
        in_specs=[row(cw) for _, cw in cols]
        + [pl.BlockSpec((D, PW), lambda i: (0, 0)), row(D), pl.BlockSpec((1, D), lambda i: (0, 0)), row(D)]
        + _dep_specs(deps),
        out_specs=[row(D), pl.BlockSpec((1, D), lambda i: (0, 0))],
        out_shape=[SDS((L, D), F32), SDS((1, D), F32)],
        compiler_params=_cparams(("arbitrary",)),
    )(*slabs, w, x, g, dx_next, *deps)


def _inproj_bwd_w_call(h, slabs):
    L = h.shape[0]
    cols = _slab_cols(slabs)
    n = len(slabs)

    def body(*refs):
        h_ref, dp_refs, dw_ref = refs[0], refs[1:1 + n], refs[1 + n]

        @pl.when(pl.program_id(0) == 0)
        def _():
            dw_ref[...] = jnp.zeros_like(dw_ref)

        hv = h_ref[...]
        for dp_ref, (c0, cw) in zip(dp_refs, cols):
            dw_ref[:, c0:c0 + cw] += _dot(hv, dp_ref[...], TN)

    row = lambda w_: pl.BlockSpec((TL, w_), lambda i: (i, 0))
    return pl.pallas_call(
        body, name="inproj_bwd_w", grid=(L // TL,),
        in_specs=[row(D)] + [row(cw) for _, cw in cols],
        out_specs=pl.BlockSpec((D, PW), lambda i: (0, 0)),
        out_shape=SDS((D, PW), F32),
        compiler_params=_cparams(("arbitrary",)),
    )(h, *slabs)


def _exchange_call(name, flows):
    n = len(flows)

    def body(*refs):
        srcs, dsts = refs[:n], refs[n:2 * n]
        send_sems, recv_sems, local_sems = refs[2 * n:]
        x, y, c = lax.axis_index("x"), lax.axis_index("y"), lax.axis_index("c")
        me = 4 * x + 2 * y + c
        copies = []
        for mask in range(1, N_DEV):
            px = 1 - x if mask & 4 else x
            py = 1 - y if mask & 2 else y
            pc = 1 - c if mask & 1 else c
            for f, (_, src_at, _, dst_at) in enumerate(flows):
                cp = pltpu.make_async_remote_copy(
                    src_ref=src_at(srcs[f], 4 * px + 2 * py + pc), dst_ref=dst_at(dsts[f], me),
                    send_sem=send_sems.at[mask - 1, f], recv_sem=recv_sems.at[mask - 1, f],
                    device_id=(px, py, pc), device_id_type=pl.DeviceIdType.MESH)
                cp.start()
                copies.append(cp)
        mine = [pltpu.make_async_copy(src_at(srcs[f], me), dst_at(dsts[f], me), local_sems.at[f])
                for f, (_, src_at, _, dst_at) in enumerate(flows)]
        for cp in mine:
            cp.start()
        for cp in copies + mine:
            cp.wait()

    return pl.pallas_call(
        body, name=name,
        in_specs=[pl.BlockSpec(memory_space=pl.ANY)] * n,
        out_specs=[pl.BlockSpec(memory_space=pl.ANY)] * n,
        out_shape=[SDS(tuple(shape), src.dtype) for src, _, shape, _ in flows],
        scratch_shapes=[pltpu.SemaphoreType.DMA((N_DEV - 1, n)), pltpu.SemaphoreType.DMA((N_DEV - 1, n)),
                        pltpu.SemaphoreType.DMA((n,))],
    )(*[f[0] for f in flows])


def _whole(ref, _):
    return ref


def _slot(ref, k):
    return ref.at[k]


_HBM_SPEC = pl.BlockSpec(memory_space=pltpu.HBM)
_SEM_SPEC = pl.BlockSpec(memory_space=pltpu.SEMAPHORE)
_DATAFLOW = pltpu.SideEffectType.DATAFLOW_SIDE_EFFECTING


def _split_copies(views, src_refs, land_refs, send_sems, recv_sems):
    x, y, c = lax.axis_index("x"), lax.axis_index("y"), lax.axis_index("c")
    me = 4 * x + 2 * y + c
    copies = []
    for mask in range(1, N_DEV):
        px = 1 - x if mask & 4 else x
        py = 1 - y if mask & 2 else y
        pc = 1 - c if mask & 1 else c
        for f, (src_at, dst_at) in enumerate(views):
            pair = (mask - 1) * len(views) + f
            copies.append(pltpu.make_async_remote_copy(
                src_ref=src_at(src_refs[f], 4 * px + 2 * py + pc), dst_ref=dst_at(land_refs[f], me),
                send_sem=send_sems.at[pair], recv_sem=recv_sems.at[pair],
                device_id=(px, py, pc), device_id_type=pl.DeviceIdType.MESH))
    return copies


def _split_start_call(name, srcs, lands, views):
    n = len(srcs)

    def body(*refs):
        src_refs, land_refs = refs[:n], refs[n:2 * n]
        send_sems, recv_sems, token = refs[2 * n], refs[2 * n + 1], refs[-1]
        for cp in _split_copies(views, src_refs, land_refs, send_sems, recv_sems):
            cp.start()
        token[...] = jnp.zeros_like(token)

    arrays = list(srcs) + list(lands)
    outs = pl.pallas_call(
        body, name=name,
        out_shape=(pltpu.SemaphoreType.DMA(((N_DEV - 1) * n,)), pltpu.SemaphoreType.DMA(((N_DEV - 1) * n,)),
                   *[pltpu.HBM(a.shape, a.dtype) for a in arrays], SDS((8, 128), F32)),
        in_specs=[_HBM_SPEC] * (2 * n),
        out_specs=(_SEM_SPEC, _SEM_SPEC, *[_HBM_SPEC] * (2 * n), pl.BlockSpec(memory_space=pltpu.VMEM)),
        input_output_aliases={i: 2 + i for i in range(2 * n)},
        compiler_params=pltpu.CompilerParams(has_side_effects=_DATAFLOW),
    )(*[pltpu.with_memory_space_constraint(a, pltpu.HBM) for a in arrays])
    return outs[0], outs[1], list(outs[2:2 + 2 * n]), outs[-1]


def _split_wait_call(name, send_sems, recv_sems, thru, views, after):
    n = len(views)

    def body(*refs):
        src_refs, land_refs = refs[:n], refs[n:2 * n]
        send, recv = refs[2 * n], refs[2 * n + 1]
        for cp in _split_copies(views, src_refs, land_refs, send, recv):
            cp.wait_send()
            cp.wait_recv()

    outs = pl.pallas_call(
        body, name=name,
        out_shape=tuple(pltpu.HBM(a.shape, a.dtype) for a in thru),
        in_specs=[_HBM_SPEC] * (2 * n) + [_SEM_SPEC, _SEM_SPEC, pl.BlockSpec(memory_space=pl.ANY)],
        out_specs=tuple([_HBM_SPEC] * (2 * n)),
        input_output_aliases={i: i for i in range(2 * n)},
        compiler_params=pltpu.CompilerParams(has_side_effects=_DATAFLOW),
    )(*thru, send_sems, recv_sems, after)
    return list(outs[n:])


def _own_slot(block, me):
    zone = lax.empty((N_DEV,) + block.shape, block.dtype)
    return lax.dynamic_update_slice(zone, block[None], (me,) + (0,) * block.ndim)


def _reduce_adamw_call(parts, w, m, v, block, name):
    nsrc = parts.shape[0]
    grid = tuple(s // b for s, b in zip(w.shape, block))
    c1 = 1.0 - ADAM_B1 ** ADAM_STEP
    c2 = 1.0 - ADAM_B2 ** ADAM_STEP

    def body(p_ref, w_ref, m_ref, v_ref, g_ref, d_ref, nm_ref, nv_ref):
        g = p_ref[0].astype(F32)
        for k in range(1, nsrc):
            g = g + p_ref[k].astype(F32)
        nm = ADAM_B1 * m_ref[...] + (1.0 - ADAM_B1) * g
        nv = ADAM_B2 * v_ref[...] + (1.0 - ADAM_B2) * (g * g)
        g_ref[...] = g
        nm_ref[...] = nm
        nv_ref[...] = nv
        d_ref[...] = -ADAM_LR * ((nm / c1) / (jnp.sqrt(nv / c2) + ADAM_EPS) + ADAM_WD * w_ref[...])

    own = pl.BlockSpec(tuple(block), lambda *i: i)
    return pl.pallas_call(
        body, name=name, grid=grid,
        in_specs=[pl.BlockSpec((nsrc,) + tuple(block), lambda *i: (0,) + i), own, own, own],
        out_specs=[own] * 4,
        out_shape=[SDS(w.shape, F32)] * 4,
        compiler_params=_cparams(("parallel",) * len(grid)),
    )(parts, w, m, v)


RELAYOUT_ROWS = 256
SHARD_COLS = IN_COLS // N_DEV


def _win_gather_layout_call(shards):
    def body(w_ref, o_ref):
        nat = jnp.concatenate([w_ref[k].astype(F32) for k in range(N_DEV)], axis=1)
        out = jnp.concatenate([nat[:, :3072], nat[:, 3080:], nat[:, 3072:3080],
                               jnp.zeros((RELAYOUT_ROWS, PW - IN_COLS), F32)], axis=1)
        o_ref[...] = out.astype(BF16)

    return pl.pallas_call(
        body, name="w_in_layout", grid=(D // RELAYOUT_ROWS,),
        in_specs=[pl.BlockSpec((N_DEV, RELAYOUT_ROWS, SHARD_COLS), lambda i: (0, i, 0))],
        out_specs=pl.BlockSpec((RELAYOUT_ROWS, PW), lambda i: (i, 0)),
        out_shape=SDS((D, PW), BF16),
        compiler_params=_cparams(("parallel",)),
    )(shards)


def _win_scatter_layout_call(grad):
    def body(g_ref, o_ref):
        g = g_ref[...]
        nat = jnp.concatenate([g[:, :3072], g[:, AB_COL:AB_COL + 8], g[:, 3072:AB_COL]], axis=1)
        for k in range(N_DEV):
            o_ref[k] = nat[:, SHARD_COLS * k:SHARD_COLS * (k + 1)].astype(BF16)

    return pl.pallas_call(
        body, name="w_in_grad_layout", grid=(D // RELAYOUT_ROWS,),
        in_specs=[pl.BlockSpec((RELAYOUT_ROWS, PW), lambda i: (i, 0))],
        out_specs=pl.BlockSpec((N_DEV, RELAYOUT_ROWS, SHARD_COLS), lambda i: (0, i, 0)),
        out_shape=SDS((N_DEV, D, SHARD_COLS), BF16),
        compiler_params=_cparams(("parallel",)),
    )(grad)


def _reduce_adamw_layers_call(parts, w, m, v, rows, name):
    _, R, C = w.shape
    c1 = 1.0 - ADAM_B1 ** ADAM_STEP
    c2 = 1.0 - ADAM_B2 ** ADAM_STEP

    def body(*refs):
        p_refs = refs[:DEPTH]
        w_ref, m_ref, v_ref, g_ref, d_ref, nm_ref, nv_ref = refs[DEPTH:]
        for l in range(DEPTH):
            @pl.when(pl.program_id(0) == l)
            def _(l=l):
                g = p_refs[l][0].astype(F32)
                for k in range(1, N_DEV):
                    g = g + p_refs[l][k].astype(F32)
                nm = ADAM_B1 * m_ref[0] + (1.0 - ADAM_B1) * g
                nv = ADAM_B2 * v_ref[0] + (1.0 - ADAM_B2) * (g * g)
                g_ref[0] = g
                nm_ref[0] = nm
                nv_ref[0] = nv
                d_ref[0] = -ADAM_LR * ((nm / c1) / (jnp.sqrt(nv / c2) + ADAM_EPS) + ADAM_WD * w_ref[0])

    def part_spec(l):
        return pl.BlockSpec((N_DEV, rows, C), lambda j, i: (0, jnp.where(j == l, i, 0), 0))

    own = pl.BlockSpec((1, rows, C), lambda j, i: (j, i, 0))
    return pl.pallas_call(
        body, name=name, grid=(DEPTH, R // rows),
        in_specs=[part_spec(l) for l in range(DEPTH)] + [own, own, own],
        out_specs=[own] * 4,
        out_shape=[SDS(w.shape, F32)] * 4,
        compiler_params=_cparams(("arbitrary", "arbitrary")),
    )(*parts, w, m, v)


_BIG = ("w_in", "w_out", "a_pw_w", "s5_glu_w")
_CONV = ("a_conv_w", "c_conv_w", "d_conv_w")
_CONV_TAPS = {"a_conv_w": KA, "c_conv_w": KC, "d_conv_w": KD}
_CONV_ROWS = {"a_conv_w": HALO, "c_conv_w": HALO_S, "d_conv_w": HALO_S}
_CONV_WIDTH = {"a_conv_w": BR, "c_conv_w": BR, "d_conv_w": 3 * BR}
_REPLICATED = ("norm_g", "a_conv_b", "a_ln_g", "a_ln_b", "a_pw_b", "s5_lambda_re", "s5_lambda_im", "s5_b_re", "s5_b_im",
               "s5_c_re", "s5_c_im", "s5_d", "s5_log_dt", "s5_glu_b", "d_a_log", "d_dt_bias", "d_norm_g", "final_g")
_REP_TAIL = ("norm_g", "final_g")
_REP_LAYER = tuple(n for n in _REPLICATED if n not in _REP_TAIL)
_WEIGHTS = ("norm_g", "w_in", "a_conv_w", "a_conv_b", "a_ln_g", "a_ln_b", "a_pw_w", "a_pw_b", "s5_lambda_re",
            "s5_lambda_im", "s5_b_re", "s5_b_im", "s5_c_re", "s5_c_im", "s5_d", "s5_log_dt", "s5_glu_w", "s5_glu_b",
            "c_conv_w", "d_conv_w", "d_a_log", "d_dt_bias", "d_norm_g", "w_out", "final_g")


def _size(shape):
    n = 1
    for s in shape:
        n *= s
    return n


PACK_ALIGN = 1024


def _piece_rows(n):
    return -(-n // PACK_ALIGN) * (PACK_ALIGN // 128)


def _pack_rows(pieces, row_mult):
    rows = []
    for p in pieces:
        flat = p.reshape(-1)
        rows.append(jnp.pad(flat, (0, _piece_rows(flat.shape[0]) * 128 - flat.shape[0])).reshape(-1, 128))
    out = jnp.concatenate(rows, axis=0)
    return jnp.pad(out, ((0, (-out.shape[0]) % row_mult), (0, 0)))


def _pack_layers(pieces):
    rows = []
    for p in pieces:
        flat = p.reshape(DEPTH, -1)
        nr = _piece_rows(flat.shape[1])
        rows.append(jnp.pad(flat, ((0, 0), (0, nr * 128 - flat.shape[1]))).reshape(DEPTH, nr, 128))
    return jnp.concatenate(rows, axis=1)


def _unpack_layers(packed, shapes):
    out, row = [], 0
    for s in shapes:
        n = _size(s[1:])
        nr = _piece_rows(n)
        out.append(packed[:, row:row + nr].reshape(DEPTH, -1)[:, :n].reshape(s))
        row += nr
    return out


def _unpack(packed, shapes):
    out, row = [], 0
    for s in shapes:
        n = _size(s)
        nr = _piece_rows(n)
        out.append(packed[row:row + nr].reshape(-1)[:n].reshape(s))
        row += nr
    return out


_GATHER_VIEWS = [(_whole, _slot)] * 5


def _gather_start(shards, layer, me):
    srcs = [shards[n].astype(BF16) for n in _BIG]
    srcs.append(_pack_rows([shards[n] for n in _CONV], 8))
    lands = [_own_slot(s, me) for s in srcs]
    return _split_start_call("gather_start_%d" % layer, srcs, lands, _GATHER_VIEWS)


def _gather_finish(weights, layer, started, after):
    send, recv, thru, _ = started
    w_in, w_out, a_pw, glu, conv_all = _split_wait_call("gather_wait_%d" % layer, send, recv, thru, _GATHER_VIEWS, after)
    full = {"w_in": _win_gather_layout_call(w_in), "w_out": w_out.reshape(D, D), "a_pw_w": a_pw.reshape(BR, BR),
            "s5_glu_w": glu.reshape(BR, BR)}
    shapes = [weights[n].shape[1:] for n in _CONV]
    per_dev = [_unpack(conv_all[k], shapes) for k in range(N_DEV)]
    for i, n in enumerate(_CONV):
        whole = jnp.concatenate([per_dev[k][i] for k in range(N_DEV)], axis=-1)
        full[n] = jnp.pad(whole, ((0, _CONV_ROWS[n] - _CONV_TAPS[n]), (0, 0)))
    return full


def _rows_view(rows):
    return lambda ref, k: ref.at[pl.ds(k * rows, rows), :]


_SCATTER_VIEWS = [(_slot, _slot), (_rows_view(D // N_DEV), _slot), (_rows_view(BR // N_DEV), _slot),
                  (_rows_view(BR // N_DEV), _slot), (_whole, _slot)]


def _scatter_start(grads, small, layer, me):
    srcs = [_win_scatter_layout_call(grads["w_in"])] + [grads[n].astype(BF16) for n in _BIG[1:]]
    own = [lax.dynamic_index_in_dim(srcs[0], me, 0, keepdims=False)]
    for s, rows in zip(srcs[1:], (D // N_DEV, BR // N_DEV, BR // N_DEV)):
        own.append(lax.dynamic_slice_in_dim(s, me * rows, rows, axis=0))
    lands = [_own_slot(o, me) for o in own + [small]]
    return _split_start_call("scatter_start_%d" % layer, srcs + [small], lands, _SCATTER_VIEWS)


_S5_KERNEL_SHAPES = {"s5_lambda_re": (1, NSTATE), "s5_lambda_im": (1, NSTATE), "s5_log_dt": (1, 16),
                     "s5_b_re": (NSTATE, 16), "s5_b_im": (NSTATE, 16), "s5_c_re": (BR, 64), "s5_c_im": (BR, 64)}
_S5_KEYS = {"s5_lambda_re": "lam_re", "s5_lambda_im": "lam_im", "s5_log_dt": "log_dt", "s5_b_re": "b_re",
            "s5_b_im": "b_im", "s5_c_re": "c_re", "s5_c_im": "c_im"}


def _s5_inputs_all(weights):
    return {n: weights[n].reshape((DEPTH,) + s) for n, s in _S5_KERNEL_SHAPES.items()}


def _s5_inputs(p):
    return {_S5_KEYS[n]: p["s5_in"][n] for n in _S5_KERNEL_SHAPES}


def _row(a, width=None):
    a = a.reshape(1, -1)
    return a if width is None else jnp.pad(a, ((0, 0), (0, width - a.shape[1])))


def _layer_params(p):
    q = dict(p)
    for n in ("norm_g", "a_conv_b", "a_ln_g", "a_ln_b", "a_pw_b", "s5_d", "s5_glu_b", "d_norm_g"):
        q[n] = _row(p[n])
    q["d_a_log"] = _row(p["d_a_log"], 128)
    q["d_dt_bias"] = _row(p["d_dt_bias"], 128)
    return q


def _layer_fwd(x, p, deps=()):
    q = _layer_params(p)
    proj, h = _inproj_call(x, q["norm_g"], q["w_in"], deps)
    ya, yc = _ac_fwd_call(proj, q)
    prep = p["s5_prep"]
    yb, cin_r, cin_i = _s5_fwd_call(proj, prep, q)
    dq, dk, dv, dgb = _dn_pre_fwd_call(proj, q)
    yd, ssave = _dn_core_fwd_call(proj, dq, dk, dv, dgb, q["d_norm_g"])
    x_next = _outproj_call(x, (ya, yb, yc, yd), q["w_out"])
    saved = dict(x=x, proj=proj, h=h, ya=ya, yb=yb, yc=yc, yd=yd, cin_r=cin_r, cin_i=cin_i,
                 q=dq, k=dk, v=dv, gb=dgb, ssave=ssave, prep=prep)
    return x_next, saved


def _layer_bwd(dx, p, sv, deps=(), on_weight_grads=None):
    q = _layer_params(p)
    proj = sv["proj"]
    dya, dyb, dyc, dyd, g_wout = _outproj_bwd_call(dx, (sv["ya"], sv["yb"], sv["yc"], sv["yd"]), q["w_out"], deps)
    dpa, dpc, g_acw, g_acb, g_alg, g_alb, g_apw, g_apb, g_ccw = _ac_bwd_call(proj, q, dya, dyc)
    dpb, *s5g = _s5_bwd_call(proj, sv["prep"], q, sv["cin_r"], sv["cin_i"], dyb)
    g_sd, g_gw, g_gb = s5g[6:]
    g_lre, g_lim, g_ldt, g_bre, g_bim, g_cre, g_cim = _s5_prep_bwd_call(_s5_inputs(p), s5g[:6])
    dq, dk, dv, dgb, dz, g_ng = _dn_core_bwd_call(proj, sv["q"], sv["k"], sv["v"], sv["gb"], q["d_norm_g"], sv["ssave"], dyd)
    dqkv, dab, g_dcw, g_alog, g_dtb = _dn_pre_bwd_call(proj, q, dq, dk, dv, dgb)
    slabs = (dpa, dpb, dpc, dqkv, dz, dab)
    g_win = _inproj_bwd_w_call(sv["h"], slabs)
    grads = {"w_in": g_win, "a_conv_w": g_acw, "a_conv_b": g_acb, "a_ln_g": g_alg, "a_ln_b": g_alb,
             "a_pw_w": g_apw, "a_pw_b": g_apb, "s5_lambda_re": g_lre, "s5_lambda_im": g_lim, "s5_b_re": g_bre,
             "s5_b_im": g_bim, "s5_c_re": g_cre, "s5_c_im": g_cim, "s5_d": g_sd, "s5_log_dt": g_ldt, "s5_glu_w": g_gw,
             "s5_glu_b": g_gb, "c_conv_w": g_ccw, "d_conv_w": g_dcw, "d_a_log": g_alog[:, :NH], "d_dt_bias": g_dtb[:, :NH],
             "d_norm_g": g_ng, "w_out": g_wout}
    tokens = ()
    if on_weight_grads is not None:
        small = _pack_rows([grads[n] for n in _REP_LAYER + _CONV], 8)
        tokens = on_weight_grads({n: grads[n] for n in _BIG}, small)
    dx_prev, grads["norm_g"] = _inproj_bwd_x_call(slabs, q["w_in"], sv["x"], q["norm_g"], dx, tokens)
    return dx_prev, grads


def _step(x, target, weights, moments_m, moments_v):
    me = 4 * lax.axis_index("x") + 2 * lax.axis_index("y") + lax.axis_index("c")
    layer_names = [n for n in _WEIGHTS if n != "final_g"]
    s5_all = _s5_inputs_all(weights)

    sharded = _BIG + _CONV
    gather = _gather_start({n: weights[n][0] for n in sharded}, 0, me)
    preps = [_s5_prep_call({_S5_KEYS[n]: a[l] for n, a in s5_all.items()}, [gather[3]]) for l in range(DEPTH)]
    x_out, after, layers, saved = x, preps[-1][0], [], []
    for l in range(DEPTH):
        full = _gather_finish(weights, l, gather, after)
        deps = ()
        if l + 1 < DEPTH:
            nxt, full["w_out"] = lax.optimization_barrier(({n: weights[n][l + 1] for n in sharded}, full["w_out"]))
            gather = _gather_start(nxt, l + 1, me)
            deps = [gather[3]]
        p = {n: (full[n] if n in full else weights[n][l]) for n in layer_names}
        p["s5_in"] = {n: a[l] for n, a in s5_all.items()}
        p["s5_prep"] = preps[l]
        layers.append(p)
        x_out, sv = _layer_fwd(x_out, p, deps)
        after = x_out
        saved.append(sv)
    dx0, g_final, loss_part = _loss_call(x_out, _row(weights["final_g"]), target)

    per_layer, scatters = [None] * DEPTH, [None] * DEPTH
    for l in range(DEPTH - 1, -1, -1):
        def start(big, small, l=l):
            scatters[l] = _scatter_start(big, small, l, me)
            return [scatters[l][3]]
        dx0, per_layer[l] = _layer_bwd(dx0, layers[l], saved[l], [scatters[l + 1][3]] if l + 1 < DEPTH else (), start)
    loss = lax.psum(loss_part[0, 0], ("x", "y", "c"))
    results = {}

    parts = [_split_wait_call("scatter_wait_%d" % l, scatters[l][0], scatters[l][1], scatters[l][2], _SCATTER_VIEWS, dx0)
             for l in range(DEPTH - 1, -1, -1)][::-1]
    rows = {"w_in": RELAYOUT_ROWS, "w_out": D // N_DEV, "a_pw_w": BR // N_DEV, "s5_glu_w": BR // N_DEV}
    for i, n in enumerate(_BIG):
        results[n] = _reduce_adamw_layers_call([parts[l][i] for l in range(DEPTH)], weights[n], moments_m[n],
                                               moments_v[n], rows[n], "adamw_" + n)

    conv_shapes = [(DEPTH, _CONV_ROWS[n], _CONV_WIDTH[n]) for n in _CONV]
    pack = lambda d: _pack_layers([d[n] for n in _REP_LAYER] + [jnp.zeros(s, F32) for s in conv_shapes])
    wpack = pack(weights)
    res = _reduce_adamw_layers_call([parts[l][len(_BIG)] for l in range(DEPTH)], wpack, pack(moments_m), pack(moments_v),
                                    wpack.shape[1], "adamw_replicated")
    shapes = [weights[n].shape for n in _REP_LAYER] + conv_shapes
    res = [_unpack_layers(r, shapes) for r in res]
    for i, n in enumerate(_REP_LAYER):
        results[n] = tuple(r[i] for r in res)

    tail_g = [jnp.stack([g["norm_g"] for g in per_layer]).reshape(DEPTH, D), g_final.reshape(D)]
    packt = lambda arrs: _pack_rows(arrs, 8)
    gathered, = _exchange_call("gather_tail_grads", [(packt(tail_g), _whole, (N_DEV,) + packt(tail_g).shape, _slot)])
    tail = _reduce_adamw_call(gathered, packt([weights[n] for n in _REP_TAIL]), packt([moments_m[n] for n in _REP_TAIL]),
                              packt([moments_v[n] for n in _REP_TAIL]), packt(tail_g).shape, "adamw_tail")
    tail = [_unpack(r, [weights[n].shape for n in _REP_TAIL]) for r in tail]
    for i, n in enumerate(_REP_TAIL):
        results[n] = tuple(r[i] for r in tail)

    own_g = []
    for i, n in enumerate(_CONV):
        width = _CONV_WIDTH[n] // N_DEV
        summed = res[0][len(_REP_LAYER) + i][:, :_CONV_TAPS[n], :]
        own_g.append(lax.dynamic_slice_in_dim(summed, me * width, width, axis=2))
    packc = lambda arrs: _pack_rows(arrs, 8)
    res = _reduce_adamw_call(packc(own_g)[None], packc([weights[n] for n in _CONV]), packc([moments_m[n] for n in _CONV]),
                             packc([moments_v[n] for n in _CONV]), packc(own_g).shape, "adamw_conv")
    res = [_unpack(r, [weights[n].shape for n in _CONV]) for r in res]
    for i, n in enumerate(_CONV):
        results[n] = tuple(r[i] for r in res)

    outs = [loss, dx0]
    for kind in range(4):
        outs += [results[n][kind] for n in _WEIGHTS]
    return tuple(outs)


def kernel(x, norm_g, w_in, a_conv_w, a_conv_b, a_ln_g, a_ln_b, a_pw_w, a_pw_b, s5_lambda_re, s5_lambda_im, s5_b_re, s5_b_im, s5_c_re, s5_c_im, s5_d, s5_log_dt, s5_glu_w, s5_glu_b, c_conv_w, d_conv_w, d_a_log, d_dt_bias, d_norm_g, w_out, final_g, loss_target, m_norm_g, m_w_in, m_a_conv_w, m_a_conv_b, m_a_ln_g, m_a_ln_b, m_a_pw_w, m_a_pw_b, m_s5_lambda_re, m_s5_lambda_im, m_s5_b_re, m_s5_b_im, m_s5_c_re, m_s5_c_im, m_s5_d, m_s5_log_dt, m_s5_glu_w, m_s5_glu_b, m_c_conv_w, m_d_conv_w, m_d_a_log, m_d_dt_bias, m_d_norm_g, m_w_out, m_final_g, v_norm_g, v_w_in, v_a_conv_w, v_a_conv_b, v_a_ln_g, v_a_ln_b, v_a_pw_w, v_a_pw_b, v_s5_lambda_re, v_s5_lambda_im, v_s5_b_re, v_s5_b_im, v_s5_c_re, v_s5_c_im, v_s5_d, v_s5_log_dt, v_s5_glu_w, v_s5_glu_b, v_c_conv_w, v_d_conv_w, v_d_a_log, v_d_dt_bias, v_d_norm_g, v_w_out, v_final_g):
    weights = dict(norm_g=norm_g, w_in=w_in, a_conv_w=a_conv_w, a_conv_b=a_conv_b, a_ln_g=a_ln_g, a_ln_b=a_ln_b, a_pw_w=a_pw_w, a_pw_b=a_pw_b, s5_lambda_re=s5_lambda_re, s5_lambda_im=s5_lambda_im, s5_b_re=s5_b_re, s5_b_im=s5_b_im, s5_c_re=s5_c_re, s5_c_im=s5_c_im, s5_d=s5_d, s5_log_dt=s5_log_dt, s5_glu_w=s5_glu_w, s5_glu_b=s5_glu_b, c_conv_w=c_conv_w, d_conv_w=d_conv_w, d_a_log=d_a_log, d_dt_bias=d_dt_bias, d_norm_g=d_norm_g, w_out=w_out, final_g=final_g)
    mom_m = dict(norm_g=m_norm_g, w_in=m_w_in, a_conv_w=m_a_conv_w, a_conv_b=m_a_conv_b, a_ln_g=m_a_ln_g, a_ln_b=m_a_ln_b, a_pw_w=m_a_pw_w, a_pw_b=m_a_pw_b, s5_lambda_re=m_s5_lambda_re, s5_lambda_im=m_s5_lambda_im, s5_b_re=m_s5_b_re, s5_b_im=m_s5_b_im, s5_c_re=m_s5_c_re, s5_c_im=m_s5_c_im, s5_d=m_s5_d, s5_log_dt=m_s5_log_dt, s5_glu_w=m_s5_glu_w, s5_glu_b=m_s5_glu_b, c_conv_w=m_c_conv_w, d_conv_w=m_d_conv_w, d_a_log=m_d_a_log, d_dt_bias=m_d_dt_bias, d_norm_g=m_d_norm_g, w_out=m_w_out, final_g=m_final_g)
    mom_v = dict(norm_g=v_norm_g, w_in=v_w_in, a_conv_w=v_a_conv_w, a_conv_b=v_a_conv_b, a_ln_g=v_a_ln_g, a_ln_b=v_a_ln_b, a_pw_w=v_a_pw_w, a_pw_b=v_a_pw_b, s5_lambda_re=v_s5_lambda_re, s5_lambda_im=v_s5_lambda_im, s5_b_re=v_s5_b_re, s5_b_im=v_s5_b_im, s5_c_re=v_s5_c_re, s5_c_im=v_s5_c_im, s5_d=v_s5_d, s5_log_dt=v_s5_log_dt, s5_glu_w=v_s5_glu_w, s5_glu_b=v_s5_glu_b, c_conv_w=v_c_conv_w, d_conv_w=v_d_conv_w, d_a_log=v_d_a_log, d_dt_bias=v_d_dt_bias, d_norm_g=v_d_norm_g, w_out=v_w_out, final_g=v_final_g)
    outs = _step(x[0], loss_target[0], weights, mom_m, mom_v)
    return (outs[0], outs[1][None]) + outs[2:]
```

```python
import functools

import jax
import jax.numpy as jnp
from jax import lax
from jax.experimental import pallas as pl
from jax.experimental.pallas import tpu as pltpu

F32 = jnp.float32
BF16 = jnp.bfloat16
HI = lax.Precision.HIGHEST
SDS = jax.ShapeDtypeStruct

N_DEV = 8
D = 1024
BR = 256
DEPTH = 4
IN_COLS = 3336
PW = 3456
AB_COL = 3328
EPS = 1e-6
TL = 512
SEG = TL // 8
HALO = 32
HALO_S = 8
KA, KC, KD = 31, 3, 4
CH = 64
DN_GROUP = 4
DN_FWD_GROUPS = 2
NH, HD = 4, 64
NSTATE = 1024
S5_PARTS = 2
VMEM_LIMIT = 56 * 1024 * 1024

ADAM_LR, ADAM_B1, ADAM_B2, ADAM_EPS, ADAM_WD, ADAM_STEP = 0.001, 0.9, 0.999, 1e-08, 0.01, 10

NN = ((1,), (0,))
NT = ((1,), (1,))
TN = ((0,), (0,))


def _dot(a, b, dims, prec=None):
    return lax.dot_general(a, b, (dims, ((), ())), precision=prec, preferred_element_type=F32)


def _make_mm(cast, prec, fwd_dims):
    def prep(t):
        return t.astype(cast) if cast is not None else t

    @jax.custom_vjp
    def mm(a, w):
        return _dot(prep(a), prep(w), fwd_dims, prec)

    def fwd(a, w):
        return mm(a, w), (a, w)

    def bwd(res, dy):
        a, w = res
        a, w, dy = prep(a), prep(w), prep(dy)
        if fwd_dims == NN:
            return _dot(dy, w, NT, prec), _dot(a, dy, TN, prec)
        if fwd_dims == NT:
            return _dot(dy, w, NN, prec), _dot(dy, a, TN, prec)
        return _dot(w, dy, NT, prec), _dot(a, dy, NN, prec)

    mm.defvjp(fwd, bwd)
    return mm


mm = _make_mm(BF16, None, NN)
mm_nt = _make_mm(BF16, None, NT)
mm_tn = _make_mm(BF16, None, TN)
mmh = _make_mm(None, HI, NN)
mmh_nt = _make_mm(None, HI, NT)


def _sigmoid(x):
    return jax.nn.sigmoid(x)


def _silu(x):
    return x * jax.nn.sigmoid(x)


def _gelu(x):
    return 0.5 * x * (1.0 + jnp.tanh(0.7978845608028654 * (x + 0.044715 * (x * x * x))))


def _softplus(x):
    return jnp.maximum(x, 0.0) + jnp.log1p(jnp.exp(-jnp.abs(x)))


def _rms(x, g):
    return x * lax.rsqrt(jnp.mean(x * x, axis=-1, keepdims=True) + EPS) * g


def _cparams(sem):
    return pltpu.CompilerParams(dimension_semantics=sem, vmem_limit_bytes=VMEM_LIMIT)


def _tap_offsets(halo, taps):
    return [halo - (taps - 1) + k for k in range(taps)]


def _conv_fwd_impl(acat, w, tile, halo, taps):
    n = tile + halo
    out = None
    for k, off in enumerate(_tap_offsets(halo, taps)):
        src = jnp.roll(acat, n - off, axis=0)[:tile, :] if off != halo else acat[halo:, :]
        term = src * w[k:k + 1, :]
        out = term if out is None else out + term
    return out


def _make_conv(tile, halo, taps):
    @jax.custom_vjp
    def conv(acat, w):
        return _conv_fwd_impl(acat, w, tile, halo, taps)

    def fwd(acat, w):
        return conv(acat, w), (acat, w)

    def bwd(res, dy):
        acat, w = res
        n = tile + halo
        dyp = jnp.concatenate([dy, jnp.zeros((halo, dy.shape[1]), F32)], axis=0)
        rows = lax.broadcasted_iota(jnp.int32, w.shape, 0)
        dacat = None
        dw = jnp.zeros(w.shape, F32)
        for k, off in enumerate(_tap_offsets(halo, taps)):
            term = jnp.roll(dyp, off, axis=0) * w[k:k + 1, :]
            dacat = term if dacat is None else dacat + term
            src = jnp.roll(acat, n - off, axis=0)[:tile, :] if off != halo else acat[halo:, :]
            dw = dw + jnp.where(rows == k, jnp.sum(dy * src, axis=0, keepdims=True), 0.0)
        return dacat, dw

    conv.defvjp(fwd, bwd)
    return conv


def _halo_spec(tile, halo, width, col):
    per = tile // halo
    return pl.BlockSpec((halo, width), lambda i: (jnp.maximum(i * per - 1, 0), col))


def _halo_spec_rev(nt, tile, halo, width, col):
    per = tile // halo
    return pl.BlockSpec((halo, width), lambda i: (jnp.maximum((nt - 1 - i) * per - 1, 0), col))


def _dep_specs(deps):
    return [pl.BlockSpec((8, 128), lambda *_: (0, 0)) for _ in deps]


def _inproj_call(x, g, w, deps=()):
    L = x.shape[0]

    def body(x_ref, g_ref, w_ref, *rest):
        p_ref, h_ref = rest[len(deps):]
        h = _rms(x_ref[...], g_ref[...]).astype(BF16)
        h_ref[...] = h
        p_ref[...] = _dot(h, w_ref[...], NN)

    return pl.pallas_call(
        body, name="inproj", grid=(L // TL,),
        in_specs=[pl.BlockSpec((TL, D), lambda i: (i, 0)), pl.BlockSpec((1, D), lambda i: (0, 0)),
                  pl.BlockSpec((D, PW), lambda i: (0, 0))] + _dep_specs(deps),
        out_specs=[pl.BlockSpec((TL, PW), lambda i: (i, 0)), pl.BlockSpec((TL, D), lambda i: (i, 0))],
        out_shape=[SDS((L, PW), F32), SDS((L, D), BF16)],
        compiler_params=_cparams(("parallel",)),
    )(x, g, w, *deps)


def _outproj_call(x, ys, w):
    L = x.shape[0]

    def body(x_ref, a_ref, b_ref, c_ref, d_ref, w_ref, o_ref):
        mixed = jnp.concatenate([y_ref[...].astype(BF16) for y_ref in (a_ref, b_ref, c_ref, d_ref)], axis=1)
        o_ref[...] = x_ref[...] + _dot(mixed, w_ref[...], NN)

    yspec = pl.BlockSpec((TL, BR), lambda i: (i, 0))
    return pl.pallas_call(
        body, name="outproj", grid=(L // TL,),
        in_specs=[pl.BlockSpec((TL, D), lambda i: (i, 0)), yspec, yspec, yspec, yspec,
                  pl.BlockSpec((D, D), lambda i: (0, 0))],
        out_specs=pl.BlockSpec((TL, D), lambda i: (i, 0)),
        out_shape=SDS((L, D), F32),
        compiler_params=_cparams(("parallel",)),
    )(x, *ys, w)


def _loss_call(x, g, target):
    L = x.shape[0]

    def body(x_ref, g_ref, t_ref, dx_ref, dg_ref, loss_ref):
        @pl.when(pl.program_id(0) == 0)
        def _():
            dg_ref[...] = jnp.zeros_like(dg_ref)
            loss_ref[...] = jnp.zeros_like(loss_ref)

        y, vjp = jax.vjp(_rms, x_ref[...], g_ref[...])
        err = y - t_ref[...]
        dx, dg = vjp(err * (1.0 / D))
        dx_ref[...] = dx
        dg_ref[...] += dg
        tot = jnp.sum(jnp.sum(err * err, axis=1, keepdims=True), axis=0, keepdims=True)
        loss_ref[...] += jnp.broadcast_to(tot * (0.5 / D), loss_ref.shape)

    return pl.pallas_call(
        body, name="loss_head", grid=(L // TL,),
        in_specs=[pl.BlockSpec((TL, D), lambda i: (i, 0)), pl.BlockSpec((1, D), lambda i: (0, 0)),
                  pl.BlockSpec((TL, D), lambda i: (i, 0))],
        out_specs=[pl.BlockSpec((TL, D), lambda i: (i, 0)), pl.BlockSpec((1, D), lambda i: (0, 0)),
                   pl.BlockSpec((1, 128), lambda i: (0, 0))],
        out_shape=[SDS((L, D), F32), SDS((1, D), F32), SDS((1, 128), F32)],
        compiler_params=_cparams(("arbitrary",)),
    )(x, g, target)


def _branch_a(valw, gatew, z, cw, cb, lg, lb, pw, pb, conv):
    a = conv(valw * _sigmoid(gatew), cw) + cb
    mu = jnp.mean(a, axis=-1, keepdims=True)
    xc = a - mu
    y = xc * lax.rsqrt(jnp.mean(xc * xc, axis=-1, keepdims=True) + EPS) * lg + lb
    y = mm(_silu(y), pw) + pb
    return y * _silu(z)


def _branch_c(bg, cw_, xw, z, w3, conv):
    return bg * conv(cw_ * xw, w3) * _silu(z)


def _ac_fwd_call(proj, p):
    L = proj.shape[0]

    def body(val, gate, za, hval, hgate, cb_, cc, cx, cz, hcc, hcx,
             acw, acb, alg, alb, apw, apb, ccw, ya_ref, yc_ref):
        nf = (pl.program_id(0) > 0).astype(F32)
        win = lambda h, m: jnp.concatenate([h[...] * nf, m[...]], axis=0)
        conv_a = functools.partial(_conv_fwd_impl, tile=TL, halo=HALO, taps=KA)
        conv_c = functools.partial(_conv_fwd_impl, tile=TL, halo=HALO, taps=KC)
        ya_ref[...] = _branch_a(win(hval, val), win(hgate, gate), za[...], acw[...], acb[...], alg[...], alb[...],
                                apw[...], apb[...], conv_a)
        yc_ref[...] = _branch_c(cb_[...], win(hcc, cc), win(hcx, cx), cz[...], ccw[...], conv_c)

    col = lambda j: pl.BlockSpec((TL, BR), lambda i: (i, j))
    hal = lambda j: _halo_spec(TL, HALO, BR, j)
    full = lambda a: pl.BlockSpec(a.shape, lambda i: (0,) * a.ndim)
    params = (p["a_conv_w"], p["a_conv_b"], p["a_ln_g"], p["a_ln_b"], p["a_pw_w"], p["a_pw_b"], p["c_conv_w"])
    return pl.pallas_call(
        body, name="ac_fwd", grid=(L // TL,),
        in_specs=[col(0), col(1), col(2), hal(0), hal(1), col(5), col(6), col(7), col(8), hal(6), hal(7)]
        + [full(a) for a in params],
        out_specs=[pl.BlockSpec((TL, BR), lambda i: (i, 0))] * 2,
        out_shape=[SDS((L, BR), F32)] * 2,
        compiler_params=_cparams(("parallel",)),
    )(*([proj] * 11), *params)


def _ac_bwd_call(proj, p, dya, dyc):
    L = proj.shape[0]
    nt = L // TL

    def body(val, gate, za, hval, hgate, cb_, cc, cx, cz, hcc, hcx,
             acw, acb, alg, alb, apw, apb, ccw, dya_ref, dyc_ref,
             da_ref, dc_ref, g_acw, g_acb, g_alg, g_alb, g_apw, g_apb, g_ccw, carry):
        i = pl.program_id(0)
        gouts = (g_acw, g_acb, g_alg, g_alb, g_apw, g_apb, g_ccw)

        @pl.when(i == 0)
        def _():
            carry[...] = jnp.zeros_like(carry)
            for r in gouts:
                r[...] = jnp.zeros_like(r)

        nf = (i < nt - 1).astype(F32)
        win = lambda h, m: jnp.concatenate([h[...] * nf, m[...]], axis=0)
        conv_a = _make_conv(TL, HALO, KA)
        conv_c = _make_conv(TL, HALO, KC)

        def f(valw, gatew, z, bg, ccw_, cxw, czv, w1, b1, lg, lb, pw, pb, w3):
            return (_branch_a(valw, gatew, z, w1, b1, lg, lb, pw, pb, conv_a),
                    _branch_c(bg, ccw_, cxw, czv, w3, conv_c))

        _, vjp = jax.vjp(f, win(hval, val), win(hgate, gate), za[...], cb_[...], win(hcc, cc), win(hcx, cx), cz[...],
                         acw[...], acb[...], alg[...], alb[...], apw[...].astype(F32), apb[...], ccw[...])
        (dvalw, dgatew, dz, dbg, dccw, dcxw, dczv, d1, d2, d3, d4, d5, d6, d7) = vjp((dya_ref[...], dyc_ref[...]))

        def settle(slot, dwin):
            tail = jnp.concatenate([jnp.zeros((TL - HALO, BR), F32), carry[slot]], axis=0)
            carry[slot] = dwin[:HALO, :]
            return (dwin[HALO:, :] + tail).astype(BF16)

        da_ref[:, 0:BR] = settle(0, dvalw)
        da_ref[:, BR:2 * BR] = settle(1, dgatew)
        da_ref[:, 2 * BR:3 * BR] = dz.astype(BF16)
        dc_ref[:, 0:BR] = dbg.astype(BF16)
        dc_ref[:, BR:2 * BR] = settle(2, dccw)
        dc_ref[:, 2 * BR:3 * BR] = settle(3, dcxw)
        dc_ref[:, 3 * BR:4 * BR] = dczv.astype(BF16)
        for r, g in zip(gouts, (d1, d2, d3, d4, d5, d6, d7)):
            r[...] += g

    col = lambda j: pl.BlockSpec((TL, BR), lambda i: (nt - 1 - i, j))
    hal = lambda j: _halo_spec_rev(nt, TL, HALO, BR, j)
    full = lambda a: pl.BlockSpec(a.shape, lambda i: (0,) * a.ndim)
    params = (p["a_conv_w"], p["a_conv_b"], p["a_ln_g"], p["a_ln_b"], p["a_pw_w"], p["a_pw_b"], p["c_conv_w"])
    rev = lambda w: pl.BlockSpec((TL, w), lambda i: (nt - 1 - i, 0))
    return pl.pallas_call(
        body, name="ac_bwd", grid=(nt,),
        in_specs=[col(0), col(1), col(2), hal(0), hal(1), col(5), col(6), col(7), col(8), hal(6), hal(7)]
        + [full(a) for a in params] + [rev(BR), rev(BR)],
        out_specs=[rev(3 * BR), rev(4 * BR)] + [full(a) for a in params],
        out_shape=[SDS((L, 3 * BR), BF16), SDS((L, 4 * BR), BF16)] + [SDS(a.shape, F32) for a in params],
        scratch_shapes=[pltpu.VMEM((4, HALO, BR), F32)],
        compiler_params=_cparams(("arbitrary",)),
    )(*([proj] * 11), *params, dya, dyc)


def _iota2(shape, dim):
    return lax.broadcasted_iota(jnp.int32, shape, dim)


def _s5_params(lam_re, lam_im, logdt, b_re, b_im, c_re, c_im):
    eg = (_iota2((16, NSTATE), 1) >> 6 == _iota2((16, NSTATE), 0)).astype(F32)
    dt = jnp.exp(mmh(jnp.broadcast_to(logdt, (8, 16)), eg)[0:1, :])
    lr = jnp.minimum(lam_re, -1e-4)
    li = lam_im
    mag = jnp.exp(lr * dt)
    lbr = mag * jnp.cos(li * dt)
    lbi = mag * jnp.sin(li * dt)
    den = lr * lr + li * li
    nr = lbr - 1.0
    fr = (nr * lr + lbi * li) / den
    fi = (lbi * lr - nr * li) / den
    row = _iota2((8, NSTATE), 0)
    f8 = jnp.where(row == 0, fr, jnp.where(row == 1, fi, 0.0))
    eye = (_iota2((NSTATE, NSTATE), 0) == _iota2((NSTATE, NSTATE), 1)).astype(F32)
    fcol = mmh_nt(eye, f8)
    frc, fic = fcol[:, 0:1], fcol[:, 1:2]
    bbr = frc * b_re - fic * b_im
    bbi = frc * b_im + fic * b_re
    e1 = ((_iota2((16, BR), 1) & 15) == _iota2((16, BR), 0)).astype(F32)
    m1 = ((_iota2((NSTATE, BR), 0) >> 6) == (_iota2((NSTATE, BR), 1) >> 4)).astype(F32)
    wbr = mmh(bbr, e1) * m1
    wbi = mmh(bbi, e1) * m1
    e2 = ((_iota2((64, NSTATE), 1) & 63) == _iota2((64, NSTATE), 0)).astype(F32)
    m2 = ((_iota2((BR, NSTATE), 0) >> 4) == (_iota2((BR, NSTATE), 1) >> 6)).astype(F32)
    wcr = mmh(c_re, e2) * m2
    wci = mmh(c_im, e2) * m2
    return lbr, lbi, wbr, wbi, wcr, wci


_S5_OUT = [(1, NSTATE), (1, NSTATE), (NSTATE, BR), (NSTATE, BR), (BR, NSTATE), (BR, NSTATE)]


def _s5_prep_call(sp, deps=()):
    def body(lre, lim, ldt, bre, bim, cre, cim, *rest):
        o_lbr, o_lbi, o_wbr, o_wbi, o_wcr, o_wci, pwr, pwi, qwr, qwi = rest[len(deps):]
        lbr, lbi, wbr, wbi, wcr, wci = _s5_params(lre[...], lim[...], ldt[...], bre[...], bim[...], cre[...], cim[...])
        o_lbr[...], o_lbi[...], o_wbr[...], o_wbi[...], o_wcr[...], o_wci[...] = lbr, lbi, wbr, wbi, wcr, wci
        pr, pi = lbr, lbi
        for i in range(SEG):
            pwr[i:i + 1, :] = pr
            pwi[i:i + 1, :] = pi
            qwr[SEG - 1 - i:SEG - i, :] = pr
            qwi[SEG - 1 - i:SEG - i, :] = -pi
            pr, pi = pr * lbr - pi * lbi, pr * lbi + pi * lbr

    args = (sp["lam_re"], sp["lam_im"], sp["log_dt"], sp["b_re"], sp["b_im"], sp["c_re"], sp["c_im"])
    return pl.pallas_call(
        body, name="s5_prep",
        out_shape=[SDS(s, F32) for s in _S5_OUT] + [SDS((SEG, NSTATE), F32)] * 4,
        compiler_params=pltpu.CompilerParams(vmem_limit_bytes=VMEM_LIMIT),
    )(*args, *deps)


def _s5_prep_bwd_call(sp, cots):
    def body(lre, lim, ldt, bre, bim, cre, cim, c0, c1, c2, c3, c4, c5, *outs):
        _, vjp = jax.vjp(_s5_params, lre[...], lim[...], ldt[...], bre[...], bim[...], cre[...], cim[...])
        grads = vjp((c0[...], c1[...], c2[...], c3[...], c4[...], c5[...]))
        for o, g in zip(outs, grads):
            o[...] = g

    args = (sp["lam_re"], sp["lam_im"], sp["log_dt"], sp["b_re"], sp["b_im"], sp["c_re"], sp["c_im"])
    return pl.pallas_call(
        body, name="s5_prep_bwd",
        out_shape=[SDS(a.shape, F32) for a in args],
        compiler_params=pltpu.CompilerParams(vmem_limit_bytes=VMEM_LIMIT),
    )(*args, *cots)


def _lanes(v, j):
    return v[:, j * 128:(j + 1) * 128]


def _s5_scan(sre, sim, pwr, pwi, cin_r, cin_i, reverse):
    row = _iota2((8, 128), 0)
    steps = (1, 2, 4)

    def lane_consts(j):
        lanes = slice(j * 128, (j + 1) * 128)
        if reverse:
            mult = [(jnp.broadcast_to(pwr[SEG - d:SEG - d + 1, lanes], (8, 128)),
                     jnp.broadcast_to(pwi[SEG - d:SEG - d + 1, lanes], (8, 128))) for d in steps]
            return mult, pwr[SEG - 8:SEG, lanes], pwi[SEG - 8:SEG, lanes]
        mult = [(jnp.broadcast_to(pwr[d - 1:d, lanes], (8, 128)),
                 jnp.broadcast_to(pwi[d - 1:d, lanes], (8, 128))) for d in steps]
        return mult, pwr[0:8, lanes], pwi[0:8, lanes]

    consts = [lane_consts(j) for j in range(8)]
    nblk = TL // 8

    def block(t, carry):
        b = nblk - 1 - t if reverse else t
        rows = pl.ds(pl.multiple_of(b * 8, 8), 8)
        new = []
        for j in range(8):
            mult, p8r, p8i = consts[j]
            vr, vi = sre.at[j], sim.at[j]
            sr, si = vr[rows, :], vi[rows, :]
            for d, (mr, mi) in zip(steps, mult):
                if reverse:
                    hr = jnp.where(row < 8 - d, pltpu.roll(sr, 8 - d, 0), 0.0)
                    hi = jnp.where(row < 8 - d, pltpu.roll(si, 8 - d, 0), 0.0)
                else:
                    hr = jnp.where(row >= d, pltpu.roll(sr, d, 0), 0.0)
                    hi = jnp.where(row >= d, pltpu.roll(si, d, 0), 0.0)
                sr, si = sr + mr * hr - mi * hi, si + mr * hi + mi * hr
            cr, ci = carry[2 * j], carry[2 * j + 1]
            sr, si = sr + p8r * cr - p8i * ci, si + p8r * ci + p8i * cr
            vr[rows, :] = sr
            vi[rows, :] = si
            edge = slice(0, 1) if reverse else slice(7, 8)
            new += [sr[edge, :], si[edge, :]]
        return tuple(new)

    init = []
    for j in range(8):
        init += [_lanes(cin_r, j), _lanes(cin_i, j)]
    ends = lax.fori_loop(0, nblk, block, tuple(init))
    return (jnp.concatenate([ends[2 * j] for j in range(8)], axis=1),
            jnp.concatenate([ends[2 * j + 1] for j in range(8)], axis=1))


def _bdot(a, b, dims):
    return _dot(a.astype(BF16), b.astype(BF16), dims)


def _s5_states(u, wbr, wbi, sre, sim):
    bur = _bdot(u, wbr, NT)
    bui = _bdot(u, wbi, NT)
    for j in range(8):
        sre[j] = _lanes(bur, j)
        sim[j] = _lanes(bui, j)


def _gather_lanes(s):
    return jnp.concatenate([s[j] for j in range(8)], axis=1)


def _s5_tail(y, u, z, dsk, gw, gb):
    yg = _gelu(y + dsk * u)
    return yg * _sigmoid(mm(yg, gw) + gb) * _silu(z)


def _s5_post(s_re, s_im, u, z, wcr, wci, dsk, gw, gb):
    return _s5_tail(mm_nt(s_re, wcr) - mm_nt(s_im, wci), u, z, dsk, gw, gb)


def _s5_fwd_call(proj, prep, p):
    L = proj.shape[0]
    nt = L // TL
    lbr, lbi, wbr, wbi, wcr, wci, pwr, pwi, _, _ = prep

    def body(u_ref, z_ref, lbr_r, lbi_r, wbr_r, wbi_r, wcr_r, wci_r, pwr_r, pwi_r, d_r, gw_r, gb_r,
             yb_ref, cinr_ref, cini_ref, sre, sim, car, cai):
        @pl.when(pl.program_id(0) == 0)
        def _():
            car[...] = jnp.zeros_like(car)
            cai[...] = jnp.zeros_like(cai)

        u = u_ref[...]
        cinr_ref[0] = car[...]
        cini_ref[0] = cai[...]
        _s5_states(u, wbr_r[...], wbi_r[...], sre, sim)
        nr, ni = _s5_scan(sre, sim, pwr_r, pwi_r, car[...], cai[...], False)
        car[...] = nr
        cai[...] = ni
        yb_ref[...] = _s5_post(_gather_lanes(sre), _gather_lanes(sim), u, z_ref[...], wcr_r[...], wci_r[...],
                               d_r[...], gw_r[...], gb_r[...])

    full = lambda a: pl.BlockSpec(a.shape, lambda i: (0,) * a.ndim)
    consts = (lbr, lbi, wbr, wbi, wcr, wci, pwr, pwi, p["s5_d"], p["s5_glu_w"], p["s5_glu_b"])
    cspec = pl.BlockSpec((1, 1, NSTATE), lambda i: (i, 0, 0))
    return pl.pallas_call(
        body, name="s5_fwd", grid=(nt,),
        in_specs=[pl.BlockSpec((TL, BR), lambda i: (i, 3)), pl.BlockSpec((TL, BR), lambda i: (i, 4))]
        + [full(a) for a in consts],
        out_specs=[pl.BlockSpec((TL, BR), lambda i: (i, 0)), cspec, cspec],
        out_shape=[SDS((L, BR), F32), SDS((nt, 1, NSTATE), F32), SDS((nt, 1, NSTATE), F32)],
        scratch_shapes=[pltpu.VMEM((8, TL, 128), F32), pltpu.VMEM((8, TL, 128), F32),
                        pltpu.VMEM((1, NSTATE), F32), pltpu.VMEM((1, NSTATE), F32)],
        compiler_params=_cparams(("arbitrary",)),
    )(proj, proj, *consts)


def _s5_bwd_call(proj, prep, p, cin_r, cin_i, dyb):
    L = proj.shape[0]
    nt = L // TL
    lbr, lbi, wbr, wbi, wcr, wci, pwr, pwi, qwr, qwi = prep

    def body(u_ref, z_ref, lbr_r, lbi_r, wbr_r, wbi_r, wcr_r, wci_r, pwr_r, pwi_r, qwr_r, qwi_r, d_r, gw_r, gb_r,
             cinr_ref, cini_ref, dy_ref,
             db_ref, g_lbr, g_lbi, g_wbr, g_wbi, g_wcr, g_wci, g_d, g_gw, g_gb, sre, sim, gre, gim, car, cai):
        gouts = (g_lbr, g_lbi, g_wbr, g_wbi, g_wcr, g_wci, g_d, g_gw, g_gb)

        @pl.when(pl.program_id(0) == 0)
        def _():
            car[...] = jnp.zeros_like(car)
            cai[...] = jnp.zeros_like(cai)
            for r in gouts:
                r[...] = jnp.zeros_like(r)

        u = u_ref[...]
        c0r, c0i = cinr_ref[0], cini_ref[0]
        width = NSTATE // S5_PARTS
        tiles = width // 128
        part = lambda scr, h: jnp.concatenate([scr[h * tiles + j] for j in range(tiles)], axis=1)
        _s5_states(u, wbr_r[...], wbi_r[...], sre, sim)
        _s5_scan(sre, sim, pwr_r, pwi_r, c0r, c0i, False)
        y = None
        for h in range(S5_PARTS):
            lanes = slice(h * width, (h + 1) * width)
            yh = _bdot(part(sre, h), wcr_r[:, lanes], NT) - _bdot(part(sim, h), wci_r[:, lanes], NT)
            y = yh if y is None else y + yh
        _, vjp = jax.vjp(_s5_tail, y, u, z_ref[...], d_r[...], gw_r[...].astype(F32), gb_r[...])
        dy, du, dz, dd, dgw, dgb = vjp(dy_ref[...])
        for h in range(S5_PARTS):
            lanes = slice(h * width, (h + 1) * width)
            ds_re = _bdot(dy, wcr_r[:, lanes], NN)
            ds_im = -_bdot(dy, wci_r[:, lanes], NN)
            for j in range(tiles):
                gre[h * tiles + j] = _lanes(ds_re, j)
                gim[h * tiles + j] = _lanes(ds_im, j)
        nr, ni = _s5_scan(gre, gim, qwr_r, qwi_r, car[...], cai[...], True)
        car[...] = nr
        cai[...] = ni
        first = _iota2((TL, width), 0) == 0
        for h in range(S5_PARTS):
            lanes = slice(h * width, (h + 1) * width)
            s_re, s_im, a_re, a_im = part(sre, h), part(sim, h), part(gre, h), part(gim, h)
            g_wcr[:, lanes] += _bdot(dy, s_re, TN)
            g_wci[:, lanes] += -_bdot(dy, s_im, TN)
            p_re = jnp.where(first, c0r[:, lanes], jnp.roll(s_re, 1, axis=0))
            p_im = jnp.where(first, c0i[:, lanes], jnp.roll(s_im, 1, axis=0))
            g_lbr[:, lanes] += jnp.sum(a_re * p_re + a_im * p_im, axis=0, keepdims=True)
            g_lbi[:, lanes] += jnp.sum(a_im * p_re - a_re * p_im, axis=0, keepdims=True)
            du = du + _bdot(a_re, wbr_r[lanes, :], NN) + _bdot(a_im, wbi_r[lanes, :], NN)
            g_wbr[lanes, :] += _bdot(a_re, u, TN)
            g_wbi[lanes, :] += _bdot(a_im, u, TN)
        g_d[...] += dd
        g_gw[...] += dgw
        g_gb[...] += dgb
        db_ref[:, 0:BR] = du.astype(BF16)
        db_ref[:, BR:2 * BR] = dz.astype(BF16)

    full = lambda a: pl.BlockSpec(a.shape, lambda i: (0,) * a.ndim)
    consts = (lbr, lbi, wbr, wbi, wcr, wci, pwr, pwi, qwr, qwi, p["s5_d"], p["s5_glu_w"], p["s5_glu_b"])
    cspec = pl.BlockSpec((1, 1, NSTATE), lambda i: (nt - 1 - i, 0, 0))
    gshapes = _S5_OUT + [(1, BR), (BR, BR), (1, BR)]
    return pl.pallas_call(
        body, name="s5_bwd", grid=(nt,),
        in_specs=[pl.BlockSpec((TL, BR), lambda i: (nt - 1 - i, 3)), pl.BlockSpec((TL, BR), lambda i: (nt - 1 - i, 4))]
        + [full(a) for a in consts] + [cspec, cspec, pl.BlockSpec((TL, BR), lambda i: (nt - 1 - i, 0))],
        out_specs=[pl.BlockSpec((TL, 2 * BR), lambda i: (nt - 1 - i, 0))]
        + [pl.BlockSpec(s, lambda i: (0, 0)) for s in gshapes],
        out_shape=[SDS((L, 2 * BR), BF16)] + [SDS(s, F32) for s in gshapes],
        scratch_shapes=[pltpu.VMEM((8, TL, 128), F32)] * 4 + [pltpu.VMEM((1, NSTATE), F32)] * 2,
        compiler_params=_cparams(("arbitrary",)),
    )(proj, proj, *consts, cin_r, cin_i, dyb)


def _heads(x):
    return [x[:, h * HD:(h + 1) * HD] for h in range(NH)]


def _l2n(x, scale):
    return jnp.concatenate([xh * (lax.rsqrt(jnp.sum(xh * xh, axis=-1, keepdims=True) + EPS) * scale)
                            for xh in _heads(x)], axis=1)


def _dn_pre(qkvw, ab, cw, alog, dtb, conv, rows):
    c = _silu(conv(qkvw, cw))
    q = _l2n(c[:, 0:BR], HD ** -0.5)
    k = _l2n(c[:, BR:2 * BR], 1.0)
    v = c[:, 2 * BR:3 * BR]
    g = -jnp.exp(alog) * _softplus(ab + dtb)
    ri, ci = _iota2((rows, rows), 0), _iota2((rows, rows), 1)
    tri = ((ri >= ci) & ((ri >> 6) == (ci >> 6))).astype(F32)
    gc = mmh(tri, g)
    lane = _iota2(ab.shape, 1)
    return q, k, v, jnp.where(lane < NH, gc, jnp.where(lane < 2 * NH, _sigmoid(ab), 0.0))


def _dn_pre_fwd_call(proj, p):
    L = proj.shape[0]

    def body(m_ref, h_ref, ab_ref, cw, alog, dtb, q_ref, k_ref, v_ref, gb_ref):
        nf = (pl.program_id(0) > 0).astype(F32)
        qkvw = jnp.concatenate([h_ref[...] * nf, m_ref[...]], axis=0)
        conv = functools.partial(_conv_fwd_impl, tile=TL, halo=HALO_S, taps=KD)
        q_ref[...], k_ref[...], v_ref[...], gb_ref[...] = _dn_pre(qkvw, ab_ref[...], cw[...], alog[...], dtb[...], conv, TL)

    full = lambda a: pl.BlockSpec(a.shape, lambda i: (0,) * a.ndim)
    params = (p["d_conv_w"], p["d_a_log"], p["d_dt_bias"])
    o = pl.BlockSpec((TL, BR), lambda i: (i, 0))
    return pl.pallas_call(
        body, name="dn_pre_fwd", grid=(L // TL,),
        in_specs=[pl.BlockSpec((TL, 3 * BR), lambda i: (i, 3)), _halo_spec(TL, HALO_S, 3 * BR, 3),
                  pl.BlockSpec((TL, 128), lambda i: (i, AB_COL // 128))] + [full(a) for a in params],
        out_specs=[o, o, o, pl.BlockSpec((TL, 128), lambda i: (i, 0))],
        out_shape=[SDS((L, BR), F32)] * 3 + [SDS((L, 128), F32)],
        compiler_params=_cparams(("parallel",)),
    )(proj, proj, proj, *params)


def _dn_pre_bwd_call(proj, p, dq, dk, dv, dgb):
    L = proj.shape[0]
    nt = L // TL

    def body(m_ref, h_ref, ab_ref, cw, alog, dtb, dq_r, dk_r, dv_r, dgb_r,
             dqkv_ref, dab_ref, g_cw, g_alog, g_dtb, carry):
        i = pl.program_id(0)

        @pl.when(i == 0)
        def _():
            carry[...] = jnp.zeros_like(carry)
            for r in (g_cw, g_alog, g_dtb):
                r[...] = jnp.zeros_like(r)

        nf = (i < nt - 1).astype(F32)
        qkvw = jnp.concatenate([h_ref[...] * nf, m_ref[...]], axis=0)
        conv = _make_conv(TL, HALO_S, KD)
        _, vjp = jax.vjp(lambda a, b, c, d, e: _dn_pre(a, b, c, d, e, conv, TL),
                         qkvw, ab_ref[...], cw[...], alog[...], dtb[...])
        dwin, dab, dcw, dalog, ddtb = vjp((dq_r[...], dk_r[...], dv_r[...], dgb_r[...]))
        tail = jnp.concatenate([jnp.zeros((TL - HALO_S, 3 * BR), F32), carry[...]], axis=0)
        carry[...] = dwin[:HALO_S, :]
        dqkv_ref[...] = (dwin[HALO_S:, :] + tail).astype(BF16)
        dab_ref[...] = dab.astype(BF16)
        g_cw[...] += dcw
        g_alog[...] += dalog
        g_dtb[...] += ddtb

    full = lambda a: pl.BlockSpec(a.shape, lambda i: (0,) * a.ndim)
    params = (p["d_conv_w"], p["d_a_log"], p["d_dt_bias"])
    rev = lambda w: pl.BlockSpec((TL, w), lambda i: (nt - 1 - i, 0))
    return pl.pallas_call(
        body, name="dn_pre_bwd", grid=(nt,),
        in_specs=[pl.BlockSpec((TL, 3 * BR), lambda i: (nt - 1 - i, 3)), _halo_spec_rev(nt, TL, HALO_S, 3 * BR, 3),
                  pl.BlockSpec((TL, 128), lambda i: (nt - 1 - i, AB_COL // 128))] + [full(a) for a in params]
        + [rev(BR), rev(BR), rev(BR), rev(128)],
        out_specs=[rev(3 * BR), rev(128)] + [full(a) for a in params],
        out_shape=[SDS((L, 3 * BR), BF16), SDS((L, 128), BF16)] + [SDS(a.shape, F32) for a in params],
        scratch_shapes=[pltpu.VMEM((HALO_S, 3 * BR), F32)],
        compiler_params=_cparams(("arbitrary",)),
    )(proj, proj, proj, *params, dq, dk, dv, dgb)


def _dn_group(q, k, v, gb, z, ng, *s):
    return _dn_chunks(q, k, v, gb, z, ng, s, DN_GROUP, None)


def _dn_chunks(q, k, v, gb, z, ng, s, chunks, entering):
    ri, ci = _iota2((CH, CH), 0), _iota2((CH, CH), 1)
    causal, strict = ri >= ci, ri > ci
    eye = (ri == ci).astype(F32)
    s = list(s)
    pairs = []
    for c in range(chunks):
        rows = slice(c * CH, (c + 1) * CH)
        gbc = gb[rows, :]
        for h, (qh, kh, vh, zh) in enumerate(zip(_heads(q[rows, :]), _heads(k[rows, :]), _heads(v[rows, :]),
                                                 _heads(z[rows, :]))):
            gc = jnp.broadcast_to(gbc[:, h:h + 1], (CH, HD))
            beta = gbc[:, NH + h:NH + h + 1]
            decay = jnp.where(causal, jnp.exp(jnp.where(causal, gc - gc.T, 0.0)), 0.0)
            egc = jnp.exp(gc)
            glast = gc[CH - 1:CH, :]
            kb = kh * beta
            pairs.append(dict(q=qh, k=kh, z=zh, decay=decay, qe=qh * egc, kd=kh * jnp.exp(glast - gc),
                              sdec=jnp.exp(glast[:, 0:1]), kb=kb, rhs=jnp.concatenate([vh * beta, kb * egc], axis=1)))
    for p in pairs:
        p["pw"] = jnp.where(strict, mm_nt(p["kb"], p["k"]) * p["decay"], 0.0)
    for p in pairs:
        p["t"] = eye - p["pw"]
    for _ in range(5):
        for p in pairs:
            p["pw"] = mm(p["pw"], p["pw"])
        for p in pairs:
            p["t"] = mm(p["t"], eye + p["pw"])
    for p in pairs:
        p["uw"] = mm(p["t"], p["rhs"])
    for p in pairs:
        p["attn"] = mm_nt(p["q"], p["k"]) * p["decay"]
    out_rows = []
    for c in range(chunks):
        if entering is not None and c > 0 and c % DN_GROUP == 0:
            entering.append(list(s))
        grp = pairs[c * NH:(c + 1) * NH]
        ws = [mm(jnp.concatenate([p["uw"][:, HD:], p["qe"]], axis=0), s[h]) for h, p in enumerate(grp)]
        v_new = [p["uw"][:, :HD] - w_[:CH, :] for p, w_ in zip(grp, ws)]
        o = [w_[CH:, :] + mm(p["attn"], vn) for p, w_, vn in zip(grp, ws, v_new)]
        s = [s[h] * p["sdec"] + mm_tn(p["kd"], vn) for h, (p, vn) in enumerate(zip(grp, v_new))]
        o = [oh * lax.rsqrt(jnp.mean(oh * oh, axis=-1, keepdims=True) + EPS) * ng * _silu(p["z"]) for oh, p in zip(o, grp)]
        out_rows.append(jnp.concatenate(o, axis=1))
    return (jnp.concatenate(out_rows, axis=0), *s)


def _dn_core_fwd_call(proj, q, k, v, gb, ng):
    L = q.shape[0]
    rows = DN_FWD_GROUPS * DN_GROUP * CH
    steps = L // rows

    def body(q_r, k_r, v_r, gb_r, z_r, ng_r, yd_ref, ssave_ref, s_scr):
        @pl.when(pl.program_id(0) == 0)
        def _():
            s_scr[...] = jnp.zeros_like(s_scr)

        entering = [[s_scr[h] for h in range(NH)]]
        yd, *s2 = _dn_chunks(q_r[...], k_r[...], v_r[...], gb_r[...], z_r[...], ng_r[...], entering[0],
                             DN_FWD_GROUPS * DN_GROUP, entering)
        yd_ref[...] = yd
        for h in range(NH):
            s_scr[h] = s2[h]
            for g, states in enumerate(entering):
                ssave_ref[g, h] = states[h]

    c = pl.BlockSpec((rows, BR), lambda i: (i, 0))
    return pl.pallas_call(
        body, name="dn_core_fwd", grid=(steps,),
        in_specs=[c, c, c, pl.BlockSpec((rows, 128), lambda i: (i, 0)), pl.BlockSpec((rows, BR), lambda i: (i, 12)),
                  pl.BlockSpec((1, HD), lambda i: (0, 0))],
        out_specs=[c, pl.BlockSpec((DN_FWD_GROUPS, NH, HD, HD), lambda i: (i, 0, 0, 0))],
        out_shape=[SDS((L, BR), F32), SDS((steps * DN_FWD_GROUPS, NH, HD, HD), F32)],
        scratch_shapes=[pltpu.VMEM((NH, HD, HD), F32)],
        compiler_params=_cparams(("arbitrary",)),
    )(q, k, v, gb, proj, ng)


def _dn_core_bwd_call(proj, q, k, v, gb, ng, ssave, dyd):
    L = q.shape[0]
    rows = DN_GROUP * CH
    ng_ = L // rows

    def body(q_r, k_r, v_r, gb_r, z_r, ng_r, s_r, dy_r, dq_ref, dk_ref, dv_ref, dgb_ref, dz_ref, g_ng, ds_scr):
        @pl.when(pl.program_id(0) == 0)
        def _():
            ds_scr[...] = jnp.zeros_like(ds_scr)
            g_ng[...] = jnp.zeros_like(g_ng)

        _, vjp = jax.vjp(_dn_group, q_r[...], k_r[...], v_r[...], gb_r[...], z_r[...], ng_r[...],
                         *[s_r[0, h] for h in range(NH)])
        dq, dk, dv, dgb, dz, dng, *ds = vjp((dy_r[...], *[ds_scr[h] for h in range(NH)]))
        dq_ref[...], dk_ref[...], dv_ref[...], dgb_ref[...] = dq, dk, dv, dgb
        dz_ref[...] = dz.astype(BF16)
        g_ng[...] += dng
        for h in range(NH):
            ds_scr[h] = ds[h]

    c = pl.BlockSpec((rows, BR), lambda i: (ng_ - 1 - i, 0))
    c128 = pl.BlockSpec((rows, 128), lambda i: (ng_ - 1 - i, 0))
    return pl.pallas_call(
        body, name="dn_core_bwd", grid=(ng_,),
        in_specs=[c, c, c, c128, pl.BlockSpec((rows, BR), lambda i: (ng_ - 1 - i, 12)),
                  pl.BlockSpec((1, HD), lambda i: (0, 0)),
                  pl.BlockSpec((1, NH, HD, HD), lambda i: (ng_ - 1 - i, 0, 0, 0)), c],
        out_specs=[c, c, c, c128, c, pl.BlockSpec((1, HD), lambda i: (0, 0))],
        out_shape=[SDS((L, BR), F32)] * 3 + [SDS((L, 128), F32), SDS((L, BR), BF16), SDS((1, HD), F32)],
        scratch_shapes=[pltpu.VMEM((NH, HD, HD), F32)],
        compiler_params=_cparams(("arbitrary",)),
    )(q, k, v, gb, proj, ng, ssave, dyd)


def _outproj_bwd_call(dx, ys, w, deps=()):
    L = dx.shape[0]

    def body(dx_ref, a_ref, b_ref, c_ref, d_ref, w_ref, *rest):
        da, db, dc, dd, dw_ref = rest[len(deps):]

        @pl.when(pl.program_id(0) == 0)
        def _():
            dw_ref[...] = jnp.zeros_like(dw_ref)

        dxb = dx_ref[...].astype(BF16)
        dmixed = _dot(dxb, w_ref[...], NT)
        for b, o_ref in enumerate((da, db, dc, dd)):
            o_ref[...] = dmixed[:, b * BR:(b + 1) * BR]
        mixed = jnp.concatenate([y_ref[...].astype(BF16) for y_ref in (a_ref, b_ref, c_ref, d_ref)], axis=1)
        dw_ref[...] += _dot(mixed, dxb, TN)

    yspec = pl.BlockSpec((TL, BR), lambda i: (i, 0))
    return pl.pallas_call(
        body, name="outproj_bwd", grid=(L // TL,),
        in_specs=[pl.BlockSpec((TL, D), lambda i: (i, 0)), yspec, yspec, yspec, yspec,
                  pl.BlockSpec((D, D), lambda i: (0, 0))] + _dep_specs(deps),
        out_specs=[yspec] * 4 + [pl.BlockSpec((D, D), lambda i: (0, 0))],
        out_shape=[SDS((L, BR), F32)] * 4 + [SDS((D, D), F32)],
        compiler_params=_cparams(("arbitrary",)),
    )(dx, *ys, w, *deps)


def _slab_cols(slabs):
    widths = [s.shape[1] for s in slabs]
    starts = [sum(widths[:i]) for i in range(len(widths))]
    assert starts[-1] + widths[-1] == PW
    return list(zip(starts, widths))


def _inproj_bwd_x_call(slabs, w, x, g, dx_next, deps=()):
    L = x.shape[0]
    cols = _slab_cols(slabs)
    n = len(slabs)

    def body(*refs):
        dp_refs, (w_ref, x_ref, g_ref, dxn_ref) = refs[:n], refs[n:n + 4]
        dx_ref, dg_ref = refs[n + 4 + len(deps):]

        @pl.when(pl.program_id(0) == 0)
        def _():
            dg_ref[...] = jnp.zeros_like(dg_ref)

        dh = _dot(jnp.concatenate([dp_ref[...] for dp_ref in dp_refs], axis=1), w_ref[...], NT)
        _, vjp = jax.vjp(_rms, x_ref[...], g_ref[...])
        dx, dg = vjp(dh)
        dx_ref[...] = dx + dxn_ref[...]
        dg_ref[...] += dg

    row = lambda w_: pl.BlockSpec((TL, w_), lambda i: (i, 0))
    return pl.pallas_call(
        body, name="inproj_bwd_x", grid=(L // TL,),
        in_specs=[row(cw) for _, cw in cols]
        + [pl.BlockSpec((D, PW), lambda i: (0, 0)), row(D), pl.BlockSpec((1, D), lambda i: (0, 0)), row(D)]
        + _dep_specs(deps),
        out_specs=[row(D), pl.BlockSpec((1, D), lambda i: (0, 0))],
        out_shape=[SDS((L, D), F32), SDS((1, D), F32)],
        compiler_params=_cparams(("arbitrary",)),
    )(*slabs, w, x, g, dx_next, *deps)


def _inproj_bwd_w_call(h, slabs):
    L = h.shape[0]
    cols = _slab_cols(slabs)
    n = len(slabs)

    def body(*refs):
        h_ref, dp_refs, dw_ref = refs[0], refs[1:1 + n], refs[1 + n]

        @pl.when(pl.program_id(0) == 0)
        def _():
            dw_ref[...] = jnp.zeros_like(dw_ref)

        hv = h_ref[...]
        for dp_ref, (c0, cw) in zip(dp_refs, cols):
            dw_ref[:, c0:c0 + cw] += _dot(hv, dp_ref[...], TN)

    row = lambda w_: pl.BlockSpec((TL, w_), lambda i: (i, 0))
    return pl.pallas_call(
        body, name="inproj_bwd_w", grid=(L // TL,),
        in_specs=[row(D)] + [row(cw) for _, cw in cols],
        out_specs=pl.BlockSpec((D, PW), lambda i: (0, 0)),
        out_shape=SDS((D, PW), F32),
        compiler_params=_cparams(("arbitrary",)),
    )(h, *slabs)


def _exchange_call(name, flows):
    n = len(flows)

    def body(*refs):
        srcs, dsts = refs[:n], refs[n:2 * n]
        send_sems, recv_sems, local_sems = refs[2 * n:]
        x, y, c = lax.axis_index("x"), lax.axis_index("y"), lax.axis_index("c")
        me = 4 * x + 2 * y + c
        copies = []
        for mask in range(1, N_DEV):
            px = 1 - x if mask & 4 else x
            py = 1 - y if mask & 2 else y
            pc = 1 - c if mask & 1 else c
            for f, (_, src_at, _, dst_at) in enumerate(flows):
                cp = pltpu.make_async_remote_copy(
                    src_ref=src_at(srcs[f], 4 * px + 2 * py + pc), dst_ref=dst_at(dsts[f], me),
                    send_sem=send_sems.at[mask - 1, f], recv_sem=recv_sems.at[mask - 1, f],
                    device_id=(px, py, pc), device_id_type=pl.DeviceIdType.MESH)
                cp.start()
                copies.append(cp)
        mine = [pltpu.make_async_copy(src_at(srcs[f], me), dst_at(dsts[f], me), local_sems.at[f])
                for f, (_, src_at, _, dst_at) in enumerate(flows)]
        for cp in mine:
            cp.start()
        for cp in copies + mine:
            cp.wait()

    return pl.pallas_call(
        body, name=name,
        in_specs=[pl.BlockSpec(memory_space=pl.ANY)] * n,
        out_specs=[pl.BlockSpec(memory_space=pl.ANY)] * n,
        out_shape=[SDS(tuple(shape), src.dtype) for src, _, shape, _ in flows],
        scratch_shapes=[pltpu.SemaphoreType.DMA((N_DEV - 1, n)), pltpu.SemaphoreType.DMA((N_DEV - 1, n)),
                        pltpu.SemaphoreType.DMA((n,))],
    )(*[f[0] for f in flows])


def _whole(ref, _):
    return ref


def _slot(ref, k):
    return ref.at[k]


_HBM_SPEC = pl.BlockSpec(memory_space=pltpu.HBM)
_SEM_SPEC = pl.BlockSpec(memory_space=pltpu.SEMAPHORE)
_DATAFLOW = pltpu.SideEffectType.DATAFLOW_SIDE_EFFECTING


def _split_copies(views, src_refs, land_refs, send_sems, recv_sems):
    x, y, c = lax.axis_index("x"), lax.axis_index("y"), lax.axis_index("c")
    me = 4 * x + 2 * y + c
    copies = []
    for mask in range(1, N_DEV):
        px = 1 - x if mask & 4 else x
        py = 1 - y if mask & 2 else y
        pc = 1 - c if mask & 1 else c
        for f, (src_at, dst_at) in enumerate(views):
            pair = (mask - 1) * len(views) + f
            copies.append(pltpu.make_async_remote_copy(
                src_ref=src_at(src_refs[f], 4 * px + 2 * py + pc), dst_ref=dst_at(land_refs[f], me),
                send_sem=send_sems.at[pair], recv_sem=recv_sems.at[pair],
                device_id=(px, py, pc), device_id_type=pl.DeviceIdType.MESH))
    return copies


def _split_start_call(name, srcs, lands, views):
    n = len(srcs)

    def body(*refs):
        src_refs, land_refs = refs[:n], refs[n:2 * n]
        send_sems, recv_sems, token = refs[2 * n], refs[2 * n + 1], refs[-1]
        for cp in _split_copies(views, src_refs, land_refs, send_sems, recv_sems):
            cp.start()
        token[...] = jnp.zeros_like(token)

    arrays = list(srcs) + list(lands)
    outs = pl.pallas_call(
        body, name=name,
        out_shape=(pltpu.SemaphoreType.DMA(((N_DEV - 1) * n,)), pltpu.SemaphoreType.DMA(((N_DEV - 1) * n,)),
                   *[pltpu.HBM(a.shape, a.dtype) for a in arrays], SDS((8, 128), F32)),
        in_specs=[_HBM_SPEC] * (2 * n),
        out_specs=(_SEM_SPEC, _SEM_SPEC, *[_HBM_SPEC] * (2 * n), pl.BlockSpec(memory_space=pltpu.VMEM)),
        input_output_aliases={i: 2 + i for i in range(2 * n)},
        compiler_params=pltpu.CompilerParams(has_side_effects=_DATAFLOW),
    )(*[pltpu.with_memory_space_constraint(a, pltpu.HBM) for a in arrays])
    return outs[0], outs[1], list(outs[2:2 + 2 * n]), outs[-1]


def _split_wait_call(name, send_sems, recv_sems, thru, views, after):
    n = len(views)

    def body(*refs):
        src_refs, land_refs = refs[:n], refs[n:2 * n]
        send, recv = refs[2 * n], refs[2 * n + 1]
        for cp in _split_copies(views, src_refs, land_refs, send, recv):
            cp.wait_send()
            cp.wait_recv()

    outs = pl.pallas_call(
        body, name=name,
        out_shape=tuple(pltpu.HBM(a.shape, a.dtype) for a in thru),
        in_specs=[_HBM_SPEC] * (2 * n) + [_SEM_SPEC, _SEM_SPEC, pl.BlockSpec(memory_space=pl.ANY)],
        out_specs=tuple([_HBM_SPEC] * (2 * n)),
        input_output_aliases={i: i for i in range(2 * n)},
        compiler_params=pltpu.CompilerParams(has_side_effects=_DATAFLOW),
    )(*thru, send_sems, recv_sems, after)
    return list(outs[n:])


def _own_slot(block, me):
    zone = lax.empty((N_DEV,) + block.shape, block.dtype)
    return lax.dynamic_update_slice(zone, block[None], (me,) + (0,) * block.ndim)


def _reduce_adamw_call(parts, w, m, v, block, name):
    nsrc = parts.shape[0]
    grid = tuple(s // b for s, b in zip(w.shape, block))
    c1 = 1.0 - ADAM_B1 ** ADAM_STEP
    c2 = 1.0 - ADAM_B2 ** ADAM_STEP

    def body(p_ref, w_ref, m_ref, v_ref, g_ref, d_ref, nm_ref, nv_ref):
        g = p_ref[0].astype(F32)
        for k in range(1, nsrc):
            g = g + p_ref[k].astype(F32)
        nm = ADAM_B1 * m_ref[...] + (1.0 - ADAM_B1) * g
        nv = ADAM_B2 * v_ref[...] + (1.0 - ADAM_B2) * (g * g)
        g_ref[...] = g
        nm_ref[...] = nm
        nv_ref[...] = nv
        d_ref[...] = -ADAM_LR * ((nm / c1) / (jnp.sqrt(nv / c2) + ADAM_EPS) + ADAM_WD * w_ref[...])

    own = pl.BlockSpec(tuple(block), lambda *i: i)
    return pl.pallas_call(
        body, name=name, grid=grid,
        in_specs=[pl.BlockSpec((nsrc,) + tuple(block), lambda *i: (0,) + i), own, own, own],
        out_specs=[own] * 4,
        out_shape=[SDS(w.shape, F32)] * 4,
        compiler_params=_cparams(("parallel",) * len(grid)),
    )(parts, w, m, v)


RELAYOUT_ROWS = 256
SHARD_COLS = IN_COLS // N_DEV


def _win_gather_layout_call(shards):
    def body(w_ref, o_ref):
        nat = jnp.concatenate([w_ref[k].astype(F32) for k in range(N_DEV)], axis=1)
        out = jnp.concatenate([nat[:, :3072], nat[:, 3080:], nat[:, 3072:3080],
                               jnp.zeros((RELAYOUT_ROWS, PW - IN_COLS), F32)], axis=1)
        o_ref[...] = out.astype(BF16)

    return pl.pallas_call(
        body, name="w_in_layout", grid=(D // RELAYOUT_ROWS,),
        in_specs=[pl.BlockSpec((N_DEV, RELAYOUT_ROWS, SHARD_COLS), lambda i: (0, i, 0))],
        out_specs=pl.BlockSpec((RELAYOUT_ROWS, PW), lambda i: (i, 0)),
        out_shape=SDS((D, PW), BF16),
        compiler_params=_cparams(("parallel",)),
    )(shards)


def _win_scatter_layout_call(grad):
    def body(g_ref, o_ref):
        g = g_ref[...]
        nat = jnp.concatenate([g[:, :3072], g[:, AB_COL:AB_COL + 8], g[:, 3072:AB_COL]], axis=1)
        for k in range(N_DEV):
            o_ref[k] = nat[:, SHARD_COLS * k:SHARD_COLS * (k + 1)].astype(BF16)

    return pl.pallas_call(
        body, name="w_in_grad_layout", grid=(D // RELAYOUT_ROWS,),
        in_specs=[pl.BlockSpec((RELAYOUT_ROWS, PW), lambda i: (i, 0))],
        out_specs=pl.BlockSpec((N_DEV, RELAYOUT_ROWS, SHARD_COLS), lambda i: (0, i, 0)),
        out_shape=SDS((N_DEV, D, SHARD_COLS), BF16),
        compiler_params=_cparams(("parallel",)),
    )(grad)


def _reduce_adamw_layers_call(parts, w, m, v, rows, name):
    _, R, C = w.shape
    c1 = 1.0 - ADAM_B1 ** ADAM_STEP
    c2 = 1.0 - ADAM_B2 ** ADAM_STEP

    def body(*refs):
        p_refs = refs[:DEPTH]
        w_ref, m_ref, v_ref, g_ref, d_ref, nm_ref, nv_ref = refs[DEPTH:]
        for l in range(DEPTH):
            @pl.when(pl.program_id(0) == l)
            def _(l=l):
                g = p_refs[l][0].astype(F32)
                for k in range(1, N_DEV):
                    g = g + p_refs[l][k].astype(F32)
                nm = ADAM_B1 * m_ref[0] + (1.0 - ADAM_B1) * g
                nv = ADAM_B2 * v_ref[0] + (1.0 - ADAM_B2) * (g * g)
                g_ref[0] = g
                nm_ref[0] = nm
                nv_ref[0] = nv
                d_ref[0] = -ADAM_LR * ((nm / c1) / (jnp.sqrt(nv / c2) + ADAM_EPS) + ADAM_WD * w_ref[0])

    def part_spec(l):
        return pl.BlockSpec((N_DEV, rows, C), lambda j, i: (0, jnp.where(j == l, i, 0), 0))

    own = pl.BlockSpec((1, rows, C), lambda j, i: (j, i, 0))
    return pl.pallas_call(
        body, name=name, grid=(DEPTH, R // rows),
        in_specs=[part_spec(l) for l in range(DEPTH)] + [own, own, own],
        out_specs=[own] * 4,
        out_shape=[SDS(w.shape, F32)] * 4,
        compiler_params=_cparams(("arbitrary", "arbitrary")),
    )(*parts, w, m, v)


_BIG = ("w_in", "w_out", "a_pw_w", "s5_glu_w")
_CONV = ("a_conv_w", "c_conv_w", "d_conv_w")
_CONV_TAPS = {"a_conv_w": KA, "c_conv_w": KC, "d_conv_w": KD}
_CONV_ROWS = {"a_conv_w": HALO, "c_conv_w": HALO_S, "d_conv_w": HALO_S}
_CONV_WIDTH = {"a_conv_w": BR, "c_conv_w": BR, "d_conv_w": 3 * BR}
_REPLICATED = ("norm_g", "a_conv_b", "a_ln_g", "a_ln_b", "a_pw_b", "s5_lambda_re", "s5_lambda_im", "s5_b_re", "s5_b_im",
               "s5_c_re", "s5_c_im", "s5_d", "s5_log_dt", "s5_glu_b", "d_a_log", "d_dt_bias", "d_norm_g", "final_g")
_REP_TAIL = ("norm_g", "final_g")
_REP_LAYER = tuple(n for n in _REPLICATED if n not in _REP_TAIL)
_WEIGHTS = ("norm_g", "w_in", "a_conv_w", "a_conv_b", "a_ln_g", "a_ln_b", "a_pw_w", "a_pw_b", "s5_lambda_re",
            "s5_lambda_im", "s5_b_re", "s5_b_im", "s5_c_re", "s5_c_im", "s5_d", "s5_log_dt", "s5_glu_w", "s5_glu_b",
            "c_conv_w", "d_conv_w", "d_a_log", "d_dt_bias", "d_norm_g", "w_out", "final_g")


def _size(shape):
    n = 1
    for s in shape:
        n *= s
    return n


PACK_ALIGN = 1024


def _piece_rows(n):
    return -(-n // PACK_ALIGN) * (PACK_ALIGN // 128)


def _pack_rows(pieces, row_mult):
    rows = []
    for p in pieces:
        flat = p.reshape(-1)
        rows.append(jnp.pad(flat, (0, _piece_rows(flat.shape[0]) * 128 - flat.shape[0])).reshape(-1, 128))
    out = jnp.concatenate(rows, axis=0)
    return jnp.pad(out, ((0, (-out.shape[0]) % row_mult), (0, 0)))


def _pack_layers(pieces):
    rows = []
    for p in pieces:
        flat = p.reshape(DEPTH, -1)
        nr = _piece_rows(flat.shape[1])
        rows.append(jnp.pad(flat, ((0, 0), (0, nr * 128 - flat.shape[1]))).reshape(DEPTH, nr, 128))
    return jnp.concatenate(rows, axis=1)


def _unpack_layers(packed, shapes):
    out, row = [], 0
    for s in shapes:
        n = _size(s[1:])
        nr = _piece_rows(n)
        out.append(packed[:, row:row + nr].reshape(DEPTH, -1)[:, :n].reshape(s))
        row += nr
    return out


def _unpack(packed, shapes):
    out, row = [], 0
    for s in shapes:
        n = _size(s)
        nr = _piece_rows(n)
        out.append(packed[row:row + nr].reshape(-1)[:n].reshape(s))
        row += nr
    return out


_GATHER_VIEWS = [(_whole, _slot)] * 5


def _gather_start(shards, layer, me):
    srcs = [shards[n].astype(BF16) for n in _BIG]
    srcs.append(_pack_rows([shards[n] for n in _CONV], 8))
    lands = [_own_slot(s, me) for s in srcs]
    return _split_start_call("gather_start_%d" % layer, srcs, lands, _GATHER_VIEWS)


def _gather_finish(weights, layer, started, after):
    send, recv, thru, _ = started
    w_in, w_out, a_pw, glu, conv_all = _split_wait_call("gather_wait_%d" % layer, send, recv, thru, _GATHER_VIEWS, after)
    full = {"w_in": _win_gather_layout_call(w_in), "w_out": w_out.reshape(D, D), "a_pw_w": a_pw.reshape(BR, BR),
            "s5_glu_w": glu.reshape(BR, BR)}
    shapes = [weights[n].shape[1:] for n in _CONV]
    per_dev = [_unpack(conv_all[k], shapes) for k in range(N_DEV)]
    for i, n in enumerate(_CONV):
        whole = jnp.concatenate([per_dev[k][i] for k in range(N_DEV)], axis=-1)
        full[n] = jnp.pad(whole, ((0, _CONV_ROWS[n] - _CONV_TAPS[n]), (0, 0)))
    return full


def _rows_view(rows):
    return lambda ref, k: ref.at[pl.ds(k * rows, rows), :]


_SCATTER_VIEWS = [(_slot, _slot), (_rows_view(D // N_DEV), _slot), (_rows_view(BR // N_DEV), _slot),
                  (_rows_view(BR // N_DEV), _slot), (_whole, _slot)]


def _scatter_start(grads, small, layer, me):
    srcs = [_win_scatter_layout_call(grads["w_in"])] + [grads[n].astype(BF16) for n in _BIG[1:]]
    own = [lax.dynamic_index_in_dim(srcs[0], me, 0, keepdims=False)]
    for s, rows in zip(srcs[1:], (D // N_DEV, BR // N_DEV, BR // N_DEV)):
        own.append(lax.dynamic_slice_in_dim(s, me * rows, rows, axis=0))
    lands = [_own_slot(o, me) for o in own + [small]]
    return _split_start_call("scatter_start_%d" % layer, srcs + [small], lands, _SCATTER_VIEWS)


_S5_KERNEL_SHAPES = {"s5_lambda_re": (1, NSTATE), "s5_lambda_im": (1, NSTATE), "s5_log_dt": (1, 16),
                     "s5_b_re": (NSTATE, 16), "s5_b_im": (NSTATE, 16), "s5_c_re": (BR, 64), "s5_c_im": (BR, 64)}
_S5_KEYS = {"s5_lambda_re": "lam_re", "s5_lambda_im": "lam_im", "s5_log_dt": "log_dt", "s5_b_re": "b_re",
            "s5_b_im": "b_im", "s5_c_re": "c_re", "s5_c_im": "c_im"}


def _s5_inputs_all(weights):
    return {n: weights[n].reshape((DEPTH,) + s) for n, s in _S5_KERNEL_SHAPES.items()}


def _s5_inputs(p):
    return {_S5_KEYS[n]: p["s5_in"][n] for n in _S5_KERNEL_SHAPES}


def _row(a, width=None):
    a = a.reshape(1, -1)
    return a if width is None else jnp.pad(a, ((0, 0), (0, width - a.shape[1])))


def _layer_params(p):
    q = dict(p)
    for n in ("norm_g", "a_conv_b", "a_ln_g", "a_ln_b", "a_pw_b", "s5_d", "s5_glu_b", "d_norm_g"):
        q[n] = _row(p[n])
    q["d_a_log"] = _row(p["d_a_log"], 128)
    q["d_dt_bias"] = _row(p["d_dt_bias"], 128)
    return q


def _layer_fwd(x, p, deps=()):
    q = _layer_params(p)
    proj, h = _inproj_call(x, q["norm_g"], q["w_in"], deps)
    ya, yc = _ac_fwd_call(proj, q)
    prep = p["s5_prep"]
    yb, cin_r, cin_i = _s5_fwd_call(proj, prep, q)
    dq, dk, dv, dgb = _dn_pre_fwd_call(proj, q)
    yd, ssave = _dn_core_fwd_call(proj, dq, dk, dv, dgb, q["d_norm_g"])
    x_next = _outproj_call(x, (ya, yb, yc, yd), q["w_out"])
    saved = dict(x=x, proj=proj, h=h, ya=ya, yb=yb, yc=yc, yd=yd, cin_r=cin_r, cin_i=cin_i,
                 q=dq, k=dk, v=dv, gb=dgb, ssave=ssave, prep=prep)
    return x_next, saved


def _layer_bwd(dx, p, sv, deps=(), on_weight_grads=None):
    q = _layer_params(p)
    proj = sv["proj"]
    dya, dyb, dyc, dyd, g_wout = _outproj_bwd_call(dx, (sv["ya"], sv["yb"], sv["yc"], sv["yd"]), q["w_out"], deps)
    dpa, dpc, g_acw, g_acb, g_alg, g_alb, g_apw, g_apb, g_ccw = _ac_bwd_call(proj, q, dya, dyc)
    dpb, *s5g = _s5_bwd_call(proj, sv["prep"], q, sv["cin_r"], sv["cin_i"], dyb)
    g_sd, g_gw, g_gb = s5g[6:]
    g_lre, g_lim, g_ldt, g_bre, g_bim, g_cre, g_cim = _s5_prep_bwd_call(_s5_inputs(p), s5g[:6])
    dq, dk, dv, dgb, dz, g_ng = _dn_core_bwd_call(proj, sv["q"], sv["k"], sv["v"], sv["gb"], q["d_norm_g"], sv["ssave"], dyd)
    dqkv, dab, g_dcw, g_alog, g_dtb = _dn_pre_bwd_call(proj, q, dq, dk, dv, dgb)
    slabs = (dpa, dpb, dpc, dqkv, dz, dab)
    g_win = _inproj_bwd_w_call(sv["h"], slabs)
    grads = {"w_in": g_win, "a_conv_w": g_acw, "a_conv_b": g_acb, "a_ln_g": g_alg, "a_ln_b": g_alb,
             "a_pw_w": g_apw, "a_pw_b": g_apb, "s5_lambda_re": g_lre, "s5_lambda_im": g_lim, "s5_b_re": g_bre,
             "s5_b_im": g_bim, "s5_c_re": g_cre, "s5_c_im": g_cim, "s5_d": g_sd, "s5_log_dt": g_ldt, "s5_glu_w": g_gw,
             "s5_glu_b": g_gb, "c_conv_w": g_ccw, "d_conv_w": g_dcw, "d_a_log": g_alog[:, :NH], "d_dt_bias": g_dtb[:, :NH],
             "d_norm_g": g_ng, "w_out": g_wout}
    tokens = ()
    if on_weight_grads is not None:
        small = _pack_rows([grads[n] for n in _REP_LAYER + _CONV], 8)
        tokens = on_weight_grads({n: grads[n] for n in _BIG}, small)
    dx_prev, grads["norm_g"] = _inproj_bwd_x_call(slabs, q["w_in"], sv["x"], q["norm_g"], dx, tokens)
    return dx_prev, grads


def _step(x, target, weights, moments_m, moments_v):
    me = 4 * lax.axis_index("x") + 2 * lax.axis_index("y") + lax.axis_index("c")
    layer_names = [n for n in _WEIGHTS if n != "final_g"]
    s5_all = _s5_inputs_all(weights)

    sharded = _BIG + _CONV
    gather = _gather_start({n: weights[n][0] for n in sharded}, 0, me)
    preps = [_s5_prep_call({_S5_KEYS[n]: a[l] for n, a in s5_all.items()}, [gather[3]]) for l in range(DEPTH)]
    x_out, after, layers, saved = x, preps[-1][0], [], []
    for l in range(DEPTH):
        full = _gather_finish(weights, l, gather, after)
        deps = ()
        if l + 1 < DEPTH:
            nxt, full["w_out"] = lax.optimization_barrier(({n: weights[n][l + 1] for n in sharded}, full["w_out"]))
            gather = _gather_start(nxt, l + 1, me)
            deps = [gather[3]]
        p = {n: (full[n] if n in full else weights[n][l]) for n in layer_names}
        p["s5_in"] = {n: a[l] for n, a in s5_all.items()}
        p["s5_prep"] = preps[l]
        layers.append(p)
        x_out, sv = _layer_fwd(x_out, p, deps)
        after = x_out
        saved.append(sv)
    dx0, g_final, loss_part = _loss_call(x_out, _row(weights["final_g"]), target)

    per_layer, scatters = [None] * DEPTH, [None] * DEPTH
    for l in range(DEPTH - 1, -1, -1):
        def start(big, small, l=l):
            scatters[l] = _scatter_start(big, small, l, me)
            return [scatters[l][3]]
        dx0, per_layer[l] = _layer_bwd(dx0, layers[l], saved[l], [scatters[l + 1][3]] if l + 1 < DEPTH else (), start)
    loss = lax.psum(loss_part[0, 0], ("x", "y", "c"))
    results = {}

    parts = [_split_wait_call("scatter_wait_%d" % l, scatters[l][0], scatters[l][1], scatters[l][2], _SCATTER_VIEWS, dx0)
             for l in range(DEPTH - 1, -1, -1)][::-1]
    rows = {"w_in": RELAYOUT_ROWS, "w_out": D // N_DEV, "a_pw_w": BR // N_DEV, "s5_glu_w": BR // N_DEV}
    for i, n in enumerate(_BIG):
        results[n] = _reduce_adamw_layers_call([parts[l][i] for l in range(DEPTH)], weights[n], moments_m[n],
                                               moments_v[n], rows[n], "adamw_" + n)

    conv_shapes = [(DEPTH, _CONV_ROWS[n], _CONV_WIDTH[n]) for n in _CONV]
    pack = lambda d: _pack_layers([d[n] for n in _REP_LAYER] + [jnp.zeros(s, F32) for s in conv_shapes])
    wpack = pack(weights)
    res = _reduce_adamw_layers_call([parts[l][len(_BIG)] for l in range(DEPTH)], wpack, pack(moments_m), pack(moments_v),
                                    wpack.shape[1], "adamw_replicated")
    shapes = [weights[n].shape for n in _REP_LAYER] + conv_shapes
    res = [_unpack_layers(r, shapes) for r in res]
    for i, n in enumerate(_REP_LAYER):
        results[n] = tuple(r[i] for r in res)

    tail_g = [jnp.stack([g["norm_g"] for g in per_layer]).reshape(DEPTH, D), g_final.reshape(D)]
    packt = lambda arrs: _pack_rows(arrs, 8)
    gathered, = _exchange_call("gather_tail_grads", [(packt(tail_g), _whole, (N_DEV,) + packt(tail_g).shape, _slot)])
    tail = _reduce_adamw_call(gathered, packt([weights[n] for n in _REP_TAIL]), packt([moments_m[n] for n in _REP_TAIL]),
                              packt([moments_v[n] for n in _REP_TAIL]), packt(tail_g).shape, "adamw_tail")
    tail = [_unpack(r, [weights[n].shape for n in _REP_TAIL]) for r in tail]
    for i, n in enumerate(_REP_TAIL):
        results[n] = tuple(r[i] for r in tail)

    own_g = []
    for i, n in enumerate(_CONV):
        width = _CONV_WIDTH[n] // N_DEV
        summed = res[0][len(_REP_LAYER) + i][:, :_CONV_TAPS[n], :]
        own_g.append(lax.dynamic_slice_in_dim(summed, me * width, width, axis=2))
    packc = lambda arrs: _pack_rows(arrs, 8)
    res = _reduce_adamw_call(packc(own_g)[None], packc([weights[n] for n in _CONV]), packc([moments_m[n] for n in _CONV]),
                             packc([moments_v[n] for n in _CONV]), packc(own_g).shape, "adamw_conv")
    res = [_unpack(r, [weights[n].shape for n in _CONV]) for r in res]
    for i, n in enumerate(_CONV):
        results[n] = tuple(r[i] for r in res)

    outs = [loss, dx0]
    for kind in range(4):
        outs += [results[n][kind] for n in _WEIGHTS]
    return tuple(outs)


def kernel(x, norm_g, w_in, a_conv_w, a_conv_b, a_ln_g, a_ln_b, a_pw_w, a_pw_b, s5_lambda_re, s5_lambda_im, s5_b_re, s5_b_im, s5_c_re, s5_c_im, s5_d, s5_log_dt, s5_glu_w, s5_glu_b, c_conv_w, d_conv_w, d_a_log, d_dt_bias, d_norm_g, w_out, final_g, loss_target, m_norm_g, m_w_in, m_a_conv_w, m_a_conv_b, m_a_ln_g, m_a_ln_b, m_a_pw_w, m_a_pw_b, m_s5_lambda_re, m_s5_lambda_im, m_s5_b_re, m_s5_b_im, m_s5_c_re, m_s5_c_im, m_s5_d, m_s5_log_dt, m_s5_glu_w, m_s5_glu_b, m_c_conv_w, m_d_conv_w, m_d_a_log, m_d_dt_bias, m_d_norm_g, m_w_out, m_final_g, v_norm_g, v_w_in, v_a_conv_w, v_a_conv_b, v_a_ln_g, v_a_ln_b, v_a_pw_w, v_a_pw_b, v_s5_lambda_re, v_s5_lambda_im, v_s5_b_re, v_s5_b_im, v_s5_c_re, v_s5_c_im, v_s5_d, v_s5_log_dt, v_s5_glu_w, v_s5_glu_b, v_c_conv_w, v_d_conv_w, v_d_a_log, v_d_dt_bias, v_d_norm_g, v_w_out, v_final_g):
    weights = dict(norm_g=norm_g, w_in=w_in, a_conv_w=a_conv_w, a_conv_b=a_conv_b, a_ln_g=a_ln_g, a_ln_b=a_ln_b, a_pw_w=a_pw_w, a_pw_b=a_pw_b, s5_lambda_re=s5_lambda_re, s5_lambda_im=s5_lambda_im, s5_b_re=s5_b_re, s5_b_im=s5_b_im, s5_c_re=s5_c_re, s5_c_im=s5_c_im, s5_d=s5_d, s5_log_dt=s5_log_dt, s5_glu_w=s5_glu_w, s5_glu_b=s5_glu_b, c_conv_w=c_conv_w, d_conv_w=d_conv_w, d_a_log=d_a_log, d_dt_bias=d_dt_bias, d_norm_g=d_norm_g, w_out=w_out, final_g=final_g)
    mom_m = dict(norm_g=m_norm_g, w_in=m_w_in, a_conv_w=m_a_conv_w, a_conv_b=m_a_conv_b, a_ln_g=m_a_ln_g, a_ln_b=m_a_ln_b, a_pw_w=m_a_pw_w, a_pw_b=m_a_pw_b, s5_lambda_re=m_s5_lambda_re, s5_lambda_im=m_s5_lambda_im, s5_b_re=m_s5_b_re, s5_b_im=m_s5_b_im, s5_c_re=m_s5_c_re, s5_c_im=m_s5_c_im, s5_d=m_s5_d, s5_log_dt=m_s5_log_dt, s5_glu_w=m_s5_glu_w, s5_glu_b=m_s5_glu_b, c_conv_w=m_c_conv_w, d_conv_w=m_d_conv_w, d_a_log=m_d_a_log, d_dt_bias=m_d_dt_bias, d_norm_g=m_d_norm_g, w_out=m_w_out, final_g=m_final_g)
    mom_v = dict(norm_g=v_norm_g, w_in=v_w_in, a_conv_w=v_a_conv_w, a_conv_b=v_a_conv_b, a_ln_g=v_a_ln_g, a_ln_b=v_a_ln_b, a_pw_w=v_a_pw_w, a_pw_b=v_a_pw_b, s5_lambda_re=v_s5_lambda_re, s5_lambda_im=v_s5_lambda_im, s5_b_re=v_s5_b_re, s5_b_im=v_s5_b_im, s5_c_re=v_s5_c_re, s5_c_im=v_s5_c_im, s5_d=v_s5_d, s5_log_dt=v_s5_log_dt, s5_glu_w=v_s5_glu_w, s5_glu_b=v_s5_glu_b, c_conv_w=v_c_conv_w, d_conv_w=v_d_conv_w, d_a_log=v_d_a_log, d_dt_bias=v_d_dt_bias, d_norm_g=v_d_norm_g, w_out=v_w_out, final_g=v_final_g)
    outs = _step(x[0], loss_target[0], weights, mom_m, mom_v)
    return (outs[0], outs[1][None]) + outs[2:]
```

```python
import functools

import jax
import jax.numpy as jnp
from jax import lax
from jax.experimental import pallas as pl
from jax.experimental.pallas import tpu as pltpu

F32 = jnp.float32
BF16 = jnp.bfloat16
HI = lax.Precision.HIGHEST
SDS = jax.ShapeDtypeStruct

N_DEV = 8
D = 1024
BR = 256
DEPTH = 4
IN_COLS = 3336
PW = 3456
AB_COL = 3328
EPS = 1e-6
TL = 512
SEG = TL // 8
HALO = 32
HALO_S = 8
KA, KC, KD = 31, 3, 4
CH = 64
DN_GROUP = 4
DN_FWD_GROUPS = 2
NH, HD = 4, 64
NSTATE = 1024
S5_PARTS = 2
VMEM_LIMIT = 56 * 1024 * 1024

ADAM_LR, ADAM_B1, ADAM_B2, ADAM_EPS, ADAM_WD, ADAM_STEP = 0.001, 0.9, 0.999, 1e-08, 0.01, 10

NN = ((1,), (0,))
NT = ((1,), (1,))
TN = ((0,), (0,))


def _dot(a, b, dims, prec=None):
    return lax.dot_general(a, b, (dims, ((), ())), precision=prec, preferred_element_type=F32)


def _make_mm(cast, prec, fwd_dims):
    def prep(t):
        return t.astype(cast) if cast is not None else t

    @jax.custom_vjp
    def mm(a, w):
        return _dot(prep(a), prep(w), fwd_dims, prec)

    def fwd(a, w):
        return mm(a, w), (a, w)

    def bwd(res, dy):
        a, w = res
        a, w, dy = prep(a), prep(w), prep(dy)
        if fwd_dims == NN:
            return _dot(dy, w, NT, prec), _dot(a, dy, TN, prec)
        if fwd_dims == NT:
            return _dot(dy, w, NN, prec), _dot(dy, a, TN, prec)
        return _dot(w, dy, NT, prec), _dot(a, dy, NN, prec)

    mm.defvjp(fwd, bwd)
    return mm


mm = _make_mm(BF16, None, NN)
mm_nt = _make_mm(BF16, None, NT)
mm_tn = _make_mm(BF16, None, TN)
mmh = _make_mm(None, HI, NN)
mmh_nt = _make_mm(None, HI, NT)


def _sigmoid(x):
    return jax.nn.sigmoid(x)


def _silu(x):
    return x * jax.nn.sigmoid(x)


def _gelu(x):
    return 0.5 * x * (1.0 + jnp.tanh(0.7978845608028654 * (x + 0.044715 * (x * x * x))))


def _softplus(x):
    return jnp.maximum(x, 0.0) + jnp.log1p(jnp.exp(-jnp.abs(x)))


def _rms(x, g):
    return x * lax.rsqrt(jnp.mean(x * x, axis=-1, keepdims=True) + EPS) * g


def _cparams(sem):
    return pltpu.CompilerParams(dimension_semantics=sem, vmem_limit_bytes=VMEM_LIMIT)


def _tap_offsets(halo, taps):
    return [halo - (taps - 1) + k for k in range(taps)]


def _conv_fwd_impl(acat, w, tile, halo, taps):
    n = tile + halo
    out = None
    for k, off in enumerate(_tap_offsets(halo, taps)):
        src = jnp.roll(acat, n - off, axis=0)[:tile, :] if off != halo else acat[halo:, :]
        term = src * w[k:k + 1, :]
        out = term if out is None else out + term
    return out


def _make_conv(tile, halo, taps):
    @jax.custom_vjp
    def conv(acat, w):
        return _conv_fwd_impl(acat, w, tile, halo, taps)

    def fwd(acat, w):
        return conv(acat, w), (acat, w)

    def bwd(res, dy):
        acat, w = res
        n = tile + halo
        dyp = jnp.concatenate([dy, jnp.zeros((halo, dy.shape[1]), F32)], axis=0)
        rows = lax.broadcasted_iota(jnp.int32, w.shape, 0)
        dacat = None
        dw = jnp.zeros(w.shape, F32)
        for k, off in enumerate(_tap_offsets(halo, taps)):
            term = jnp.roll(dyp, off, axis=0) * w[k:k + 1, :]
            dacat = term if dacat is None else dacat + term
            src = jnp.roll(acat, n - off, axis=0)[:tile, :] if off != halo else acat[halo:, :]
            dw = dw + jnp.where(rows == k, jnp.sum(dy * src, axis=0, keepdims=True), 0.0)
        return dacat, dw

    conv.defvjp(fwd, bwd)
    return conv


def _halo_spec(tile, halo, width, col):
    per = tile // halo
    return pl.BlockSpec((halo, width), lambda i: (jnp.maximum(i * per - 1, 0), col))


def _halo_spec_rev(nt, tile, halo, width, col):
    per = tile // halo
    return pl.BlockSpec((halo, width), lambda i: (jnp.maximum((nt - 1 - i) * per - 1, 0), col))


def _dep_specs(deps):
    return [pl.BlockSpec((8, 128), lambda *_: (0, 0)) for _ in deps]


def _inproj_call(x, g, w, deps=()):
    L = x.shape[0]

    def body(x_ref, g_ref, w_ref, *rest):
        p_ref, h_ref = rest[len(deps):]
        h = _rms(x_ref[...], g_ref[...]).astype(BF16)
        h_ref[...] = h
        p_ref[...] = _dot(h, w_ref[...], NN)

    return pl.pallas_call(
        body, name="inproj", grid=(L // TL,),
        in_specs=[pl.BlockSpec((TL, D), lambda i: (i, 0)), pl.BlockSpec((1, D), lambda i: (0, 0)),
                  pl.BlockSpec((D, PW), lambda i: (0, 0))] + _dep_specs(deps),
        out_specs=[pl.BlockSpec((TL, PW), lambda i: (i, 0)), pl.BlockSpec((TL, D), lambda i: (i, 0))],
        out_shape=[SDS((L, PW), F32), SDS((L, D), BF16)],
        compiler_params=_cparams(("parallel",)),
    )(x, g, w, *deps)


def _outproj_call(x, ys, w):
    L = x.shape[0]

    def body(x_ref, a_ref, b_ref, c_ref, d_ref, w_ref, o_ref):
        mixed = jnp.concatenate([y_ref[...].astype(BF16) for y_ref in (a_ref, b_ref, c_ref, d_ref)], axis=1)
        o_ref[...] = x_ref[...] + _dot(mixed, w_ref[...], NN)

    yspec = pl.BlockSpec((TL, BR), lambda i: (i, 0))
    return pl.pallas_call(
        body, name="outproj", grid=(L // TL,),
        in_specs=[pl.BlockSpec((TL, D), lambda i: (i, 0)), yspec, yspec, yspec, yspec,
                  pl.BlockSpec((D, D), lambda i: (0, 0))],
        out_specs=pl.BlockSpec((TL, D), lambda i: (i, 0)),
        out_shape=SDS((L, D), F32),
        compiler_params=_cparams(("parallel",)),
    )(x, *ys, w)


def _loss_call(x, g, target):
    L = x.shape[0]

    def body(x_ref, g_ref, t_ref, dx_ref, dg_ref, loss_ref):
        @pl.when(pl.program_id(0) == 0)
        def _():
            dg_ref[...] = jnp.zeros_like(dg_ref)
            loss_ref[...] = jnp.zeros_like(loss_ref)

        y, vjp = jax.vjp(_rms, x_ref[...], g_ref[...])
        err = y - t_ref[...]
        dx, dg = vjp(err * (1.0 / D))
        dx_ref[...] = dx
        dg_ref[...] += dg
        tot = jnp.sum(jnp.sum(err * err, axis=1, keepdims=True), axis=0, keepdims=True)
        loss_ref[...] += jnp.broadcast_to(tot * (0.5 / D), loss_ref.shape)

    return pl.pallas_call(
        body, name="loss_head", grid=(L // TL,),
        in_specs=[pl.BlockSpec((TL, D), lambda i: (i, 0)), pl.BlockSpec((1, D), lambda i: (0, 0)),
                  pl.BlockSpec((TL, D), lambda i: (i, 0))],
        out_specs=[pl.BlockSpec((TL, D), lambda i: (i, 0)), pl.BlockSpec((1, D), lambda i: (0, 0)),
                   pl.BlockSpec((1, 128), lambda i: (0, 0))],
        out_shape=[SDS((L, D), F32), SDS((1, D), F32), SDS((1, 128), F32)],
        compiler_params=_cparams(("arbitrary",)),
    )(x, g, target)


def _branch_a(valw, gatew, z, cw, cb, lg, lb, pw, pb, conv):
    a = conv(valw * _sigmoid(gatew), cw) + cb
    mu = jnp.mean(a, axis=-1, keepdims=True)
    xc = a - mu
    y = xc * lax.rsqrt(jnp.mean(xc * xc, axis=-1, keepdims=True) + EPS) * lg + lb
    y = mm(_silu(y), pw) + pb
    return y * _silu(z)


def _branch_c(bg, cw_, xw, z, w3, conv):
    return bg * conv(cw_ * xw, w3) * _silu(z)


def _ac_fwd_call(proj, p):
    L = proj.shape[0]

    def body(val, gate, za, hval, hgate, cb_, cc, cx, cz, hcc, hcx,
             acw, acb, alg, alb, apw, apb, ccw, ya_ref, yc_ref):
        nf = (pl.program_id(0) > 0).astype(F32)
        win = lambda h, m: jnp.concatenate([h[...] * nf, m[...]], axis=0)
        conv_a = functools.partial(_conv_fwd_impl, tile=TL, halo=HALO, taps=KA)
        conv_c = functools.partial(_conv_fwd_impl, tile=TL, halo=HALO, taps=KC)
        ya_ref[...] = _branch_a(win(hval, val), win(hgate, gate), za[...], acw[...], acb[...], alg[...], alb[...],
                                apw[...], apb[...], conv_a)
        yc_ref[...] = _branch_c(cb_[...], win(hcc, cc), win(hcx, cx), cz[...], ccw[...], conv_c)

    col = lambda j: pl.BlockSpec((TL, BR), lambda i: (i, j))
    hal = lambda j: _halo_spec(TL, HALO, BR, j)
    full = lambda a: pl.BlockSpec(a.shape, lambda i: (0,) * a.ndim)
    params = (p["a_conv_w"], p["a_conv_b"], p["a_ln_g"], p["a_ln_b"], p["a_pw_w"], p["a_pw_b"], p["c_conv_w"])
    return pl.pallas_call(
        body, name="ac_fwd", grid=(L // TL,),
        in_specs=[col(0), col(1), col(2), hal(0), hal(1), col(5), col(6), col(7), col(8), hal(6), hal(7)]
        + [full(a) for a in params],
        out_specs=[pl.BlockSpec((TL, BR), lambda i: (i, 0))] * 2,
        out_shape=[SDS((L, BR), F32)] * 2,
        compiler_params=_cparams(("parallel",)),
    )(*([proj] * 11), *params)


def _ac_bwd_call(proj, p, dya, dyc):
    L = proj.shape[0]
    nt = L // TL

    def body(val, gate, za, hval, hgate, cb_, cc, cx, cz, hcc, hcx,
             acw, acb, alg, alb, apw, apb, ccw, dya_ref, dyc_ref,
             da_ref, dc_ref, g_acw, g_acb, g_alg, g_alb, g_apw, g_apb, g_ccw, carry):
        i = pl.program_id(0)
        gouts = (g_acw, g_acb, g_alg, g_alb, g_apw, g_apb, g_ccw)

        @pl.when(i == 0)
        def _():
            carry[...] = jnp.zeros_like(carry)
            for r in gouts:
                r[...] = jnp.zeros_like(r)

        nf = (i < nt - 1).astype(F32)
        win = lambda h, m: jnp.concatenate([h[...] * nf, m[...]], axis=0)
        conv_a = _make_conv(TL, HALO, KA)
        conv_c = _make_conv(TL, HALO, KC)

        def f(valw, gatew, z, bg, ccw_, cxw, czv, w1, b1, lg, lb, pw, pb, w3):
            return (_branch_a(valw, gatew, z, w1, b1, lg, lb, pw, pb, conv_a),
                    _branch_c(bg, ccw_, cxw, czv, w3, conv_c))

        _, vjp = jax.vjp(f, win(hval, val), win(hgate, gate), za[...], cb_[...], win(hcc, cc), win(hcx, cx), cz[...],
                         acw[...], acb[...], alg[...], alb[...], apw[...].astype(F32), apb[...], ccw[...])
        (dvalw, dgatew, dz, dbg, dccw, dcxw, dczv, d1, d2, d3, d4, d5, d6, d7) = vjp((dya_ref[...], dyc_ref[...]))

        def settle(slot, dwin):
            tail = jnp.concatenate([jnp.zeros((TL - HALO, BR), F32), carry[slot]], axis=0)
            carry[slot] = dwin[:HALO, :]
            return (dwin[HALO:, :] + tail).astype(BF16)

        da_ref[:, 0:BR] = settle(0, dvalw)
        da_ref[:, BR:2 * BR] = settle(1, dgatew)
        da_ref[:, 2 * BR:3 * BR] = dz.astype(BF16)
        dc_ref[:, 0:BR] = dbg.astype(BF16)
        dc_ref[:, BR:2 * BR] = settle(2, dccw)
        dc_ref[:, 2 * BR:3 * BR] = settle(3, dcxw)
        dc_ref[:, 3 * BR:4 * BR] = dczv.astype(BF16)
        for r, g in zip(gouts, (d1, d2, d3, d4, d5, d6, d7)):
            r[...] += g

    col = lambda j: pl.BlockSpec((TL, BR), lambda i: (nt - 1 - i, j))
    hal = lambda j: _halo_spec_rev(nt, TL, HALO, BR, j)
    full = lambda a: pl.BlockSpec(a.shape, lambda i: (0,) * a.ndim)
    params = (p["a_conv_w"], p["a_conv_b"], p["a_ln_g"], p["a_ln_b"], p["a_pw_w"], p["a_pw_b"], p["c_conv_w"])
    rev = lambda w: pl.BlockSpec((TL, w), lambda i: (nt - 1 - i, 0))
    return pl.pallas_call(
        body, name="ac_bwd", grid=(nt,),
        in_specs=[col(0), col(1), col(2), hal(0), hal(1), col(5), col(6), col(7), col(8), hal(6), hal(7)]
        + [full(a) for a in params] + [rev(BR), rev(BR)],
        out_specs=[rev(3 * BR), rev(4 * BR)] + [full(a) for a in params],
        out_shape=[SDS((L, 3 * BR), BF16), SDS((L, 4 * BR), BF16)] + [SDS(a.shape, F32) for a in params],
        scratch_shapes=[pltpu.VMEM((4, HALO, BR), F32)],
        compiler_params=_cparams(("arbitrary",)),
    )(*([proj] * 11), *params, dya, dyc)


def _iota2(shape, dim):
    return lax.broadcasted_iota(jnp.int32, shape, dim)


def _s5_params(lam_re, lam_im, logdt, b_re, b_im, c_re, c_im):
    eg = (_iota2((16, NSTATE), 1) >> 6 == _iota2((16, NSTATE), 0)).astype(F32)
    dt = jnp.exp(mmh(jnp.broadcast_to(logdt, (8, 16)), eg)[0:1, :])
    lr = jnp.minimum(lam_re, -1e-4)
    li = lam_im
    mag = jnp.exp(lr * dt)
    lbr = mag * jnp.cos(li * dt)
    lbi = mag * jnp.sin(li * dt)
    den = lr * lr + li * li
    nr = lbr - 1.0
    fr = (nr * lr + lbi * li) / den
    fi = (lbi * lr - nr * li) / den
    row = _iota2((8, NSTATE), 0)
    f8 = jnp.where(row == 0, fr, jnp.where(row == 1, fi, 0.0))
    eye = (_iota2((NSTATE, NSTATE), 0) == _iota2((NSTATE, NSTATE), 1)).astype(F32)
    fcol = mmh_nt(eye, f8)
    frc, fic = fcol[:, 0:1], fcol[:, 1:2]
    bbr = frc * b_re - fic * b_im
    bbi = frc * b_im + fic * b_re
    e1 = ((_iota2((16, BR), 1) & 15) == _iota2((16, BR), 0)).astype(F32)
    m1 = ((_iota2((NSTATE, BR), 0) >> 6) == (_iota2((NSTATE, BR), 1) >> 4)).astype(F32)
    wbr = mmh(bbr, e1) * m1
    wbi = mmh(bbi, e1) * m1
    e2 = ((_iota2((64, NSTATE), 1) & 63) == _iota2((64, NSTATE), 0)).astype(F32)
    m2 = ((_iota2((BR, NSTATE), 0) >> 4) == (_iota2((BR, NSTATE), 1) >> 6)).astype(F32)
    wcr = mmh(c_re, e2) * m2
    wci = mmh(c_im, e2) * m2
    return lbr, lbi, wbr, wbi, wcr, wci


_S5_OUT = [(1, NSTATE), (1, NSTATE), (NSTATE, BR), (NSTATE, BR), (BR, NSTATE), (BR, NSTATE)]


def _s5_prep_call(sp, deps=()):
    def body(lre, lim, ldt, bre, bim, cre, cim, *rest):
        o_lbr, o_lbi, o_wbr, o_wbi, o_wcr, o_wci, pwr, pwi, qwr, qwi = rest[len(deps):]
        lbr, lbi, wbr, wbi, wcr, wci = _s5_params(lre[...], lim[...], ldt[...], bre[...], bim[...], cre[...], cim[...])
        o_lbr[...], o_lbi[...], o_wbr[...], o_wbi[...], o_wcr[...], o_wci[...] = lbr, lbi, wbr, wbi, wcr, wci
        pr, pi = lbr, lbi
        for i in range(SEG):
            pwr[i:i + 1, :] = pr
            pwi[i:i + 1, :] = pi
            qwr[SEG - 1 - i:SEG - i, :] = pr
            qwi[SEG - 1 - i:SEG - i, :] = -pi
            pr, pi = pr * lbr - pi * lbi, pr * lbi + pi * lbr

    args = (sp["lam_re"], sp["lam_im"], sp["log_dt"], sp["b_re"], sp["b_im"], sp["c_re"], sp["c_im"])
    return pl.pallas_call(
        body, name="s5_prep",
        out_shape=[SDS(s, F32) for s in _S5_OUT] + [SDS((SEG, NSTATE), F32)] * 4,
        compiler_params=pltpu.CompilerParams(vmem_limit_bytes=VMEM_LIMIT),
    )(*args, *deps)


def _s5_prep_bwd_call(sp, cots):
    def body(lre, lim, ldt, bre, bim, cre, cim, c0, c1, c2, c3, c4, c5, *outs):
        _, vjp = jax.vjp(_s5_params, lre[...], lim[...], ldt[...], bre[...], bim[...], cre[...], cim[...])
        grads = vjp((c0[...], c1[...], c2[...], c3[...], c4[...], c5[...]))
        for o, g in zip(outs, grads):
            o[...] = g

    args = (sp["lam_re"], sp["lam_im"], sp["log_dt"], sp["b_re"], sp["b_im"], sp["c_re"], sp["c_im"])
    return pl.pallas_call(
        body, name="s5_prep_bwd",
        out_shape=[SDS(a.shape, F32) for a in args],
        compiler_params=pltpu.CompilerParams(vmem_limit_bytes=VMEM_LIMIT),
    )(*args, *cots)


def _lanes(v, j):
    return v[:, j * 128:(j + 1) * 128]


def _s5_scan(sre, sim, pwr, pwi, cin_r, cin_i, reverse):
    row = _iota2((8, 128), 0)
    steps = (1, 2, 4)

    def lane_consts(j):
        lanes = slice(j * 128, (j + 1) * 128)
        if reverse:
            mult = [(jnp.where(row < 8 - d, pwr[SEG - d:SEG - d + 1, lanes], 0.0),
                     jnp.where(row < 8 - d, pwi[SEG - d:SEG - d + 1, lanes], 0.0)) for d in steps]
            return mult, pwr[SEG - 8:SEG, lanes], pwi[SEG - 8:SEG, lanes]
        mult = [(jnp.where(row >= d, pwr[d - 1:d, lanes], 0.0),
                 jnp.where(row >= d, pwi[d - 1:d, lanes], 0.0)) for d in steps]
        return mult, pwr[0:8, lanes], pwi[0:8, lanes]

    consts = [lane_consts(j) for j in range(8)]
    nblk = TL // 8

    def block(t, carry):
        b = nblk - 1 - t if reverse else t
        rows = pl.ds(pl.multiple_of(b * 8, 8), 8)
        new = []
        for j in range(8):
            mult, p8r, p8i = consts[j]
            vr, vi = sre.at[j], sim.at[j]
            sr, si = vr[rows, :], vi[rows, :]
            for d, (mr, mi) in zip(steps, mult):
                shift = 8 - d if reverse else d
                hr, hi = pltpu.roll(sr, shift, 0), pltpu.roll(si, shift, 0)
                sr, si = sr + mr * hr - mi * hi, si + mr * hi + mi * hr
            cr, ci = carry[2 * j], carry[2 * j + 1]
            sr, si = sr + p8r * cr - p8i * ci, si + p8r * ci + p8i * cr
            vr[rows, :] = sr
            vi[rows, :] = si
            edge = slice(0, 1) if reverse else slice(7, 8)
            new += [sr[edge, :], si[edge, :]]
        return tuple(new)

    init = []
    for j in range(8):
        init += [_lanes(cin_r, j), _lanes(cin_i, j)]
    ends = lax.fori_loop(0, nblk, block, tuple(init))
    return (jnp.concatenate([ends[2 * j] for j in range(8)], axis=1),
            jnp.concatenate([ends[2 * j + 1] for j in range(8)], axis=1))


def _bdot(a, b, dims):
    return _dot(a.astype(BF16), b.astype(BF16), dims)


def _s5_states(u, wbr, wbi, sre, sim):
    bur = _bdot(u, wbr, NT)
    bui = _bdot(u, wbi, NT)
    for j in range(8):
        sre[j] = _lanes(bur, j)
        sim[j] = _lanes(bui, j)


def _gather_lanes(s):
    return jnp.concatenate([s[j] for j in range(8)], axis=1)


def _s5_tail(y, u, z, dsk, gw, gb):
    yg = _gelu(y + dsk * u)
    return yg * _sigmoid(mm(yg, gw) + gb) * _silu(z)


def _s5_post(s_re, s_im, u, z, wcr, wci, dsk, gw, gb):
    return _s5_tail(mm_nt(s_re, wcr) - mm_nt(s_im, wci), u, z, dsk, gw, gb)


def _s5_fwd_call(proj, prep, p):
    L = proj.shape[0]
    nt = L // TL
    lbr, lbi, wbr, wbi, wcr, wci, pwr, pwi, _, _ = prep

    def body(u_ref, z_ref, lbr_r, lbi_r, wbr_r, wbi_r, wcr_r, wci_r, pwr_r, pwi_r, d_r, gw_r, gb_r,
             yb_ref, cinr_ref, cini_ref, sre, sim, car, cai):
        @pl.when(pl.program_id(0) == 0)
        def _():
            car[...] = jnp.zeros_like(car)
            cai[...] = jnp.zeros_like(cai)

        u = u_ref[...]
        cinr_ref[0] = car[...]
        cini_ref[0] = cai[...]
        _s5_states(u, wbr_r[...], wbi_r[...], sre, sim)
        nr, ni = _s5_scan(sre, sim, pwr_r, pwi_r, car[...], cai[...], False)
        car[...] = nr
        cai[...] = ni
        yb_ref[...] = _s5_post(_gather_lanes(sre), _gather_lanes(sim), u, z_ref[...], wcr_r[...], wci_r[...],
                               d_r[...], gw_r[...], gb_r[...])

    full = lambda a: pl.BlockSpec(a.shape, lambda i: (0,) * a.ndim)
    consts = (lbr, lbi, wbr, wbi, wcr, wci, pwr, pwi, p["s5_d"], p["s5_glu_w"], p["s5_glu_b"])
    cspec = pl.BlockSpec((1, 1, NSTATE), lambda i: (i, 0, 0))
    return pl.pallas_call(
        body, name="s5_fwd", grid=(nt,),
        in_specs=[pl.BlockSpec((TL, BR), lambda i: (i, 3)), pl.BlockSpec((TL, BR), lambda i: (i, 4))]
        + [full(a) for a in consts],
        out_specs=[pl.BlockSpec((TL, BR), lambda i: (i, 0)), cspec, cspec],
        out_shape=[SDS((L, BR), F32), SDS((nt, 1, NSTATE), F32), SDS((nt, 1, NSTATE), F32)],
        scratch_shapes=[pltpu.VMEM((8, TL, 128), F32), pltpu.VMEM((8, TL, 128), F32),
                        pltpu.VMEM((1, NSTATE), F32), pltpu.VMEM((1, NSTATE), F32)],
        compiler_params=_cparams(("arbitrary",)),
    )(proj, proj, *consts)


def _s5_bwd_call(proj, prep, p, cin_r, cin_i, dyb):
    L = proj.shape[0]
    nt = L // TL
    lbr, lbi, wbr, wbi, wcr, wci, pwr, pwi, qwr, qwi = prep

    def body(u_ref, z_ref, lbr_r, lbi_r, wbr_r, wbi_r, wcr_r, wci_r, pwr_r, pwi_r, qwr_r, qwi_r, d_r, gw_r, gb_r,
             cinr_ref, cini_ref, dy_ref,
             db_ref, g_lbr, g_lbi, g_wbr, g_wbi, g_wcr, g_wci, g_d, g_gw, g_gb, sre, sim, gre, gim, car, cai):
        gouts = (g_lbr, g_lbi, g_wbr, g_wbi, g_wcr, g_wci, g_d, g_gw, g_gb)

        @pl.when(pl.program_id(0) == 0)
        def _():
            car[...] = jnp.zeros_like(car)
            cai[...] = jnp.zeros_like(cai)
            for r in gouts:
                r[...] = jnp.zeros_like(r)

        u = u_ref[...]
        c0r, c0i = cinr_ref[0], cini_ref[0]
        width = NSTATE // S5_PARTS
        tiles = width // 128
        part = lambda scr, h: jnp.concatenate([scr[h * tiles + j] for j in range(tiles)], axis=1)
        _s5_states(u, wbr_r[...], wbi_r[...], sre, sim)
        _s5_scan(sre, sim, pwr_r, pwi_r, c0r, c0i, False)
        y = None
        for h in range(S5_PARTS):
            lanes = slice(h * width, (h + 1) * width)
            yh = _bdot(part(sre, h), wcr_r[:, lanes], NT) - _bdot(part(sim, h), wci_r[:, lanes], NT)
            y = yh if y is None else y + yh
        _, vjp = jax.vjp(_s5_tail, y, u, z_ref[...], d_r[...], gw_r[...].astype(F32), gb_r[...])
        dy, du, dz, dd, dgw, dgb = vjp(dy_ref[...])
        for h in range(S5_PARTS):
            lanes = slice(h * width, (h + 1) * width)
            ds_re = _bdot(dy, wcr_r[:, lanes], NN)
            ds_im = -_bdot(dy, wci_r[:, lanes], NN)
            for j in range(tiles):
                gre[h * tiles + j] = _lanes(ds_re, j)
                gim[h * tiles + j] = _lanes(ds_im, j)
        nr, ni = _s5_scan(gre, gim, qwr_r, qwi_r, car[...], cai[...], True)
        car[...] = nr
        cai[...] = ni
        first = _iota2((TL, width), 0) == 0
        for h in range(S5_PARTS):
            lanes = slice(h * width, (h + 1) * width)
            s_re, s_im, a_re, a_im = part(sre, h), part(sim, h), part(gre, h), part(gim, h)
            g_wcr[:, lanes] += _bdot(dy, s_re, TN)
            g_wci[:, lanes] += -_bdot(dy, s_im, TN)
            p_re = jnp.where(first, c0r[:, lanes], jnp.roll(s_re, 1, axis=0))
            p_im = jnp.where(first, c0i[:, lanes], jnp.roll(s_im, 1, axis=0))
            g_lbr[:, lanes] += jnp.sum(a_re * p_re + a_im * p_im, axis=0, keepdims=True)
            g_lbi[:, lanes] += jnp.sum(a_im * p_re - a_re * p_im, axis=0, keepdims=True)
            du = du + _bdot(a_re, wbr_r[lanes, :], NN) + _bdot(a_im, wbi_r[lanes, :], NN)
            g_wbr[lanes, :] += _bdot(a_re, u, TN)
            g_wbi[lanes, :] += _bdot(a_im, u, TN)
        g_d[...] += dd
        g_gw[...] += dgw
        g_gb[...] += dgb
        db_ref[:, 0:BR] = du.astype(BF16)
        db_ref[:, BR:2 * BR] = dz.astype(BF16)

    full = lambda a: pl.BlockSpec(a.shape, lambda i: (0,) * a.ndim)
    consts = (lbr, lbi, wbr, wbi, wcr, wci, pwr, pwi, qwr, qwi, p["s5_d"], p["s5_glu_w"], p["s5_glu_b"])
    cspec = pl.BlockSpec((1, 1, NSTATE), lambda i: (nt - 1 - i, 0, 0))
    gshapes = _S5_OUT + [(1, BR), (BR, BR), (1, BR)]
    return pl.pallas_call(
        body, name="s5_bwd", grid=(nt,),
        in_specs=[pl.BlockSpec((TL, BR), lambda i: (nt - 1 - i, 3)), pl.BlockSpec((TL, BR), lambda i: (nt - 1 - i, 4))]
        + [full(a) for a in consts] + [cspec, cspec, pl.BlockSpec((TL, BR), lambda i: (nt - 1 - i, 0))],
        out_specs=[pl.BlockSpec((TL, 2 * BR), lambda i: (nt - 1 - i, 0))]
        + [pl.BlockSpec(s, lambda i: (0, 0)) for s in gshapes],
        out_shape=[SDS((L, 2 * BR), BF16)] + [SDS(s, F32) for s in gshapes],
        scratch_shapes=[pltpu.VMEM((8, TL, 128), F32)] * 4 + [pltpu.VMEM((1, NSTATE), F32)] * 2,
        compiler_params=_cparams(("arbitrary",)),
    )(proj, proj, *consts, cin_r, cin_i, dyb)


def _heads(x):
    return [x[:, h * HD:(h + 1) * HD] for h in range(NH)]


def _l2n(x, scale):
    return jnp.concatenate([xh * (lax.rsqrt(jnp.sum(xh * xh, axis=-1, keepdims=True) + EPS) * scale)
                            for xh in _heads(x)], axis=1)


def _dn_pre(qkvw, ab, cw, alog, dtb, conv, rows):
    c = _silu(conv(qkvw, cw))
    q = _l2n(c[:, 0:BR], HD ** -0.5)
    k = _l2n(c[:, BR:2 * BR], 1.0)
    v = c[:, 2 * BR:3 * BR]
    g = -jnp.exp(alog) * _softplus(ab + dtb)
    ri, ci = _iota2((rows, rows), 0), _iota2((rows, rows), 1)
    tri = ((ri >= ci) & ((ri >> 6) == (ci >> 6))).astype(F32)
    gc = mmh(tri, g)
    lane = _iota2(ab.shape, 1)
    return q, k, v, jnp.where(lane < NH, gc, jnp.where(lane < 2 * NH, _sigmoid(ab), 0.0))


def _dn_pre_fwd_call(proj, p):
    L = proj.shape[0]

    def body(m_ref, h_ref, ab_ref, cw, alog, dtb, q_ref, k_ref, v_ref, gb_ref):
        nf = (pl.program_id(0) > 0).astype(F32)
        qkvw = jnp.concatenate([h_ref[...] * nf, m_ref[...]], axis=0)
        conv = functools.partial(_conv_fwd_impl, tile=TL, halo=HALO_S, taps=KD)
        q_ref[...], k_ref[...], v_ref[...], gb_ref[...] = _dn_pre(qkvw, ab_ref[...], cw[...], alog[...], dtb[...], conv, TL)

    full = lambda a: pl.BlockSpec(a.shape, lambda i: (0,) * a.ndim)
    params = (p["d_conv_w"], p["d_a_log"], p["d_dt_bias"])
    o = pl.BlockSpec((TL, BR), lambda i: (i, 0))
    return pl.pallas_call(
        body, name="dn_pre_fwd", grid=(L // TL,),
        in_specs=[pl.BlockSpec((TL, 3 * BR), lambda i: (i, 3)), _halo_spec(TL, HALO_S, 3 * BR, 3),
                  pl.BlockSpec((TL, 128), lambda i: (i, AB_COL // 128))] + [full(a) for a in params],
        out_specs=[o, o, o, pl.BlockSpec((TL, 128), lambda i: (i, 0))],
        out_shape=[SDS((L, BR), F32)] * 3 + [SDS((L, 128), F32)],
        compiler_params=_cparams(("parallel",)),
    )(proj, proj, proj, *params)


def _dn_pre_bwd_call(proj, p, dq, dk, dv, dgb):
    L = proj.shape[0]
    nt = L // TL

    def body(m_ref, h_ref, ab_ref, cw, alog, dtb, dq_r, dk_r, dv_r, dgb_r,
             dqkv_ref, dab_ref, g_cw, g_alog, g_dtb, carry):
        i = pl.program_id(0)

        @pl.when(i == 0)
        def _():
            carry[...] = jnp.zeros_like(carry)
            for r in (g_cw, g_alog, g_dtb):
                r[...] = jnp.zeros_like(r)

        nf = (i < nt - 1).astype(F32)
        qkvw = jnp.concatenate([h_ref[...] * nf, m_ref[...]], axis=0)
        conv = _make_conv(TL, HALO_S, KD)
        _, vjp = jax.vjp(lambda a, b, c, d, e: _dn_pre(a, b, c, d, e, conv, TL),
                         qkvw, ab_ref[...], cw[...], alog[...], dtb[...])
        dwin, dab, dcw, dalog, ddtb = vjp((dq_r[...], dk_r[...], dv_r[...], dgb_r[...]))
        tail = jnp.concatenate([jnp.zeros((TL - HALO_S, 3 * BR), F32), carry[...]], axis=0)
        carry[...] = dwin[:HALO_S, :]
        dqkv_ref[...] = (dwin[HALO_S:, :] + tail).astype(BF16)
        dab_ref[...] = dab.astype(BF16)
        g_cw[...] += dcw
        g_alog[...] += dalog
        g_dtb[...] += ddtb

    full = lambda a: pl.BlockSpec(a.shape, lambda i: (0,) * a.ndim)
    params = (p["d_conv_w"], p["d_a_log"], p["d_dt_bias"])
    rev = lambda w: pl.BlockSpec((TL, w), lambda i: (nt - 1 - i, 0))
    return pl.pallas_call(
        body, name="dn_pre_bwd", grid=(nt,),
        in_specs=[pl.BlockSpec((TL, 3 * BR), lambda i: (nt - 1 - i, 3)), _halo_spec_rev(nt, TL, HALO_S, 3 * BR, 3),
                  pl.BlockSpec((TL, 128), lambda i: (nt - 1 - i, AB_COL // 128))] + [full(a) for a in params]
        + [rev(BR), rev(BR), rev(BR), rev(128)],
        out_specs=[rev(3 * BR), rev(128)] + [full(a) for a in params],
        out_shape=[SDS((L, 3 * BR), BF16), SDS((L, 128), BF16)] + [SDS(a.shape, F32) for a in params],
        scratch_shapes=[pltpu.VMEM((HALO_S, 3 * BR), F32)],
        compiler_params=_cparams(("arbitrary",)),
    )(proj, proj, proj, *params, dq, dk, dv, dgb)


def _dn_group(q, k, v, gb, z, ng, *s):
    return _dn_chunks(q, k, v, gb, z, ng, s, DN_GROUP, None)


def _dn_chunks(q, k, v, gb, z, ng, s, chunks, entering):
    ri, ci = _iota2((CH, CH), 0), _iota2((CH, CH), 1)
    causal, strict = ri >= ci, ri > ci
    eye = (ri == ci).astype(F32)
    s = list(s)
    pairs = []
    for c in range(chunks):
        rows = slice(c * CH, (c + 1) * CH)
        gbc = gb[rows, :]
        for h, (qh, kh, vh, zh) in enumerate(zip(_heads(q[rows, :]), _heads(k[rows, :]), _heads(v[rows, :]),
                                                 _heads(z[rows, :]))):
            gc = jnp.broadcast_to(gbc[:, h:h + 1], (CH, HD))
            beta = gbc[:, NH + h:NH + h + 1]
            decay = jnp.where(causal, jnp.exp(jnp.where(causal, gc - gc.T, 0.0)), 0.0)
            egc = jnp.exp(gc)
            glast = gc[CH - 1:CH, :]
            kb = kh * beta
            pairs.append(dict(q=qh, k=kh, z=zh, decay=decay, qe=qh * egc, kd=kh * jnp.exp(glast - gc),
                              sdec=jnp.exp(glast[:, 0:1]), kb=kb, rhs=jnp.concatenate([vh * beta, kb * egc], axis=1)))
    for p in pairs:
        p["pw"] = jnp.where(strict, mm_nt(p["kb"], p["k"]) * p["decay"], 0.0)
    for p in pairs:
        p["t"] = eye - p["pw"]
    for _ in range(5):
        for p in pairs:
            p["pw"] = mm(p["pw"], p["pw"])
        for p in pairs:
            p["t"] = mm(p["t"], eye + p["pw"])
    for p in pairs:
        p["uw"] = mm(p["t"], p["rhs"])
    for p in pairs:
        p["attn"] = mm_nt(p["q"], p["k"]) * p["decay"]
    out_rows = []
    for c in range(chunks):
        if entering is not None and c > 0 and c % DN_GROUP == 0:
            entering.append(list(s))
        grp = pairs[c * NH:(c + 1) * NH]
        ws = [mm(jnp.concatenate([p["uw"][:, HD:], p["qe"]], axis=0), s[h]) for h, p in enumerate(grp)]
        v_new = [p["uw"][:, :HD] - w_[:CH, :] for p, w_ in zip(grp, ws)]
        o = [w_[CH:, :] + mm(p["attn"], vn) for p, w_, vn in zip(grp, ws, v_new)]
        s = [s[h] * p["sdec"] + mm_tn(p["kd"], vn) for h, (p, vn) in enumerate(zip(grp, v_new))]
        o = [oh * lax.rsqrt(jnp.mean(oh * oh, axis=-1, keepdims=True) + EPS) * ng * _silu(p["z"]) for oh, p in zip(o, grp)]
        out_rows.append(jnp.concatenate(o, axis=1))
    return (jnp.concatenate(out_rows, axis=0), *s)


def _dn_core_fwd_call(proj, q, k, v, gb, ng):
    L = q.shape[0]
    rows = DN_FWD_GROUPS * DN_GROUP * CH
    steps = L // rows

    def body(q_r, k_r, v_r, gb_r, z_r, ng_r, yd_ref, ssave_ref, s_scr):
        @pl.when(pl.program_id(0) == 0)
        def _():
            s_scr[...] = jnp.zeros_like(s_scr)

        entering = [[s_scr[h] for h in range(NH)]]
        yd, *s2 = _dn_chunks(q_r[...], k_r[...], v_r[...], gb_r[...], z_r[...], ng_r[...], entering[0],
                             DN_FWD_GROUPS * DN_GROUP, entering)
        yd_ref[...] = yd
        for h in range(NH):
            s_scr[h] = s2[h]
            for g, states in enumerate(entering):
                ssave_ref[g, h] = states[h]

    c = pl.BlockSpec((rows, BR), lambda i: (i, 0))
    return pl.pallas_call(
        body, name="dn_core_fwd", grid=(steps,),
        in_specs=[c, c, c, pl.BlockSpec((rows, 128), lambda i: (i, 0)), pl.BlockSpec((rows, BR), lambda i: (i, 12)),
                  pl.BlockSpec((1, HD), lambda i: (0, 0))],
        out_specs=[c, pl.BlockSpec((DN_FWD_GROUPS, NH, HD, HD), lambda i: (i, 0, 0, 0))],
        out_shape=[SDS((L, BR), F32), SDS((steps * DN_FWD_GROUPS, NH, HD, HD), F32)],
        scratch_shapes=[pltpu.VMEM((NH, HD, HD), F32)],
        compiler_params=_cparams(("arbitrary",)),
    )(q, k, v, gb, proj, ng)


def _dn_core_bwd_call(proj, q, k, v, gb, ng, ssave, dyd):
    L = q.shape[0]
    rows = DN_GROUP * CH
    ng_ = L // rows

    def body(q_r, k_r, v_r, gb_r, z_r, ng_r, s_r, dy_r, dq_ref, dk_ref, dv_ref, dgb_ref, dz_ref, g_ng, ds_scr):
        @pl.when(pl.program_id(0) == 0)
        def _():
            ds_scr[...] = jnp.zeros_like(ds_scr)
            g_ng[...] = jnp.zeros_like(g_ng)

        _, vjp = jax.vjp(_dn_group, q_r[...], k_r[...], v_r[...], gb_r[...], z_r[...], ng_r[...],
                         *[s_r[0, h] for h in range(NH)])
        dq, dk, dv, dgb, dz, dng, *ds = vjp((dy_r[...], *[ds_scr[h] for h in range(NH)]))
        dq_ref[...], dk_ref[...], dv_ref[...], dgb_ref[...] = dq, dk, dv, dgb
        dz_ref[...] = dz.astype(BF16)
        g_ng[...] += dng
        for h in range(NH):
            ds_scr[h] = ds[h]

    c = pl.BlockSpec((rows, BR), lambda i: (ng_ - 1 - i, 0))
    c128 = pl.BlockSpec((rows, 128), lambda i: (ng_ - 1 - i, 0))
    return pl.pallas_call(
        body, name="dn_core_bwd", grid=(ng_,),
        in_specs=[c, c, c, c128, pl.BlockSpec((rows, BR), lambda i: (ng_ - 1 - i, 12)),
                  pl.BlockSpec((1, HD), lambda i: (0, 0)),
                  pl.BlockSpec((1, NH, HD, HD), lambda i: (ng_ - 1 - i, 0, 0, 0)), c],
        out_specs=[c, c, c, c128, c, pl.BlockSpec((1, HD), lambda i: (0, 0))],
        out_shape=[SDS((L, BR), F32)] * 3 + [SDS((L, 128), F32), SDS((L, BR), BF16), SDS((1, HD), F32)],
        scratch_shapes=[pltpu.VMEM((NH, HD, HD), F32)],
        compiler_params=_cparams(("arbitrary",)),
    )(q, k, v, gb, proj, ng, ssave, dyd)


def _outproj_bwd_call(dx, ys, w, deps=()):
    L = dx.shape[0]

    def body(dx_ref, a_ref, b_ref, c_ref, d_ref, w_ref, *rest):
        da, db, dc, dd, dw_ref = rest[len(deps):]

        @pl.when(pl.program_id(0) == 0)
        def _():
            dw_ref[...] = jnp.zeros_like(dw_ref)

        dxb = dx_ref[...].astype(BF16)
        dmixed = _dot(dxb, w_ref[...], NT)
        for b, o_ref in enumerate((da, db, dc, dd)):
            o_ref[...] = dmixed[:, b * BR:(b + 1) * BR]
        mixed = jnp.concatenate([y_ref[...].astype(BF16) for y_ref in (a_ref, b_ref, c_ref, d_ref)], axis=1)
        dw_ref[...] += _dot(mixed, dxb, TN)

    yspec = pl.BlockSpec((TL, BR), lambda i: (i, 0))
    return pl.pallas_call(
        body, name="outproj_bwd", grid=(L // TL,),
        in_specs=[pl.BlockSpec((TL, D), lambda i: (i, 0)), yspec, yspec, yspec, yspec,
                  pl.BlockSpec((D, D), lambda i: (0, 0))] + _dep_specs(deps),
        out_specs=[yspec] * 4 + [pl.BlockSpec((D, D), lambda i: (0, 0))],
        out_shape=[SDS((L, BR), F32)] * 4 + [SDS((D, D), F32)],
        compiler_params=_cparams(("arbitrary",)),
    )(dx, *ys, w, *deps)


def _slab_cols(slabs):
    widths = [s.shape[1] for s in slabs]
    starts = [sum(widths[:i]) for i in range(len(widths))]
    assert starts[-1] + widths[-1] == PW
    return list(zip(starts, widths))


def _inproj_bwd_x_call(slabs, w, x, g, dx_next, deps=()):
    L = x.shape[0]
    cols = _slab_cols(slabs)
    n = len(slabs)

    def body(*refs):
        dp_refs, (w_ref, x_ref, g_ref, dxn_ref) = refs[:n], refs[n:n + 4]
        dx_ref, dg_ref = refs[n + 4 + len(deps):]

        @pl.when(pl.program_id(0) == 0)
        def _():
            dg_ref[...] = jnp.zeros_like(dg_ref)

        dh = _dot(jnp.concatenate([dp_ref[...] for dp_ref in dp_refs], axis=1), w_ref[...], NT)
        _, vjp = jax.vjp(_rms, x_ref[...], g_ref[...])
        dx, dg = vjp(dh)
        dx_ref[...] = dx + dxn_ref[...]
        dg_ref[...] += dg

    row = lambda w_: pl.BlockSpec((TL, w_), lambda i: (i, 0))
    return pl.pallas_call(
        body, name="inproj_bwd_x", grid=(L // TL,),
        in_specs=[row(cw) for _, cw in cols]
        + [pl.BlockSpec((D, PW), lambda i: (0, 0)), row(D), pl.BlockSpec((1, D), lambda i: (0, 0)), row(D)]
        + _dep_specs(deps),
        out_specs=[row(D), pl.BlockSpec((1, D), lambda i: (0, 0))],
        out_shape=[SDS((L, D), F32), SDS((1, D), F32)],
        compiler_params=_cparams(("arbitrary",)),
    )(*slabs, w, x, g, dx_next, *deps)


def _inproj_bwd_w_call(h, slabs):
    L = h.shape[0]
    cols = _slab_cols(slabs)
    n = len(slabs)

    def body(*refs):
        h_ref, dp_refs, dw_ref = refs[0], refs[1:1 + n], refs[1 + n]

        @pl.when(pl.program_id(0) == 0)
        def _():
            dw_ref[...] = jnp.zeros_like(dw_ref)

        hv = h_ref[...]
        for dp_ref, (c0, cw) in zip(dp_refs, cols):
            dw_ref[:, c0:c0 + cw] += _dot(hv, dp_ref[...], TN)

    row = lambda w_: pl.BlockSpec((TL, w_), lambda i: (i, 0))
    return pl.pallas_call(
        body, name="inproj_bwd_w", grid=(L // TL,),
        in_specs=[row(D)] + [row(cw) for _, cw in cols],
        out_specs=pl.BlockSpec((D, PW), lambda i: (0, 0)),
        out_shape=SDS((D, PW), F32),
        compiler_params=_cparams(("arbitrary",)),
    )(h, *slabs)


def _exchange_call(name, flows):
    n = len(flows)

    def body(*refs):
        srcs, dsts = refs[:n], refs[n:2 * n]
        send_sems, recv_sems, local_sems = refs[2 * n:]
        x, y, c = lax.axis_index("x"), lax.axis_index("y"), lax.axis_index("c")
        me = 4 * x + 2 * y + c
        copies = []
        for mask in range(1, N_DEV):
            px = 1 - x if mask & 4 else x
            py = 1 - y if mask & 2 else y
            pc = 1 - c if mask & 1 else c
            for f, (_, src_at, _, dst_at) in enumerate(flows):
                cp = pltpu.make_async_remote_copy(
                    src_ref=src_at(srcs[f], 4 * px + 2 * py + pc), dst_ref=dst_at(dsts[f], me),
                    send_sem=send_sems.at[mask - 1, f], recv_sem=recv_sems.at[mask - 1, f],
                    device_id=(px, py, pc), device_id_type=pl.DeviceIdType.MESH)
                cp.start()
                copies.append(cp)
        mine = [pltpu.make_async_copy(src_at(srcs[f], me), dst_at(dsts[f], me), local_sems.at[f])
                for f, (_, src_at, _, dst_at) in enumerate(flows)]
        for cp in mine:
            cp.start()
        for cp in copies + mine:
            cp.wait()

    return pl.pallas_call(
        body, name=name,
        in_specs=[pl.BlockSpec(memory_space=pl.ANY)] * n,
        out_specs=[pl.BlockSpec(memory_space=pl.ANY)] * n,
        out_shape=[SDS(tuple(shape), src.dtype) for src, _, shape, _ in flows],
        scratch_shapes=[pltpu.SemaphoreType.DMA((N_DEV - 1, n)), pltpu.SemaphoreType.DMA((N_DEV - 1, n)),
                        pltpu.SemaphoreType.DMA((n,))],
    )(*[f[0] for f in flows])


def _whole(ref, _):
    return ref


def _slot(ref, k):
    return ref.at[k]


_HBM_SPEC = pl.BlockSpec(memory_space=pltpu.HBM)
_SEM_SPEC = pl.BlockSpec(memory_space=pltpu.SEMAPHORE)
_DATAFLOW = pltpu.SideEffectType.DATAFLOW_SIDE_EFFECTING


def _split_copies(views, src_refs, land_refs, send_sems, recv_sems):
    x, y, c = lax.axis_index("x"), lax.axis_index("y"), lax.axis_index("c")
    me = 4 * x + 2 * y + c
    copies = []
    for mask in range(1, N_DEV):
        px = 1 - x if mask & 4 else x
        py = 1 - y if mask & 2 else y
        pc = 1 - c if mask & 1 else c
        for f, (src_at, dst_at) in enumerate(views):
            pair = (mask - 1) * len(views) + f
            copies.append(pltpu.make_async_remote_copy(
                src_ref=src_at(src_refs[f], 4 * px + 2 * py + pc), dst_ref=dst_at(land_refs[f], me),
                send_sem=send_sems.at[pair], recv_sem=recv_sems.at[pair],
                device_id=(px, py, pc), device_id_type=pl.DeviceIdType.MESH))
    return copies


def _split_start_call(name, srcs, lands, views):
    n = len(srcs)

    def body(*refs):
        src_refs, land_refs = refs[:n], refs[n:2 * n]
        send_sems, recv_sems, token = refs[2 * n], refs[2 * n + 1], refs[-1]
        for cp in _split_copies(views, src_refs, land_refs, send_sems, recv_sems):
            cp.start()
        token[...] = jnp.zeros_like(token)

    arrays = list(srcs) + list(lands)
    outs = pl.pallas_call(
        body, name=name,
        out_shape=(pltpu.SemaphoreType.DMA(((N_DEV - 1) * n,)), pltpu.SemaphoreType.DMA(((N_DEV - 1) * n,)),
                   *[pltpu.HBM(a.shape, a.dtype) for a in arrays], SDS((8, 128), F32)),
        in_specs=[_HBM_SPEC] * (2 * n),
        out_specs=(_SEM_SPEC, _SEM_SPEC, *[_HBM_SPEC] * (2 * n), pl.BlockSpec(memory_space=pltpu.VMEM)),
        input_output_aliases={i: 2 + i for i in range(2 * n)},
        compiler_params=pltpu.CompilerParams(has_side_effects=_DATAFLOW),
    )(*[pltpu.with_memory_space_constraint(a, pltpu.HBM) for a in arrays])
    return outs[0], outs[1], list(outs[2:2 + 2 * n]), outs[-1]


def _split_wait_call(name, send_sems, recv_sems, thru, views, after):
    n = len(views)

    def body(*refs):
        src_refs, land_refs = refs[:n], refs[n:2 * n]
        send, recv = refs[2 * n], refs[2 * n + 1]
        for cp in _split_copies(views, src_refs, land_refs, send, recv):
            cp.wait_send()
            cp.wait_recv()

    outs = pl.pallas_call(
        body, name=name,
        out_shape=tuple(pltpu.HBM(a.shape, a.dtype) for a in thru),
        in_specs=[_HBM_SPEC] * (2 * n) + [_SEM_SPEC, _SEM_SPEC, pl.BlockSpec(memory_space=pl.ANY)],
        out_specs=tuple([_HBM_SPEC] * (2 * n)),
        input_output_aliases={i: i for i in range(2 * n)},
        compiler_params=pltpu.CompilerParams(has_side_effects=_DATAFLOW),
    )(*thru, send_sems, recv_sems, after)
    return list(outs[n:])


def _own_slot(block, me):
    zone = lax.empty((N_DEV,) + block.shape, block.dtype)
    return lax.dynamic_update_slice(zone, block[None], (me,) + (0,) * block.ndim)


def _reduce_adamw_call(parts, w, m, v, block, name):
    nsrc = parts.shape[0]
    grid = tuple(s // b for s, b in zip(w.shape, block))
    c1 = 1.0 - ADAM_B1 ** ADAM_STEP
    c2 = 1.0 - ADAM_B2 ** ADAM_STEP

    def body(p_ref, w_ref, m_ref, v_ref, g_ref, d_ref, nm_ref, nv_ref):
        g = p_ref[0].astype(F32)
        for k in range(1, nsrc):
            g = g + p_ref[k].astype(F32)
        nm = ADAM_B1 * m_ref[...] + (1.0 - ADAM_B1) * g
        nv = ADAM_B2 * v_ref[...] + (1.0 - ADAM_B2) * (g * g)
        g_ref[...] = g
        nm_ref[...] = nm
        nv_ref[...] = nv
        d_ref[...] = -ADAM_LR * ((nm / c1) / (jnp.sqrt(nv / c2) + ADAM_EPS) + ADAM_WD * w_ref[...])

    own = pl.BlockSpec(tuple(block), lambda *i: i)
    return pl.pallas_call(
        body, name=name, grid=grid,
        in_specs=[pl.BlockSpec((nsrc,) + tuple(block), lambda *i: (0,) + i), own, own, own],
        out_specs=[own] * 4,
        out_shape=[SDS(w.shape, F32)] * 4,
        compiler_params=_cparams(("parallel",) * len(grid)),
    )(parts, w, m, v)


RELAYOUT_ROWS = 256
SHARD_COLS = IN_COLS // N_DEV


def _win_gather_layout_call(shards):
    def body(w_ref, o_ref):
        nat = jnp.concatenate([w_ref[k].astype(F32) for k in range(N_DEV)], axis=1)
        out = jnp.concatenate([nat[:, :3072], nat[:, 3080:], nat[:, 3072:3080],
                               jnp.zeros((RELAYOUT_ROWS, PW - IN_COLS), F32)], axis=1)
        o_ref[...] = out.astype(BF16)

    return pl.pallas_call(
        body, name="w_in_layout", grid=(D // RELAYOUT_ROWS,),
        in_specs=[pl.BlockSpec((N_DEV, RELAYOUT_ROWS, SHARD_COLS), lambda i: (0, i, 0))],
        out_specs=pl.BlockSpec((RELAYOUT_ROWS, PW), lambda i: (i, 0)),
        out_shape=SDS((D, PW), BF16),
        compiler_params=_cparams(("parallel",)),
    )(shards)


def _win_scatter_layout_call(grad):
    def body(g_ref, o_ref):
        g = g_ref[...]
        nat = jnp.concatenate([g[:, :3072], g[:, AB_COL:AB_COL + 8], g[:, 3072:AB_COL]], axis=1)
        for k in range(N_DEV):
            o_ref[k] = nat[:, SHARD_COLS * k:SHARD_COLS * (k + 1)].astype(BF16)

    return pl.pallas_call(
        body, name="w_in_grad_layout", grid=(D // RELAYOUT_ROWS,),
        in_specs=[pl.BlockSpec((RELAYOUT_ROWS, PW), lambda i: (i, 0))],
        out_specs=pl.BlockSpec((N_DEV, RELAYOUT_ROWS, SHARD_COLS), lambda i: (0, i, 0)),
        out_shape=SDS((N_DEV, D, SHARD_COLS), BF16),
        compiler_params=_cparams(("parallel",)),
    )(grad)


def _reduce_adamw_layers_call(parts, w, m, v, rows, name):
    _, R, C = w.shape
    c1 = 1.0 - ADAM_B1 ** ADAM_STEP
    c2 = 1.0 - ADAM_B2 ** ADAM_STEP

    def body(*refs):
        p_refs = refs[:DEPTH]
        w_ref, m_ref, v_ref, g_ref, d_ref, nm_ref, nv_ref = refs[DEPTH:]
        for l in range(DEPTH):
            @pl.when(pl.program_id(0) == l)
            def _(l=l):
                g = p_refs[l][0].astype(F32)
                for k in range(1, N_DEV):
                    g = g + p_refs[l][k].astype(F32)
                nm = ADAM_B1 * m_ref[0] + (1.0 - ADAM_B1) * g
                nv = ADAM_B2 * v_ref[0] + (1.0 - ADAM_B2) * (g * g)
                g_ref[0] = g
                nm_ref[0] = nm
                nv_ref[0] = nv
                d_ref[0] = -ADAM_LR * ((nm / c1) / (jnp.sqrt(nv / c2) + ADAM_EPS) + ADAM_WD * w_ref[0])

    def part_spec(l):
        return pl.BlockSpec((N_DEV, rows, C), lambda j, i: (0, jnp.where(j == l, i, 0), 0))

    own = pl.BlockSpec((1, rows, C), lambda j, i: (j, i, 0))
    return pl.pallas_call(
        body, name=name, grid=(DEPTH, R // rows),
        in_specs=[part_spec(l) for l in range(DEPTH)] + [own, own, own],
        out_specs=[own] * 4,
        out_shape=[SDS(w.shape, F32)] * 4,
        compiler_params=_cparams(("arbitrary", "arbitrary")),
    )(*parts, w, m, v)


_BIG = ("w_in", "w_out", "a_pw_w", "s5_glu_w")
_CONV = ("a_conv_w", "c_conv_w", "d_conv_w")
_CONV_TAPS = {"a_conv_w": KA, "c_conv_w": KC, "d_conv_w": KD}
_CONV_ROWS = {"a_conv_w": HALO, "c_conv_w": HALO_S, "d_conv_w": HALO_S}
_CONV_WIDTH = {"a_conv_w": BR, "c_conv_w": BR, "d_conv_w": 3 * BR}
_REPLICATED = ("norm_g", "a_conv_b", "a_ln_g", "a_ln_b", "a_pw_b", "s5_lambda_re", "s5_lambda_im", "s5_b_re", "s5_b_im",
               "s5_c_re", "s5_c_im", "s5_d", "s5_log_dt", "s5_glu_b", "d_a_log", "d_dt_bias", "d_norm_g", "final_g")
_REP_TAIL = ("norm_g", "final_g")
_REP_LAYER = tuple(n for n in _REPLICATED if n not in _REP_TAIL)
_WEIGHTS = ("norm_g", "w_in", "a_conv_w", "a_conv_b", "a_ln_g", "a_ln_b", "a_pw_w", "a_pw_b", "s5_lambda_re",
            "s5_lambda_im", "s5_b_re", "s5_b_im", "s5_c_re", "s5_c_im", "s5_d", "s5_log_dt", "s5_glu_w", "s5_glu_b",
            "c_conv_w", "d_conv_w", "d_a_log", "d_dt_bias", "d_norm_g", "w_out", "final_g")


def _size(shape):
    n = 1
    for s in shape:
        n *= s
    return n


PACK_ALIGN = 1024


def _piece_rows(n):
    return -(-n // PACK_ALIGN) * (PACK_ALIGN // 128)


def _pack_rows(pieces, row_mult):
    rows = []
    for p in pieces:
        flat = p.reshape(-1)
        rows.append(jnp.pad(flat, (0, _piece_rows(flat.shape[0]) * 128 - flat.shape[0])).reshape(-1, 128))
    out = jnp.concatenate(rows, axis=0)
    return jnp.pad(out, ((0, (-out.shape[0]) % row_mult), (0, 0)))


def _pack_layers(pieces):
    rows = []
    for p in pieces:
        flat = p.reshape(DEPTH, -1)
        nr = _piece_rows(flat.shape[1])
        rows.append(jnp.pad(flat, ((0, 0), (0, nr * 128 - flat.shape[1]))).reshape(DEPTH, nr, 128))
    return jnp.concatenate(rows, axis=1)


def _unpack_layers(packed, shapes):
    out, row = [], 0
    for s in shapes:
        n = _size(s[1:])
        nr = _piece_rows(n)
        out.append(packed[:, row:row + nr].reshape(DEPTH, -1)[:, :n].reshape(s))
        row += nr
    return out


def _unpack(packed, shapes):
    out, row = [], 0
    for s in shapes:
        n = _size(s)
        nr = _piece_rows(n)
        out.append(packed[row:row + nr].reshape(-1)[:n].reshape(s))
        row += nr
    return out


_GATHER_VIEWS = [(_whole, _slot)] * 5


def _gather_start(shards, layer, me):
    srcs = [shards[n].astype(BF16) for n in _BIG]
    srcs.append(_pack_rows([shards[n] for n in _CONV], 8))
    lands = [_own_slot(s, me) for s in srcs]
    return _split_start_call("gather_start_%d" % layer, srcs, lands, _GATHER_VIEWS)


def _gather_finish(weights, layer, started, after):
    send, recv, thru, _ = started
    w_in, w_out, a_pw, glu, conv_all = _split_wait_call("gather_wait_%d" % layer, send, recv, thru, _GATHER_VIEWS, after)
    full = {"w_in": _win_gather_layout_call(w_in), "w_out": w_out.reshape(D, D), "a_pw_w": a_pw.reshape(BR, BR),
            "s5_glu_w": glu.reshape(BR, BR)}
    shapes = [weights[n].shape[1:] for n in _CONV]
    per_dev = [_unpack(conv_all[k], shapes) for k in range(N_DEV)]
    for i, n in enumerate(_CONV):
        whole = jnp.concatenate([per_dev[k][i] for k in range(N_DEV)], axis=-1)
        full[n] = jnp.pad(whole, ((0, _CONV_ROWS[n] - _CONV_TAPS[n]), (0, 0)))
    return full


def _rows_view(rows):
    return lambda ref, k: ref.at[pl.ds(k * rows, rows), :]


_SCATTER_VIEWS = [(_slot, _slot), (_rows_view(D // N_DEV), _slot), (_rows_view(BR // N_DEV), _slot),
                  (_rows_view(BR // N_DEV), _slot), (_whole, _slot)]


def _scatter_start(grads, small, layer, me):
    srcs = [_win_scatter_layout_call(grads["w_in"])] + [grads[n].astype(BF16) for n in _BIG[1:]]
    own = [lax.dynamic_index_in_dim(srcs[0], me, 0, keepdims=False)]
    for s, rows in zip(srcs[1:], (D // N_DEV, BR // N_DEV, BR // N_DEV)):
        own.append(lax.dynamic_slice_in_dim(s, me * rows, rows, axis=0))
    lands = [_own_slot(o, me) for o in own + [small]]
    return _split_start_call("scatter_start_%d" % layer, srcs + [small], lands, _SCATTER_VIEWS)


_S5_KERNEL_SHAPES = {"s5_lambda_re": (1, NSTATE), "s5_lambda_im": (1, NSTATE), "s5_log_dt": (1, 16),
                     "s5_b_re": (NSTATE, 16), "s5_b_im": (NSTATE, 16), "s5_c_re": (BR, 64), "s5_c_im": (BR, 64)}
_S5_KEYS = {"s5_lambda_re": "lam_re", "s5_lambda_im": "lam_im", "s5_log_dt": "log_dt", "s5_b_re": "b_re",
            "s5_b_im": "b_im", "s5_c_re": "c_re", "s5_c_im": "c_im"}


def _s5_inputs_all(weights):
    return {n: weights[n].reshape((DEPTH,) + s) for n, s in _S5_KERNEL_SHAPES.items()}


def _s5_inputs(p):
    return {_S5_KEYS[n]: p["s5_in"][n] for n in _S5_KERNEL_SHAPES}


def _row(a, width=None):
    a = a.reshape(1, -1)
    return a if width is None else jnp.pad(a, ((0, 0), (0, width - a.shape[1])))


def _layer_params(p):
    q = dict(p)
    for n in ("norm_g", "a_conv_b", "a_ln_g", "a_ln_b", "a_pw_b", "s5_d", "s5_glu_b", "d_norm_g"):
        q[n] = _row(p[n])
    q["d_a_log"] = _row(p["d_a_log"], 128)
    q["d_dt_bias"] = _row(p["d_dt_bias"], 128)
    return q


def _layer_fwd(x, p, deps=()):
    q = _layer_params(p)
    proj, h = _inproj_call(x, q["norm_g"], q["w_in"], deps)
    ya, yc = _ac_fwd_call(proj, q)
    prep = p["s5_prep"]
    yb, cin_r, cin_i = _s5_fwd_call(proj, prep, q)
    dq, dk, dv, dgb = _dn_pre_fwd_call(proj, q)
    yd, ssave = _dn_core_fwd_call(proj, dq, dk, dv, dgb, q["d_norm_g"])
    x_next = _outproj_call(x, (ya, yb, yc, yd), q["w_out"])
    saved = dict(x=x, proj=proj, h=h, ya=ya, yb=yb, yc=yc, yd=yd, cin_r=cin_r, cin_i=cin_i,
                 q=dq, k=dk, v=dv, gb=dgb, ssave=ssave, prep=prep)
    return x_next, saved


def _layer_bwd(dx, p, sv, deps=(), on_weight_grads=None):
    q = _layer_params(p)
    proj = sv["proj"]
    dya, dyb, dyc, dyd, g_wout = _outproj_bwd_call(dx, (sv["ya"], sv["yb"], sv["yc"], sv["yd"]), q["w_out"], deps)
    dpa, dpc, g_acw, g_acb, g_alg, g_alb, g_apw, g_apb, g_ccw = _ac_bwd_call(proj, q, dya, dyc)
    dpb, *s5g = _s5_bwd_call(proj, sv["prep"], q, sv["cin_r"], sv["cin_i"], dyb)
    g_sd, g_gw, g_gb = s5g[6:]
    g_lre, g_lim, g_ldt, g_bre, g_bim, g_cre, g_cim = _s5_prep_bwd_call(_s5_inputs(p), s5g[:6])
    dq, dk, dv, dgb, dz, g_ng = _dn_core_bwd_call(proj, sv["q"], sv["k"], sv["v"], sv["gb"], q["d_norm_g"], sv["ssave"], dyd)
    dqkv, dab, g_dcw, g_alog, g_dtb = _dn_pre_bwd_call(proj, q, dq, dk, dv, dgb)
    slabs = (dpa, dpb, dpc, dqkv, dz, dab)
    g_win = _inproj_bwd_w_call(sv["h"], slabs)
    grads = {"w_in": g_win, "a_conv_w": g_acw, "a_conv_b": g_acb, "a_ln_g": g_alg, "a_ln_b": g_alb,
             "a_pw_w": g_apw, "a_pw_b": g_apb, "s5_lambda_re": g_lre, "s5_lambda_im": g_lim, "s5_b_re": g_bre,
             "s5_b_im": g_bim, "s5_c_re": g_cre, "s5_c_im": g_cim, "s5_d": g_sd, "s5_log_dt": g_ldt, "s5_glu_w": g_gw,
             "s5_glu_b": g_gb, "c_conv_w": g_ccw, "d_conv_w": g_dcw, "d_a_log": g_alog[:, :NH], "d_dt_bias": g_dtb[:, :NH],
             "d_norm_g": g_ng, "w_out": g_wout}
    tokens = ()
    if on_weight_grads is not None:
        small = _pack_rows([grads[n] for n in _REP_LAYER + _CONV], 8)
        tokens = on_weight_grads({n: grads[n] for n in _BIG}, small)
    dx_prev, grads["norm_g"] = _inproj_bwd_x_call(slabs, q["w_in"], sv["x"], q["norm_g"], dx, tokens)
    return dx_prev, grads


def _step(x, target, weights, moments_m, moments_v):
    me = 4 * lax.axis_index("x") + 2 * lax.axis_index("y") + lax.axis_index("c")
    layer_names = [n for n in _WEIGHTS if n != "final_g"]
    s5_all = _s5_inputs_all(weights)

    sharded = _BIG + _CONV
    gather = _gather_start({n: weights[n][0] for n in sharded}, 0, me)
    preps = [_s5_prep_call({_S5_KEYS[n]: a[l] for n, a in s5_all.items()}, [gather[3]]) for l in range(DEPTH)]
    x_out, after, layers, saved = x, preps[-1][0], [], []
    for l in range(DEPTH):
        full = _gather_finish(weights, l, gather, after)
        deps = ()
        if l + 1 < DEPTH:
            nxt, full["w_out"] = lax.optimization_barrier(({n: weights[n][l + 1] for n in sharded}, full["w_out"]))
            gather = _gather_start(nxt, l + 1, me)
            deps = [gather[3]]
        p = {n: (full[n] if n in full else weights[n][l]) for n in layer_names}
        p["s5_in"] = {n: a[l] for n, a in s5_all.items()}
        p["s5_prep"] = preps[l]
        layers.append(p)
        x_out, sv = _layer_fwd(x_out, p, deps)
        after = x_out
        saved.append(sv)
    dx0, g_final, loss_part = _loss_call(x_out, _row(weights["final_g"]), target)

    per_layer, scatters = [None] * DEPTH, [None] * DEPTH
    for l in range(DEPTH - 1, -1, -1):
        def start(big, small, l=l):
            scatters[l] = _scatter_start(big, small, l, me)
            return [scatters[l][3]]
        dx0, per_layer[l] = _layer_bwd(dx0, layers[l], saved[l], [scatters[l + 1][3]] if l + 1 < DEPTH else (), start)
    loss = lax.psum(loss_part[0, 0], ("x", "y", "c"))
    results = {}

    parts = [_split_wait_call("scatter_wait_%d" % l, scatters[l][0], scatters[l][1], scatters[l][2], _SCATTER_VIEWS, dx0)
             for l in range(DEPTH - 1, -1, -1)][::-1]
    rows = {"w_in": RELAYOUT_ROWS, "w_out": D // N_DEV, "a_pw_w": BR // N_DEV, "s5_glu_w": BR // N_DEV}
    for i, n in enumerate(_BIG):
        results[n] = _reduce_adamw_layers_call([parts[l][i] for l in range(DEPTH)], weights[n], moments_m[n],
                                               moments_v[n], rows[n], "adamw_" + n)

    conv_shapes = [(DEPTH, _CONV_ROWS[n], _CONV_WIDTH[n]) for n in _CONV]
    pack = lambda d: _pack_layers([d[n] for n in _REP_LAYER] + [jnp.zeros(s, F32) for s in conv_shapes])
    wpack = pack(weights)
    res = _reduce_adamw_layers_call([parts[l][len(_BIG)] for l in range(DEPTH)], wpack, pack(moments_m), pack(moments_v),
                                    wpack.shape[1], "adamw_replicated")
    shapes = [weights[n].shape for n in _REP_LAYER] + conv_shapes
    res = [_unpack_layers(r, shapes) for r in res]
    for i, n in enumerate(_REP_LAYER):
        results[n] = tuple(r[i] for r in res)

    tail_g = [jnp.stack([g["norm_g"] for g in per_layer]).reshape(DEPTH, D), g_final.reshape(D)]
    packt = lambda arrs: _pack_rows(arrs, 8)
    gathered, = _exchange_call("gather_tail_grads", [(packt(tail_g), _whole, (N_DEV,) + packt(tail_g).shape, _slot)])
    tail = _reduce_adamw_call(gathered, packt([weights[n] for n in _REP_TAIL]), packt([moments_m[n] for n in _REP_TAIL]),
                              packt([moments_v[n] for n in _REP_TAIL]), packt(tail_g).shape, "adamw_tail")
    tail = [_unpack(r, [weights[n].shape for n in _REP_TAIL]) for r in tail]
    for i, n in enumerate(_REP_TAIL):
        results[n] = tuple(r[i] for r in tail)

    own_g = []
    for i, n in enumerate(_CONV):
        width = _CONV_WIDTH[n] // N_DEV
        summed = res[0][len(_REP_LAYER) + i][:, :_CONV_TAPS[n], :]
        own_g.append(lax.dynamic_slice_in_dim(summed, me * width, width, axis=2))
    packc = lambda arrs: _pack_rows(arrs, 8)
    res = _reduce_adamw_call(packc(own_g)[None], packc([weights[n] for n in _CONV]), packc([moments_m[n] for n in _CONV]),
                             packc([moments_v[n] for n in _CONV]), packc(own_g).shape, "adamw_conv")
    res = [_unpack(r, [weights[n].shape for n in _CONV]) for r in res]
    for i, n in enumerate(_CONV):
        results[n] = tuple(r[i] for r in res)

    outs = [loss, dx0]
    for kind in range(4):
        outs += [results[n][kind] for n in _WEIGHTS]
    return tuple(outs)


def kernel(x, norm_g, w_in, a_conv_w, a_conv_b, a_ln_g, a_ln_b, a_pw_w, a_pw_b, s5_lambda_re, s5_lambda_im, s5_b_re, s5_b_im, s5_c_re, s5_c_im, s5_d, s5_log_dt, s5_glu_w, s5_glu_b, c_conv_w, d_conv_w, d_a_log, d_dt_bias, d_norm_g, w_out, final_g, loss_target, m_norm_g, m_w_in, m_a_conv_w, m_a_conv_b, m_a_ln_g, m_a_ln_b, m_a_pw_w, m_a_pw_b, m_s5_lambda_re, m_s5_lambda_im, m_s5_b_re, m_s5_b_im, m_s5_c_re, m_s5_c_im, m_s5_d, m_s5_log_dt, m_s5_glu_w, m_s5_glu_b, m_c_conv_w, m_d_conv_w, m_d_a_log, m_d_dt_bias, m_d_norm_g, m_w_out, m_final_g, v_norm_g, v_w_in, v_a_conv_w, v_a_conv_b, v_a_ln_g, v_a_ln_b, v_a_pw_w, v_a_pw_b, v_s5_lambda_re, v_s5_lambda_im, v_s5_b_re, v_s5_b_im, v_s5_c_re, v_s5_c_im, v_s5_d, v_s5_log_dt, v_s5_glu_w, v_s5_glu_b, v_c_conv_w, v_d_conv_w, v_d_a_log, v_d_dt_bias, v_d_norm_g, v_w_out, v_final_g):
    weights = dict(norm_g=norm_g, w_in=w_in, a_conv_w=a_conv_w, a_conv_b=a_conv_b, a_ln_g=a_ln_g, a_ln_b=a_ln_b, a_pw_w=a_pw_w, a_pw_b=a_pw_b, s5_lambda_re=s5_lambda_re, s5_lambda_im=s5_lambda_im, s5_b_re=s5_b_re, s5_b_im=s5_b_im, s5_c_re=s5_c_re, s5_c_im=s5_c_im, s5_d=s5_d, s5_log_dt=s5_log_dt, s5_glu_w=s5_glu_w, s5_glu_b=s5_glu_b, c_conv_w=c_conv_w, d_conv_w=d_conv_w, d_a_log=d_a_log, d_dt_bias=d_dt_bias, d_norm_g=d_norm_g, w_out=w_out, final_g=final_g)
    mom_m = dict(norm_g=m_norm_g, w_in=m_w_in, a_conv_w=m_a_conv_w, a_conv_b=m_a_conv_b, a_ln_g=m_a_ln_g, a_ln_b=m_a_ln_b, a_pw_w=m_a_pw_w, a_pw_b=m_a_pw_b, s5_lambda_re=m_s5_lambda_re, s5_lambda_im=m_s5_lambda_im, s5_b_re=m_s5_b_re, s5_b_im=m_s5_b_im, s5_c_re=m_s5_c_re, s5_c_im=m_s5_c_im, s5_d=m_s5_d, s5_log_dt=m_s5_log_dt, s5_glu_w=m_s5_glu_w, s5_glu_b=m_s5_glu_b, c_conv_w=m_c_conv_w, d_conv_w=m_d_conv_w, d_a_log=m_d_a_log, d_dt_bias=m_d_dt_bias, d_norm_g=m_d_norm_g, w_out=m_w_out, final_g=m_final_g)
    mom_v = dict(norm_g=v_norm_g, w_in=v_w_in, a_conv_w=v_a_conv_w, a_conv_b=v_a_conv_b, a_ln_g=v_a_ln_g, a_ln_b=v_a_ln_b, a_pw_w=v_a_pw_w, a_pw_b=v_a_pw_b, s5_lambda_re=v_s5_lambda_re, s5_lambda_im=v_s5_lambda_im, s5_b_re=v_s5_b_re, s5_b_im=v_s5_b_im, s5_c_re=v_s5_c_re, s5_c_im=v_s5_c_im, s5_d=v_s5_d, s5_log_dt=v_s5_log_dt, s5_glu_w=v_s5_glu_w, s5_glu_b=v_s5_glu_b, c_conv_w=v_c_conv_w, d_conv_w=v_d_conv_w, d_a_log=v_d_a_log, d_dt_bias=v_d_dt_bias, d_norm_g=v_d_norm_g, w_out=v_w_out, final_g=v_final_g)
    outs = _step(x[0], loss_target[0], weights, mom_m, mom_v)
    return (outs[0], outs[1][None]) + outs[2:]
```

```python
import functools

import jax
import jax.numpy as jnp
from jax import lax
from jax.experimental import pallas as pl
from jax.experimental.pallas import tpu as pltpu

F32 = jnp.float32
BF16 = jnp.bfloat16
HI = lax.Precision.HIGHEST
SDS = jax.ShapeDtypeStruct

N_DEV = 8
D = 1024
BR = 256
DEPTH = 4
IN_COLS = 3336
PW = 3456
AB_COL = 3328
EPS = 1e-6
TL = 512
SEG = TL // 8
HALO = 32
HALO_S = 8
KA, KC, KD = 31, 3, 4
CH = 64
DN_GROUP = 4
DN_FWD_GROUPS = 2
NH, HD = 4, 64
NSTATE = 1024
S5_PARTS = 2
VMEM_LIMIT = 56 * 1024 * 1024

ADAM_LR, ADAM_B1, ADAM_B2, ADAM_EPS, ADAM_WD, ADAM_STEP = 0.001, 0.9, 0.999, 1e-08, 0.01, 10

NN = ((1,), (0,))
NT = ((1,), (1,))
TN = ((0,), (0,))


def _dot(a, b, dims, prec=None):
    return lax.dot_general(a, b, (dims, ((), ())), precision=prec, preferred_element_type=F32)


def _make_mm(cast, prec, fwd_dims):
    def prep(t):
        return t.astype(cast) if cast is not None else t

    @jax.custom_vjp
    def mm(a, w):
        return _dot(prep(a), prep(w), fwd_dims, prec)

    def fwd(a, w):
        return mm(a, w), (a, w)

    def bwd(res, dy):
        a, w = res
        a, w, dy = prep(a), prep(w), prep(dy)
        if fwd_dims == NN:
            return _dot(dy, w, NT, prec), _dot(a, dy, TN, prec)
        if fwd_dims == NT:
            return _dot(dy, w, NN, prec), _dot(dy, a, TN, prec)
        return _dot(w, dy, NT, prec), _dot(a, dy, NN, prec)

    mm.defvjp(fwd, bwd)
    return mm


mm = _make_mm(BF16, None, NN)
mm_nt = _make_mm(BF16, None, NT)
mm_tn = _make_mm(BF16, None, TN)
mmh = _make_mm(None, HI, NN)
mmh_nt = _make_mm(None, HI, NT)


def _sigmoid(x):
    return jax.nn.sigmoid(x)


def _silu(x):
    return x * jax.nn.sigmoid(x)


def _gelu(x):
    return 0.5 * x * (1.0 + jnp.tanh(0.7978845608028654 * (x + 0.044715 * (x * x * x))))


def _softplus(x):
    return jnp.maximum(x, 0.0) + jnp.log1p(jnp.exp(-jnp.abs(x)))


def _rms(x, g):
    return x * lax.rsqrt(jnp.mean(x * x, axis=-1, keepdims=True) + EPS) * g


def _cparams(sem):
    return pltpu.CompilerParams(dimension_semantics=sem, vmem_limit_bytes=VMEM_LIMIT)


def _tap_offsets(halo, taps):
    return [halo - (taps - 1) + k for k in range(taps)]


def _conv_fwd_impl(acat, w, tile, halo, taps):
    n = tile + halo
    out = None
    for k, off in enumerate(_tap_offsets(halo, taps)):
        src = jnp.roll(acat, n - off, axis=0)[:tile, :] if off != halo else acat[halo:, :]
        term = src * w[k:k + 1, :]
        out = term if out is None else out + term
    return out


def _make_conv(tile, halo, taps):
    @jax.custom_vjp
    def conv(acat, w):
        return _conv_fwd_impl(acat, w, tile, halo, taps)

    def fwd(acat, w):
        return conv(acat, w), (acat, w)

    def bwd(res, dy):
        acat, w = res
        n = tile + halo
        dyp = jnp.concatenate([dy, jnp.zeros((halo, dy.shape[1]), F32)], axis=0)
        rows = lax.broadcasted_iota(jnp.int32, w.shape, 0)
        dacat = None
        dw = jnp.zeros(w.shape, F32)
        for k, off in enumerate(_tap_offsets(halo, taps)):
            term = jnp.roll(dyp, off, axis=0) * w[k:k + 1, :]
            dacat = term if dacat is None else dacat + term
            src = jnp.roll(acat, n - off, axis=0)[:tile, :] if off != halo else acat[halo:, :]
            dw = dw + jnp.where(rows == k, jnp.sum(dy * src, axis=0, keepdims=True), 0.0)
        return dacat, dw

    conv.defvjp(fwd, bwd)
    return conv


def _halo_spec(tile, halo, width, col):
    per = tile // halo
    return pl.BlockSpec((halo, width), lambda i: (jnp.maximum(i * per - 1, 0), col))


def _halo_spec_rev(nt, tile, halo, width, col):
    per = tile // halo
    return pl.BlockSpec((halo, width), lambda i: (jnp.maximum((nt - 1 - i) * per - 1, 0), col))


def _dep_specs(deps):
    return [pl.BlockSpec((8, 128), lambda *_: (0, 0)) for _ in deps]


def _inproj_call(x, g, w, deps=()):
    L = x.shape[0]

    def body(x_ref, g_ref, w_ref, *rest):
        p_ref, h_ref = rest[len(deps):]
        h = _rms(x_ref[...], g_ref[...]).astype(BF16)
        h_ref[...] = h
        p_ref[...] = _dot(h, w_ref[...], NN)

    return pl.pallas_call(
        body, name="inproj", grid=(L // TL,),
        in_specs=[pl.BlockSpec((TL, D), lambda i: (i, 0)), pl.BlockSpec((1, D), lambda i: (0, 0)),
                  pl.BlockSpec((D, PW), lambda i: (0, 0))] + _dep_specs(deps),
        out_specs=[pl.BlockSpec((TL, PW), lambda i: (i, 0)), pl.BlockSpec((TL, D), lambda i: (i, 0))],
        out_shape=[SDS((L, PW), F32), SDS((L, D), BF16)],
        compiler_params=_cparams(("parallel",)),
    )(x, g, w, *deps)


def _outproj_call(x, ys, w):
    L = x.shape[0]

    def body(x_ref, a_ref, b_ref, c_ref, d_ref, w_ref, o_ref):
        mixed = jnp.concatenate([y_ref[...].astype(BF16) for y_ref in (a_ref, b_ref, c_ref, d_ref)], axis=1)
        o_ref[...] = x_ref[...] + _dot(mixed, w_ref[...], NN)

    yspec = pl.BlockSpec((TL, BR), lambda i: (i, 0))
    return pl.pallas_call(
        body, name="outproj", grid=(L // TL,),
        in_specs=[pl.BlockSpec((TL, D), lambda i: (i, 0)), yspec, yspec, yspec, yspec,
                  pl.BlockSpec((D, D), lambda i: (0, 0))],
        out_specs=pl.BlockSpec((TL, D), lambda i: (i, 0)),
        out_shape=SDS((L, D), F32),
        compiler_params=_cparams(("parallel",)),
    )(x, *ys, w)


def _loss_call(x, g, target):
    L = x.shape[0]

    def body(x_ref, g_ref, t_ref, dx_ref, dg_ref, loss_ref):
        @pl.when(pl.program_id(0) == 0)
        def _():
            dg_ref[...] = jnp.zeros_like(dg_ref)
            loss_ref[...] = jnp.zeros_like(loss_ref)

        y, vjp = jax.vjp(_rms, x_ref[...], g_ref[...])
        err = y - t_ref[...]
        dx, dg = vjp(err * (1.0 / D))
        dx_ref[...] = dx
        dg_ref[...] += dg
        tot = jnp.sum(jnp.sum(err * err, axis=1, keepdims=True), axis=0, keepdims=True)
        loss_ref[...] += jnp.broadcast_to(tot * (0.5 / D), loss_ref.shape)

    return pl.pallas_call(
        body, name="loss_head", grid=(L // TL,),
        in_specs=[pl.BlockSpec((TL, D), lambda i: (i, 0)), pl.BlockSpec((1, D), lambda i: (0, 0)),
                  pl.BlockSpec((TL, D), lambda i: (i, 0))],
        out_specs=[pl.BlockSpec((TL, D), lambda i: (i, 0)), pl.BlockSpec((1, D), lambda i: (0, 0)),
                   pl.BlockSpec((1, 128), lambda i: (0, 0))],
        out_shape=[SDS((L, D), F32), SDS((1, D), F32), SDS((1, 128), F32)],
        compiler_params=_cparams(("arbitrary",)),
    )(x, g, target)


def _branch_a(valw, gatew, z, cw, cb, lg, lb, pw, pb, conv):
    a = conv(valw * _sigmoid(gatew), cw) + cb
    mu = jnp.mean(a, axis=-1, keepdims=True)
    xc = a - mu
    y = xc * lax.rsqrt(jnp.mean(xc * xc, axis=-1, keepdims=True) + EPS) * lg + lb
    y = mm(_silu(y), pw) + pb
    return y * _silu(z)


def _branch_c(bg, cw_, xw, z, w3, conv):
    return bg * conv(cw_ * xw, w3) * _silu(z)


def _ac_fwd_call(proj, p):
    L = proj.shape[0]

    def body(val, gate, za, hval, hgate, cb_, cc, cx, cz, hcc, hcx,
             acw, acb, alg, alb, apw, apb, ccw, ya_ref, yc_ref):
        nf = (pl.program_id(0) > 0).astype(F32)
        win = lambda h, m: jnp.concatenate([h[...] * nf, m[...]], axis=0)
        conv_a = functools.partial(_conv_fwd_impl, tile=TL, halo=HALO, taps=KA)
        conv_c = functools.partial(_conv_fwd_impl, tile=TL, halo=HALO, taps=KC)
        ya_ref[...] = _branch_a(win(hval, val), win(hgate, gate), za[...], acw[...], acb[...], alg[...], alb[...],
                                apw[...], apb[...], conv_a)
        yc_ref[...] = _branch_c(cb_[...], win(hcc, cc), win(hcx, cx), cz[...], ccw[...], conv_c)

    col = lambda j: pl.BlockSpec((TL, BR), lambda i: (i, j))
    hal = lambda j: _halo_spec(TL, HALO, BR, j)
    full = lambda a: pl.BlockSpec(a.shape, lambda i: (0,) * a.ndim)
    params = (p["a_conv_w"], p["a_conv_b"], p["a_ln_g"], p["a_ln_b"], p["a_pw_w"], p["a_pw_b"], p["c_conv_w"])
    return pl.pallas_call(
        body, name="ac_fwd", grid=(L // TL,),
        in_specs=[col(0), col(1), col(2), hal(0), hal(1), col(5), col(6), col(7), col(8), hal(6), hal(7)]
        + [full(a) for a in params],
        out_specs=[pl.BlockSpec((TL, BR), lambda i: (i, 0))] * 2,
        out_shape=[SDS((L, BR), F32)] * 2,
        compiler_params=_cparams(("parallel",)),
    )(*([proj] * 11), *params)


def _ac_bwd_call(proj, p, dya, dyc):
    L = proj.shape[0]
    nt = L // TL

    def body(val, gate, za, hval, hgate, cb_, cc, cx, cz, hcc, hcx,
             acw, acb, alg, alb, apw, apb, ccw, dya_ref, dyc_ref,
             da_ref, dc_ref, g_acw, g_acb, g_alg, g_alb, g_apw, g_apb, g_ccw, carry):
        i = pl.program_id(0)
        gouts = (g_acw, g_acb, g_alg, g_alb, g_apw, g_apb, g_ccw)

        @pl.when(i == 0)
        def _():
            carry[...] = jnp.zeros_like(carry)
            for r in gouts:
                r[...] = jnp.zeros_like(r)

        nf = (i < nt - 1).astype(F32)
        win = lambda h, m: jnp.concatenate([h[...] * nf, m[...]], axis=0)
        conv_a = _make_conv(TL, HALO, KA)
        conv_c = _make_conv(TL, HALO, KC)

        def f(valw, gatew, z, bg, ccw_, cxw, czv, w1, b1, lg, lb, pw, pb, w3):
            return (_branch_a(valw, gatew, z, w1, b1, lg, lb, pw, pb, conv_a),
                    _branch_c(bg, ccw_, cxw, czv, w3, conv_c))

        _, vjp = jax.vjp(f, win(hval, val), win(hgate, gate), za[...], cb_[...], win(hcc, cc), win(hcx, cx), cz[...],
                         acw[...], acb[...], alg[...], alb[...], apw[...].astype(F32), apb[...], ccw[...])
        (dvalw, dgatew, dz, dbg, dccw, dcxw, dczv, d1, d2, d3, d4, d5, d6, d7) = vjp((dya_ref[...], dyc_ref[...]))

        def settle(slot, dwin):
            tail = jnp.concatenate([jnp.zeros((TL - HALO, BR), F32), carry[slot]], axis=0)
            carry[slot] = dwin[:HALO, :]
            return (dwin[HALO:, :] + tail).astype(BF16)

        da_ref[:, 0:BR] = settle(0, dvalw)
        da_ref[:, BR:2 * BR] = settle(1, dgatew)
        da_ref[:, 2 * BR:3 * BR] = dz.astype(BF16)
        dc_ref[:, 0:BR] = dbg.astype(BF16)
        dc_ref[:, BR:2 * BR] = settle(2, dccw)
        dc_ref[:, 2 * BR:3 * BR] = settle(3, dcxw)
        dc_ref[:, 3 * BR:4 * BR] = dczv.astype(BF16)
        for r, g in zip(gouts, (d1, d2, d3, d4, d5, d6, d7)):
            r[...] += g

    col = lambda j: pl.BlockSpec((TL, BR), lambda i: (nt - 1 - i, j))
    hal = lambda j: _halo_spec_rev(nt, TL, HALO, BR, j)
    full = lambda a: pl.BlockSpec(a.shape, lambda i: (0,) * a.ndim)
    params = (p["a_conv_w"], p["a_conv_b"], p["a_ln_g"], p["a_ln_b"], p["a_pw_w"], p["a_pw_b"], p["c_conv_w"])
    rev = lambda w: pl.BlockSpec((TL, w), lambda i: (nt - 1 - i, 0))
    return pl.pallas_call(
        body, name="ac_bwd", grid=(nt,),
        in_specs=[col(0), col(1), col(2), hal(0), hal(1), col(5), col(6), col(7), col(8), hal(6), hal(7)]
        + [full(a) for a in params] + [rev(BR), rev(BR)],
        out_specs=[rev(3 * BR), rev(4 * BR)] + [full(a) for a in params],
        out_shape=[SDS((L, 3 * BR), BF16), SDS((L, 4 * BR), BF16)] + [SDS(a.shape, F32) for a in params],
        scratch_shapes=[pltpu.VMEM((4, HALO, BR), F32)],
        compiler_params=_cparams(("arbitrary",)),
    )(*([proj] * 11), *params, dya, dyc)


def _iota2(shape, dim):
    return lax.broadcasted_iota(jnp.int32, shape, dim)


def _s5_params(lam_re, lam_im, logdt, b_re, b_im, c_re, c_im):
    eg = (_iota2((16, NSTATE), 1) >> 6 == _iota2((16, NSTATE), 0)).astype(F32)
    dt = jnp.exp(mmh(jnp.broadcast_to(logdt, (8, 16)), eg)[0:1, :])
    lr = jnp.minimum(lam_re, -1e-4)
    li = lam_im
    mag = jnp.exp(lr * dt)
    lbr = mag * jnp.cos(li * dt)
    lbi = mag * jnp.sin(li * dt)
    den = lr * lr + li * li
    nr = lbr - 1.0
    fr = (nr * lr + lbi * li) / den
    fi = (lbi * lr - nr * li) / den
    row = _iota2((8, NSTATE), 0)
    f8 = jnp.where(row == 0, fr, jnp.where(row == 1, fi, 0.0))
    eye = (_iota2((NSTATE, NSTATE), 0) == _iota2((NSTATE, NSTATE), 1)).astype(F32)
    fcol = mmh_nt(eye, f8)
    frc, fic = fcol[:, 0:1], fcol[:, 1:2]
    bbr = frc * b_re - fic * b_im
    bbi = frc * b_im + fic * b_re
    e1 = ((_iota2((16, BR), 1) & 15) == _iota2((16, BR), 0)).astype(F32)
    m1 = ((_iota2((NSTATE, BR), 0) >> 6) == (_iota2((NSTATE, BR), 1) >> 4)).astype(F32)
    wbr = mmh(bbr, e1) * m1
    wbi = mmh(bbi, e1) * m1
    e2 = ((_iota2((64, NSTATE), 1) & 63) == _iota2((64, NSTATE), 0)).astype(F32)
    m2 = ((_iota2((BR, NSTATE), 0) >> 4) == (_iota2((BR, NSTATE), 1) >> 6)).astype(F32)
    wcr = mmh(c_re, e2) * m2
    wci = mmh(c_im, e2) * m2
    return lbr, lbi, wbr, wbi, wcr, wci


_S5_OUT = [(1, NSTATE), (1, NSTATE), (NSTATE, BR), (NSTATE, BR), (BR, NSTATE), (BR, NSTATE)]


def _s5_prep_call(sp, deps=()):
    def body(lre, lim, ldt, bre, bim, cre, cim, *rest):
        o_lbr, o_lbi, o_wbr, o_wbi, o_wcr, o_wci, pwr, pwi, qwr, qwi = rest[len(deps):]
        lbr, lbi, wbr, wbi, wcr, wci = _s5_params(lre[...], lim[...], ldt[...], bre[...], bim[...], cre[...], cim[...])
        o_lbr[...], o_lbi[...], o_wbr[...], o_wbi[...], o_wcr[...], o_wci[...] = lbr, lbi, wbr, wbi, wcr, wci
        pr, pi = lbr, lbi
        for i in range(SEG):
            pwr[i:i + 1, :] = pr
            pwi[i:i + 1, :] = pi
            qwr[SEG - 1 - i:SEG - i, :] = pr
            qwi[SEG - 1 - i:SEG - i, :] = -pi
            pr, pi = pr * lbr - pi * lbi, pr * lbi + pi * lbr

    args = (sp["lam_re"], sp["lam_im"], sp["log_dt"], sp["b_re"], sp["b_im"], sp["c_re"], sp["c_im"])
    return pl.pallas_call(
        body, name="s5_prep",
        out_shape=[SDS(s, F32) for s in _S5_OUT] + [SDS((SEG, NSTATE), F32)] * 4,
        compiler_params=pltpu.CompilerParams(vmem_limit_bytes=VMEM_LIMIT),
    )(*args, *deps)


def _s5_prep_bwd_call(sp, cots):
    def body(lre, lim, ldt, bre, bim, cre, cim, c0, c1, c2, c3, c4, c5, *outs):
        _, vjp = jax.vjp(_s5_params, lre[...], lim[...], ldt[...], bre[...], bim[...], cre[...], cim[...])
        grads = vjp((c0[...], c1[...], c2[...], c3[...], c4[...], c5[...]))
        for o, g in zip(outs, grads):
            o[...] = g

    args = (sp["lam_re"], sp["lam_im"], sp["log_dt"], sp["b_re"], sp["b_im"], sp["c_re"], sp["c_im"])
    return pl.pallas_call(
        body, name="s5_prep_bwd",
        out_shape=[SDS(a.shape, F32) for a in args],
        compiler_params=pltpu.CompilerParams(vmem_limit_bytes=VMEM_LIMIT),
    )(*args, *cots)


def _lanes(v, j):
    return v[:, j * 128:(j + 1) * 128]


def _s5_scan(sre, sim, pwr, pwi, cin_r, cin_i, reverse):
    row = _iota2((8, 128), 0)
    steps = (1, 2, 4)

    def lane_consts(j):
        lanes = slice(j * 128, (j + 1) * 128)
        if reverse:
            mult = [(jnp.where(row < 8 - d, pwr[SEG - d:SEG - d + 1, lanes], 0.0),
                     jnp.where(row < 8 - d, pwi[SEG - d:SEG - d + 1, lanes], 0.0)) for d in steps]
            return mult, pwr[SEG - 8:SEG, lanes], pwi[SEG - 8:SEG, lanes]
        mult = [(jnp.where(row >= d, pwr[d - 1:d, lanes], 0.0),
                 jnp.where(row >= d, pwi[d - 1:d, lanes], 0.0)) for d in steps]
        return mult, pwr[0:8, lanes], pwi[0:8, lanes]

    consts = [lane_consts(j) for j in range(8)]
    nblk = TL // 8

    def block(t, carry):
        b = nblk - 1 - t if reverse else t
        rows = pl.ds(pl.multiple_of(b * 8, 8), 8)
        new = []
        for j in range(8):
            mult, p8r, p8i = consts[j]
            vr, vi = sre.at[j], sim.at[j]
            sr, si = vr[rows, :], vi[rows, :]
            for d, (mr, mi) in zip(steps, mult):
                shift = 8 - d if reverse else d
                hr, hi = pltpu.roll(sr, shift, 0), pltpu.roll(si, shift, 0)
                sr, si = sr + mr * hr - mi * hi, si + mr * hi + mi * hr
            cr, ci = carry[2 * j], carry[2 * j + 1]
            sr, si = sr + p8r * cr - p8i * ci, si + p8r * ci + p8i * cr
            vr[rows, :] = sr
            vi[rows, :] = si
            edge = slice(0, 1) if reverse else slice(7, 8)
            new += [sr[edge, :], si[edge, :]]
        return tuple(new)

    init = []
    for j in range(8):
        init += [_lanes(cin_r, j), _lanes(cin_i, j)]
    ends = lax.fori_loop(0, nblk, block, tuple(init))
    return (jnp.concatenate([ends[2 * j] for j in range(8)], axis=1),
            jnp.concatenate([ends[2 * j + 1] for j in range(8)], axis=1))


def _bdot(a, b, dims):
    return _dot(a.astype(BF16), b.astype(BF16), dims)


def _s5_states(u, wbr, wbi, sre, sim):
    bur = _bdot(u, wbr, NT)
    bui = _bdot(u, wbi, NT)
    for j in range(8):
        sre[j] = _lanes(bur, j)
        sim[j] = _lanes(bui, j)


def _gather_lanes(s):
    return jnp.concatenate([s[j] for j in range(8)], axis=1)


def _s5_tail(y, u, z, dsk, gw, gb):
    yg = _gelu(y + dsk * u)
    return yg * _sigmoid(mm(yg, gw) + gb) * _silu(z)


def _s5_post(s_re, s_im, u, z, wcr, wci, dsk, gw, gb):
    return _s5_tail(mm_nt(s_re, wcr) - mm_nt(s_im, wci), u, z, dsk, gw, gb)


def _s5_fwd_call(proj, prep, p):
    L = proj.shape[0]
    nt = L // TL
    lbr, lbi, wbr, wbi, wcr, wci, pwr, pwi, _, _ = prep

    def body(u_ref, z_ref, lbr_r, lbi_r, wbr_r, wbi_r, wcr_r, wci_r, pwr_r, pwi_r, d_r, gw_r, gb_r,
             yb_ref, cinr_ref, cini_ref, sre, sim, car, cai):
        @pl.when(pl.program_id(0) == 0)
        def _():
            car[...] = jnp.zeros_like(car)
            cai[...] = jnp.zeros_like(cai)

        u = u_ref[...]
        cinr_ref[0] = car[...]
        cini_ref[0] = cai[...]
        _s5_states(u, wbr_r[...], wbi_r[...], sre, sim)
        nr, ni = _s5_scan(sre, sim, pwr_r, pwi_r, car[...], cai[...], False)
        car[...] = nr
        cai[...] = ni
        yb_ref[...] = _s5_post(_gather_lanes(sre), _gather_lanes(sim), u, z_ref[...], wcr_r[...], wci_r[...],
                               d_r[...], gw_r[...], gb_r[...])

    full = lambda a: pl.BlockSpec(a.shape, lambda i: (0,) * a.ndim)
    consts = (lbr, lbi, wbr, wbi, wcr, wci, pwr, pwi, p["s5_d"], p["s5_glu_w"], p["s5_glu_b"])
    cspec = pl.BlockSpec((1, 1, NSTATE), lambda i: (i, 0, 0))
    return pl.pallas_call(
        body, name="s5_fwd", grid=(nt,),
        in_specs=[pl.BlockSpec((TL, BR), lambda i: (i, 3)), pl.BlockSpec((TL, BR), lambda i: (i, 4))]
        + [full(a) for a in consts],
        out_specs=[pl.BlockSpec((TL, BR), lambda i: (i, 0)), cspec, cspec],
        out_shape=[SDS((L, BR), F32), SDS((nt, 1, NSTATE), F32), SDS((nt, 1, NSTATE), F32)],
        scratch_shapes=[pltpu.VMEM((8, TL, 128), F32), pltpu.VMEM((8, TL, 128), F32),
                        pltpu.VMEM((1, NSTATE), F32), pltpu.VMEM((1, NSTATE), F32)],
        compiler_params=_cparams(("arbitrary",)),
    )(proj, proj, *consts)


def _s5_bwd_call(proj, prep, p, cin_r, cin_i, dyb):
    L = proj.shape[0]
    nt = L // TL
    lbr, lbi, wbr, wbi, wcr, wci, pwr, pwi, qwr, qwi = prep

    def body(u_ref, z_ref, lbr_r, lbi_r, wbr_r, wbi_r, wcr_r, wci_r, pwr_r, pwi_r, qwr_r, qwi_r, d_r, gw_r, gb_r,
             cinr_ref, cini_ref, dy_ref,
             db_ref, g_lbr, g_lbi, g_wbr, g_wbi, g_wcr, g_wci, g_d, g_gw, g_gb, sre, sim, gre, gim, car, cai):
        gouts = (g_lbr, g_lbi, g_wbr, g_wbi, g_wcr, g_wci, g_d, g_gw, g_gb)

        @pl.when(pl.program_id(0) == 0)
        def _():
            car[...] = jnp.zeros_like(car)
            cai[...] = jnp.zeros_like(cai)
            for r in gouts:
                r[...] = jnp.zeros_like(r)

        u = u_ref[...]
        c0r, c0i = cinr_ref[0], cini_ref[0]
        width = NSTATE // S5_PARTS
        tiles = width // 128
        part = lambda scr, h: jnp.concatenate([scr[h * tiles + j] for j in range(tiles)], axis=1)
        _s5_states(u, wbr_r[...], wbi_r[...], sre, sim)
        _s5_scan(sre, sim, pwr_r, pwi_r, c0r, c0i, False)
        y = None
        for h in range(S5_PARTS):
            lanes = slice(h * width, (h + 1) * width)
            yh = _bdot(part(sre, h), wcr_r[:, lanes], NT) - _bdot(part(sim, h), wci_r[:, lanes], NT)
            y = yh if y is None else y + yh
        _, vjp = jax.vjp(_s5_tail, y, u, z_ref[...], d_r[...], gw_r[...].astype(F32), gb_r[...])
        dy, du, dz, dd, dgw, dgb = vjp(dy_ref[...])
        for h in range(S5_PARTS):
            lanes = slice(h * width, (h + 1) * width)
            ds_re = _bdot(dy, wcr_r[:, lanes], NN)
            ds_im = -_bdot(dy, wci_r[:, lanes], NN)
            for j in range(tiles):
                gre[h * tiles + j] = _lanes(ds_re, j)
                gim[h * tiles + j] = _lanes(ds_im, j)
        nr, ni = _s5_scan(gre, gim, qwr_r, qwi_r, car[...], cai[...], True)
        car[...] = nr
        cai[...] = ni
        first = _iota2((TL, width), 0) == 0
        for h in range(S5_PARTS):
            lanes = slice(h * width, (h + 1) * width)
            s_re, s_im, a_re, a_im = part(sre, h), part(sim, h), part(gre, h), part(gim, h)
            g_wcr[:, lanes] += _bdot(dy, s_re, TN)
            g_wci[:, lanes] += -_bdot(dy, s_im, TN)
            p_re = jnp.where(first, c0r[:, lanes], jnp.roll(s_re, 1, axis=0))
            p_im = jnp.where(first, c0i[:, lanes], jnp.roll(s_im, 1, axis=0))
            g_lbr[:, lanes] += jnp.sum(a_re * p_re + a_im * p_im, axis=0, keepdims=True)
            g_lbi[:, lanes] += jnp.sum(a_im * p_re - a_re * p_im, axis=0, keepdims=True)
            du = du + _bdot(a_re, wbr_r[lanes, :], NN) + _bdot(a_im, wbi_r[lanes, :], NN)
            g_wbr[lanes, :] += _bdot(a_re, u, TN)
            g_wbi[lanes, :] += _bdot(a_im, u, TN)
        g_d[...] += dd
        g_gw[...] += dgw
        g_gb[...] += dgb
        db_ref[:, 0:BR] = du.astype(BF16)
        db_ref[:, BR:2 * BR] = dz.astype(BF16)

    full = lambda a: pl.BlockSpec(a.shape, lambda i: (0,) * a.ndim)
    consts = (lbr, lbi, wbr, wbi, wcr, wci, pwr, pwi, qwr, qwi, p["s5_d"], p["s5_glu_w"], p["s5_glu_b"])
    cspec = pl.BlockSpec((1, 1, NSTATE), lambda i: (nt - 1 - i, 0, 0))
    gshapes = _S5_OUT + [(1, BR), (BR, BR), (1, BR)]
    return pl.pallas_call(
        body, name="s5_bwd", grid=(nt,),
        in_specs=[pl.BlockSpec((TL, BR), lambda i: (nt - 1 - i, 3)), pl.BlockSpec((TL, BR), lambda i: (nt - 1 - i, 4))]
        + [full(a) for a in consts] + [cspec, cspec, pl.BlockSpec((TL, BR), lambda i: (nt - 1 - i, 0))],
        out_specs=[pl.BlockSpec((TL, 2 * BR), lambda i: (nt - 1 - i, 0))]
        + [pl.BlockSpec(s, lambda i: (0, 0)) for s in gshapes],
        out_shape=[SDS((L, 2 * BR), BF16)] + [SDS(s, F32) for s in gshapes],
        scratch_shapes=[pltpu.VMEM((8, TL, 128), F32)] * 4 + [pltpu.VMEM((1, NSTATE), F32)] * 2,
        compiler_params=_cparams(("arbitrary",)),
    )(proj, proj, *consts, cin_r, cin_i, dyb)


def _heads(x):
    return [x[:, h * HD:(h + 1) * HD] for h in range(NH)]


def _l2n(x, scale):
    return jnp.concatenate([xh * (lax.rsqrt(jnp.sum(xh * xh, axis=-1, keepdims=True) + EPS) * scale)
                            for xh in _heads(x)], axis=1)


def _dn_pre(qkvw, ab, cw, alog, dtb, conv, rows):
    c = _silu(conv(qkvw, cw))
    q = _l2n(c[:, 0:BR], HD ** -0.5)
    k = _l2n(c[:, BR:2 * BR], 1.0)
    v = c[:, 2 * BR:3 * BR]
    g = -jnp.exp(alog) * _softplus(ab + dtb)
    ri, ci = _iota2((rows, rows), 0), _iota2((rows, rows), 1)
    tri = ((ri >= ci) & ((ri >> 6) == (ci >> 6))).astype(F32)
    gc = mmh(tri, g)
    lane = _iota2(ab.shape, 1)
    return q, k, v, jnp.where(lane < NH, gc, jnp.where(lane < 2 * NH, _sigmoid(ab), 0.0))


def _dn_pre_fwd_call(proj, p):
    L = proj.shape[0]

    def body(m_ref, h_ref, ab_ref, cw, alog, dtb, q_ref, k_ref, v_ref, gb_ref):
        nf = (pl.program_id(0) > 0).astype(F32)
        qkvw = jnp.concatenate([h_ref[...] * nf, m_ref[...]], axis=0)
        conv = functools.partial(_conv_fwd_impl, tile=TL, halo=HALO_S, taps=KD)
        q_ref[...], k_ref[...], v_ref[...], gb_ref[...] = _dn_pre(qkvw, ab_ref[...], cw[...], alog[...], dtb[...], conv, TL)

    full = lambda a: pl.BlockSpec(a.shape, lambda i: (0,) * a.ndim)
    params = (p["d_conv_w"], p["d_a_log"], p["d_dt_bias"])
    o = pl.BlockSpec((TL, BR), lambda i: (i, 0))
    return pl.pallas_call(
        body, name="dn_pre_fwd", grid=(L // TL,),
        in_specs=[pl.BlockSpec((TL, 3 * BR), lambda i: (i, 3)), _halo_spec(TL, HALO_S, 3 * BR, 3),
                  pl.BlockSpec((TL, 128), lambda i: (i, AB_COL // 128))] + [full(a) for a in params],
        out_specs=[o, o, o, pl.BlockSpec((TL, 128), lambda i: (i, 0))],
        out_shape=[SDS((L, BR), F32)] * 3 + [SDS((L, 128), F32)],
        compiler_params=_cparams(("parallel",)),
    )(proj, proj, proj, *params)


def _dn_pre_bwd_call(proj, p, dq, dk, dv, dgb):
    L = proj.shape[0]
    nt = L // TL

    def body(m_ref, h_ref, ab_ref, cw, alog, dtb, dq_r, dk_r, dv_r, dgb_r,
             dqkv_ref, dab_ref, g_cw, g_alog, g_dtb, carry):
        i = pl.program_id(0)

        @pl.when(i == 0)
        def _():
            carry[...] = jnp.zeros_like(carry)
            for r in (g_cw, g_alog, g_dtb):
                r[...] = jnp.zeros_like(r)

        nf = (i < nt - 1).astype(F32)
        qkvw = jnp.concatenate([h_ref[...] * nf, m_ref[...]], axis=0)
        conv = _make_conv(TL, HALO_S, KD)
        _, vjp = jax.vjp(lambda a, b, c, d, e: _dn_pre(a, b, c, d, e, conv, TL),
                         qkvw, ab_ref[...], cw[...], alog[...], dtb[...])
        dwin, dab, dcw, dalog, ddtb = vjp((dq_r[...], dk_r[...], dv_r[...], dgb_r[...]))
        tail = jnp.concatenate([jnp.zeros((TL - HALO_S, 3 * BR), F32), carry[...]], axis=0)
        carry[...] = dwin[:HALO_S, :]
        dqkv_ref[...] = (dwin[HALO_S:, :] + tail).astype(BF16)
        dab_ref[...] = dab.astype(BF16)
        g_cw[...] += dcw
        g_alog[...] += dalog
        g_dtb[...] += ddtb

    full = lambda a: pl.BlockSpec(a.shape, lambda i: (0,) * a.ndim)
    params = (p["d_conv_w"], p["d_a_log"], p["d_dt_bias"])
    rev = lambda w: pl.BlockSpec((TL, w), lambda i: (nt - 1 - i, 0))
    return pl.pallas_call(
        body, name="dn_pre_bwd", grid=(nt,),
        in_specs=[pl.BlockSpec((TL, 3 * BR), lambda i: (nt - 1 - i, 3)), _halo_spec_rev(nt, TL, HALO_S, 3 * BR, 3),
                  pl.BlockSpec((TL, 128), lambda i: (nt - 1 - i, AB_COL // 128))] + [full(a) for a in params]
        + [rev(BR), rev(BR), rev(BR), rev(128)],
        out_specs=[rev(3 * BR), rev(128)] + [full(a) for a in params],
        out_shape=[SDS((L, 3 * BR), BF16), SDS((L, 128), BF16)] + [SDS(a.shape, F32) for a in params],
        scratch_shapes=[pltpu.VMEM((HALO_S, 3 * BR), F32)],
        compiler_params=_cparams(("arbitrary",)),
    )(proj, proj, proj, *params, dq, dk, dv, dgb)


def _dn_group(q, k, v, gb, z, ng, *s):
    return _dn_chunks(q, k, v, gb, z, ng, s, DN_GROUP, None)


def _dn_chunks(q, k, v, gb, z, ng, s, chunks, entering):
    ri, ci = _iota2((CH, CH), 0), _iota2((CH, CH), 1)
    causal, strict = ri >= ci, ri > ci
    eye = (ri == ci).astype(F32)
    s = list(s)
    pairs = []
    for c in range(chunks):
        rows = slice(c * CH, (c + 1) * CH)
        gbc = gb[rows, :]
        for h, (qh, kh, vh, zh) in enumerate(zip(_heads(q[rows, :]), _heads(k[rows, :]), _heads(v[rows, :]),
                                                 _heads(z[rows, :]))):
            gc = jnp.broadcast_to(gbc[:, h:h + 1], (CH, HD))
            beta = gbc[:, NH + h:NH + h + 1]
            decay = jnp.where(causal, jnp.exp(jnp.where(causal, gc - gc.T, 0.0)), 0.0)
            egc = jnp.exp(gc)
            glast = gc[CH - 1:CH, :]
            kb = kh * beta
            pairs.append(dict(q=qh, k=kh, z=zh, decay=decay, qe=qh * egc, kd=kh * jnp.exp(glast - gc),
                              sdec=jnp.exp(glast[:, 0:1]), kb=kb, rhs=jnp.concatenate([vh * beta, kb * egc], axis=1)))
    for p in pairs:
        p["pw"] = jnp.where(strict, mm_nt(p["kb"], p["k"]) * p["decay"], 0.0)
    for p in pairs:
        p["t"] = eye - p["pw"]
    for _ in range(5):
        for p in pairs:
            p["pw"] = mm(p["pw"], p["pw"])
        for p in pairs:
            p["t"] = mm(p["t"], eye + p["pw"])
    for p in pairs:
        p["uw"] = mm(p["t"], p["rhs"])
    for p in pairs:
        p["attn"] = mm_nt(p["q"], p["k"]) * p["decay"]
    out_rows = []
    for c in range(chunks):
        if entering is not None and c > 0 and c % DN_GROUP == 0:
            entering.append(list(s))
        grp = pairs[c * NH:(c + 1) * NH]
        ws = [mm(jnp.concatenate([p["uw"][:, HD:], p["qe"]], axis=0), s[h]) for h, p in enumerate(grp)]
        v_new = [p["uw"][:, :HD] - w_[:CH, :] for p, w_ in zip(grp, ws)]
        o = [w_[CH:, :] + mm(p["attn"], vn) for p, w_, vn in zip(grp, ws, v_new)]
        s = [s[h] * p["sdec"] + mm_tn(p["kd"], vn) for h, (p, vn) in enumerate(zip(grp, v_new))]
        o = [oh * lax.rsqrt(jnp.mean(oh * oh, axis=-1, keepdims=True) + EPS) * ng * _silu(p["z"]) for oh, p in zip(o, grp)]
        out_rows.append(jnp.concatenate(o, axis=1))
    return (jnp.concatenate(out_rows, axis=0), *s)


def _dn_core_fwd_call(proj, q, k, v, gb, ng):
    L = q.shape[0]
    rows = DN_FWD_GROUPS * DN_GROUP * CH
    steps = L // rows

    def body(q_r, k_r, v_r, gb_r, z_r, ng_r, yd_ref, ssave_ref, s_scr):
        @pl.when(pl.program_id(0) == 0)
        def _():
            s_scr[...] = jnp.zeros_like(s_scr)

        entering = [[s_scr[h] for h in range(NH)]]
        yd, *s2 = _dn_chunks(q_r[...], k_r[...], v_r[...], gb_r[...], z_r[...], ng_r[...], entering[0],
                             DN_FWD_GROUPS * DN_GROUP, entering)
        yd_ref[...] = yd
        for h in range(NH):
            s_scr[h] = s2[h]
            for g, states in enumerate(entering):
                ssave_ref[g, h] = states[h]

    c = pl.BlockSpec((rows, BR), lambda i: (i, 0))
    return pl.pallas_call(
        body, name="dn_core_fwd", grid=(steps,),
        in_specs=[c, c, c, pl.BlockSpec((rows, 128), lambda i: (i, 0)), pl.BlockSpec((rows, BR), lambda i: (i, 12)),
                  pl.BlockSpec((1, HD), lambda i: (0, 0))],
        out_specs=[c, pl.BlockSpec((DN_FWD_GROUPS, NH, HD, HD), lambda i: (i, 0, 0, 0))],
        out_shape=[SDS((L, BR), F32), SDS((steps * DN_FWD_GROUPS, NH, HD, HD), F32)],
        scratch_shapes=[pltpu.VMEM((NH, HD, HD), F32)],
        compiler_params=_cparams(("arbitrary",)),
    )(q, k, v, gb, proj, ng)


def _dn_core_bwd_call(proj, q, k, v, gb, ng, ssave, dyd):
    L = q.shape[0]
    rows = DN_GROUP * CH
    ng_ = L // rows

    def body(q_r, k_r, v_r, gb_r, z_r, ng_r, s_r, dy_r, dq_ref, dk_ref, dv_ref, dgb_ref, dz_ref, g_ng, ds_scr):
        @pl.when(pl.program_id(0) == 0)
        def _():
            ds_scr[...] = jnp.zeros_like(ds_scr)
            g_ng[...] = jnp.zeros_like(g_ng)

        _, vjp = jax.vjp(_dn_group, q_r[...], k_r[...], v_r[...], gb_r[...], z_r[...], ng_r[...],
                         *[s_r[0, h] for h in range(NH)])
        dq, dk, dv, dgb, dz, dng, *ds = vjp((dy_r[...], *[ds_scr[h] for h in range(NH)]))
        dq_ref[...], dk_ref[...], dv_ref[...], dgb_ref[...] = dq, dk, dv, dgb
        dz_ref[...] = dz.astype(BF16)
        g_ng[...] += dng
        for h in range(NH):
            ds_scr[h] = ds[h]

    c = pl.BlockSpec((rows, BR), lambda i: (ng_ - 1 - i, 0))
    c128 = pl.BlockSpec((rows, 128), lambda i: (ng_ - 1 - i, 0))
    return pl.pallas_call(
        body, name="dn_core_bwd", grid=(ng_,),
        in_specs=[c, c, c, c128, pl.BlockSpec((rows, BR), lambda i: (ng_ - 1 - i, 12)),
                  pl.BlockSpec((1, HD), lambda i: (0, 0)),
                  pl.BlockSpec((1, NH, HD, HD), lambda i: (ng_ - 1 - i, 0, 0, 0)), c],
        out_specs=[c, c, c, c128, c, pl.BlockSpec((1, HD), lambda i: (0, 0))],
        out_shape=[SDS((L, BR), F32)] * 3 + [SDS((L, 128), F32), SDS((L, BR), BF16), SDS((1, HD), F32)],
        scratch_shapes=[pltpu.VMEM((NH, HD, HD), F32)],
        compiler_params=_cparams(("arbitrary",)),
    )(q, k, v, gb, proj, ng, ssave, dyd)


def _outproj_bwd_call(dx, ys, w, deps=()):
    L = dx.shape[0]

    def body(dx_ref, a_ref, b_ref, c_ref, d_ref, w_ref, *rest):
        da, db, dc, dd, dw_ref = rest[len(deps):]

        @pl.when(pl.program_id(0) == 0)
        def _():
            dw_ref[...] = jnp.zeros_like(dw_ref)

        dxb = dx_ref[...].astype(BF16)
        dmixed = _dot(dxb, w_ref[...], NT)
        for b, o_ref in enumerate((da, db, dc, dd)):
            o_ref[...] = dmixed[:, b * BR:(b + 1) * BR]
        mixed = jnp.concatenate([y_ref[...].astype(BF16) for y_ref in (a_ref, b_ref, c_ref, d_ref)], axis=1)
        dw_ref[...] += _dot(mixed, dxb, TN)

    yspec = pl.BlockSpec((TL, BR), lambda i: (i, 0))
    return pl.pallas_call(
        body, name="outproj_bwd", grid=(L // TL,),
        in_specs=[pl.BlockSpec((TL, D), lambda i: (i, 0)), yspec, yspec, yspec, yspec,
                  pl.BlockSpec((D, D), lambda i: (0, 0))] + _dep_specs(deps),
        out_specs=[yspec] * 4 + [pl.BlockSpec((D, D), lambda i: (0, 0))],
        out_shape=[SDS((L, BR), F32)] * 4 + [SDS((D, D), F32)],
        compiler_params=_cparams(("arbitrary",)),
    )(dx, *ys, w, *deps)


def _slab_cols(slabs):
    widths = [s.shape[1] for s in slabs]
    starts = [sum(widths[:i]) for i in range(len(widths))]
    assert starts[-1] + widths[-1] == PW
    return list(zip(starts, widths))


def _inproj_bwd_x_call(slabs, w, x, g, dx_next, deps=()):
    L = x.shape[0]
    cols = _slab_cols(slabs)
    n = len(slabs)

    def body(*refs):
        dp_refs, (w_ref, x_ref, g_ref, dxn_ref) = refs[:n], refs[n:n + 4]
        dx_ref, dg_ref = refs[n + 4 + len(deps):]

        @pl.when(pl.program_id(0) == 0)
        def _():
            dg_ref[...] = jnp.zeros_like(dg_ref)

        dh = _dot(jnp.concatenate([dp_ref[...] for dp_ref in dp_refs], axis=1), w_ref[...], NT)
        _, vjp = jax.vjp(_rms, x_ref[...], g_ref[...])
        dx, dg = vjp(dh)
        dx_ref[...] = dx + dxn_ref[...]
        dg_ref[...] += dg

    row = lambda w_: pl.BlockSpec((TL, w_), lambda i: (i, 0))
    return pl.pallas_call(
        body, name="inproj_bwd_x", grid=(L // TL,),
        in_specs=[row(cw) for _, cw in cols]
        + [pl.BlockSpec((D, PW), lambda i: (0, 0)), row(D), pl.BlockSpec((1, D), lambda i: (0, 0)), row(D)]
        + _dep_specs(deps),
        out_specs=[row(D), pl.BlockSpec((1, D), lambda i: (0, 0))],
        out_shape=[SDS((L, D), F32), SDS((1, D), F32)],
        compiler_params=_cparams(("arbitrary",)),
    )(*slabs, w, x, g, dx_next, *deps)


def _inproj_bwd_w_call(h, slabs):
    L = h.shape[0]
    cols = _slab_cols(slabs)
    n = len(slabs)

    def body(*refs):
        h_ref, dp_refs, dw_ref = refs[0], refs[1:1 + n], refs[1 + n]

        @pl.when(pl.program_id(0) == 0)
        def _():
            dw_ref[...] = jnp.zeros_like(dw_ref)

        hv = h_ref[...]
        for dp_ref, (c0, cw) in zip(dp_refs, cols):
            dw_ref[:, c0:c0 + cw] += _dot(hv, dp_ref[...], TN)

    row = lambda w_: pl.BlockSpec((TL, w_), lambda i: (i, 0))
    return pl.pallas_call(
        body, name="inproj_bwd_w", grid=(L // TL,),
        in_specs=[row(D)] + [row(cw) for _, cw in cols],
        out_specs=pl.BlockSpec((D, PW), lambda i: (0, 0)),
        out_shape=SDS((D, PW), F32),
        compiler_params=_cparams(("arbitrary",)),
    )(h, *slabs)


def _exchange_call(name, flows):
    n = len(flows)

    def body(*refs):
        srcs, dsts = refs[:n], refs[n:2 * n]
        send_sems, recv_sems, local_sems = refs[2 * n:]
        x, y, c = lax.axis_index("x"), lax.axis_index("y"), lax.axis_index("c")
        me = 4 * x + 2 * y + c
        copies = []
        for mask in range(1, N_DEV):
            px = 1 - x if mask & 4 else x
            py = 1 - y if mask & 2 else y
            pc = 1 - c if mask & 1 else c
            for f, (_, src_at, _, dst_at) in enumerate(flows):
                cp = pltpu.make_async_remote_copy(
                    src_ref=src_at(srcs[f], 4 * px + 2 * py + pc), dst_ref=dst_at(dsts[f], me),
                    send_sem=send_sems.at[mask - 1, f], recv_sem=recv_sems.at[mask - 1, f],
                    device_id=(px, py, pc), device_id_type=pl.DeviceIdType.MESH)
                cp.start()
                copies.append(cp)
        mine = [pltpu.make_async_copy(src_at(srcs[f], me), dst_at(dsts[f], me), local_sems.at[f])
                for f, (_, src_at, _, dst_at) in enumerate(flows)]
        for cp in mine:
            cp.start()
        for cp in copies + mine:
            cp.wait()

    return pl.pallas_call(
        body, name=name,
        in_specs=[pl.BlockSpec(memory_space=pl.ANY)] * n,
        out_specs=[pl.BlockSpec(memory_space=pl.ANY)] * n,
        out_shape=[SDS(tuple(shape), src.dtype) for src, _, shape, _ in flows],
        scratch_shapes=[pltpu.SemaphoreType.DMA((N_DEV - 1, n)), pltpu.SemaphoreType.DMA((N_DEV - 1, n)),
                        pltpu.SemaphoreType.DMA((n,))],
    )(*[f[0] for f in flows])


def _whole(ref, _):
    return ref


def _slot(ref, k):
    return ref.at[k]


_HBM_SPEC = pl.BlockSpec(memory_space=pltpu.HBM)
_SEM_SPEC = pl.BlockSpec(memory_space=pltpu.SEMAPHORE)
_DATAFLOW = pltpu.SideEffectType.DATAFLOW_SIDE_EFFECTING


def _split_copies(views, src_refs, land_refs, send_sems, recv_sems):
    x, y, c = lax.axis_index("x"), lax.axis_index("y"), lax.axis_index("c")
    me = 4 * x + 2 * y + c
    copies = []
    for mask in range(1, N_DEV):
        px = 1 - x if mask & 4 else x
        py = 1 - y if mask & 2 else y
        pc = 1 - c if mask & 1 else c
        for f, (src_at, dst_at) in enumerate(views):
            pair = (mask - 1) * len(views) + f
            copies.append(pltpu.make_async_remote_copy(
                src_ref=src_at(src_refs[f], 4 * px + 2 * py + pc), dst_ref=dst_at(land_refs[f], me),
                send_sem=send_sems.at[pair], recv_sem=recv_sems.at[pair],
                device_id=(px, py, pc), device_id_type=pl.DeviceIdType.MESH))
    return copies


def _split_start_call(name, srcs, lands, views):
    n = len(srcs)

    def body(*refs):
        src_refs, land_refs = refs[:n], refs[n:2 * n]
        send_sems, recv_sems, token = refs[2 * n], refs[2 * n + 1], refs[-1]
        for cp in _split_copies(views, src_refs, land_refs, send_sems, recv_sems):
            cp.start()
        token[...] = jnp.zeros_like(token)

    arrays = list(srcs) + list(lands)
    outs = pl.pallas_call(
        body, name=name,
        out_shape=(pltpu.SemaphoreType.DMA(((N_DEV - 1) * n,)), pltpu.SemaphoreType.DMA(((N_DEV - 1) * n,)),
                   *[pltpu.HBM(a.shape, a.dtype) for a in arrays], SDS((8, 128), F32)),
        in_specs=[_HBM_SPEC] * (2 * n),
        out_specs=(_SEM_SPEC, _SEM_SPEC, *[_HBM_SPEC] * (2 * n), pl.BlockSpec(memory_space=pltpu.VMEM)),
        input_output_aliases={i: 2 + i for i in range(2 * n)},
        compiler_params=pltpu.CompilerParams(has_side_effects=_DATAFLOW),
    )(*[pltpu.with_memory_space_constraint(a, pltpu.HBM) for a in arrays])
    return outs[0], outs[1], list(outs[2:2 + 2 * n]), outs[-1]


def _split_wait_call(name, send_sems, recv_sems, thru, views, after):
    n = len(views)

    def body(*refs):
        src_refs, land_refs = refs[:n], refs[n:2 * n]
        send, recv = refs[2 * n], refs[2 * n + 1]
        for cp in _split_copies(views, src_refs, land_refs, send, recv):
            cp.wait_send()
            cp.wait_recv()

    outs = pl.pallas_call(
        body, name=name,
        out_shape=tuple(pltpu.HBM(a.shape, a.dtype) for a in thru),
        in_specs=[_HBM_SPEC] * (2 * n) + [_SEM_SPEC, _SEM_SPEC, pl.BlockSpec(memory_space=pl.ANY)],
        out_specs=tuple([_HBM_SPEC] * (2 * n)),
        input_output_aliases={i: i for i in range(2 * n)},
        compiler_params=pltpu.CompilerParams(has_side_effects=_DATAFLOW),
    )(*thru, send_sems, recv_sems, after)
    return list(outs[n:])


def _own_slot(block, me):
    zone = lax.empty((N_DEV,) + block.shape, block.dtype)
    return lax.dynamic_update_slice(zone, block[None], (me,) + (0,) * block.ndim)


def _reduce_adamw_call(parts, w, m, v, block, name):
    nsrc = parts.shape[0]
    grid = tuple(s // b for s, b in zip(w.shape, block))
    c1 = 1.0 - ADAM_B1 ** ADAM_STEP
    c2 = 1.0 - ADAM_B2 ** ADAM_STEP

    def body(p_ref, w_ref, m_ref, v_ref, g_ref, d_ref, nm_ref, nv_ref):
        g = p_ref[0].astype(F32)
        for k in range(1, nsrc):
            g = g + p_ref[k].astype(F32)
        nm = ADAM_B1 * m_ref[...] + (1.0 - ADAM_B1) * g
        nv = ADAM_B2 * v_ref[...] + (1.0 - ADAM_B2) * (g * g)
        g_ref[...] = g
        nm_ref[...] = nm
        nv_ref[...] = nv
        d_ref[...] = -ADAM_LR * ((nm / c1) / (jnp.sqrt(nv / c2) + ADAM_EPS) + ADAM_WD * w_ref[...])

    own = pl.BlockSpec(tuple(block), lambda *i: i)
    return pl.pallas_call(
        body, name=name, grid=grid,
        in_specs=[pl.BlockSpec((nsrc,) + tuple(block), lambda *i: (0,) + i), own, own, own],
        out_specs=[own] * 4,
        out_shape=[SDS(w.shape, F32)] * 4,
        compiler_params=_cparams(("parallel",) * len(grid)),
    )(parts, w, m, v)


RELAYOUT_ROWS = 256
SHARD_COLS = IN_COLS // N_DEV


def _win_gather_layout_call(shards):
    def body(w_ref, o_ref):
        nat = jnp.concatenate([w_ref[k].astype(F32) for k in range(N_DEV)], axis=1)
        out = jnp.concatenate([nat[:, :3072], nat[:, 3080:], nat[:, 3072:3080],
                               jnp.zeros((RELAYOUT_ROWS, PW - IN_COLS), F32)], axis=1)
        o_ref[...] = out.astype(BF16)

    return pl.pallas_call(
        body, name="w_in_layout", grid=(D // RELAYOUT_ROWS,),
        in_specs=[pl.BlockSpec((N_DEV, RELAYOUT_ROWS, SHARD_COLS), lambda i: (0, i, 0))],
        out_specs=pl.BlockSpec((RELAYOUT_ROWS, PW), lambda i: (i, 0)),
        out_shape=SDS((D, PW), BF16),
        compiler_params=_cparams(("parallel",)),
    )(shards)


def _win_scatter_layout_call(grad):
    def body(g_ref, o_ref):
        g = g_ref[...]
        nat = jnp.concatenate([g[:, :3072], g[:, AB_COL:AB_COL + 8], g[:, 3072:AB_COL]], axis=1)
        for k in range(N_DEV):
            o_ref[k] = nat[:, SHARD_COLS * k:SHARD_COLS * (k + 1)].astype(BF16)

    return pl.pallas_call(
        body, name="w_in_grad_layout", grid=(D // RELAYOUT_ROWS,),
        in_specs=[pl.BlockSpec((RELAYOUT_ROWS, PW), lambda i: (i, 0))],
        out_specs=pl.BlockSpec((N_DEV, RELAYOUT_ROWS, SHARD_COLS), lambda i: (0, i, 0)),
        out_shape=SDS((N_DEV, D, SHARD_COLS), BF16),
        compiler_params=_cparams(("parallel",)),
    )(grad)


def _reduce_adamw_layers_call(parts, w, m, v, rows, name):
    _, R, C = w.shape
    c1 = 1.0 - ADAM_B1 ** ADAM_STEP
    c2 = 1.0 - ADAM_B2 ** ADAM_STEP

    def body(*refs):
        p_refs = refs[:DEPTH]
        w_ref, m_ref, v_ref, g_ref, d_ref, nm_ref, nv_ref = refs[DEPTH:]
        for l in range(DEPTH):
            @pl.when(pl.program_id(0) == l)
            def _(l=l):
                g = p_refs[l][0].astype(F32)
                for k in range(1, N_DEV):
                    g = g + p_refs[l][k].astype(F32)
                nm = ADAM_B1 * m_ref[0] + (1.0 - ADAM_B1) * g
                nv = ADAM_B2 * v_ref[0] + (1.0 - ADAM_B2) * (g * g)
                g_ref[0] = g
                nm_ref[0] = nm
                nv_ref[0] = nv
                d_ref[0] = -ADAM_LR * ((nm / c1) / (jnp.sqrt(nv / c2) + ADAM_EPS) + ADAM_WD * w_ref[0])

    def part_spec(l):
        return pl.BlockSpec((N_DEV, rows, C), lambda j, i: (0, jnp.where(j == l, i, 0), 0))

    own = pl.BlockSpec((1, rows, C), lambda j, i: (j, i, 0))
    return pl.pallas_call(
        body, name=name, grid=(DEPTH, R // rows),
        in_specs=[part_spec(l) for l in range(DEPTH)] + [own, own, own],
        out_specs=[own] * 4,
        out_shape=[SDS(w.shape, F32)] * 4,
        compiler_params=_cparams(("arbitrary", "arbitrary")),
    )(*parts, w, m, v)


_BIG = ("w_in", "w_out", "a_pw_w", "s5_glu_w")
_CONV = ("a_conv_w", "c_conv_w", "d_conv_w")
_CONV_TAPS = {"a_conv_w": KA, "c_conv_w": KC, "d_conv_w": KD}
_CONV_ROWS = {"a_conv_w": HALO, "c_conv_w": HALO_S, "d_conv_w": HALO_S}
_CONV_WIDTH = {"a_conv_w": BR, "c_conv_w": BR, "d_conv_w": 3 * BR}
_REPLICATED = ("norm_g", "a_conv_b", "a_ln_g", "a_ln_b", "a_pw_b", "s5_lambda_re", "s5_lambda_im", "s5_b_re", "s5_b_im",
               "s5_c_re", "s5_c_im", "s5_d", "s5_log_dt", "s5_glu_b", "d_a_log", "d_dt_bias", "d_norm_g", "final_g")
_REP_TAIL = ("norm_g", "final_g")
_REP_LAYER = tuple(n for n in _REPLICATED if n not in _REP_TAIL)
_WEIGHTS = ("norm_g", "w_in", "a_conv_w", "a_conv_b", "a_ln_g", "a_ln_b", "a_pw_w", "a_pw_b", "s5_lambda_re",
            "s5_lambda_im", "s5_b_re", "s5_b_im", "s5_c_re", "s5_c_im", "s5_d", "s5_log_dt", "s5_glu_w", "s5_glu_b",
            "c_conv_w", "d_conv_w", "d_a_log", "d_dt_bias", "d_norm_g", "w_out", "final_g")


def _size(shape):
    n = 1
    for s in shape:
        n *= s
    return n


PACK_ALIGN = 1024


def _piece_rows(n):
    return -(-n // PACK_ALIGN) * (PACK_ALIGN // 128)


def _pack_rows(pieces, row_mult):
    rows = []
    for p in pieces:
        flat = p.reshape(-1)
        rows.append(jnp.pad(flat, (0, _piece_rows(flat.shape[0]) * 128 - flat.shape[0])).reshape(-1, 128))
    out = jnp.concatenate(rows, axis=0)
    return jnp.pad(out, ((0, (-out.shape[0]) % row_mult), (0, 0)))


def _pack_layers(pieces):
    rows = []
    for p in pieces:
        flat = p.reshape(DEPTH, -1)
        nr = _piece_rows(flat.shape[1])
        rows.append(jnp.pad(flat, ((0, 0), (0, nr * 128 - flat.shape[1]))).reshape(DEPTH, nr, 128))
    return jnp.concatenate(rows, axis=1)


def _unpack_layers(packed, shapes):
    out, row = [], 0
    for s in shapes:
        n = _size(s[1:])
        nr = _piece_rows(n)
        out.append(packed[:, row:row + nr].reshape(DEPTH, -1)[:, :n].reshape(s))
        row += nr
    return out


def _unpack(packed, shapes):
    out, row = [], 0
    for s in shapes:
        n = _size(s)
        nr = _piece_rows(n)
        out.append(packed[row:row + nr].reshape(-1)[:n].reshape(s))
        row += nr
    return out


_GATHER_VIEWS = [(_whole, _slot)] * 5


def _gather_start(shards, layer, me):
    srcs = [shards[n].astype(BF16) for n in _BIG]
    srcs.append(_pack_rows([shards[n] for n in _CONV], 8))
    lands = [_own_slot(s, me) for s in srcs]
    return _split_start_call("gather_start_%d" % layer, srcs, lands, _GATHER_VIEWS)


def _gather_finish(weights, layer, started, after):
    send, recv, thru, _ = started
    w_in, w_out, a_pw, glu, conv_all = _split_wait_call("gather_wait_%d" % layer, send, recv, thru, _GATHER_VIEWS, after)
    full = {"w_in": _win_gather_layout_call(w_in), "w_out": w_out.reshape(D, D), "a_pw_w": a_pw.reshape(BR, BR),
            "s5_glu_w": glu.reshape(BR, BR)}
    shapes = [weights[n].shape[1:] for n in _CONV]
    per_dev = [_unpack(conv_all[k], shapes) for k in range(N_DEV)]
    for i, n in enumerate(_CONV):
        whole = jnp.concatenate([per_dev[k][i] for k in range(N_DEV)], axis=-1)
        full[n] = jnp.pad(whole, ((0, _CONV_ROWS[n] - _CONV_TAPS[n]), (0, 0)))
    return full


def _rows_view(rows):
    return lambda ref, k: ref.at[pl.ds(k * rows, rows), :]


_SCATTER_VIEWS = [(_slot, _slot), (_rows_view(D // N_DEV), _slot), (_rows_view(BR // N_DEV), _slot),
                  (_rows_view(BR // N_DEV), _slot), (_whole, _slot)]


def _scatter_start(grads, small, layer, me):
    srcs = [_win_scatter_layout_call(grads["w_in"])] + [grads[n].astype(BF16) for n in _BIG[1:]]
    own = [lax.dynamic_index_in_dim(srcs[0], me, 0, keepdims=False)]
    for s, rows in zip(srcs[1:], (D // N_DEV, BR // N_DEV, BR // N_DEV)):
        own.append(lax.dynamic_slice_in_dim(s, me * rows, rows, axis=0))
    lands = [_own_slot(o, me) for o in own + [small]]
    return _split_start_call("scatter_start_%d" % layer, srcs + [small], lands, _SCATTER_VIEWS)


_S5_KERNEL_SHAPES = {"s5_lambda_re": (1, NSTATE), "s5_lambda_im": (1, NSTATE), "s5_log_dt": (1, 16),
                     "s5_b_re": (NSTATE, 16), "s5_b_im": (NSTATE, 16), "s5_c_re": (BR, 64), "s5_c_im": (BR, 64)}
_S5_KEYS = {"s5_lambda_re": "lam_re", "s5_lambda_im": "lam_im", "s5_log_dt": "log_dt", "s5_b_re": "b_re",
            "s5_b_im": "b_im", "s5_c_re": "c_re", "s5_c_im": "c_im"}


def _s5_inputs_all(weights):
    return {n: weights[n].reshape((DEPTH,) + s) for n, s in _S5_KERNEL_SHAPES.items()}


def _s5_inputs(p):
    return {_S5_KEYS[n]: p["s5_in"][n] for n in _S5_KERNEL_SHAPES}


def _row(a, width=None):
    a = a.reshape(1, -1)
    return a if width is None else jnp.pad(a, ((0, 0), (0, width - a.shape[1])))


def _layer_params(p):
    q = dict(p)
    for n in ("norm_g", "a_conv_b", "a_ln_g", "a_ln_b", "a_pw_b", "s5_d", "s5_glu_b", "d_norm_g"):
        q[n] = _row(p[n])
    q["d_a_log"] = _row(p["d_a_log"], 128)
    q["d_dt_bias"] = _row(p["d_dt_bias"], 128)
    return q


def _layer_fwd(x, p, deps=()):
    q = _layer_params(p)
    proj, h = _inproj_call(x, q["norm_g"], q["w_in"], deps)
    ya, yc = _ac_fwd_call(proj, q)
    prep = p["s5_prep"]
    yb, cin_r, cin_i = _s5_fwd_call(proj, prep, q)
    dq, dk, dv, dgb = _dn_pre_fwd_call(proj, q)
    yd, ssave = _dn_core_fwd_call(proj, dq, dk, dv, dgb, q["d_norm_g"])
    x_next = _outproj_call(x, (ya, yb, yc, yd), q["w_out"])
    saved = dict(x=x, proj=proj, h=h, ya=ya, yb=yb, yc=yc, yd=yd, cin_r=cin_r, cin_i=cin_i,
                 q=dq, k=dk, v=dv, gb=dgb, ssave=ssave, prep=prep)
    return x_next, saved


def _layer_bwd(dx, p, sv, deps=(), on_weight_grads=None):
    q = _layer_params(p)
    proj = sv["proj"]
    dya, dyb, dyc, dyd, g_wout = _outproj_bwd_call(dx, (sv["ya"], sv["yb"], sv["yc"], sv["yd"]), q["w_out"], deps)
    dpa, dpc, g_acw, g_acb, g_alg, g_alb, g_apw, g_apb, g_ccw = _ac_bwd_call(proj, q, dya, dyc)
    dpb, *s5g = _s5_bwd_call(proj, sv["prep"], q, sv["cin_r"], sv["cin_i"], dyb)
    g_sd, g_gw, g_gb = s5g[6:]
    g_lre, g_lim, g_ldt, g_bre, g_bim, g_cre, g_cim = _s5_prep_bwd_call(_s5_inputs(p), s5g[:6])
    dq, dk, dv, dgb, dz, g_ng = _dn_core_bwd_call(proj, sv["q"], sv["k"], sv["v"], sv["gb"], q["d_norm_g"], sv["ssave"], dyd)
    dqkv, dab, g_dcw, g_alog, g_dtb = _dn_pre_bwd_call(proj, q, dq, dk, dv, dgb)
    slabs = (dpa, dpb, dpc, dqkv, dz, dab)
    g_win = _inproj_bwd_w_call(sv["h"], slabs)
    grads = {"w_in": g_win, "a_conv_w": g_acw, "a_conv_b": g_acb, "a_ln_g": g_alg, "a_ln_b": g_alb,
             "a_pw_w": g_apw, "a_pw_b": g_apb, "s5_lambda_re": g_lre, "s5_lambda_im": g_lim, "s5_b_re": g_bre,
             "s5_b_im": g_bim, "s5_c_re": g_cre, "s5_c_im": g_cim, "s5_d": g_sd, "s5_log_dt": g_ldt, "s5_glu_w": g_gw,
             "s5_glu_b": g_gb, "c_conv_w": g_ccw, "d_conv_w": g_dcw, "d_a_log": g_alog[:, :NH], "d_dt_bias": g_dtb[:, :NH],
             "d_norm_g": g_ng, "w_out": g_wout}
    tokens = ()
    if on_weight_grads is not None:
        small = _pack_rows([grads[n] for n in _REP_LAYER + _CONV], 8)
        tokens = on_weight_grads({n: grads[n] for n in _BIG}, small)
    dx_prev, grads["norm_g"] = _inproj_bwd_x_call(slabs, q["w_in"], sv["x"], q["norm_g"], dx, tokens)
    return dx_prev, grads


def _step(x, target, weights, moments_m, moments_v):
    me = 4 * lax.axis_index("x") + 2 * lax.axis_index("y") + lax.axis_index("c")
    layer_names = [n for n in _WEIGHTS if n != "final_g"]
    s5_all = _s5_inputs_all(weights)

    sharded = _BIG + _CONV
    gather = _gather_start({n: weights[n][0] for n in sharded}, 0, me)
    preps = [_s5_prep_call({_S5_KEYS[n]: a[l] for n, a in s5_all.items()}, [gather[3]]) for l in range(DEPTH)]
    x_out, after, layers, saved = x, preps[-1][0], [], []
    for l in range(DEPTH):
        full = _gather_finish(weights, l, gather, after)
        deps = ()
        if l + 1 < DEPTH:
            nxt, full["w_out"] = lax.optimization_barrier(({n: weights[n][l + 1] for n in sharded}, full["w_out"]))
            gather = _gather_start(nxt, l + 1, me)
            deps = [gather[3]]
        p = {n: (full[n] if n in full else weights[n][l]) for n in layer_names}
        p["s5_in"] = {n: a[l] for n, a in s5_all.items()}
        p["s5_prep"] = preps[l]
        layers.append(p)
        x_out, sv = _layer_fwd(x_out, p, deps)
        after = x_out
        saved.append(sv)
    dx0, g_final, loss_part = _loss_call(x_out, _row(weights["final_g"]), target)

    per_layer, scatters = [None] * DEPTH, [None] * DEPTH
    for l in range(DEPTH - 1, -1, -1):
        def start(big, small, l=l):
            scatters[l] = _scatter_start(big, small, l, me)
            return [scatters[l][3]]
        dx0, per_layer[l] = _layer_bwd(dx0, layers[l], saved[l], [scatters[l + 1][3]] if l + 1 < DEPTH else (), start)
    loss = lax.psum(loss_part[0, 0], ("x", "y", "c"))
    results = {}

    tail_g = [jnp.stack([g["norm_g"] for g in per_layer]).reshape(DEPTH, D), g_final.reshape(D)]
    packt = lambda arrs: _pack_rows(arrs, 8)
    gathered, = _exchange_call("gather_tail_grads", [(packt(tail_g), _whole, (N_DEV,) + packt(tail_g).shape, _slot)])
    tail = _reduce_adamw_call(gathered, packt([weights[n] for n in _REP_TAIL]), packt([moments_m[n] for n in _REP_TAIL]),
                              packt([moments_v[n] for n in _REP_TAIL]), packt(tail_g).shape, "adamw_tail")
    after = tail[0]
    tail = [_unpack(r, [weights[n].shape for n in _REP_TAIL]) for r in tail]
    for i, n in enumerate(_REP_TAIL):
        results[n] = tuple(r[i] for r in tail)

    parts = [_split_wait_call("scatter_wait_%d" % l, scatters[l][0], scatters[l][1], scatters[l][2], _SCATTER_VIEWS, after)
             for l in range(DEPTH - 1, -1, -1)][::-1]
    rows = {"w_in": RELAYOUT_ROWS, "w_out": D // N_DEV, "a_pw_w": BR // N_DEV, "s5_glu_w": BR // N_DEV}
    for i, n in enumerate(_BIG):
        results[n] = _reduce_adamw_layers_call([parts[l][i] for l in range(DEPTH)], weights[n], moments_m[n],
                                               moments_v[n], rows[n], "adamw_" + n)

    conv_shapes = [(DEPTH, _CONV_ROWS[n], _CONV_WIDTH[n]) for n in _CONV]
    pack = lambda d: _pack_layers([d[n] for n in _REP_LAYER] + [jnp.zeros(s, F32) for s in conv_shapes])
    wpack = pack(weights)
    res = _reduce_adamw_layers_call([parts[l][len(_BIG)] for l in range(DEPTH)], wpack, pack(moments_m), pack(moments_v),
                                    wpack.shape[1], "adamw_replicated")
    shapes = [weights[n].shape for n in _REP_LAYER] + conv_shapes
    res = [_unpack_layers(r, shapes) for r in res]
    for i, n in enumerate(_REP_LAYER):
        results[n] = tuple(r[i] for r in res)

    own_g = []
    for i, n in enumerate(_CONV):
        width = _CONV_WIDTH[n] // N_DEV
        summed = res[0][len(_REP_LAYER) + i][:, :_CONV_TAPS[n], :]
        own_g.append(lax.dynamic_slice_in_dim(summed, me * width, width, axis=2))
    packc = lambda arrs: _pack_rows(arrs, 8)
    res = _reduce_adamw_call(packc(own_g)[None], packc([weights[n] for n in _CONV]), packc([moments_m[n] for n in _CONV]),
                             packc([moments_v[n] for n in _CONV]), packc(own_g).shape, "adamw_conv")
    res = [_unpack(r, [weights[n].shape for n in _CONV]) for r in res]
    for i, n in enumerate(_CONV):
        results[n] = tuple(r[i] for r in res)

    outs = [loss, dx0]
    for kind in range(4):
        outs += [results[n][kind] for n in _WEIGHTS]
    return tuple(outs)


def kernel(x, norm_g, w_in, a_conv_w, a_conv_b, a_ln_g, a_ln_b, a_pw_w, a_pw_b, s5_lambda_re, s5_lambda_im, s5_b_re, s5_b_im, s5_c_re, s5_c_im, s5_d, s5_log_dt, s5_glu_w, s5_glu_b, c_conv_w, d_conv_w, d_a_log, d_dt_bias, d_norm_g, w_out, final_g, loss_target, m_norm_g, m_w_in, m_a_conv_w, m_a_conv_b, m_a_ln_g, m_a_ln_b, m_a_pw_w, m_a_pw_b, m_s5_lambda_re, m_s5_lambda_im, m_s5_b_re, m_s5_b_im, m_s5_c_re, m_s5_c_im, m_s5_d, m_s5_log_dt, m_s5_glu_w, m_s5_glu_b, m_c_conv_w, m_d_conv_w, m_d_a_log, m_d_dt_bias, m_d_norm_g, m_w_out, m_final_g, v_norm_g, v_w_in, v_a_conv_w, v_a_conv_b, v_a_ln_g, v_a_ln_b, v_a_pw_w, v_a_pw_b, v_s5_lambda_re, v_s5_lambda_im, v_s5_b_re, v_s5_b_im, v_s5_c_re, v_s5_c_im, v_s5_d, v_s5_log_dt, v_s5_glu_w, v_s5_glu_b, v_c_conv_w, v_d_conv_w, v_d_a_log, v_d_dt_bias, v_d_norm_g, v_w_out, v_final_g):
    weights = dict(norm_g=norm_g, w_in=w_in, a_conv_w=a_conv_w, a_conv_b=a_conv_b, a_ln_g=a_ln_g, a_ln_b=a_ln_b, a_pw_w=a_pw_w, a_pw_b=a_pw_b, s5_lambda_re=s5_lambda_re, s5_lambda_im=s5_lambda_im, s5_b_re=s5_b_re, s5_b_im=s5_b_im, s5_c_re=s5_c_re, s5_c_im=s5_c_im, s5_d=s5_d, s5_log_dt=s5_log_dt, s5_glu_w=s5_glu_w, s5_glu_b=s5_glu_b, c_conv_w=c_conv_w, d_conv_w=d_conv_w, d_a_log=d_a_log, d_dt_bias=d_dt_bias, d_norm_g=d_norm_g, w_out=w_out, final_g=final_g)
    mom_m = dict(norm_g=m_norm_g, w_in=m_w_in, a_conv_w=m_a_conv_w, a_conv_b=m_a_conv_b, a_ln_g=m_a_ln_g, a_ln_b=m_a_ln_b, a_pw_w=m_a_pw_w, a_pw_b=m_a_pw_b, s5_lambda_re=m_s5_lambda_re, s5_lambda_im=m_s5_lambda_im, s5_b_re=m_s5_b_re, s5_b_im=m_s5_b_im, s5_c_re=m_s5_c_re, s5_c_im=m_s5_c_im, s5_d=m_s5_d, s5_log_dt=m_s5_log_dt, s5_glu_w=m_s5_glu_w, s5_glu_b=m_s5_glu_b, c_conv_w=m_c_conv_w, d_conv_w=m_d_conv_w, d_a_log=m_d_a_log, d_dt_bias=m_d_dt_bias, d_norm_g=m_d_norm_g, w_out=m_w_out, final_g=m_final_g)
    mom_v = dict(norm_g=v_norm_g, w_in=v_w_in, a_conv_w=v_a_conv_w, a_conv_b=v_a_conv_b, a_ln_g=v_a_ln_g, a_ln_b=v_a_ln_b, a_pw_w=v_a_pw_w, a_pw_b=v_a_pw_b, s5_lambda_re=v_s5_lambda_re, s5_lambda_im=v_s5_lambda_im, s5_b_re=v_s5_b_re, s5_b_im=v_s5_b_im, s5_c_re=v_s5_c_re, s5_c_im=v_s5_c_im, s5_d=v_s5_d, s5_log_dt=v_s5_log_dt, s5_glu_w=v_s5_glu_w, s5_glu_b=v_s5_glu_b, c_conv_w=v_c_conv_w, d_conv_w=v_d_conv_w, d_a_log=v_d_a_log, d_dt_bias=v_d_dt_bias, d_norm_g=v_d_norm_g, w_out=v_w_out, final_g=v_final_g)
    outs = _step(x[0], loss_target[0], weights, mom_m, mom_v)
    return (outs[0], outs[1][None]) + outs[2:]
```
